```python
import math
import jax
import jax.numpy as jnp
from jax import lax
import numpy as np

D_MODEL = 1024
BATCH = 2
SEQ = 8192
DEPTH = 2

GRID_W = 64
CTX_LEN = 256
N_BRANCH = 4
BRANCH_W = 256
ROPE_DIM = 32
ROPE_BASE = 10000.0
Q_BLOCK = 128
NORM_EPS = 1e-6

MLA_HEADS = 4
MLA_NOPE = 64
MLA_ROPE = ROPE_DIM
MLA_V = 64
MLA_Q_LORA = 256
MLA_KV_LORA = 128

RW_HEADS = 4
RW_HEAD = 64
RW_W_LORA = 64
RW_A_LORA = 64
RW_G_LORA = 128
RW_LN_EPS = 64e-5

S5_GROUPS = 16
S5_GROUP_CH = 16
S5_STATE = 64

DIFF_HEADS = 4
DIFF_HD = ROPE_DIM
DIFF_EPS = 1e-5

MOE_GROUPS = 4
MOE_PER_GROUP = 8
MOE_EXPERTS = MOE_GROUPS * MOE_PER_GROUP
MOE_TOPK = 2
D_EXPERT = 512
MOE_BLOCK = 128

MLA_IN = MLA_Q_LORA + MLA_KV_LORA + MLA_ROPE
RW_IN = 3 * RW_HEADS * RW_HEAD + RW_W_LORA + RW_A_LORA + RW_G_LORA
S5_IN = S5_GROUPS * S5_GROUP_CH
DIFF_IN = 3 * DIFF_HEADS * 2 * DIFF_HD
GATE_IN = N_BRANCH * D_MODEL
IN_SPLITS = [MLA_IN, MLA_IN + RW_IN, MLA_IN + RW_IN + S5_IN, MLA_IN + RW_IN + S5_IN + DIFF_IN]
N_IN = MLA_IN + RW_IN + S5_IN + DIFF_IN + GATE_IN

kernel_name = 'hybrid_gated_mixers_hmoe_dit'


def rmsnorm(x, g, eps=NORM_EPS):
    x32 = x.astype(jnp.float32)
    y = x32 * lax.rsqrt(jnp.mean(x32 * x32, axis=-1, keepdims=True) + eps)
    return (y * g.astype(jnp.float32)).astype(x.dtype)


def axial_rope_tables(n_lat, dim):
    rows = n_lat // GRID_W
    row = jnp.repeat(jnp.arange(rows, dtype=jnp.float32), GRID_W)
    col = (jnp.arange(rows * GRID_W) % GRID_W).astype(jnp.float32)
    nf = dim // 4
    inv = ROPE_BASE ** (-jnp.arange(nf, dtype=jnp.float32) / nf)
    ar = row[:, None] * inv
    ac = col[:, None] * inv
    ang = jnp.concatenate([ar, ar, ac, ac], axis=-1)
    return jnp.cos(ang), jnp.sin(ang)


def rotate_axial(x):
    s = x.shape
    x4 = x.reshape(s[:-1] + (2, 2, s[-1] // 4))
    return jnp.stack([-x4[..., 1, :], x4[..., 0, :]], axis=-2).reshape(s)


def apply_rope(x, cos, sin):
    return x * cos + rotate_axial(x) * sin


def ctx_last(t, n_ctx):
    return jnp.concatenate([t[:, n_ctx:], t[:, :n_ctx]], axis=1)


def ctx_first(t, n_ctx):
    return jnp.concatenate([t[:, -n_ctx:], t[:, :-n_ctx]], axis=1)


def sweep_query_blocks(fn, q):
    b, s = q.shape[:2]
    nb = s // Q_BLOCK
    qb = jnp.moveaxis(q.reshape((b, nb, Q_BLOCK) + q.shape[2:]), 1, 0)
    out = lax.map(fn, qb)
    return jnp.moveaxis(out, 0, 1).reshape((b, s) + out.shape[3:])


def softmax_attend(q, k, v, scale):
    s = jnp.einsum('bqhd,bkhd->bhqk', q, k).astype(jnp.float32) * scale
    p = jax.nn.softmax(s, axis=-1).astype(v.dtype)
    return jnp.einsum('bhqk,bkhd->bqhd', p, v)


def diff_attend(q, k, v, lam, scale):
    s = jnp.einsum('bqhnd,bkhnd->bhnqk', q, k).astype(jnp.float32) * scale
    p = jax.nn.softmax(s, axis=-1)
    a = (p[:, :, 0] - lam * p[:, :, 1]).astype(v.dtype)
    return jnp.einsum('bhqk,bkhd->bqhd', a, v)


def mla_branch(pc, pl, cos, sin, q_norm_g, kv_norm_g, w_uq, w_ukv, need_ctx):
    scale = (MLA_NOPE + MLA_ROPE) ** -0.5

    def project(pz, rotary):
        b, t, _ = pz.shape
        cq = pz[..., :MLA_Q_LORA]
        ckv = pz[..., MLA_Q_LORA:MLA_Q_LORA + MLA_KV_LORA]
        kr = pz[..., MLA_Q_LORA + MLA_KV_LORA:]
        q = (rmsnorm(cq, q_norm_g) @ w_uq).reshape(b, t, MLA_HEADS, MLA_NOPE + MLA_ROPE)
        kv = (rmsnorm(ckv, kv_norm_g) @ w_ukv).reshape(b, t, MLA_HEADS, MLA_NOPE + MLA_V)
        q_nope, q_rope = q[..., :MLA_NOPE], q[..., MLA_NOPE:]
        k_nope, v = kv[..., :MLA_NOPE], kv[..., MLA_NOPE:]
        if rotary:
            q_rope = apply_rope(q_rope, cos[:, None], sin[:, None])
            kr = apply_rope(kr, cos, sin)
        q = jnp.concatenate([q_nope, q_rope], axis=-1)
        k = jnp.concatenate([k_nope, jnp.broadcast_to(kr[:, :, None], (b, t, MLA_HEADS, MLA_ROPE))], axis=-1)
        return q, k, v

    qc, kc, vc = project(pc, False)
    ql, kl, vl = project(pl, True)
    k_all = jnp.concatenate([kc, kl], axis=1)
    v_all = jnp.concatenate([vc, vl], axis=1)
    yl = sweep_query_blocks(lambda qb: softmax_attend(qb, k_all, v_all, scale), ql)
    yl = yl.reshape(yl.shape[0], yl.shape[1], MLA_HEADS * MLA_V)
    yc = None
    if need_ctx:
        yc = softmax_attend(qc, kc, vc, scale)
        yc = yc.reshape(yc.shape[0], yc.shape[1], MLA_HEADS * MLA_V)
    return yc, yl


def token_shift_centred(z, mu):
    zp = jnp.pad(z, ((0, 0), (1, 1), (0, 0)))
    return z + (0.5 * (zp[:, :-2] + zp[:, 2:]) - z) * mu


def rwkv7_scan(r, decay, k, v, kk, a, reverse):
    b, t, h, n = r.shape

    def step(state, inp):
        r_t, w_t, k_t, v_t, kk_t, a_t = inp
        sa = jnp.einsum('bhvk,bhk->bhv', state, -kk_t)
        state = (state * w_t[:, :, None, :] + sa[..., None] * (kk_t * a_t)[:, :, None, :]
                 + v_t[..., None] * k_t[:, :, None, :])
        return state, jnp.einsum('bhvk,bhk->bhv', state, r_t)

    xs = tuple(jnp.moveaxis(z.astype(jnp.float32), 1, 0) for z in (r, decay, k, v, kk, a))
    _, y = lax.scan(step, jnp.zeros((b, h, n, n), jnp.float32), xs, reverse=reverse)
    return jnp.moveaxis(y, 0, 1)


def rwkv_branch(pc, pl, mu, w0, w2, a0, a2, g2, k_k, k_a, r_k, lnx_g, lnx_b, need_ctx):
    n_ctx = pc.shape[1]
    z = jnp.concatenate([token_shift_centred(pc, mu), token_shift_centred(pl, mu)], axis=1).astype(jnp.float32)
    b, t, _ = z.shape
    hw = RW_HEADS * RW_HEAD
    r, k, v = z[..., :hw], z[..., hw:2 * hw], z[..., 2 * hw:3 * hw]
    o = 3 * hw
    w_lo = z[..., o:o + RW_W_LORA]
    a_lo = z[..., o + RW_W_LORA:o + RW_W_LORA + RW_A_LORA]
    g_lo = z[..., o + RW_W_LORA + RW_A_LORA:]

    def heads(y):
        return y.reshape(b, t, RW_HEADS, RW_HEAD)

    kk = heads(k * k_k)
    kk = kk / jnp.maximum(jnp.sqrt(jnp.sum(kk * kk, axis=-1, keepdims=True)), 1e-12)
    rh, vh = heads(r), heads(v)
    ys, bonuses = [], []
    for d in range(2):
        wd = -jax.nn.softplus(-(w0[d] + jnp.tanh(w_lo) @ w2[d])) - 0.5
        decay = jnp.exp(-jnp.exp(wd))
        ad = jax.nn.sigmoid(a0[d] + a_lo @ a2[d])
        kd = heads(k * (1.0 + (ad - 1.0) * k_a))
        ins = (rh, heads(decay), kd, vh, kk, heads(ad))
        if d == 0:
            ys.append(rwkv7_scan(*ins, reverse=False))
        else:
            ys.append(ctx_first(rwkv7_scan(*[ctx_last(u, n_ctx) for u in ins], reverse=True), n_ctx))
        bonuses.append(jnp.sum(rh * kd * r_k, axis=-1, keepdims=True) * vh)
    y_sum = ys[0] + ys[1]
    mean = jnp.mean(y_sum, axis=-1, keepdims=True)
    var = jnp.mean(jnp.square(y_sum - mean), axis=-1, keepdims=True)
    y = ((y_sum - mean) * lax.rsqrt(var + RW_LN_EPS)).reshape(b, t, hw) * lnx_g + lnx_b
    y = (y + (bonuses[0] + bonuses[1]).reshape(b, t, hw)) * (jax.nn.sigmoid(g_lo) @ g2)
    yc = y[:, :n_ctx] if need_ctx else None
    return yc, y[:, n_ctx:]


def ssm_combine(e, l):
    ar1, ai1, br1, bi1 = e
    ar2, ai2, br2, bi2 = l
    return (ar2 * ar1 - ai2 * ai1, ar2 * ai1 + ai2 * ar1,
            ar2 * br1 - ai2 * bi1 + br2, ar2 * bi1 + ai2 * br1 + bi2)


def s5_branch(uc, ul, a_re, a_im, log_dt, b_re, b_im, c_re, c_im, d_skip, glu_w, glu_b, need_ctx):
    n_ctx = uc.shape[1]
    u = jnp.concatenate([uc, ul], axis=1).astype(jnp.float32)
    b, t, _ = u.shape
    u = u.reshape(b, t, S5_GROUPS, S5_GROUP_CH)
    ys = []
    for d in range(2):
        lr, li = a_re[d].astype(jnp.float32), a_im[d].astype(jnp.float32)
        dt = jnp.exp(log_dt[d].astype(jnp.float32))[:, None]
        mag = jnp.exp(lr * dt)
        ab_re, ab_im = mag * jnp.cos(li * dt), mag * jnp.sin(li * dt)
        den = lr * lr + li * li
        nr, ni = ab_re - 1.0, ab_im
        cf_re = (nr * lr + ni * li) / den
        cf_im = (ni * lr - nr * li) / den
        bre, bim = b_re[d].astype(jnp.float32), b_im[d].astype(jnp.float32)
        bb_re = cf_re[..., None] * bre - cf_im[..., None] * bim
        bb_im = cf_re[..., None] * bim + cf_im[..., None] * bre
        ud = u if d == 0 else ctx_last(u, n_ctx)
        bu_re = jnp.einsum('gpc,btgc->btgp', bb_re, ud)
        bu_im = jnp.einsum('gpc,btgc->btgp', bb_im, ud)
        elems = (jnp.broadcast_to(ab_re, bu_re.shape), jnp.broadcast_to(ab_im, bu_re.shape), bu_re, bu_im)
        _, _, x_re, x_im = lax.associative_scan(ssm_combine, elems, reverse=(d == 1), axis=1)
        yd = (jnp.einsum('gcp,btgp->btgc', c_re[d].astype(jnp.float32), x_re)
              - jnp.einsum('gcp,btgp->btgc', c_im[d].astype(jnp.float32), x_im))
        ys.append(yd if d == 0 else ctx_first(yd, n_ctx))
    y = (ys[0] + ys[1] + d_skip.astype(jnp.float32) * u).reshape(b, t, S5_GROUPS * S5_GROUP_CH)
    zg = jax.nn.gelu(y)
    out = zg * jax.nn.sigmoid(zg @ glu_w + glu_b)
    yc = out[:, :n_ctx] if need_ctx else None
    return yc, out[:, n_ctx:]


def diff_branch(pc, pl, cos, sin, lq1, lk1, lq2, lk2, subln_g, layer, need_ctx):
    nq = DIFF_HEADS * 2 * DIFF_HD

    def project(pz, rotary):
        b, t, _ = pz.shape
        q = pz[..., :nq].reshape(b, t, DIFF_HEADS, 2, DIFF_HD)
        k = pz[..., nq:2 * nq].reshape(b, t, DIFF_HEADS, 2, DIFF_HD)
        v = pz[..., 2 * nq:].reshape(b, t, DIFF_HEADS, 2 * DIFF_HD)
        if rotary:
            q = apply_rope(q, cos[:, None, None], sin[:, None, None])
            k = apply_rope(k, cos[:, None, None], sin[:, None, None])
        return q, k, v

    lam_init = 0.8 - 0.6 * math.exp(-0.3 * layer)
    lam = (jnp.exp(jnp.sum(lq1.astype(jnp.float32) * lk1.astype(jnp.float32)))
           - jnp.exp(jnp.sum(lq2.astype(jnp.float32) * lk2.astype(jnp.float32))) + lam_init)
    scale = DIFF_HD ** -0.5

    def finish(o):
        b, t = o.shape[:2]
        return (rmsnorm(o, subln_g, DIFF_EPS) * (1.0 - lam_init)).reshape(b, t, DIFF_HEADS * 2 * DIFF_HD)

    qc, kc, vc = project(pc, False)
    ql, kl, vl = project(pl, True)
    k_all = jnp.concatenate([kc, kl], axis=1)
    v_all = jnp.concatenate([vc, vl], axis=1)
    yl = finish(sweep_query_blocks(lambda qb: diff_attend(qb, k_all, v_all, lam, scale), ql))
    yc = finish(diff_attend(qc, kc, vc, lam, scale)) if need_ctx else None
    return yc, yl


def merge_branches(ys, gate_pre, w_branch, w_out):
    y = jnp.concatenate(ys, axis=-1)
    b, t, _ = y.shape
    proj = jnp.einsum('btni,nid->btnd', y.reshape(b, t, N_BRANCH, BRANCH_W), w_branch)
    gates = jax.nn.sigmoid(gate_pre.reshape(b, t, N_BRANCH, D_MODEL).astype(jnp.float32))
    return jnp.einsum('btnd,btnd->btd', gates.astype(proj.dtype), proj) @ w_out


def hier_route(x, wg, bg, we, be):
    n = x.shape[0]
    pg = jax.nn.softmax((x @ wg).astype(jnp.float32) + bg.astype(jnp.float32), axis=-1)
    pg_top, g_sel = lax.top_k(pg, 1)
    le = ((x @ we).astype(jnp.float32) + be.astype(jnp.float32)).reshape(n, MOE_GROUPS, MOE_PER_GROUP)
    le_sel = le[jnp.arange(n), g_sel[:, 0]]
    pe_top, e_sel = lax.top_k(jax.nn.softmax(le_sel, axis=-1), MOE_TOPK)
    wts = pg_top * pe_top / jnp.sum(pe_top, axis=-1, keepdims=True)
    return g_sel * MOE_PER_GROUP + e_sel, wts


def moe_dispatch(x, idx, wts, w_gate, w_up, w_down):
    n_tok, d = x.shape
    n_as = n_tok * MOE_TOPK
    flat_e = idx.reshape(n_as)
    order = jnp.argsort(flat_e)
    sorted_e = flat_e[order]
    counts = jnp.bincount(flat_e, length=MOE_EXPERTS)
    padded = (counts + MOE_BLOCK - 1) // MOE_BLOCK * MOE_BLOCK
    pad_end = jnp.cumsum(padded)
    pad_start = pad_end - padded
    start = jnp.cumsum(counts) - counts
    slot = pad_start[sorted_e] + jnp.arange(n_as, dtype=jnp.int32) - start[sorted_e]
    n_blocks = (n_as + MOE_EXPERTS * (MOE_BLOCK - 1) + MOE_BLOCK - 1) // MOE_BLOCK
    n_slots = n_blocks * MOE_BLOCK
    slot_tok = jnp.full((n_slots,), n_tok, jnp.int32).at[slot].set((order // MOE_TOPK).astype(jnp.int32))
    slot_w = jnp.zeros((n_slots,), jnp.float32).at[slot].set(wts.reshape(n_as)[order].astype(jnp.float32))
    block_e = jnp.minimum(jnp.searchsorted(pad_end, jnp.arange(n_blocks) * MOE_BLOCK, side='right'),
                          MOE_EXPERTS - 1)
    x_pad = jnp.concatenate([x, jnp.zeros((1, d), x.dtype)], axis=0)

    def run_block(args):
        tok, e = args
        xb = x_pad[tok]
        hb = jax.nn.silu(xb @ w_gate[e]) * (xb @ w_up[e])
        return hb @ w_down[e]

    yb = lax.map(run_block, (slot_tok.reshape(n_blocks, MOE_BLOCK), block_e))
    yb = yb.reshape(n_slots, d) * slot_w[:, None].astype(yb.dtype)
    y = jnp.zeros((n_tok + 1, d), yb.dtype).at[slot_tok].add(yb)
    return y[:n_tok]


def setup_inputs(seed: int = 0) -> dict:
    key = jax.random.key(seed)
    ks = iter(jax.random.split(key, 64))
    L, D = DEPTH, D_MODEL

    def nrm(shape, std):
        return jax.random.normal(next(ks), shape, jnp.float32) * std

    def uni(shape, lo, hi):
        return jax.random.uniform(next(ks), shape, jnp.float32, lo, hi)

    hw = RW_HEADS * RW_HEAD
    inp = {}
    inp['x'] = nrm((BATCH, SEQ, D), 1.0)
    inp['c'] = nrm((BATCH, D), 1.0)
    inp['ctx'] = nrm((BATCH, CTX_LEN, D), 1.0)
    inp['c_ctx'] = nrm((D,), 1.0)
    inp['w_mod'] = nrm((L, D, 6 * D), 0.5 * D ** -0.5)
    inp['b_mod'] = nrm((L, 6 * D), 0.02)
    inp['norm1_g'] = 1.0 + nrm((L, D), 0.02)
    inp['norm2_g'] = 1.0 + nrm((L, D), 0.02)
    inp['w_in'] = nrm((L, D, N_IN), D ** -0.5)
    inp['mla_q_norm_g'] = 1.0 + nrm((L, MLA_Q_LORA), 0.02)
    inp['mla_kv_norm_g'] = 1.0 + nrm((L, MLA_KV_LORA), 0.02)
    inp['mla_w_uq'] = nrm((L, MLA_Q_LORA, MLA_HEADS * (MLA_NOPE + MLA_ROPE)), MLA_Q_LORA ** -0.5)
    inp['mla_w_ukv'] = nrm((L, MLA_KV_LORA, MLA_HEADS * (MLA_NOPE + MLA_V)), MLA_KV_LORA ** -0.5)
    inp['rw_mu'] = uni((L, RW_IN), 0.0, 1.0)
    inp['rw_w0'] = uni((L, 2, hw), -6.5, -1.5)
    inp['rw_w2'] = nrm((L, 2, RW_W_LORA, hw), 0.1)
    inp['rw_a0'] = nrm((L, 2, hw), 0.01)
    inp['rw_a2'] = nrm((L, 2, RW_A_LORA, hw), 0.1)
    inp['rw_g2'] = nrm((L, RW_G_LORA, hw), RW_G_LORA ** -0.5)
    inp['rw_k_k'] = 0.85 + nrm((L, hw), 0.02)
    inp['rw_k_a'] = 1.0 + nrm((L, hw), 0.02)
    inp['rw_r_k'] = nrm((L, RW_HEADS, RW_HEAD), 0.1)
    inp['rw_lnx_g'] = 1.0 + nrm((L, hw), 0.02)
    inp['rw_lnx_b'] = nrm((L, hw), 0.02)
    inp['s5_a_re'] = -0.5 + nrm((L, 2, S5_GROUPS, S5_STATE), 0.01)
    inp['s5_a_im'] = math.pi * jnp.arange(S5_STATE, dtype=jnp.float32) + nrm((L, 2, S5_GROUPS, S5_STATE), 0.01)
    inp['s5_log_dt'] = uni((L, 2, S5_GROUPS), math.log(1e-3), math.log(1e-1))
    inp['s5_b_re'] = nrm((L, 2, S5_GROUPS, S5_STATE, S5_GROUP_CH), (2 * S5_GROUP_CH) ** -0.5)
    inp['s5_b_im'] = nrm((L, 2, S5_GROUPS, S5_STATE, S5_GROUP_CH), (2 * S5_GROUP_CH) ** -0.5)
    inp['s5_c_re'] = nrm((L, 2, S5_GROUPS, S5_GROUP_CH, S5_STATE), S5_STATE ** -0.5)
    inp['s5_c_im'] = nrm((L, 2, S5_GROUPS, S5_GROUP_CH, S5_STATE), S5_STATE ** -0.5)
    inp['s5_d'] = nrm((L, S5_GROUPS, S5_GROUP_CH), 1.0)
    inp['s5_glu_w'] = nrm((L, S5_IN, S5_IN), S5_IN ** -0.5)
    inp['s5_glu_b'] = nrm((L, S5_IN), 0.02)
    inp['diff_lq1'] = nrm((L, DIFF_HD), 0.1)
    inp['diff_lk1'] = nrm((L, DIFF_HD), 0.1)
    inp['diff_lq2'] = nrm((L, DIFF_HD), 0.1)
    inp['diff_lk2'] = nrm((L, DIFF_HD), 0.1)
    inp['diff_subln_g'] = 1.0 + nrm((L, 2 * DIFF_HD), 0.02)
    inp['w_branch'] = nrm((L, N_BRANCH, BRANCH_W, D), BRANCH_W ** -0.5)
    inp['w_out'] = nrm((L, D, D), D ** -0.5)
    inp['router_g_w'] = nrm((L, D, MOE_GROUPS), D ** -0.5)
    inp['router_g_b'] = nrm((L, MOE_GROUPS), 0.01)
    inp['router_e_w'] = nrm((L, D, MOE_EXPERTS), D ** -0.5)
    inp['router_e_b'] = nrm((L, MOE_EXPERTS), 0.01)
    inp['exp_w_gate'] = nrm((L, MOE_EXPERTS, D, D_EXPERT), D ** -0.5)
    inp['exp_w_up'] = nrm((L, MOE_EXPERTS, D, D_EXPERT), D ** -0.5)
    inp['exp_w_down'] = nrm((L, MOE_EXPERTS, D_EXPERT, D), D_EXPERT ** -0.5)
    inp['final_norm_g'] = 1.0 + nrm((D,), 0.02)
    return inp


def reference(x, c, ctx, c_ctx, w_mod, b_mod, norm1_g, norm2_g, w_in,
              mla_q_norm_g, mla_kv_norm_g, mla_w_uq, mla_w_ukv,
              rw_mu, rw_w0, rw_w2, rw_a0, rw_a2, rw_g2, rw_k_k, rw_k_a, rw_r_k, rw_lnx_g, rw_lnx_b,
              s5_a_re, s5_a_im, s5_log_dt, s5_b_re, s5_b_im, s5_c_re, s5_c_im, s5_d, s5_glu_w, s5_glu_b,
              diff_lq1, diff_lk1, diff_lq2, diff_lk2, diff_subln_g,
              w_branch, w_out,
              router_g_w, router_g_b, router_e_w, router_e_b, exp_w_gate, exp_w_up, exp_w_down,
              final_norm_g):
    n_lat = x.shape[1]
    cos, sin = axial_rope_tables(n_lat, ROPE_DIM)
    silu_c = jax.nn.silu(c)[:, None, :]
    silu_cc = jax.nn.silu(c_ctx)
    h, hc = x, ctx
    for l in range(DEPTH):
        need_ctx = l < DEPTH - 1
        mod = jnp.split(silu_c @ w_mod[l] + b_mod[l], 6, axis=-1)
        mod_c = jnp.split(silu_cc @ w_mod[l] + b_mod[l], 6, axis=-1)

        xl = rmsnorm(h, norm1_g[l]) * (1.0 + mod[1]) + mod[0]
        xc = rmsnorm(hc, norm1_g[l]) * (1.0 + mod_c[1]) + mod_c[0]
        pl = jnp.split(xl @ w_in[l], IN_SPLITS, axis=-1)
        pc = jnp.split(xc @ w_in[l], IN_SPLITS, axis=-1)
        ya = mla_branch(pc[0], pl[0], cos, sin, mla_q_norm_g[l], mla_kv_norm_g[l], mla_w_uq[l], mla_w_ukv[l],
                        need_ctx)
        yb = rwkv_branch(pc[1], pl[1], rw_mu[l], rw_w0[l], rw_w2[l], rw_a0[l], rw_a2[l], rw_g2[l],
                         rw_k_k[l], rw_k_a[l], rw_r_k[l], rw_lnx_g[l], rw_lnx_b[l], need_ctx)
        ys5 = s5_branch(pc[2], pl[2], s5_a_re[l], s5_a_im[l], s5_log_dt[l], s5_b_re[l], s5_b_im[l],
                        s5_c_re[l], s5_c_im[l], s5_d[l], s5_glu_w[l], s5_glu_b[l], need_ctx)
        yd = diff_branch(pc[3], pl[3], cos, sin, diff_lq1[l], diff_lk1[l], diff_lq2[l], diff_lk2[l],
                         diff_subln_g[l], l, need_ctx)
        h = h + mod[2] * merge_branches([ya[1], yb[1], ys5[1], yd[1]], pl[4], w_branch[l], w_out[l])
        if need_ctx:
            hc = hc + mod_c[2] * merge_branches([ya[0], yb[0], ys5[0], yd[0]], pc[4], w_branch[l], w_out[l])

        xl2 = (rmsnorm(h, norm2_g[l]) * (1.0 + mod[4]) + mod[3]).reshape(-1, D_MODEL)
        if need_ctx:
            xc2 = (rmsnorm(hc, norm2_g[l]) * (1.0 + mod_c[4]) + mod_c[3]).reshape(-1, D_MODEL)
            tokens = jnp.concatenate([xl2, xc2], axis=0)
        else:
            tokens = xl2
        idx, wts = hier_route(tokens, router_g_w[l], router_g_b[l], router_e_w[l], router_e_b[l])
        y = moe_dispatch(tokens, idx, wts, exp_w_gate[l], exp_w_up[l], exp_w_down[l])
        n_l = xl2.shape[0]
        h = h + mod[5] * y[:n_l].reshape(h.shape)
        if need_ctx:
            hc = hc + mod_c[5] * y[n_l:].reshape(hc.shape)
    return rmsnorm(h, final_norm_g)
```

```python
import functools
import math

import jax
import jax.numpy as jnp
import numpy as np
from jax import lax
from jax.experimental import pallas as pl
from jax.experimental.pallas import tpu as pltpu

F32 = jnp.float32
BF16 = jnp.bfloat16

TM = 256
VMEM_LIMIT = 48 * 1024 * 1024

D_MODEL = 1024
GRID_W = 64
ROPE_DIM = 32
ROPE_BASE = 10000.0
NORM_EPS = 1e-6
MLA_HEADS, MLA_NOPE, MLA_ROPE, MLA_V = 4, 64, 32, 64
MLA_Q_LORA, MLA_KV_LORA = 256, 128
RW_HEADS, RW_HEAD = 4, 64
RW_LN_EPS = 64e-5
S5_GROUPS, S5_GROUP_CH, S5_STATE = 16, 16, 64
DIFF_HEADS, DIFF_HD = 4, 32
DIFF_EPS = 1e-5
MOE_GROUPS, MOE_PER_GROUP, MOE_TOPK = 4, 8, 2
MOE_EXPERTS = MOE_GROUPS * MOE_PER_GROUP
D_EXPERT = 512
MOE_BLK = 256
RW_CHUNK = 64
S5_CHUNK = 128

EXT_RW, EXT_MLA, EXT_S5, EXT_DIFF, EXT_GATE = 0, 1024, 1536, 1792, 3072
N_EXT = 7168


def _cparams(sem, vmem=VMEM_LIMIT):
    return pltpu.CompilerParams(dimension_semantics=sem, vmem_limit_bytes=vmem)


def _dot(a, b):
    return jnp.dot(a, b, preferred_element_type=F32)


def _dot_nt(a, b):
    return lax.dot_general(a, b, (((1,), (1,)), ((), ())), preferred_element_type=F32)


def _split_dot(x, e):
    hi = x.astype(BF16)
    mid = (x - hi.astype(F32)).astype(BF16)
    return _dot(hi, e) + _dot(mid, e)


def _block_ones(n, blk):
    r = lax.broadcasted_iota(jnp.int32, (n, n), 0) // blk
    c = lax.broadcasted_iota(jnp.int32, (n, n), 1) // blk
    return (r == c).astype(BF16)


def _sigmoid(x):
    return 1.0 / (1.0 + jnp.exp(-x))


def _silu(x):
    return x * _sigmoid(x)


def _softplus(x):
    return jnp.maximum(x, 0.0) + jnp.log(1.0 + jnp.exp(-jnp.abs(x)))


def _rms(x, g, eps):
    return x * lax.rsqrt(jnp.mean(x * x, axis=-1, keepdims=True) + eps) * g


def _mm_kernel(x_ref, w_ref, b_ref, o_ref, *, pre_silu):
    x = x_ref[...].astype(F32)
    if pre_silu:
        x = _silu(x)
    o_ref[...] = _dot(x.astype(BF16), w_ref[...]) + b_ref[...]


def _mm(x, w, b, *, tm, tn, pre_silu=False, name="mm"):
    m, k = x.shape
    n = w.shape[1]
    return pl.pallas_call(
        functools.partial(_mm_kernel, pre_silu=pre_silu),
        grid=(m // tm, n // tn),
        in_specs=[pl.BlockSpec((tm, k), lambda i, j: (i, 0)),
                  pl.BlockSpec((k, tn), lambda i, j: (0, j)),
                  pl.BlockSpec((1, tn), lambda i, j: (0, j))],
        out_specs=pl.BlockSpec((tm, tn), lambda i, j: (i, j)),
        out_shape=jax.ShapeDtypeStruct((m, n), F32),
        compiler_params=_cparams(("parallel", "arbitrary")),
        name=name,
    )(x, w, b)


def _inproj_kernel(h_ref, g_ref, sh_ref, sc_ref, w_ref, o_ref, xn_ref):
    @pl.when(pl.program_id(2) == 0)
    def _():
        x = _rms(h_ref[...], g_ref[...], NORM_EPS)
        xn_ref[...] = (x * (1.0 + sc_ref[0]) + sh_ref[0]).astype(BF16)
    o_ref[...] = _dot(xn_ref[...], w_ref[...])


def _inproj(h, g, shift, scale, w_ext, *, nb, tpb):
    n = h.shape[0]
    tn = 1024
    modmap = lambda b, r, j: (jnp.where(r == 0, nb, b), 0, 0)
    return pl.pallas_call(
        _inproj_kernel,
        grid=(nb, tpb, N_EXT // tn),
        in_specs=[pl.BlockSpec((TM, D_MODEL), lambda b, r, j: (b * tpb + r, 0)),
                  pl.BlockSpec((1, D_MODEL), lambda b, r, j: (0, 0)),
                  pl.BlockSpec((1, 1, D_MODEL), modmap),
                  pl.BlockSpec((1, 1, D_MODEL), modmap),
                  pl.BlockSpec((D_MODEL, tn), lambda b, r, j: (0, j))],
        out_specs=pl.BlockSpec((TM, tn), lambda b, r, j: (b * tpb + r, j)),
        out_shape=jax.ShapeDtypeStruct((n, N_EXT), F32),
        scratch_shapes=[pltpu.VMEM((TM, D_MODEL), BF16)],
        compiler_params=_cparams(("parallel", "parallel", "arbitrary")),
        name="inproj",
    )(h, g, shift, scale, w_ext)


def _mla_prep_kernel(p_ref, gq_ref, gkv_ref, wa_ref, wb_ref, wk_ref, wv_ref, pk_ref,
                     cq_ref, sq_ref, ck_ref, q_ref, k_ref, v_ref):
    seg = p_ref[...]
    nq = _rms(seg[:, :MLA_Q_LORA], gq_ref[...], NORM_EPS).astype(BF16)
    nkv = _rms(seg[:, MLA_Q_LORA:MLA_Q_LORA + MLA_KV_LORA], gkv_ref[...], NORM_EPS).astype(BF16)
    cq = jnp.concatenate([cq_ref[...]] * MLA_HEADS, axis=1)
    sq = jnp.concatenate([sq_ref[...]] * MLA_HEADS, axis=1)
    q = _dot(nq, wa_ref[...]) * cq + _dot(nq, wb_ref[...]) * sq
    q_ref[...] = q.astype(BF16)
    kr = (seg[:, 384:512] * ck_ref[...]).astype(BF16)
    k_ref[...] = (_dot(nkv, wk_ref[...]) + _dot(kr, pk_ref[...])).astype(BF16)
    v_ref[...] = _dot(nkv, wv_ref[...]).astype(BF16)


def _mla_prep(p, gq, gkv, wa, wb, wk, wv, pk, cq_tab, sq_tab, ck_tab, *, nb, tpb):
    n = p.shape[0]
    tok = lambda b, r: (b * tpb + r, 0)
    pos = lambda b, r: (r, 0)
    full = lambda b, r: (0, 0)
    return pl.pallas_call(
        _mla_prep_kernel,
        grid=(nb, tpb),
        in_specs=[pl.BlockSpec((TM, 512), lambda b, r: (b * tpb + r, EXT_MLA // 512)),
                  pl.BlockSpec((1, MLA_Q_LORA), full),
                  pl.BlockSpec((1, MLA_KV_LORA), full),
                  pl.BlockSpec((MLA_Q_LORA, 512), full),
                  pl.BlockSpec((MLA_Q_LORA, 512), full),
                  pl.BlockSpec((MLA_KV_LORA, 512), full),
                  pl.BlockSpec((MLA_KV_LORA, 256), full),
                  pl.BlockSpec((128, 512), full),
                  pl.BlockSpec((TM, 128), pos),
                  pl.BlockSpec((TM, 128), pos),
                  pl.BlockSpec((TM, 128), pos)],
        out_specs=[pl.BlockSpec((TM, 512), tok),
                   pl.BlockSpec((TM, 512), tok),
                   pl.BlockSpec((TM, 256), tok)],
        out_shape=[jax.ShapeDtypeStruct((n, 512), BF16),
                   jax.ShapeDtypeStruct((n, 512), BF16),
                   jax.ShapeDtypeStruct((n, 256), BF16)],
        compiler_params=_cparams(("parallel", "parallel")),
        name="mla_prep",
    )(p, gq, gkv, wa, wb, wk, wv, pk, cq_tab, sq_tab, ck_tab)


def _flash_mla_kernel(q_ref, k_ref, v_ref, o_ref, m_ref, l_ref, acc_ref):
    kk = pl.program_id(3)

    @pl.when(kk == 0)
    def _():
        m_ref[...] = jnp.full(m_ref.shape, -jnp.inf, F32)
        l_ref[...] = jnp.zeros(l_ref.shape, F32)
        acc_ref[...] = jnp.zeros(acc_ref.shape, F32)

    v = v_ref[...]
    lane = lax.broadcasted_iota(jnp.int32, acc_ref.shape, 1)
    first = lane < MLA_V
    alphas, pvs = [], []
    for h in range(2):
        s = _dot_nt(q_ref[:, h * 128:(h + 1) * 128], k_ref[:, h * 128:(h + 1) * 128])
        m_prev = m_ref[h]
        m_new = jnp.maximum(m_prev, jnp.max(s, axis=-1, keepdims=True))
        alpha = jnp.exp(m_prev - m_new)
        p = jnp.exp(s - m_new)
        l_ref[h] = alpha * l_ref[h] + jnp.sum(p, axis=-1, keepdims=True)
        m_ref[h] = m_new
        alphas.append(alpha)
        pvs.append(_dot(p.astype(BF16), v))
    acc_ref[...] = (jnp.where(first, alphas[0], alphas[1]) * acc_ref[...]
                    + jnp.where(first, pvs[0], pvs[1]))

    @pl.when(kk == pl.num_programs(3) - 1)
    def _():
        o_ref[...] = acc_ref[...] / jnp.where(first, l_ref[0], l_ref[1])


def _flash_mla(q, k, v, *, nb, t_all, q_start, q_len, k_len, tk):
    tq = TM
    tpb_q, tpb_k = t_all // tq, t_all // tk
    q0 = q_start // tq
    nq = q_len // tq
    return pl.pallas_call(
        _flash_mla_kernel,
        grid=(nb, 2, nq, k_len // tk),
        in_specs=[pl.BlockSpec((tq, 256), lambda b, hp, i, kk: (b * tpb_q + q0 + i, hp)),
                  pl.BlockSpec((tk, 256), lambda b, hp, i, kk: (b * tpb_k + kk, hp)),
                  pl.BlockSpec((tk, 128), lambda b, hp, i, kk: (b * tpb_k + kk, hp))],
        out_specs=pl.BlockSpec((tq, 128), lambda b, hp, i, kk: (b * nq + i, hp)),
        out_shape=jax.ShapeDtypeStruct((nb * q_len, 256), F32),
        scratch_shapes=[pltpu.VMEM((2, tq, 1), F32), pltpu.VMEM((2, tq, 1), F32),
                        pltpu.VMEM((tq, 128), F32)],
        compiler_params=_cparams(("parallel", "parallel", "parallel", "arbitrary")),
        name="flash_mla",
    )(q, k, v)


def _diff_prep(p, cos_tab, sin_tab, *, nb, tpb):
    n = p.shape[0]
    tok = lambda b, r: (b * tpb + r, 0)
    pos = lambda b, r: (r, 0)
    return pl.pallas_call(
        _diff_prep_kernel_cols,
        grid=(nb, tpb),
        in_specs=[pl.BlockSpec((TM, 256), lambda b, r, c=c: (b * tpb + r, EXT_DIFF // 256 + c))
                  for c in range(5)]
                 + [pl.BlockSpec((TM, 256), pos), pl.BlockSpec((TM, 256), pos)],
        out_specs=[pl.BlockSpec((TM, 256), tok)] * 3,
        out_shape=[jax.ShapeDtypeStruct((n, 256), BF16)] * 3,
        compiler_params=_cparams(("parallel", "parallel")),
        name="diff_prep",
    )(p, p, p, p, p, cos_tab, sin_tab)


def _diff_prep_kernel_cols(q_in, k_in, v_in, qr_in, kr_in, cos_ref, sin_ref, q_ref, k_ref, v_ref):
    cos, sin = cos_ref[...], sin_ref[...]
    scale = DIFF_HD ** -0.5
    q_ref[...] = ((q_in[...] * cos + qr_in[...] * sin) * scale).astype(BF16)
    k_ref[...] = (k_in[...] * cos + kr_in[...] * sin).astype(BF16)
    v_ref[...] = v_in[...].astype(BF16)


def _flash_diff_kernel(q_ref, k_ref, v_ref, lam_ref, g_ref, o_ref, m_ref, l_ref, acc_ref, *, lam_init):
    kk = pl.program_id(2)
    tq = q_ref.shape[0]

    @pl.when(kk == 0)
    def _():
        m_ref[...] = jnp.full(m_ref.shape, -jnp.inf, F32)
        l_ref[...] = jnp.zeros(l_ref.shape, F32)
        acc_ref[...] = jnp.zeros(acc_ref.shape, F32)

    q = q_ref[...]
    k = k_ref[...]
    v = v_ref[...]
    lane = lax.broadcasted_iota(jnp.int32, (tq, 256), 1)
    for n in range(2):
        acc = acc_ref[n]
        for h in range(DIFF_HEADS):
            idx = h * 2 + n
            qm = jnp.where((lane // DIFF_HD) == idx, q, jnp.zeros_like(q))
            s = _dot_nt(qm, k)
            m_prev = m_ref[idx]
            m_new = jnp.maximum(m_prev, jnp.max(s, axis=-1, keepdims=True))
            alpha = jnp.exp(m_prev - m_new)
            p = jnp.exp(s - m_new)
            l_ref[idx] = alpha * l_ref[idx] + jnp.sum(p, axis=-1, keepdims=True)
            m_ref[idx] = m_new
            pv = _dot(p.astype(BF16), v)
            acc = jnp.where((lane // (2 * DIFF_HD)) == h, alpha * acc + pv, acc)
        acc_ref[n] = acc

    @pl.when(kk == pl.num_programs(2) - 1)
    def _():
        outs = []
        for n in range(2):
            inv = jnp.zeros((tq, 256), F32)
            for h in range(DIFF_HEADS):
                inv = jnp.where((lane // (2 * DIFF_HD)) == h, 1.0 / l_ref[h * 2 + n], inv)
            outs.append(acc_ref[n] * inv)
        o = outs[0] - lam_ref[...] * outs[1]
        ms = _split_dot(o * o, _block_ones(256, 2 * DIFF_HD)) * (1.0 / (2 * DIFF_HD))
        o_ref[...] = o * lax.rsqrt(ms + DIFF_EPS) * g_ref[...] * (1.0 - lam_init)


def _flash_diff(q, k, v, lam_row, g_row, *, lam_init, nb, t_all, q_start, q_len, k_len, tk):
    tq = TM
    tpb_q, tpb_k = t_all // tq, t_all // tk
    q0 = q_start // tq
    nq = q_len // tq
    return pl.pallas_call(
        functools.partial(_flash_diff_kernel, lam_init=lam_init),
        grid=(nb, nq, k_len // tk),
        in_specs=[pl.BlockSpec((tq, 256), lambda b, i, kk: (b * tpb_q + q0 + i, 0)),
                  pl.BlockSpec((tk, 256), lambda b, i, kk: (b * tpb_k + kk, 0)),
                  pl.BlockSpec((tk, 256), lambda b, i, kk: (b * tpb_k + kk, 0)),
                  pl.BlockSpec((1, 256), lambda b, i, kk: (0, 0)),
                  pl.BlockSpec((1, 256), lambda b, i, kk: (0, 0))],
        out_specs=pl.BlockSpec((tq, 256), lambda b, i, kk: (b * nq + i, 0)),
        out_shape=jax.ShapeDtypeStruct((nb * q_len, 256), F32),
        scratch_shapes=[pltpu.VMEM((8, tq, 1), F32), pltpu.VMEM((8, tq, 1), F32),
                        pltpu.VMEM((2, tq, 256), F32)],
        compiler_params=_cparams(("parallel", "parallel", "arbitrary")),
        name="flash_diff",
    )(q, k, v, lam_row, g_row)


def _rw_prep_kernel(p_ref, prev_ref, next_ref, mu_ref, wlo_ref, g2_ref, vec_ref,
                    r_ref, v_ref, kk_ref, w0_ref, k0_ref, b0_ref, w1_ref, k1_ref, b1_ref,
                    bonus_ref, gate_ref, *, r0, lat_last):
    r = pl.program_id(1) + r0
    p = p_ref[...]
    row = lax.broadcasted_iota(jnp.int32, p.shape, 0)
    first_tile = jnp.logical_or(r == 0, r == 1)
    last_tile = jnp.logical_or(r == 0, r == lat_last)
    prev_row = jnp.where(first_tile, 0.0, prev_ref[7:8, :])
    next_row = jnp.where(last_tile, 0.0, next_ref[0:1, :])
    up = jnp.where(row == 0, prev_row, pltpu.roll(p, 1, 0))
    dn = jnp.where(row == TM - 1, next_row, pltpu.roll(p, TM - 1, 0))
    z = p + (0.5 * (up + dn) - p) * mu_ref[...]

    hw = RW_HEADS * RW_HEAD
    rr, k, v = z[:, :hw], z[:, hw:2 * hw], z[:, 2 * hw:3 * hw]
    lo = z[:, 3 * hw:3 * hw + 128]
    lane = lax.broadcasted_iota(jnp.int32, lo.shape, 1)
    lo = jnp.where(lane < 64, jnp.tanh(lo), lo).astype(BF16)
    wa = _dot(lo, wlo_ref[...])
    gate_ref[...] = _dot(_sigmoid(z[:, 3 * hw + 128:]).astype(BF16), g2_ref[...])

    e4 = _block_ones(hw, RW_HEAD)
    k_k, k_a, r_k = vec_ref[0:1, :], vec_ref[1:2, :], vec_ref[2:3, :]
    kk = k * k_k
    nrm = jnp.maximum(jnp.sqrt(_split_dot(kk * kk, e4)), 1e-12)
    kk = kk / nrm
    r_ref[...] = rr
    v_ref[...] = v
    kk_ref[...] = kk
    ksum = jnp.zeros_like(k)
    for d, (w_ref, kd_ref, b_ref) in enumerate(((w0_ref, k0_ref, b0_ref), (w1_ref, k1_ref, b1_ref))):
        w0 = vec_ref[3 + d:4 + d, :]
        a0 = vec_ref[5 + d:6 + d, :]
        wd = -_softplus(-(w0 + wa[:, d * hw:(d + 1) * hw])) - 0.5
        w_ref[...] = jnp.exp(-jnp.exp(wd))
        ad = _sigmoid(a0 + wa[:, (2 + d) * hw:(3 + d) * hw])
        kd = k * (1.0 + (ad - 1.0) * k_a)
        kd_ref[...] = kd
        b_ref[...] = kk * ad
        ksum = ksum + kd
    bonus_ref[...] = _split_dot(rr * ksum * r_k, e4) * v


def _rw_prep(p, mu, wlo, g2, vecs, *, nb, tpb, r0, nt):
    n_out = nb * nt * TM
    hw = RW_HEADS * RW_HEAD
    n_rows8 = p.shape[0] // 8
    tok = lambda b, r: (b * tpb + r0 + r, EXT_RW // 1024)
    prev = lambda b, r: (jnp.maximum((b * tpb + r0 + r) * (TM // 8) - 1, 0), EXT_RW // 1024)
    nxt = lambda b, r: (jnp.minimum((b * tpb + r0 + r + 1) * (TM // 8), n_rows8 - 1), EXT_RW // 1024)
    out = lambda b, r: (b * nt + r, 0)
    full = lambda b, r: (0, 0)
    return pl.pallas_call(
        functools.partial(_rw_prep_kernel, r0=r0, lat_last=tpb - 1),
        grid=(nb, nt),
        in_specs=[pl.BlockSpec((TM, 1024), tok),
                  pl.BlockSpec((8, 1024), prev),
                  pl.BlockSpec((8, 1024), nxt),
                  pl.BlockSpec((1, 1024), full),
                  pl.BlockSpec((128, 4 * hw), full),
                  pl.BlockSpec((128, hw), full),
                  pl.BlockSpec((8, hw), full)],
        out_specs=[pl.BlockSpec((TM, hw), out)] * 11,
        out_shape=[jax.ShapeDtypeStruct((n_out, hw), F32)] * 11,
        compiler_params=_cparams(("parallel", "parallel")),
        name="rw_prep",
    )(p, p, p, mu, wlo, g2, vecs)


def _rw_scan_kernel(rf, vf, kkf, wf, kf, bf, rb, vb, kkb, wb, kb, bb, yf_ref, yb_ref, s_ref, *, nb):
    c = RW_CHUNK

    @pl.when(pl.program_id(0) == 0)
    def _():
        s_ref[...] = jnp.zeros(s_ref.shape, F32)

    yf_ref[...] = jnp.zeros(yf_ref.shape, F32)
    yb_ref[...] = jnp.zeros(yb_ref.shape, F32)
    e2 = _block_ones(128, RW_HEAD)
    lane = lax.broadcasted_iota(jnp.int32, (RW_HEAD, 128), 1)
    sub = lax.broadcasted_iota(jnp.int32, (RW_HEAD, 128), 0)
    diag = (lane % RW_HEAD) == sub
    lane_t = lane % RW_HEAD
    dirs = ((rf, vf, kkf, wf, kf, bf, yf_ref), (rb, vb, kkb, wb, kb, bb, yb_ref))

    def group(g, carry):
        tiles = {}
        for d, refs in enumerate(dirs):
            base = pl.multiple_of((g if d == 0 else c // 8 - 1 - g) * 8, 8)
            for b in range(nb):
                for hp in range(2):
                    tiles[d, b, hp] = (base, [ref[b, pl.ds(base, 8), pl.ds(hp * 128, 128)] for ref in refs[:6]])
        for jj in range(8):
            for d, refs in enumerate(dirs):
                y_ref = refs[6]
                j = jj if d == 0 else 7 - jj
                for b in range(nb):
                    for hp in range(2):
                        u = (d * nb + b) * 2 + hp
                        base, tl = tiles[d, b, hp]
                        r_t, v_t, kk_t, w_t, k_t, b_t = [x[j:j + 1, :] for x in tl]
                        s = s_ref[u]
                        sa = _split_dot(s * kk_t, e2)
                        vcol = _split_dot(jnp.where(diag, v_t, 0.0), e2)
                        s = s * w_t - sa * b_t + vcol * k_t
                        s_ref[u] = s
                        y = _split_dot(s * r_t, e2)
                        y_ref[b, hp, 0] = jnp.where(lane_t == base + j, y, y_ref[b, hp, 0])
        return carry

    lax.fori_loop(0, c // 8, group, 0)


def _rw_scan(r, v, kk, w0, k0, b0, w1, k1, b1, *, nb, t_all, n_ctx):
    c = RW_CHUNK
    nc, ncc = t_all // c, n_ctx // c
    hw = RW_HEADS * RW_HEAD
    shp = lambda a: a.reshape(nb, t_all, hw)
    fwd = lambda j: (0, j, 0)
    bwd = lambda j: (0, jnp.where(j < ncc, ncc - 1 - j, nc - 1 - (j - ncc)), 0)
    blk = (nb, c, hw)
    yblk = (nb, 2, 1, RW_HEAD, 128)
    yshape = jax.ShapeDtypeStruct((nb, 2, nc, RW_HEAD, 128), F32)
    yf, yb = pl.pallas_call(
        functools.partial(_rw_scan_kernel, nb=nb),
        grid=(nc,),
        in_specs=[pl.BlockSpec(blk, fwd)] * 6 + [pl.BlockSpec(blk, bwd)] * 6,
        out_specs=[pl.BlockSpec(yblk, lambda j: (0, 0, fwd(j)[1], 0, 0)),
                   pl.BlockSpec(yblk, lambda j: (0, 0, bwd(j)[1], 0, 0))],
        out_shape=[yshape, yshape],
        scratch_shapes=[pltpu.VMEM((2 * nb * 2, RW_HEAD, 128), F32)],
        compiler_params=_cparams(("arbitrary",)),
        name="rw_scan",
    )(shp(r), shp(v), shp(kk), shp(w0), shp(k0), shp(b0),
      shp(r), shp(v), shp(kk), shp(w1), shp(k1), shp(b1))

    def untangle(y):
        y = y.reshape(nb, 2, nc, RW_HEAD, 2, c)
        return y.transpose(0, 2, 5, 1, 4, 3).reshape(nb * t_all, hw)
    return untangle(yf), untangle(yb)


def _rw_post_kernel(yf_ref, yb_ref, bonus_ref, gate_ref, vec_ref, o_ref):
    e4 = _block_ones(RW_HEADS * RW_HEAD, RW_HEAD)
    y = yf_ref[...] + yb_ref[...]
    mean = _split_dot(y, e4) * (1.0 / RW_HEAD)
    yc = y - mean
    var = _split_dot(yc * yc, e4) * (1.0 / RW_HEAD)
    yn = yc * lax.rsqrt(var + RW_LN_EPS) * vec_ref[0:1, :] + vec_ref[1:2, :]
    o_ref[...] = (yn + bonus_ref[...]) * gate_ref[...]


def _rw_post(yf, yb, bonus, gate, vecs, *, nb, tpb_y, r0_y, nt):
    hw = RW_HEADS * RW_HEAD
    n_out = bonus.shape[0]
    ytok = lambda b, r: (b * tpb_y + r0_y + r, 0)
    tok = lambda b, r: (b * nt + r, 0)
    return pl.pallas_call(
        _rw_post_kernel,
        grid=(nb, nt),
        in_specs=[pl.BlockSpec((TM, hw), ytok), pl.BlockSpec((TM, hw), ytok),
                  pl.BlockSpec((TM, hw), tok), pl.BlockSpec((TM, hw), tok),
                  pl.BlockSpec((8, hw), lambda b, r: (0, 0))],
        out_specs=pl.BlockSpec((TM, hw), tok),
        out_shape=jax.ShapeDtypeStruct((n_out, hw), F32),
        compiler_params=_cparams(("parallel", "parallel")),
        name="rw_post",
    )(yf, yb, bonus, gate, vecs)


def _s5_bu_kernel(p_ref, w_ref, o_ref):
    o_ref[...] = _dot(p_ref[...].astype(BF16), w_ref[...])


def _s5_bu(p, w_b, *, nb, tpb):
    n = p.shape[0]
    nst = S5_GROUPS * S5_STATE
    return pl.pallas_call(
        _s5_bu_kernel,
        grid=(nb * tpb, 4),
        in_specs=[pl.BlockSpec((TM, 256), lambda i, j: (i, EXT_S5 // 256)),
                  pl.BlockSpec((256, nst), lambda i, j: (0, j))],
        out_specs=pl.BlockSpec((TM, nst), lambda i, j: (i, j)),
        out_shape=jax.ShapeDtypeStruct((n, 4 * nst), F32),
        compiler_params=_cparams(("parallel", "arbitrary")),
        name="s5_bu",
    )(p, w_b)


def _s5_scan_kernel(buf_ref, bub_ref, ab_ref, xf_ref, xb_ref, st_ref, *, nb):
    c = S5_CHUNK
    nst = S5_GROUPS * S5_STATE

    @pl.when(pl.program_id(0) == 0)
    def _():
        st_ref[...] = jnp.zeros(st_ref.shape, F32)

    dirs = ((buf_ref, xf_ref), (bub_ref, xb_ref))

    def group(g, carry):
        for d, (bu_ref, x_ref) in enumerate(dirs):
            base = pl.multiple_of((g if d == 0 else c // 8 - 1 - g) * 8, 8)
            ar = ab_ref[d, :, 0:nst]
            ai = ab_ref[d, :, nst:2 * nst]
            for b in range(nb):
                u = d * nb + b
                xr = st_ref[u, :, 0:nst]
                xi = st_ref[u, :, nst:2 * nst]
                bur = bu_ref[b, pl.ds(base, 8), 0:nst]
                bui = bu_ref[b, pl.ds(base, 8), nst:2 * nst]
                rows_r, rows_i = [None] * 8, [None] * 8
                for jj in range(8):
                    j = jj if d == 0 else 7 - jj
                    xr, xi = (ar * xr - ai * xi + bur[j:j + 1, :], ar * xi + ai * xr + bui[j:j + 1, :])
                    rows_r[j], rows_i[j] = xr, xi
                st_ref[u, :, 0:nst] = xr
                st_ref[u, :, nst:2 * nst] = xi
                x_ref[b, pl.ds(base, 8), 0:nst] = jnp.concatenate(rows_r, axis=0)
                x_ref[b, pl.ds(base, 8), nst:2 * nst] = jnp.concatenate(rows_i, axis=0)
        return carry

    lax.fori_loop(0, c // 8, group, 0)


def _s5_scan(bu, ab, *, nb, t_all, n_ctx):
    c = S5_CHUNK
    nc, ncc = t_all // c, n_ctx // c
    nst2 = 2 * S5_GROUPS * S5_STATE
    bu3 = bu.reshape(nb, t_all, 2 * nst2)
    fwd = lambda j: j
    bwd = lambda j: jnp.where(j < ncc, ncc - 1 - j, nc - 1 - (j - ncc))
    xshape = jax.ShapeDtypeStruct((nb, t_all, nst2), F32)
    xf, xb = pl.pallas_call(
        functools.partial(_s5_scan_kernel, nb=nb),
        grid=(nc,),
        in_specs=[pl.BlockSpec((nb, c, nst2), lambda j: (0, fwd(j), 0)),
                  pl.BlockSpec((nb, c, nst2), lambda j: (0, bwd(j), 1)),
                  pl.BlockSpec((2, 1, nst2), lambda j: (0, 0, 0))],
        out_specs=[pl.BlockSpec((nb, c, nst2), lambda j: (0, fwd(j), 0)),
                   pl.BlockSpec((nb, c, nst2), lambda j: (0, bwd(j), 0))],
        out_shape=[xshape, xshape],
        scratch_shapes=[pltpu.VMEM((2 * nb, 1, nst2), F32)],
        compiler_params=_cparams(("arbitrary",)),
        name="s5_scan",
    )(bu3, bu3, ab)
    return xf.reshape(nb * t_all, nst2), xb.reshape(nb * t_all, nst2)


def _s5_post_kernel(xf_ref, xb_ref, u_ref, cf_ref, cb_ref, d_ref, gw_ref, gb_ref, o_ref):
    y = (_dot(xf_ref[...].astype(BF16), cf_ref[...]) + _dot(xb_ref[...].astype(BF16), cb_ref[...])
         + d_ref[...] * u_ref[...])
    zg = 0.5 * y * (1.0 + jnp.tanh(math.sqrt(2.0 / math.pi) * (y + 0.044715 * (y * y * y))))
    o_ref[...] = zg * _sigmoid(_dot(zg.astype(BF16), gw_ref[...]) + gb_ref[...])


def _s5_post(xf, xb, p, cf, cb, d_row, glu_w, glu_b, *, nb, tpb, r0, nt):
    nst2 = 2 * S5_GROUPS * S5_STATE
    tok = lambda b, r: (b * tpb + r0 + r, 0)
    full = lambda b, r: (0, 0)
    return pl.pallas_call(
        _s5_post_kernel,
        grid=(nb, nt),
        in_specs=[pl.BlockSpec((TM, nst2), tok), pl.BlockSpec((TM, nst2), tok),
                  pl.BlockSpec((TM, 256), lambda b, r: (b * tpb + r0 + r, EXT_S5 // 256)),
                  pl.BlockSpec((nst2, 256), full), pl.BlockSpec((nst2, 256), full),
                  pl.BlockSpec((1, 256), full), pl.BlockSpec((256, 256), full),
                  pl.BlockSpec((1, 256), full)],
        out_specs=pl.BlockSpec((TM, 256), lambda b, r: (b * nt + r, 0)),
        out_shape=jax.ShapeDtypeStruct((nb * nt * TM, 256), F32),
        compiler_params=_cparams(("parallel", "parallel")),
        name="s5_post",
    )(xf, xb, p, cf, cb, d_row, glu_w, glu_b)


def _merge_kernel(ya_ref, yb_ref, ys_ref, yd_ref, g0, g1, g2, g3, h_ref, m2_ref, wb_ref, wo_ref, o_ref):
    acc = None
    for n, (y_ref, g_ref) in enumerate(((ya_ref, g0), (yb_ref, g1), (ys_ref, g2), (yd_ref, g3))):
        term = _sigmoid(g_ref[...]) * _dot(y_ref[...].astype(BF16), wb_ref[n])
        acc = term if acc is None else acc + term
    o_ref[...] = h_ref[...] + m2_ref[0] * _dot(acc.astype(BF16), wo_ref[...])


def _merge(ya, yb, ys, yd, p, h, mod2, w_branch, w_out, *, nb, tpb, r0, nt):
    tok_in = lambda b, r: (b * tpb + r0 + r, 0)
    tok_out = lambda b, r: (b * nt + r, 0)
    full2 = lambda b, r: (0, 0)
    gate = [pl.BlockSpec((TM, D_MODEL), lambda b, r, c=c: (b * tpb + r0 + r, EXT_GATE // D_MODEL + c))
            for c in range(4)]
    return pl.pallas_call(
        _merge_kernel,
        grid=(nb, nt),
        in_specs=[pl.BlockSpec((TM, 256), tok_out)] * 4 + gate
                 + [pl.BlockSpec((TM, D_MODEL), tok_in),
                    pl.BlockSpec((1, 1, D_MODEL), lambda b, r: (jnp.where(r0 + r == 0, nb, b), 0, 0)),
                    pl.BlockSpec((4, 256, D_MODEL), lambda b, r: (0, 0, 0)),
                    pl.BlockSpec((D_MODEL, D_MODEL), full2)],
        out_specs=pl.BlockSpec((TM, D_MODEL), tok_out),
        out_shape=jax.ShapeDtypeStruct((nb * nt * TM, D_MODEL), F32),
        compiler_params=_cparams(("parallel", "parallel")),
        name="merge",
    )(ya, yb, ys, yd, p, p, p, p, h, mod2, w_branch, w_out)


def _router_kernel(h_ref, g_ref, sh_ref, sc_ref, wh_ref, wm_ref, b_ref, x_ref, lg_ref):
    x = _rms(h_ref[...], g_ref[...], NORM_EPS) * (1.0 + sc_ref[0]) + sh_ref[0]
    xh = x.astype(BF16)
    xm = (x - xh.astype(F32)).astype(BF16)
    x_ref[...] = xh
    lg_ref[...] = (_dot(xh, wh_ref[...]) + _dot(xm, wh_ref[...]) + _dot(xh, wm_ref[...])) + b_ref[...]


def _router(h, g, shift, scale, wh, wm, bias, *, nb, nt, ctx_first):
    tok = lambda b, r: (b * nt + r, 0)
    full = lambda b, r: (0, 0)
    if ctx_first:
        modmap = lambda b, r: (jnp.where(r == 0, nb, b), 0, 0)
    else:
        modmap = lambda b, r: (b, 0, 0)
    n = h.shape[0]
    return pl.pallas_call(
        _router_kernel,
        grid=(nb, nt),
        in_specs=[pl.BlockSpec((TM, D_MODEL), tok), pl.BlockSpec((1, D_MODEL), full),
                  pl.BlockSpec((1, 1, D_MODEL), modmap), pl.BlockSpec((1, 1, D_MODEL), modmap),
                  pl.BlockSpec((D_MODEL, 128), full), pl.BlockSpec((D_MODEL, 128), full),
                  pl.BlockSpec((1, 128), full)],
        out_specs=[pl.BlockSpec((TM, D_MODEL), tok), pl.BlockSpec((TM, 128), tok)],
        out_shape=[jax.ShapeDtypeStruct((n, D_MODEL), BF16), jax.ShapeDtypeStruct((n, 128), F32)],
        compiler_params=_cparams(("parallel", "parallel")),
        name="router",
    )(h, g, shift, scale, wh, wm, bias)


def _expert_kernel(be_ref, nv_ref, x_ref, wg_ref, wu_ref, wd_ref, o_ref):
    @pl.when(pl.program_id(0) < nv_ref[0])
    def _():
        x = x_ref[...]
        hb = _silu(_dot(x, wg_ref[0])) * _dot(x, wu_ref[0])
        o_ref[...] = _dot(hb.astype(BF16), wd_ref[0])

    @pl.when(pl.program_id(0) >= nv_ref[0])
    def _():
        o_ref[...] = jnp.zeros(o_ref.shape, F32)


def _experts(xs, block_e, n_valid, w_gate, w_up, w_down):
    n_slots = xs.shape[0]
    n_blocks = n_slots // MOE_BLK
    wmap = lambda i, be, nv: (be[i], 0, 0)
    return pl.pallas_call(
        _expert_kernel,
        grid_spec=pltpu.PrefetchScalarGridSpec(
            num_scalar_prefetch=2,
            grid=(n_blocks,),
            in_specs=[pl.BlockSpec((MOE_BLK, D_MODEL), lambda i, be, nv: (i, 0)),
                      pl.BlockSpec((1, D_MODEL, D_EXPERT), wmap),
                      pl.BlockSpec((1, D_MODEL, D_EXPERT), wmap),
                      pl.BlockSpec((1, D_EXPERT, D_MODEL), wmap)],
            out_specs=pl.BlockSpec((MOE_BLK, D_MODEL), lambda i, be, nv: (i, 0))),
        out_shape=jax.ShapeDtypeStruct((n_slots, D_MODEL), F32),
        compiler_params=_cparams(("arbitrary",)),
        name="experts",
    )(block_e, n_valid, xs, w_gate, w_up, w_down)


def _combine_kernel(h_ref, y0_ref, y1_ref, w_ref, m5_ref, g_ref, o_ref, *, final):
    w = w_ref[...]
    y = y0_ref[...] * w[:, 0:1] + y1_ref[...] * w[:, 1:2]
    h = h_ref[...] + m5_ref[0] * y
    if final:
        h = _rms(h, g_ref[...], NORM_EPS)
    o_ref[...] = h


def _combine(h, y0, y1, wts, mod5, g_final, *, nb, nt, ctx_first, final):
    tok = lambda b, r: (b * nt + r, 0)
    if ctx_first:
        modmap = lambda b, r: (jnp.where(r == 0, nb, b), 0, 0)
    else:
        modmap = lambda b, r: (b, 0, 0)
    return pl.pallas_call(
        functools.partial(_combine_kernel, final=final),
        grid=(nb, nt),
        in_specs=[pl.BlockSpec((TM, D_MODEL), tok)] * 3
                 + [pl.BlockSpec((TM, 128), tok), pl.BlockSpec((1, 1, D_MODEL), modmap),
                    pl.BlockSpec((1, D_MODEL), lambda b, r: (0, 0))],
        out_specs=pl.BlockSpec((TM, D_MODEL), tok),
        out_shape=jax.ShapeDtypeStruct(h.shape, F32),
        compiler_params=_cparams(("parallel", "parallel")),
        name="combine",
    )(h, y0, y1, wts, mod5, g_final)


def _route(logits, bias_unused=None):
    n = logits.shape[0]
    pg = jax.nn.softmax(logits[:, :MOE_GROUPS], axis=-1)
    pg_top, g_sel = lax.top_k(pg, 1)
    le = logits[:, MOE_GROUPS:MOE_GROUPS + MOE_EXPERTS].reshape(n, MOE_GROUPS, MOE_PER_GROUP)
    le_sel = jnp.take_along_axis(le, g_sel[:, :, None], axis=1)[:, 0]
    pe_top, e_sel = lax.top_k(jax.nn.softmax(le_sel, axis=-1), MOE_TOPK)
    wts = pg_top * pe_top / jnp.sum(pe_top, axis=-1, keepdims=True)
    return g_sel * MOE_PER_GROUP + e_sel, wts


def _moe(h, g2, shift, scale, mod5, wh, wm, rbias, w_gate, w_up, w_down, g_final, *, nb, nt, ctx_first, final):
    n = h.shape[0]
    x_bf, logits = _router(h, g2, shift, scale, wh, wm, rbias, nb=nb, nt=nt, ctx_first=ctx_first)
    idx, wts = _route(logits)
    n_as = n * MOE_TOPK
    flat_e = idx.reshape(n_as)
    onehot = (flat_e[:, None] == jnp.arange(MOE_EXPERTS, dtype=jnp.int32)[None, :]).astype(jnp.int32)
    csum = jnp.cumsum(onehot, axis=0)
    counts = csum[-1]
    rank = jnp.sum(jnp.where(onehot > 0, csum - 1, 0), axis=1)
    padded = (counts + MOE_BLK - 1) // MOE_BLK * MOE_BLK
    pad_end = jnp.cumsum(padded)
    pad_start = pad_end - padded
    slot = pad_start[flat_e] + rank
    n_blocks = (n_as + MOE_EXPERTS * (MOE_BLK - 1) + MOE_BLK - 1) // MOE_BLK
    n_slots = n_blocks * MOE_BLK
    slot_tok = jnp.zeros((n_slots,), jnp.int32).at[slot].set(jnp.arange(n_as, dtype=jnp.int32) // MOE_TOPK)
    block_e = jnp.minimum(jnp.searchsorted(pad_end, jnp.arange(n_blocks, dtype=jnp.int32) * MOE_BLK,
                                           side='right'), MOE_EXPERTS - 1).astype(jnp.int32)
    n_valid = (pad_end[-1:] // MOE_BLK).astype(jnp.int32)
    xs = jnp.take(x_bf, slot_tok, axis=0)
    ys = _experts(xs, block_e, n_valid, w_gate, w_up, w_down)
    slot2 = slot.reshape(n, MOE_TOPK)
    y0 = jnp.take(ys, slot2[:, 0], axis=0)
    y1 = jnp.take(ys, slot2[:, 1], axis=0)
    wts_pad = jnp.pad(wts.astype(F32), ((0, 0), (0, 128 - MOE_TOPK)))
    return _combine(h, y0, y1, wts_pad, mod5, g_final, nb=nb, nt=nt, ctx_first=ctx_first, final=final)


_ROT_SRC = np.array(list(range(8, 16)) + list(range(0, 8)) + list(range(24, 32)) + list(range(16, 24)))
_ROT_SIGN = np.array([-1.0] * 8 + [1.0] * 8 + [-1.0] * 8 + [1.0] * 8, np.float32)


def _rot_cols(w):
    k = w.shape[-1] // ROPE_DIM
    src = np.concatenate([_ROT_SRC + ROPE_DIM * i for i in range(k)])
    sign = np.tile(_ROT_SIGN, k)
    return w[..., src] * sign


def _rope_tables(n_ctx, n_lat):
    rows = n_lat // GRID_W
    row = jnp.repeat(jnp.arange(rows, dtype=F32), GRID_W)
    col = (jnp.arange(rows * GRID_W) % GRID_W).astype(F32)
    nf = ROPE_DIM // 4
    inv = ROPE_BASE ** (-jnp.arange(nf, dtype=F32) / nf)
    ar = row[:, None] * inv
    ac = col[:, None] * inv
    ang = jnp.concatenate([ar, ar, ac, ac], axis=-1)
    cos = jnp.concatenate([jnp.ones((n_ctx, ROPE_DIM), F32), jnp.cos(ang)], axis=0)
    sin = jnp.concatenate([jnp.zeros((n_ctx, ROPE_DIM), F32), jnp.sin(ang)], axis=0)
    return cos, sin


def _block_diag(blocks):
    g, a, b = blocks.shape
    eye = jnp.eye(g, dtype=blocks.dtype)
    return (eye[:, None, :, None] * blocks[:, :, None, :]).reshape(g * a, g * b)


def _pick_tk(t_all):
    best = 128
    for tk in range(128, 1537, 128):
        if t_all % tk == 0:
            best = tk
    return best


def kernel(x, c, ctx, c_ctx, w_mod, b_mod, norm1_g, norm2_g, w_in, mla_q_norm_g, mla_kv_norm_g, mla_w_uq, mla_w_ukv, rw_mu, rw_w0, rw_w2, rw_a0, rw_a2, rw_g2, rw_k_k, rw_k_a, rw_r_k, rw_lnx_g, rw_lnx_b, s5_a_re, s5_a_im, s5_log_dt, s5_b_re, s5_b_im, s5_c_re, s5_c_im, s5_d, s5_glu_w, s5_glu_b, diff_lq1, diff_lk1, diff_lq2, diff_lk2, diff_subln_g, w_branch, w_out, router_g_w, router_g_b, router_e_w, router_e_b, exp_w_gate, exp_w_up, exp_w_down, final_norm_g):
    nb, n_lat, d = x.shape
    n_ctx = ctx.shape[1]
    depth = w_mod.shape[0]
    t_all = n_ctx + n_lat
    assert d == D_MODEL and n_ctx == TM and n_lat % TM == 0
    tpb = t_all // TM
    tk = _pick_tk(t_all)
    hw = RW_HEADS * RW_HEAD

    cos, sin = _rope_tables(n_ctx, n_lat)
    mla_scale = (MLA_NOPE + MLA_ROPE) ** -0.5
    z32 = jnp.zeros((t_all, 32), F32)
    cq_tab = jnp.concatenate([jnp.ones((t_all, 64), F32), cos, z32], axis=1) * mla_scale
    sq_tab = jnp.concatenate([jnp.zeros((t_all, 64), F32), sin, z32], axis=1) * mla_scale
    ck_tab = jnp.concatenate([cos, sin, jnp.zeros((t_all, 64), F32)], axis=1)
    dcos = jnp.tile(cos, (1, 8))
    dsin = jnp.tile(sin, (1, 8))

    c_rows = jnp.concatenate([c, c_ctx[None, :], jnp.zeros((8 - nb - 1, d), F32)], axis=0)

    h = jnp.concatenate([ctx, x], axis=1).reshape(nb * t_all, d)

    for l in range(depth):
        last = l == depth - 1
        r0, nt = (1, tpb - 1) if last else (0, tpb)

        mod = _mm(c_rows, w_mod[l].astype(BF16), b_mod[l][None, :], tm=8, tn=1536, pre_silu=True, name="mod")
        mods = [mod[:nb + 1, i * d:(i + 1) * d].reshape(nb + 1, 1, d) for i in range(6)]

        wi = w_in[l]
        o_rw, o_s5, o_df, o_gt = 416, 1440, 1696, 2464
        w_kr = wi[:, 384:416]
        w_dq, w_dk, w_dv = wi[:, o_df:o_df + 256], wi[:, o_df + 256:o_df + 512], wi[:, o_df + 512:o_df + 768]
        w_ext = jnp.concatenate(
            [wi[:, o_rw:o_s5],
             wi[:, :416], _rot_cols(w_kr), jnp.zeros((d, 64), F32),
             wi[:, o_s5:o_df],
             w_dq, w_dk, w_dv, _rot_cols(w_dq), _rot_cols(w_dk),
             wi[:, o_gt:]], axis=1).astype(BF16)
        p = _inproj(h, norm1_g[l][None, :], mods[0], mods[1], w_ext, nb=nb, tpb=tpb)

        wq = mla_w_uq[l].reshape(MLA_Q_LORA, MLA_HEADS, MLA_NOPE + MLA_ROPE)
        zq = jnp.zeros((MLA_Q_LORA, MLA_HEADS, 32), F32)
        wa = jnp.concatenate([wq, zq], axis=2).reshape(MLA_Q_LORA, 512).astype(BF16)
        wb = jnp.concatenate([jnp.zeros((MLA_Q_LORA, MLA_HEADS, 64), F32), _rot_cols(wq[:, :, MLA_NOPE:]), zq],
                             axis=2).reshape(MLA_Q_LORA, 512).astype(BF16)
        wkv = mla_w_ukv[l].reshape(MLA_KV_LORA, MLA_HEADS, MLA_NOPE + MLA_V)
        wk = jnp.concatenate([wkv[:, :, :MLA_NOPE], jnp.zeros((MLA_KV_LORA, MLA_HEADS, 64), F32)],
                             axis=2).reshape(MLA_KV_LORA, 512).astype(BF16)
        wv = wkv[:, :, MLA_NOPE:].reshape(MLA_KV_LORA, MLA_HEADS * MLA_V).astype(BF16)
        pk_np = np.zeros((128, 512), np.float32)
        for hh in range(MLA_HEADS):
            for i in range(32):
                pk_np[i, hh * 128 + 64 + i] = 1.0
                pk_np[32 + i, hh * 128 + 64 + i] = 1.0
        pk = jnp.asarray(pk_np, BF16)
        q_m, k_m, v_m = _mla_prep(p, mla_q_norm_g[l][None, :], mla_kv_norm_g[l][None, :], wa, wb, wk, wv, pk,
                                  cq_tab, sq_tab, ck_tab, nb=nb, tpb=tpb)
        ya_lat = _flash_mla(q_m, k_m, v_m, nb=nb, t_all=t_all, q_start=n_ctx, q_len=n_lat, k_len=t_all, tk=tk)

        q_d, k_d, v_d = _diff_prep(p, dcos, dsin, nb=nb, tpb=tpb)
        lam_init = 0.8 - 0.6 * math.exp(-0.3 * l)
        lam = (jnp.exp(jnp.sum(diff_lq1[l] * diff_lk1[l])) - jnp.exp(jnp.sum(diff_lq2[l] * diff_lk2[l])) + lam_init)
        lam_row = jnp.full((1, 256), lam, F32)
        g_row = jnp.tile(diff_subln_g[l], DIFF_HEADS)[None, :]
        yd_lat = _flash_diff(q_d, k_d, v_d, lam_row, g_row, lam_init=lam_init, nb=nb, t_all=t_all,
                             q_start=n_ctx, q_len=n_lat, k_len=t_all, tk=tk)
        if last:
            ya, yd = ya_lat, yd_lat
        else:
            ya_ctx = _flash_mla(q_m, k_m, v_m, nb=nb, t_all=t_all, q_start=0, q_len=n_ctx, k_len=n_ctx, tk=n_ctx)
            yd_ctx = _flash_diff(q_d, k_d, v_d, lam_row, g_row, lam_init=lam_init, nb=nb, t_all=t_all,
                                 q_start=0, q_len=n_ctx, k_len=n_ctx, tk=n_ctx)
            comb = lambda a_c, a_l: jnp.concatenate(
                [a_c.reshape(nb, n_ctx, -1), a_l.reshape(nb, n_lat, -1)], axis=1).reshape(nb * t_all, -1)
            ya, yd = comb(ya_ctx, ya_lat), comb(yd_ctx, yd_lat)

        wlo = jnp.zeros((128, 4 * hw), F32)
        wlo = wlo.at[:64, 0:hw].set(rw_w2[l, 0]).at[:64, hw:2 * hw].set(rw_w2[l, 1])
        wlo = wlo.at[64:, 2 * hw:3 * hw].set(rw_a2[l, 0]).at[64:, 3 * hw:].set(rw_a2[l, 1])
        vecs = jnp.stack([rw_k_k[l], rw_k_a[l], rw_r_k[l].reshape(hw), rw_w0[l, 0], rw_w0[l, 1],
                          rw_a0[l, 0], rw_a0[l, 1], jnp.zeros((hw,), F32)], axis=0)
        (r_, v_, kk_, w0_, k0_, b0_, w1_, k1_, b1_, bonus, gate_rw) = _rw_prep(
            p, rw_mu[l][None, :], wlo.astype(BF16), rw_g2[l].astype(BF16), vecs, nb=nb, tpb=tpb, r0=0, nt=tpb)
        yf, yb_ = _rw_scan(r_, v_, kk_, w0_, k0_, b0_, w1_, k1_, b1_, nb=nb, t_all=t_all, n_ctx=n_ctx)
        ln_vecs = jnp.concatenate([rw_lnx_g[l][None, :], rw_lnx_b[l][None, :], jnp.zeros((6, hw), F32)], axis=0)
        if last:
            trim = lambda a: a.reshape(nb, t_all, hw)[:, n_ctx:].reshape(nb * n_lat, hw)
            bonus, gate_rw = trim(bonus), trim(gate_rw)
        y_rw = _rw_post(yf, yb_, bonus, gate_rw, ln_vecs, nb=nb, tpb_y=tpb, r0_y=r0, nt=nt)

        bbs, abs_, cfs = [], [], []
        for dd in range(2):
            lr, li = s5_a_re[l, dd], s5_a_im[l, dd]
            dt = jnp.exp(s5_log_dt[l, dd])[:, None]
            mag = jnp.exp(lr * dt)
            ab_re, ab_im = mag * jnp.cos(li * dt), mag * jnp.sin(li * dt)
            den = lr * lr + li * li
            nr, ni = ab_re - 1.0, ab_im
            cf_re = (nr * lr + ni * li) / den
            cf_im = (ni * lr - nr * li) / den
            bre, bim = s5_b_re[l, dd], s5_b_im[l, dd]
            bb_re = cf_re[..., None] * bre - cf_im[..., None] * bim
            bb_im = cf_re[..., None] * bim + cf_im[..., None] * bre
            bbs.append(jnp.concatenate([_block_diag(bb_re.transpose(0, 2, 1)),
                                        _block_diag(bb_im.transpose(0, 2, 1))], axis=1))
            abs_.append(jnp.concatenate([ab_re.reshape(-1), ab_im.reshape(-1)])[None, :])
            cfs.append(jnp.concatenate([_block_diag(s5_c_re[l, dd].transpose(0, 2, 1)),
                                        -_block_diag(s5_c_im[l, dd].transpose(0, 2, 1))], axis=0))
        bu = _s5_bu(p, jnp.concatenate(bbs, axis=1).astype(BF16), nb=nb, tpb=tpb)
        xf, xb = _s5_scan(bu, jnp.stack(abs_, axis=0), nb=nb, t_all=t_all, n_ctx=n_ctx)
        y_s5 = _s5_post(xf, xb, p, cfs[0].astype(BF16), cfs[1].astype(BF16), s5_d[l].reshape(1, 256),
                        s5_glu_w[l].astype(BF16), s5_glu_b[l][None, :], nb=nb, tpb=tpb, r0=r0, nt=nt)

        h = _merge(ya, y_rw, y_s5, yd, p, h, mods[2], w_branch[l].astype(BF16), w_out[l].astype(BF16),
                   nb=nb, tpb=tpb, r0=r0, nt=nt)

        wr = jnp.concatenate([router_g_w[l], router_e_w[l], jnp.zeros((d, 128 - MOE_GROUPS - MOE_EXPERTS), F32)], axis=1)
        wr_h = wr.astype(BF16)
        wr_m = (wr - wr_h.astype(F32)).astype(BF16)
        rbias = jnp.concatenate([router_g_b[l], router_e_b[l],
                                 jnp.zeros((128 - MOE_GROUPS - MOE_EXPERTS,), F32)])[None, :]
        h = _moe(h, norm2_g[l][None, :], mods[3], mods[4], mods[5], wr_h, wr_m, rbias,
                 exp_w_gate[l].astype(BF16), exp_w_up[l].astype(BF16), exp_w_down[l].astype(BF16),
                 final_norm_g[None, :], nb=nb, nt=nt, ctx_first=not last, final=last)

    return h.reshape(nb, n_lat, d)
```

```python
import functools
import math

import jax
import jax.numpy as jnp
import numpy as np
from jax import lax
from jax.experimental import pallas as pl
from jax.experimental.pallas import tpu as pltpu

F32 = jnp.float32
BF16 = jnp.bfloat16

TM = 256
VMEM_LIMIT = 48 * 1024 * 1024

D_MODEL = 1024
GRID_W = 64
ROPE_DIM = 32
ROPE_BASE = 10000.0
NORM_EPS = 1e-6
MLA_HEADS, MLA_NOPE, MLA_ROPE, MLA_V = 4, 64, 32, 64
MLA_Q_LORA, MLA_KV_LORA = 256, 128
RW_HEADS, RW_HEAD = 4, 64
RW_LN_EPS = 64e-5
S5_GROUPS, S5_GROUP_CH, S5_STATE = 16, 16, 64
DIFF_HEADS, DIFF_HD = 4, 32
DIFF_EPS = 1e-5
MOE_GROUPS, MOE_PER_GROUP, MOE_TOPK = 4, 8, 2
MOE_EXPERTS = MOE_GROUPS * MOE_PER_GROUP
D_EXPERT = 512
MOE_BLK = 256
RW_CHUNK = 64
S5_CHUNK = 128

EXT_RW, EXT_MLA, EXT_S5, EXT_DIFF, EXT_GATE = 0, 1024, 1536, 1792, 3072
N_EXT = 7168


def _cparams(sem, vmem=VMEM_LIMIT):
    return pltpu.CompilerParams(dimension_semantics=sem, vmem_limit_bytes=vmem)


def _dot(a, b):
    return jnp.dot(a, b, preferred_element_type=F32)


def _dot_nt(a, b):
    return lax.dot_general(a, b, (((1,), (1,)), ((), ())), preferred_element_type=F32)


def _split_dot(x, e):
    hi = x.astype(BF16)
    mid = (x - hi.astype(F32)).astype(BF16)
    return _dot(hi, e) + _dot(mid, e)


def _block_ones(n, blk):
    r = lax.broadcasted_iota(jnp.int32, (n, n), 0) // blk
    c = lax.broadcasted_iota(jnp.int32, (n, n), 1) // blk
    return (r == c).astype(BF16)


def _sigmoid(x):
    return 1.0 / (1.0 + jnp.exp(-x))


def _silu(x):
    return x * _sigmoid(x)


def _softplus(x):
    return jnp.maximum(x, 0.0) + jnp.log(1.0 + jnp.exp(-jnp.abs(x)))


def _rms(x, g, eps):
    return x * lax.rsqrt(jnp.mean(x * x, axis=-1, keepdims=True) + eps) * g


def _mm_kernel(x_ref, w_ref, b_ref, o_ref, *, pre_silu):
    x = x_ref[...].astype(F32)
    if pre_silu:
        x = _silu(x)
    o_ref[...] = _dot(x.astype(BF16), w_ref[...]) + b_ref[...]


def _mm(x, w, b, *, tm, tn, pre_silu=False, name="mm"):
    m, k = x.shape
    n = w.shape[1]
    return pl.pallas_call(
        functools.partial(_mm_kernel, pre_silu=pre_silu),
        grid=(m // tm, n // tn),
        in_specs=[pl.BlockSpec((tm, k), lambda i, j: (i, 0)),
                  pl.BlockSpec((k, tn), lambda i, j: (0, j)),
                  pl.BlockSpec((1, tn), lambda i, j: (0, j))],
        out_specs=pl.BlockSpec((tm, tn), lambda i, j: (i, j)),
        out_shape=jax.ShapeDtypeStruct((m, n), F32),
        compiler_params=_cparams(("parallel", "arbitrary")),
        name=name,
    )(x, w, b)


def _inproj_kernel(h_ref, g_ref, sh_ref, sc_ref, w_ref, o_ref, xn_ref):
    @pl.when(pl.program_id(2) == 0)
    def _():
        x = _rms(h_ref[...], g_ref[...], NORM_EPS)
        xn_ref[...] = (x * (1.0 + sc_ref[0]) + sh_ref[0]).astype(BF16)
    o_ref[...] = _dot(xn_ref[...], w_ref[...])


def _inproj(h, g, shift, scale, w_ext, *, nb, tpb):
    n = h.shape[0]
    tn = 1024
    modmap = lambda b, r, j: (jnp.where(r == 0, nb, b), 0, 0)
    return pl.pallas_call(
        _inproj_kernel,
        grid=(nb, tpb, N_EXT // tn),
        in_specs=[pl.BlockSpec((TM, D_MODEL), lambda b, r, j: (b * tpb + r, 0)),
                  pl.BlockSpec((1, D_MODEL), lambda b, r, j: (0, 0)),
                  pl.BlockSpec((1, 1, D_MODEL), modmap),
                  pl.BlockSpec((1, 1, D_MODEL), modmap),
                  pl.BlockSpec((D_MODEL, tn), lambda b, r, j: (0, j))],
        out_specs=pl.BlockSpec((TM, tn), lambda b, r, j: (b * tpb + r, j)),
        out_shape=jax.ShapeDtypeStruct((n, N_EXT), F32),
        scratch_shapes=[pltpu.VMEM((TM, D_MODEL), BF16)],
        compiler_params=_cparams(("parallel", "parallel", "arbitrary")),
        name="inproj",
    )(h, g, shift, scale, w_ext)


def _mla_prep_kernel(p_ref, gq_ref, gkv_ref, wa_ref, wb_ref, wk_ref, wv_ref, pk_ref,
                     cq_ref, sq_ref, ck_ref, q_ref, k_ref, v_ref):
    seg = p_ref[...]
    nq = _rms(seg[:, :MLA_Q_LORA], gq_ref[...], NORM_EPS).astype(BF16)
    nkv = _rms(seg[:, MLA_Q_LORA:MLA_Q_LORA + MLA_KV_LORA], gkv_ref[...], NORM_EPS).astype(BF16)
    cq = jnp.concatenate([cq_ref[...]] * MLA_HEADS, axis=1)
    sq = jnp.concatenate([sq_ref[...]] * MLA_HEADS, axis=1)
    q = _dot(nq, wa_ref[...]) * cq + _dot(nq, wb_ref[...]) * sq
    q_ref[...] = q.astype(BF16)
    kr = (seg[:, 384:512] * ck_ref[...]).astype(BF16)
    k_ref[...] = (_dot(nkv, wk_ref[...]) + _dot(kr, pk_ref[...])).astype(BF16)
    v_ref[...] = _dot(nkv, wv_ref[...]).astype(BF16)


def _mla_prep(p, gq, gkv, wa, wb, wk, wv, pk, cq_tab, sq_tab, ck_tab, *, nb, tpb):
    n = p.shape[0]
    tok = lambda b, r: (b * tpb + r, 0)
    pos = lambda b, r: (r, 0)
    full = lambda b, r: (0, 0)
    return pl.pallas_call(
        _mla_prep_kernel,
        grid=(nb, tpb),
        in_specs=[pl.BlockSpec((TM, 512), lambda b, r: (b * tpb + r, EXT_MLA // 512)),
                  pl.BlockSpec((1, MLA_Q_LORA), full),
                  pl.BlockSpec((1, MLA_KV_LORA), full),
                  pl.BlockSpec((MLA_Q_LORA, 512), full),
                  pl.BlockSpec((MLA_Q_LORA, 512), full),
                  pl.BlockSpec((MLA_KV_LORA, 512), full),
                  pl.BlockSpec((MLA_KV_LORA, 256), full),
                  pl.BlockSpec((128, 512), full),
                  pl.BlockSpec((TM, 128), pos),
                  pl.BlockSpec((TM, 128), pos),
                  pl.BlockSpec((TM, 128), pos)],
        out_specs=[pl.BlockSpec((TM, 512), tok),
                   pl.BlockSpec((TM, 512), tok),
                   pl.BlockSpec((TM, 256), tok)],
        out_shape=[jax.ShapeDtypeStruct((n, 512), BF16),
                   jax.ShapeDtypeStruct((n, 512), BF16),
                   jax.ShapeDtypeStruct((n, 256), BF16)],
        compiler_params=_cparams(("parallel", "parallel")),
        name="mla_prep",
    )(p, gq, gkv, wa, wb, wk, wv, pk, cq_tab, sq_tab, ck_tab)


def _flash_mla_kernel(q_ref, k_ref, v_ref, o_ref, m_ref, l_ref, acc_ref):
    kk = pl.program_id(3)

    @pl.when(kk == 0)
    def _():
        m_ref[...] = jnp.full(m_ref.shape, -jnp.inf, F32)
        l_ref[...] = jnp.zeros(l_ref.shape, F32)
        acc_ref[...] = jnp.zeros(acc_ref.shape, F32)

    v = v_ref[...]
    lane = lax.broadcasted_iota(jnp.int32, acc_ref.shape, 1)
    first = lane < MLA_V
    alphas, pvs = [], []
    for h in range(2):
        s = _dot_nt(q_ref[:, h * 128:(h + 1) * 128], k_ref[:, h * 128:(h + 1) * 128])
        m_prev = m_ref[h]
        m_new = jnp.maximum(m_prev, jnp.max(s, axis=-1, keepdims=True))
        alpha = jnp.exp(m_prev - m_new)
        p = jnp.exp(s - m_new)
        l_ref[h] = alpha * l_ref[h] + jnp.sum(p, axis=-1, keepdims=True)
        m_ref[h] = m_new
        alphas.append(alpha)
        pvs.append(_dot(p.astype(BF16), v))
    acc_ref[...] = (jnp.where(first, alphas[0], alphas[1]) * acc_ref[...]
                    + jnp.where(first, pvs[0], pvs[1]))

    @pl.when(kk == pl.num_programs(3) - 1)
    def _():
        o_ref[...] = acc_ref[...] / jnp.where(first, l_ref[0], l_ref[1])


def _flash_mla(q, k, v, *, nb, t_all, q_start, q_len, k_len, tk):
    tq = TM
    tpb_q, tpb_k = t_all // tq, t_all // tk
    q0 = q_start // tq
    nq = q_len // tq
    return pl.pallas_call(
        _flash_mla_kernel,
        grid=(nb, 2, nq, k_len // tk),
        in_specs=[pl.BlockSpec((tq, 256), lambda b, hp, i, kk: (b * tpb_q + q0 + i, hp)),
                  pl.BlockSpec((tk, 256), lambda b, hp, i, kk: (b * tpb_k + kk, hp)),
                  pl.BlockSpec((tk, 128), lambda b, hp, i, kk: (b * tpb_k + kk, hp))],
        out_specs=pl.BlockSpec((tq, 128), lambda b, hp, i, kk: (b * nq + i, hp)),
        out_shape=jax.ShapeDtypeStruct((nb * q_len, 256), F32),
        scratch_shapes=[pltpu.VMEM((2, tq, 1), F32), pltpu.VMEM((2, tq, 1), F32),
                        pltpu.VMEM((tq, 128), F32)],
        compiler_params=_cparams(("parallel", "parallel", "parallel", "arbitrary")),
        name="flash_mla",
    )(q, k, v)


def _diff_prep(p, cos_tab, sin_tab, *, nb, tpb):
    n = p.shape[0]
    tok = lambda b, r: (b * tpb + r, 0)
    pos = lambda b, r: (r, 0)
    return pl.pallas_call(
        _diff_prep_kernel_cols,
        grid=(nb, tpb),
        in_specs=[pl.BlockSpec((TM, 256), lambda b, r, c=c: (b * tpb + r, EXT_DIFF // 256 + c))
                  for c in range(5)]
                 + [pl.BlockSpec((TM, 256), pos), pl.BlockSpec((TM, 256), pos)],
        out_specs=[pl.BlockSpec((TM, 256), tok)] * 3,
        out_shape=[jax.ShapeDtypeStruct((n, 256), BF16)] * 3,
        compiler_params=_cparams(("parallel", "parallel")),
        name="diff_prep",
    )(p, p, p, p, p, cos_tab, sin_tab)


def _diff_prep_kernel_cols(q_in, k_in, v_in, qr_in, kr_in, cos_ref, sin_ref, q_ref, k_ref, v_ref):
    cos, sin = cos_ref[...], sin_ref[...]
    scale = DIFF_HD ** -0.5
    q_ref[...] = ((q_in[...] * cos + qr_in[...] * sin) * scale).astype(BF16)
    k_ref[...] = (k_in[...] * cos + kr_in[...] * sin).astype(BF16)
    v_ref[...] = v_in[...].astype(BF16)


def _flash_diff_kernel(q_ref, k_ref, v_ref, lam_ref, g_ref, o_ref, m_ref, l_ref, acc_ref, *, lam_init):
    kk = pl.program_id(2)
    tq = q_ref.shape[0]

    @pl.when(kk == 0)
    def _():
        m_ref[...] = jnp.full(m_ref.shape, -jnp.inf, F32)
        l_ref[...] = jnp.zeros(l_ref.shape, F32)
        acc_ref[...] = jnp.zeros(acc_ref.shape, F32)

    q = q_ref[...]
    k = k_ref[...]
    v = v_ref[...]
    lane = lax.broadcasted_iota(jnp.int32, (tq, 256), 1)
    for n in range(2):
        acc = acc_ref[n]
        for h in range(DIFF_HEADS):
            idx = h * 2 + n
            qm = jnp.where((lane // DIFF_HD) == idx, q, jnp.zeros_like(q))
            s = _dot_nt(qm, k)
            m_prev = m_ref[idx]
            m_new = jnp.maximum(m_prev, jnp.max(s, axis=-1, keepdims=True))
            alpha = jnp.exp(m_prev - m_new)
            p = jnp.exp(s - m_new)
            l_ref[idx] = alpha * l_ref[idx] + jnp.sum(p, axis=-1, keepdims=True)
            m_ref[idx] = m_new
            pv = _dot(p.astype(BF16), v)
            acc = jnp.where((lane // (2 * DIFF_HD)) == h, alpha * acc + pv, acc)
        acc_ref[n] = acc

    @pl.when(kk == pl.num_programs(2) - 1)
    def _():
        outs = []
        for n in range(2):
            inv = jnp.zeros((tq, 256), F32)
            for h in range(DIFF_HEADS):
                inv = jnp.where((lane // (2 * DIFF_HD)) == h, 1.0 / l_ref[h * 2 + n], inv)
            outs.append(acc_ref[n] * inv)
        o = outs[0] - lam_ref[...] * outs[1]
        ms = _split_dot(o * o, _block_ones(256, 2 * DIFF_HD)) * (1.0 / (2 * DIFF_HD))
        o_ref[...] = o * lax.rsqrt(ms + DIFF_EPS) * g_ref[...] * (1.0 - lam_init)


def _flash_diff(q, k, v, lam_row, g_row, *, lam_init, nb, t_all, q_start, q_len, k_len, tk):
    tq = TM
    tpb_q, tpb_k = t_all // tq, t_all // tk
    q0 = q_start // tq
    nq = q_len // tq
    return pl.pallas_call(
        functools.partial(_flash_diff_kernel, lam_init=lam_init),
        grid=(nb, nq, k_len // tk),
        in_specs=[pl.BlockSpec((tq, 256), lambda b, i, kk: (b * tpb_q + q0 + i, 0)),
                  pl.BlockSpec((tk, 256), lambda b, i, kk: (b * tpb_k + kk, 0)),
                  pl.BlockSpec((tk, 256), lambda b, i, kk: (b * tpb_k + kk, 0)),
                  pl.BlockSpec((1, 256), lambda b, i, kk: (0, 0)),
                  pl.BlockSpec((1, 256), lambda b, i, kk: (0, 0))],
        out_specs=pl.BlockSpec((tq, 256), lambda b, i, kk: (b * nq + i, 0)),
        out_shape=jax.ShapeDtypeStruct((nb * q_len, 256), F32),
        scratch_shapes=[pltpu.VMEM((8, tq, 1), F32), pltpu.VMEM((8, tq, 1), F32),
                        pltpu.VMEM((2, tq, 256), F32)],
        compiler_params=_cparams(("parallel", "parallel", "arbitrary")),
        name="flash_diff",
    )(q, k, v, lam_row, g_row)


def _rw_prep_kernel(p_ref, prev_ref, next_ref, mu_ref, wlo_ref, g2_ref, vec_ref,
                    r_ref, v_ref, kk_ref, w0_ref, k0_ref, b0_ref, w1_ref, k1_ref, b1_ref,
                    bonus_ref, gate_ref, *, r0, lat_last):
    r = pl.program_id(1) + r0
    p = p_ref[...]
    row = lax.broadcasted_iota(jnp.int32, p.shape, 0)
    first_tile = jnp.logical_or(r == 0, r == 1)
    last_tile = jnp.logical_or(r == 0, r == lat_last)
    prev_row = jnp.where(first_tile, 0.0, prev_ref[7:8, :])
    next_row = jnp.where(last_tile, 0.0, next_ref[0:1, :])
    up = jnp.where(row == 0, prev_row, pltpu.roll(p, 1, 0))
    dn = jnp.where(row == TM - 1, next_row, pltpu.roll(p, TM - 1, 0))
    z = p + (0.5 * (up + dn) - p) * mu_ref[...]

    hw = RW_HEADS * RW_HEAD
    rr, k, v = z[:, :hw], z[:, hw:2 * hw], z[:, 2 * hw:3 * hw]
    lo = z[:, 3 * hw:3 * hw + 128]
    lane = lax.broadcasted_iota(jnp.int32, lo.shape, 1)
    lo = jnp.where(lane < 64, jnp.tanh(lo), lo).astype(BF16)
    wa = _dot(lo, wlo_ref[...])
    gate_ref[...] = _dot(_sigmoid(z[:, 3 * hw + 128:]).astype(BF16), g2_ref[...])

    e4 = _block_ones(hw, RW_HEAD)
    k_k, k_a, r_k = vec_ref[0:1, :], vec_ref[1:2, :], vec_ref[2:3, :]
    kk = k * k_k
    nrm = jnp.maximum(jnp.sqrt(_split_dot(kk * kk, e4)), 1e-12)
    kk = kk / nrm
    r_ref[...] = rr
    v_ref[...] = v
    kk_ref[...] = kk
    ksum = jnp.zeros_like(k)
    for d, (w_ref, kd_ref, b_ref) in enumerate(((w0_ref, k0_ref, b0_ref), (w1_ref, k1_ref, b1_ref))):
        w0 = vec_ref[3 + d:4 + d, :]
        a0 = vec_ref[5 + d:6 + d, :]
        wd = -_softplus(-(w0 + wa[:, d * hw:(d + 1) * hw])) - 0.5
        w_ref[...] = jnp.exp(-jnp.exp(wd))
        ad = _sigmoid(a0 + wa[:, (2 + d) * hw:(3 + d) * hw])
        kd = k * (1.0 + (ad - 1.0) * k_a)
        kd_ref[...] = kd
        b_ref[...] = kk * ad
        ksum = ksum + kd
    bonus_ref[...] = _split_dot(rr * ksum * r_k, e4) * v


def _rw_prep(p, mu, wlo, g2, vecs, *, nb, tpb, r0, nt):
    n_out = nb * nt * TM
    hw = RW_HEADS * RW_HEAD
    n_rows8 = p.shape[0] // 8
    tok = lambda b, r: (b * tpb + r0 + r, EXT_RW // 1024)
    prev = lambda b, r: (jnp.maximum((b * tpb + r0 + r) * (TM // 8) - 1, 0), EXT_RW // 1024)
    nxt = lambda b, r: (jnp.minimum((b * tpb + r0 + r + 1) * (TM // 8), n_rows8 - 1), EXT_RW // 1024)
    out = lambda b, r: (b * nt + r, 0)
    full = lambda b, r: (0, 0)
    return pl.pallas_call(
        functools.partial(_rw_prep_kernel, r0=r0, lat_last=tpb - 1),
        grid=(nb, nt),
        in_specs=[pl.BlockSpec((TM, 1024), tok),
                  pl.BlockSpec((8, 1024), prev),
                  pl.BlockSpec((8, 1024), nxt),
                  pl.BlockSpec((1, 1024), full),
                  pl.BlockSpec((128, 4 * hw), full),
                  pl.BlockSpec((128, hw), full),
                  pl.BlockSpec((8, hw), full)],
        out_specs=[pl.BlockSpec((TM, hw), out)] * 11,
        out_shape=[jax.ShapeDtypeStruct((n_out, hw), F32)] * 11,
        compiler_params=_cparams(("parallel", "parallel")),
        name="rw_prep",
    )(p, p, p, mu, wlo, g2, vecs)


def _rw_scan_kernel(rf, vf, kkf, wf, kf, bf, rb, vb, kkb, wb, kb, bb, yf_ref, yb_ref, s_ref, *, nb):
    c = RW_CHUNK

    @pl.when(pl.program_id(0) == 0)
    def _():
        s_ref[...] = jnp.zeros(s_ref.shape, F32)

    e22 = jnp.concatenate([_block_ones(128, RW_HEAD)] * 2, axis=0)
    lane = lax.broadcasted_iota(jnp.int32, (RW_HEAD, 128), 1)
    sub = lax.broadcasted_iota(jnp.int32, (RW_HEAD, 128), 0)
    diag = (lane % RW_HEAD) == sub
    sub8 = lax.broadcasted_iota(jnp.int32, (8, 128), 0)
    dirs = ((rf, vf, kkf, wf, kf, bf, yf_ref), (rb, vb, kkb, wb, kb, bb, yb_ref))

    def allreduce_rows(x):
        t = x[0:8]
        for i in range(1, 8):
            t = t + x[8 * i:8 * i + 8]
        for sh in (4, 2, 1):
            t = t + pltpu.roll(t, sh, 0)
        return t

    def group(g, carry):
        tiles, ytiles = {}, {}
        for d, refs in enumerate(dirs):
            base = pl.multiple_of((g if d == 0 else c // 8 - 1 - g) * 8, 8)
            for b in range(nb):
                for hp in range(2):
                    r_, v_, kk_, w_, k_, b_ = [ref[b, pl.ds(base, 8), pl.ds(hp * 128, 128)] for ref in refs[:6]]
                    his, mids = [], []
                    for x in (w_, b_, k_, kk_, r_):
                        hi = x.astype(BF16).astype(F32)
                        his.append(hi)
                        mids.append(x - hi)
                    tiles[d, b, hp] = (base, v_, his, mids)
                    ytiles[d, b, hp] = jnp.zeros((8, 128), F32)
        for jj in range(8):
            for d, refs in enumerate(dirs):
                j = jj if d == 0 else 7 - jj
                for b in range(nb):
                    for hp in range(2):
                        u = (d * nb + b) * 2 + hp
                        base, v_, his, mids = tiles[d, b, hp]
                        parts = [jnp.concatenate([jnp.where(diag, hi[j:j + 1, :], 0.0).astype(BF16),
                                                  jnp.where(diag, mid[j:j + 1, :], 0.0).astype(BF16)], axis=1)
                                 for hi, mid in zip(his, mids)]
                        cm = _dot(jnp.concatenate(parts, axis=0), e22)
                        wc, bc, kc, kkc, rc = [cm[RW_HEAD * i:RW_HEAD * (i + 1)] for i in range(5)]
                        s = s_ref[u]
                        sa = jnp.concatenate([allreduce_rows(kkc * s)] * 8, axis=0)
                        s = wc * s - bc * sa + kc * v_[j:j + 1, :]
                        s_ref[u] = s
                        ytiles[d, b, hp] = jnp.where(sub8 == j, allreduce_rows(rc * s), ytiles[d, b, hp])
        for d, refs in enumerate(dirs):
            for b in range(nb):
                for hp in range(2):
                    refs[6][b, pl.ds(tiles[d, b, hp][0], 8), pl.ds(hp * 128, 128)] = ytiles[d, b, hp]
        return carry

    lax.fori_loop(0, c // 8, group, 0)


def _rw_scan(r, v, kk, w0, k0, b0, w1, k1, b1, *, nb, t_all, n_ctx):
    c = RW_CHUNK
    nc, ncc = t_all // c, n_ctx // c
    hw = RW_HEADS * RW_HEAD
    shp = lambda a: a.reshape(nb, t_all, hw)
    fwd = lambda j: (0, j, 0)
    bwd = lambda j: (0, jnp.where(j < ncc, ncc - 1 - j, nc - 1 - (j - ncc)), 0)
    blk = (nb, c, hw)
    yshape = jax.ShapeDtypeStruct((nb, t_all, hw), F32)
    yf, yb = pl.pallas_call(
        functools.partial(_rw_scan_kernel, nb=nb),
        grid=(nc,),
        in_specs=[pl.BlockSpec(blk, fwd)] * 6 + [pl.BlockSpec(blk, bwd)] * 6,
        out_specs=[pl.BlockSpec(blk, fwd), pl.BlockSpec(blk, bwd)],
        out_shape=[yshape, yshape],
        scratch_shapes=[pltpu.VMEM((2 * nb * 2, RW_HEAD, 128), F32)],
        compiler_params=_cparams(("arbitrary",)),
        name="rw_scan",
    )(shp(r), shp(v), shp(kk), shp(w0), shp(k0), shp(b0),
      shp(r), shp(v), shp(kk), shp(w1), shp(k1), shp(b1))

    return yf.reshape(nb * t_all, hw), yb.reshape(nb * t_all, hw)


def _rw_post_kernel(yf_ref, yb_ref, bonus_ref, gate_ref, vec_ref, o_ref):
    e4 = _block_ones(RW_HEADS * RW_HEAD, RW_HEAD)
    y = yf_ref[...] + yb_ref[...]
    mean = _split_dot(y, e4) * (1.0 / RW_HEAD)
    yc = y - mean
    var = _split_dot(yc * yc, e4) * (1.0 / RW_HEAD)
    yn = yc * lax.rsqrt(var + RW_LN_EPS) * vec_ref[0:1, :] + vec_ref[1:2, :]
    o_ref[...] = (yn + bonus_ref[...]) * gate_ref[...]


def _rw_post(yf, yb, bonus, gate, vecs, *, nb, tpb_y, r0_y, nt):
    hw = RW_HEADS * RW_HEAD
    n_out = bonus.shape[0]
    ytok = lambda b, r: (b * tpb_y + r0_y + r, 0)
    tok = lambda b, r: (b * nt + r, 0)
    return pl.pallas_call(
        _rw_post_kernel,
        grid=(nb, nt),
        in_specs=[pl.BlockSpec((TM, hw), ytok), pl.BlockSpec((TM, hw), ytok),
                  pl.BlockSpec((TM, hw), tok), pl.BlockSpec((TM, hw), tok),
                  pl.BlockSpec((8, hw), lambda b, r: (0, 0))],
        out_specs=pl.BlockSpec((TM, hw), tok),
        out_shape=jax.ShapeDtypeStruct((n_out, hw), F32),
        compiler_params=_cparams(("parallel", "parallel")),
        name="rw_post",
    )(yf, yb, bonus, gate, vecs)


def _s5_bu_kernel(p_ref, w_ref, o_ref):
    o_ref[...] = _dot(p_ref[...].astype(BF16), w_ref[...])


def _s5_bu(p, w_b, *, nb, tpb):
    n = p.shape[0]
    nst = S5_GROUPS * S5_STATE
    return pl.pallas_call(
        _s5_bu_kernel,
        grid=(nb * tpb, 4),
        in_specs=[pl.BlockSpec((TM, 256), lambda i, j: (i, EXT_S5 // 256)),
                  pl.BlockSpec((256, nst), lambda i, j: (0, j))],
        out_specs=pl.BlockSpec((TM, nst), lambda i, j: (i, j)),
        out_shape=jax.ShapeDtypeStruct((n, 4 * nst), F32),
        compiler_params=_cparams(("parallel", "arbitrary")),
        name="s5_bu",
    )(p, w_b)


def _s5_scan_kernel(buf_ref, bub_ref, ab_ref, xf_ref, xb_ref, st_ref, *, nb):
    c = S5_CHUNK
    nst = S5_GROUPS * S5_STATE

    @pl.when(pl.program_id(0) == 0)
    def _():
        st_ref[...] = jnp.zeros(st_ref.shape, F32)

    dirs = ((buf_ref, xf_ref), (bub_ref, xb_ref))

    def group(g, carry):
        for d, (bu_ref, x_ref) in enumerate(dirs):
            base = pl.multiple_of((g if d == 0 else c // 8 - 1 - g) * 8, 8)
            ar = ab_ref[d, :, 0:nst]
            ai = ab_ref[d, :, nst:2 * nst]
            for b in range(nb):
                u = d * nb + b
                xr = st_ref[u, :, 0:nst]
                xi = st_ref[u, :, nst:2 * nst]
                bur = bu_ref[b, pl.ds(base, 8), 0:nst]
                bui = bu_ref[b, pl.ds(base, 8), nst:2 * nst]
                rows_r, rows_i = [None] * 8, [None] * 8
                for jj in range(8):
                    j = jj if d == 0 else 7 - jj
                    xr, xi = (ar * xr - ai * xi + bur[j:j + 1, :], ar * xi + ai * xr + bui[j:j + 1, :])
                    rows_r[j], rows_i[j] = xr, xi
                st_ref[u, :, 0:nst] = xr
                st_ref[u, :, nst:2 * nst] = xi
                x_ref[b, pl.ds(base, 8), 0:nst] = jnp.concatenate(rows_r, axis=0)
                x_ref[b, pl.ds(base, 8), nst:2 * nst] = jnp.concatenate(rows_i, axis=0)
        return carry

    lax.fori_loop(0, c // 8, group, 0)


def _s5_scan(bu, ab, *, nb, t_all, n_ctx):
    c = S5_CHUNK
    nc, ncc = t_all // c, n_ctx // c
    nst2 = 2 * S5_GROUPS * S5_STATE
    bu3 = bu.reshape(nb, t_all, 2 * nst2)
    fwd = lambda j: j
    bwd = lambda j: jnp.where(j < ncc, ncc - 1 - j, nc - 1 - (j - ncc))
    xshape = jax.ShapeDtypeStruct((nb, t_all, nst2), F32)
    xf, xb = pl.pallas_call(
        functools.partial(_s5_scan_kernel, nb=nb),
        grid=(nc,),
        in_specs=[pl.BlockSpec((nb, c, nst2), lambda j: (0, fwd(j), 0)),
                  pl.BlockSpec((nb, c, nst2), lambda j: (0, bwd(j), 1)),
                  pl.BlockSpec((2, 1, nst2), lambda j: (0, 0, 0))],
        out_specs=[pl.BlockSpec((nb, c, nst2), lambda j: (0, fwd(j), 0)),
                   pl.BlockSpec((nb, c, nst2), lambda j: (0, bwd(j), 0))],
        out_shape=[xshape, xshape],
        scratch_shapes=[pltpu.VMEM((2 * nb, 1, nst2), F32)],
        compiler_params=_cparams(("arbitrary",)),
        name="s5_scan",
    )(bu3, bu3, ab)
    return xf.reshape(nb * t_all, nst2), xb.reshape(nb * t_all, nst2)


def _s5_post_kernel(xf_ref, xb_ref, u_ref, cf_ref, cb_ref, d_ref, gw_ref, gb_ref, o_ref):
    y = (_dot(xf_ref[...].astype(BF16), cf_ref[...]) + _dot(xb_ref[...].astype(BF16), cb_ref[...])
         + d_ref[...] * u_ref[...])
    zg = 0.5 * y * (1.0 + jnp.tanh(math.sqrt(2.0 / math.pi) * (y + 0.044715 * (y * y * y))))
    o_ref[...] = zg * _sigmoid(_dot(zg.astype(BF16), gw_ref[...]) + gb_ref[...])


def _s5_post(xf, xb, p, cf, cb, d_row, glu_w, glu_b, *, nb, tpb, r0, nt):
    nst2 = 2 * S5_GROUPS * S5_STATE
    tok = lambda b, r: (b * tpb + r0 + r, 0)
    full = lambda b, r: (0, 0)
    return pl.pallas_call(
        _s5_post_kernel,
        grid=(nb, nt),
        in_specs=[pl.BlockSpec((TM, nst2), tok), pl.BlockSpec((TM, nst2), tok),
                  pl.BlockSpec((TM, 256), lambda b, r: (b * tpb + r0 + r, EXT_S5 // 256)),
                  pl.BlockSpec((nst2, 256), full), pl.BlockSpec((nst2, 256), full),
                  pl.BlockSpec((1, 256), full), pl.BlockSpec((256, 256), full),
                  pl.BlockSpec((1, 256), full)],
        out_specs=pl.BlockSpec((TM, 256), lambda b, r: (b * nt + r, 0)),
        out_shape=jax.ShapeDtypeStruct((nb * nt * TM, 256), F32),
        compiler_params=_cparams(("parallel", "parallel")),
        name="s5_post",
    )(xf, xb, p, cf, cb, d_row, glu_w, glu_b)


def _merge_kernel(ya_ref, yb_ref, ys_ref, yd_ref, g0, g1, g2, g3, h_ref, m2_ref, wb_ref, wo_ref, o_ref):
    acc = None
    for n, (y_ref, g_ref) in enumerate(((ya_ref, g0), (yb_ref, g1), (ys_ref, g2), (yd_ref, g3))):
        term = _sigmoid(g_ref[...]) * _dot(y_ref[...].astype(BF16), wb_ref[n])
        acc = term if acc is None else acc + term
    o_ref[...] = h_ref[...] + m2_ref[0] * _dot(acc.astype(BF16), wo_ref[...])


def _merge(ya, yb, ys, yd, p, h, mod2, w_branch, w_out, *, nb, tpb, r0, nt):
    tok_in = lambda b, r: (b * tpb + r0 + r, 0)
    tok_out = lambda b, r: (b * nt + r, 0)
    full2 = lambda b, r: (0, 0)
    gate = [pl.BlockSpec((TM, D_MODEL), lambda b, r, c=c: (b * tpb + r0 + r, EXT_GATE // D_MODEL + c))
            for c in range(4)]
    return pl.pallas_call(
        _merge_kernel,
        grid=(nb, nt),
        in_specs=[pl.BlockSpec((TM, 256), tok_out)] * 4 + gate
                 + [pl.BlockSpec((TM, D_MODEL), tok_in),
                    pl.BlockSpec((1, 1, D_MODEL), lambda b, r: (jnp.where(r0 + r == 0, nb, b), 0, 0)),
                    pl.BlockSpec((4, 256, D_MODEL), lambda b, r: (0, 0, 0)),
                    pl.BlockSpec((D_MODEL, D_MODEL), full2)],
        out_specs=pl.BlockSpec((TM, D_MODEL), tok_out),
        out_shape=jax.ShapeDtypeStruct((nb * nt * TM, D_MODEL), F32),
        compiler_params=_cparams(("parallel", "parallel")),
        name="merge",
    )(ya, yb, ys, yd, p, p, p, p, h, mod2, w_branch, w_out)


def _router_kernel(h_ref, g_ref, sh_ref, sc_ref, wh_ref, wm_ref, b_ref, x_ref, lg_ref):
    x = _rms(h_ref[...], g_ref[...], NORM_EPS) * (1.0 + sc_ref[0]) + sh_ref[0]
    xh = x.astype(BF16)
    xm = (x - xh.astype(F32)).astype(BF16)
    x_ref[...] = xh
    lg_ref[...] = (_dot(xh, wh_ref[...]) + _dot(xm, wh_ref[...]) + _dot(xh, wm_ref[...])) + b_ref[...]


def _router(h, g, shift, scale, wh, wm, bias, *, nb, nt, ctx_first):
    tok = lambda b, r: (b * nt + r, 0)
    full = lambda b, r: (0, 0)
    if ctx_first:
        modmap = lambda b, r: (jnp.where(r == 0, nb, b), 0, 0)
    else:
        modmap = lambda b, r: (b, 0, 0)
    n = h.shape[0]
    return pl.pallas_call(
        _router_kernel,
        grid=(nb, nt),
        in_specs=[pl.BlockSpec((TM, D_MODEL), tok), pl.BlockSpec((1, D_MODEL), full),
                  pl.BlockSpec((1, 1, D_MODEL), modmap), pl.BlockSpec((1, 1, D_MODEL), modmap),
                  pl.BlockSpec((D_MODEL, 128), full), pl.BlockSpec((D_MODEL, 128), full),
                  pl.BlockSpec((1, 128), full)],
        out_specs=[pl.BlockSpec((TM, D_MODEL), tok), pl.BlockSpec((TM, 128), tok)],
        out_shape=[jax.ShapeDtypeStruct((n, D_MODEL), BF16), jax.ShapeDtypeStruct((n, 128), F32)],
        compiler_params=_cparams(("parallel", "parallel")),
        name="router",
    )(h, g, shift, scale, wh, wm, bias)


def _expert_kernel(be_ref, nv_ref, x_ref, wg_ref, wu_ref, wd_ref, o_ref):
    @pl.when(pl.program_id(0) < nv_ref[0])
    def _():
        x = x_ref[...]
        hb = _silu(_dot(x, wg_ref[0])) * _dot(x, wu_ref[0])
        o_ref[...] = _dot(hb.astype(BF16), wd_ref[0])

    @pl.when(pl.program_id(0) >= nv_ref[0])
    def _():
        o_ref[...] = jnp.zeros(o_ref.shape, F32)


def _experts(xs, block_e, n_valid, w_gate, w_up, w_down):
    n_slots = xs.shape[0]
    n_blocks = n_slots // MOE_BLK
    wmap = lambda i, be, nv: (be[i], 0, 0)
    return pl.pallas_call(
        _expert_kernel,
        grid_spec=pltpu.PrefetchScalarGridSpec(
            num_scalar_prefetch=2,
            grid=(n_blocks,),
            in_specs=[pl.BlockSpec((MOE_BLK, D_MODEL), lambda i, be, nv: (i, 0)),
                      pl.BlockSpec((1, D_MODEL, D_EXPERT), wmap),
                      pl.BlockSpec((1, D_MODEL, D_EXPERT), wmap),
                      pl.BlockSpec((1, D_EXPERT, D_MODEL), wmap)],
            out_specs=pl.BlockSpec((MOE_BLK, D_MODEL), lambda i, be, nv: (i, 0))),
        out_shape=jax.ShapeDtypeStruct((n_slots, D_MODEL), F32),
        compiler_params=_cparams(("arbitrary",)),
        name="experts",
    )(block_e, n_valid, xs, w_gate, w_up, w_down)


def _combine_kernel(h_ref, y0_ref, y1_ref, w_ref, m5_ref, g_ref, o_ref, *, final):
    w = w_ref[...]
    y = y0_ref[...] * w[:, 0:1] + y1_ref[...] * w[:, 1:2]
    h = h_ref[...] + m5_ref[0] * y
    if final:
        h = _rms(h, g_ref[...], NORM_EPS)
    o_ref[...] = h


def _combine(h, y0, y1, wts, mod5, g_final, *, nb, nt, ctx_first, final):
    tok = lambda b, r: (b * nt + r, 0)
    if ctx_first:
        modmap = lambda b, r: (jnp.where(r == 0, nb, b), 0, 0)
    else:
        modmap = lambda b, r: (b, 0, 0)
    return pl.pallas_call(
        functools.partial(_combine_kernel, final=final),
        grid=(nb, nt),
        in_specs=[pl.BlockSpec((TM, D_MODEL), tok)] * 3
                 + [pl.BlockSpec((TM, 128), tok), pl.BlockSpec((1, 1, D_MODEL), modmap),
                    pl.BlockSpec((1, D_MODEL), lambda b, r: (0, 0))],
        out_specs=pl.BlockSpec((TM, D_MODEL), tok),
        out_shape=jax.ShapeDtypeStruct(h.shape, F32),
        compiler_params=_cparams(("parallel", "parallel")),
        name="combine",
    )(h, y0, y1, wts, mod5, g_final)


def _route(logits, bias_unused=None):
    n = logits.shape[0]
    pg = jax.nn.softmax(logits[:, :MOE_GROUPS], axis=-1)
    pg_top, g_sel = lax.top_k(pg, 1)
    le = logits[:, MOE_GROUPS:MOE_GROUPS + MOE_EXPERTS].reshape(n, MOE_GROUPS, MOE_PER_GROUP)
    le_sel = jnp.take_along_axis(le, g_sel[:, :, None], axis=1)[:, 0]
    pe_top, e_sel = lax.top_k(jax.nn.softmax(le_sel, axis=-1), MOE_TOPK)
    wts = pg_top * pe_top / jnp.sum(pe_top, axis=-1, keepdims=True)
    return g_sel * MOE_PER_GROUP + e_sel, wts


def _moe(h, g2, shift, scale, mod5, wh, wm, rbias, w_gate, w_up, w_down, g_final, *, nb, nt, ctx_first, final):
    n = h.shape[0]
    x_bf, logits = _router(h, g2, shift, scale, wh, wm, rbias, nb=nb, nt=nt, ctx_first=ctx_first)
    idx, wts = _route(logits)
    n_as = n * MOE_TOPK
    flat_e = idx.reshape(n_as)
    onehot = (flat_e[:, None] == jnp.arange(MOE_EXPERTS, dtype=jnp.int32)[None, :]).astype(jnp.int32)
    csum = jnp.cumsum(onehot, axis=0)
    counts = csum[-1]
    rank = jnp.sum(jnp.where(onehot > 0, csum - 1, 0), axis=1)
    padded = (counts + MOE_BLK - 1) // MOE_BLK * MOE_BLK
    pad_end = jnp.cumsum(padded)
    pad_start = pad_end - padded
    slot = pad_start[flat_e] + rank
    n_blocks = (n_as + MOE_EXPERTS * (MOE_BLK - 1) + MOE_BLK - 1) // MOE_BLK
    n_slots = n_blocks * MOE_BLK
    slot_tok = jnp.zeros((n_slots,), jnp.int32).at[slot].set(jnp.arange(n_as, dtype=jnp.int32) // MOE_TOPK)
    block_e = jnp.minimum(jnp.searchsorted(pad_end, jnp.arange(n_blocks, dtype=jnp.int32) * MOE_BLK,
                                           side='right'), MOE_EXPERTS - 1).astype(jnp.int32)
    n_valid = (pad_end[-1:] // MOE_BLK).astype(jnp.int32)
    xs = jnp.take(x_bf, slot_tok, axis=0)
    ys = _experts(xs, block_e, n_valid, w_gate, w_up, w_down)
    slot2 = slot.reshape(n, MOE_TOPK)
    y0 = jnp.take(ys, slot2[:, 0], axis=0)
    y1 = jnp.take(ys, slot2[:, 1], axis=0)
    wts_pad = jnp.pad(wts.astype(F32), ((0, 0), (0, 128 - MOE_TOPK)))
    return _combine(h, y0, y1, wts_pad, mod5, g_final, nb=nb, nt=nt, ctx_first=ctx_first, final=final)


_ROT_SRC = np.array(list(range(8, 16)) + list(range(0, 8)) + list(range(24, 32)) + list(range(16, 24)))
_ROT_SIGN = np.array([-1.0] * 8 + [1.0] * 8 + [-1.0] * 8 + [1.0] * 8, np.float32)


def _rot_cols(w):
    k = w.shape[-1] // ROPE_DIM
    src = np.concatenate([_ROT_SRC + ROPE_DIM * i for i in range(k)])
    sign = np.tile(_ROT_SIGN, k)
    return w[..., src] * sign


def _rope_tables(n_ctx, n_lat):
    rows = n_lat // GRID_W
    row = jnp.repeat(jnp.arange(rows, dtype=F32), GRID_W)
    col = (jnp.arange(rows * GRID_W) % GRID_W).astype(F32)
    nf = ROPE_DIM // 4
    inv = ROPE_BASE ** (-jnp.arange(nf, dtype=F32) / nf)
    ar = row[:, None] * inv
    ac = col[:, None] * inv
    ang = jnp.concatenate([ar, ar, ac, ac], axis=-1)
    cos = jnp.concatenate([jnp.ones((n_ctx, ROPE_DIM), F32), jnp.cos(ang)], axis=0)
    sin = jnp.concatenate([jnp.zeros((n_ctx, ROPE_DIM), F32), jnp.sin(ang)], axis=0)
    return cos, sin


def _block_diag(blocks):
    g, a, b = blocks.shape
    eye = jnp.eye(g, dtype=blocks.dtype)
    return (eye[:, None, :, None] * blocks[:, :, None, :]).reshape(g * a, g * b)


def _pick_tk(t_all):
    best = 128
    for tk in range(128, 1537, 128):
        if t_all % tk == 0:
            best = tk
    return best


def kernel(x, c, ctx, c_ctx, w_mod, b_mod, norm1_g, norm2_g, w_in, mla_q_norm_g, mla_kv_norm_g, mla_w_uq, mla_w_ukv, rw_mu, rw_w0, rw_w2, rw_a0, rw_a2, rw_g2, rw_k_k, rw_k_a, rw_r_k, rw_lnx_g, rw_lnx_b, s5_a_re, s5_a_im, s5_log_dt, s5_b_re, s5_b_im, s5_c_re, s5_c_im, s5_d, s5_glu_w, s5_glu_b, diff_lq1, diff_lk1, diff_lq2, diff_lk2, diff_subln_g, w_branch, w_out, router_g_w, router_g_b, router_e_w, router_e_b, exp_w_gate, exp_w_up, exp_w_down, final_norm_g):
    nb, n_lat, d = x.shape
    n_ctx = ctx.shape[1]
    depth = w_mod.shape[0]
    t_all = n_ctx + n_lat
    assert d == D_MODEL and n_ctx == TM and n_lat % TM == 0
    tpb = t_all // TM
    tk = _pick_tk(t_all)
    hw = RW_HEADS * RW_HEAD

    cos, sin = _rope_tables(n_ctx, n_lat)
    mla_scale = (MLA_NOPE + MLA_ROPE) ** -0.5
    z32 = jnp.zeros((t_all, 32), F32)
    cq_tab = jnp.concatenate([jnp.ones((t_all, 64), F32), cos, z32], axis=1) * mla_scale
    sq_tab = jnp.concatenate([jnp.zeros((t_all, 64), F32), sin, z32], axis=1) * mla_scale
    ck_tab = jnp.concatenate([cos, sin, jnp.zeros((t_all, 64), F32)], axis=1)
    dcos = jnp.tile(cos, (1, 8))
    dsin = jnp.tile(sin, (1, 8))

    c_rows = jnp.concatenate([c, c_ctx[None, :], jnp.zeros((8 - nb - 1, d), F32)], axis=0)

    h = jnp.concatenate([ctx, x], axis=1).reshape(nb * t_all, d)

    for l in range(depth):
        last = l == depth - 1
        r0, nt = (1, tpb - 1) if last else (0, tpb)

        mod = _mm(c_rows, w_mod[l].astype(BF16), b_mod[l][None, :], tm=8, tn=1536, pre_silu=True, name="mod")
        mods = [mod[:nb + 1, i * d:(i + 1) * d].reshape(nb + 1, 1, d) for i in range(6)]

        wi = w_in[l]
        o_rw, o_s5, o_df, o_gt = 416, 1440, 1696, 2464
        w_kr = wi[:, 384:416]
        w_dq, w_dk, w_dv = wi[:, o_df:o_df + 256], wi[:, o_df + 256:o_df + 512], wi[:, o_df + 512:o_df + 768]
        w_ext = jnp.concatenate(
            [wi[:, o_rw:o_s5],
             wi[:, :416], _rot_cols(w_kr), jnp.zeros((d, 64), F32),
             wi[:, o_s5:o_df],
             w_dq, w_dk, w_dv, _rot_cols(w_dq), _rot_cols(w_dk),
             wi[:, o_gt:]], axis=1).astype(BF16)
        p = _inproj(h, norm1_g[l][None, :], mods[0], mods[1], w_ext, nb=nb, tpb=tpb)

        wq = mla_w_uq[l].reshape(MLA_Q_LORA, MLA_HEADS, MLA_NOPE + MLA_ROPE)
        zq = jnp.zeros((MLA_Q_LORA, MLA_HEADS, 32), F32)
        wa = jnp.concatenate([wq, zq], axis=2).reshape(MLA_Q_LORA, 512).astype(BF16)
        wb = jnp.concatenate([jnp.zeros((MLA_Q_LORA, MLA_HEADS, 64), F32), _rot_cols(wq[:, :, MLA_NOPE:]), zq],
                             axis=2).reshape(MLA_Q_LORA, 512).astype(BF16)
        wkv = mla_w_ukv[l].reshape(MLA_KV_LORA, MLA_HEADS, MLA_NOPE + MLA_V)
        wk = jnp.concatenate([wkv[:, :, :MLA_NOPE], jnp.zeros((MLA_KV_LORA, MLA_HEADS, 64), F32)],
                             axis=2).reshape(MLA_KV_LORA, 512).astype(BF16)
        wv = wkv[:, :, MLA_NOPE:].reshape(MLA_KV_LORA, MLA_HEADS * MLA_V).astype(BF16)
        pk_np = np.zeros((128, 512), np.float32)
        for hh in range(MLA_HEADS):
            for i in range(32):
                pk_np[i, hh * 128 + 64 + i] = 1.0
                pk_np[32 + i, hh * 128 + 64 + i] = 1.0
        pk = jnp.asarray(pk_np, BF16)
        q_m, k_m, v_m = _mla_prep(p, mla_q_norm_g[l][None, :], mla_kv_norm_g[l][None, :], wa, wb, wk, wv, pk,
                                  cq_tab, sq_tab, ck_tab, nb=nb, tpb=tpb)
        ya_lat = _flash_mla(q_m, k_m, v_m, nb=nb, t_all=t_all, q_start=n_ctx, q_len=n_lat, k_len=t_all, tk=tk)

        q_d, k_d, v_d = _diff_prep(p, dcos, dsin, nb=nb, tpb=tpb)
        lam_init = 0.8 - 0.6 * math.exp(-0.3 * l)
        lam = (jnp.exp(jnp.sum(diff_lq1[l] * diff_lk1[l])) - jnp.exp(jnp.sum(diff_lq2[l] * diff_lk2[l])) + lam_init)
        lam_row = jnp.full((1, 256), lam, F32)
        g_row = jnp.tile(diff_subln_g[l], DIFF_HEADS)[None, :]
        yd_lat = _flash_diff(q_d, k_d, v_d, lam_row, g_row, lam_init=lam_init, nb=nb, t_all=t_all,
                             q_start=n_ctx, q_len=n_lat, k_len=t_all, tk=tk)
        if last:
            ya, yd = ya_lat, yd_lat
        else:
            ya_ctx = _flash_mla(q_m, k_m, v_m, nb=nb, t_all=t_all, q_start=0, q_len=n_ctx, k_len=n_ctx, tk=n_ctx)
            yd_ctx = _flash_diff(q_d, k_d, v_d, lam_row, g_row, lam_init=lam_init, nb=nb, t_all=t_all,
                                 q_start=0, q_len=n_ctx, k_len=n_ctx, tk=n_ctx)
            comb = lambda a_c, a_l: jnp.concatenate(
                [a_c.reshape(nb, n_ctx, -1), a_l.reshape(nb, n_lat, -1)], axis=1).reshape(nb * t_all, -1)
            ya, yd = comb(ya_ctx, ya_lat), comb(yd_ctx, yd_lat)

        wlo = jnp.zeros((128, 4 * hw), F32)
        wlo = wlo.at[:64, 0:hw].set(rw_w2[l, 0]).at[:64, hw:2 * hw].set(rw_w2[l, 1])
        wlo = wlo.at[64:, 2 * hw:3 * hw].set(rw_a2[l, 0]).at[64:, 3 * hw:].set(rw_a2[l, 1])
        vecs = jnp.stack([rw_k_k[l], rw_k_a[l], rw_r_k[l].reshape(hw), rw_w0[l, 0], rw_w0[l, 1],
                          rw_a0[l, 0], rw_a0[l, 1], jnp.zeros((hw,), F32)], axis=0)
        (r_, v_, kk_, w0_, k0_, b0_, w1_, k1_, b1_, bonus, gate_rw) = _rw_prep(
            p, rw_mu[l][None, :], wlo.astype(BF16), rw_g2[l].astype(BF16), vecs, nb=nb, tpb=tpb, r0=0, nt=tpb)
        yf, yb_ = _rw_scan(r_, v_, kk_, w0_, k0_, b0_, w1_, k1_, b1_, nb=nb, t_all=t_all, n_ctx=n_ctx)
        ln_vecs = jnp.concatenate([rw_lnx_g[l][None, :], rw_lnx_b[l][None, :], jnp.zeros((6, hw), F32)], axis=0)
        if last:
            trim = lambda a: a.reshape(nb, t_all, hw)[:, n_ctx:].reshape(nb * n_lat, hw)
            bonus, gate_rw = trim(bonus), trim(gate_rw)
        y_rw = _rw_post(yf, yb_, bonus, gate_rw, ln_vecs, nb=nb, tpb_y=tpb, r0_y=r0, nt=nt)

        bbs, abs_, cfs = [], [], []
        for dd in range(2):
            lr, li = s5_a_re[l, dd], s5_a_im[l, dd]
            dt = jnp.exp(s5_log_dt[l, dd])[:, None]
            mag = jnp.exp(lr * dt)
            ab_re, ab_im = mag * jnp.cos(li * dt), mag * jnp.sin(li * dt)
            den = lr * lr + li * li
            nr, ni = ab_re - 1.0, ab_im
            cf_re = (nr * lr + ni * li) / den
            cf_im = (ni * lr - nr * li) / den
            bre, bim = s5_b_re[l, dd], s5_b_im[l, dd]
            bb_re = cf_re[..., None] * bre - cf_im[..., None] * bim
            bb_im = cf_re[..., None] * bim + cf_im[..., None] * bre
            bbs.append(jnp.concatenate([_block_diag(bb_re.transpose(0, 2, 1)),
                                        _block_diag(bb_im.transpose(0, 2, 1))], axis=1))
            abs_.append(jnp.concatenate([ab_re.reshape(-1), ab_im.reshape(-1)])[None, :])
            cfs.append(jnp.concatenate([_block_diag(s5_c_re[l, dd].transpose(0, 2, 1)),
                                        -_block_diag(s5_c_im[l, dd].transpose(0, 2, 1))], axis=0))
        bu = _s5_bu(p, jnp.concatenate(bbs, axis=1).astype(BF16), nb=nb, tpb=tpb)
        xf, xb = _s5_scan(bu, jnp.stack(abs_, axis=0), nb=nb, t_all=t_all, n_ctx=n_ctx)
        y_s5 = _s5_post(xf, xb, p, cfs[0].astype(BF16), cfs[1].astype(BF16), s5_d[l].reshape(1, 256),
                        s5_glu_w[l].astype(BF16), s5_glu_b[l][None, :], nb=nb, tpb=tpb, r0=r0, nt=nt)

        h = _merge(ya, y_rw, y_s5, yd, p, h, mods[2], w_branch[l].astype(BF16), w_out[l].astype(BF16),
                   nb=nb, tpb=tpb, r0=r0, nt=nt)

        wr = jnp.concatenate([router_g_w[l], router_e_w[l], jnp.zeros((d, 128 - MOE_GROUPS - MOE_EXPERTS), F32)], axis=1)
        wr_h = wr.astype(BF16)
        wr_m = (wr - wr_h.astype(F32)).astype(BF16)
        rbias = jnp.concatenate([router_g_b[l], router_e_b[l],
                                 jnp.zeros((128 - MOE_GROUPS - MOE_EXPERTS,), F32)])[None, :]
        h = _moe(h, norm2_g[l][None, :], mods[3], mods[4], mods[5], wr_h, wr_m, rbias,
                 exp_w_gate[l].astype(BF16), exp_w_up[l].astype(BF16), exp_w_down[l].astype(BF16),
                 final_norm_g[None, :], nb=nb, nt=nt, ctx_first=not last, final=last)

    return h.reshape(nb, n_lat, d)
```

```python
import functools
import math

import jax
import jax.numpy as jnp
import numpy as np
from jax import lax
from jax.experimental import pallas as pl
from jax.experimental.pallas import tpu as pltpu

F32 = jnp.float32
BF16 = jnp.bfloat16

TM = 256
VMEM_LIMIT = 48 * 1024 * 1024

D_MODEL = 1024
GRID_W = 64
ROPE_DIM = 32
ROPE_BASE = 10000.0
NORM_EPS = 1e-6
MLA_HEADS, MLA_NOPE, MLA_ROPE, MLA_V = 4, 64, 32, 64
MLA_Q_LORA, MLA_KV_LORA = 256, 128
RW_HEADS, RW_HEAD = 4, 64
RW_LN_EPS = 64e-5
S5_GROUPS, S5_GROUP_CH, S5_STATE = 16, 16, 64
DIFF_HEADS, DIFF_HD = 4, 32
DIFF_EPS = 1e-5
MOE_GROUPS, MOE_PER_GROUP, MOE_TOPK = 4, 8, 2
MOE_EXPERTS = MOE_GROUPS * MOE_PER_GROUP
D_EXPERT = 512
MOE_BLK = 256
RW_CHUNK = 64
S5_CHUNK = 128

EXT_RW, EXT_MLA, EXT_S5, EXT_DIFF, EXT_GATE = 0, 1024, 1536, 1792, 3072
N_EXT = 7168


def _cparams(sem, vmem=VMEM_LIMIT):
    return pltpu.CompilerParams(dimension_semantics=sem, vmem_limit_bytes=vmem)


def _dot(a, b):
    return jnp.dot(a, b, preferred_element_type=F32)


def _dot_nt(a, b):
    return lax.dot_general(a, b, (((1,), (1,)), ((), ())), preferred_element_type=F32)


def _split_dot(x, e):
    hi = x.astype(BF16)
    mid = (x - hi.astype(F32)).astype(BF16)
    return _dot(hi, e) + _dot(mid, e)


def _block_ones(n, blk):
    r = lax.broadcasted_iota(jnp.int32, (n, n), 0) // blk
    c = lax.broadcasted_iota(jnp.int32, (n, n), 1) // blk
    return (r == c).astype(BF16)


def _sigmoid(x):
    return 1.0 / (1.0 + jnp.exp(-x))


def _silu(x):
    return x * _sigmoid(x)


def _softplus(x):
    return jnp.maximum(x, 0.0) + jnp.log(1.0 + jnp.exp(-jnp.abs(x)))


def _rms(x, g, eps):
    return x * lax.rsqrt(jnp.mean(x * x, axis=-1, keepdims=True) + eps) * g


def _mm_kernel(x_ref, w_ref, b_ref, o_ref, *, pre_silu):
    x = x_ref[...].astype(F32)
    if pre_silu:
        x = _silu(x)
    o_ref[...] = _dot(x.astype(BF16), w_ref[...]) + b_ref[...]


def _mm(x, w, b, *, tm, tn, pre_silu=False, name="mm"):
    m, k = x.shape
    n = w.shape[1]
    return pl.pallas_call(
        functools.partial(_mm_kernel, pre_silu=pre_silu),
        grid=(m // tm, n // tn),
        in_specs=[pl.BlockSpec((tm, k), lambda i, j: (i, 0)),
                  pl.BlockSpec((k, tn), lambda i, j: (0, j)),
                  pl.BlockSpec((1, tn), lambda i, j: (0, j))],
        out_specs=pl.BlockSpec((tm, tn), lambda i, j: (i, j)),
        out_shape=jax.ShapeDtypeStruct((m, n), F32),
        compiler_params=_cparams(("parallel", "arbitrary")),
        name=name,
    )(x, w, b)


def _inproj_kernel(h_ref, g_ref, sh_ref, sc_ref, w_ref, o_ref):
    x = _rms(h_ref[...], g_ref[...], NORM_EPS)
    xn = (x * (1.0 + sc_ref[0]) + sh_ref[0]).astype(BF16)
    o_ref[...] = _dot(xn, w_ref[...])


def _inproj(h, g, shift, scale, w_ext, *, nb, tpb):
    n = h.shape[0]
    tn = 1024
    modmap = lambda j, b, r: (jnp.where(r == 0, nb, b), 0, 0)
    return pl.pallas_call(
        _inproj_kernel,
        grid=(N_EXT // tn, nb, tpb),
        in_specs=[pl.BlockSpec((TM, D_MODEL), lambda j, b, r: (b * tpb + r, 0)),
                  pl.BlockSpec((1, D_MODEL), lambda j, b, r: (0, 0)),
                  pl.BlockSpec((1, 1, D_MODEL), modmap),
                  pl.BlockSpec((1, 1, D_MODEL), modmap),
                  pl.BlockSpec((D_MODEL, tn), lambda j, b, r: (0, j))],
        out_specs=pl.BlockSpec((TM, tn), lambda j, b, r: (b * tpb + r, j)),
        out_shape=jax.ShapeDtypeStruct((n, N_EXT), F32),
        compiler_params=_cparams(("parallel", "parallel", "parallel")),
        name="inproj",
    )(h, g, shift, scale, w_ext)


def _mla_prep_kernel(p_ref, gq_ref, gkv_ref, wa_ref, wb_ref, wk_ref, wv_ref, pk_ref,
                     cq_ref, sq_ref, ck_ref, q_ref, k_ref, v_ref):
    seg = p_ref[...]
    nq = _rms(seg[:, :MLA_Q_LORA], gq_ref[...], NORM_EPS).astype(BF16)
    nkv = _rms(seg[:, MLA_Q_LORA:MLA_Q_LORA + MLA_KV_LORA], gkv_ref[...], NORM_EPS).astype(BF16)
    cq = jnp.concatenate([cq_ref[...]] * MLA_HEADS, axis=1)
    sq = jnp.concatenate([sq_ref[...]] * MLA_HEADS, axis=1)
    q = _dot(nq, wa_ref[...]) * cq + _dot(nq, wb_ref[...]) * sq
    q_ref[...] = q.astype(BF16)
    kr = (seg[:, 384:512] * ck_ref[...]).astype(BF16)
    k_ref[...] = (_dot(nkv, wk_ref[...]) + _dot(kr, pk_ref[...])).astype(BF16)
    v_ref[...] = _dot(nkv, wv_ref[...]).astype(BF16)


def _mla_prep(p, gq, gkv, wa, wb, wk, wv, pk, cq_tab, sq_tab, ck_tab, *, nb, tpb):
    n = p.shape[0]
    tok = lambda b, r: (b * tpb + r, 0)
    pos = lambda b, r: (r, 0)
    full = lambda b, r: (0, 0)
    return pl.pallas_call(
        _mla_prep_kernel,
        grid=(nb, tpb),
        in_specs=[pl.BlockSpec((TM, 512), lambda b, r: (b * tpb + r, EXT_MLA // 512)),
                  pl.BlockSpec((1, MLA_Q_LORA), full),
                  pl.BlockSpec((1, MLA_KV_LORA), full),
                  pl.BlockSpec((MLA_Q_LORA, 512), full),
                  pl.BlockSpec((MLA_Q_LORA, 512), full),
                  pl.BlockSpec((MLA_KV_LORA, 512), full),
                  pl.BlockSpec((MLA_KV_LORA, 256), full),
                  pl.BlockSpec((128, 512), full),
                  pl.BlockSpec((TM, 128), pos),
                  pl.BlockSpec((TM, 128), pos),
                  pl.BlockSpec((TM, 128), pos)],
        out_specs=[pl.BlockSpec((TM, 512), tok),
                   pl.BlockSpec((TM, 512), tok),
                   pl.BlockSpec((TM, 256), tok)],
        out_shape=[jax.ShapeDtypeStruct((n, 512), BF16),
                   jax.ShapeDtypeStruct((n, 512), BF16),
                   jax.ShapeDtypeStruct((n, 256), BF16)],
        compiler_params=_cparams(("parallel", "parallel")),
        name="mla_prep",
    )(p, gq, gkv, wa, wb, wk, wv, pk, cq_tab, sq_tab, ck_tab)


def _flash_mla_kernel(q_ref, k_ref, v_ref, o_ref, m_ref, l_ref, acc_ref):
    kk = pl.program_id(3)

    @pl.when(kk == 0)
    def _():
        m_ref[...] = jnp.full(m_ref.shape, -jnp.inf, F32)
        l_ref[...] = jnp.zeros(l_ref.shape, F32)
        acc_ref[...] = jnp.zeros(acc_ref.shape, F32)

    v = v_ref[...]
    lane = lax.broadcasted_iota(jnp.int32, acc_ref.shape, 1)
    first = lane < MLA_V
    alphas, pvs = [], []
    for h in range(2):
        s = _dot_nt(q_ref[:, h * 128:(h + 1) * 128], k_ref[:, h * 128:(h + 1) * 128])
        m_prev = m_ref[h]
        m_new = jnp.maximum(m_prev, jnp.max(s, axis=-1, keepdims=True))
        alpha = jnp.exp(m_prev - m_new)
        p = jnp.exp(s - m_new)
        l_ref[h] = alpha * l_ref[h] + jnp.sum(p, axis=-1, keepdims=True)
        m_ref[h] = m_new
        alphas.append(alpha)
        pvs.append(_dot(p.astype(BF16), v))
    acc_ref[...] = (jnp.where(first, alphas[0], alphas[1]) * acc_ref[...]
                    + jnp.where(first, pvs[0], pvs[1]))

    @pl.when(kk == pl.num_programs(3) - 1)
    def _():
        o_ref[...] = acc_ref[...] / jnp.where(first, l_ref[0], l_ref[1])


def _flash_mla(q, k, v, *, nb, t_all, q_start, q_len, k_len, tk):
    tq = TM
    tpb_q, tpb_k = t_all // tq, t_all // tk
    q0 = q_start // tq
    nq = q_len // tq
    return pl.pallas_call(
        _flash_mla_kernel,
        grid=(nb, 2, nq, k_len // tk),
        in_specs=[pl.BlockSpec((tq, 256), lambda b, hp, i, kk: (b * tpb_q + q0 + i, hp)),
                  pl.BlockSpec((tk, 256), lambda b, hp, i, kk: (b * tpb_k + kk, hp)),
                  pl.BlockSpec((tk, 128), lambda b, hp, i, kk: (b * tpb_k + kk, hp))],
        out_specs=pl.BlockSpec((tq, 128), lambda b, hp, i, kk: (b * nq + i, hp)),
        out_shape=jax.ShapeDtypeStruct((nb * q_len, 256), F32),
        scratch_shapes=[pltpu.VMEM((2, tq, 1), F32), pltpu.VMEM((2, tq, 1), F32),
                        pltpu.VMEM((tq, 128), F32)],
        compiler_params=_cparams(("parallel", "parallel", "parallel", "arbitrary")),
        name="flash_mla",
    )(q, k, v)


def _diff_prep(p, cos_tab, sin_tab, *, nb, tpb):
    n = p.shape[0]
    tok = lambda b, r: (b * tpb + r, 0)
    pos = lambda b, r: (r, 0)
    return pl.pallas_call(
        _diff_prep_kernel_cols,
        grid=(nb, tpb),
        in_specs=[pl.BlockSpec((TM, 256), lambda b, r, c=c: (b * tpb + r, EXT_DIFF // 256 + c))
                  for c in range(5)]
                 + [pl.BlockSpec((TM, 256), pos), pl.BlockSpec((TM, 256), pos)],
        out_specs=[pl.BlockSpec((TM, 256), tok)] * 3,
        out_shape=[jax.ShapeDtypeStruct((n, 256), BF16)] * 3,
        compiler_params=_cparams(("parallel", "parallel")),
        name="diff_prep",
    )(p, p, p, p, p, cos_tab, sin_tab)


def _diff_prep_kernel_cols(q_in, k_in, v_in, qr_in, kr_in, cos_ref, sin_ref, q_ref, k_ref, v_ref):
    cos, sin = cos_ref[...], sin_ref[...]
    scale = DIFF_HD ** -0.5
    q_ref[...] = ((q_in[...] * cos + qr_in[...] * sin) * scale).astype(BF16)
    k_ref[...] = (k_in[...] * cos + kr_in[...] * sin).astype(BF16)
    v_ref[...] = v_in[...].astype(BF16)


def _flash_diff_kernel(q_ref, k_ref, v_ref, lam_ref, g_ref, o_ref, m_ref, l_ref, acc_ref, *, lam_init):
    kk = pl.program_id(2)
    tq = q_ref.shape[0]

    @pl.when(kk == 0)
    def _():
        m_ref[...] = jnp.full(m_ref.shape, -jnp.inf, F32)
        l_ref[...] = jnp.zeros(l_ref.shape, F32)
        acc_ref[...] = jnp.zeros(acc_ref.shape, F32)

    q = q_ref[...]
    k = k_ref[...]
    v = v_ref[...]
    lane = lax.broadcasted_iota(jnp.int32, (tq, 256), 1)
    for n in range(2):
        acc = acc_ref[n]
        for h in range(DIFF_HEADS):
            idx = h * 2 + n
            qm = jnp.where((lane // DIFF_HD) == idx, q, jnp.zeros_like(q))
            s = _dot_nt(qm, k)
            m_prev = m_ref[idx]
            m_new = jnp.maximum(m_prev, jnp.max(s, axis=-1, keepdims=True))
            alpha = jnp.exp(m_prev - m_new)
            p = jnp.exp(s - m_new)
            l_ref[idx] = alpha * l_ref[idx] + jnp.sum(p, axis=-1, keepdims=True)
            m_ref[idx] = m_new
            pv = _dot(p.astype(BF16), v)
            acc = jnp.where((lane // (2 * DIFF_HD)) == h, alpha * acc + pv, acc)
        acc_ref[n] = acc

    @pl.when(kk == pl.num_programs(2) - 1)
    def _():
        outs = []
        for n in range(2):
            inv = jnp.zeros((tq, 256), F32)
            for h in range(DIFF_HEADS):
                inv = jnp.where((lane // (2 * DIFF_HD)) == h, 1.0 / l_ref[h * 2 + n], inv)
            outs.append(acc_ref[n] * inv)
        o = outs[0] - lam_ref[...] * outs[1]
        ms = _split_dot(o * o, _block_ones(256, 2 * DIFF_HD)) * (1.0 / (2 * DIFF_HD))
        o_ref[...] = o * lax.rsqrt(ms + DIFF_EPS) * g_ref[...] * (1.0 - lam_init)


def _flash_diff(q, k, v, lam_row, g_row, *, lam_init, nb, t_all, q_start, q_len, k_len, tk):
    tq = TM
    tpb_q, tpb_k = t_all // tq, t_all // tk
    q0 = q_start // tq
    nq = q_len // tq
    return pl.pallas_call(
        functools.partial(_flash_diff_kernel, lam_init=lam_init),
        grid=(nb, nq, k_len // tk),
        in_specs=[pl.BlockSpec((tq, 256), lambda b, i, kk: (b * tpb_q + q0 + i, 0)),
                  pl.BlockSpec((tk, 256), lambda b, i, kk: (b * tpb_k + kk, 0)),
                  pl.BlockSpec((tk, 256), lambda b, i, kk: (b * tpb_k + kk, 0)),
                  pl.BlockSpec((1, 256), lambda b, i, kk: (0, 0)),
                  pl.BlockSpec((1, 256), lambda b, i, kk: (0, 0))],
        out_specs=pl.BlockSpec((tq, 256), lambda b, i, kk: (b * nq + i, 0)),
        out_shape=jax.ShapeDtypeStruct((nb * q_len, 256), F32),
        scratch_shapes=[pltpu.VMEM((8, tq, 1), F32), pltpu.VMEM((8, tq, 1), F32),
                        pltpu.VMEM((2, tq, 256), F32)],
        compiler_params=_cparams(("parallel", "parallel", "arbitrary")),
        name="flash_diff",
    )(q, k, v, lam_row, g_row)


def _rw_prep_kernel(p_ref, prev_ref, next_ref, mu_ref, wlo_ref, g2_ref, vec_ref,
                    r_ref, v_ref, kk_ref, w0_ref, k0_ref, b0_ref, w1_ref, k1_ref, b1_ref,
                    bonus_ref, gate_ref, *, r0, lat_last):
    r = pl.program_id(1) + r0
    p = p_ref[...]
    row = lax.broadcasted_iota(jnp.int32, p.shape, 0)
    first_tile = jnp.logical_or(r == 0, r == 1)
    last_tile = jnp.logical_or(r == 0, r == lat_last)
    prev_row = jnp.where(first_tile, 0.0, prev_ref[7:8, :])
    next_row = jnp.where(last_tile, 0.0, next_ref[0:1, :])
    up = jnp.where(row == 0, prev_row, pltpu.roll(p, 1, 0))
    dn = jnp.where(row == TM - 1, next_row, pltpu.roll(p, TM - 1, 0))
    z = p + (0.5 * (up + dn) - p) * mu_ref[...]

    hw = RW_HEADS * RW_HEAD
    rr, k, v = z[:, :hw], z[:, hw:2 * hw], z[:, 2 * hw:3 * hw]
    lo = z[:, 3 * hw:3 * hw + 128]
    lane = lax.broadcasted_iota(jnp.int32, lo.shape, 1)
    lo = jnp.where(lane < 64, jnp.tanh(lo), lo).astype(BF16)
    wa = _dot(lo, wlo_ref[...])
    gate_ref[...] = _dot(_sigmoid(z[:, 3 * hw + 128:]).astype(BF16), g2_ref[...])

    e4 = _block_ones(hw, RW_HEAD)
    k_k, k_a, r_k = vec_ref[0:1, :], vec_ref[1:2, :], vec_ref[2:3, :]
    kk = k * k_k
    nrm = jnp.maximum(jnp.sqrt(_split_dot(kk * kk, e4)), 1e-12)
    kk = kk / nrm
    r_ref[...] = rr
    v_ref[...] = v
    kk_ref[...] = kk
    ksum = jnp.zeros_like(k)
    for d, (w_ref, kd_ref, b_ref) in enumerate(((w0_ref, k0_ref, b0_ref), (w1_ref, k1_ref, b1_ref))):
        w0 = vec_ref[3 + d:4 + d, :]
        a0 = vec_ref[5 + d:6 + d, :]
        wd = -_softplus(-(w0 + wa[:, d * hw:(d + 1) * hw])) - 0.5
        w_ref[...] = jnp.exp(-jnp.exp(wd))
        ad = _sigmoid(a0 + wa[:, (2 + d) * hw:(3 + d) * hw])
        kd = k * (1.0 + (ad - 1.0) * k_a)
        kd_ref[...] = kd
        b_ref[...] = kk * ad
        ksum = ksum + kd
    bonus_ref[...] = _split_dot(rr * ksum * r_k, e4) * v


def _rw_prep(p, mu, wlo, g2, vecs, *, nb, tpb, r0, nt):
    n_out = nb * nt * TM
    hw = RW_HEADS * RW_HEAD
    n_rows8 = p.shape[0] // 8
    tok = lambda b, r: (b * tpb + r0 + r, EXT_RW // 1024)
    prev = lambda b, r: (jnp.maximum((b * tpb + r0 + r) * (TM // 8) - 1, 0), EXT_RW // 1024)
    nxt = lambda b, r: (jnp.minimum((b * tpb + r0 + r + 1) * (TM // 8), n_rows8 - 1), EXT_RW // 1024)
    out = lambda b, r: (b * nt + r, 0)
    full = lambda b, r: (0, 0)
    return pl.pallas_call(
        functools.partial(_rw_prep_kernel, r0=r0, lat_last=tpb - 1),
        grid=(nb, nt),
        in_specs=[pl.BlockSpec((TM, 1024), tok),
                  pl.BlockSpec((8, 1024), prev),
                  pl.BlockSpec((8, 1024), nxt),
                  pl.BlockSpec((1, 1024), full),
                  pl.BlockSpec((128, 4 * hw), full),
                  pl.BlockSpec((128, hw), full),
                  pl.BlockSpec((8, hw), full)],
        out_specs=[pl.BlockSpec((TM, hw), out)] * 11,
        out_shape=[jax.ShapeDtypeStruct((n_out, hw), F32)] * 11,
        compiler_params=_cparams(("parallel", "parallel")),
        name="rw_prep",
    )(p, p, p, mu, wlo, g2, vecs)


def _rw_scan_kernel(rf, vf, kkf, wf, kf, bf, rb, vb, kkb, wb, kb, bb, yf_ref, yb_ref, s_ref, *, nb):
    c = RW_CHUNK

    @pl.when(pl.program_id(0) == 0)
    def _():
        s_ref[...] = jnp.zeros(s_ref.shape, F32)

    e22 = jnp.concatenate([_block_ones(128, RW_HEAD)] * 2, axis=0)
    lane = lax.broadcasted_iota(jnp.int32, (RW_HEAD, 128), 1)
    sub = lax.broadcasted_iota(jnp.int32, (RW_HEAD, 128), 0)
    diag = (lane % RW_HEAD) == sub
    sub8 = lax.broadcasted_iota(jnp.int32, (8, 128), 0)
    dirs = ((rf, vf, kkf, wf, kf, bf, yf_ref), (rb, vb, kkb, wb, kb, bb, yb_ref))

    def allreduce_rows(x):
        t = x[0:8]
        for i in range(1, 8):
            t = t + x[8 * i:8 * i + 8]
        for sh in (4, 2, 1):
            t = t + pltpu.roll(t, sh, 0)
        return t

    def group(g, carry):
        tiles, ytiles = {}, {}
        for d, refs in enumerate(dirs):
            base = pl.multiple_of((g if d == 0 else c // 8 - 1 - g) * 8, 8)
            for b in range(nb):
                for hp in range(2):
                    r_, v_, kk_, w_, k_, b_ = [ref[b, pl.ds(base, 8), pl.ds(hp * 128, 128)] for ref in refs[:6]]
                    his, mids = [], []
                    for x in (w_, b_, k_, kk_, r_):
                        hi = x.astype(BF16).astype(F32)
                        his.append(hi)
                        mids.append(x - hi)
                    tiles[d, b, hp] = (base, v_, his, mids)
                    ytiles[d, b, hp] = jnp.zeros((8, 128), F32)
        for jj in range(8):
            for d, refs in enumerate(dirs):
                j = jj if d == 0 else 7 - jj
                for b in range(nb):
                    for hp in range(2):
                        u = (d * nb + b) * 2 + hp
                        base, v_, his, mids = tiles[d, b, hp]
                        parts = [jnp.concatenate([jnp.where(diag, hi[j:j + 1, :], 0.0).astype(BF16),
                                                  jnp.where(diag, mid[j:j + 1, :], 0.0).astype(BF16)], axis=1)
                                 for hi, mid in zip(his, mids)]
                        cm = _dot(jnp.concatenate(parts, axis=0), e22)
                        wc, bc, kc, kkc, rc = [cm[RW_HEAD * i:RW_HEAD * (i + 1)] for i in range(5)]
                        s = s_ref[u]
                        sa = jnp.concatenate([allreduce_rows(kkc * s)] * 8, axis=0)
                        s = wc * s - bc * sa + kc * v_[j:j + 1, :]
                        s_ref[u] = s
                        ytiles[d, b, hp] = jnp.where(sub8 == j, allreduce_rows(rc * s), ytiles[d, b, hp])
        for d, refs in enumerate(dirs):
            for b in range(nb):
                for hp in range(2):
                    refs[6][b, pl.ds(tiles[d, b, hp][0], 8), pl.ds(hp * 128, 128)] = ytiles[d, b, hp]
        return carry

    lax.fori_loop(0, c // 8, group, 0)


def _rw_scan(r, v, kk, w0, k0, b0, w1, k1, b1, *, nb, t_all, n_ctx):
    c = RW_CHUNK
    nc, ncc = t_all // c, n_ctx // c
    hw = RW_HEADS * RW_HEAD
    shp = lambda a: a.reshape(nb, t_all, hw)
    fwd = lambda j: (0, j, 0)
    bwd = lambda j: (0, jnp.where(j < ncc, ncc - 1 - j, nc - 1 - (j - ncc)), 0)
    blk = (nb, c, hw)
    yshape = jax.ShapeDtypeStruct((nb, t_all, hw), F32)
    yf, yb = pl.pallas_call(
        functools.partial(_rw_scan_kernel, nb=nb),
        grid=(nc,),
        in_specs=[pl.BlockSpec(blk, fwd)] * 6 + [pl.BlockSpec(blk, bwd)] * 6,
        out_specs=[pl.BlockSpec(blk, fwd), pl.BlockSpec(blk, bwd)],
        out_shape=[yshape, yshape],
        scratch_shapes=[pltpu.VMEM((2 * nb * 2, RW_HEAD, 128), F32)],
        compiler_params=_cparams(("arbitrary",)),
        name="rw_scan",
    )(shp(r), shp(v), shp(kk), shp(w0), shp(k0), shp(b0),
      shp(r), shp(v), shp(kk), shp(w1), shp(k1), shp(b1))

    return yf.reshape(nb * t_all, hw), yb.reshape(nb * t_all, hw)


def _rw_post_kernel(yf_ref, yb_ref, bonus_ref, gate_ref, vec_ref, o_ref):
    e4 = _block_ones(RW_HEADS * RW_HEAD, RW_HEAD)
    y = yf_ref[...] + yb_ref[...]
    mean = _split_dot(y, e4) * (1.0 / RW_HEAD)
    yc = y - mean
    var = _split_dot(yc * yc, e4) * (1.0 / RW_HEAD)
    yn = yc * lax.rsqrt(var + RW_LN_EPS) * vec_ref[0:1, :] + vec_ref[1:2, :]
    o_ref[...] = (yn + bonus_ref[...]) * gate_ref[...]


def _rw_post(yf, yb, bonus, gate, vecs, *, nb, tpb_y, r0_y, nt):
    hw = RW_HEADS * RW_HEAD
    n_out = bonus.shape[0]
    ytok = lambda b, r: (b * tpb_y + r0_y + r, 0)
    tok = lambda b, r: (b * nt + r, 0)
    return pl.pallas_call(
        _rw_post_kernel,
        grid=(nb, nt),
        in_specs=[pl.BlockSpec((TM, hw), ytok), pl.BlockSpec((TM, hw), ytok),
                  pl.BlockSpec((TM, hw), tok), pl.BlockSpec((TM, hw), tok),
                  pl.BlockSpec((8, hw), lambda b, r: (0, 0))],
        out_specs=pl.BlockSpec((TM, hw), tok),
        out_shape=jax.ShapeDtypeStruct((n_out, hw), F32),
        compiler_params=_cparams(("parallel", "parallel")),
        name="rw_post",
    )(yf, yb, bonus, gate, vecs)


def _s5_bu_kernel(p_ref, w_ref, o_ref):
    o_ref[...] = _dot(p_ref[...].astype(BF16), w_ref[...])


def _s5_bu(p, w_b, *, nb, tpb):
    n = p.shape[0]
    nst = S5_GROUPS * S5_STATE
    return pl.pallas_call(
        _s5_bu_kernel,
        grid=(nb * tpb, 4),
        in_specs=[pl.BlockSpec((TM, 256), lambda i, j: (i, EXT_S5 // 256)),
                  pl.BlockSpec((256, nst), lambda i, j: (0, j))],
        out_specs=pl.BlockSpec((TM, nst), lambda i, j: (i, j)),
        out_shape=jax.ShapeDtypeStruct((n, 4 * nst), F32),
        compiler_params=_cparams(("parallel", "arbitrary")),
        name="s5_bu",
    )(p, w_b)


def _s5_scan_kernel(buf_ref, bub_ref, ab_ref, xf_ref, xb_ref, st_ref, *, nb):
    c = S5_CHUNK
    nst = S5_GROUPS * S5_STATE

    @pl.when(pl.program_id(0) == 0)
    def _():
        st_ref[...] = jnp.zeros(st_ref.shape, F32)

    dirs = ((buf_ref, xf_ref), (bub_ref, xb_ref))

    def group(g, carry):
        for d, (bu_ref, x_ref) in enumerate(dirs):
            base = pl.multiple_of((g if d == 0 else c // 8 - 1 - g) * 8, 8)
            ar = ab_ref[d, :, 0:nst]
            ai = ab_ref[d, :, nst:2 * nst]
            for b in range(nb):
                u = d * nb + b
                xr = st_ref[u, :, 0:nst]
                xi = st_ref[u, :, nst:2 * nst]
                bur = bu_ref[b, pl.ds(base, 8), 0:nst]
                bui = bu_ref[b, pl.ds(base, 8), nst:2 * nst]
                rows_r, rows_i = [None] * 8, [None] * 8
                for jj in range(8):
                    j = jj if d == 0 else 7 - jj
                    xr, xi = (ar * xr - ai * xi + bur[j:j + 1, :], ar * xi + ai * xr + bui[j:j + 1, :])
                    rows_r[j], rows_i[j] = xr, xi
                st_ref[u, :, 0:nst] = xr
                st_ref[u, :, nst:2 * nst] = xi
                x_ref[b, pl.ds(base, 8), 0:nst] = jnp.concatenate(rows_r, axis=0)
                x_ref[b, pl.ds(base, 8), nst:2 * nst] = jnp.concatenate(rows_i, axis=0)
        return carry

    lax.fori_loop(0, c // 8, group, 0)


def _s5_scan(bu, ab, *, nb, t_all, n_ctx):
    c = S5_CHUNK
    nc, ncc = t_all // c, n_ctx // c
    nst2 = 2 * S5_GROUPS * S5_STATE
    bu3 = bu.reshape(nb, t_all, 2 * nst2)
    fwd = lambda j: j
    bwd = lambda j: jnp.where(j < ncc, ncc - 1 - j, nc - 1 - (j - ncc))
    xshape = jax.ShapeDtypeStruct((nb, t_all, nst2), F32)
    xf, xb = pl.pallas_call(
        functools.partial(_s5_scan_kernel, nb=nb),
        grid=(nc,),
        in_specs=[pl.BlockSpec((nb, c, nst2), lambda j: (0, fwd(j), 0)),
                  pl.BlockSpec((nb, c, nst2), lambda j: (0, bwd(j), 1)),
                  pl.BlockSpec((2, 1, nst2), lambda j: (0, 0, 0))],
        out_specs=[pl.BlockSpec((nb, c, nst2), lambda j: (0, fwd(j), 0)),
                   pl.BlockSpec((nb, c, nst2), lambda j: (0, bwd(j), 0))],
        out_shape=[xshape, xshape],
        scratch_shapes=[pltpu.VMEM((2 * nb, 1, nst2), F32)],
        compiler_params=_cparams(("arbitrary",)),
        name="s5_scan",
    )(bu3, bu3, ab)
    return xf.reshape(nb * t_all, nst2), xb.reshape(nb * t_all, nst2)


def _s5_post_kernel(xf_ref, xb_ref, u_ref, cf_ref, cb_ref, d_ref, gw_ref, gb_ref, o_ref):
    y = (_dot(xf_ref[...].astype(BF16), cf_ref[...]) + _dot(xb_ref[...].astype(BF16), cb_ref[...])
         + d_ref[...] * u_ref[...])
    zg = 0.5 * y * (1.0 + jnp.tanh(math.sqrt(2.0 / math.pi) * (y + 0.044715 * (y * y * y))))
    o_ref[...] = zg * _sigmoid(_dot(zg.astype(BF16), gw_ref[...]) + gb_ref[...])


def _s5_post(xf, xb, p, cf, cb, d_row, glu_w, glu_b, *, nb, tpb, r0, nt):
    nst2 = 2 * S5_GROUPS * S5_STATE
    tok = lambda b, r: (b * tpb + r0 + r, 0)
    full = lambda b, r: (0, 0)
    return pl.pallas_call(
        _s5_post_kernel,
        grid=(nb, nt),
        in_specs=[pl.BlockSpec((TM, nst2), tok), pl.BlockSpec((TM, nst2), tok),
                  pl.BlockSpec((TM, 256), lambda b, r: (b * tpb + r0 + r, EXT_S5 // 256)),
                  pl.BlockSpec((nst2, 256), full), pl.BlockSpec((nst2, 256), full),
                  pl.BlockSpec((1, 256), full), pl.BlockSpec((256, 256), full),
                  pl.BlockSpec((1, 256), full)],
        out_specs=pl.BlockSpec((TM, 256), lambda b, r: (b * nt + r, 0)),
        out_shape=jax.ShapeDtypeStruct((nb * nt * TM, 256), F32),
        compiler_params=_cparams(("parallel", "parallel")),
        name="s5_post",
    )(xf, xb, p, cf, cb, d_row, glu_w, glu_b)


def _merge_kernel(ya_ref, yb_ref, ys_ref, yd_ref, g0, g1, g2, g3, h_ref, m2_ref, wb_ref, wo_ref, o_ref):
    acc = None
    for n, (y_ref, g_ref) in enumerate(((ya_ref, g0), (yb_ref, g1), (ys_ref, g2), (yd_ref, g3))):
        term = _sigmoid(g_ref[...]) * _dot(y_ref[...].astype(BF16), wb_ref[n])
        acc = term if acc is None else acc + term
    o_ref[...] = h_ref[...] + m2_ref[0] * _dot(acc.astype(BF16), wo_ref[...])


def _merge(ya, yb, ys, yd, p, h, mod2, w_branch, w_out, *, nb, tpb, r0, nt):
    tok_in = lambda b, r: (b * tpb + r0 + r, 0)
    tok_out = lambda b, r: (b * nt + r, 0)
    full2 = lambda b, r: (0, 0)
    gate = [pl.BlockSpec((TM, D_MODEL), lambda b, r, c=c: (b * tpb + r0 + r, EXT_GATE // D_MODEL + c))
            for c in range(4)]
    return pl.pallas_call(
        _merge_kernel,
        grid=(nb, nt),
        in_specs=[pl.BlockSpec((TM, 256), tok_out)] * 4 + gate
                 + [pl.BlockSpec((TM, D_MODEL), tok_in),
                    pl.BlockSpec((1, 1, D_MODEL), lambda b, r: (jnp.where(r0 + r == 0, nb, b), 0, 0)),
                    pl.BlockSpec((4, 256, D_MODEL), lambda b, r: (0, 0, 0)),
                    pl.BlockSpec((D_MODEL, D_MODEL), full2)],
        out_specs=pl.BlockSpec((TM, D_MODEL), tok_out),
        out_shape=jax.ShapeDtypeStruct((nb * nt * TM, D_MODEL), F32),
        compiler_params=_cparams(("parallel", "parallel")),
        name="merge",
    )(ya, yb, ys, yd, p, p, p, p, h, mod2, w_branch, w_out)


def _router_kernel(h_ref, g_ref, sh_ref, sc_ref, wh_ref, wm_ref, b_ref, x_ref, lg_ref):
    x = _rms(h_ref[...], g_ref[...], NORM_EPS) * (1.0 + sc_ref[0]) + sh_ref[0]
    xh = x.astype(BF16)
    xm = (x - xh.astype(F32)).astype(BF16)
    x_ref[...] = xh
    lg = (_dot(xh, wh_ref[...]) + _dot(xm, wh_ref[...]) + _dot(xh, wm_ref[...])) + b_ref[...]

    lane = lax.broadcasted_iota(jnp.int32, lg.shape, 1)
    lanef = lane.astype(F32)
    neg = jnp.float32(-jnp.inf)
    big = jnp.float32(1e9)
    rmax = lambda v: jnp.max(v, axis=-1, keepdims=True)
    rmin = lambda v: jnp.min(v, axis=-1, keepdims=True)
    rsum = lambda v: jnp.sum(v, axis=-1, keepdims=True)

    gmask = lane < MOE_GROUPS
    mg = rmax(jnp.where(gmask, lg, neg))
    eg = jnp.where(gmask, jnp.exp(lg - mg), 0.0)
    pg = eg / rsum(eg)
    pg_top = rmax(pg)
    g_sel = rmin(jnp.where(jnp.logical_and(gmask, pg == pg_top), lanef, big))
    lo = MOE_GROUPS + MOE_PER_GROUP * g_sel
    emask = jnp.logical_and(lanef >= lo, lanef < lo + MOE_PER_GROUP)
    me = rmax(jnp.where(emask, lg, neg))
    ee = jnp.where(emask, jnp.exp(lg - me), 0.0)
    pe = jnp.where(emask, ee / rsum(ee), -1.0)
    p1 = rmax(pe)
    i1 = rmin(jnp.where(pe == p1, lanef, big))
    pe2 = jnp.where(lanef == i1, -1.0, pe)
    p2 = rmax(pe2)
    i2 = rmin(jnp.where(pe2 == p2, lanef, big))
    den = p1 + p2
    out = jnp.where(lane == 0, i1 - MOE_GROUPS, 0.0)
    out = jnp.where(lane == 1, i2 - MOE_GROUPS, out)
    out = jnp.where(lane == 2, pg_top * p1 / den, out)
    out = jnp.where(lane == 3, pg_top * p2 / den, out)
    lg_ref[...] = out


def _router(h, g, shift, scale, wh, wm, bias, *, nb, nt, ctx_first):
    tok = lambda b, r: (b * nt + r, 0)
    full = lambda b, r: (0, 0)
    if ctx_first:
        modmap = lambda b, r: (jnp.where(r == 0, nb, b), 0, 0)
    else:
        modmap = lambda b, r: (b, 0, 0)
    n = h.shape[0]
    return pl.pallas_call(
        _router_kernel,
        grid=(nb, nt),
        in_specs=[pl.BlockSpec((TM, D_MODEL), tok), pl.BlockSpec((1, D_MODEL), full),
                  pl.BlockSpec((1, 1, D_MODEL), modmap), pl.BlockSpec((1, 1, D_MODEL), modmap),
                  pl.BlockSpec((D_MODEL, 128), full), pl.BlockSpec((D_MODEL, 128), full),
                  pl.BlockSpec((1, 128), full)],
        out_specs=[pl.BlockSpec((TM, D_MODEL), tok), pl.BlockSpec((TM, 128), tok)],
        out_shape=[jax.ShapeDtypeStruct((n, D_MODEL), BF16), jax.ShapeDtypeStruct((n, 128), F32)],
        compiler_params=_cparams(("parallel", "parallel")),
        name="router",
    )(h, g, shift, scale, wh, wm, bias)


def _expert_kernel(be_ref, nv_ref, x_ref, wg_ref, wu_ref, wd_ref, o_ref):
    @pl.when(pl.program_id(0) < nv_ref[0])
    def _():
        x = x_ref[...]
        hb = _silu(_dot(x, wg_ref[0])) * _dot(x, wu_ref[0])
        o_ref[...] = _dot(hb.astype(BF16), wd_ref[0])

    @pl.when(pl.program_id(0) >= nv_ref[0])
    def _():
        o_ref[...] = jnp.zeros(o_ref.shape, F32)


def _experts(xs, block_e, n_valid, w_gate, w_up, w_down):
    n_slots = xs.shape[0]
    n_blocks = n_slots // MOE_BLK
    wmap = lambda i, be, nv: (be[i], 0, 0)
    return pl.pallas_call(
        _expert_kernel,
        grid_spec=pltpu.PrefetchScalarGridSpec(
            num_scalar_prefetch=2,
            grid=(n_blocks,),
            in_specs=[pl.BlockSpec((MOE_BLK, D_MODEL), lambda i, be, nv: (i, 0)),
                      pl.BlockSpec((1, D_MODEL, D_EXPERT), wmap),
                      pl.BlockSpec((1, D_MODEL, D_EXPERT), wmap),
                      pl.BlockSpec((1, D_EXPERT, D_MODEL), wmap)],
            out_specs=pl.BlockSpec((MOE_BLK, D_MODEL), lambda i, be, nv: (i, 0))),
        out_shape=jax.ShapeDtypeStruct((n_slots, D_MODEL), F32),
        compiler_params=_cparams(("arbitrary",)),
        name="experts",
    )(block_e, n_valid, xs, w_gate, w_up, w_down)


def _combine_kernel(h_ref, y0_ref, y1_ref, w_ref, m5_ref, g_ref, o_ref, *, final):
    w = w_ref[...]
    y = y0_ref[...] * w[:, 0:1] + y1_ref[...] * w[:, 1:2]
    h = h_ref[...] + m5_ref[0] * y
    if final:
        h = _rms(h, g_ref[...], NORM_EPS)
    o_ref[...] = h


def _combine(h, y0, y1, wts, mod5, g_final, *, nb, nt, ctx_first, final):
    tok = lambda b, r: (b * nt + r, 0)
    if ctx_first:
        modmap = lambda b, r: (jnp.where(r == 0, nb, b), 0, 0)
    else:
        modmap = lambda b, r: (b, 0, 0)
    return pl.pallas_call(
        functools.partial(_combine_kernel, final=final),
        grid=(nb, nt),
        in_specs=[pl.BlockSpec((TM, D_MODEL), tok)] * 3
                 + [pl.BlockSpec((TM, 128), tok), pl.BlockSpec((1, 1, D_MODEL), modmap),
                    pl.BlockSpec((1, D_MODEL), lambda b, r: (0, 0))],
        out_specs=pl.BlockSpec((TM, D_MODEL), tok),
        out_shape=jax.ShapeDtypeStruct(h.shape, F32),
        compiler_params=_cparams(("parallel", "parallel")),
        name="combine",
    )(h, y0, y1, wts, mod5, g_final)


def _moe(h, g2, shift, scale, mod5, wh, wm, rbias, w_gate, w_up, w_down, g_final, *, nb, nt, ctx_first, final):
    n = h.shape[0]
    x_bf, route = _router(h, g2, shift, scale, wh, wm, rbias, nb=nb, nt=nt, ctx_first=ctx_first)
    idx = route[:, :MOE_TOPK].astype(jnp.int32)
    wts = route[:, MOE_TOPK:2 * MOE_TOPK]
    n_as = n * MOE_TOPK
    flat_e = idx.reshape(n_as)
    onehot = (flat_e[:, None] == jnp.arange(MOE_EXPERTS, dtype=jnp.int32)[None, :]).astype(jnp.int32)
    csum = jnp.cumsum(onehot, axis=0)
    counts = csum[-1]
    rank = jnp.sum(jnp.where(onehot > 0, csum - 1, 0), axis=1)
    padded = (counts + MOE_BLK - 1) // MOE_BLK * MOE_BLK
    pad_end = jnp.cumsum(padded)
    pad_start = pad_end - padded
    slot = pad_start[flat_e] + rank
    n_blocks = (n_as + MOE_EXPERTS * (MOE_BLK - 1) + MOE_BLK - 1) // MOE_BLK
    n_slots = n_blocks * MOE_BLK
    slot_tok = jnp.zeros((n_slots,), jnp.int32).at[slot].set(jnp.arange(n_as, dtype=jnp.int32) // MOE_TOPK)
    block_e = jnp.minimum(jnp.searchsorted(pad_end, jnp.arange(n_blocks, dtype=jnp.int32) * MOE_BLK,
                                           side='right'), MOE_EXPERTS - 1).astype(jnp.int32)
    n_valid = (pad_end[-1:] // MOE_BLK).astype(jnp.int32)
    xs = jnp.take(x_bf, slot_tok, axis=0)
    ys = _experts(xs, block_e, n_valid, w_gate, w_up, w_down)
    slot2 = slot.reshape(n, MOE_TOPK)
    y0 = jnp.take(ys, slot2[:, 0], axis=0)
    y1 = jnp.take(ys, slot2[:, 1], axis=0)
    wts_pad = jnp.pad(wts.astype(F32), ((0, 0), (0, 128 - MOE_TOPK)))
    return _combine(h, y0, y1, wts_pad, mod5, g_final, nb=nb, nt=nt, ctx_first=ctx_first, final=final)


_ROT_SRC = np.array(list(range(8, 16)) + list(range(0, 8)) + list(range(24, 32)) + list(range(16, 24)))
_ROT_SIGN = np.array([-1.0] * 8 + [1.0] * 8 + [-1.0] * 8 + [1.0] * 8, np.float32)


def _rot_cols(w):
    k = w.shape[-1] // ROPE_DIM
    src = np.concatenate([_ROT_SRC + ROPE_DIM * i for i in range(k)])
    sign = np.tile(_ROT_SIGN, k)
    return w[..., src] * sign


def _rope_tables(n_ctx, n_lat):
    rows = n_lat // GRID_W
    row = jnp.repeat(jnp.arange(rows, dtype=F32), GRID_W)
    col = (jnp.arange(rows * GRID_W) % GRID_W).astype(F32)
    nf = ROPE_DIM // 4
    inv = ROPE_BASE ** (-jnp.arange(nf, dtype=F32) / nf)
    ar = row[:, None] * inv
    ac = col[:, None] * inv
    ang = jnp.concatenate([ar, ar, ac, ac], axis=-1)
    cos = jnp.concatenate([jnp.ones((n_ctx, ROPE_DIM), F32), jnp.cos(ang)], axis=0)
    sin = jnp.concatenate([jnp.zeros((n_ctx, ROPE_DIM), F32), jnp.sin(ang)], axis=0)
    return cos, sin


def _block_diag(blocks):
    g, a, b = blocks.shape
    eye = jnp.eye(g, dtype=blocks.dtype)
    return (eye[:, None, :, None] * blocks[:, :, None, :]).reshape(g * a, g * b)


def _pick_tk(t_all):
    best = 128
    for tk in range(128, 1537, 128):
        if t_all % tk == 0:
            best = tk
    return best


def kernel(x, c, ctx, c_ctx, w_mod, b_mod, norm1_g, norm2_g, w_in, mla_q_norm_g, mla_kv_norm_g, mla_w_uq, mla_w_ukv, rw_mu, rw_w0, rw_w2, rw_a0, rw_a2, rw_g2, rw_k_k, rw_k_a, rw_r_k, rw_lnx_g, rw_lnx_b, s5_a_re, s5_a_im, s5_log_dt, s5_b_re, s5_b_im, s5_c_re, s5_c_im, s5_d, s5_glu_w, s5_glu_b, diff_lq1, diff_lk1, diff_lq2, diff_lk2, diff_subln_g, w_branch, w_out, router_g_w, router_g_b, router_e_w, router_e_b, exp_w_gate, exp_w_up, exp_w_down, final_norm_g):
    nb, n_lat, d = x.shape
    n_ctx = ctx.shape[1]
    depth = w_mod.shape[0]
    t_all = n_ctx + n_lat
    assert d == D_MODEL and n_ctx == TM and n_lat % TM == 0
    tpb = t_all // TM
    tk = _pick_tk(t_all)
    hw = RW_HEADS * RW_HEAD

    cos, sin = _rope_tables(n_ctx, n_lat)
    mla_scale = (MLA_NOPE + MLA_ROPE) ** -0.5
    z32 = jnp.zeros((t_all, 32), F32)
    cq_tab = jnp.concatenate([jnp.ones((t_all, 64), F32), cos, z32], axis=1) * mla_scale
    sq_tab = jnp.concatenate([jnp.zeros((t_all, 64), F32), sin, z32], axis=1) * mla_scale
    ck_tab = jnp.concatenate([cos, sin, jnp.zeros((t_all, 64), F32)], axis=1)
    dcos = jnp.tile(cos, (1, 8))
    dsin = jnp.tile(sin, (1, 8))

    c_rows = jnp.concatenate([c, c_ctx[None, :], jnp.zeros((8 - nb - 1, d), F32)], axis=0)

    h = jnp.concatenate([ctx, x], axis=1).reshape(nb * t_all, d)

    for l in range(depth):
        last = l == depth - 1
        r0, nt = (1, tpb - 1) if last else (0, tpb)

        mod = _mm(c_rows, w_mod[l].astype(BF16), b_mod[l][None, :], tm=8, tn=1536, pre_silu=True, name="mod")
        mods = [mod[:nb + 1, i * d:(i + 1) * d].reshape(nb + 1, 1, d) for i in range(6)]

        wi = w_in[l]
        o_rw, o_s5, o_df, o_gt = 416, 1440, 1696, 2464
        w_kr = wi[:, 384:416]
        w_dq, w_dk, w_dv = wi[:, o_df:o_df + 256], wi[:, o_df + 256:o_df + 512], wi[:, o_df + 512:o_df + 768]
        w_ext = jnp.concatenate(
            [wi[:, o_rw:o_s5],
             wi[:, :416], _rot_cols(w_kr), jnp.zeros((d, 64), F32),
             wi[:, o_s5:o_df],
             w_dq, w_dk, w_dv, _rot_cols(w_dq), _rot_cols(w_dk),
             wi[:, o_gt:]], axis=1).astype(BF16)
        p = _inproj(h, norm1_g[l][None, :], mods[0], mods[1], w_ext, nb=nb, tpb=tpb)

        wq = mla_w_uq[l].reshape(MLA_Q_LORA, MLA_HEADS, MLA_NOPE + MLA_ROPE)
        zq = jnp.zeros((MLA_Q_LORA, MLA_HEADS, 32), F32)
        wa = jnp.concatenate([wq, zq], axis=2).reshape(MLA_Q_LORA, 512).astype(BF16)
        wb = jnp.concatenate([jnp.zeros((MLA_Q_LORA, MLA_HEADS, 64), F32), _rot_cols(wq[:, :, MLA_NOPE:]), zq],
                             axis=2).reshape(MLA_Q_LORA, 512).astype(BF16)
        wkv = mla_w_ukv[l].reshape(MLA_KV_LORA, MLA_HEADS, MLA_NOPE + MLA_V)
        wk = jnp.concatenate([wkv[:, :, :MLA_NOPE], jnp.zeros((MLA_KV_LORA, MLA_HEADS, 64), F32)],
                             axis=2).reshape(MLA_KV_LORA, 512).astype(BF16)
        wv = wkv[:, :, MLA_NOPE:].reshape(MLA_KV_LORA, MLA_HEADS * MLA_V).astype(BF16)
        pk_np = np.zeros((128, 512), np.float32)
        for hh in range(MLA_HEADS):
            for i in range(32):
                pk_np[i, hh * 128 + 64 + i] = 1.0
                pk_np[32 + i, hh * 128 + 64 + i] = 1.0
        pk = jnp.asarray(pk_np, BF16)
        q_m, k_m, v_m = _mla_prep(p, mla_q_norm_g[l][None, :], mla_kv_norm_g[l][None, :], wa, wb, wk, wv, pk,
                                  cq_tab, sq_tab, ck_tab, nb=nb, tpb=tpb)
        ya_lat = _flash_mla(q_m, k_m, v_m, nb=nb, t_all=t_all, q_start=n_ctx, q_len=n_lat, k_len=t_all, tk=tk)

        q_d, k_d, v_d = _diff_prep(p, dcos, dsin, nb=nb, tpb=tpb)
        lam_init = 0.8 - 0.6 * math.exp(-0.3 * l)
        lam = (jnp.exp(jnp.sum(diff_lq1[l] * diff_lk1[l])) - jnp.exp(jnp.sum(diff_lq2[l] * diff_lk2[l])) + lam_init)
        lam_row = jnp.full((1, 256), lam, F32)
        g_row = jnp.tile(diff_subln_g[l], DIFF_HEADS)[None, :]
        yd_lat = _flash_diff(q_d, k_d, v_d, lam_row, g_row, lam_init=lam_init, nb=nb, t_all=t_all,
                             q_start=n_ctx, q_len=n_lat, k_len=t_all, tk=tk)
        if last:
            ya, yd = ya_lat, yd_lat
        else:
            ya_ctx = _flash_mla(q_m, k_m, v_m, nb=nb, t_all=t_all, q_start=0, q_len=n_ctx, k_len=n_ctx, tk=n_ctx)
            yd_ctx = _flash_diff(q_d, k_d, v_d, lam_row, g_row, lam_init=lam_init, nb=nb, t_all=t_all,
                                 q_start=0, q_len=n_ctx, k_len=n_ctx, tk=n_ctx)
            comb = lambda a_c, a_l: jnp.concatenate(
                [a_c.reshape(nb, n_ctx, -1), a_l.reshape(nb, n_lat, -1)], axis=1).reshape(nb * t_all, -1)
            ya, yd = comb(ya_ctx, ya_lat), comb(yd_ctx, yd_lat)

        wlo = jnp.zeros((128, 4 * hw), F32)
        wlo = wlo.at[:64, 0:hw].set(rw_w2[l, 0]).at[:64, hw:2 * hw].set(rw_w2[l, 1])
        wlo = wlo.at[64:, 2 * hw:3 * hw].set(rw_a2[l, 0]).at[64:, 3 * hw:].set(rw_a2[l, 1])
        vecs = jnp.stack([rw_k_k[l], rw_k_a[l], rw_r_k[l].reshape(hw), rw_w0[l, 0], rw_w0[l, 1],
                          rw_a0[l, 0], rw_a0[l, 1], jnp.zeros((hw,), F32)], axis=0)
        (r_, v_, kk_, w0_, k0_, b0_, w1_, k1_, b1_, bonus, gate_rw) = _rw_prep(
            p, rw_mu[l][None, :], wlo.astype(BF16), rw_g2[l].astype(BF16), vecs, nb=nb, tpb=tpb, r0=0, nt=tpb)
        yf, yb_ = _rw_scan(r_, v_, kk_, w0_, k0_, b0_, w1_, k1_, b1_, nb=nb, t_all=t_all, n_ctx=n_ctx)
        ln_vecs = jnp.concatenate([rw_lnx_g[l][None, :], rw_lnx_b[l][None, :], jnp.zeros((6, hw), F32)], axis=0)
        if last:
            trim = lambda a: a.reshape(nb, t_all, hw)[:, n_ctx:].reshape(nb * n_lat, hw)
            bonus, gate_rw = trim(bonus), trim(gate_rw)
        y_rw = _rw_post(yf, yb_, bonus, gate_rw, ln_vecs, nb=nb, tpb_y=tpb, r0_y=r0, nt=nt)

        bbs, abs_, cfs = [], [], []
        for dd in range(2):
            lr, li = s5_a_re[l, dd], s5_a_im[l, dd]
            dt = jnp.exp(s5_log_dt[l, dd])[:, None]
            mag = jnp.exp(lr * dt)
            ab_re, ab_im = mag * jnp.cos(li * dt), mag * jnp.sin(li * dt)
            den = lr * lr + li * li
            nr, ni = ab_re - 1.0, ab_im
            cf_re = (nr * lr + ni * li) / den
            cf_im = (ni * lr - nr * li) / den
            bre, bim = s5_b_re[l, dd], s5_b_im[l, dd]
            bb_re = cf_re[..., None] * bre - cf_im[..., None] * bim
            bb_im = cf_re[..., None] * bim + cf_im[..., None] * bre
            bbs.append(jnp.concatenate([_block_diag(bb_re.transpose(0, 2, 1)),
                                        _block_diag(bb_im.transpose(0, 2, 1))], axis=1))
            abs_.append(jnp.concatenate([ab_re.reshape(-1), ab_im.reshape(-1)])[None, :])
            cfs.append(jnp.concatenate([_block_diag(s5_c_re[l, dd].transpose(0, 2, 1)),
                                        -_block_diag(s5_c_im[l, dd].transpose(0, 2, 1))], axis=0))
        bu = _s5_bu(p, jnp.concatenate(bbs, axis=1).astype(BF16), nb=nb, tpb=tpb)
        xf, xb = _s5_scan(bu, jnp.stack(abs_, axis=0), nb=nb, t_all=t_all, n_ctx=n_ctx)
        y_s5 = _s5_post(xf, xb, p, cfs[0].astype(BF16), cfs[1].astype(BF16), s5_d[l].reshape(1, 256),
                        s5_glu_w[l].astype(BF16), s5_glu_b[l][None, :], nb=nb, tpb=tpb, r0=r0, nt=nt)

        h = _merge(ya, y_rw, y_s5, yd, p, h, mods[2], w_branch[l].astype(BF16), w_out[l].astype(BF16),
                   nb=nb, tpb=tpb, r0=r0, nt=nt)

        wr = jnp.concatenate([router_g_w[l], router_e_w[l], jnp.zeros((d, 128 - MOE_GROUPS - MOE_EXPERTS), F32)], axis=1)
        wr_h = wr.astype(BF16)
        wr_m = (wr - wr_h.astype(F32)).astype(BF16)
        rbias = jnp.concatenate([router_g_b[l], router_e_b[l],
                                 jnp.zeros((128 - MOE_GROUPS - MOE_EXPERTS,), F32)])[None, :]
        h = _moe(h, norm2_g[l][None, :], mods[3], mods[4], mods[5], wr_h, wr_m, rbias,
                 exp_w_gate[l].astype(BF16), exp_w_up[l].astype(BF16), exp_w_down[l].astype(BF16),
                 final_norm_g[None, :], nb=nb, nt=nt, ctx_first=not last, final=last)

    return h.reshape(nb, n_lat, d)
```

```python
import functools
import math

import jax
import jax.numpy as jnp
import numpy as np
from jax import lax
from jax.experimental import pallas as pl
from jax.experimental.pallas import tpu as pltpu

F32 = jnp.float32
BF16 = jnp.bfloat16

TM = 256
VMEM_LIMIT = 48 * 1024 * 1024

D_MODEL = 1024
GRID_W = 64
ROPE_DIM = 32
ROPE_BASE = 10000.0
NORM_EPS = 1e-6
MLA_HEADS, MLA_NOPE, MLA_ROPE, MLA_V = 4, 64, 32, 64
MLA_Q_LORA, MLA_KV_LORA = 256, 128
RW_HEADS, RW_HEAD = 4, 64
RW_LN_EPS = 64e-5
S5_GROUPS, S5_GROUP_CH, S5_STATE = 16, 16, 64
DIFF_HEADS, DIFF_HD = 4, 32
DIFF_EPS = 1e-5
MOE_GROUPS, MOE_PER_GROUP, MOE_TOPK = 4, 8, 2
MOE_EXPERTS = MOE_GROUPS * MOE_PER_GROUP
D_EXPERT = 512
MOE_BLK = 256
RW_CHUNK = 64
S5_CHUNK = 128
ATTN_TK_MAX = 2816
LOG2E = math.log2(math.e)

EXT_RW, EXT_MLA, EXT_S5, EXT_DIFF, EXT_GATE = 0, 1024, 1536, 1792, 3072
N_EXT = 7168


def _cparams(sem, vmem=VMEM_LIMIT):
    return pltpu.CompilerParams(dimension_semantics=sem, vmem_limit_bytes=vmem)


def _dot(a, b):
    return jnp.dot(a, b, preferred_element_type=F32)


def _dot_nt(a, b):
    return lax.dot_general(a, b, (((1,), (1,)), ((), ())), preferred_element_type=F32)


def _split_dot(x, e):
    hi = x.astype(BF16)
    mid = (x - hi.astype(F32)).astype(BF16)
    return _dot(hi, e) + _dot(mid, e)


def _block_ones(n, blk):
    r = lax.broadcasted_iota(jnp.int32, (n, n), 0) // blk
    c = lax.broadcasted_iota(jnp.int32, (n, n), 1) // blk
    return (r == c).astype(BF16)


def _sigmoid(x):
    return 1.0 / (1.0 + jnp.exp(-x))


def _silu(x):
    return x * _sigmoid(x)


def _softplus(x):
    return jnp.maximum(x, 0.0) + jnp.log(1.0 + jnp.exp(-jnp.abs(x)))


def _rms(x, g, eps):
    return x * lax.rsqrt(jnp.mean(x * x, axis=-1, keepdims=True) + eps) * g


def _mm_kernel(x_ref, w_ref, b_ref, o_ref, *, pre_silu):
    x = x_ref[...].astype(F32)
    if pre_silu:
        x = _silu(x)
    o_ref[...] = _dot(x.astype(BF16), w_ref[...]) + b_ref[...]


def _mm(x, w, b, *, tm, tn, pre_silu=False, name="mm"):
    m, k = x.shape
    n = w.shape[1]
    return pl.pallas_call(
        functools.partial(_mm_kernel, pre_silu=pre_silu),
        grid=(m // tm, n // tn),
        in_specs=[pl.BlockSpec((tm, k), lambda i, j: (i, 0)),
                  pl.BlockSpec((k, tn), lambda i, j: (0, j)),
                  pl.BlockSpec((1, tn), lambda i, j: (0, j))],
        out_specs=pl.BlockSpec((tm, tn), lambda i, j: (i, j)),
        out_shape=jax.ShapeDtypeStruct((m, n), F32),
        compiler_params=_cparams(("parallel", "arbitrary")),
        name=name,
    )(x, w, b)


def _inproj_kernel(h_ref, g_ref, sh_ref, sc_ref, w_ref, o_ref):
    x = _rms(h_ref[...], g_ref[...], NORM_EPS)
    xn = (x * (1.0 + sc_ref[0]) + sh_ref[0]).astype(BF16)
    o_ref[...] = _dot(xn, w_ref[...])


def _inproj(h, g, shift, scale, w_ext, *, nb, tpb):
    n = h.shape[0]
    tn = 1024
    modmap = lambda j, b, r: (jnp.where(r == 0, nb, b), 0, 0)
    return pl.pallas_call(
        _inproj_kernel,
        grid=(N_EXT // tn, nb, tpb),
        in_specs=[pl.BlockSpec((TM, D_MODEL), lambda j, b, r: (b * tpb + r, 0)),
                  pl.BlockSpec((1, D_MODEL), lambda j, b, r: (0, 0)),
                  pl.BlockSpec((1, 1, D_MODEL), modmap),
                  pl.BlockSpec((1, 1, D_MODEL), modmap),
                  pl.BlockSpec((D_MODEL, tn), lambda j, b, r: (0, j))],
        out_specs=pl.BlockSpec((TM, tn), lambda j, b, r: (b * tpb + r, j)),
        out_shape=jax.ShapeDtypeStruct((n, N_EXT), F32),
        compiler_params=_cparams(("parallel", "parallel", "parallel")),
        name="inproj",
    )(h, g, shift, scale, w_ext)


def _mla_prep_kernel(p_ref, gq_ref, gkv_ref, wa_ref, wb_ref, wk_ref, wv_ref, pk_ref,
                     cq_ref, sq_ref, ck_ref, q_ref, k_ref, v_ref):
    seg = p_ref[...]
    nq = _rms(seg[:, :MLA_Q_LORA], gq_ref[...], NORM_EPS).astype(BF16)
    nkv = _rms(seg[:, MLA_Q_LORA:MLA_Q_LORA + MLA_KV_LORA], gkv_ref[...], NORM_EPS).astype(BF16)
    cq = jnp.concatenate([cq_ref[...]] * MLA_HEADS, axis=1)
    sq = jnp.concatenate([sq_ref[...]] * MLA_HEADS, axis=1)
    q = _dot(nq, wa_ref[...]) * cq + _dot(nq, wb_ref[...]) * sq
    q_ref[...] = q.astype(BF16)
    kr = (seg[:, 384:512] * ck_ref[...]).astype(BF16)
    k_ref[...] = (_dot(nkv, wk_ref[...]) + _dot(kr, pk_ref[...])).astype(BF16)
    v_ref[...] = _dot(nkv, wv_ref[...]).astype(BF16)


def _mla_prep(p, gq, gkv, wa, wb, wk, wv, pk, cq_tab, sq_tab, ck_tab, *, nb, tpb):
    n = p.shape[0]
    tok = lambda b, r: (b * tpb + r, 0)
    pos = lambda b, r: (r, 0)
    full = lambda b, r: (0, 0)
    return pl.pallas_call(
        _mla_prep_kernel,
        grid=(nb, tpb),
        in_specs=[pl.BlockSpec((TM, 512), lambda b, r: (b * tpb + r, EXT_MLA // 512)),
                  pl.BlockSpec((1, MLA_Q_LORA), full),
                  pl.BlockSpec((1, MLA_KV_LORA), full),
                  pl.BlockSpec((MLA_Q_LORA, 512), full),
                  pl.BlockSpec((MLA_Q_LORA, 512), full),
                  pl.BlockSpec((MLA_KV_LORA, 512), full),
                  pl.BlockSpec((MLA_KV_LORA, 256), full),
                  pl.BlockSpec((128, 512), full),
                  pl.BlockSpec((TM, 128), pos),
                  pl.BlockSpec((TM, 128), pos),
                  pl.BlockSpec((TM, 128), pos)],
        out_specs=[pl.BlockSpec((TM, 512), tok),
                   pl.BlockSpec((TM, 512), tok),
                   pl.BlockSpec((TM, 256), tok)],
        out_shape=[jax.ShapeDtypeStruct((n, 512), BF16),
                   jax.ShapeDtypeStruct((n, 512), BF16),
                   jax.ShapeDtypeStruct((n, 256), BF16)],
        compiler_params=_cparams(("parallel", "parallel")),
        name="mla_prep",
    )(p, gq, gkv, wa, wb, wk, wv, pk, cq_tab, sq_tab, ck_tab)


def _flash_mla_kernel(q_ref, k_ref, v_ref, o_ref, m_ref, l_ref, acc_ref):
    kk = pl.program_id(2)
    tq = q_ref.shape[0]

    @pl.when(kk == 0)
    def _():
        m_ref[...] = jnp.full(m_ref.shape, -jnp.inf, F32)
        l_ref[...] = jnp.zeros(l_ref.shape, F32)
        acc_ref[...] = jnp.zeros(acc_ref.shape, F32)

    ps, alphas = [], []
    for h in range(MLA_HEADS):
        s = _dot_nt(q_ref[:, h * 128:(h + 1) * 128], k_ref[:, h * 128:(h + 1) * 128])
        m_prev = m_ref[h]
        m_new = jnp.maximum(m_prev, jnp.max(s, axis=-1, keepdims=True))
        alpha = jnp.exp2(m_prev - m_new)
        p = jnp.exp2(s - m_new)
        l_ref[h] = alpha * l_ref[h] + jnp.sum(p, axis=-1, keepdims=True)
        m_ref[h] = m_new
        alphas.append(alpha)
        ps.append(p.astype(BF16))
    pv = _dot(jnp.concatenate(ps, axis=0), v_ref[...])
    acc_ref[...] = jnp.concatenate(alphas, axis=0) * acc_ref[...] + pv

    @pl.when(kk == pl.num_programs(2) - 1)
    def _():
        lane = lax.broadcasted_iota(jnp.int32, (tq, MLA_HEADS * MLA_V), 1)
        out = jnp.zeros((tq, MLA_HEADS * MLA_V), F32)
        for h in range(MLA_HEADS):
            out = jnp.where(lane // MLA_V == h, acc_ref[h * tq:(h + 1) * tq, :] / l_ref[h], out)
        o_ref[...] = out


def _flash_mla(q, k, v, *, nb, t_all, q_start, q_len, k_len, tk):
    tq = TM
    tpb_q, tpb_k = t_all // tq, t_all // tk
    q0 = q_start // tq
    nq = q_len // tq
    return pl.pallas_call(
        _flash_mla_kernel,
        grid=(nb, nq, k_len // tk),
        in_specs=[pl.BlockSpec((tq, 512), lambda b, i, kk: (b * tpb_q + q0 + i, 0)),
                  pl.BlockSpec((tk, 512), lambda b, i, kk: (b * tpb_k + kk, 0)),
                  pl.BlockSpec((tk, 256), lambda b, i, kk: (b * tpb_k + kk, 0))],
        out_specs=pl.BlockSpec((tq, 256), lambda b, i, kk: (b * nq + i, 0)),
        out_shape=jax.ShapeDtypeStruct((nb * q_len, 256), F32),
        scratch_shapes=[pltpu.VMEM((MLA_HEADS, tq, 1), F32), pltpu.VMEM((MLA_HEADS, tq, 1), F32),
                        pltpu.VMEM((MLA_HEADS * tq, 256), F32)],
        compiler_params=_cparams(("parallel", "parallel", "arbitrary")),
        name="flash_mla",
    )(q, k, v)


def _diff_prep(p, cos_tab, sin_tab, *, nb, tpb):
    n = p.shape[0]
    tok = lambda b, r: (b * tpb + r, 0)
    pos = lambda b, r: (r, 0)
    return pl.pallas_call(
        _diff_prep_kernel_cols,
        grid=(nb, tpb),
        in_specs=[pl.BlockSpec((TM, 256), lambda b, r, c=c: (b * tpb + r, EXT_DIFF // 256 + c))
                  for c in range(5)]
                 + [pl.BlockSpec((TM, 256), pos), pl.BlockSpec((TM, 256), pos)],
        out_specs=[pl.BlockSpec((TM, 256), tok)] * 3,
        out_shape=[jax.ShapeDtypeStruct((n, 256), BF16)] * 3,
        compiler_params=_cparams(("parallel", "parallel")),
        name="diff_prep",
    )(p, p, p, p, p, cos_tab, sin_tab)


def _diff_prep_kernel_cols(q_in, k_in, v_in, qr_in, kr_in, cos_ref, sin_ref, q_ref, k_ref, v_ref):
    cos, sin = cos_ref[...], sin_ref[...]
    scale = DIFF_HD ** -0.5 * LOG2E
    q_ref[...] = ((q_in[...] * cos + qr_in[...] * sin) * scale).astype(BF16)
    k_ref[...] = (k_in[...] * cos + kr_in[...] * sin).astype(BF16)
    v_ref[...] = v_in[...].astype(BF16)


def _flash_diff_kernel(q_ref, k_ref, v_ref, lam_ref, g_ref, o_ref, qs_ref, m_ref, l_ref, acc_ref, *, lam_init):
    kk = pl.program_id(2)
    tq = q_ref.shape[0]
    nsm = 2 * DIFF_HEADS

    @pl.when(kk == 0)
    def _():
        m_ref[...] = jnp.full(m_ref.shape, -jnp.inf, F32)
        l_ref[...] = jnp.zeros(l_ref.shape, F32)
        acc_ref[...] = jnp.zeros(acc_ref.shape, F32)
        q = q_ref[...]
        lane = lax.broadcasted_iota(jnp.int32, (tq, 256), 1)
        for i in range(nsm):
            qs_ref[i * tq:(i + 1) * tq, :] = jnp.where((lane // DIFF_HD) == i, q, jnp.zeros_like(q))

    k = k_ref[...]
    ps, alphas = [], []
    for i in range(nsm):
        rows = slice(i * tq, (i + 1) * tq)
        s = _dot_nt(qs_ref[rows, :], k)
        m_prev = m_ref[rows, :]
        m_new = jnp.maximum(m_prev, jnp.max(s, axis=-1, keepdims=True))
        alpha = jnp.exp2(m_prev - m_new)
        p = jnp.exp2(s - m_new)
        l_ref[rows, :] = alpha * l_ref[rows, :] + jnp.sum(p, axis=-1, keepdims=True)
        m_ref[rows, :] = m_new
        alphas.append(alpha)
        ps.append(p.astype(BF16))
    pv = _dot(jnp.concatenate(ps, axis=0), v_ref[...])
    acc_ref[...] = jnp.concatenate(alphas, axis=0) * acc_ref[...] + pv

    @pl.when(kk == pl.num_programs(2) - 1)
    def _():
        lane = lax.broadcasted_iota(jnp.int32, (tq, 256), 1)
        o = jnp.zeros((tq, 256), F32)
        for h in range(DIFF_HEADS):
            r0, r1 = (2 * h) * tq, (2 * h + 1) * tq
            o0 = acc_ref[r0:r0 + tq, :] / l_ref[r0:r0 + tq, :]
            o1 = acc_ref[r1:r1 + tq, :] / l_ref[r1:r1 + tq, :]
            o = jnp.where((lane // (2 * DIFF_HD)) == h, o0 - lam_ref[...] * o1, o)
        ms = _split_dot(o * o, _block_ones(256, 2 * DIFF_HD)) * (1.0 / (2 * DIFF_HD))
        o_ref[...] = o * lax.rsqrt(ms + DIFF_EPS) * g_ref[...] * (1.0 - lam_init)


def _flash_diff(q, k, v, lam_row, g_row, *, lam_init, nb, t_all, q_start, q_len, k_len, tk):
    tq = TM
    tpb_q, tpb_k = t_all // tq, t_all // tk
    q0 = q_start // tq
    nq = q_len // tq
    return pl.pallas_call(
        functools.partial(_flash_diff_kernel, lam_init=lam_init),
        grid=(nb, nq, k_len // tk),
        in_specs=[pl.BlockSpec((tq, 256), lambda b, i, kk: (b * tpb_q + q0 + i, 0)),
                  pl.BlockSpec((tk, 256), lambda b, i, kk: (b * tpb_k + kk, 0)),
                  pl.BlockSpec((tk, 256), lambda b, i, kk: (b * tpb_k + kk, 0)),
                  pl.BlockSpec((1, 256), lambda b, i, kk: (0, 0)),
                  pl.BlockSpec((1, 256), lambda b, i, kk: (0, 0))],
        out_specs=pl.BlockSpec((tq, 256), lambda b, i, kk: (b * nq + i, 0)),
        out_shape=jax.ShapeDtypeStruct((nb * q_len, 256), F32),
        scratch_shapes=[pltpu.VMEM((2 * DIFF_HEADS * tq, 256), BF16),
                        pltpu.VMEM((2 * DIFF_HEADS * tq, 1), F32),
                        pltpu.VMEM((2 * DIFF_HEADS * tq, 1), F32),
                        pltpu.VMEM((2 * DIFF_HEADS * tq, 256), F32)],
        compiler_params=_cparams(("parallel", "parallel", "arbitrary")),
        name="flash_diff",
    )(q, k, v, lam_row, g_row)


def _rw_prep_kernel(p_ref, prev_ref, next_ref, mu_ref, wlo_ref, g2_ref, vec_ref,
                    r_ref, v_ref, kk_ref, w0_ref, k0_ref, b0_ref, w1_ref, k1_ref, b1_ref,
                    bonus_ref, gate_ref, *, r0, lat_last):
    r = pl.program_id(1) + r0
    p = p_ref[...]
    row = lax.broadcasted_iota(jnp.int32, p.shape, 0)
    first_tile = jnp.logical_or(r == 0, r == 1)
    last_tile = jnp.logical_or(r == 0, r == lat_last)
    prev_row = jnp.where(first_tile, 0.0, prev_ref[7:8, :])
    next_row = jnp.where(last_tile, 0.0, next_ref[0:1, :])
    up = jnp.where(row == 0, prev_row, pltpu.roll(p, 1, 0))
    dn = jnp.where(row == TM - 1, next_row, pltpu.roll(p, TM - 1, 0))
    z = p + (0.5 * (up + dn) - p) * mu_ref[...]

    hw = RW_HEADS * RW_HEAD
    rr, k, v = z[:, :hw], z[:, hw:2 * hw], z[:, 2 * hw:3 * hw]
    lo = z[:, 3 * hw:3 * hw + 128]
    lane = lax.broadcasted_iota(jnp.int32, lo.shape, 1)
    lo = jnp.where(lane < 64, jnp.tanh(lo), lo).astype(BF16)
    wa = _dot(lo, wlo_ref[...])
    gate_ref[...] = _dot(_sigmoid(z[:, 3 * hw + 128:]).astype(BF16), g2_ref[...])

    e4 = _block_ones(hw, RW_HEAD)
    k_k, k_a, r_k = vec_ref[0:1, :], vec_ref[1:2, :], vec_ref[2:3, :]
    kk = k * k_k
    nrm = jnp.maximum(jnp.sqrt(_split_dot(kk * kk, e4)), 1e-12)
    kk = kk / nrm
    r_ref[...] = rr
    v_ref[...] = v
    kk_ref[...] = kk
    ksum = jnp.zeros_like(k)
    for d, (w_ref, kd_ref, b_ref) in enumerate(((w0_ref, k0_ref, b0_ref), (w1_ref, k1_ref, b1_ref))):
        w0 = vec_ref[3 + d:4 + d, :]
        a0 = vec_ref[5 + d:6 + d, :]
        wd = -_softplus(-(w0 + wa[:, d * hw:(d + 1) * hw])) - 0.5
        w_ref[...] = jnp.exp(-jnp.exp(wd))
        ad = _sigmoid(a0 + wa[:, (2 + d) * hw:(3 + d) * hw])
        kd = k * (1.0 + (ad - 1.0) * k_a)
        kd_ref[...] = kd
        b_ref[...] = kk * ad
        ksum = ksum + kd
    bonus_ref[...] = _split_dot(rr * ksum * r_k, e4) * v


def _rw_prep(p, mu, wlo, g2, vecs, *, nb, tpb, r0, nt):
    n_out = nb * nt * TM
    hw = RW_HEADS * RW_HEAD
    n_rows8 = p.shape[0] // 8
    tok = lambda b, r: (b * tpb + r0 + r, EXT_RW // 1024)
    prev = lambda b, r: (jnp.maximum((b * tpb + r0 + r) * (TM // 8) - 1, 0), EXT_RW // 1024)
    nxt = lambda b, r: (jnp.minimum((b * tpb + r0 + r + 1) * (TM // 8), n_rows8 - 1), EXT_RW // 1024)
    out = lambda b, r: (b * nt + r, 0)
    full = lambda b, r: (0, 0)
    return pl.pallas_call(
        functools.partial(_rw_prep_kernel, r0=r0, lat_last=tpb - 1),
        grid=(nb, nt),
        in_specs=[pl.BlockSpec((TM, 1024), tok),
                  pl.BlockSpec((8, 1024), prev),
                  pl.BlockSpec((8, 1024), nxt),
                  pl.BlockSpec((1, 1024), full),
                  pl.BlockSpec((128, 4 * hw), full),
                  pl.BlockSpec((128, hw), full),
                  pl.BlockSpec((8, hw), full)],
        out_specs=[pl.BlockSpec((TM, hw), out)] * 11,
        out_shape=[jax.ShapeDtypeStruct((n_out, hw), F32)] * 11,
        compiler_params=_cparams(("parallel", "parallel")),
        name="rw_prep",
    )(p, p, p, mu, wlo, g2, vecs)


def _rw_scan_kernel(rf, vf, kkf, wf, kf, bf, rb, vb, kkb, wb, kb, bb, yf_ref, yb_ref, s_ref, *, nb):
    c = RW_CHUNK

    @pl.when(pl.program_id(0) == 0)
    def _():
        s_ref[...] = jnp.zeros(s_ref.shape, F32)

    e2 = _block_ones(128, RW_HEAD)
    z2 = jnp.zeros((128, 128), BF16)
    left, right = jnp.concatenate([e2, z2], axis=1), jnp.concatenate([z2, e2], axis=1)
    rhs_pair1 = jnp.concatenate([left, right], axis=0)
    rhs_pair2 = jnp.concatenate([left, left, right, right], axis=0)
    e22 = jnp.concatenate([e2, e2], axis=0)
    lane = lax.broadcasted_iota(jnp.int32, (RW_HEAD, 128), 1)
    sub = lax.broadcasted_iota(jnp.int32, (RW_HEAD, 128), 0)
    diag = (lane % RW_HEAD) == sub
    sub8 = lax.broadcasted_iota(jnp.int32, (8, 128), 0)
    dirs = ((rf, vf, kkf, wf, kf, bf, yf_ref), (rb, vb, kkb, wb, kb, bb, yb_ref))

    def allreduce_rows(x):
        t = x[0:8]
        for i in range(1, 8):
            t = t + x[8 * i:8 * i + 8]
        for sh in (4, 2, 1):
            t = t + pltpu.roll(t, sh, 0)
        return t

    def col(row):
        return jnp.where(diag, row, 0.0).astype(BF16)

    def split(x):
        hi = x.astype(BF16).astype(F32)
        return hi, x - hi

    def group(g, carry):
        tiles, ytiles = {}, {}
        for d, refs in enumerate(dirs):
            base = pl.multiple_of((g if d == 0 else c // 8 - 1 - g) * 8, 8)
            if d == 0:
                prev = lambda x, sh: jnp.where(sub8 >= sh, pltpu.roll(x, sh, 0), 1.0)
            else:
                prev = lambda x, sh: jnp.where(sub8 < 8 - sh, pltpu.roll(x, 8 - sh, 0), 1.0)
            last = 7 if d == 0 else 0
            for b in range(nb):
                for hp in range(2):
                    r_, v_, kk_, w_, k_, b_ = [ref[b, pl.ds(base, 8), pl.ds(hp * 128, 128)] for ref in refs[:6]]
                    gam = w_
                    for sh in (1, 2, 4):
                        gam = gam * prev(gam, sh)
                    inv = 1.0 / gam
                    tiles[d, b, hp] = (base, v_, split(b_ * inv), split(kk_ * prev(gam, 1)), k_ * inv, r_ * gam,
                                       split(gam[last:last + 1, :]))
                    ytiles[d, b, hp] = jnp.zeros((8, 128), F32)
        units = [(d, b, hp) for d in range(2) for b in range(nb) for hp in range(2)]
        for jj in range(8):
            lhs2, lhs1 = [], []
            for (d, b, hp) in units:
                rw = slice(jj, jj + 1) if d == 0 else slice(7 - jj, 8 - jj)
                _, _, (bh, bm), (kkh, kkm), kh, rh, _ = tiles[d, b, hp]
                lhs2.append(jnp.concatenate([col(bh[rw]), col(bm[rw]), col(kkh[rw]), col(kkm[rw])], axis=1))
                lhs1.append(jnp.concatenate([col(kh[rw]), col(rh[rw])], axis=1))
            c2 = _dot(jnp.concatenate(lhs2, axis=0), rhs_pair2)
            c1 = _dot(jnp.concatenate(lhs1, axis=0), rhs_pair1)
            for u, (d, b, hp) in enumerate(units):
                j = jj if d == 0 else 7 - jj
                v_ = tiles[d, b, hp][1]
                rows = slice(u * RW_HEAD, (u + 1) * RW_HEAD)
                bc, kkc, kc, rc = c2[rows, :128], c2[rows, 128:], c1[rows, :128], c1[rows, 128:]
                s = s_ref[u]
                sa = jnp.concatenate([allreduce_rows(kkc * s)] * 8, axis=0)
                s = s - bc * sa + kc * v_[j:j + 1]
                s_ref[u] = s
                ytiles[d, b, hp] = jnp.where(sub8 == j, allreduce_rows(rc * s), ytiles[d, b, hp])
        for d, refs in enumerate(dirs):
            for b in range(nb):
                for hp in range(2):
                    u = (d * nb + b) * 2 + hp
                    gh, gm = tiles[d, b, hp][6]
                    s_ref[u] = s_ref[u] * _dot(jnp.concatenate([col(gh), col(gm)], axis=1), e22)
                    refs[6][b, pl.ds(tiles[d, b, hp][0], 8), pl.ds(hp * 128, 128)] = ytiles[d, b, hp]
        return carry

    lax.fori_loop(0, c // 8, group, 0)


def _rw_scan(r, v, kk, w0, k0, b0, w1, k1, b1, *, nb, t_all, n_ctx):
    c = RW_CHUNK
    nc, ncc = t_all // c, n_ctx // c
    hw = RW_HEADS * RW_HEAD
    shp = lambda a: a.reshape(nb, t_all, hw)
    fwd = lambda j: (0, j, 0)
    bwd = lambda j: (0, jnp.where(j < ncc, ncc - 1 - j, nc - 1 - (j - ncc)), 0)
    blk = (nb, c, hw)
    yshape = jax.ShapeDtypeStruct((nb, t_all, hw), F32)
    yf, yb = pl.pallas_call(
        functools.partial(_rw_scan_kernel, nb=nb),
        grid=(nc,),
        in_specs=[pl.BlockSpec(blk, fwd)] * 6 + [pl.BlockSpec(blk, bwd)] * 6,
        out_specs=[pl.BlockSpec(blk, fwd), pl.BlockSpec(blk, bwd)],
        out_shape=[yshape, yshape],
        scratch_shapes=[pltpu.VMEM((2 * nb * 2, RW_HEAD, 128), F32)],
        compiler_params=_cparams(("arbitrary",)),
        name="rw_scan",
    )(shp(r), shp(v), shp(kk), shp(w0), shp(k0), shp(b0),
      shp(r), shp(v), shp(kk), shp(w1), shp(k1), shp(b1))

    return yf.reshape(nb * t_all, hw), yb.reshape(nb * t_all, hw)


def _rw_post_kernel(yf_ref, yb_ref, bonus_ref, gate_ref, vec_ref, o_ref):
    e4 = _block_ones(RW_HEADS * RW_HEAD, RW_HEAD)
    y = yf_ref[...] + yb_ref[...]
    mean = _split_dot(y, e4) * (1.0 / RW_HEAD)
    yc = y - mean
    var = _split_dot(yc * yc, e4) * (1.0 / RW_HEAD)
    yn = yc * lax.rsqrt(var + RW_LN_EPS) * vec_ref[0:1, :] + vec_ref[1:2, :]
    o_ref[...] = (yn + bonus_ref[...]) * gate_ref[...]


def _rw_post(yf, yb, bonus, gate, vecs, *, nb, tpb_y, r0_y, nt):
    hw = RW_HEADS * RW_HEAD
    n_out = bonus.shape[0]
    ytok = lambda b, r: (b * tpb_y + r0_y + r, 0)
    tok = lambda b, r: (b * nt + r, 0)
    return pl.pallas_call(
        _rw_post_kernel,
        grid=(nb, nt),
        in_specs=[pl.BlockSpec((TM, hw), ytok), pl.BlockSpec((TM, hw), ytok),
                  pl.BlockSpec((TM, hw), tok), pl.BlockSpec((TM, hw), tok),
                  pl.BlockSpec((8, hw), lambda b, r: (0, 0))],
        out_specs=pl.BlockSpec((TM, hw), tok),
        out_shape=jax.ShapeDtypeStruct((n_out, hw), F32),
        compiler_params=_cparams(("parallel", "parallel")),
        name="rw_post",
    )(yf, yb, bonus, gate, vecs)


def _s5_bu_kernel(p_ref, w_ref, o_ref):
    o_ref[...] = _dot(p_ref[...].astype(BF16), w_ref[...])


def _s5_bu(p, w_b, *, nb, tpb):
    n = p.shape[0]
    nst = S5_GROUPS * S5_STATE
    return pl.pallas_call(
        _s5_bu_kernel,
        grid=(nb * tpb, 4),
        in_specs=[pl.BlockSpec((TM, 256), lambda i, j: (i, EXT_S5 // 256)),
                  pl.BlockSpec((256, nst), lambda i, j: (0, j))],
        out_specs=pl.BlockSpec((TM, nst), lambda i, j: (i, j)),
        out_shape=jax.ShapeDtypeStruct((n, 4 * nst), F32),
        compiler_params=_cparams(("parallel", "arbitrary")),
        name="s5_bu",
    )(p, w_b)


def _s5_scan_kernel(buf_ref, bub_ref, ab_ref, xf_ref, xb_ref, st_ref, *, nb):
    c = S5_CHUNK
    nst = S5_GROUPS * S5_STATE

    @pl.when(pl.program_id(0) == 0)
    def _():
        st_ref[...] = jnp.zeros(st_ref.shape, F32)

    dirs = ((buf_ref, xf_ref), (bub_ref, xb_ref))

    def group(g, carry):
        for d, (bu_ref, x_ref) in enumerate(dirs):
            base = pl.multiple_of((g if d == 0 else c // 8 - 1 - g) * 8, 8)
            ar = ab_ref[d, :, 0:nst]
            ai = ab_ref[d, :, nst:2 * nst]
            for b in range(nb):
                u = d * nb + b
                xr = st_ref[u, :, 0:nst]
                xi = st_ref[u, :, nst:2 * nst]
                bur = bu_ref[b, pl.ds(base, 8), 0:nst]
                bui = bu_ref[b, pl.ds(base, 8), nst:2 * nst]
                rows_r, rows_i = [None] * 8, [None] * 8
                for jj in range(8):
                    j = jj if d == 0 else 7 - jj
                    xr, xi = (ar * xr - ai * xi + bur[j:j + 1, :], ar * xi + ai * xr + bui[j:j + 1, :])
                    rows_r[j], rows_i[j] = xr, xi
                st_ref[u, :, 0:nst] = xr
                st_ref[u, :, nst:2 * nst] = xi
                x_ref[b, pl.ds(base, 8), 0:nst] = jnp.concatenate(rows_r, axis=0)
                x_ref[b, pl.ds(base, 8), nst:2 * nst] = jnp.concatenate(rows_i, axis=0)
        return carry

    lax.fori_loop(0, c // 8, group, 0)


def _s5_scan(bu, ab, *, nb, t_all, n_ctx):
    c = S5_CHUNK
    nc, ncc = t_all // c, n_ctx // c
    nst2 = 2 * S5_GROUPS * S5_STATE
    bu3 = bu.reshape(nb, t_all, 2 * nst2)
    fwd = lambda j: j
    bwd = lambda j: jnp.where(j < ncc, ncc - 1 - j, nc - 1 - (j - ncc))
    xshape = jax.ShapeDtypeStruct((nb, t_all, nst2), F32)
    xf, xb = pl.pallas_call(
        functools.partial(_s5_scan_kernel, nb=nb),
        grid=(nc,),
        in_specs=[pl.BlockSpec((nb, c, nst2), lambda j: (0, fwd(j), 0)),
                  pl.BlockSpec((nb, c, nst2), lambda j: (0, bwd(j), 1)),
                  pl.BlockSpec((2, 1, nst2), lambda j: (0, 0, 0))],
        out_specs=[pl.BlockSpec((nb, c, nst2), lambda j: (0, fwd(j), 0)),
                   pl.BlockSpec((nb, c, nst2), lambda j: (0, bwd(j), 0))],
        out_shape=[xshape, xshape],
        scratch_shapes=[pltpu.VMEM((2 * nb, 1, nst2), F32)],
        compiler_params=_cparams(("arbitrary",)),
        name="s5_scan",
    )(bu3, bu3, ab)
    return xf.reshape(nb * t_all, nst2), xb.reshape(nb * t_all, nst2)


def _s5_post_kernel(xf_ref, xb_ref, u_ref, cf_ref, cb_ref, d_ref, gw_ref, gb_ref, o_ref):
    y = (_dot(xf_ref[...].astype(BF16), cf_ref[...]) + _dot(xb_ref[...].astype(BF16), cb_ref[...])
         + d_ref[...] * u_ref[...])
    zg = 0.5 * y * (1.0 + jnp.tanh(math.sqrt(2.0 / math.pi) * (y + 0.044715 * (y * y * y))))
    o_ref[...] = zg * _sigmoid(_dot(zg.astype(BF16), gw_ref[...]) + gb_ref[...])


def _s5_post(xf, xb, p, cf, cb, d_row, glu_w, glu_b, *, nb, tpb, r0, nt):
    nst2 = 2 * S5_GROUPS * S5_STATE
    tok = lambda b, r: (b * tpb + r0 + r, 0)
    full = lambda b, r: (0, 0)
    return pl.pallas_call(
        _s5_post_kernel,
        grid=(nb, nt),
        in_specs=[pl.BlockSpec((TM, nst2), tok), pl.BlockSpec((TM, nst2), tok),
                  pl.BlockSpec((TM, 256), lambda b, r: (b * tpb + r0 + r, EXT_S5 // 256)),
                  pl.BlockSpec((nst2, 256), full), pl.BlockSpec((nst2, 256), full),
                  pl.BlockSpec((1, 256), full), pl.BlockSpec((256, 256), full),
                  pl.BlockSpec((1, 256), full)],
        out_specs=pl.BlockSpec((TM, 256), lambda b, r: (b * nt + r, 0)),
        out_shape=jax.ShapeDtypeStruct((nb * nt * TM, 256), F32),
        compiler_params=_cparams(("parallel", "parallel")),
        name="s5_post",
    )(xf, xb, p, cf, cb, d_row, glu_w, glu_b)


def _merge_kernel(ya_ref, yb_ref, ys_ref, yd_ref, g0, g1, g2, g3, h_ref, m2_ref, wb_ref, wo_ref, o_ref):
    acc = None
    for n, (y_ref, g_ref) in enumerate(((ya_ref, g0), (yb_ref, g1), (ys_ref, g2), (yd_ref, g3))):
        term = _sigmoid(g_ref[...]) * _dot(y_ref[...].astype(BF16), wb_ref[n])
        acc = term if acc is None else acc + term
    o_ref[...] = h_ref[...] + m2_ref[0] * _dot(acc.astype(BF16), wo_ref[...])


def _merge(ya, yb, ys, yd, p, h, mod2, w_branch, w_out, *, nb, tpb, r0, nt):
    tok_in = lambda b, r: (b * tpb + r0 + r, 0)
    tok_out = lambda b, r: (b * nt + r, 0)
    full2 = lambda b, r: (0, 0)
    gate = [pl.BlockSpec((TM, D_MODEL), lambda b, r, c=c: (b * tpb + r0 + r, EXT_GATE // D_MODEL + c))
            for c in range(4)]
    return pl.pallas_call(
        _merge_kernel,
        grid=(nb, nt),
        in_specs=[pl.BlockSpec((TM, 256), tok_out)] * 4 + gate
                 + [pl.BlockSpec((TM, D_MODEL), tok_in),
                    pl.BlockSpec((1, 1, D_MODEL), lambda b, r: (jnp.where(r0 + r == 0, nb, b), 0, 0)),
                    pl.BlockSpec((4, 256, D_MODEL), lambda b, r: (0, 0, 0)),
                    pl.BlockSpec((D_MODEL, D_MODEL), full2)],
        out_specs=pl.BlockSpec((TM, D_MODEL), tok_out),
        out_shape=jax.ShapeDtypeStruct((nb * nt * TM, D_MODEL), F32),
        compiler_params=_cparams(("parallel", "parallel")),
        name="merge",
    )(ya, yb, ys, yd, p, p, p, p, h, mod2, w_branch, w_out)


def _router_kernel(h_ref, g_ref, sh_ref, sc_ref, wh_ref, wm_ref, b_ref, x_ref, lg_ref):
    x = _rms(h_ref[...], g_ref[...], NORM_EPS) * (1.0 + sc_ref[0]) + sh_ref[0]
    xh = x.astype(BF16)
    xm = (x - xh.astype(F32)).astype(BF16)
    x_ref[...] = xh
    lg = (_dot(xh, wh_ref[...]) + _dot(xm, wh_ref[...]) + _dot(xh, wm_ref[...])) + b_ref[...]

    lane = lax.broadcasted_iota(jnp.int32, lg.shape, 1)
    lanef = lane.astype(F32)
    neg = jnp.float32(-jnp.inf)
    big = jnp.float32(1e9)
    rmax = lambda v: jnp.max(v, axis=-1, keepdims=True)
    rmin = lambda v: jnp.min(v, axis=-1, keepdims=True)
    rsum = lambda v: jnp.sum(v, axis=-1, keepdims=True)

    gmask = lane < MOE_GROUPS
    mg = rmax(jnp.where(gmask, lg, neg))
    eg = jnp.where(gmask, jnp.exp(lg - mg), 0.0)
    pg = eg / rsum(eg)
    pg_top = rmax(pg)
    g_sel = rmin(jnp.where(jnp.logical_and(gmask, pg == pg_top), lanef, big))
    lo = MOE_GROUPS + MOE_PER_GROUP * g_sel
    emask = jnp.logical_and(lanef >= lo, lanef < lo + MOE_PER_GROUP)
    me = rmax(jnp.where(emask, lg, neg))
    ee = jnp.where(emask, jnp.exp(lg - me), 0.0)
    pe = jnp.where(emask, ee / rsum(ee), -1.0)
    p1 = rmax(pe)
    i1 = rmin(jnp.where(pe == p1, lanef, big))
    pe2 = jnp.where(lanef == i1, -1.0, pe)
    p2 = rmax(pe2)
    i2 = rmin(jnp.where(pe2 == p2, lanef, big))
    den = p1 + p2
    out = jnp.where(lane == 0, i1 - MOE_GROUPS, 0.0)
    out = jnp.where(lane == 1, i2 - MOE_GROUPS, out)
    out = jnp.where(lane == 2, pg_top * p1 / den, out)
    out = jnp.where(lane == 3, pg_top * p2 / den, out)
    lg_ref[...] = out


def _router(h, g, shift, scale, wh, wm, bias, *, nb, nt, ctx_first):
    tok = lambda b, r: (b * nt + r, 0)
    full = lambda b, r: (0, 0)
    if ctx_first:
        modmap = lambda b, r: (jnp.where(r == 0, nb, b), 0, 0)
    else:
        modmap = lambda b, r: (b, 0, 0)
    n = h.shape[0]
    return pl.pallas_call(
        _router_kernel,
        grid=(nb, nt),
        in_specs=[pl.BlockSpec((TM, D_MODEL), tok), pl.BlockSpec((1, D_MODEL), full),
                  pl.BlockSpec((1, 1, D_MODEL), modmap), pl.BlockSpec((1, 1, D_MODEL), modmap),
                  pl.BlockSpec((D_MODEL, 128), full), pl.BlockSpec((D_MODEL, 128), full),
                  pl.BlockSpec((1, 128), full)],
        out_specs=[pl.BlockSpec((TM, D_MODEL), tok), pl.BlockSpec((TM, 128), tok)],
        out_shape=[jax.ShapeDtypeStruct((n, D_MODEL), BF16), jax.ShapeDtypeStruct((n, 128), F32)],
        compiler_params=_cparams(("parallel", "parallel")),
        name="router",
    )(h, g, shift, scale, wh, wm, bias)


def _expert_kernel(be_ref, nv_ref, x_ref, wg_ref, wu_ref, wd_ref, o_ref):
    @pl.when(pl.program_id(0) < nv_ref[0])
    def _():
        x = x_ref[...]
        hb = _silu(_dot(x, wg_ref[0])) * _dot(x, wu_ref[0])
        o_ref[...] = _dot(hb.astype(BF16), wd_ref[0])

    @pl.when(pl.program_id(0) >= nv_ref[0])
    def _():
        o_ref[...] = jnp.zeros(o_ref.shape, F32)


def _experts(xs, block_e, n_valid, w_gate, w_up, w_down):
    n_slots = xs.shape[0]
    n_blocks = n_slots // MOE_BLK
    wmap = lambda i, be, nv: (be[i], 0, 0)
    return pl.pallas_call(
        _expert_kernel,
        grid_spec=pltpu.PrefetchScalarGridSpec(
            num_scalar_prefetch=2,
            grid=(n_blocks,),
            in_specs=[pl.BlockSpec((MOE_BLK, D_MODEL), lambda i, be, nv: (i, 0)),
                      pl.BlockSpec((1, D_MODEL, D_EXPERT), wmap),
                      pl.BlockSpec((1, D_MODEL, D_EXPERT), wmap),
                      pl.BlockSpec((1, D_EXPERT, D_MODEL), wmap)],
            out_specs=pl.BlockSpec((MOE_BLK, D_MODEL), lambda i, be, nv: (i, 0))),
        out_shape=jax.ShapeDtypeStruct((n_slots, D_MODEL), F32),
        compiler_params=_cparams(("arbitrary",)),
        name="experts",
    )(block_e, n_valid, xs, w_gate, w_up, w_down)


def _combine_kernel(h_ref, y0_ref, y1_ref, w_ref, m5_ref, g_ref, o_ref, *, final):
    w = w_ref[...]
    y = y0_ref[...] * w[:, 0:1] + y1_ref[...] * w[:, 1:2]
    h = h_ref[...] + m5_ref[0] * y
    if final:
        h = _rms(h, g_ref[...], NORM_EPS)
    o_ref[...] = h


def _combine(h, y0, y1, wts, mod5, g_final, *, nb, nt, ctx_first, final):
    tok = lambda b, r: (b * nt + r, 0)
    if ctx_first:
        modmap = lambda b, r: (jnp.where(r == 0, nb, b), 0, 0)
    else:
        modmap = lambda b, r: (b, 0, 0)
    return pl.pallas_call(
        functools.partial(_combine_kernel, final=final),
        grid=(nb, nt),
        in_specs=[pl.BlockSpec((TM, D_MODEL), tok)] * 3
                 + [pl.BlockSpec((TM, 128), tok), pl.BlockSpec((1, 1, D_MODEL), modmap),
                    pl.BlockSpec((1, D_MODEL), lambda b, r: (0, 0))],
        out_specs=pl.BlockSpec((TM, D_MODEL), tok),
        out_shape=jax.ShapeDtypeStruct(h.shape, F32),
        compiler_params=_cparams(("parallel", "parallel")),
        name="combine",
    )(h, y0, y1, wts, mod5, g_final)


def _moe(h, g2, shift, scale, mod5, wh, wm, rbias, w_gate, w_up, w_down, g_final, *, nb, nt, ctx_first, final):
    n = h.shape[0]
    x_bf, route = _router(h, g2, shift, scale, wh, wm, rbias, nb=nb, nt=nt, ctx_first=ctx_first)
    idx = route[:, :MOE_TOPK].astype(jnp.int32)
    wts = route[:, MOE_TOPK:2 * MOE_TOPK]
    n_as = n * MOE_TOPK
    flat_e = idx.reshape(n_as)
    onehot = (flat_e[:, None] == jnp.arange(MOE_EXPERTS, dtype=jnp.int32)[None, :]).astype(jnp.int32)
    csum = jnp.cumsum(onehot, axis=0)
    counts = csum[-1]
    rank = jnp.sum(jnp.where(onehot > 0, csum - 1, 0), axis=1)
    padded = (counts + MOE_BLK - 1) // MOE_BLK * MOE_BLK
    pad_end = jnp.cumsum(padded)
    pad_start = pad_end - padded
    slot = pad_start[flat_e] + rank
    n_blocks = (n_as + MOE_EXPERTS * (MOE_BLK - 1) + MOE_BLK - 1) // MOE_BLK
    n_slots = n_blocks * MOE_BLK
    slot_tok = jnp.zeros((n_slots,), jnp.int32).at[slot].set(jnp.arange(n_as, dtype=jnp.int32) // MOE_TOPK)
    block_e = jnp.minimum(jnp.searchsorted(pad_end, jnp.arange(n_blocks, dtype=jnp.int32) * MOE_BLK,
                                           side='right'), MOE_EXPERTS - 1).astype(jnp.int32)
    n_valid = (pad_end[-1:] // MOE_BLK).astype(jnp.int32)
    xs = jnp.take(x_bf, slot_tok, axis=0)
    ys = _experts(xs, block_e, n_valid, w_gate, w_up, w_down)
    slot2 = slot.reshape(n, MOE_TOPK)
    y0 = jnp.take(ys, slot2[:, 0], axis=0)
    y1 = jnp.take(ys, slot2[:, 1], axis=0)
    wts_pad = jnp.pad(wts.astype(F32), ((0, 0), (0, 128 - MOE_TOPK)))
    return _combine(h, y0, y1, wts_pad, mod5, g_final, nb=nb, nt=nt, ctx_first=ctx_first, final=final)


_ROT_SRC = np.array(list(range(8, 16)) + list(range(0, 8)) + list(range(24, 32)) + list(range(16, 24)))
_ROT_SIGN = np.array([-1.0] * 8 + [1.0] * 8 + [-1.0] * 8 + [1.0] * 8, np.float32)


def _rot_cols(w):
    k = w.shape[-1] // ROPE_DIM
    src = np.concatenate([_ROT_SRC + ROPE_DIM * i for i in range(k)])
    sign = np.tile(_ROT_SIGN, k)
    return w[..., src] * sign


def _rope_tables(n_ctx, n_lat):
    rows = n_lat // GRID_W
    row = jnp.repeat(jnp.arange(rows, dtype=F32), GRID_W)
    col = (jnp.arange(rows * GRID_W) % GRID_W).astype(F32)
    nf = ROPE_DIM // 4
    inv = ROPE_BASE ** (-jnp.arange(nf, dtype=F32) / nf)
    ar = row[:, None] * inv
    ac = col[:, None] * inv
    ang = jnp.concatenate([ar, ar, ac, ac], axis=-1)
    cos = jnp.concatenate([jnp.ones((n_ctx, ROPE_DIM), F32), jnp.cos(ang)], axis=0)
    sin = jnp.concatenate([jnp.zeros((n_ctx, ROPE_DIM), F32), jnp.sin(ang)], axis=0)
    return cos, sin


def _block_diag(blocks):
    g, a, b = blocks.shape
    eye = jnp.eye(g, dtype=blocks.dtype)
    return (eye[:, None, :, None] * blocks[:, :, None, :]).reshape(g * a, g * b)


def _pick_tk(t_all):
    best = 128
    for tk in range(128, ATTN_TK_MAX + 1, 128):
        if t_all % tk == 0:
            best = tk
    return best


def kernel(x, c, ctx, c_ctx, w_mod, b_mod, norm1_g, norm2_g, w_in, mla_q_norm_g, mla_kv_norm_g, mla_w_uq, mla_w_ukv, rw_mu, rw_w0, rw_w2, rw_a0, rw_a2, rw_g2, rw_k_k, rw_k_a, rw_r_k, rw_lnx_g, rw_lnx_b, s5_a_re, s5_a_im, s5_log_dt, s5_b_re, s5_b_im, s5_c_re, s5_c_im, s5_d, s5_glu_w, s5_glu_b, diff_lq1, diff_lk1, diff_lq2, diff_lk2, diff_subln_g, w_branch, w_out, router_g_w, router_g_b, router_e_w, router_e_b, exp_w_gate, exp_w_up, exp_w_down, final_norm_g):
    nb, n_lat, d = x.shape
    n_ctx = ctx.shape[1]
    depth = w_mod.shape[0]
    t_all = n_ctx + n_lat
    assert d == D_MODEL and n_ctx == TM and n_lat % TM == 0
    tpb = t_all // TM
    tk = _pick_tk(t_all)
    hw = RW_HEADS * RW_HEAD

    cos, sin = _rope_tables(n_ctx, n_lat)
    mla_scale = (MLA_NOPE + MLA_ROPE) ** -0.5 * LOG2E
    z32 = jnp.zeros((t_all, 32), F32)
    cq_tab = jnp.concatenate([jnp.ones((t_all, 64), F32), cos, z32], axis=1) * mla_scale
    sq_tab = jnp.concatenate([jnp.zeros((t_all, 64), F32), sin, z32], axis=1) * mla_scale
    ck_tab = jnp.concatenate([cos, sin, jnp.zeros((t_all, 64), F32)], axis=1)
    dcos = jnp.tile(cos, (1, 8))
    dsin = jnp.tile(sin, (1, 8))

    c_rows = jnp.concatenate([c, c_ctx[None, :], jnp.zeros((8 - nb - 1, d), F32)], axis=0)

    h = jnp.concatenate([ctx, x], axis=1).reshape(nb * t_all, d)

    for l in range(depth):
        last = l == depth - 1
        r0, nt = (1, tpb - 1) if last else (0, tpb)

        mod = _mm(c_rows, w_mod[l].astype(BF16), b_mod[l][None, :], tm=8, tn=1536, pre_silu=True, name="mod")
        mods = [mod[:nb + 1, i * d:(i + 1) * d].reshape(nb + 1, 1, d) for i in range(6)]

        wi = w_in[l]
        o_rw, o_s5, o_df, o_gt = 416, 1440, 1696, 2464
        w_kr = wi[:, 384:416]
        w_dq, w_dk, w_dv = wi[:, o_df:o_df + 256], wi[:, o_df + 256:o_df + 512], wi[:, o_df + 512:o_df + 768]
        w_ext = jnp.concatenate(
            [wi[:, o_rw:o_s5],
             wi[:, :416], _rot_cols(w_kr), jnp.zeros((d, 64), F32),
             wi[:, o_s5:o_df],
             w_dq, w_dk, w_dv, _rot_cols(w_dq), _rot_cols(w_dk),
             wi[:, o_gt:]], axis=1).astype(BF16)
        p = _inproj(h, norm1_g[l][None, :], mods[0], mods[1], w_ext, nb=nb, tpb=tpb)

        wq = mla_w_uq[l].reshape(MLA_Q_LORA, MLA_HEADS, MLA_NOPE + MLA_ROPE)
        zq = jnp.zeros((MLA_Q_LORA, MLA_HEADS, 32), F32)
        wa = jnp.concatenate([wq, zq], axis=2).reshape(MLA_Q_LORA, 512).astype(BF16)
        wb = jnp.concatenate([jnp.zeros((MLA_Q_LORA, MLA_HEADS, 64), F32), _rot_cols(wq[:, :, MLA_NOPE:]), zq],
                             axis=2).reshape(MLA_Q_LORA, 512).astype(BF16)
        wkv = mla_w_ukv[l].reshape(MLA_KV_LORA, MLA_HEADS, MLA_NOPE + MLA_V)
        wk = jnp.concatenate([wkv[:, :, :MLA_NOPE], jnp.zeros((MLA_KV_LORA, MLA_HEADS, 64), F32)],
                             axis=2).reshape(MLA_KV_LORA, 512).astype(BF16)
        wv = wkv[:, :, MLA_NOPE:].reshape(MLA_KV_LORA, MLA_HEADS * MLA_V).astype(BF16)
        pk_np = np.zeros((128, 512), np.float32)
        for hh in range(MLA_HEADS):
            for i in range(32):
                pk_np[i, hh * 128 + 64 + i] = 1.0
                pk_np[32 + i, hh * 128 + 64 + i] = 1.0
        pk = jnp.asarray(pk_np, BF16)
        q_m, k_m, v_m = _mla_prep(p, mla_q_norm_g[l][None, :], mla_kv_norm_g[l][None, :], wa, wb, wk, wv, pk,
                                  cq_tab, sq_tab, ck_tab, nb=nb, tpb=tpb)
        ya_lat = _flash_mla(q_m, k_m, v_m, nb=nb, t_all=t_all, q_start=n_ctx, q_len=n_lat, k_len=t_all, tk=tk)

        q_d, k_d, v_d = _diff_prep(p, dcos, dsin, nb=nb, tpb=tpb)
        lam_init = 0.8 - 0.6 * math.exp(-0.3 * l)
        lam = (jnp.exp(jnp.sum(diff_lq1[l] * diff_lk1[l])) - jnp.exp(jnp.sum(diff_lq2[l] * diff_lk2[l])) + lam_init)
        lam_row = jnp.full((1, 256), lam, F32)
        g_row = jnp.tile(diff_subln_g[l], DIFF_HEADS)[None, :]
        yd_lat = _flash_diff(q_d, k_d, v_d, lam_row, g_row, lam_init=lam_init, nb=nb, t_all=t_all,
                             q_start=n_ctx, q_len=n_lat, k_len=t_all, tk=tk)
        if last:
            ya, yd = ya_lat, yd_lat
        else:
            ya_ctx = _flash_mla(q_m, k_m, v_m, nb=nb, t_all=t_all, q_start=0, q_len=n_ctx, k_len=n_ctx, tk=n_ctx)
            yd_ctx = _flash_diff(q_d, k_d, v_d, lam_row, g_row, lam_init=lam_init, nb=nb, t_all=t_all,
                                 q_start=0, q_len=n_ctx, k_len=n_ctx, tk=n_ctx)
            comb = lambda a_c, a_l: jnp.concatenate(
                [a_c.reshape(nb, n_ctx, -1), a_l.reshape(nb, n_lat, -1)], axis=1).reshape(nb * t_all, -1)
            ya, yd = comb(ya_ctx, ya_lat), comb(yd_ctx, yd_lat)

        wlo = jnp.zeros((128, 4 * hw), F32)
        wlo = wlo.at[:64, 0:hw].set(rw_w2[l, 0]).at[:64, hw:2 * hw].set(rw_w2[l, 1])
        wlo = wlo.at[64:, 2 * hw:3 * hw].set(rw_a2[l, 0]).at[64:, 3 * hw:].set(rw_a2[l, 1])
        vecs = jnp.stack([rw_k_k[l], rw_k_a[l], rw_r_k[l].reshape(hw), rw_w0[l, 0], rw_w0[l, 1],
                          rw_a0[l, 0], rw_a0[l, 1], jnp.zeros((hw,), F32)], axis=0)
        (r_, v_, kk_, w0_, k0_, b0_, w1_, k1_, b1_, bonus, gate_rw) = _rw_prep(
            p, rw_mu[l][None, :], wlo.astype(BF16), rw_g2[l].astype(BF16), vecs, nb=nb, tpb=tpb, r0=0, nt=tpb)
        yf, yb_ = _rw_scan(r_, v_, kk_, w0_, k0_, b0_, w1_, k1_, b1_, nb=nb, t_all=t_all, n_ctx=n_ctx)
        ln_vecs = jnp.concatenate([rw_lnx_g[l][None, :], rw_lnx_b[l][None, :], jnp.zeros((6, hw), F32)], axis=0)
        if last:
            trim = lambda a: a.reshape(nb, t_all, hw)[:, n_ctx:].reshape(nb * n_lat, hw)
            bonus, gate_rw = trim(bonus), trim(gate_rw)
        y_rw = _rw_post(yf, yb_, bonus, gate_rw, ln_vecs, nb=nb, tpb_y=tpb, r0_y=r0, nt=nt)

        bbs, abs_, cfs = [], [], []
        for dd in range(2):
            lr, li = s5_a_re[l, dd], s5_a_im[l, dd]
            dt = jnp.exp(s5_log_dt[l, dd])[:, None]
            mag = jnp.exp(lr * dt)
            ab_re, ab_im = mag * jnp.cos(li * dt), mag * jnp.sin(li * dt)
            den = lr * lr + li * li
            nr, ni = ab_re - 1.0, ab_im
            cf_re = (nr * lr + ni * li) / den
            cf_im = (ni * lr - nr * li) / den
            bre, bim = s5_b_re[l, dd], s5_b_im[l, dd]
            bb_re = cf_re[..., None] * bre - cf_im[..., None] * bim
            bb_im = cf_re[..., None] * bim + cf_im[..., None] * bre
            bbs.append(jnp.concatenate([_block_diag(bb_re.transpose(0, 2, 1)),
                                        _block_diag(bb_im.transpose(0, 2, 1))], axis=1))
            abs_.append(jnp.concatenate([ab_re.reshape(-1), ab_im.reshape(-1)])[None, :])
            cfs.append(jnp.concatenate([_block_diag(s5_c_re[l, dd].transpose(0, 2, 1)),
                                        -_block_diag(s5_c_im[l, dd].transpose(0, 2, 1))], axis=0))
        bu = _s5_bu(p, jnp.concatenate(bbs, axis=1).astype(BF16), nb=nb, tpb=tpb)
        xf, xb = _s5_scan(bu, jnp.stack(abs_, axis=0), nb=nb, t_all=t_all, n_ctx=n_ctx)
        y_s5 = _s5_post(xf, xb, p, cfs[0].astype(BF16), cfs[1].astype(BF16), s5_d[l].reshape(1, 256),
                        s5_glu_w[l].astype(BF16), s5_glu_b[l][None, :], nb=nb, tpb=tpb, r0=r0, nt=nt)

        h = _merge(ya, y_rw, y_s5, yd, p, h, mods[2], w_branch[l].astype(BF16), w_out[l].astype(BF16),
                   nb=nb, tpb=tpb, r0=r0, nt=nt)

        wr = jnp.concatenate([router_g_w[l], router_e_w[l], jnp.zeros((d, 128 - MOE_GROUPS - MOE_EXPERTS), F32)], axis=1)
        wr_h = wr.astype(BF16)
        wr_m = (wr - wr_h.astype(F32)).astype(BF16)
        rbias = jnp.concatenate([router_g_b[l], router_e_b[l],
                                 jnp.zeros((128 - MOE_GROUPS - MOE_EXPERTS,), F32)])[None, :]
        h = _moe(h, norm2_g[l][None, :], mods[3], mods[4], mods[5], wr_h, wr_m, rbias,
                 exp_w_gate[l].astype(BF16), exp_w_up[l].astype(BF16), exp_w_down[l].astype(BF16),
                 final_norm_g[None, :], nb=nb, nt=nt, ctx_first=not last, final=last)

    return h.reshape(nb, n_lat, d)
```

```python
import functools
import math

import jax
import jax.numpy as jnp
import numpy as np
from jax import lax
from jax.experimental import pallas as pl
from jax.experimental.pallas import tpu as pltpu

F32 = jnp.float32
BF16 = jnp.bfloat16

TM = 256
VMEM_LIMIT = 48 * 1024 * 1024

D_MODEL = 1024
GRID_W = 64
ROPE_DIM = 32
ROPE_BASE = 10000.0
NORM_EPS = 1e-6
MLA_HEADS, MLA_NOPE, MLA_ROPE, MLA_V = 4, 64, 32, 64
MLA_Q_LORA, MLA_KV_LORA = 256, 128
RW_HEADS, RW_HEAD = 4, 64
RW_LN_EPS = 64e-5
S5_GROUPS, S5_GROUP_CH, S5_STATE = 16, 16, 64
DIFF_HEADS, DIFF_HD = 4, 32
DIFF_EPS = 1e-5
MOE_GROUPS, MOE_PER_GROUP, MOE_TOPK = 4, 8, 2
MOE_EXPERTS = MOE_GROUPS * MOE_PER_GROUP
D_EXPERT = 512
MOE_BLK = 256
RW_CHUNK = 64
S5_CHUNK = 128
ATTN_TK_MAX = 2816
LOG2E = math.log2(math.e)

EXT_RW, EXT_MLA, EXT_S5, EXT_DIFF, EXT_GATE = 0, 1024, 1536, 1792, 3072
N_EXT = 7168


def _cparams(sem, vmem=VMEM_LIMIT):
    return pltpu.CompilerParams(dimension_semantics=sem, vmem_limit_bytes=vmem)


def _dot(a, b):
    return jnp.dot(a, b, preferred_element_type=F32)


def _dot_nt(a, b):
    return lax.dot_general(a, b, (((1,), (1,)), ((), ())), preferred_element_type=F32)


def _split_dot(x, e):
    hi = x.astype(BF16)
    mid = (x - hi.astype(F32)).astype(BF16)
    return _dot(hi, e) + _dot(mid, e)


def _block_ones(n, blk):
    r = lax.broadcasted_iota(jnp.int32, (n, n), 0) // blk
    c = lax.broadcasted_iota(jnp.int32, (n, n), 1) // blk
    return (r == c).astype(BF16)


def _sigmoid(x):
    return 1.0 / (1.0 + jnp.exp(-x))


def _silu(x):
    return x * _sigmoid(x)


def _softplus(x):
    return jnp.maximum(x, 0.0) + jnp.log(1.0 + jnp.exp(-jnp.abs(x)))


def _rms(x, g, eps):
    return x * lax.rsqrt(jnp.mean(x * x, axis=-1, keepdims=True) + eps) * g


def _mm_kernel(x_ref, w_ref, b_ref, o_ref, *, pre_silu):
    x = x_ref[...].astype(F32)
    if pre_silu:
        x = _silu(x)
    o_ref[...] = _dot(x.astype(BF16), w_ref[...]) + b_ref[...]


def _mm(x, w, b, *, tm, tn, pre_silu=False, name="mm"):
    m, k = x.shape
    n = w.shape[1]
    return pl.pallas_call(
        functools.partial(_mm_kernel, pre_silu=pre_silu),
        grid=(m // tm, n // tn),
        in_specs=[pl.BlockSpec((tm, k), lambda i, j: (i, 0)),
                  pl.BlockSpec((k, tn), lambda i, j: (0, j)),
                  pl.BlockSpec((1, tn), lambda i, j: (0, j))],
        out_specs=pl.BlockSpec((tm, tn), lambda i, j: (i, j)),
        out_shape=jax.ShapeDtypeStruct((m, n), F32),
        compiler_params=_cparams(("parallel", "arbitrary")),
        name=name,
    )(x, w, b)


def _inproj_kernel(h_ref, g_ref, sh_ref, sc_ref, w_ref, o_ref):
    x = _rms(h_ref[...], g_ref[...], NORM_EPS)
    xn = (x * (1.0 + sc_ref[0]) + sh_ref[0]).astype(BF16)
    o_ref[...] = _dot(xn, w_ref[...])


def _inproj(h, g, shift, scale, w_ext, *, nb, tpb):
    n = h.shape[0]
    tn = N_EXT // 2
    modmap = lambda j, b, r: (jnp.where(r == 0, nb, b), 0, 0)
    return pl.pallas_call(
        _inproj_kernel,
        grid=(N_EXT // tn, nb, tpb),
        in_specs=[pl.BlockSpec((TM, D_MODEL), lambda j, b, r: (b * tpb + r, 0)),
                  pl.BlockSpec((1, D_MODEL), lambda j, b, r: (0, 0)),
                  pl.BlockSpec((1, 1, D_MODEL), modmap),
                  pl.BlockSpec((1, 1, D_MODEL), modmap),
                  pl.BlockSpec((D_MODEL, tn), lambda j, b, r: (0, j))],
        out_specs=pl.BlockSpec((TM, tn), lambda j, b, r: (b * tpb + r, j)),
        out_shape=jax.ShapeDtypeStruct((n, N_EXT), F32),
        compiler_params=_cparams(("parallel", "parallel", "parallel")),
        name="inproj",
    )(h, g, shift, scale, w_ext)


def _mla_prep_kernel(p_ref, gq_ref, gkv_ref, wa_ref, wb_ref, wk_ref, wv_ref, pk_ref,
                     cq_ref, sq_ref, ck_ref, q_ref, k_ref, v_ref):
    seg = p_ref[...]
    nq = _rms(seg[:, :MLA_Q_LORA], gq_ref[...], NORM_EPS).astype(BF16)
    nkv = _rms(seg[:, MLA_Q_LORA:MLA_Q_LORA + MLA_KV_LORA], gkv_ref[...], NORM_EPS).astype(BF16)
    cq = jnp.concatenate([cq_ref[...]] * MLA_HEADS, axis=1)
    sq = jnp.concatenate([sq_ref[...]] * MLA_HEADS, axis=1)
    q = _dot(nq, wa_ref[...]) * cq + _dot(nq, wb_ref[...]) * sq
    q_ref[...] = q.astype(BF16)
    kr = (seg[:, 384:512] * ck_ref[...]).astype(BF16)
    k_ref[...] = (_dot(nkv, wk_ref[...]) + _dot(kr, pk_ref[...])).astype(BF16)
    v_ref[...] = _dot(nkv, wv_ref[...]).astype(BF16)


def _mla_prep(p, gq, gkv, wa, wb, wk, wv, pk, cq_tab, sq_tab, ck_tab, *, nb, tpb):
    n = p.shape[0]
    tok = lambda b, r: (b * tpb + r, 0)
    pos = lambda b, r: (r, 0)
    full = lambda b, r: (0, 0)
    return pl.pallas_call(
        _mla_prep_kernel,
        grid=(nb, tpb),
        in_specs=[pl.BlockSpec((TM, 512), lambda b, r: (b * tpb + r, EXT_MLA // 512)),
                  pl.BlockSpec((1, MLA_Q_LORA), full),
                  pl.BlockSpec((1, MLA_KV_LORA), full),
                  pl.BlockSpec((MLA_Q_LORA, 512), full),
                  pl.BlockSpec((MLA_Q_LORA, 512), full),
                  pl.BlockSpec((MLA_KV_LORA, 512), full),
                  pl.BlockSpec((MLA_KV_LORA, 256), full),
                  pl.BlockSpec((128, 512), full),
                  pl.BlockSpec((TM, 128), pos),
                  pl.BlockSpec((TM, 128), pos),
                  pl.BlockSpec((TM, 128), pos)],
        out_specs=[pl.BlockSpec((TM, 512), tok),
                   pl.BlockSpec((TM, 512), tok),
                   pl.BlockSpec((TM, 256), tok)],
        out_shape=[jax.ShapeDtypeStruct((n, 512), BF16),
                   jax.ShapeDtypeStruct((n, 512), BF16),
                   jax.ShapeDtypeStruct((n, 256), BF16)],
        compiler_params=_cparams(("parallel", "parallel")),
        name="mla_prep",
    )(p, gq, gkv, wa, wb, wk, wv, pk, cq_tab, sq_tab, ck_tab)


def _flash_mla_kernel(q_ref, k_ref, v_ref, o_ref, m_ref, l_ref, acc_ref):
    kk = pl.program_id(2)
    tq = q_ref.shape[0]

    @pl.when(kk == 0)
    def _():
        m_ref[...] = jnp.full(m_ref.shape, -jnp.inf, F32)
        l_ref[...] = jnp.zeros(l_ref.shape, F32)
        acc_ref[...] = jnp.zeros(acc_ref.shape, F32)

    ps, alphas = [], []
    for h in range(MLA_HEADS):
        s = _dot_nt(q_ref[:, h * 128:(h + 1) * 128], k_ref[:, h * 128:(h + 1) * 128])
        m_prev = m_ref[h]
        m_new = jnp.maximum(m_prev, jnp.max(s, axis=-1, keepdims=True))
        alpha = jnp.exp2(m_prev - m_new)
        p = jnp.exp2(s - m_new)
        l_ref[h] = alpha * l_ref[h] + jnp.sum(p, axis=-1, keepdims=True)
        m_ref[h] = m_new
        alphas.append(alpha)
        ps.append(p.astype(BF16))
    pv = _dot(jnp.concatenate(ps, axis=0), v_ref[...])
    acc_ref[...] = jnp.concatenate(alphas, axis=0) * acc_ref[...] + pv

    @pl.when(kk == pl.num_programs(2) - 1)
    def _():
        lane = lax.broadcasted_iota(jnp.int32, (tq, MLA_HEADS * MLA_V), 1)
        out = jnp.zeros((tq, MLA_HEADS * MLA_V), F32)
        for h in range(MLA_HEADS):
            out = jnp.where(lane // MLA_V == h, acc_ref[h * tq:(h + 1) * tq, :] / l_ref[h], out)
        o_ref[...] = out


def _flash_mla(q, k, v, *, nb, t_all, q_start, q_len, k_len, tk):
    tq = TM
    tpb_q, tpb_k = t_all // tq, t_all // tk
    q0 = q_start // tq
    nq = q_len // tq
    return pl.pallas_call(
        _flash_mla_kernel,
        grid=(nb, nq, k_len // tk),
        in_specs=[pl.BlockSpec((tq, 512), lambda b, i, kk: (b * tpb_q + q0 + i, 0)),
                  pl.BlockSpec((tk, 512), lambda b, i, kk: (b * tpb_k + kk, 0)),
                  pl.BlockSpec((tk, 256), lambda b, i, kk: (b * tpb_k + kk, 0))],
        out_specs=pl.BlockSpec((tq, 256), lambda b, i, kk: (b * nq + i, 0)),
        out_shape=jax.ShapeDtypeStruct((nb * q_len, 256), F32),
        scratch_shapes=[pltpu.VMEM((MLA_HEADS, tq, 1), F32), pltpu.VMEM((MLA_HEADS, tq, 1), F32),
                        pltpu.VMEM((MLA_HEADS * tq, 256), F32)],
        compiler_params=_cparams(("parallel", "parallel", "arbitrary")),
        name="flash_mla",
    )(q, k, v)


def _diff_prep(p, cos_tab, sin_tab, *, nb, tpb):
    n = p.shape[0]
    tok = lambda b, r: (b * tpb + r, 0)
    pos = lambda b, r: (r, 0)
    return pl.pallas_call(
        _diff_prep_kernel_cols,
        grid=(nb, tpb),
        in_specs=[pl.BlockSpec((TM, 256), lambda b, r, c=c: (b * tpb + r, EXT_DIFF // 256 + c))
                  for c in range(5)]
                 + [pl.BlockSpec((TM, 256), pos), pl.BlockSpec((TM, 256), pos)],
        out_specs=[pl.BlockSpec((TM, 256), tok)] * 3,
        out_shape=[jax.ShapeDtypeStruct((n, 256), BF16)] * 3,
        compiler_params=_cparams(("parallel", "parallel")),
        name="diff_prep",
    )(p, p, p, p, p, cos_tab, sin_tab)


def _diff_prep_kernel_cols(q_in, k_in, v_in, qr_in, kr_in, cos_ref, sin_ref, q_ref, k_ref, v_ref):
    cos, sin = cos_ref[...], sin_ref[...]
    scale = DIFF_HD ** -0.5 * LOG2E
    q_ref[...] = ((q_in[...] * cos + qr_in[...] * sin) * scale).astype(BF16)
    k_ref[...] = (k_in[...] * cos + kr_in[...] * sin).astype(BF16)
    v_ref[...] = v_in[...].astype(BF16)


def _flash_diff_kernel(q_ref, k_ref, v_ref, lam_ref, g_ref, o_ref, qs_ref, m_ref, l_ref, acc_ref, *, lam_init):
    kk = pl.program_id(2)
    tq = q_ref.shape[0]
    nsm = 2 * DIFF_HEADS

    @pl.when(kk == 0)
    def _():
        m_ref[...] = jnp.full(m_ref.shape, -jnp.inf, F32)
        l_ref[...] = jnp.zeros(l_ref.shape, F32)
        acc_ref[...] = jnp.zeros(acc_ref.shape, F32)
        q = q_ref[...]
        lane = lax.broadcasted_iota(jnp.int32, (tq, 256), 1)
        for i in range(nsm):
            qs_ref[i * tq:(i + 1) * tq, :] = jnp.where((lane // DIFF_HD) == i, q, jnp.zeros_like(q))

    k = k_ref[...]
    ps, alphas = [], []
    for i in range(nsm):
        rows = slice(i * tq, (i + 1) * tq)
        s = _dot_nt(qs_ref[rows, :], k)
        m_prev = m_ref[rows, :]
        m_new = jnp.maximum(m_prev, jnp.max(s, axis=-1, keepdims=True))
        alpha = jnp.exp2(m_prev - m_new)
        p = jnp.exp2(s - m_new)
        l_ref[rows, :] = alpha * l_ref[rows, :] + jnp.sum(p, axis=-1, keepdims=True)
        m_ref[rows, :] = m_new
        alphas.append(alpha)
        ps.append(p.astype(BF16))
    pv = _dot(jnp.concatenate(ps, axis=0), v_ref[...])
    acc_ref[...] = jnp.concatenate(alphas, axis=0) * acc_ref[...] + pv

    @pl.when(kk == pl.num_programs(2) - 1)
    def _():
        lane = lax.broadcasted_iota(jnp.int32, (tq, 256), 1)
        o = jnp.zeros((tq, 256), F32)
        for h in range(DIFF_HEADS):
            r0, r1 = (2 * h) * tq, (2 * h + 1) * tq
            o0 = acc_ref[r0:r0 + tq, :] / l_ref[r0:r0 + tq, :]
            o1 = acc_ref[r1:r1 + tq, :] / l_ref[r1:r1 + tq, :]
            o = jnp.where((lane // (2 * DIFF_HD)) == h, o0 - lam_ref[...] * o1, o)
        ms = _split_dot(o * o, _block_ones(256, 2 * DIFF_HD)) * (1.0 / (2 * DIFF_HD))
        o_ref[...] = o * lax.rsqrt(ms + DIFF_EPS) * g_ref[...] * (1.0 - lam_init)


def _flash_diff(q, k, v, lam_row, g_row, *, lam_init, nb, t_all, q_start, q_len, k_len, tk):
    tq = TM
    tpb_q, tpb_k = t_all // tq, t_all // tk
    q0 = q_start // tq
    nq = q_len // tq
    return pl.pallas_call(
        functools.partial(_flash_diff_kernel, lam_init=lam_init),
        grid=(nb, nq, k_len // tk),
        in_specs=[pl.BlockSpec((tq, 256), lambda b, i, kk: (b * tpb_q + q0 + i, 0)),
                  pl.BlockSpec((tk, 256), lambda b, i, kk: (b * tpb_k + kk, 0)),
                  pl.BlockSpec((tk, 256), lambda b, i, kk: (b * tpb_k + kk, 0)),
                  pl.BlockSpec((1, 256), lambda b, i, kk: (0, 0)),
                  pl.BlockSpec((1, 256), lambda b, i, kk: (0, 0))],
        out_specs=pl.BlockSpec((tq, 256), lambda b, i, kk: (b * nq + i, 0)),
        out_shape=jax.ShapeDtypeStruct((nb * q_len, 256), F32),
        scratch_shapes=[pltpu.VMEM((2 * DIFF_HEADS * tq, 256), BF16),
                        pltpu.VMEM((2 * DIFF_HEADS * tq, 1), F32),
                        pltpu.VMEM((2 * DIFF_HEADS * tq, 1), F32),
                        pltpu.VMEM((2 * DIFF_HEADS * tq, 256), F32)],
        compiler_params=_cparams(("parallel", "parallel", "arbitrary")),
        name="flash_diff",
    )(q, k, v, lam_row, g_row)


def _rw_prep_kernel(p_ref, prev_ref, next_ref, mu_ref, wlo_ref, g2_ref, vec_ref,
                    r_ref, v_ref, kk_ref, w0_ref, k0_ref, b0_ref, w1_ref, k1_ref, b1_ref,
                    bonus_ref, gate_ref, *, r0, lat_last):
    r = pl.program_id(1) + r0
    p = p_ref[...]
    row = lax.broadcasted_iota(jnp.int32, p.shape, 0)
    first_tile = jnp.logical_or(r == 0, r == 1)
    last_tile = jnp.logical_or(r == 0, r == lat_last)
    prev_row = jnp.where(first_tile, 0.0, prev_ref[7:8, :])
    next_row = jnp.where(last_tile, 0.0, next_ref[0:1, :])
    up = jnp.where(row == 0, prev_row, pltpu.roll(p, 1, 0))
    dn = jnp.where(row == TM - 1, next_row, pltpu.roll(p, TM - 1, 0))
    z = p + (0.5 * (up + dn) - p) * mu_ref[...]

    hw = RW_HEADS * RW_HEAD
    rr, k, v = z[:, :hw], z[:, hw:2 * hw], z[:, 2 * hw:3 * hw]
    lo = z[:, 3 * hw:3 * hw + 128]
    lane = lax.broadcasted_iota(jnp.int32, lo.shape, 1)
    lo = jnp.where(lane < 64, jnp.tanh(lo), lo).astype(BF16)
    wa = _dot(lo, wlo_ref[...])
    gate_ref[...] = _dot(_sigmoid(z[:, 3 * hw + 128:]).astype(BF16), g2_ref[...])

    e4 = _block_ones(hw, RW_HEAD)
    k_k, k_a, r_k = vec_ref[0:1, :], vec_ref[1:2, :], vec_ref[2:3, :]
    kk = k * k_k
    nrm = jnp.maximum(jnp.sqrt(_split_dot(kk * kk, e4)), 1e-12)
    kk = kk / nrm
    r_ref[...] = rr
    v_ref[...] = v
    kk_ref[...] = kk
    ksum = jnp.zeros_like(k)
    for d, (w_ref, kd_ref, b_ref) in enumerate(((w0_ref, k0_ref, b0_ref), (w1_ref, k1_ref, b1_ref))):
        w0 = vec_ref[3 + d:4 + d, :]
        a0 = vec_ref[5 + d:6 + d, :]
        wd = -_softplus(-(w0 + wa[:, d * hw:(d + 1) * hw])) - 0.5
        w_ref[...] = jnp.exp(-jnp.exp(wd))
        ad = _sigmoid(a0 + wa[:, (2 + d) * hw:(3 + d) * hw])
        kd = k * (1.0 + (ad - 1.0) * k_a)
        kd_ref[...] = kd
        b_ref[...] = kk * ad
        ksum = ksum + kd
    bonus_ref[...] = _split_dot(rr * ksum * r_k, e4) * v


def _rw_prep(p, mu, wlo, g2, vecs, *, nb, tpb, r0, nt):
    n_out = nb * nt * TM
    hw = RW_HEADS * RW_HEAD
    n_rows8 = p.shape[0] // 8
    tok = lambda b, r: (b * tpb + r0 + r, EXT_RW // 1024)
    prev = lambda b, r: (jnp.maximum((b * tpb + r0 + r) * (TM // 8) - 1, 0), EXT_RW // 1024)
    nxt = lambda b, r: (jnp.minimum((b * tpb + r0 + r + 1) * (TM // 8), n_rows8 - 1), EXT_RW // 1024)
    out = lambda b, r: (b * nt + r, 0)
    full = lambda b, r: (0, 0)
    return pl.pallas_call(
        functools.partial(_rw_prep_kernel, r0=r0, lat_last=tpb - 1),
        grid=(nb, nt),
        in_specs=[pl.BlockSpec((TM, 1024), tok),
                  pl.BlockSpec((8, 1024), prev),
                  pl.BlockSpec((8, 1024), nxt),
                  pl.BlockSpec((1, 1024), full),
                  pl.BlockSpec((128, 4 * hw), full),
                  pl.BlockSpec((128, hw), full),
                  pl.BlockSpec((8, hw), full)],
        out_specs=[pl.BlockSpec((TM, hw), out)] * 11,
        out_shape=[jax.ShapeDtypeStruct((n_out, hw), F32)] * 11,
        compiler_params=_cparams(("parallel", "parallel")),
        name="rw_prep",
    )(p, p, p, mu, wlo, g2, vecs)


def _rw_scan_kernel(rf, vf, kkf, wf, kf, bf, rb, vb, kkb, wb, kb, bb, yf_ref, yb_ref, s_ref, *, nb):
    c = RW_CHUNK

    @pl.when(pl.program_id(0) == 0)
    def _():
        s_ref[...] = jnp.zeros(s_ref.shape, F32)

    e2 = _block_ones(128, RW_HEAD)
    z2 = jnp.zeros((128, 128), BF16)
    left, right = jnp.concatenate([e2, z2], axis=1), jnp.concatenate([z2, e2], axis=1)
    rhs_pair1 = jnp.concatenate([left, right], axis=0)
    rhs_pair2 = jnp.concatenate([left, left, right, right], axis=0)
    e22 = jnp.concatenate([e2, e2], axis=0)
    lane = lax.broadcasted_iota(jnp.int32, (RW_HEAD, 128), 1)
    sub = lax.broadcasted_iota(jnp.int32, (RW_HEAD, 128), 0)
    diag = (lane % RW_HEAD) == sub
    sub8 = lax.broadcasted_iota(jnp.int32, (8, 128), 0)
    dirs = ((rf, vf, kkf, wf, kf, bf, yf_ref), (rb, vb, kkb, wb, kb, bb, yb_ref))

    def allreduce_rows(x):
        t = x[0:8]
        for i in range(1, 8):
            t = t + x[8 * i:8 * i + 8]
        for sh in (4, 2, 1):
            t = t + pltpu.roll(t, sh, 0)
        return t

    def col(row):
        return jnp.where(diag, row, 0.0).astype(BF16)

    def split(x):
        hi = x.astype(BF16).astype(F32)
        return hi, x - hi

    def group(g, carry):
        tiles, ytiles = {}, {}
        for d, refs in enumerate(dirs):
            base = pl.multiple_of((g if d == 0 else c // 8 - 1 - g) * 8, 8)
            if d == 0:
                prev = lambda x, sh: jnp.where(sub8 >= sh, pltpu.roll(x, sh, 0), 1.0)
            else:
                prev = lambda x, sh: jnp.where(sub8 < 8 - sh, pltpu.roll(x, 8 - sh, 0), 1.0)
            last = 7 if d == 0 else 0
            for b in range(nb):
                for hp in range(2):
                    r_, v_, kk_, w_, k_, b_ = [ref[b, pl.ds(base, 8), pl.ds(hp * 128, 128)] for ref in refs[:6]]
                    gam = w_
                    for sh in (1, 2, 4):
                        gam = gam * prev(gam, sh)
                    inv = 1.0 / gam
                    tiles[d, b, hp] = (base, v_, split(b_ * inv), split(kk_ * prev(gam, 1)), k_ * inv, r_ * gam,
                                       split(gam[last:last + 1, :]))
                    ytiles[d, b, hp] = jnp.zeros((8, 128), F32)
        units = [(d, b, hp) for d in range(2) for b in range(nb) for hp in range(2)]
        for jj in range(8):
            lhs2, lhs1 = [], []
            for (d, b, hp) in units:
                rw = slice(jj, jj + 1) if d == 0 else slice(7 - jj, 8 - jj)
                _, _, (bh, bm), (kkh, kkm), kh, rh, _ = tiles[d, b, hp]
                lhs2.append(jnp.concatenate([col(bh[rw]), col(bm[rw]), col(kkh[rw]), col(kkm[rw])], axis=1))
                lhs1.append(jnp.concatenate([col(kh[rw]), col(rh[rw])], axis=1))
            c2 = _dot(jnp.concatenate(lhs2, axis=0), rhs_pair2)
            c1 = _dot(jnp.concatenate(lhs1, axis=0), rhs_pair1)
            for u, (d, b, hp) in enumerate(units):
                j = jj if d == 0 else 7 - jj
                v_ = tiles[d, b, hp][1]
                rows = slice(u * RW_HEAD, (u + 1) * RW_HEAD)
                bc, kkc, kc, rc = c2[rows, :128], c2[rows, 128:], c1[rows, :128], c1[rows, 128:]
                s = s_ref[u]
                sa = jnp.concatenate([allreduce_rows(kkc * s)] * 8, axis=0)
                s = s - bc * sa + kc * v_[j:j + 1]
                s_ref[u] = s
                ytiles[d, b, hp] = jnp.where(sub8 == j, allreduce_rows(rc * s), ytiles[d, b, hp])
        for d, refs in enumerate(dirs):
            for b in range(nb):
                for hp in range(2):
                    u = (d * nb + b) * 2 + hp
                    gh, gm = tiles[d, b, hp][6]
                    s_ref[u] = s_ref[u] * _dot(jnp.concatenate([col(gh), col(gm)], axis=1), e22)
                    refs[6][b, pl.ds(tiles[d, b, hp][0], 8), pl.ds(hp * 128, 128)] = ytiles[d, b, hp]
        return carry

    lax.fori_loop(0, c // 8, group, 0)


def _rw_scan(r, v, kk, w0, k0, b0, w1, k1, b1, *, nb, t_all, n_ctx):
    c = RW_CHUNK
    nc, ncc = t_all // c, n_ctx // c
    hw = RW_HEADS * RW_HEAD
    shp = lambda a: a.reshape(nb, t_all, hw)
    fwd = lambda j: (0, j, 0)
    bwd = lambda j: (0, jnp.where(j < ncc, ncc - 1 - j, nc - 1 - (j - ncc)), 0)
    blk = (nb, c, hw)
    yshape = jax.ShapeDtypeStruct((nb, t_all, hw), F32)
    yf, yb = pl.pallas_call(
        functools.partial(_rw_scan_kernel, nb=nb),
        grid=(nc,),
        in_specs=[pl.BlockSpec(blk, fwd)] * 6 + [pl.BlockSpec(blk, bwd)] * 6,
        out_specs=[pl.BlockSpec(blk, fwd), pl.BlockSpec(blk, bwd)],
        out_shape=[yshape, yshape],
        scratch_shapes=[pltpu.VMEM((2 * nb * 2, RW_HEAD, 128), F32)],
        compiler_params=_cparams(("arbitrary",)),
        name="rw_scan",
    )(shp(r), shp(v), shp(kk), shp(w0), shp(k0), shp(b0),
      shp(r), shp(v), shp(kk), shp(w1), shp(k1), shp(b1))

    return yf.reshape(nb * t_all, hw), yb.reshape(nb * t_all, hw)


def _rw_post_kernel(yf_ref, yb_ref, bonus_ref, gate_ref, vec_ref, o_ref):
    e4 = _block_ones(RW_HEADS * RW_HEAD, RW_HEAD)
    y = yf_ref[...] + yb_ref[...]
    mean = _split_dot(y, e4) * (1.0 / RW_HEAD)
    yc = y - mean
    var = _split_dot(yc * yc, e4) * (1.0 / RW_HEAD)
    yn = yc * lax.rsqrt(var + RW_LN_EPS) * vec_ref[0:1, :] + vec_ref[1:2, :]
    o_ref[...] = (yn + bonus_ref[...]) * gate_ref[...]


def _rw_post(yf, yb, bonus, gate, vecs, *, nb, tpb_y, r0_y, nt):
    hw = RW_HEADS * RW_HEAD
    n_out = bonus.shape[0]
    ytok = lambda b, r: (b * tpb_y + r0_y + r, 0)
    tok = lambda b, r: (b * nt + r, 0)
    return pl.pallas_call(
        _rw_post_kernel,
        grid=(nb, nt),
        in_specs=[pl.BlockSpec((TM, hw), ytok), pl.BlockSpec((TM, hw), ytok),
                  pl.BlockSpec((TM, hw), tok), pl.BlockSpec((TM, hw), tok),
                  pl.BlockSpec((8, hw), lambda b, r: (0, 0))],
        out_specs=pl.BlockSpec((TM, hw), tok),
        out_shape=jax.ShapeDtypeStruct((n_out, hw), F32),
        compiler_params=_cparams(("parallel", "parallel")),
        name="rw_post",
    )(yf, yb, bonus, gate, vecs)


def _s5_bu_kernel(p_ref, w_ref, o_ref):
    o_ref[...] = _dot(p_ref[...].astype(BF16), w_ref[...])


def _s5_bu(p, w_b, *, nb, tpb):
    n = p.shape[0]
    nst = S5_GROUPS * S5_STATE
    return pl.pallas_call(
        _s5_bu_kernel,
        grid=(nb * tpb, 4),
        in_specs=[pl.BlockSpec((TM, 256), lambda i, j: (i, EXT_S5 // 256)),
                  pl.BlockSpec((256, nst), lambda i, j: (0, j))],
        out_specs=pl.BlockSpec((TM, nst), lambda i, j: (i, j)),
        out_shape=jax.ShapeDtypeStruct((n, 4 * nst), F32),
        compiler_params=_cparams(("parallel", "arbitrary")),
        name="s5_bu",
    )(p, w_b)


def _s5_scan_kernel(buf_ref, bub_ref, ab_ref, xf_ref, xb_ref, st_ref, *, nb):
    c = S5_CHUNK
    nst = S5_GROUPS * S5_STATE

    @pl.when(pl.program_id(0) == 0)
    def _():
        st_ref[...] = jnp.zeros(st_ref.shape, F32)

    dirs = ((buf_ref, xf_ref), (bub_ref, xb_ref))

    def group(g, carry):
        for d, (bu_ref, x_ref) in enumerate(dirs):
            base = pl.multiple_of((g if d == 0 else c // 8 - 1 - g) * 8, 8)
            ar = ab_ref[d, :, 0:nst]
            ai = ab_ref[d, :, nst:2 * nst]
            for b in range(nb):
                u = d * nb + b
                xr = st_ref[u, :, 0:nst]
                xi = st_ref[u, :, nst:2 * nst]
                bur = bu_ref[b, pl.ds(base, 8), 0:nst]
                bui = bu_ref[b, pl.ds(base, 8), nst:2 * nst]
                rows_r, rows_i = [None] * 8, [None] * 8
                for jj in range(8):
                    j = jj if d == 0 else 7 - jj
                    xr, xi = (ar * xr - ai * xi + bur[j:j + 1, :], ar * xi + ai * xr + bui[j:j + 1, :])
                    rows_r[j], rows_i[j] = xr, xi
                st_ref[u, :, 0:nst] = xr
                st_ref[u, :, nst:2 * nst] = xi
                x_ref[b, pl.ds(base, 8), 0:nst] = jnp.concatenate(rows_r, axis=0)
                x_ref[b, pl.ds(base, 8), nst:2 * nst] = jnp.concatenate(rows_i, axis=0)
        return carry

    lax.fori_loop(0, c // 8, group, 0)


def _s5_scan(bu, ab, *, nb, t_all, n_ctx):
    c = S5_CHUNK
    nc, ncc = t_all // c, n_ctx // c
    nst2 = 2 * S5_GROUPS * S5_STATE
    bu3 = bu.reshape(nb, t_all, 2 * nst2)
    fwd = lambda j: j
    bwd = lambda j: jnp.where(j < ncc, ncc - 1 - j, nc - 1 - (j - ncc))
    xshape = jax.ShapeDtypeStruct((nb, t_all, nst2), F32)
    xf, xb = pl.pallas_call(
        functools.partial(_s5_scan_kernel, nb=nb),
        grid=(nc,),
        in_specs=[pl.BlockSpec((nb, c, nst2), lambda j: (0, fwd(j), 0)),
                  pl.BlockSpec((nb, c, nst2), lambda j: (0, bwd(j), 1)),
                  pl.BlockSpec((2, 1, nst2), lambda j: (0, 0, 0))],
        out_specs=[pl.BlockSpec((nb, c, nst2), lambda j: (0, fwd(j), 0)),
                   pl.BlockSpec((nb, c, nst2), lambda j: (0, bwd(j), 0))],
        out_shape=[xshape, xshape],
        scratch_shapes=[pltpu.VMEM((2 * nb, 1, nst2), F32)],
        compiler_params=_cparams(("arbitrary",)),
        name="s5_scan",
    )(bu3, bu3, ab)
    return xf.reshape(nb * t_all, nst2), xb.reshape(nb * t_all, nst2)


def _s5_post_kernel(xf_ref, xb_ref, u_ref, cf_ref, cb_ref, d_ref, gw_ref, gb_ref, o_ref):
    y = (_dot(xf_ref[...].astype(BF16), cf_ref[...]) + _dot(xb_ref[...].astype(BF16), cb_ref[...])
         + d_ref[...] * u_ref[...])
    zg = 0.5 * y * (1.0 + jnp.tanh(math.sqrt(2.0 / math.pi) * (y + 0.044715 * (y * y * y))))
    o_ref[...] = zg * _sigmoid(_dot(zg.astype(BF16), gw_ref[...]) + gb_ref[...])


def _s5_post(xf, xb, p, cf, cb, d_row, glu_w, glu_b, *, nb, tpb, r0, nt):
    nst2 = 2 * S5_GROUPS * S5_STATE
    tok = lambda b, r: (b * tpb + r0 + r, 0)
    full = lambda b, r: (0, 0)
    return pl.pallas_call(
        _s5_post_kernel,
        grid=(nb, nt),
        in_specs=[pl.BlockSpec((TM, nst2), tok), pl.BlockSpec((TM, nst2), tok),
                  pl.BlockSpec((TM, 256), lambda b, r: (b * tpb + r0 + r, EXT_S5 // 256)),
                  pl.BlockSpec((nst2, 256), full), pl.BlockSpec((nst2, 256), full),
                  pl.BlockSpec((1, 256), full), pl.BlockSpec((256, 256), full),
                  pl.BlockSpec((1, 256), full)],
        out_specs=pl.BlockSpec((TM, 256), lambda b, r: (b * nt + r, 0)),
        out_shape=jax.ShapeDtypeStruct((nb * nt * TM, 256), F32),
        compiler_params=_cparams(("parallel", "parallel")),
        name="s5_post",
    )(xf, xb, p, cf, cb, d_row, glu_w, glu_b)


def _merge_kernel(ya_ref, yb_ref, ys_ref, yd_ref, g0, g1, g2, g3, h_ref, m2_ref, wb_ref, wo_ref, o_ref):
    acc = None
    for n, (y_ref, g_ref) in enumerate(((ya_ref, g0), (yb_ref, g1), (ys_ref, g2), (yd_ref, g3))):
        term = _sigmoid(g_ref[...]) * _dot(y_ref[...].astype(BF16), wb_ref[n])
        acc = term if acc is None else acc + term
    o_ref[...] = h_ref[...] + m2_ref[0] * _dot(acc.astype(BF16), wo_ref[...])


def _merge(ya, yb, ys, yd, p, h, mod2, w_branch, w_out, *, nb, tpb, r0, nt):
    tok_in = lambda b, r: (b * tpb + r0 + r, 0)
    tok_out = lambda b, r: (b * nt + r, 0)
    full2 = lambda b, r: (0, 0)
    gate = [pl.BlockSpec((TM, D_MODEL), lambda b, r, c=c: (b * tpb + r0 + r, EXT_GATE // D_MODEL + c))
            for c in range(4)]
    return pl.pallas_call(
        _merge_kernel,
        grid=(nb, nt),
        in_specs=[pl.BlockSpec((TM, 256), tok_out)] * 4 + gate
                 + [pl.BlockSpec((TM, D_MODEL), tok_in),
                    pl.BlockSpec((1, 1, D_MODEL), lambda b, r: (jnp.where(r0 + r == 0, nb, b), 0, 0)),
                    pl.BlockSpec((4, 256, D_MODEL), lambda b, r: (0, 0, 0)),
                    pl.BlockSpec((D_MODEL, D_MODEL), full2)],
        out_specs=pl.BlockSpec((TM, D_MODEL), tok_out),
        out_shape=jax.ShapeDtypeStruct((nb * nt * TM, D_MODEL), F32),
        compiler_params=_cparams(("parallel", "parallel")),
        name="merge",
    )(ya, yb, ys, yd, p, p, p, p, h, mod2, w_branch, w_out)


def _router_kernel(h_ref, g_ref, sh_ref, sc_ref, wh_ref, wm_ref, b_ref, x_ref, lg_ref):
    x = _rms(h_ref[...], g_ref[...], NORM_EPS) * (1.0 + sc_ref[0]) + sh_ref[0]
    xh = x.astype(BF16)
    xm = (x - xh.astype(F32)).astype(BF16)
    x_ref[...] = x
    lg = (_dot(xh, wh_ref[...]) + _dot(xm, wh_ref[...]) + _dot(xh, wm_ref[...])) + b_ref[...]

    lane = lax.broadcasted_iota(jnp.int32, lg.shape, 1)
    lanef = lane.astype(F32)
    neg = jnp.float32(-jnp.inf)
    big = jnp.float32(1e9)
    rmax = lambda v: jnp.max(v, axis=-1, keepdims=True)
    rmin = lambda v: jnp.min(v, axis=-1, keepdims=True)
    rsum = lambda v: jnp.sum(v, axis=-1, keepdims=True)

    gmask = lane < MOE_GROUPS
    mg = rmax(jnp.where(gmask, lg, neg))
    eg = jnp.where(gmask, jnp.exp(lg - mg), 0.0)
    pg = eg / rsum(eg)
    pg_top = rmax(pg)
    g_sel = rmin(jnp.where(jnp.logical_and(gmask, pg == pg_top), lanef, big))
    lo = MOE_GROUPS + MOE_PER_GROUP * g_sel
    emask = jnp.logical_and(lanef >= lo, lanef < lo + MOE_PER_GROUP)
    me = rmax(jnp.where(emask, lg, neg))
    ee = jnp.where(emask, jnp.exp(lg - me), 0.0)
    pe = jnp.where(emask, ee / rsum(ee), -1.0)
    p1 = rmax(pe)
    i1 = rmin(jnp.where(pe == p1, lanef, big))
    pe2 = jnp.where(lanef == i1, -1.0, pe)
    p2 = rmax(pe2)
    i2 = rmin(jnp.where(pe2 == p2, lanef, big))
    den = p1 + p2
    out = jnp.where(lane == 0, i1 - MOE_GROUPS, 0.0)
    out = jnp.where(lane == 1, i2 - MOE_GROUPS, out)
    out = jnp.where(lane == 2, pg_top * p1 / den, out)
    out = jnp.where(lane == 3, pg_top * p2 / den, out)
    lg_ref[...] = out


def _router(h, g, shift, scale, wh, wm, bias, *, nb, nt, ctx_first):
    tok = lambda b, r: (b * nt + r, 0)
    full = lambda b, r: (0, 0)
    if ctx_first:
        modmap = lambda b, r: (jnp.where(r == 0, nb, b), 0, 0)
    else:
        modmap = lambda b, r: (b, 0, 0)
    n = h.shape[0]
    return pl.pallas_call(
        _router_kernel,
        grid=(nb, nt),
        in_specs=[pl.BlockSpec((TM, D_MODEL), tok), pl.BlockSpec((1, D_MODEL), full),
                  pl.BlockSpec((1, 1, D_MODEL), modmap), pl.BlockSpec((1, 1, D_MODEL), modmap),
                  pl.BlockSpec((D_MODEL, 128), full), pl.BlockSpec((D_MODEL, 128), full),
                  pl.BlockSpec((1, 128), full)],
        out_specs=[pl.BlockSpec((TM, D_MODEL), tok), pl.BlockSpec((TM, 128), tok)],
        out_shape=[jax.ShapeDtypeStruct((n, D_MODEL), F32), jax.ShapeDtypeStruct((n, 128), F32)],
        compiler_params=_cparams(("parallel", "parallel")),
        name="router",
    )(h, g, shift, scale, wh, wm, bias)


def _expert_kernel(be_ref, nv_ref, x_ref, wg_ref, wu_ref, wd_ref, o_ref, wgb_ref, wub_ref, wdb_ref):
    i = pl.program_id(0)

    @pl.when(jnp.logical_or(i == 0, be_ref[i] != be_ref[jnp.maximum(i - 1, 0)]))
    def _():
        wgb_ref[...] = wg_ref[0].astype(BF16)
        wub_ref[...] = wu_ref[0].astype(BF16)
        wdb_ref[...] = wd_ref[0].astype(BF16)

    @pl.when(i < nv_ref[0])
    def _():
        x = x_ref[...].astype(BF16)
        hb = _silu(_dot(x, wgb_ref[...])) * _dot(x, wub_ref[...])
        o_ref[...] = _dot(hb.astype(BF16), wdb_ref[...])

    @pl.when(i >= nv_ref[0])
    def _():
        o_ref[...] = jnp.zeros(o_ref.shape, F32)


def _experts(xs, block_e, n_valid, w_gate, w_up, w_down):
    n_slots = xs.shape[0]
    n_blocks = n_slots // MOE_BLK
    wmap = lambda i, be, nv: (be[i], 0, 0)
    return pl.pallas_call(
        _expert_kernel,
        grid_spec=pltpu.PrefetchScalarGridSpec(
            num_scalar_prefetch=2,
            grid=(n_blocks,),
            in_specs=[pl.BlockSpec((MOE_BLK, D_MODEL), lambda i, be, nv: (i, 0)),
                      pl.BlockSpec((1, D_MODEL, D_EXPERT), wmap),
                      pl.BlockSpec((1, D_MODEL, D_EXPERT), wmap),
                      pl.BlockSpec((1, D_EXPERT, D_MODEL), wmap)],
            out_specs=pl.BlockSpec((MOE_BLK, D_MODEL), lambda i, be, nv: (i, 0)),
            scratch_shapes=[pltpu.VMEM((D_MODEL, D_EXPERT), BF16), pltpu.VMEM((D_MODEL, D_EXPERT), BF16),
                            pltpu.VMEM((D_EXPERT, D_MODEL), BF16)]),
        out_shape=jax.ShapeDtypeStruct((n_slots, D_MODEL), F32),
        compiler_params=_cparams(("arbitrary",)),
        name="experts",
    )(block_e, n_valid, xs, w_gate, w_up, w_down)


def _combine_kernel(h_ref, y0_ref, y1_ref, w_ref, m5_ref, g_ref, o_ref, *, final):
    w = w_ref[...]
    y = y0_ref[...] * w[:, 0:1] + y1_ref[...] * w[:, 1:2]
    h = h_ref[...] + m5_ref[0] * y
    if final:
        h = _rms(h, g_ref[...], NORM_EPS)
    o_ref[...] = h


def _combine(h, y0, y1, wts, mod5, g_final, *, nb, nt, ctx_first, final):
    tok = lambda b, r: (b * nt + r, 0)
    if ctx_first:
        modmap = lambda b, r: (jnp.where(r == 0, nb, b), 0, 0)
    else:
        modmap = lambda b, r: (b, 0, 0)
    return pl.pallas_call(
        functools.partial(_combine_kernel, final=final),
        grid=(nb, nt),
        in_specs=[pl.BlockSpec((TM, D_MODEL), tok)] * 3
                 + [pl.BlockSpec((TM, 128), tok), pl.BlockSpec((1, 1, D_MODEL), modmap),
                    pl.BlockSpec((1, D_MODEL), lambda b, r: (0, 0))],
        out_specs=pl.BlockSpec((TM, D_MODEL), tok),
        out_shape=jax.ShapeDtypeStruct(h.shape, F32),
        compiler_params=_cparams(("parallel", "parallel")),
        name="combine",
    )(h, y0, y1, wts, mod5, g_final)


def _moe(h, g2, shift, scale, mod5, wh, wm, rbias, w_gate, w_up, w_down, g_final, *, nb, nt, ctx_first, final):
    n = h.shape[0]
    x_bf, route = _router(h, g2, shift, scale, wh, wm, rbias, nb=nb, nt=nt, ctx_first=ctx_first)
    idx = route[:, :MOE_TOPK].astype(jnp.int32)
    wts = route[:, MOE_TOPK:2 * MOE_TOPK]
    n_as = n * MOE_TOPK
    flat_e = idx.reshape(n_as)
    onehot = (flat_e[:, None] == jnp.arange(MOE_EXPERTS, dtype=jnp.int32)[None, :]).astype(jnp.int32)
    csum = jnp.cumsum(onehot, axis=0)
    counts = csum[-1]
    rank = jnp.sum(jnp.where(onehot > 0, csum - 1, 0), axis=1)
    padded = (counts + MOE_BLK - 1) // MOE_BLK * MOE_BLK
    pad_end = jnp.cumsum(padded)
    pad_start = pad_end - padded
    slot = pad_start[flat_e] + rank
    n_blocks = (n_as + MOE_EXPERTS * (MOE_BLK - 1) + MOE_BLK - 1) // MOE_BLK
    n_slots = n_blocks * MOE_BLK
    slot_tok = jnp.zeros((n_slots,), jnp.int32).at[slot].set(jnp.arange(n_as, dtype=jnp.int32) // MOE_TOPK)
    starts = jnp.arange(n_blocks, dtype=jnp.int32) * MOE_BLK
    block_e = jnp.minimum(jnp.sum((pad_end[None, :] <= starts[:, None]).astype(jnp.int32), axis=1),
                          MOE_EXPERTS - 1)
    n_valid = (pad_end[-1:] // MOE_BLK).astype(jnp.int32)
    xs = jnp.take(x_bf, slot_tok, axis=0)
    ys = _experts(xs, block_e, n_valid, w_gate, w_up, w_down)
    slot2 = slot.reshape(n, MOE_TOPK)
    y0 = jnp.take(ys, slot2[:, 0], axis=0)
    y1 = jnp.take(ys, slot2[:, 1], axis=0)
    wts_pad = jnp.pad(wts.astype(F32), ((0, 0), (0, 128 - MOE_TOPK)))
    return _combine(h, y0, y1, wts_pad, mod5, g_final, nb=nb, nt=nt, ctx_first=ctx_first, final=final)


_ROT_SRC = np.array(list(range(8, 16)) + list(range(0, 8)) + list(range(24, 32)) + list(range(16, 24)))
_ROT_SIGN = np.array([-1.0] * 8 + [1.0] * 8 + [-1.0] * 8 + [1.0] * 8, np.float32)


def _rot_cols(w):
    k = w.shape[-1] // ROPE_DIM
    src = np.concatenate([_ROT_SRC + ROPE_DIM * i for i in range(k)])
    sign = np.tile(_ROT_SIGN, k)
    return w[..., src] * sign


def _rope_tables(n_ctx, n_lat):
    rows = n_lat // GRID_W
    row = jnp.repeat(jnp.arange(rows, dtype=F32), GRID_W)
    col = (jnp.arange(rows * GRID_W) % GRID_W).astype(F32)
    nf = ROPE_DIM // 4
    inv = ROPE_BASE ** (-jnp.arange(nf, dtype=F32) / nf)
    ar = row[:, None] * inv
    ac = col[:, None] * inv
    ang = jnp.concatenate([ar, ar, ac, ac], axis=-1)
    cos = jnp.concatenate([jnp.ones((n_ctx, ROPE_DIM), F32), jnp.cos(ang)], axis=0)
    sin = jnp.concatenate([jnp.zeros((n_ctx, ROPE_DIM), F32), jnp.sin(ang)], axis=0)
    return cos, sin


def _block_diag(blocks):
    g, a, b = blocks.shape
    eye = jnp.eye(g, dtype=blocks.dtype)
    return (eye[:, None, :, None] * blocks[:, :, None, :]).reshape(g * a, g * b)


def _pick_tk(t_all):
    best = 128
    for tk in range(128, ATTN_TK_MAX + 1, 128):
        if t_all % tk == 0:
            best = tk
    return best


def kernel(x, c, ctx, c_ctx, w_mod, b_mod, norm1_g, norm2_g, w_in, mla_q_norm_g, mla_kv_norm_g, mla_w_uq, mla_w_ukv, rw_mu, rw_w0, rw_w2, rw_a0, rw_a2, rw_g2, rw_k_k, rw_k_a, rw_r_k, rw_lnx_g, rw_lnx_b, s5_a_re, s5_a_im, s5_log_dt, s5_b_re, s5_b_im, s5_c_re, s5_c_im, s5_d, s5_glu_w, s5_glu_b, diff_lq1, diff_lk1, diff_lq2, diff_lk2, diff_subln_g, w_branch, w_out, router_g_w, router_g_b, router_e_w, router_e_b, exp_w_gate, exp_w_up, exp_w_down, final_norm_g):
    nb, n_lat, d = x.shape
    n_ctx = ctx.shape[1]
    depth = w_mod.shape[0]
    t_all = n_ctx + n_lat
    assert d == D_MODEL and n_ctx == TM and n_lat % TM == 0
    tpb = t_all // TM
    tk = _pick_tk(t_all)
    hw = RW_HEADS * RW_HEAD

    cos, sin = _rope_tables(n_ctx, n_lat)
    mla_scale = (MLA_NOPE + MLA_ROPE) ** -0.5 * LOG2E
    z32 = jnp.zeros((t_all, 32), F32)
    cq_tab = jnp.concatenate([jnp.ones((t_all, 64), F32), cos, z32], axis=1) * mla_scale
    sq_tab = jnp.concatenate([jnp.zeros((t_all, 64), F32), sin, z32], axis=1) * mla_scale
    ck_tab = jnp.concatenate([cos, sin, jnp.zeros((t_all, 64), F32)], axis=1)
    dcos = jnp.tile(cos, (1, 8))
    dsin = jnp.tile(sin, (1, 8))

    c_rows = jnp.concatenate([c, c_ctx[None, :], jnp.zeros((8 - nb - 1, d), F32)], axis=0)

    h = jnp.concatenate([ctx, x], axis=1).reshape(nb * t_all, d)

    for l in range(depth):
        last = l == depth - 1
        r0, nt = (1, tpb - 1) if last else (0, tpb)

        mod = _mm(c_rows, w_mod[l].astype(BF16), b_mod[l][None, :], tm=8, tn=1536, pre_silu=True, name="mod")
        mods = [mod[:nb + 1, i * d:(i + 1) * d].reshape(nb + 1, 1, d) for i in range(6)]

        wi = w_in[l]
        o_rw, o_s5, o_df, o_gt = 416, 1440, 1696, 2464
        w_kr = wi[:, 384:416]
        w_dq, w_dk, w_dv = wi[:, o_df:o_df + 256], wi[:, o_df + 256:o_df + 512], wi[:, o_df + 512:o_df + 768]
        w_ext = jnp.concatenate(
            [wi[:, o_rw:o_s5],
             wi[:, :416], _rot_cols(w_kr), jnp.zeros((d, 64), F32),
             wi[:, o_s5:o_df],
             w_dq, w_dk, w_dv, _rot_cols(w_dq), _rot_cols(w_dk),
             wi[:, o_gt:]], axis=1).astype(BF16)
        p = _inproj(h, norm1_g[l][None, :], mods[0], mods[1], w_ext, nb=nb, tpb=tpb)

        wq = mla_w_uq[l].reshape(MLA_Q_LORA, MLA_HEADS, MLA_NOPE + MLA_ROPE)
        zq = jnp.zeros((MLA_Q_LORA, MLA_HEADS, 32), F32)
        wa = jnp.concatenate([wq, zq], axis=2).reshape(MLA_Q_LORA, 512).astype(BF16)
        wb = jnp.concatenate([jnp.zeros((MLA_Q_LORA, MLA_HEADS, 64), F32), _rot_cols(wq[:, :, MLA_NOPE:]), zq],
                             axis=2).reshape(MLA_Q_LORA, 512).astype(BF16)
        wkv = mla_w_ukv[l].reshape(MLA_KV_LORA, MLA_HEADS, MLA_NOPE + MLA_V)
        wk = jnp.concatenate([wkv[:, :, :MLA_NOPE], jnp.zeros((MLA_KV_LORA, MLA_HEADS, 64), F32)],
                             axis=2).reshape(MLA_KV_LORA, 512).astype(BF16)
        wv = wkv[:, :, MLA_NOPE:].reshape(MLA_KV_LORA, MLA_HEADS * MLA_V).astype(BF16)
        pk_np = np.zeros((128, 512), np.float32)
        for hh in range(MLA_HEADS):
            for i in range(32):
                pk_np[i, hh * 128 + 64 + i] = 1.0
                pk_np[32 + i, hh * 128 + 64 + i] = 1.0
        pk = jnp.asarray(pk_np, BF16)
        q_m, k_m, v_m = _mla_prep(p, mla_q_norm_g[l][None, :], mla_kv_norm_g[l][None, :], wa, wb, wk, wv, pk,
                                  cq_tab, sq_tab, ck_tab, nb=nb, tpb=tpb)
        ya_lat = _flash_mla(q_m, k_m, v_m, nb=nb, t_all=t_all, q_start=n_ctx, q_len=n_lat, k_len=t_all, tk=tk)

        q_d, k_d, v_d = _diff_prep(p, dcos, dsin, nb=nb, tpb=tpb)
        lam_init = 0.8 - 0.6 * math.exp(-0.3 * l)
        lam = (jnp.exp(jnp.sum(diff_lq1[l] * diff_lk1[l])) - jnp.exp(jnp.sum(diff_lq2[l] * diff_lk2[l])) + lam_init)
        lam_row = jnp.full((1, 256), lam, F32)
        g_row = jnp.tile(diff_subln_g[l], DIFF_HEADS)[None, :]
        yd_lat = _flash_diff(q_d, k_d, v_d, lam_row, g_row, lam_init=lam_init, nb=nb, t_all=t_all,
                             q_start=n_ctx, q_len=n_lat, k_len=t_all, tk=tk)
        if last:
            ya, yd = ya_lat, yd_lat
        else:
            ya_ctx = _flash_mla(q_m, k_m, v_m, nb=nb, t_all=t_all, q_start=0, q_len=n_ctx, k_len=n_ctx, tk=n_ctx)
            yd_ctx = _flash_diff(q_d, k_d, v_d, lam_row, g_row, lam_init=lam_init, nb=nb, t_all=t_all,
                                 q_start=0, q_len=n_ctx, k_len=n_ctx, tk=n_ctx)
            comb = lambda a_c, a_l: jnp.concatenate(
                [a_c.reshape(nb, n_ctx, -1), a_l.reshape(nb, n_lat, -1)], axis=1).reshape(nb * t_all, -1)
            ya, yd = comb(ya_ctx, ya_lat), comb(yd_ctx, yd_lat)

        wlo = jnp.zeros((128, 4 * hw), F32)
        wlo = wlo.at[:64, 0:hw].set(rw_w2[l, 0]).at[:64, hw:2 * hw].set(rw_w2[l, 1])
        wlo = wlo.at[64:, 2 * hw:3 * hw].set(rw_a2[l, 0]).at[64:, 3 * hw:].set(rw_a2[l, 1])
        vecs = jnp.stack([rw_k_k[l], rw_k_a[l], rw_r_k[l].reshape(hw), rw_w0[l, 0], rw_w0[l, 1],
                          rw_a0[l, 0], rw_a0[l, 1], jnp.zeros((hw,), F32)], axis=0)
        (r_, v_, kk_, w0_, k0_, b0_, w1_, k1_, b1_, bonus, gate_rw) = _rw_prep(
            p, rw_mu[l][None, :], wlo.astype(BF16), rw_g2[l].astype(BF16), vecs, nb=nb, tpb=tpb, r0=0, nt=tpb)
        yf, yb_ = _rw_scan(r_, v_, kk_, w0_, k0_, b0_, w1_, k1_, b1_, nb=nb, t_all=t_all, n_ctx=n_ctx)
        ln_vecs = jnp.concatenate([rw_lnx_g[l][None, :], rw_lnx_b[l][None, :], jnp.zeros((6, hw), F32)], axis=0)
        if last:
            trim = lambda a: a.reshape(nb, t_all, hw)[:, n_ctx:].reshape(nb * n_lat, hw)
            bonus, gate_rw = trim(bonus), trim(gate_rw)
        y_rw = _rw_post(yf, yb_, bonus, gate_rw, ln_vecs, nb=nb, tpb_y=tpb, r0_y=r0, nt=nt)

        bbs, abs_, cfs = [], [], []
        for dd in range(2):
            lr, li = s5_a_re[l, dd], s5_a_im[l, dd]
            dt = jnp.exp(s5_log_dt[l, dd])[:, None]
            mag = jnp.exp(lr * dt)
            ab_re, ab_im = mag * jnp.cos(li * dt), mag * jnp.sin(li * dt)
            den = lr * lr + li * li
            nr, ni = ab_re - 1.0, ab_im
            cf_re = (nr * lr + ni * li) / den
            cf_im = (ni * lr - nr * li) / den
            bre, bim = s5_b_re[l, dd], s5_b_im[l, dd]
            bb_re = cf_re[..., None] * bre - cf_im[..., None] * bim
            bb_im = cf_re[..., None] * bim + cf_im[..., None] * bre
            bbs.append(jnp.concatenate([_block_diag(bb_re.transpose(0, 2, 1)),
                                        _block_diag(bb_im.transpose(0, 2, 1))], axis=1))
            abs_.append(jnp.concatenate([ab_re.reshape(-1), ab_im.reshape(-1)])[None, :])
            cfs.append(jnp.concatenate([_block_diag(s5_c_re[l, dd].transpose(0, 2, 1)),
                                        -_block_diag(s5_c_im[l, dd].transpose(0, 2, 1))], axis=0))
        bu = _s5_bu(p, jnp.concatenate(bbs, axis=1).astype(BF16), nb=nb, tpb=tpb)
        xf, xb = _s5_scan(bu, jnp.stack(abs_, axis=0), nb=nb, t_all=t_all, n_ctx=n_ctx)
        y_s5 = _s5_post(xf, xb, p, cfs[0].astype(BF16), cfs[1].astype(BF16), s5_d[l].reshape(1, 256),
                        s5_glu_w[l].astype(BF16), s5_glu_b[l][None, :], nb=nb, tpb=tpb, r0=r0, nt=nt)

        h = _merge(ya, y_rw, y_s5, yd, p, h, mods[2], w_branch[l].astype(BF16), w_out[l].astype(BF16),
                   nb=nb, tpb=tpb, r0=r0, nt=nt)

        wr = jnp.concatenate([router_g_w[l], router_e_w[l], jnp.zeros((d, 128 - MOE_GROUPS - MOE_EXPERTS), F32)], axis=1)
        wr_h = wr.astype(BF16)
        wr_m = (wr - wr_h.astype(F32)).astype(BF16)
        rbias = jnp.concatenate([router_g_b[l], router_e_b[l],
                                 jnp.zeros((128 - MOE_GROUPS - MOE_EXPERTS,), F32)])[None, :]
        h = _moe(h, norm2_g[l][None, :], mods[3], mods[4], mods[5], wr_h, wr_m, rbias,
                 exp_w_gate[l], exp_w_up[l], exp_w_down[l],
                 final_norm_g[None, :], nb=nb, nt=nt, ctx_first=not last, final=last)

    return h.reshape(nb, n_lat, d)
```

```python
import functools
import math

import jax
import jax.numpy as jnp
import numpy as np
from jax import lax
from jax.experimental import pallas as pl
from jax.experimental.pallas import tpu as pltpu

F32 = jnp.float32
BF16 = jnp.bfloat16

TM = 256
VMEM_LIMIT = 48 * 1024 * 1024

D_MODEL = 1024
GRID_W = 64
ROPE_DIM = 32
ROPE_BASE = 10000.0
NORM_EPS = 1e-6
MLA_HEADS, MLA_NOPE, MLA_ROPE, MLA_V = 4, 64, 32, 64
MLA_Q_LORA, MLA_KV_LORA = 256, 128
RW_HEADS, RW_HEAD = 4, 64
RW_LN_EPS = 64e-5
S5_GROUPS, S5_GROUP_CH, S5_STATE = 16, 16, 64
DIFF_HEADS, DIFF_HD = 4, 32
DIFF_EPS = 1e-5
MOE_GROUPS, MOE_PER_GROUP, MOE_TOPK = 4, 8, 2
MOE_EXPERTS = MOE_GROUPS * MOE_PER_GROUP
D_EXPERT = 512
MOE_BLK = 256
RW_CHUNK = 64
S5_CHUNK = 128
ATTN_TK_MAX = 2816
LOG2E = math.log2(math.e)

EXT_RW, EXT_MLA, EXT_S5, EXT_DIFF, EXT_GATE = 0, 1024, 1536, 1792, 3072
N_EXT = 7168


def _cparams(sem, vmem=VMEM_LIMIT):
    return pltpu.CompilerParams(dimension_semantics=sem, vmem_limit_bytes=vmem)


def _dot(a, b):
    return jnp.dot(a, b, preferred_element_type=F32)


def _dot_nt(a, b):
    return lax.dot_general(a, b, (((1,), (1,)), ((), ())), preferred_element_type=F32)


def _split_dot(x, e):
    hi = x.astype(BF16)
    mid = (x - hi.astype(F32)).astype(BF16)
    return _dot(hi, e) + _dot(mid, e)


def _block_ones(n, blk):
    r = lax.broadcasted_iota(jnp.int32, (n, n), 0) // blk
    c = lax.broadcasted_iota(jnp.int32, (n, n), 1) // blk
    return (r == c).astype(BF16)


def _sigmoid(x):
    return 1.0 / (1.0 + jnp.exp(-x))


def _silu(x):
    return x * _sigmoid(x)


def _softplus(x):
    return jnp.maximum(x, 0.0) + jnp.log(1.0 + jnp.exp(-jnp.abs(x)))


def _rms(x, g, eps):
    return x * lax.rsqrt(jnp.mean(x * x, axis=-1, keepdims=True) + eps) * g


def _mm_kernel(x_ref, w_ref, b_ref, o_ref, *, pre_silu):
    x = x_ref[...].astype(F32)
    if pre_silu:
        x = _silu(x)
    o_ref[...] = _dot(x.astype(BF16), w_ref[...]) + b_ref[...]


def _mm(x, w, b, *, tm, tn, pre_silu=False, name="mm"):
    m, k = x.shape
    n = w.shape[1]
    return pl.pallas_call(
        functools.partial(_mm_kernel, pre_silu=pre_silu),
        grid=(m // tm, n // tn),
        in_specs=[pl.BlockSpec((tm, k), lambda i, j: (i, 0)),
                  pl.BlockSpec((k, tn), lambda i, j: (0, j)),
                  pl.BlockSpec((1, tn), lambda i, j: (0, j))],
        out_specs=pl.BlockSpec((tm, tn), lambda i, j: (i, j)),
        out_shape=jax.ShapeDtypeStruct((m, n), F32),
        compiler_params=_cparams(("parallel", "arbitrary")),
        name=name,
    )(x, w, b)


def _inproj_kernel(h_ref, g_ref, sh_ref, sc_ref, w_ref, o_ref):
    x = _rms(h_ref[...], g_ref[...], NORM_EPS)
    xn = (x * (1.0 + sc_ref[0]) + sh_ref[0]).astype(BF16)
    o_ref[...] = _dot(xn, w_ref[...])


def _inproj(h, g, shift, scale, w_ext, *, nb, tpb):
    n = h.shape[0]
    tn = N_EXT // 2
    modmap = lambda j, b, r: (jnp.where(r == 0, nb, b), 0, 0)
    return pl.pallas_call(
        _inproj_kernel,
        grid=(N_EXT // tn, nb, tpb),
        in_specs=[pl.BlockSpec((TM, D_MODEL), lambda j, b, r: (b * tpb + r, 0)),
                  pl.BlockSpec((1, D_MODEL), lambda j, b, r: (0, 0)),
                  pl.BlockSpec((1, 1, D_MODEL), modmap),
                  pl.BlockSpec((1, 1, D_MODEL), modmap),
                  pl.BlockSpec((D_MODEL, tn), lambda j, b, r: (0, j))],
        out_specs=pl.BlockSpec((TM, tn), lambda j, b, r: (b * tpb + r, j)),
        out_shape=jax.ShapeDtypeStruct((n, N_EXT), F32),
        compiler_params=_cparams(("parallel", "parallel", "parallel")),
        name="inproj",
    )(h, g, shift, scale, w_ext)


def _mla_prep_kernel(p_ref, gq_ref, gkv_ref, wa_ref, wb_ref, wk_ref, wv_ref, pk_ref,
                     cq_ref, sq_ref, ck_ref, q_ref, k_ref, v_ref):
    seg = p_ref[...]
    nq = _rms(seg[:, :MLA_Q_LORA], gq_ref[...], NORM_EPS).astype(BF16)
    nkv = _rms(seg[:, MLA_Q_LORA:MLA_Q_LORA + MLA_KV_LORA], gkv_ref[...], NORM_EPS).astype(BF16)
    cq = jnp.concatenate([cq_ref[...]] * MLA_HEADS, axis=1)
    sq = jnp.concatenate([sq_ref[...]] * MLA_HEADS, axis=1)
    q = _dot(nq, wa_ref[...]) * cq + _dot(nq, wb_ref[...]) * sq
    q_ref[...] = q.astype(BF16)
    kr = (seg[:, 384:512] * ck_ref[...]).astype(BF16)
    k_ref[...] = (_dot(nkv, wk_ref[...]) + _dot(kr, pk_ref[...])).astype(BF16)
    v_ref[...] = _dot(nkv, wv_ref[...]).astype(BF16)


def _mla_prep(p, gq, gkv, wa, wb, wk, wv, pk, cq_tab, sq_tab, ck_tab, *, nb, tpb):
    n = p.shape[0]
    tok = lambda b, r: (b * tpb + r, 0)
    pos = lambda b, r: (r, 0)
    full = lambda b, r: (0, 0)
    return pl.pallas_call(
        _mla_prep_kernel,
        grid=(nb, tpb),
        in_specs=[pl.BlockSpec((TM, 512), lambda b, r: (b * tpb + r, EXT_MLA // 512)),
                  pl.BlockSpec((1, MLA_Q_LORA), full),
                  pl.BlockSpec((1, MLA_KV_LORA), full),
                  pl.BlockSpec((MLA_Q_LORA, 512), full),
                  pl.BlockSpec((MLA_Q_LORA, 512), full),
                  pl.BlockSpec((MLA_KV_LORA, 512), full),
                  pl.BlockSpec((MLA_KV_LORA, 256), full),
                  pl.BlockSpec((128, 512), full),
                  pl.BlockSpec((TM, 128), pos),
                  pl.BlockSpec((TM, 128), pos),
                  pl.BlockSpec((TM, 128), pos)],
        out_specs=[pl.BlockSpec((TM, 512), tok),
                   pl.BlockSpec((TM, 512), tok),
                   pl.BlockSpec((TM, 256), tok)],
        out_shape=[jax.ShapeDtypeStruct((n, 512), BF16),
                   jax.ShapeDtypeStruct((n, 512), BF16),
                   jax.ShapeDtypeStruct((n, 256), BF16)],
        compiler_params=_cparams(("parallel", "parallel")),
        name="mla_prep",
    )(p, gq, gkv, wa, wb, wk, wv, pk, cq_tab, sq_tab, ck_tab)


def _flash_mla_kernel(q_ref, k_ref, v_ref, o_ref, m_ref, l_ref, acc_ref):
    kk = pl.program_id(2)
    tq = q_ref.shape[0]

    @pl.when(kk == 0)
    def _():
        m_ref[...] = jnp.full(m_ref.shape, -jnp.inf, F32)
        l_ref[...] = jnp.zeros(l_ref.shape, F32)
        acc_ref[...] = jnp.zeros(acc_ref.shape, F32)

    ps, alphas = [], []
    for h in range(MLA_HEADS):
        s = _dot_nt(q_ref[:, h * 128:(h + 1) * 128], k_ref[:, h * 128:(h + 1) * 128])
        m_prev = m_ref[h]
        m_new = jnp.maximum(m_prev, jnp.max(s, axis=-1, keepdims=True))
        alpha = jnp.exp2(m_prev - m_new)
        p = jnp.exp2(s - m_new)
        l_ref[h] = alpha * l_ref[h] + jnp.sum(p, axis=-1, keepdims=True)
        m_ref[h] = m_new
        alphas.append(alpha)
        ps.append(p.astype(BF16))
    pv = _dot(jnp.concatenate(ps, axis=0), v_ref[...])
    acc_ref[...] = jnp.concatenate(alphas, axis=0) * acc_ref[...] + pv

    @pl.when(kk == pl.num_programs(2) - 1)
    def _():
        lane = lax.broadcasted_iota(jnp.int32, (tq, MLA_HEADS * MLA_V), 1)
        out = jnp.zeros((tq, MLA_HEADS * MLA_V), F32)
        for h in range(MLA_HEADS):
            out = jnp.where(lane // MLA_V == h, acc_ref[h * tq:(h + 1) * tq, :] / l_ref[h], out)
        o_ref[...] = out


def _flash_mla(q, k, v, *, nb, t_all, q_start, q_len, k_len, tk):
    tq = TM
    tpb_q, tpb_k = t_all // tq, t_all // tk
    q0 = q_start // tq
    nq = q_len // tq
    return pl.pallas_call(
        _flash_mla_kernel,
        grid=(nb, nq, k_len // tk),
        in_specs=[pl.BlockSpec((tq, 512), lambda b, i, kk: (b * tpb_q + q0 + i, 0)),
                  pl.BlockSpec((tk, 512), lambda b, i, kk: (b * tpb_k + kk, 0)),
                  pl.BlockSpec((tk, 256), lambda b, i, kk: (b * tpb_k + kk, 0))],
        out_specs=pl.BlockSpec((tq, 256), lambda b, i, kk: (b * nq + i, 0)),
        out_shape=jax.ShapeDtypeStruct((nb * q_len, 256), F32),
        scratch_shapes=[pltpu.VMEM((MLA_HEADS, tq, 1), F32), pltpu.VMEM((MLA_HEADS, tq, 1), F32),
                        pltpu.VMEM((MLA_HEADS * tq, 256), F32)],
        compiler_params=_cparams(("parallel", "parallel", "arbitrary")),
        name="flash_mla",
    )(q, k, v)


def _diff_prep(p, cos_tab, sin_tab, *, nb, tpb):
    n = p.shape[0]
    tok = lambda b, r: (b * tpb + r, 0)
    pos = lambda b, r: (r, 0)
    return pl.pallas_call(
        _diff_prep_kernel_cols,
        grid=(nb, tpb),
        in_specs=[pl.BlockSpec((TM, 256), lambda b, r, c=c: (b * tpb + r, EXT_DIFF // 256 + c))
                  for c in range(5)]
                 + [pl.BlockSpec((TM, 256), pos), pl.BlockSpec((TM, 256), pos)],
        out_specs=[pl.BlockSpec((TM, 256), tok)] * 3,
        out_shape=[jax.ShapeDtypeStruct((n, 256), BF16)] * 3,
        compiler_params=_cparams(("parallel", "parallel")),
        name="diff_prep",
    )(p, p, p, p, p, cos_tab, sin_tab)


def _diff_prep_kernel_cols(q_in, k_in, v_in, qr_in, kr_in, cos_ref, sin_ref, q_ref, k_ref, v_ref):
    cos, sin = cos_ref[...], sin_ref[...]
    scale = DIFF_HD ** -0.5 * LOG2E
    q_ref[...] = ((q_in[...] * cos + qr_in[...] * sin) * scale).astype(BF16)
    k_ref[...] = (k_in[...] * cos + kr_in[...] * sin).astype(BF16)
    v_ref[...] = v_in[...].astype(BF16)


def _flash_diff_kernel(q_ref, k_ref, v_ref, lam_ref, g_ref, o_ref, qs_ref, m_ref, l_ref, acc_ref, *, lam_init):
    kk = pl.program_id(2)
    tq = q_ref.shape[0]
    nsm = 2 * DIFF_HEADS

    @pl.when(kk == 0)
    def _():
        m_ref[...] = jnp.full(m_ref.shape, -jnp.inf, F32)
        l_ref[...] = jnp.zeros(l_ref.shape, F32)
        acc_ref[...] = jnp.zeros(acc_ref.shape, F32)
        q = q_ref[...]
        lane = lax.broadcasted_iota(jnp.int32, (tq, 256), 1)
        for i in range(nsm):
            qs_ref[i * tq:(i + 1) * tq, :] = jnp.where((lane // DIFF_HD) == i, q, jnp.zeros_like(q))

    k = k_ref[...]
    ps, alphas = [], []
    for i in range(nsm):
        rows = slice(i * tq, (i + 1) * tq)
        s = _dot_nt(qs_ref[rows, :], k)
        m_prev = m_ref[rows, :]
        m_new = jnp.maximum(m_prev, jnp.max(s, axis=-1, keepdims=True))
        alpha = jnp.exp2(m_prev - m_new)
        p = jnp.exp2(s - m_new)
        l_ref[rows, :] = alpha * l_ref[rows, :] + jnp.sum(p, axis=-1, keepdims=True)
        m_ref[rows, :] = m_new
        alphas.append(alpha)
        ps.append(p.astype(BF16))
    pv = _dot(jnp.concatenate(ps, axis=0), v_ref[...])
    acc_ref[...] = jnp.concatenate(alphas, axis=0) * acc_ref[...] + pv

    @pl.when(kk == pl.num_programs(2) - 1)
    def _():
        lane = lax.broadcasted_iota(jnp.int32, (tq, 256), 1)
        o = jnp.zeros((tq, 256), F32)
        for h in range(DIFF_HEADS):
            r0, r1 = (2 * h) * tq, (2 * h + 1) * tq
            o0 = acc_ref[r0:r0 + tq, :] / l_ref[r0:r0 + tq, :]
            o1 = acc_ref[r1:r1 + tq, :] / l_ref[r1:r1 + tq, :]
            o = jnp.where((lane // (2 * DIFF_HD)) == h, o0 - lam_ref[...] * o1, o)
        ms = _split_dot(o * o, _block_ones(256, 2 * DIFF_HD)) * (1.0 / (2 * DIFF_HD))
        o_ref[...] = o * lax.rsqrt(ms + DIFF_EPS) * g_ref[...] * (1.0 - lam_init)


def _flash_diff(q, k, v, lam_row, g_row, *, lam_init, nb, t_all, q_start, q_len, k_len, tk):
    tq = TM
    tpb_q, tpb_k = t_all // tq, t_all // tk
    q0 = q_start // tq
    nq = q_len // tq
    return pl.pallas_call(
        functools.partial(_flash_diff_kernel, lam_init=lam_init),
        grid=(nb, nq, k_len // tk),
        in_specs=[pl.BlockSpec((tq, 256), lambda b, i, kk: (b * tpb_q + q0 + i, 0)),
                  pl.BlockSpec((tk, 256), lambda b, i, kk: (b * tpb_k + kk, 0)),
                  pl.BlockSpec((tk, 256), lambda b, i, kk: (b * tpb_k + kk, 0)),
                  pl.BlockSpec((1, 256), lambda b, i, kk: (0, 0)),
                  pl.BlockSpec((1, 256), lambda b, i, kk: (0, 0))],
        out_specs=pl.BlockSpec((tq, 256), lambda b, i, kk: (b * nq + i, 0)),
        out_shape=jax.ShapeDtypeStruct((nb * q_len, 256), F32),
        scratch_shapes=[pltpu.VMEM((2 * DIFF_HEADS * tq, 256), BF16),
                        pltpu.VMEM((2 * DIFF_HEADS * tq, 1), F32),
                        pltpu.VMEM((2 * DIFF_HEADS * tq, 1), F32),
                        pltpu.VMEM((2 * DIFF_HEADS * tq, 256), F32)],
        compiler_params=_cparams(("parallel", "parallel", "arbitrary")),
        name="flash_diff",
    )(q, k, v, lam_row, g_row)


def _rw_prep_kernel(p_ref, prev_ref, next_ref, mu_ref, wlo_ref, g2_ref, vec_ref,
                    r_ref, v_ref, kk_ref, w0_ref, k0_ref, b0_ref, w1_ref, k1_ref, b1_ref,
                    bonus_ref, gate_ref, *, r0, lat_last):
    r = pl.program_id(1) + r0
    p = p_ref[...]
    row = lax.broadcasted_iota(jnp.int32, p.shape, 0)
    first_tile = jnp.logical_or(r == 0, r == 1)
    last_tile = jnp.logical_or(r == 0, r == lat_last)
    prev_row = jnp.where(first_tile, 0.0, prev_ref[7:8, :])
    next_row = jnp.where(last_tile, 0.0, next_ref[0:1, :])
    up = jnp.where(row == 0, prev_row, pltpu.roll(p, 1, 0))
    dn = jnp.where(row == TM - 1, next_row, pltpu.roll(p, TM - 1, 0))
    z = p + (0.5 * (up + dn) - p) * mu_ref[...]

    hw = RW_HEADS * RW_HEAD
    rr, k, v = z[:, :hw], z[:, hw:2 * hw], z[:, 2 * hw:3 * hw]
    lo = z[:, 3 * hw:3 * hw + 128]
    lane = lax.broadcasted_iota(jnp.int32, lo.shape, 1)
    lo = jnp.where(lane < 64, jnp.tanh(lo), lo).astype(BF16)
    wa = _dot(lo, wlo_ref[...])
    gate_ref[...] = _dot(_sigmoid(z[:, 3 * hw + 128:]).astype(BF16), g2_ref[...])

    e4 = _block_ones(hw, RW_HEAD)
    k_k, k_a, r_k = vec_ref[0:1, :], vec_ref[1:2, :], vec_ref[2:3, :]
    kk = k * k_k
    nrm = jnp.maximum(jnp.sqrt(_split_dot(kk * kk, e4)), 1e-12)
    kk = kk / nrm
    r_ref[...] = rr
    v_ref[...] = v
    kk_ref[...] = kk
    ksum = jnp.zeros_like(k)
    for d, (w_ref, kd_ref, b_ref) in enumerate(((w0_ref, k0_ref, b0_ref), (w1_ref, k1_ref, b1_ref))):
        w0 = vec_ref[3 + d:4 + d, :]
        a0 = vec_ref[5 + d:6 + d, :]
        wd = -_softplus(-(w0 + wa[:, d * hw:(d + 1) * hw])) - 0.5
        w_ref[...] = jnp.exp(-jnp.exp(wd))
        ad = _sigmoid(a0 + wa[:, (2 + d) * hw:(3 + d) * hw])
        kd = k * (1.0 + (ad - 1.0) * k_a)
        kd_ref[...] = kd
        b_ref[...] = kk * ad
        ksum = ksum + kd
    bonus_ref[...] = _split_dot(rr * ksum * r_k, e4) * v


def _rw_prep(p, mu, wlo, g2, vecs, *, nb, tpb, r0, nt):
    n_out = nb * nt * TM
    hw = RW_HEADS * RW_HEAD
    n_rows8 = p.shape[0] // 8
    tok = lambda b, r: (b * tpb + r0 + r, EXT_RW // 1024)
    prev = lambda b, r: (jnp.maximum((b * tpb + r0 + r) * (TM // 8) - 1, 0), EXT_RW // 1024)
    nxt = lambda b, r: (jnp.minimum((b * tpb + r0 + r + 1) * (TM // 8), n_rows8 - 1), EXT_RW // 1024)
    out = lambda b, r: (b * nt + r, 0)
    full = lambda b, r: (0, 0)
    return pl.pallas_call(
        functools.partial(_rw_prep_kernel, r0=r0, lat_last=tpb - 1),
        grid=(nb, nt),
        in_specs=[pl.BlockSpec((TM, 1024), tok),
                  pl.BlockSpec((8, 1024), prev),
                  pl.BlockSpec((8, 1024), nxt),
                  pl.BlockSpec((1, 1024), full),
                  pl.BlockSpec((128, 4 * hw), full),
                  pl.BlockSpec((128, hw), full),
                  pl.BlockSpec((8, hw), full)],
        out_specs=[pl.BlockSpec((TM, hw), out)] * 11,
        out_shape=[jax.ShapeDtypeStruct((n_out, hw), F32)] * 11,
        compiler_params=_cparams(("parallel", "parallel")),
        name="rw_prep",
    )(p, p, p, mu, wlo, g2, vecs)


def _rw_scan_kernel(rf, vf, kkf, wf, kf, bf, rb, vb, kkb, wb, kb, bb, yf_ref, yb_ref, s_ref, *, nb):
    c = RW_CHUNK

    @pl.when(pl.program_id(0) == 0)
    def _():
        s_ref[...] = jnp.zeros(s_ref.shape, F32)

    e2 = _block_ones(128, RW_HEAD)
    z2 = jnp.zeros((128, 128), BF16)
    left, right = jnp.concatenate([e2, z2], axis=1), jnp.concatenate([z2, e2], axis=1)
    rhs_pair1 = jnp.concatenate([left, right], axis=0)
    rhs_pair2 = jnp.concatenate([left, left, right, right], axis=0)
    e22 = jnp.concatenate([e2, e2], axis=0)
    lane = lax.broadcasted_iota(jnp.int32, (RW_HEAD, 128), 1)
    sub = lax.broadcasted_iota(jnp.int32, (RW_HEAD, 128), 0)
    diag = (lane % RW_HEAD) == sub
    sub8 = lax.broadcasted_iota(jnp.int32, (8, 128), 0)
    dirs = ((rf, vf, kkf, wf, kf, bf, yf_ref), (rb, vb, kkb, wb, kb, bb, yb_ref))

    def allreduce_rows(x):
        t = x[0:8]
        for i in range(1, 8):
            t = t + x[8 * i:8 * i + 8]
        for sh in (4, 2, 1):
            t = t + pltpu.roll(t, sh, 0)
        return t

    def col(row):
        return jnp.where(diag, row, 0.0).astype(BF16)

    def split(x):
        hi = x.astype(BF16).astype(F32)
        return hi, x - hi

    def group(g, carry):
        tiles, ytiles = {}, {}
        for d, refs in enumerate(dirs):
            base = pl.multiple_of((g if d == 0 else c // 8 - 1 - g) * 8, 8)
            if d == 0:
                prev = lambda x, sh: jnp.where(sub8 >= sh, pltpu.roll(x, sh, 0), 1.0)
            else:
                prev = lambda x, sh: jnp.where(sub8 < 8 - sh, pltpu.roll(x, 8 - sh, 0), 1.0)
            last = 7 if d == 0 else 0
            for b in range(nb):
                for hp in range(2):
                    r_, v_, kk_, w_, k_, b_ = [ref[b, pl.ds(base, 8), pl.ds(hp * 128, 128)] for ref in refs[:6]]
                    gam = w_
                    for sh in (1, 2, 4):
                        gam = gam * prev(gam, sh)
                    inv = 1.0 / gam
                    tiles[d, b, hp] = (base, v_, split(b_ * inv), split(kk_ * prev(gam, 1)), k_ * inv, r_ * gam,
                                       split(gam[last:last + 1, :]))
                    ytiles[d, b, hp] = jnp.zeros((8, 128), F32)
        units = [(d, b, hp) for d in range(2) for b in range(nb) for hp in range(2)]
        for jj in range(8):
            lhs2, lhs1 = [], []
            for (d, b, hp) in units:
                rw = slice(jj, jj + 1) if d == 0 else slice(7 - jj, 8 - jj)
                _, _, (bh, bm), (kkh, kkm), kh, rh, _ = tiles[d, b, hp]
                lhs2.append(jnp.concatenate([col(bh[rw]), col(bm[rw]), col(kkh[rw]), col(kkm[rw])], axis=1))
                lhs1.append(jnp.concatenate([col(kh[rw]), col(rh[rw])], axis=1))
            c2 = _dot(jnp.concatenate(lhs2, axis=0), rhs_pair2)
            c1 = _dot(jnp.concatenate(lhs1, axis=0), rhs_pair1)
            for u, (d, b, hp) in enumerate(units):
                j = jj if d == 0 else 7 - jj
                v_ = tiles[d, b, hp][1]
                rows = slice(u * RW_HEAD, (u + 1) * RW_HEAD)
                bc, kkc, kc, rc = c2[rows, :128], c2[rows, 128:], c1[rows, :128], c1[rows, 128:]
                s = s_ref[u]
                sa = jnp.concatenate([allreduce_rows(kkc * s)] * 8, axis=0)
                s = s - bc * sa + kc * v_[j:j + 1]
                s_ref[u] = s
                ytiles[d, b, hp] = jnp.where(sub8 == j, allreduce_rows(rc * s), ytiles[d, b, hp])
        for d, refs in enumerate(dirs):
            for b in range(nb):
                for hp in range(2):
                    u = (d * nb + b) * 2 + hp
                    gh, gm = tiles[d, b, hp][6]
                    s_ref[u] = s_ref[u] * _dot(jnp.concatenate([col(gh), col(gm)], axis=1), e22)
                    refs[6][b, pl.ds(tiles[d, b, hp][0], 8), pl.ds(hp * 128, 128)] = ytiles[d, b, hp]
        return carry

    lax.fori_loop(0, c // 8, group, 0)


def _rw_scan(r, v, kk, w0, k0, b0, w1, k1, b1, *, nb, t_all, n_ctx):
    c = RW_CHUNK
    nc, ncc = t_all // c, n_ctx // c
    hw = RW_HEADS * RW_HEAD
    shp = lambda a: a.reshape(nb, t_all, hw)
    fwd = lambda j: (0, j, 0)
    bwd = lambda j: (0, jnp.where(j < ncc, ncc - 1 - j, nc - 1 - (j - ncc)), 0)
    blk = (nb, c, hw)
    yshape = jax.ShapeDtypeStruct((nb, t_all, hw), F32)
    yf, yb = pl.pallas_call(
        functools.partial(_rw_scan_kernel, nb=nb),
        grid=(nc,),
        in_specs=[pl.BlockSpec(blk, fwd)] * 6 + [pl.BlockSpec(blk, bwd)] * 6,
        out_specs=[pl.BlockSpec(blk, fwd), pl.BlockSpec(blk, bwd)],
        out_shape=[yshape, yshape],
        scratch_shapes=[pltpu.VMEM((2 * nb * 2, RW_HEAD, 128), F32)],
        compiler_params=_cparams(("arbitrary",)),
        name="rw_scan",
    )(shp(r), shp(v), shp(kk), shp(w0), shp(k0), shp(b0),
      shp(r), shp(v), shp(kk), shp(w1), shp(k1), shp(b1))

    return yf.reshape(nb * t_all, hw), yb.reshape(nb * t_all, hw)


def _rw_post_kernel(yf_ref, yb_ref, bonus_ref, gate_ref, vec_ref, o_ref):
    e4 = _block_ones(RW_HEADS * RW_HEAD, RW_HEAD)
    y = yf_ref[...] + yb_ref[...]
    mean = _split_dot(y, e4) * (1.0 / RW_HEAD)
    yc = y - mean
    var = _split_dot(yc * yc, e4) * (1.0 / RW_HEAD)
    yn = yc * lax.rsqrt(var + RW_LN_EPS) * vec_ref[0:1, :] + vec_ref[1:2, :]
    o_ref[...] = (yn + bonus_ref[...]) * gate_ref[...]


def _rw_post(yf, yb, bonus, gate, vecs, *, nb, tpb_y, r0_y, nt):
    hw = RW_HEADS * RW_HEAD
    n_out = bonus.shape[0]
    ytok = lambda b, r: (b * tpb_y + r0_y + r, 0)
    tok = lambda b, r: (b * nt + r, 0)
    return pl.pallas_call(
        _rw_post_kernel,
        grid=(nb, nt),
        in_specs=[pl.BlockSpec((TM, hw), ytok), pl.BlockSpec((TM, hw), ytok),
                  pl.BlockSpec((TM, hw), tok), pl.BlockSpec((TM, hw), tok),
                  pl.BlockSpec((8, hw), lambda b, r: (0, 0))],
        out_specs=pl.BlockSpec((TM, hw), tok),
        out_shape=jax.ShapeDtypeStruct((n_out, hw), F32),
        compiler_params=_cparams(("parallel", "parallel")),
        name="rw_post",
    )(yf, yb, bonus, gate, vecs)


def _s5_bu_kernel(p_ref, w_ref, o_ref):
    o_ref[...] = _dot(p_ref[...].astype(BF16), w_ref[...])


def _s5_bu(p, w_b, *, nb, tpb):
    n = p.shape[0]
    nst = S5_GROUPS * S5_STATE
    return pl.pallas_call(
        _s5_bu_kernel,
        grid=(nb * tpb, 4),
        in_specs=[pl.BlockSpec((TM, 256), lambda i, j: (i, EXT_S5 // 256)),
                  pl.BlockSpec((256, nst), lambda i, j: (0, j))],
        out_specs=pl.BlockSpec((TM, nst), lambda i, j: (i, j)),
        out_shape=jax.ShapeDtypeStruct((n, 4 * nst), F32),
        compiler_params=_cparams(("parallel", "arbitrary")),
        name="s5_bu",
    )(p, w_b)


def _s5_scan_kernel(buf_ref, bub_ref, ab_ref, xf_ref, xb_ref, st_ref, *, nb):
    c = S5_CHUNK
    nst = S5_GROUPS * S5_STATE

    @pl.when(pl.program_id(0) == 0)
    def _():
        st_ref[...] = jnp.zeros(st_ref.shape, F32)

    dirs = ((buf_ref, xf_ref), (bub_ref, xb_ref))

    def group(g, carry):
        for d, (bu_ref, x_ref) in enumerate(dirs):
            base = pl.multiple_of((g if d == 0 else c // 8 - 1 - g) * 8, 8)
            ar = ab_ref[d, :, 0:nst]
            ai = ab_ref[d, :, nst:2 * nst]
            for b in range(nb):
                u = d * nb + b
                xr = st_ref[u, :, 0:nst]
                xi = st_ref[u, :, nst:2 * nst]
                bur = bu_ref[b, pl.ds(base, 8), 0:nst]
                bui = bu_ref[b, pl.ds(base, 8), nst:2 * nst]
                rows_r, rows_i = [None] * 8, [None] * 8
                for jj in range(8):
                    j = jj if d == 0 else 7 - jj
                    xr, xi = (ar * xr - ai * xi + bur[j:j + 1, :], ar * xi + ai * xr + bui[j:j + 1, :])
                    rows_r[j], rows_i[j] = xr, xi
                st_ref[u, :, 0:nst] = xr
                st_ref[u, :, nst:2 * nst] = xi
                x_ref[b, pl.ds(base, 8), 0:nst] = jnp.concatenate(rows_r, axis=0)
                x_ref[b, pl.ds(base, 8), nst:2 * nst] = jnp.concatenate(rows_i, axis=0)
        return carry

    lax.fori_loop(0, c // 8, group, 0)


def _s5_scan(bu, ab, *, nb, t_all, n_ctx):
    c = S5_CHUNK
    nc, ncc = t_all // c, n_ctx // c
    nst2 = 2 * S5_GROUPS * S5_STATE
    bu3 = bu.reshape(nb, t_all, 2 * nst2)
    fwd = lambda j: j
    bwd = lambda j: jnp.where(j < ncc, ncc - 1 - j, nc - 1 - (j - ncc))
    xshape = jax.ShapeDtypeStruct((nb, t_all, nst2), F32)
    xf, xb = pl.pallas_call(
        functools.partial(_s5_scan_kernel, nb=nb),
        grid=(nc,),
        in_specs=[pl.BlockSpec((nb, c, nst2), lambda j: (0, fwd(j), 0)),
                  pl.BlockSpec((nb, c, nst2), lambda j: (0, bwd(j), 1)),
                  pl.BlockSpec((2, 1, nst2), lambda j: (0, 0, 0))],
        out_specs=[pl.BlockSpec((nb, c, nst2), lambda j: (0, fwd(j), 0)),
                   pl.BlockSpec((nb, c, nst2), lambda j: (0, bwd(j), 0))],
        out_shape=[xshape, xshape],
        scratch_shapes=[pltpu.VMEM((2 * nb, 1, nst2), F32)],
        compiler_params=_cparams(("arbitrary",)),
        name="s5_scan",
    )(bu3, bu3, ab)
    return xf.reshape(nb * t_all, nst2), xb.reshape(nb * t_all, nst2)


def _s5_post_kernel(xf_ref, xb_ref, u_ref, cf_ref, cb_ref, d_ref, gw_ref, gb_ref, o_ref):
    y = (_dot(xf_ref[...].astype(BF16), cf_ref[...]) + _dot(xb_ref[...].astype(BF16), cb_ref[...])
         + d_ref[...] * u_ref[...])
    zg = 0.5 * y * (1.0 + jnp.tanh(math.sqrt(2.0 / math.pi) * (y + 0.044715 * (y * y * y))))
    o_ref[...] = zg * _sigmoid(_dot(zg.astype(BF16), gw_ref[...]) + gb_ref[...])


def _s5_post(xf, xb, p, cf, cb, d_row, glu_w, glu_b, *, nb, tpb, r0, nt):
    nst2 = 2 * S5_GROUPS * S5_STATE
    tok = lambda b, r: (b * tpb + r0 + r, 0)
    full = lambda b, r: (0, 0)
    return pl.pallas_call(
        _s5_post_kernel,
        grid=(nb, nt),
        in_specs=[pl.BlockSpec((TM, nst2), tok), pl.BlockSpec((TM, nst2), tok),
                  pl.BlockSpec((TM, 256), lambda b, r: (b * tpb + r0 + r, EXT_S5 // 256)),
                  pl.BlockSpec((nst2, 256), full), pl.BlockSpec((nst2, 256), full),
                  pl.BlockSpec((1, 256), full), pl.BlockSpec((256, 256), full),
                  pl.BlockSpec((1, 256), full)],
        out_specs=pl.BlockSpec((TM, 256), lambda b, r: (b * nt + r, 0)),
        out_shape=jax.ShapeDtypeStruct((nb * nt * TM, 256), F32),
        compiler_params=_cparams(("parallel", "parallel")),
        name="s5_post",
    )(xf, xb, p, cf, cb, d_row, glu_w, glu_b)


def _merge_kernel(ya_ref, yb_ref, ys_ref, yd_ref, g0, g1, g2, g3, h_ref, m2_ref, wb_ref, wo_ref, o_ref):
    acc = None
    for n, (y_ref, g_ref) in enumerate(((ya_ref, g0), (yb_ref, g1), (ys_ref, g2), (yd_ref, g3))):
        term = _sigmoid(g_ref[...]) * _dot(y_ref[...].astype(BF16), wb_ref[n])
        acc = term if acc is None else acc + term
    o_ref[...] = h_ref[...] + m2_ref[0] * _dot(acc.astype(BF16), wo_ref[...])


def _merge(ya, yb, ys, yd, p, h, mod2, w_branch, w_out, *, nb, tpb, r0, nt):
    tok_in = lambda b, r: (b * tpb + r0 + r, 0)
    tok_out = lambda b, r: (b * nt + r, 0)
    full2 = lambda b, r: (0, 0)
    gate = [pl.BlockSpec((TM, D_MODEL), lambda b, r, c=c: (b * tpb + r0 + r, EXT_GATE // D_MODEL + c))
            for c in range(4)]
    return pl.pallas_call(
        _merge_kernel,
        grid=(nb, nt),
        in_specs=[pl.BlockSpec((TM, 256), tok_out)] * 4 + gate
                 + [pl.BlockSpec((TM, D_MODEL), tok_in),
                    pl.BlockSpec((1, 1, D_MODEL), lambda b, r: (jnp.where(r0 + r == 0, nb, b), 0, 0)),
                    pl.BlockSpec((4, 256, D_MODEL), lambda b, r: (0, 0, 0)),
                    pl.BlockSpec((D_MODEL, D_MODEL), full2)],
        out_specs=pl.BlockSpec((TM, D_MODEL), tok_out),
        out_shape=jax.ShapeDtypeStruct((nb * nt * TM, D_MODEL), F32),
        compiler_params=_cparams(("parallel", "parallel")),
        name="merge",
    )(ya, yb, ys, yd, p, p, p, p, h, mod2, w_branch, w_out)


def _router_kernel(h_ref, g_ref, sh_ref, sc_ref, wh_ref, wm_ref, b_ref, x_ref, lg_ref):
    x = _rms(h_ref[...], g_ref[...], NORM_EPS) * (1.0 + sc_ref[0]) + sh_ref[0]
    xh = x.astype(BF16)
    xm = (x - xh.astype(F32)).astype(BF16)
    bits = lax.bitcast_convert_type(xh.astype(F32), jnp.uint32)
    half = D_MODEL // 2
    x_ref[...] = (bits[:, :half] >> 16) | (bits[:, half:] & jnp.uint32(0xFFFF0000))
    lg = (_dot(xh, wh_ref[...]) + _dot(xm, wh_ref[...]) + _dot(xh, wm_ref[...])) + b_ref[...]

    lane = lax.broadcasted_iota(jnp.int32, lg.shape, 1)
    lanef = lane.astype(F32)
    neg = jnp.float32(-jnp.inf)
    big = jnp.float32(1e9)
    rmax = lambda v: jnp.max(v, axis=-1, keepdims=True)
    rmin = lambda v: jnp.min(v, axis=-1, keepdims=True)
    rsum = lambda v: jnp.sum(v, axis=-1, keepdims=True)

    gmask = lane < MOE_GROUPS
    mg = rmax(jnp.where(gmask, lg, neg))
    eg = jnp.where(gmask, jnp.exp(lg - mg), 0.0)
    pg = eg / rsum(eg)
    pg_top = rmax(pg)
    g_sel = rmin(jnp.where(jnp.logical_and(gmask, pg == pg_top), lanef, big))
    lo = MOE_GROUPS + MOE_PER_GROUP * g_sel
    emask = jnp.logical_and(lanef >= lo, lanef < lo + MOE_PER_GROUP)
    me = rmax(jnp.where(emask, lg, neg))
    ee = jnp.where(emask, jnp.exp(lg - me), 0.0)
    pe = jnp.where(emask, ee / rsum(ee), -1.0)
    p1 = rmax(pe)
    i1 = rmin(jnp.where(pe == p1, lanef, big))
    pe2 = jnp.where(lanef == i1, -1.0, pe)
    p2 = rmax(pe2)
    i2 = rmin(jnp.where(pe2 == p2, lanef, big))
    den = p1 + p2
    out = jnp.where(lane == 0, i1 - MOE_GROUPS, 0.0)
    out = jnp.where(lane == 1, i2 - MOE_GROUPS, out)
    out = jnp.where(lane == 2, pg_top * p1 / den, out)
    out = jnp.where(lane == 3, pg_top * p2 / den, out)
    lg_ref[...] = out


def _router(h, g, shift, scale, wh, wm, bias, *, nb, nt, ctx_first):
    tok = lambda b, r: (b * nt + r, 0)
    full = lambda b, r: (0, 0)
    if ctx_first:
        modmap = lambda b, r: (jnp.where(r == 0, nb, b), 0, 0)
    else:
        modmap = lambda b, r: (b, 0, 0)
    n = h.shape[0]
    return pl.pallas_call(
        _router_kernel,
        grid=(nb, nt),
        in_specs=[pl.BlockSpec((TM, D_MODEL), tok), pl.BlockSpec((1, D_MODEL), full),
                  pl.BlockSpec((1, 1, D_MODEL), modmap), pl.BlockSpec((1, 1, D_MODEL), modmap),
                  pl.BlockSpec((D_MODEL, 128), full), pl.BlockSpec((D_MODEL, 128), full),
                  pl.BlockSpec((1, 128), full)],
        out_specs=[pl.BlockSpec((TM, D_MODEL // 2), tok), pl.BlockSpec((TM, 128), tok)],
        out_shape=[jax.ShapeDtypeStruct((n, D_MODEL // 2), jnp.uint32), jax.ShapeDtypeStruct((n, 128), F32)],
        compiler_params=_cparams(("parallel", "parallel")),
        name="router",
    )(h, g, shift, scale, wh, wm, bias)


def _expert_kernel(be_ref, nv_ref, x_ref, wg_ref, wu_ref, wd_ref, o_ref, wgb_ref, wub_ref, wdb_ref):
    i = pl.program_id(0)

    @pl.when(jnp.logical_or(i == 0, be_ref[i] != be_ref[jnp.maximum(i - 1, 0)]))
    def _():
        wgb_ref[...] = wg_ref[0].astype(BF16)
        wub_ref[...] = wu_ref[0].astype(BF16)
        wdb_ref[...] = wd_ref[0].astype(BF16)

    @pl.when(i < nv_ref[0])
    def _():
        u = x_ref[...]
        x = jnp.concatenate([lax.bitcast_convert_type(u << 16, F32),
                             lax.bitcast_convert_type(u & jnp.uint32(0xFFFF0000), F32)], axis=1).astype(BF16)
        hb = _silu(_dot(x, wgb_ref[...])) * _dot(x, wub_ref[...])
        o_ref[...] = _dot(hb.astype(BF16), wdb_ref[...])

    @pl.when(i >= nv_ref[0])
    def _():
        o_ref[...] = jnp.zeros(o_ref.shape, F32)


def _experts(xs, block_e, n_valid, w_gate, w_up, w_down):
    n_slots = xs.shape[0]
    n_blocks = n_slots // MOE_BLK
    wmap = lambda i, be, nv: (be[i], 0, 0)
    return pl.pallas_call(
        _expert_kernel,
        grid_spec=pltpu.PrefetchScalarGridSpec(
            num_scalar_prefetch=2,
            grid=(n_blocks,),
            in_specs=[pl.BlockSpec((MOE_BLK, D_MODEL // 2), lambda i, be, nv: (i, 0)),
                      pl.BlockSpec((1, D_MODEL, D_EXPERT), wmap),
                      pl.BlockSpec((1, D_MODEL, D_EXPERT), wmap),
                      pl.BlockSpec((1, D_EXPERT, D_MODEL), wmap)],
            out_specs=pl.BlockSpec((MOE_BLK, D_MODEL), lambda i, be, nv: (i, 0)),
            scratch_shapes=[pltpu.VMEM((D_MODEL, D_EXPERT), BF16), pltpu.VMEM((D_MODEL, D_EXPERT), BF16),
                            pltpu.VMEM((D_EXPERT, D_MODEL), BF16)]),
        out_shape=jax.ShapeDtypeStruct((n_slots, D_MODEL), F32),
        compiler_params=_cparams(("arbitrary",)),
        name="experts",
    )(block_e, n_valid, xs, w_gate, w_up, w_down)


def _combine_kernel(h_ref, y0_ref, y1_ref, w_ref, m5_ref, g_ref, o_ref, *, final):
    w = w_ref[...]
    y = y0_ref[...] * w[:, 0:1] + y1_ref[...] * w[:, 1:2]
    h = h_ref[...] + m5_ref[0] * y
    if final:
        h = _rms(h, g_ref[...], NORM_EPS)
    o_ref[...] = h


def _combine(h, y0, y1, wts, mod5, g_final, *, nb, nt, ctx_first, final):
    tok = lambda b, r: (b * nt + r, 0)
    if ctx_first:
        modmap = lambda b, r: (jnp.where(r == 0, nb, b), 0, 0)
    else:
        modmap = lambda b, r: (b, 0, 0)
    return pl.pallas_call(
        functools.partial(_combine_kernel, final=final),
        grid=(nb, nt),
        in_specs=[pl.BlockSpec((TM, D_MODEL), tok)] * 3
                 + [pl.BlockSpec((TM, 128), tok), pl.BlockSpec((1, 1, D_MODEL), modmap),
                    pl.BlockSpec((1, D_MODEL), lambda b, r: (0, 0))],
        out_specs=pl.BlockSpec((TM, D_MODEL), tok),
        out_shape=jax.ShapeDtypeStruct(h.shape, F32),
        compiler_params=_cparams(("parallel", "parallel")),
        name="combine",
    )(h, y0, y1, wts, mod5, g_final)


def _moe(h, g2, shift, scale, mod5, wh, wm, rbias, w_gate, w_up, w_down, g_final, *, layer, nb, nt, ctx_first,
         final):
    n = h.shape[0]
    x_bf, route = _router(h, g2, shift, scale, wh, wm, rbias, nb=nb, nt=nt, ctx_first=ctx_first)
    idx = route[:, :MOE_TOPK].astype(jnp.int32)
    wts = route[:, MOE_TOPK:2 * MOE_TOPK]
    n_as = n * MOE_TOPK
    flat_e = idx.reshape(n_as)
    onehot = (flat_e[:, None] == jnp.arange(MOE_EXPERTS, dtype=jnp.int32)[None, :]).astype(jnp.int32)
    csum = jnp.cumsum(onehot, axis=0)
    counts = csum[-1]
    rank = jnp.sum(jnp.where(onehot > 0, csum - 1, 0), axis=1)
    padded = (counts + MOE_BLK - 1) // MOE_BLK * MOE_BLK
    pad_end = jnp.cumsum(padded)
    pad_start = pad_end - padded
    slot = pad_start[flat_e] + rank
    n_blocks = (n_as + MOE_EXPERTS * (MOE_BLK - 1) + MOE_BLK - 1) // MOE_BLK
    n_slots = n_blocks * MOE_BLK
    slot_tok = jnp.zeros((n_slots,), jnp.int32).at[slot].set(jnp.arange(n_as, dtype=jnp.int32) // MOE_TOPK)
    starts = jnp.arange(n_blocks, dtype=jnp.int32) * MOE_BLK
    block_e = jnp.minimum(jnp.sum((pad_end[None, :] <= starts[:, None]).astype(jnp.int32), axis=1),
                          MOE_EXPERTS - 1)
    n_valid = (pad_end[-1:] // MOE_BLK).astype(jnp.int32)
    xs = jnp.take(x_bf, slot_tok, axis=0)
    ys = _experts(xs, block_e + layer * MOE_EXPERTS, n_valid, w_gate, w_up, w_down)
    slot2 = slot.reshape(n, MOE_TOPK)
    y0 = jnp.take(ys, slot2[:, 0], axis=0)
    y1 = jnp.take(ys, slot2[:, 1], axis=0)
    wts_pad = jnp.pad(wts.astype(F32), ((0, 0), (0, 128 - MOE_TOPK)))
    return _combine(h, y0, y1, wts_pad, mod5, g_final, nb=nb, nt=nt, ctx_first=ctx_first, final=final)


_ROT_SRC = np.array(list(range(8, 16)) + list(range(0, 8)) + list(range(24, 32)) + list(range(16, 24)))
_ROT_SIGN = np.array([-1.0] * 8 + [1.0] * 8 + [-1.0] * 8 + [1.0] * 8, np.float32)


def _rot_cols(w):
    k = w.shape[-1] // ROPE_DIM
    src = np.concatenate([_ROT_SRC + ROPE_DIM * i for i in range(k)])
    sign = np.tile(_ROT_SIGN, k)
    return w[..., src] * sign


def _rope_tables(n_ctx, n_lat):
    rows = n_lat // GRID_W
    row = jnp.repeat(jnp.arange(rows, dtype=F32), GRID_W)
    col = (jnp.arange(rows * GRID_W) % GRID_W).astype(F32)
    nf = ROPE_DIM // 4
    inv = ROPE_BASE ** (-jnp.arange(nf, dtype=F32) / nf)
    ar = row[:, None] * inv
    ac = col[:, None] * inv
    ang = jnp.concatenate([ar, ar, ac, ac], axis=-1)
    cos = jnp.concatenate([jnp.ones((n_ctx, ROPE_DIM), F32), jnp.cos(ang)], axis=0)
    sin = jnp.concatenate([jnp.zeros((n_ctx, ROPE_DIM), F32), jnp.sin(ang)], axis=0)
    return cos, sin


def _block_diag(blocks):
    g, a, b = blocks.shape
    tiled = jnp.tile(blocks.reshape(g * a, b), (1, g))
    rows = lax.broadcasted_iota(jnp.int32, (g * a, g * b), 0) // a
    cols = lax.broadcasted_iota(jnp.int32, (g * a, g * b), 1) // b
    return jnp.where(rows == cols, tiled, 0.0)


def _pick_tk(t_all):
    best = 128
    for tk in range(128, ATTN_TK_MAX + 1, 128):
        if t_all % tk == 0:
            best = tk
    return best


def kernel(x, c, ctx, c_ctx, w_mod, b_mod, norm1_g, norm2_g, w_in, mla_q_norm_g, mla_kv_norm_g, mla_w_uq, mla_w_ukv, rw_mu, rw_w0, rw_w2, rw_a0, rw_a2, rw_g2, rw_k_k, rw_k_a, rw_r_k, rw_lnx_g, rw_lnx_b, s5_a_re, s5_a_im, s5_log_dt, s5_b_re, s5_b_im, s5_c_re, s5_c_im, s5_d, s5_glu_w, s5_glu_b, diff_lq1, diff_lk1, diff_lq2, diff_lk2, diff_subln_g, w_branch, w_out, router_g_w, router_g_b, router_e_w, router_e_b, exp_w_gate, exp_w_up, exp_w_down, final_norm_g):
    nb, n_lat, d = x.shape
    n_ctx = ctx.shape[1]
    depth = w_mod.shape[0]
    t_all = n_ctx + n_lat
    assert d == D_MODEL and n_ctx == TM and n_lat % TM == 0
    tpb = t_all // TM
    tk = _pick_tk(t_all)
    hw = RW_HEADS * RW_HEAD

    cos, sin = _rope_tables(n_ctx, n_lat)
    mla_scale = (MLA_NOPE + MLA_ROPE) ** -0.5 * LOG2E
    z32 = jnp.zeros((t_all, 32), F32)
    cq_tab = jnp.concatenate([jnp.ones((t_all, 64), F32), cos, z32], axis=1) * mla_scale
    sq_tab = jnp.concatenate([jnp.zeros((t_all, 64), F32), sin, z32], axis=1) * mla_scale
    ck_tab = jnp.concatenate([cos, sin, jnp.zeros((t_all, 64), F32)], axis=1)
    dcos = jnp.tile(cos, (1, 8))
    dsin = jnp.tile(sin, (1, 8))

    c_rows = jnp.concatenate([c, c_ctx[None, :], jnp.zeros((8 - nb - 1, d), F32)], axis=0)

    h = jnp.concatenate([ctx, x], axis=1).reshape(nb * t_all, d)

    for l in range(depth):
        last = l == depth - 1
        r0, nt = (1, tpb - 1) if last else (0, tpb)

        mod = _mm(c_rows, w_mod[l].astype(BF16), b_mod[l][None, :], tm=8, tn=1536, pre_silu=True, name="mod")
        mods = [mod[:nb + 1, i * d:(i + 1) * d].reshape(nb + 1, 1, d) for i in range(6)]

        wi = w_in[l]
        o_rw, o_s5, o_df, o_gt = 416, 1440, 1696, 2464
        w_kr = wi[:, 384:416]
        w_dq, w_dk, w_dv = wi[:, o_df:o_df + 256], wi[:, o_df + 256:o_df + 512], wi[:, o_df + 512:o_df + 768]
        w_ext = jnp.concatenate(
            [wi[:, o_rw:o_s5],
             wi[:, :416], _rot_cols(w_kr), jnp.zeros((d, 64), F32),
             wi[:, o_s5:o_df],
             w_dq, w_dk, w_dv, _rot_cols(w_dq), _rot_cols(w_dk),
             wi[:, o_gt:]], axis=1).astype(BF16)
        p = _inproj(h, norm1_g[l][None, :], mods[0], mods[1], w_ext, nb=nb, tpb=tpb)

        wq = mla_w_uq[l].reshape(MLA_Q_LORA, MLA_HEADS, MLA_NOPE + MLA_ROPE)
        zq = jnp.zeros((MLA_Q_LORA, MLA_HEADS, 32), F32)
        wa = jnp.concatenate([wq, zq], axis=2).reshape(MLA_Q_LORA, 512).astype(BF16)
        wb = jnp.concatenate([jnp.zeros((MLA_Q_LORA, MLA_HEADS, 64), F32), _rot_cols(wq[:, :, MLA_NOPE:]), zq],
                             axis=2).reshape(MLA_Q_LORA, 512).astype(BF16)
        wkv = mla_w_ukv[l].reshape(MLA_KV_LORA, MLA_HEADS, MLA_NOPE + MLA_V)
        wk = jnp.concatenate([wkv[:, :, :MLA_NOPE], jnp.zeros((MLA_KV_LORA, MLA_HEADS, 64), F32)],
                             axis=2).reshape(MLA_KV_LORA, 512).astype(BF16)
        wv = wkv[:, :, MLA_NOPE:].reshape(MLA_KV_LORA, MLA_HEADS * MLA_V).astype(BF16)
        pk_np = np.zeros((128, 512), np.float32)
        for hh in range(MLA_HEADS):
            for i in range(32):
                pk_np[i, hh * 128 + 64 + i] = 1.0
                pk_np[32 + i, hh * 128 + 64 + i] = 1.0
        pk = jnp.asarray(pk_np, BF16)
        q_m, k_m, v_m = _mla_prep(p, mla_q_norm_g[l][None, :], mla_kv_norm_g[l][None, :], wa, wb, wk, wv, pk,
                                  cq_tab, sq_tab, ck_tab, nb=nb, tpb=tpb)
        ya_lat = _flash_mla(q_m, k_m, v_m, nb=nb, t_all=t_all, q_start=n_ctx, q_len=n_lat, k_len=t_all, tk=tk)

        q_d, k_d, v_d = _diff_prep(p, dcos, dsin, nb=nb, tpb=tpb)
        lam_init = 0.8 - 0.6 * math.exp(-0.3 * l)
        lam = (jnp.exp(jnp.sum(diff_lq1[l] * diff_lk1[l])) - jnp.exp(jnp.sum(diff_lq2[l] * diff_lk2[l])) + lam_init)
        lam_row = jnp.full((1, 256), lam, F32)
        g_row = jnp.tile(diff_subln_g[l], DIFF_HEADS)[None, :]
        yd_lat = _flash_diff(q_d, k_d, v_d, lam_row, g_row, lam_init=lam_init, nb=nb, t_all=t_all,
                             q_start=n_ctx, q_len=n_lat, k_len=t_all, tk=tk)
        if last:
            ya, yd = ya_lat, yd_lat
        else:
            ya_ctx = _flash_mla(q_m, k_m, v_m, nb=nb, t_all=t_all, q_start=0, q_len=n_ctx, k_len=n_ctx, tk=n_ctx)
            yd_ctx = _flash_diff(q_d, k_d, v_d, lam_row, g_row, lam_init=lam_init, nb=nb, t_all=t_all,
                                 q_start=0, q_len=n_ctx, k_len=n_ctx, tk=n_ctx)
            comb = lambda a_c, a_l: jnp.concatenate(
                [a_c.reshape(nb, n_ctx, -1), a_l.reshape(nb, n_lat, -1)], axis=1).reshape(nb * t_all, -1)
            ya, yd = comb(ya_ctx, ya_lat), comb(yd_ctx, yd_lat)

        wlo = jnp.zeros((128, 4 * hw), F32)
        wlo = wlo.at[:64, 0:hw].set(rw_w2[l, 0]).at[:64, hw:2 * hw].set(rw_w2[l, 1])
        wlo = wlo.at[64:, 2 * hw:3 * hw].set(rw_a2[l, 0]).at[64:, 3 * hw:].set(rw_a2[l, 1])
        vecs = jnp.stack([rw_k_k[l], rw_k_a[l], rw_r_k[l].reshape(hw), rw_w0[l, 0], rw_w0[l, 1],
                          rw_a0[l, 0], rw_a0[l, 1], jnp.zeros((hw,), F32)], axis=0)
        (r_, v_, kk_, w0_, k0_, b0_, w1_, k1_, b1_, bonus, gate_rw) = _rw_prep(
            p, rw_mu[l][None, :], wlo.astype(BF16), rw_g2[l].astype(BF16), vecs, nb=nb, tpb=tpb, r0=0, nt=tpb)
        yf, yb_ = _rw_scan(r_, v_, kk_, w0_, k0_, b0_, w1_, k1_, b1_, nb=nb, t_all=t_all, n_ctx=n_ctx)
        ln_vecs = jnp.concatenate([rw_lnx_g[l][None, :], rw_lnx_b[l][None, :], jnp.zeros((6, hw), F32)], axis=0)
        if last:
            trim = lambda a: a.reshape(nb, t_all, hw)[:, n_ctx:].reshape(nb * n_lat, hw)
            bonus, gate_rw = trim(bonus), trim(gate_rw)
        y_rw = _rw_post(yf, yb_, bonus, gate_rw, ln_vecs, nb=nb, tpb_y=tpb, r0_y=r0, nt=nt)

        bbs, abs_, cfs = [], [], []
        for dd in range(2):
            lr, li = s5_a_re[l, dd], s5_a_im[l, dd]
            dt = jnp.exp(s5_log_dt[l, dd])[:, None]
            mag = jnp.exp(lr * dt)
            ab_re, ab_im = mag * jnp.cos(li * dt), mag * jnp.sin(li * dt)
            den = lr * lr + li * li
            nr, ni = ab_re - 1.0, ab_im
            cf_re = (nr * lr + ni * li) / den
            cf_im = (ni * lr - nr * li) / den
            bre, bim = s5_b_re[l, dd], s5_b_im[l, dd]
            bb_re = cf_re[..., None] * bre - cf_im[..., None] * bim
            bb_im = cf_re[..., None] * bim + cf_im[..., None] * bre
            bbs.append(jnp.concatenate([_block_diag(bb_re.transpose(0, 2, 1)),
                                        _block_diag(bb_im.transpose(0, 2, 1))], axis=1))
            abs_.append(jnp.concatenate([ab_re.reshape(-1), ab_im.reshape(-1)])[None, :])
            cfs.append(jnp.concatenate([_block_diag(s5_c_re[l, dd].transpose(0, 2, 1)),
                                        -_block_diag(s5_c_im[l, dd].transpose(0, 2, 1))], axis=0))
        bu = _s5_bu(p, jnp.concatenate(bbs, axis=1).astype(BF16), nb=nb, tpb=tpb)
        xf, xb = _s5_scan(bu, jnp.stack(abs_, axis=0), nb=nb, t_all=t_all, n_ctx=n_ctx)
        y_s5 = _s5_post(xf, xb, p, cfs[0].astype(BF16), cfs[1].astype(BF16), s5_d[l].reshape(1, 256),
                        s5_glu_w[l].astype(BF16), s5_glu_b[l][None, :], nb=nb, tpb=tpb, r0=r0, nt=nt)

        h = _merge(ya, y_rw, y_s5, yd, p, h, mods[2], w_branch[l].astype(BF16), w_out[l].astype(BF16),
                   nb=nb, tpb=tpb, r0=r0, nt=nt)

        wr = jnp.concatenate([router_g_w[l], router_e_w[l], jnp.zeros((d, 128 - MOE_GROUPS - MOE_EXPERTS), F32)], axis=1)
        wr_h = wr.astype(BF16)
        wr_m = (wr - wr_h.astype(F32)).astype(BF16)
        rbias = jnp.concatenate([router_g_b[l], router_e_b[l],
                                 jnp.zeros((128 - MOE_GROUPS - MOE_EXPERTS,), F32)])[None, :]
        h = _moe(h, norm2_g[l][None, :], mods[3], mods[4], mods[5], wr_h, wr_m, rbias,
                 exp_w_gate.reshape(depth * MOE_EXPERTS, d, D_EXPERT), exp_w_up.reshape(depth * MOE_EXPERTS, d, D_EXPERT),
                 exp_w_down.reshape(depth * MOE_EXPERTS, D_EXPERT, d),
                 final_norm_g[None, :], layer=l, nb=nb, nt=nt, ctx_first=not last, final=last)

    return h.reshape(nb, n_lat, d)
```

```python
import functools
import math

import jax
import jax.numpy as jnp
import numpy as np
from jax import lax
from jax.experimental import pallas as pl
from jax.experimental.pallas import tpu as pltpu

F32 = jnp.float32
BF16 = jnp.bfloat16

TM = 256
VMEM_LIMIT = 48 * 1024 * 1024

D_MODEL = 1024
GRID_W = 64
ROPE_DIM = 32
ROPE_BASE = 10000.0
NORM_EPS = 1e-6
MLA_HEADS, MLA_NOPE, MLA_ROPE, MLA_V = 4, 64, 32, 64
MLA_Q_LORA, MLA_KV_LORA = 256, 128
RW_HEADS, RW_HEAD = 4, 64
RW_LN_EPS = 64e-5
S5_GROUPS, S5_GROUP_CH, S5_STATE = 16, 16, 64
DIFF_HEADS, DIFF_HD = 4, 32
DIFF_EPS = 1e-5
MOE_GROUPS, MOE_PER_GROUP, MOE_TOPK = 4, 8, 2
MOE_EXPERTS = MOE_GROUPS * MOE_PER_GROUP
D_EXPERT = 512
MOE_BLK = 256
RW_CHUNK = 64
S5_CHUNK = 128
ATTN_TK_MAX = 2816
LOG2E = math.log2(math.e)

EXT_RW, EXT_MLA, EXT_S5, EXT_DIFF, EXT_GATE = 0, 1024, 1536, 1792, 3072
N_EXT = 7168


def _cparams(sem, vmem=VMEM_LIMIT):
    return pltpu.CompilerParams(dimension_semantics=sem, vmem_limit_bytes=vmem)


def _dot(a, b):
    return jnp.dot(a, b, preferred_element_type=F32)


def _dot_nt(a, b):
    return lax.dot_general(a, b, (((1,), (1,)), ((), ())), preferred_element_type=F32)


def _split_dot(x, e):
    hi = x.astype(BF16)
    mid = (x - hi.astype(F32)).astype(BF16)
    return _dot(hi, e) + _dot(mid, e)


def _block_ones(n, blk):
    r = lax.broadcasted_iota(jnp.int32, (n, n), 0) // blk
    c = lax.broadcasted_iota(jnp.int32, (n, n), 1) // blk
    return (r == c).astype(BF16)


def _sigmoid(x):
    return 1.0 / (1.0 + jnp.exp(-x))


def _silu(x):
    return x * _sigmoid(x)


def _softplus(x):
    return jnp.maximum(x, 0.0) + jnp.log(1.0 + jnp.exp(-jnp.abs(x)))


def _rms(x, g, eps):
    return x * lax.rsqrt(jnp.mean(x * x, axis=-1, keepdims=True) + eps) * g


def _mm_kernel(x_ref, w_ref, b_ref, o_ref, *, pre_silu):
    x = x_ref[...].astype(F32)
    if pre_silu:
        x = _silu(x)
    o_ref[...] = _dot(x.astype(BF16), w_ref[...]) + b_ref[...]


def _mm(x, w, b, *, tm, tn, pre_silu=False, name="mm"):
    m, k = x.shape
    n = w.shape[1]
    return pl.pallas_call(
        functools.partial(_mm_kernel, pre_silu=pre_silu),
        grid=(m // tm, n // tn),
        in_specs=[pl.BlockSpec((tm, k), lambda i, j: (i, 0)),
                  pl.BlockSpec((k, tn), lambda i, j: (0, j)),
                  pl.BlockSpec((1, tn), lambda i, j: (0, j))],
        out_specs=pl.BlockSpec((tm, tn), lambda i, j: (i, j)),
        out_shape=jax.ShapeDtypeStruct((m, n), F32),
        compiler_params=_cparams(("parallel", "arbitrary")),
        name=name,
    )(x, w, b)


def _inproj_kernel(h_ref, g_ref, sh_ref, sc_ref, w_ref, o_ref):
    x = _rms(h_ref[...], g_ref[...], NORM_EPS)
    xn = (x * (1.0 + sc_ref[0]) + sh_ref[0]).astype(BF16)
    o_ref[...] = _dot(xn, w_ref[...])


def _inproj(h, g, shift, scale, w_ext, *, nb, tpb):
    n = h.shape[0]
    tn = N_EXT // 2
    modmap = lambda j, b, r: (jnp.where(r == 0, nb, b), 0, 0)
    return pl.pallas_call(
        _inproj_kernel,
        grid=(N_EXT // tn, nb, tpb),
        in_specs=[pl.BlockSpec((TM, D_MODEL), lambda j, b, r: (b * tpb + r, 0)),
                  pl.BlockSpec((1, D_MODEL), lambda j, b, r: (0, 0)),
                  pl.BlockSpec((1, 1, D_MODEL), modmap),
                  pl.BlockSpec((1, 1, D_MODEL), modmap),
                  pl.BlockSpec((D_MODEL, tn), lambda j, b, r: (0, j))],
        out_specs=pl.BlockSpec((TM, tn), lambda j, b, r: (b * tpb + r, j)),
        out_shape=jax.ShapeDtypeStruct((n, N_EXT), F32),
        compiler_params=_cparams(("parallel", "parallel", "parallel")),
        name="inproj",
    )(h, g, shift, scale, w_ext)


def _mla_prep_kernel(p_ref, gq_ref, gkv_ref, wa_ref, wb_ref, wk_ref, wv_ref, pk_ref,
                     cq_ref, sq_ref, ck_ref, q_ref, k_ref, v_ref):
    seg = p_ref[...]
    nq = _rms(seg[:, :MLA_Q_LORA], gq_ref[...], NORM_EPS).astype(BF16)
    nkv = _rms(seg[:, MLA_Q_LORA:MLA_Q_LORA + MLA_KV_LORA], gkv_ref[...], NORM_EPS).astype(BF16)
    cq = jnp.concatenate([cq_ref[...]] * MLA_HEADS, axis=1)
    sq = jnp.concatenate([sq_ref[...]] * MLA_HEADS, axis=1)
    q = _dot(nq, wa_ref[...]) * cq + _dot(nq, wb_ref[...]) * sq
    q_ref[...] = q.astype(BF16)
    kr = (seg[:, 384:512] * ck_ref[...]).astype(BF16)
    k_ref[...] = (_dot(nkv, wk_ref[...]) + _dot(kr, pk_ref[...])).astype(BF16)
    v_ref[...] = _dot(nkv, wv_ref[...]).astype(BF16)


def _mla_prep(p, gq, gkv, wa, wb, wk, wv, pk, cq_tab, sq_tab, ck_tab, *, nb, tpb):
    n = p.shape[0]
    tok = lambda b, r: (b * tpb + r, 0)
    pos = lambda b, r: (r, 0)
    full = lambda b, r: (0, 0)
    return pl.pallas_call(
        _mla_prep_kernel,
        grid=(nb, tpb),
        in_specs=[pl.BlockSpec((TM, 512), lambda b, r: (b * tpb + r, EXT_MLA // 512)),
                  pl.BlockSpec((1, MLA_Q_LORA), full),
                  pl.BlockSpec((1, MLA_KV_LORA), full),
                  pl.BlockSpec((MLA_Q_LORA, 512), full),
                  pl.BlockSpec((MLA_Q_LORA, 512), full),
                  pl.BlockSpec((MLA_KV_LORA, 512), full),
                  pl.BlockSpec((MLA_KV_LORA, 256), full),
                  pl.BlockSpec((128, 512), full),
                  pl.BlockSpec((TM, 128), pos),
                  pl.BlockSpec((TM, 128), pos),
                  pl.BlockSpec((TM, 128), pos)],
        out_specs=[pl.BlockSpec((TM, 512), tok),
                   pl.BlockSpec((TM, 512), tok),
                   pl.BlockSpec((TM, 256), tok)],
        out_shape=[jax.ShapeDtypeStruct((n, 512), BF16),
                   jax.ShapeDtypeStruct((n, 512), BF16),
                   jax.ShapeDtypeStruct((n, 256), BF16)],
        compiler_params=_cparams(("parallel", "parallel")),
        name="mla_prep",
    )(p, gq, gkv, wa, wb, wk, wv, pk, cq_tab, sq_tab, ck_tab)


def _flash_mla_kernel(q_ref, k_ref, v_ref, o_ref, m_ref, l_ref, acc_ref):
    kk = pl.program_id(2)
    tq = q_ref.shape[0]

    @pl.when(kk == 0)
    def _():
        m_ref[...] = jnp.full(m_ref.shape, -jnp.inf, F32)
        l_ref[...] = jnp.zeros(l_ref.shape, F32)
        acc_ref[...] = jnp.zeros(acc_ref.shape, F32)

    ps, alphas = [], []
    for h in range(MLA_HEADS):
        s = _dot_nt(q_ref[:, h * 128:(h + 1) * 128], k_ref[:, h * 128:(h + 1) * 128])
        m_prev = m_ref[h]
        m_new = jnp.maximum(m_prev, jnp.max(s, axis=-1, keepdims=True))
        alpha = jnp.exp2(m_prev - m_new)
        p = jnp.exp2(s - m_new)
        l_ref[h] = alpha * l_ref[h] + jnp.sum(p, axis=-1, keepdims=True)
        m_ref[h] = m_new
        alphas.append(alpha)
        ps.append(p.astype(BF16))
    pv = _dot(jnp.concatenate(ps, axis=0), v_ref[...])
    acc_ref[...] = jnp.concatenate(alphas, axis=0) * acc_ref[...] + pv

    @pl.when(kk == pl.num_programs(2) - 1)
    def _():
        lane = lax.broadcasted_iota(jnp.int32, (tq, MLA_HEADS * MLA_V), 1)
        out = jnp.zeros((tq, MLA_HEADS * MLA_V), F32)
        for h in range(MLA_HEADS):
            out = jnp.where(lane // MLA_V == h, acc_ref[h * tq:(h + 1) * tq, :] / l_ref[h], out)
        o_ref[...] = out


def _flash_mla(q, k, v, *, nb, t_all, q_start, q_len, k_len, tk):
    tq = TM
    tpb_q, tpb_k = t_all // tq, t_all // tk
    q0 = q_start // tq
    nq = q_len // tq
    return pl.pallas_call(
        _flash_mla_kernel,
        grid=(nb, nq, k_len // tk),
        in_specs=[pl.BlockSpec((tq, 512), lambda b, i, kk: (b * tpb_q + q0 + i, 0)),
                  pl.BlockSpec((tk, 512), lambda b, i, kk: (b * tpb_k + kk, 0)),
                  pl.BlockSpec((tk, 256), lambda b, i, kk: (b * tpb_k + kk, 0))],
        out_specs=pl.BlockSpec((tq, 256), lambda b, i, kk: (b * nq + i, 0)),
        out_shape=jax.ShapeDtypeStruct((nb * q_len, 256), F32),
        scratch_shapes=[pltpu.VMEM((MLA_HEADS, tq, 1), F32), pltpu.VMEM((MLA_HEADS, tq, 1), F32),
                        pltpu.VMEM((MLA_HEADS * tq, 256), F32)],
        compiler_params=_cparams(("parallel", "parallel", "arbitrary")),
        name="flash_mla",
    )(q, k, v)


def _diff_prep(p, cos_tab, sin_tab, *, nb, tpb):
    n = p.shape[0]
    tok = lambda b, r: (b * tpb + r, 0)
    pos = lambda b, r: (r, 0)
    return pl.pallas_call(
        _diff_prep_kernel_cols,
        grid=(nb, tpb),
        in_specs=[pl.BlockSpec((TM, 256), lambda b, r, c=c: (b * tpb + r, EXT_DIFF // 256 + c))
                  for c in range(5)]
                 + [pl.BlockSpec((TM, 256), pos), pl.BlockSpec((TM, 256), pos)],
        out_specs=[pl.BlockSpec((TM, 256), tok)] * 3,
        out_shape=[jax.ShapeDtypeStruct((n, 256), BF16)] * 3,
        compiler_params=_cparams(("parallel", "parallel")),
        name="diff_prep",
    )(p, p, p, p, p, cos_tab, sin_tab)


def _diff_prep_kernel_cols(q_in, k_in, v_in, qr_in, kr_in, cos_ref, sin_ref, q_ref, k_ref, v_ref):
    cos, sin = cos_ref[...], sin_ref[...]
    scale = DIFF_HD ** -0.5 * LOG2E
    q_ref[...] = ((q_in[...] * cos + qr_in[...] * sin) * scale).astype(BF16)
    k_ref[...] = (k_in[...] * cos + kr_in[...] * sin).astype(BF16)
    v_ref[...] = v_in[...].astype(BF16)


def _flash_diff_kernel(q_ref, k_ref, v_ref, lam_ref, g_ref, o_ref, qs_ref, m_ref, l_ref, acc_ref, *, lam_init):
    kk = pl.program_id(2)
    tq = q_ref.shape[0]
    nsm = 2 * DIFF_HEADS

    @pl.when(kk == 0)
    def _():
        m_ref[...] = jnp.full(m_ref.shape, -jnp.inf, F32)
        l_ref[...] = jnp.zeros(l_ref.shape, F32)
        acc_ref[...] = jnp.zeros(acc_ref.shape, F32)
        q = q_ref[...]
        lane = lax.broadcasted_iota(jnp.int32, (tq, 256), 1)
        for i in range(nsm):
            qs_ref[i * tq:(i + 1) * tq, :] = jnp.where((lane // DIFF_HD) == i, q, jnp.zeros_like(q))

    k = k_ref[...]
    ps, alphas = [], []
    for i in range(nsm):
        rows = slice(i * tq, (i + 1) * tq)
        s = _dot_nt(qs_ref[rows, :], k)
        m_prev = m_ref[rows, :]
        m_new = jnp.maximum(m_prev, jnp.max(s, axis=-1, keepdims=True))
        alpha = jnp.exp2(m_prev - m_new)
        p = jnp.exp2(s - m_new)
        l_ref[rows, :] = alpha * l_ref[rows, :] + jnp.sum(p, axis=-1, keepdims=True)
        m_ref[rows, :] = m_new
        alphas.append(alpha)
        ps.append(p.astype(BF16))
    pv = _dot(jnp.concatenate(ps, axis=0), v_ref[...])
    acc_ref[...] = jnp.concatenate(alphas, axis=0) * acc_ref[...] + pv

    @pl.when(kk == pl.num_programs(2) - 1)
    def _():
        lane = lax.broadcasted_iota(jnp.int32, (tq, 256), 1)
        o = jnp.zeros((tq, 256), F32)
        for h in range(DIFF_HEADS):
            r0, r1 = (2 * h) * tq, (2 * h + 1) * tq
            o0 = acc_ref[r0:r0 + tq, :] / l_ref[r0:r0 + tq, :]
            o1 = acc_ref[r1:r1 + tq, :] / l_ref[r1:r1 + tq, :]
            o = jnp.where((lane // (2 * DIFF_HD)) == h, o0 - lam_ref[...] * o1, o)
        ms = _split_dot(o * o, _block_ones(256, 2 * DIFF_HD)) * (1.0 / (2 * DIFF_HD))
        o_ref[...] = o * lax.rsqrt(ms + DIFF_EPS) * g_ref[...] * (1.0 - lam_init)


def _flash_diff(q, k, v, lam_row, g_row, *, lam_init, nb, t_all, q_start, q_len, k_len, tk):
    tq = TM
    tpb_q, tpb_k = t_all // tq, t_all // tk
    q0 = q_start // tq
    nq = q_len // tq
    return pl.pallas_call(
        functools.partial(_flash_diff_kernel, lam_init=lam_init),
        grid=(nb, nq, k_len // tk),
        in_specs=[pl.BlockSpec((tq, 256), lambda b, i, kk: (b * tpb_q + q0 + i, 0)),
                  pl.BlockSpec((tk, 256), lambda b, i, kk: (b * tpb_k + kk, 0)),
                  pl.BlockSpec((tk, 256), lambda b, i, kk: (b * tpb_k + kk, 0)),
                  pl.BlockSpec((1, 256), lambda b, i, kk: (0, 0)),
                  pl.BlockSpec((1, 256), lambda b, i, kk: (0, 0))],
        out_specs=pl.BlockSpec((tq, 256), lambda b, i, kk: (b * nq + i, 0)),
        out_shape=jax.ShapeDtypeStruct((nb * q_len, 256), F32),
        scratch_shapes=[pltpu.VMEM((2 * DIFF_HEADS * tq, 256), BF16),
                        pltpu.VMEM((2 * DIFF_HEADS * tq, 1), F32),
                        pltpu.VMEM((2 * DIFF_HEADS * tq, 1), F32),
                        pltpu.VMEM((2 * DIFF_HEADS * tq, 256), F32)],
        compiler_params=_cparams(("parallel", "parallel", "arbitrary")),
        name="flash_diff",
    )(q, k, v, lam_row, g_row)


def _rw_prep_kernel(p_ref, prev_ref, next_ref, mu_ref, wlo_ref, g2_ref, vec_ref,
                    r_ref, v_ref, kk_ref, w0_ref, k0_ref, b0_ref, w1_ref, k1_ref, b1_ref,
                    bonus_ref, gate_ref, *, r0, lat_last):
    r = pl.program_id(1) + r0
    p = p_ref[...]
    row = lax.broadcasted_iota(jnp.int32, p.shape, 0)
    first_tile = jnp.logical_or(r == 0, r == 1)
    last_tile = jnp.logical_or(r == 0, r == lat_last)
    prev_row = jnp.where(first_tile, 0.0, prev_ref[7:8, :])
    next_row = jnp.where(last_tile, 0.0, next_ref[0:1, :])
    up = jnp.where(row == 0, prev_row, pltpu.roll(p, 1, 0))
    dn = jnp.where(row == TM - 1, next_row, pltpu.roll(p, TM - 1, 0))
    z = p + (0.5 * (up + dn) - p) * mu_ref[...]

    hw = RW_HEADS * RW_HEAD
    rr, k, v = z[:, :hw], z[:, hw:2 * hw], z[:, 2 * hw:3 * hw]
    lo = z[:, 3 * hw:3 * hw + 128]
    lane = lax.broadcasted_iota(jnp.int32, lo.shape, 1)
    lo = jnp.where(lane < 64, jnp.tanh(lo), lo).astype(BF16)
    wa = _dot(lo, wlo_ref[...])
    gate_ref[...] = _dot(_sigmoid(z[:, 3 * hw + 128:]).astype(BF16), g2_ref[...])

    e4 = _block_ones(hw, RW_HEAD)
    k_k, k_a, r_k = vec_ref[0:1, :], vec_ref[1:2, :], vec_ref[2:3, :]
    kk = k * k_k
    nrm = jnp.maximum(jnp.sqrt(_split_dot(kk * kk, e4)), 1e-12)
    kk = kk / nrm
    r_ref[...] = rr
    v_ref[...] = v
    kk_ref[...] = kk
    ksum = jnp.zeros_like(k)
    for d, (w_ref, kd_ref, b_ref) in enumerate(((w0_ref, k0_ref, b0_ref), (w1_ref, k1_ref, b1_ref))):
        w0 = vec_ref[3 + d:4 + d, :]
        a0 = vec_ref[5 + d:6 + d, :]
        wd = -_softplus(-(w0 + wa[:, d * hw:(d + 1) * hw])) - 0.5
        w_ref[...] = jnp.exp(-jnp.exp(wd))
        ad = _sigmoid(a0 + wa[:, (2 + d) * hw:(3 + d) * hw])
        kd = k * (1.0 + (ad - 1.0) * k_a)
        kd_ref[...] = kd
        b_ref[...] = kk * ad
        ksum = ksum + kd
    bonus_ref[...] = _split_dot(rr * ksum * r_k, e4) * v


def _rw_prep(p, mu, wlo, g2, vecs, *, nb, tpb, r0, nt):
    n_out = nb * nt * TM
    hw = RW_HEADS * RW_HEAD
    n_rows8 = p.shape[0] // 8
    tok = lambda b, r: (b * tpb + r0 + r, EXT_RW // 1024)
    prev = lambda b, r: (jnp.maximum((b * tpb + r0 + r) * (TM // 8) - 1, 0), EXT_RW // 1024)
    nxt = lambda b, r: (jnp.minimum((b * tpb + r0 + r + 1) * (TM // 8), n_rows8 - 1), EXT_RW // 1024)
    out = lambda b, r: (b * nt + r, 0)
    full = lambda b, r: (0, 0)
    return pl.pallas_call(
        functools.partial(_rw_prep_kernel, r0=r0, lat_last=tpb - 1),
        grid=(nb, nt),
        in_specs=[pl.BlockSpec((TM, 1024), tok),
                  pl.BlockSpec((8, 1024), prev),
                  pl.BlockSpec((8, 1024), nxt),
                  pl.BlockSpec((1, 1024), full),
                  pl.BlockSpec((128, 4 * hw), full),
                  pl.BlockSpec((128, hw), full),
                  pl.BlockSpec((8, hw), full)],
        out_specs=[pl.BlockSpec((TM, hw), out)] * 11,
        out_shape=[jax.ShapeDtypeStruct((n_out, hw), F32)] * 11,
        compiler_params=_cparams(("parallel", "parallel")),
        name="rw_prep",
    )(p, p, p, mu, wlo, g2, vecs)


def _rw_scan_kernel(rf, vf, kkf, wf, kf, bf, rb, vb, kkb, wb, kb, bb, yf_ref, yb_ref, s_ref, *, nb):
    c = RW_CHUNK

    @pl.when(pl.program_id(0) == 0)
    def _():
        s_ref[...] = jnp.zeros(s_ref.shape, F32)

    e2 = _block_ones(128, RW_HEAD)
    z2 = jnp.zeros((128, 128), BF16)
    left, right = jnp.concatenate([e2, z2], axis=1), jnp.concatenate([z2, e2], axis=1)
    rhs_pair1 = jnp.concatenate([left, right], axis=0)
    rhs_pair2 = jnp.concatenate([left, left, right, right], axis=0)
    e22 = jnp.concatenate([e2, e2], axis=0)
    lane = lax.broadcasted_iota(jnp.int32, (RW_HEAD, 128), 1)
    sub = lax.broadcasted_iota(jnp.int32, (RW_HEAD, 128), 0)
    diag = (lane % RW_HEAD) == sub
    sub8 = lax.broadcasted_iota(jnp.int32, (8, 128), 0)
    dirs = ((rf, vf, kkf, wf, kf, bf, yf_ref), (rb, vb, kkb, wb, kb, bb, yb_ref))

    def allreduce_rows(x):
        t = x[0:8]
        for i in range(1, 8):
            t = t + x[8 * i:8 * i + 8]
        for sh in (4, 2, 1):
            t = t + pltpu.roll(t, sh, 0)
        return t

    def col(row):
        return jnp.where(diag, row, 0.0).astype(BF16)

    def split(x):
        hi = x.astype(BF16).astype(F32)
        return hi, x - hi

    def group(g, carry):
        tiles, ytiles = {}, {}
        for d, refs in enumerate(dirs):
            base = pl.multiple_of((g if d == 0 else c // 8 - 1 - g) * 8, 8)
            if d == 0:
                prev = lambda x, sh: jnp.where(sub8 >= sh, pltpu.roll(x, sh, 0), 1.0)
            else:
                prev = lambda x, sh: jnp.where(sub8 < 8 - sh, pltpu.roll(x, 8 - sh, 0), 1.0)
            last = 7 if d == 0 else 0
            for b in range(nb):
                for hp in range(2):
                    r_, v_, kk_, w_, k_, b_ = [ref[b, pl.ds(base, 8), pl.ds(hp * 128, 128)] for ref in refs[:6]]
                    gam = w_
                    for sh in (1, 2, 4):
                        gam = gam * prev(gam, sh)
                    inv = 1.0 / gam
                    tiles[d, b, hp] = (base, v_, split(b_ * inv), split(kk_ * prev(gam, 1)), k_ * inv, r_ * gam,
                                       split(gam[last:last + 1, :]))
                    ytiles[d, b, hp] = jnp.zeros((8, 128), F32)
        units = [(d, b, hp) for d in range(2) for b in range(nb) for hp in range(2)]
        for jj in range(8):
            lhs2, lhs1 = [], []
            for (d, b, hp) in units:
                rw = slice(jj, jj + 1) if d == 0 else slice(7 - jj, 8 - jj)
                _, _, (bh, bm), (kkh, kkm), kh, rh, _ = tiles[d, b, hp]
                lhs2.append(jnp.concatenate([col(bh[rw]), col(bm[rw]), col(kkh[rw]), col(kkm[rw])], axis=1))
                lhs1.append(jnp.concatenate([col(kh[rw]), col(rh[rw])], axis=1))
            c2 = _dot(jnp.concatenate(lhs2, axis=0), rhs_pair2)
            c1 = _dot(jnp.concatenate(lhs1, axis=0), rhs_pair1)
            for u, (d, b, hp) in enumerate(units):
                j = jj if d == 0 else 7 - jj
                v_ = tiles[d, b, hp][1]
                rows = slice(u * RW_HEAD, (u + 1) * RW_HEAD)
                bc, kkc, kc, rc = c2[rows, :128], c2[rows, 128:], c1[rows, :128], c1[rows, 128:]
                s = s_ref[u]
                sa = jnp.concatenate([allreduce_rows(kkc * s)] * 8, axis=0)
                s = s - bc * sa + kc * v_[j:j + 1]
                s_ref[u] = s
                ytiles[d, b, hp] = jnp.where(sub8 == j, allreduce_rows(rc * s), ytiles[d, b, hp])
        for d, refs in enumerate(dirs):
            for b in range(nb):
                for hp in range(2):
                    u = (d * nb + b) * 2 + hp
                    gh, gm = tiles[d, b, hp][6]
                    s_ref[u] = s_ref[u] * _dot(jnp.concatenate([col(gh), col(gm)], axis=1), e22)
                    refs[6][b, pl.ds(tiles[d, b, hp][0], 8), pl.ds(hp * 128, 128)] = ytiles[d, b, hp]
        return carry

    lax.fori_loop(0, c // 8, group, 0)


def _rw_scan(r, v, kk, w0, k0, b0, w1, k1, b1, *, nb, t_all, n_ctx):
    c = RW_CHUNK
    nc, ncc = t_all // c, n_ctx // c
    hw = RW_HEADS * RW_HEAD
    shp = lambda a: a.reshape(nb, t_all, hw)
    fwd = lambda j: (0, j, 0)
    bwd = lambda j: (0, jnp.where(j < ncc, ncc - 1 - j, nc - 1 - (j - ncc)), 0)
    blk = (nb, c, hw)
    yshape = jax.ShapeDtypeStruct((nb, t_all, hw), F32)
    yf, yb = pl.pallas_call(
        functools.partial(_rw_scan_kernel, nb=nb),
        grid=(nc,),
        in_specs=[pl.BlockSpec(blk, fwd)] * 6 + [pl.BlockSpec(blk, bwd)] * 6,
        out_specs=[pl.BlockSpec(blk, fwd), pl.BlockSpec(blk, bwd)],
        out_shape=[yshape, yshape],
        scratch_shapes=[pltpu.VMEM((2 * nb * 2, RW_HEAD, 128), F32)],
        compiler_params=_cparams(("arbitrary",)),
        name="rw_scan",
    )(shp(r), shp(v), shp(kk), shp(w0), shp(k0), shp(b0),
      shp(r), shp(v), shp(kk), shp(w1), shp(k1), shp(b1))

    return yf.reshape(nb * t_all, hw), yb.reshape(nb * t_all, hw)


def _rw_post_kernel(yf_ref, yb_ref, bonus_ref, gate_ref, vec_ref, o_ref):
    e4 = _block_ones(RW_HEADS * RW_HEAD, RW_HEAD)
    y = yf_ref[...] + yb_ref[...]
    mean = _split_dot(y, e4) * (1.0 / RW_HEAD)
    yc = y - mean
    var = _split_dot(yc * yc, e4) * (1.0 / RW_HEAD)
    yn = yc * lax.rsqrt(var + RW_LN_EPS) * vec_ref[0:1, :] + vec_ref[1:2, :]
    o_ref[...] = (yn + bonus_ref[...]) * gate_ref[...]


def _rw_post(yf, yb, bonus, gate, vecs, *, nb, tpb_y, r0_y, nt):
    hw = RW_HEADS * RW_HEAD
    n_out = bonus.shape[0]
    ytok = lambda b, r: (b * tpb_y + r0_y + r, 0)
    tok = lambda b, r: (b * nt + r, 0)
    return pl.pallas_call(
        _rw_post_kernel,
        grid=(nb, nt),
        in_specs=[pl.BlockSpec((TM, hw), ytok), pl.BlockSpec((TM, hw), ytok),
                  pl.BlockSpec((TM, hw), tok), pl.BlockSpec((TM, hw), tok),
                  pl.BlockSpec((8, hw), lambda b, r: (0, 0))],
        out_specs=pl.BlockSpec((TM, hw), tok),
        out_shape=jax.ShapeDtypeStruct((n_out, hw), F32),
        compiler_params=_cparams(("parallel", "parallel")),
        name="rw_post",
    )(yf, yb, bonus, gate, vecs)


def _s5_scan_kernel(uf_ref, ub_ref, wb_ref, ab_ref, cf_ref, cb_ref, yf_ref, yb_ref, x_ref, st_ref, *, nb):
    c = S5_CHUNK
    nst = S5_GROUPS * S5_STATE

    @pl.when(pl.program_id(0) == 0)
    def _():
        st_ref[...] = jnp.zeros(st_ref.shape, F32)

    dirs = ((uf_ref, cf_ref, yf_ref), (ub_ref, cb_ref, yb_ref))
    for d, (u_ref, _, _) in enumerate(dirs):
        for b in range(nb):
            x_ref[d, b] = _dot(u_ref[b].astype(BF16), wb_ref[:, d * 2 * nst:(d + 1) * 2 * nst])

    def group(g, carry):
        for d in range(2):
            base = pl.multiple_of((g if d == 0 else c // 8 - 1 - g) * 8, 8)
            ar = ab_ref[d, :, 0:nst]
            ai = ab_ref[d, :, nst:2 * nst]
            for b in range(nb):
                u = d * nb + b
                xr = st_ref[u, :, 0:nst]
                xi = st_ref[u, :, nst:2 * nst]
                bur = x_ref[d, b, pl.ds(base, 8), 0:nst]
                bui = x_ref[d, b, pl.ds(base, 8), nst:2 * nst]
                rows_r, rows_i = [None] * 8, [None] * 8
                for jj in range(8):
                    j = jj if d == 0 else 7 - jj
                    xr, xi = (ar * xr - ai * xi + bur[j:j + 1, :], ar * xi + ai * xr + bui[j:j + 1, :])
                    rows_r[j], rows_i[j] = xr, xi
                st_ref[u, :, 0:nst] = xr
                st_ref[u, :, nst:2 * nst] = xi
                x_ref[d, b, pl.ds(base, 8), 0:nst] = jnp.concatenate(rows_r, axis=0)
                x_ref[d, b, pl.ds(base, 8), nst:2 * nst] = jnp.concatenate(rows_i, axis=0)
        return carry

    lax.fori_loop(0, c // 8, group, 0)

    for d, (_, c_ref, y_ref) in enumerate(dirs):
        for b in range(nb):
            y_ref[b] = _dot(x_ref[d, b].astype(BF16), c_ref[...])


def _s5_scan(p, w_b, ab, cf, cb, *, nb, t_all, n_ctx):
    c = S5_CHUNK
    nc, ncc = t_all // c, n_ctx // c
    nst2 = 2 * S5_GROUPS * S5_STATE
    width = S5_GROUPS * S5_GROUP_CH
    p3 = p.reshape(nb, t_all, N_EXT)
    fwd = lambda j: j
    bwd = lambda j: jnp.where(j < ncc, ncc - 1 - j, nc - 1 - (j - ncc))
    yshape = jax.ShapeDtypeStruct((nb, t_all, width), F32)
    full = lambda j: (0, 0)
    yf, yb = pl.pallas_call(
        functools.partial(_s5_scan_kernel, nb=nb),
        grid=(nc,),
        in_specs=[pl.BlockSpec((nb, c, width), lambda j: (0, fwd(j), EXT_S5 // width)),
                  pl.BlockSpec((nb, c, width), lambda j: (0, bwd(j), EXT_S5 // width)),
                  pl.BlockSpec((width, 2 * nst2), full),
                  pl.BlockSpec((2, 1, nst2), lambda j: (0, 0, 0)),
                  pl.BlockSpec((nst2, width), full), pl.BlockSpec((nst2, width), full)],
        out_specs=[pl.BlockSpec((nb, c, width), lambda j: (0, fwd(j), 0)),
                   pl.BlockSpec((nb, c, width), lambda j: (0, bwd(j), 0))],
        out_shape=[yshape, yshape],
        scratch_shapes=[pltpu.VMEM((2, nb, c, nst2), F32), pltpu.VMEM((2 * nb, 1, nst2), F32)],
        compiler_params=_cparams(("arbitrary",)),
        name="s5_scan",
    )(p3, p3, w_b, ab, cf, cb)
    return yf.reshape(nb * t_all, width), yb.reshape(nb * t_all, width)


def _s5_post_kernel(yf_ref, yb_ref, u_ref, d_ref, gw_ref, gb_ref, o_ref):
    y = yf_ref[...] + yb_ref[...] + d_ref[...] * u_ref[...]
    zg = 0.5 * y * (1.0 + jnp.tanh(math.sqrt(2.0 / math.pi) * (y + 0.044715 * (y * y * y))))
    o_ref[...] = zg * _sigmoid(_dot(zg.astype(BF16), gw_ref[...]) + gb_ref[...])


def _s5_post(yf, yb, p, d_row, glu_w, glu_b, *, nb, tpb, r0, nt):
    tok = lambda b, r: (b * tpb + r0 + r, 0)
    full = lambda b, r: (0, 0)
    return pl.pallas_call(
        _s5_post_kernel,
        grid=(nb, nt),
        in_specs=[pl.BlockSpec((TM, 256), tok), pl.BlockSpec((TM, 256), tok),
                  pl.BlockSpec((TM, 256), lambda b, r: (b * tpb + r0 + r, EXT_S5 // 256)),
                  pl.BlockSpec((1, 256), full), pl.BlockSpec((256, 256), full),
                  pl.BlockSpec((1, 256), full)],
        out_specs=pl.BlockSpec((TM, 256), lambda b, r: (b * nt + r, 0)),
        out_shape=jax.ShapeDtypeStruct((nb * nt * TM, 256), F32),
        compiler_params=_cparams(("parallel", "parallel")),
        name="s5_post",
    )(yf, yb, p, d_row, glu_w, glu_b)


def _merge_kernel(ya_ref, yb_ref, ys_ref, yd_ref, g0, g1, g2, g3, h_ref, m2_ref, wb_ref, wo_ref, o_ref):
    acc = None
    for n, (y_ref, g_ref) in enumerate(((ya_ref, g0), (yb_ref, g1), (ys_ref, g2), (yd_ref, g3))):
        term = _sigmoid(g_ref[...]) * _dot(y_ref[...].astype(BF16), wb_ref[n])
        acc = term if acc is None else acc + term
    o_ref[...] = h_ref[...] + m2_ref[0] * _dot(acc.astype(BF16), wo_ref[...])


def _merge(ya, yb, ys, yd, p, h, mod2, w_branch, w_out, *, nb, tpb, r0, nt):
    tok_in = lambda b, r: (b * tpb + r0 + r, 0)
    tok_out = lambda b, r: (b * nt + r, 0)
    full2 = lambda b, r: (0, 0)
    gate = [pl.BlockSpec((TM, D_MODEL), lambda b, r, c=c: (b * tpb + r0 + r, EXT_GATE // D_MODEL + c))
            for c in range(4)]
    return pl.pallas_call(
        _merge_kernel,
        grid=(nb, nt),
        in_specs=[pl.BlockSpec((TM, 256), tok_out)] * 4 + gate
                 + [pl.BlockSpec((TM, D_MODEL), tok_in),
                    pl.BlockSpec((1, 1, D_MODEL), lambda b, r: (jnp.where(r0 + r == 0, nb, b), 0, 0)),
                    pl.BlockSpec((4, 256, D_MODEL), lambda b, r: (0, 0, 0)),
                    pl.BlockSpec((D_MODEL, D_MODEL), full2)],
        out_specs=pl.BlockSpec((TM, D_MODEL), tok_out),
        out_shape=jax.ShapeDtypeStruct((nb * nt * TM, D_MODEL), F32),
        compiler_params=_cparams(("parallel", "parallel")),
        name="merge",
    )(ya, yb, ys, yd, p, p, p, p, h, mod2, w_branch, w_out)


def _router_kernel(h_ref, g_ref, sh_ref, sc_ref, wh_ref, wm_ref, b_ref, x_ref, lg_ref):
    x = _rms(h_ref[...], g_ref[...], NORM_EPS) * (1.0 + sc_ref[0]) + sh_ref[0]
    xh = x.astype(BF16)
    xm = (x - xh.astype(F32)).astype(BF16)
    bits = lax.bitcast_convert_type(xh.astype(F32), jnp.uint32)
    half = D_MODEL // 2
    x_ref[...] = (bits[:, :half] >> 16) | (bits[:, half:] & jnp.uint32(0xFFFF0000))
    lg = (_dot(xh, wh_ref[...]) + _dot(xm, wh_ref[...]) + _dot(xh, wm_ref[...])) + b_ref[...]

    lane = lax.broadcasted_iota(jnp.int32, lg.shape, 1)
    lanef = lane.astype(F32)
    neg = jnp.float32(-jnp.inf)
    big = jnp.float32(1e9)
    rmax = lambda v: jnp.max(v, axis=-1, keepdims=True)
    rmin = lambda v: jnp.min(v, axis=-1, keepdims=True)
    rsum = lambda v: jnp.sum(v, axis=-1, keepdims=True)

    gmask = lane < MOE_GROUPS
    mg = rmax(jnp.where(gmask, lg, neg))
    eg = jnp.where(gmask, jnp.exp(lg - mg), 0.0)
    pg = eg / rsum(eg)
    pg_top = rmax(pg)
    g_sel = rmin(jnp.where(jnp.logical_and(gmask, pg == pg_top), lanef, big))
    lo = MOE_GROUPS + MOE_PER_GROUP * g_sel
    emask = jnp.logical_and(lanef >= lo, lanef < lo + MOE_PER_GROUP)
    me = rmax(jnp.where(emask, lg, neg))
    ee = jnp.where(emask, jnp.exp(lg - me), 0.0)
    pe = jnp.where(emask, ee / rsum(ee), -1.0)
    p1 = rmax(pe)
    i1 = rmin(jnp.where(pe == p1, lanef, big))
    pe2 = jnp.where(lanef == i1, -1.0, pe)
    p2 = rmax(pe2)
    i2 = rmin(jnp.where(pe2 == p2, lanef, big))
    den = p1 + p2
    out = jnp.where(lane == 0, i1 - MOE_GROUPS, 0.0)
    out = jnp.where(lane == 1, i2 - MOE_GROUPS, out)
    out = jnp.where(lane == 2, pg_top * p1 / den, out)
    out = jnp.where(lane == 3, pg_top * p2 / den, out)
    lg_ref[...] = out


def _router(h, g, shift, scale, wh, wm, bias, *, nb, nt, ctx_first):
    tok = lambda b, r: (b * nt + r, 0)
    full = lambda b, r: (0, 0)
    if ctx_first:
        modmap = lambda b, r: (jnp.where(r == 0, nb, b), 0, 0)
    else:
        modmap = lambda b, r: (b, 0, 0)
    n = h.shape[0]
    return pl.pallas_call(
        _router_kernel,
        grid=(nb, nt),
        in_specs=[pl.BlockSpec((TM, D_MODEL), tok), pl.BlockSpec((1, D_MODEL), full),
                  pl.BlockSpec((1, 1, D_MODEL), modmap), pl.BlockSpec((1, 1, D_MODEL), modmap),
                  pl.BlockSpec((D_MODEL, 128), full), pl.BlockSpec((D_MODEL, 128), full),
                  pl.BlockSpec((1, 128), full)],
        out_specs=[pl.BlockSpec((TM, D_MODEL // 2), tok), pl.BlockSpec((TM, 128), tok)],
        out_shape=[jax.ShapeDtypeStruct((n, D_MODEL // 2), jnp.uint32), jax.ShapeDtypeStruct((n, 128), F32)],
        compiler_params=_cparams(("parallel", "parallel")),
        name="router",
    )(h, g, shift, scale, wh, wm, bias)


def _expert_kernel(be_ref, nv_ref, x_ref, wg_ref, wu_ref, wd_ref, o_ref, wgb_ref, wub_ref, wdb_ref):
    i = pl.program_id(0)

    @pl.when(jnp.logical_or(i == 0, be_ref[i] != be_ref[jnp.maximum(i - 1, 0)]))
    def _():
        wgb_ref[...] = wg_ref[0].astype(BF16)
        wub_ref[...] = wu_ref[0].astype(BF16)
        wdb_ref[...] = wd_ref[0].astype(BF16)

    @pl.when(i < nv_ref[0])
    def _():
        u = x_ref[...]
        x = jnp.concatenate([lax.bitcast_convert_type(u << 16, F32),
                             lax.bitcast_convert_type(u & jnp.uint32(0xFFFF0000), F32)], axis=1).astype(BF16)
        hb = _silu(_dot(x, wgb_ref[...])) * _dot(x, wub_ref[...])
        o_ref[...] = _dot(hb.astype(BF16), wdb_ref[...])

    @pl.when(i >= nv_ref[0])
    def _():
        o_ref[...] = jnp.zeros(o_ref.shape, F32)


def _experts(xs, block_e, n_valid, w_gate, w_up, w_down):
    n_slots = xs.shape[0]
    n_blocks = n_slots // MOE_BLK
    wmap = lambda i, be, nv: (be[i], 0, 0)
    return pl.pallas_call(
        _expert_kernel,
        grid_spec=pltpu.PrefetchScalarGridSpec(
            num_scalar_prefetch=2,
            grid=(n_blocks,),
            in_specs=[pl.BlockSpec((MOE_BLK, D_MODEL // 2), lambda i, be, nv: (i, 0)),
                      pl.BlockSpec((1, D_MODEL, D_EXPERT), wmap),
                      pl.BlockSpec((1, D_MODEL, D_EXPERT), wmap),
                      pl.BlockSpec((1, D_EXPERT, D_MODEL), wmap)],
            out_specs=pl.BlockSpec((MOE_BLK, D_MODEL), lambda i, be, nv: (i, 0)),
            scratch_shapes=[pltpu.VMEM((D_MODEL, D_EXPERT), BF16), pltpu.VMEM((D_MODEL, D_EXPERT), BF16),
                            pltpu.VMEM((D_EXPERT, D_MODEL), BF16)]),
        out_shape=jax.ShapeDtypeStruct((n_slots, D_MODEL), F32),
        compiler_params=_cparams(("arbitrary",)),
        name="experts",
    )(block_e, n_valid, xs, w_gate, w_up, w_down)


def _combine_kernel(h_ref, y0_ref, y1_ref, w_ref, m5_ref, g_ref, o_ref, *, final):
    w = w_ref[...]
    y = y0_ref[...] * w[:, 0:1] + y1_ref[...] * w[:, 1:2]
    h = h_ref[...] + m5_ref[0] * y
    if final:
        h = _rms(h, g_ref[...], NORM_EPS)
    o_ref[...] = h


def _combine(h, y0, y1, wts, mod5, g_final, *, nb, nt, ctx_first, final):
    tok = lambda b, r: (b * nt + r, 0)
    if ctx_first:
        modmap = lambda b, r: (jnp.where(r == 0, nb, b), 0, 0)
    else:
        modmap = lambda b, r: (b, 0, 0)
    return pl.pallas_call(
        functools.partial(_combine_kernel, final=final),
        grid=(nb, nt),
        in_specs=[pl.BlockSpec((TM, D_MODEL), tok)] * 3
                 + [pl.BlockSpec((TM, 128), tok), pl.BlockSpec((1, 1, D_MODEL), modmap),
                    pl.BlockSpec((1, D_MODEL), lambda b, r: (0, 0))],
        out_specs=pl.BlockSpec((TM, D_MODEL), tok),
        out_shape=jax.ShapeDtypeStruct(h.shape, F32),
        compiler_params=_cparams(("parallel", "parallel")),
        name="combine",
    )(h, y0, y1, wts, mod5, g_final)


def _moe(h, g2, shift, scale, mod5, wh, wm, rbias, w_gate, w_up, w_down, g_final, *, layer, nb, nt, ctx_first,
         final):
    n = h.shape[0]
    x_bf, route = _router(h, g2, shift, scale, wh, wm, rbias, nb=nb, nt=nt, ctx_first=ctx_first)
    idx = route[:, :MOE_TOPK].astype(jnp.int32)
    wts = route[:, MOE_TOPK:2 * MOE_TOPK]
    n_as = n * MOE_TOPK
    flat_e = idx.reshape(n_as)
    onehot = (flat_e[:, None] == jnp.arange(MOE_EXPERTS, dtype=jnp.int32)[None, :]).astype(jnp.int32)
    csum = jnp.cumsum(onehot, axis=0)
    counts = csum[-1]
    rank = jnp.sum(jnp.where(onehot > 0, csum - 1, 0), axis=1)
    padded = (counts + MOE_BLK - 1) // MOE_BLK * MOE_BLK
    pad_end = jnp.cumsum(padded)
    pad_start = pad_end - padded
    slot = pad_start[flat_e] + rank
    n_blocks = (n_as + MOE_EXPERTS * (MOE_BLK - 1) + MOE_BLK - 1) // MOE_BLK
    n_slots = n_blocks * MOE_BLK
    slot_tok = jnp.zeros((n_slots,), jnp.int32).at[slot].set(jnp.arange(n_as, dtype=jnp.int32) // MOE_TOPK)
    starts = jnp.arange(n_blocks, dtype=jnp.int32) * MOE_BLK
    block_e = jnp.minimum(jnp.sum((pad_end[None, :] <= starts[:, None]).astype(jnp.int32), axis=1),
                          MOE_EXPERTS - 1)
    n_valid = (pad_end[-1:] // MOE_BLK).astype(jnp.int32)
    xs = jnp.take(x_bf, slot_tok, axis=0)
    ys = _experts(xs, block_e + layer * MOE_EXPERTS, n_valid, w_gate, w_up, w_down)
    slot2 = slot.reshape(n, MOE_TOPK)
    y0 = jnp.take(ys, slot2[:, 0], axis=0)
    y1 = jnp.take(ys, slot2[:, 1], axis=0)
    wts_pad = jnp.pad(wts.astype(F32), ((0, 0), (0, 128 - MOE_TOPK)))
    return _combine(h, y0, y1, wts_pad, mod5, g_final, nb=nb, nt=nt, ctx_first=ctx_first, final=final)


_ROT_SRC = np.array(list(range(8, 16)) + list(range(0, 8)) + list(range(24, 32)) + list(range(16, 24)))
_ROT_SIGN = np.array([-1.0] * 8 + [1.0] * 8 + [-1.0] * 8 + [1.0] * 8, np.float32)


def _rot_cols(w):
    k = w.shape[-1] // ROPE_DIM
    src = np.concatenate([_ROT_SRC + ROPE_DIM * i for i in range(k)])
    sign = np.tile(_ROT_SIGN, k)
    return w[..., src] * sign


def _rope_tables(n_ctx, n_lat):
    rows = n_lat // GRID_W
    row = jnp.repeat(jnp.arange(rows, dtype=F32), GRID_W)
    col = (jnp.arange(rows * GRID_W) % GRID_W).astype(F32)
    nf = ROPE_DIM // 4
    inv = ROPE_BASE ** (-jnp.arange(nf, dtype=F32) / nf)
    ar = row[:, None] * inv
    ac = col[:, None] * inv
    ang = jnp.concatenate([ar, ar, ac, ac], axis=-1)
    cos = jnp.concatenate([jnp.ones((n_ctx, ROPE_DIM), F32), jnp.cos(ang)], axis=0)
    sin = jnp.concatenate([jnp.zeros((n_ctx, ROPE_DIM), F32), jnp.sin(ang)], axis=0)
    return cos, sin


def _block_diag(blocks):
    g, a, b = blocks.shape
    tiled = jnp.tile(blocks.reshape(g * a, b), (1, g))
    rows = lax.broadcasted_iota(jnp.int32, (g * a, g * b), 0) // a
    cols = lax.broadcasted_iota(jnp.int32, (g * a, g * b), 1) // b
    return jnp.where(rows == cols, tiled, 0.0)


def _pick_tk(t_all):
    best = 128
    for tk in range(128, ATTN_TK_MAX + 1, 128):
        if t_all % tk == 0:
            best = tk
    return best


def kernel(x, c, ctx, c_ctx, w_mod, b_mod, norm1_g, norm2_g, w_in, mla_q_norm_g, mla_kv_norm_g, mla_w_uq, mla_w_ukv, rw_mu, rw_w0, rw_w2, rw_a0, rw_a2, rw_g2, rw_k_k, rw_k_a, rw_r_k, rw_lnx_g, rw_lnx_b, s5_a_re, s5_a_im, s5_log_dt, s5_b_re, s5_b_im, s5_c_re, s5_c_im, s5_d, s5_glu_w, s5_glu_b, diff_lq1, diff_lk1, diff_lq2, diff_lk2, diff_subln_g, w_branch, w_out, router_g_w, router_g_b, router_e_w, router_e_b, exp_w_gate, exp_w_up, exp_w_down, final_norm_g):
    nb, n_lat, d = x.shape
    n_ctx = ctx.shape[1]
    depth = w_mod.shape[0]
    t_all = n_ctx + n_lat
    assert d == D_MODEL and n_ctx == TM and n_lat % TM == 0
    tpb = t_all // TM
    tk = _pick_tk(t_all)
    hw = RW_HEADS * RW_HEAD

    cos, sin = _rope_tables(n_ctx, n_lat)
    mla_scale = (MLA_NOPE + MLA_ROPE) ** -0.5 * LOG2E
    z32 = jnp.zeros((t_all, 32), F32)
    cq_tab = jnp.concatenate([jnp.ones((t_all, 64), F32), cos, z32], axis=1) * mla_scale
    sq_tab = jnp.concatenate([jnp.zeros((t_all, 64), F32), sin, z32], axis=1) * mla_scale
    ck_tab = jnp.concatenate([cos, sin, jnp.zeros((t_all, 64), F32)], axis=1)
    dcos = jnp.tile(cos, (1, 8))
    dsin = jnp.tile(sin, (1, 8))

    c_rows = jnp.concatenate([c, c_ctx[None, :], jnp.zeros((8 - nb - 1, d), F32)], axis=0)

    h = jnp.concatenate([ctx, x], axis=1).reshape(nb * t_all, d)

    for l in range(depth):
        last = l == depth - 1
        r0, nt = (1, tpb - 1) if last else (0, tpb)

        mod = _mm(c_rows, w_mod[l].astype(BF16), b_mod[l][None, :], tm=8, tn=1536, pre_silu=True, name="mod")
        mods = [mod[:nb + 1, i * d:(i + 1) * d].reshape(nb + 1, 1, d) for i in range(6)]

        wi = w_in[l]
        o_rw, o_s5, o_df, o_gt = 416, 1440, 1696, 2464
        w_kr = wi[:, 384:416]
        w_dq, w_dk, w_dv = wi[:, o_df:o_df + 256], wi[:, o_df + 256:o_df + 512], wi[:, o_df + 512:o_df + 768]
        w_ext = jnp.concatenate(
            [wi[:, o_rw:o_s5],
             wi[:, :416], _rot_cols(w_kr), jnp.zeros((d, 64), F32),
             wi[:, o_s5:o_df],
             w_dq, w_dk, w_dv, _rot_cols(w_dq), _rot_cols(w_dk),
             wi[:, o_gt:]], axis=1).astype(BF16)
        p = _inproj(h, norm1_g[l][None, :], mods[0], mods[1], w_ext, nb=nb, tpb=tpb)

        wq = mla_w_uq[l].reshape(MLA_Q_LORA, MLA_HEADS, MLA_NOPE + MLA_ROPE)
        zq = jnp.zeros((MLA_Q_LORA, MLA_HEADS, 32), F32)
        wa = jnp.concatenate([wq, zq], axis=2).reshape(MLA_Q_LORA, 512).astype(BF16)
        wb = jnp.concatenate([jnp.zeros((MLA_Q_LORA, MLA_HEADS, 64), F32), _rot_cols(wq[:, :, MLA_NOPE:]), zq],
                             axis=2).reshape(MLA_Q_LORA, 512).astype(BF16)
        wkv = mla_w_ukv[l].reshape(MLA_KV_LORA, MLA_HEADS, MLA_NOPE + MLA_V)
        wk = jnp.concatenate([wkv[:, :, :MLA_NOPE], jnp.zeros((MLA_KV_LORA, MLA_HEADS, 64), F32)],
                             axis=2).reshape(MLA_KV_LORA, 512).astype(BF16)
        wv = wkv[:, :, MLA_NOPE:].reshape(MLA_KV_LORA, MLA_HEADS * MLA_V).astype(BF16)
        pk_np = np.zeros((128, 512), np.float32)
        for hh in range(MLA_HEADS):
            for i in range(32):
                pk_np[i, hh * 128 + 64 + i] = 1.0
                pk_np[32 + i, hh * 128 + 64 + i] = 1.0
        pk = jnp.asarray(pk_np, BF16)
        q_m, k_m, v_m = _mla_prep(p, mla_q_norm_g[l][None, :], mla_kv_norm_g[l][None, :], wa, wb, wk, wv, pk,
                                  cq_tab, sq_tab, ck_tab, nb=nb, tpb=tpb)
        ya_lat = _flash_mla(q_m, k_m, v_m, nb=nb, t_all=t_all, q_start=n_ctx, q_len=n_lat, k_len=t_all, tk=tk)

        q_d, k_d, v_d = _diff_prep(p, dcos, dsin, nb=nb, tpb=tpb)
        lam_init = 0.8 - 0.6 * math.exp(-0.3 * l)
        lam = (jnp.exp(jnp.sum(diff_lq1[l] * diff_lk1[l])) - jnp.exp(jnp.sum(diff_lq2[l] * diff_lk2[l])) + lam_init)
        lam_row = jnp.full((1, 256), lam, F32)
        g_row = jnp.tile(diff_subln_g[l], DIFF_HEADS)[None, :]
        yd_lat = _flash_diff(q_d, k_d, v_d, lam_row, g_row, lam_init=lam_init, nb=nb, t_all=t_all,
                             q_start=n_ctx, q_len=n_lat, k_len=t_all, tk=tk)
        if last:
            ya, yd = ya_lat, yd_lat
        else:
            ya_ctx = _flash_mla(q_m, k_m, v_m, nb=nb, t_all=t_all, q_start=0, q_len=n_ctx, k_len=n_ctx, tk=n_ctx)
            yd_ctx = _flash_diff(q_d, k_d, v_d, lam_row, g_row, lam_init=lam_init, nb=nb, t_all=t_all,
                                 q_start=0, q_len=n_ctx, k_len=n_ctx, tk=n_ctx)
            comb = lambda a_c, a_l: jnp.concatenate(
                [a_c.reshape(nb, n_ctx, -1), a_l.reshape(nb, n_lat, -1)], axis=1).reshape(nb * t_all, -1)
            ya, yd = comb(ya_ctx, ya_lat), comb(yd_ctx, yd_lat)

        wlo = jnp.zeros((128, 4 * hw), F32)
        wlo = wlo.at[:64, 0:hw].set(rw_w2[l, 0]).at[:64, hw:2 * hw].set(rw_w2[l, 1])
        wlo = wlo.at[64:, 2 * hw:3 * hw].set(rw_a2[l, 0]).at[64:, 3 * hw:].set(rw_a2[l, 1])
        vecs = jnp.stack([rw_k_k[l], rw_k_a[l], rw_r_k[l].reshape(hw), rw_w0[l, 0], rw_w0[l, 1],
                          rw_a0[l, 0], rw_a0[l, 1], jnp.zeros((hw,), F32)], axis=0)
        (r_, v_, kk_, w0_, k0_, b0_, w1_, k1_, b1_, bonus, gate_rw) = _rw_prep(
            p, rw_mu[l][None, :], wlo.astype(BF16), rw_g2[l].astype(BF16), vecs, nb=nb, tpb=tpb, r0=0, nt=tpb)
        yf, yb_ = _rw_scan(r_, v_, kk_, w0_, k0_, b0_, w1_, k1_, b1_, nb=nb, t_all=t_all, n_ctx=n_ctx)
        ln_vecs = jnp.concatenate([rw_lnx_g[l][None, :], rw_lnx_b[l][None, :], jnp.zeros((6, hw), F32)], axis=0)
        if last:
            trim = lambda a: a.reshape(nb, t_all, hw)[:, n_ctx:].reshape(nb * n_lat, hw)
            bonus, gate_rw = trim(bonus), trim(gate_rw)
        y_rw = _rw_post(yf, yb_, bonus, gate_rw, ln_vecs, nb=nb, tpb_y=tpb, r0_y=r0, nt=nt)

        bbs, abs_, cfs = [], [], []
        for dd in range(2):
            lr, li = s5_a_re[l, dd], s5_a_im[l, dd]
            dt = jnp.exp(s5_log_dt[l, dd])[:, None]
            mag = jnp.exp(lr * dt)
            ab_re, ab_im = mag * jnp.cos(li * dt), mag * jnp.sin(li * dt)
            den = lr * lr + li * li
            nr, ni = ab_re - 1.0, ab_im
            cf_re = (nr * lr + ni * li) / den
            cf_im = (ni * lr - nr * li) / den
            bre, bim = s5_b_re[l, dd], s5_b_im[l, dd]
            bb_re = cf_re[..., None] * bre - cf_im[..., None] * bim
            bb_im = cf_re[..., None] * bim + cf_im[..., None] * bre
            bbs.append(jnp.concatenate([_block_diag(bb_re.transpose(0, 2, 1)),
                                        _block_diag(bb_im.transpose(0, 2, 1))], axis=1))
            abs_.append(jnp.concatenate([ab_re.reshape(-1), ab_im.reshape(-1)])[None, :])
            cfs.append(jnp.concatenate([_block_diag(s5_c_re[l, dd].transpose(0, 2, 1)),
                                        -_block_diag(s5_c_im[l, dd].transpose(0, 2, 1))], axis=0))
        yf_s5, yb_s5 = _s5_scan(p, jnp.concatenate(bbs, axis=1).astype(BF16), jnp.stack(abs_, axis=0),
                                cfs[0].astype(BF16), cfs[1].astype(BF16), nb=nb, t_all=t_all, n_ctx=n_ctx)
        y_s5 = _s5_post(yf_s5, yb_s5, p, s5_d[l].reshape(1, 256), s5_glu_w[l].astype(BF16),
                        s5_glu_b[l][None, :], nb=nb, tpb=tpb, r0=r0, nt=nt)

        h = _merge(ya, y_rw, y_s5, yd, p, h, mods[2], w_branch[l].astype(BF16), w_out[l].astype(BF16),
                   nb=nb, tpb=tpb, r0=r0, nt=nt)

        wr = jnp.concatenate([router_g_w[l], router_e_w[l], jnp.zeros((d, 128 - MOE_GROUPS - MOE_EXPERTS), F32)], axis=1)
        wr_h = wr.astype(BF16)
        wr_m = (wr - wr_h.astype(F32)).astype(BF16)
        rbias = jnp.concatenate([router_g_b[l], router_e_b[l],
                                 jnp.zeros((128 - MOE_GROUPS - MOE_EXPERTS,), F32)])[None, :]
        h = _moe(h, norm2_g[l][None, :], mods[3], mods[4], mods[5], wr_h, wr_m, rbias,
                 exp_w_gate.reshape(depth * MOE_EXPERTS, d, D_EXPERT), exp_w_up.reshape(depth * MOE_EXPERTS, d, D_EXPERT),
                 exp_w_down.reshape(depth * MOE_EXPERTS, D_EXPERT, d),
                 final_norm_g[None, :], layer=l, nb=nb, nt=nt, ctx_first=not last, final=last)

    return h.reshape(nb, n_lat, d)
```

```python
import functools
import math

import jax
import jax.numpy as jnp
import numpy as np
from jax import lax
from jax.experimental import pallas as pl
from jax.experimental.pallas import tpu as pltpu

F32 = jnp.float32
BF16 = jnp.bfloat16

TM = 256
VMEM_LIMIT = 48 * 1024 * 1024

D_MODEL = 1024
GRID_W = 64
ROPE_DIM = 32
ROPE_BASE = 10000.0
NORM_EPS = 1e-6
MLA_HEADS, MLA_NOPE, MLA_ROPE, MLA_V = 4, 64, 32, 64
MLA_Q_LORA, MLA_KV_LORA = 256, 128
RW_HEADS, RW_HEAD = 4, 64
RW_LN_EPS = 64e-5
S5_GROUPS, S5_GROUP_CH, S5_STATE = 16, 16, 64
DIFF_HEADS, DIFF_HD = 4, 32
DIFF_EPS = 1e-5
MOE_GROUPS, MOE_PER_GROUP, MOE_TOPK = 4, 8, 2
MOE_EXPERTS = MOE_GROUPS * MOE_PER_GROUP
D_EXPERT = 512
MOE_BLK = 256
RW_CHUNK = 64
S5_CHUNK = 128
ATTN_TK_MAX = 2816
LOG2E = math.log2(math.e)

EXT_RW, EXT_MLA, EXT_S5, EXT_DIFF, EXT_GATE = 0, 1024, 1536, 1792, 3072
N_EXT = 7168


def _cparams(sem, vmem=VMEM_LIMIT):
    return pltpu.CompilerParams(dimension_semantics=sem, vmem_limit_bytes=vmem)


def _dot(a, b):
    return jnp.dot(a, b, preferred_element_type=F32)


def _dot_nt(a, b):
    return lax.dot_general(a, b, (((1,), (1,)), ((), ())), preferred_element_type=F32)


def _split_dot(x, e):
    hi = x.astype(BF16)
    mid = (x - hi.astype(F32)).astype(BF16)
    return _dot(hi, e) + _dot(mid, e)


def _block_ones(n, blk):
    r = lax.broadcasted_iota(jnp.int32, (n, n), 0) // blk
    c = lax.broadcasted_iota(jnp.int32, (n, n), 1) // blk
    return (r == c).astype(BF16)


def _sigmoid(x):
    return 1.0 / (1.0 + jnp.exp(-x))


def _silu(x):
    return x * _sigmoid(x)


def _softplus(x):
    return jnp.maximum(x, 0.0) + jnp.log(1.0 + jnp.exp(-jnp.abs(x)))


def _rms(x, g, eps):
    return x * lax.rsqrt(jnp.mean(x * x, axis=-1, keepdims=True) + eps) * g


def _mm_kernel(x_ref, w_ref, b_ref, o_ref, *, pre_silu):
    x = x_ref[...].astype(F32)
    if pre_silu:
        x = _silu(x)
    o_ref[...] = _dot(x.astype(BF16), w_ref[...]) + b_ref[...]


def _mm(x, w, b, *, tm, tn, pre_silu=False, name="mm"):
    m, k = x.shape
    n = w.shape[1]
    return pl.pallas_call(
        functools.partial(_mm_kernel, pre_silu=pre_silu),
        grid=(m // tm, n // tn),
        in_specs=[pl.BlockSpec((tm, k), lambda i, j: (i, 0)),
                  pl.BlockSpec((k, tn), lambda i, j: (0, j)),
                  pl.BlockSpec((1, tn), lambda i, j: (0, j))],
        out_specs=pl.BlockSpec((tm, tn), lambda i, j: (i, j)),
        out_shape=jax.ShapeDtypeStruct((m, n), F32),
        compiler_params=_cparams(("parallel", "arbitrary")),
        name=name,
    )(x, w, b)


def _inproj_kernel(h_ref, g_ref, sh_ref, sc_ref, w_ref, o_ref):
    x = _rms(h_ref[...], g_ref[...], NORM_EPS)
    xn = (x * (1.0 + sc_ref[0]) + sh_ref[0]).astype(BF16)
    o_ref[...] = _dot(xn, w_ref[...])


def _inproj(h, g, shift, scale, w_ext, *, nb, tpb):
    n = h.shape[0]
    tn = N_EXT // 2
    modmap = lambda j, b, r: (jnp.where(r == 0, nb, b), 0, 0)
    return pl.pallas_call(
        _inproj_kernel,
        grid=(N_EXT // tn, nb, tpb),
        in_specs=[pl.BlockSpec((TM, D_MODEL), lambda j, b, r: (b * tpb + r, 0)),
                  pl.BlockSpec((1, D_MODEL), lambda j, b, r: (0, 0)),
                  pl.BlockSpec((1, 1, D_MODEL), modmap),
                  pl.BlockSpec((1, 1, D_MODEL), modmap),
                  pl.BlockSpec((D_MODEL, tn), lambda j, b, r: (0, j))],
        out_specs=pl.BlockSpec((TM, tn), lambda j, b, r: (b * tpb + r, j)),
        out_shape=jax.ShapeDtypeStruct((n, N_EXT), F32),
        compiler_params=_cparams(("parallel", "parallel", "parallel")),
        name="inproj",
    )(h, g, shift, scale, w_ext)


def _mla_prep_kernel(p_ref, gq_ref, gkv_ref, wa_ref, wb_ref, wk_ref, wv_ref, pk_ref,
                     cq_ref, sq_ref, ck_ref, q_ref, k_ref, v_ref):
    seg = p_ref[...]
    nq = _rms(seg[:, :MLA_Q_LORA], gq_ref[...], NORM_EPS).astype(BF16)
    nkv = _rms(seg[:, MLA_Q_LORA:MLA_Q_LORA + MLA_KV_LORA], gkv_ref[...], NORM_EPS).astype(BF16)
    cq = jnp.concatenate([cq_ref[...]] * MLA_HEADS, axis=1)
    sq = jnp.concatenate([sq_ref[...]] * MLA_HEADS, axis=1)
    q = _dot(nq, wa_ref[...]) * cq + _dot(nq, wb_ref[...]) * sq
    q_ref[...] = q.astype(BF16)
    kr = (seg[:, 384:512] * ck_ref[...]).astype(BF16)
    k_ref[...] = (_dot(nkv, wk_ref[...]) + _dot(kr, pk_ref[...])).astype(BF16)
    v_ref[...] = _dot(nkv, wv_ref[...]).astype(BF16)


def _mla_prep(p, gq, gkv, wa, wb, wk, wv, pk, cq_tab, sq_tab, ck_tab, *, nb, tpb):
    n = p.shape[0]
    tok = lambda b, r: (b * tpb + r, 0)
    pos = lambda b, r: (r, 0)
    full = lambda b, r: (0, 0)
    return pl.pallas_call(
        _mla_prep_kernel,
        grid=(nb, tpb),
        in_specs=[pl.BlockSpec((TM, 512), lambda b, r: (b * tpb + r, EXT_MLA // 512)),
                  pl.BlockSpec((1, MLA_Q_LORA), full),
                  pl.BlockSpec((1, MLA_KV_LORA), full),
                  pl.BlockSpec((MLA_Q_LORA, 512), full),
                  pl.BlockSpec((MLA_Q_LORA, 512), full),
                  pl.BlockSpec((MLA_KV_LORA, 512), full),
                  pl.BlockSpec((MLA_KV_LORA, 256), full),
                  pl.BlockSpec((128, 512), full),
                  pl.BlockSpec((TM, 128), pos),
                  pl.BlockSpec((TM, 128), pos),
                  pl.BlockSpec((TM, 128), pos)],
        out_specs=[pl.BlockSpec((TM, 512), tok),
                   pl.BlockSpec((TM, 512), tok),
                   pl.BlockSpec((TM, 256), tok)],
        out_shape=[jax.ShapeDtypeStruct((n, 512), BF16),
                   jax.ShapeDtypeStruct((n, 512), BF16),
                   jax.ShapeDtypeStruct((n, 256), BF16)],
        compiler_params=_cparams(("parallel", "parallel")),
        name="mla_prep",
    )(p, gq, gkv, wa, wb, wk, wv, pk, cq_tab, sq_tab, ck_tab)


def _flash_mla_kernel(q_ref, k_ref, v_ref, o_ref, m_ref, l_ref, acc_ref):
    kk = pl.program_id(2)
    tq = q_ref.shape[0]

    @pl.when(kk == 0)
    def _():
        m_ref[...] = jnp.full(m_ref.shape, -jnp.inf, F32)
        l_ref[...] = jnp.zeros(l_ref.shape, F32)
        acc_ref[...] = jnp.zeros(acc_ref.shape, F32)

    ps, alphas = [], []
    for h in range(MLA_HEADS):
        s = _dot_nt(q_ref[:, h * 128:(h + 1) * 128], k_ref[:, h * 128:(h + 1) * 128])
        m_prev = m_ref[h]
        m_new = jnp.maximum(m_prev, jnp.max(s, axis=-1, keepdims=True))
        alpha = jnp.exp2(m_prev - m_new)
        p = jnp.exp2(s - m_new)
        l_ref[h] = alpha * l_ref[h] + jnp.sum(p, axis=-1, keepdims=True)
        m_ref[h] = m_new
        alphas.append(alpha)
        ps.append(p.astype(BF16))
    pv = _dot(jnp.concatenate(ps, axis=0), v_ref[...])
    acc_ref[...] = jnp.concatenate(alphas, axis=0) * acc_ref[...] + pv

    @pl.when(kk == pl.num_programs(2) - 1)
    def _():
        lane = lax.broadcasted_iota(jnp.int32, (tq, MLA_HEADS * MLA_V), 1)
        out = jnp.zeros((tq, MLA_HEADS * MLA_V), F32)
        for h in range(MLA_HEADS):
            out = jnp.where(lane // MLA_V == h, acc_ref[h * tq:(h + 1) * tq, :] / l_ref[h], out)
        o_ref[...] = out


def _flash_mla(q, k, v, *, nb, t_all, q_start, q_len, k_len, tk):
    tq = TM
    tpb_q, tpb_k = t_all // tq, t_all // tk
    q0 = q_start // tq
    nq = q_len // tq
    return pl.pallas_call(
        _flash_mla_kernel,
        grid=(nb, nq, k_len // tk),
        in_specs=[pl.BlockSpec((tq, 512), lambda b, i, kk: (b * tpb_q + q0 + i, 0)),
                  pl.BlockSpec((tk, 512), lambda b, i, kk: (b * tpb_k + kk, 0)),
                  pl.BlockSpec((tk, 256), lambda b, i, kk: (b * tpb_k + kk, 0))],
        out_specs=pl.BlockSpec((tq, 256), lambda b, i, kk: (b * nq + i, 0)),
        out_shape=jax.ShapeDtypeStruct((nb * q_len, 256), F32),
        scratch_shapes=[pltpu.VMEM((MLA_HEADS, tq, 1), F32), pltpu.VMEM((MLA_HEADS, tq, 1), F32),
                        pltpu.VMEM((MLA_HEADS * tq, 256), F32)],
        compiler_params=_cparams(("parallel", "parallel", "arbitrary")),
        name="flash_mla",
    )(q, k, v)


def _diff_prep(p, cos_tab, sin_tab, *, nb, tpb):
    n = p.shape[0]
    tok = lambda b, r: (b * tpb + r, 0)
    pos = lambda b, r: (r, 0)
    return pl.pallas_call(
        _diff_prep_kernel_cols,
        grid=(nb, tpb),
        in_specs=[pl.BlockSpec((TM, 256), lambda b, r, c=c: (b * tpb + r, EXT_DIFF // 256 + c))
                  for c in range(5)]
                 + [pl.BlockSpec((TM, 256), pos), pl.BlockSpec((TM, 256), pos)],
        out_specs=[pl.BlockSpec((TM, 256), tok)] * 3,
        out_shape=[jax.ShapeDtypeStruct((n, 256), BF16)] * 3,
        compiler_params=_cparams(("parallel", "parallel")),
        name="diff_prep",
    )(p, p, p, p, p, cos_tab, sin_tab)


def _diff_prep_kernel_cols(q_in, k_in, v_in, qr_in, kr_in, cos_ref, sin_ref, q_ref, k_ref, v_ref):
    cos, sin = cos_ref[...], sin_ref[...]
    scale = DIFF_HD ** -0.5 * LOG2E
    q_ref[...] = ((q_in[...] * cos + qr_in[...] * sin) * scale).astype(BF16)
    k_ref[...] = (k_in[...] * cos + kr_in[...] * sin).astype(BF16)
    v_ref[...] = v_in[...].astype(BF16)


def _flash_diff_kernel(q_ref, k_ref, v_ref, lam_ref, g_ref, o_ref, qs_ref, m_ref, l_ref, acc_ref, *, lam_init):
    kk = pl.program_id(2)
    tq = q_ref.shape[0]
    nsm = 2 * DIFF_HEADS

    @pl.when(kk == 0)
    def _():
        m_ref[...] = jnp.full(m_ref.shape, -jnp.inf, F32)
        l_ref[...] = jnp.zeros(l_ref.shape, F32)
        acc_ref[...] = jnp.zeros(acc_ref.shape, F32)
        q = q_ref[...]
        lane = lax.broadcasted_iota(jnp.int32, (tq, 256), 1)
        for i in range(nsm):
            qs_ref[i * tq:(i + 1) * tq, :] = jnp.where((lane // DIFF_HD) == i, q, jnp.zeros_like(q))

    k = k_ref[...]
    ps, alphas = [], []
    for i in range(nsm):
        rows = slice(i * tq, (i + 1) * tq)
        s = _dot_nt(qs_ref[rows, :], k)
        m_prev = m_ref[rows, :]
        m_new = jnp.maximum(m_prev, jnp.max(s, axis=-1, keepdims=True))
        alpha = jnp.exp2(m_prev - m_new)
        p = jnp.exp2(s - m_new)
        l_ref[rows, :] = alpha * l_ref[rows, :] + jnp.sum(p, axis=-1, keepdims=True)
        m_ref[rows, :] = m_new
        alphas.append(alpha)
        ps.append(p.astype(BF16))
    pv = _dot(jnp.concatenate(ps, axis=0), v_ref[...])
    acc_ref[...] = jnp.concatenate(alphas, axis=0) * acc_ref[...] + pv

    @pl.when(kk == pl.num_programs(2) - 1)
    def _():
        lane = lax.broadcasted_iota(jnp.int32, (tq, 256), 1)
        o = jnp.zeros((tq, 256), F32)
        for h in range(DIFF_HEADS):
            r0, r1 = (2 * h) * tq, (2 * h + 1) * tq
            o0 = acc_ref[r0:r0 + tq, :] / l_ref[r0:r0 + tq, :]
            o1 = acc_ref[r1:r1 + tq, :] / l_ref[r1:r1 + tq, :]
            o = jnp.where((lane // (2 * DIFF_HD)) == h, o0 - lam_ref[...] * o1, o)
        ms = _split_dot(o * o, _block_ones(256, 2 * DIFF_HD)) * (1.0 / (2 * DIFF_HD))
        o_ref[...] = o * lax.rsqrt(ms + DIFF_EPS) * g_ref[...] * (1.0 - lam_init)


def _flash_diff(q, k, v, lam_row, g_row, *, lam_init, nb, t_all, q_start, q_len, k_len, tk):
    tq = TM
    tpb_q, tpb_k = t_all // tq, t_all // tk
    q0 = q_start // tq
    nq = q_len // tq
    return pl.pallas_call(
        functools.partial(_flash_diff_kernel, lam_init=lam_init),
        grid=(nb, nq, k_len // tk),
        in_specs=[pl.BlockSpec((tq, 256), lambda b, i, kk: (b * tpb_q + q0 + i, 0)),
                  pl.BlockSpec((tk, 256), lambda b, i, kk: (b * tpb_k + kk, 0)),
                  pl.BlockSpec((tk, 256), lambda b, i, kk: (b * tpb_k + kk, 0)),
                  pl.BlockSpec((1, 256), lambda b, i, kk: (0, 0)),
                  pl.BlockSpec((1, 256), lambda b, i, kk: (0, 0))],
        out_specs=pl.BlockSpec((tq, 256), lambda b, i, kk: (b * nq + i, 0)),
        out_shape=jax.ShapeDtypeStruct((nb * q_len, 256), F32),
        scratch_shapes=[pltpu.VMEM((2 * DIFF_HEADS * tq, 256), BF16),
                        pltpu.VMEM((2 * DIFF_HEADS * tq, 1), F32),
                        pltpu.VMEM((2 * DIFF_HEADS * tq, 1), F32),
                        pltpu.VMEM((2 * DIFF_HEADS * tq, 256), F32)],
        compiler_params=_cparams(("parallel", "parallel", "arbitrary")),
        name="flash_diff",
    )(q, k, v, lam_row, g_row)


def _rw_prep_kernel(p_ref, prev_ref, next_ref, mu_ref, wlo_ref, g2_ref, vec_ref,
                    r_ref, v_ref, kk_ref, w0_ref, k0_ref, b0_ref, w1_ref, k1_ref, b1_ref,
                    bonus_ref, gate_ref, *, r0, lat_last):
    r = pl.program_id(1) + r0
    p = p_ref[...]
    row = lax.broadcasted_iota(jnp.int32, p.shape, 0)
    first_tile = jnp.logical_or(r == 0, r == 1)
    last_tile = jnp.logical_or(r == 0, r == lat_last)
    prev_row = jnp.where(first_tile, 0.0, prev_ref[7:8, :])
    next_row = jnp.where(last_tile, 0.0, next_ref[0:1, :])
    up = jnp.where(row == 0, prev_row, pltpu.roll(p, 1, 0))
    dn = jnp.where(row == TM - 1, next_row, pltpu.roll(p, TM - 1, 0))
    z = p + (0.5 * (up + dn) - p) * mu_ref[...]

    hw = RW_HEADS * RW_HEAD
    rr, k, v = z[:, :hw], z[:, hw:2 * hw], z[:, 2 * hw:3 * hw]
    lo = z[:, 3 * hw:3 * hw + 128]
    lane = lax.broadcasted_iota(jnp.int32, lo.shape, 1)
    lo = jnp.where(lane < 64, jnp.tanh(lo), lo).astype(BF16)
    wa = _dot(lo, wlo_ref[...])
    gate_ref[...] = _dot(_sigmoid(z[:, 3 * hw + 128:]).astype(BF16), g2_ref[...])

    e4 = _block_ones(hw, RW_HEAD)
    k_k, k_a, r_k = vec_ref[0:1, :], vec_ref[1:2, :], vec_ref[2:3, :]
    kk = k * k_k
    nrm = jnp.maximum(jnp.sqrt(_split_dot(kk * kk, e4)), 1e-12)
    kk = kk / nrm
    r_ref[...] = rr
    v_ref[...] = v
    kk_ref[...] = kk
    ksum = jnp.zeros_like(k)
    for d, (w_ref, kd_ref, b_ref) in enumerate(((w0_ref, k0_ref, b0_ref), (w1_ref, k1_ref, b1_ref))):
        w0 = vec_ref[3 + d:4 + d, :]
        a0 = vec_ref[5 + d:6 + d, :]
        wd = -_softplus(-(w0 + wa[:, d * hw:(d + 1) * hw])) - 0.5
        w_ref[...] = jnp.exp(-jnp.exp(wd))
        ad = _sigmoid(a0 + wa[:, (2 + d) * hw:(3 + d) * hw])
        kd = k * (1.0 + (ad - 1.0) * k_a)
        kd_ref[...] = kd
        b_ref[...] = kk * ad
        ksum = ksum + kd
    bonus_ref[...] = _split_dot(rr * ksum * r_k, e4) * v


def _rw_prep(p, mu, wlo, g2, vecs, *, nb, tpb, r0, nt):
    n_out = nb * nt * TM
    hw = RW_HEADS * RW_HEAD
    n_rows8 = p.shape[0] // 8
    tok = lambda b, r: (b * tpb + r0 + r, EXT_RW // 1024)
    prev = lambda b, r: (jnp.maximum((b * tpb + r0 + r) * (TM // 8) - 1, 0), EXT_RW // 1024)
    nxt = lambda b, r: (jnp.minimum((b * tpb + r0 + r + 1) * (TM // 8), n_rows8 - 1), EXT_RW // 1024)
    out = lambda b, r: (b * nt + r, 0)
    full = lambda b, r: (0, 0)
    return pl.pallas_call(
        functools.partial(_rw_prep_kernel, r0=r0, lat_last=tpb - 1),
        grid=(nb, nt),
        in_specs=[pl.BlockSpec((TM, 1024), tok),
                  pl.BlockSpec((8, 1024), prev),
                  pl.BlockSpec((8, 1024), nxt),
                  pl.BlockSpec((1, 1024), full),
                  pl.BlockSpec((128, 4 * hw), full),
                  pl.BlockSpec((128, hw), full),
                  pl.BlockSpec((8, hw), full)],
        out_specs=[pl.BlockSpec((TM, hw), out)] * 11,
        out_shape=[jax.ShapeDtypeStruct((n_out, hw), F32)] * 11,
        compiler_params=_cparams(("parallel", "parallel")),
        name="rw_prep",
    )(p, p, p, mu, wlo, g2, vecs)


def _rw_scan_kernel(rf, vf, kkf, wf, kf, bf, rb, vb, kkb, wb, kb, bb, yf_ref, yb_ref, s_ref, *, nb):
    c = RW_CHUNK

    @pl.when(pl.program_id(0) == 0)
    def _():
        s_ref[...] = jnp.zeros(s_ref.shape, F32)

    e2 = _block_ones(128, RW_HEAD)
    z2 = jnp.zeros((128, 128), BF16)
    rhs_pair = jnp.concatenate([jnp.concatenate([e2, z2], axis=1), jnp.concatenate([z2, e2], axis=1)], axis=0)
    e22 = jnp.concatenate([e2, e2], axis=0)
    lane = lax.broadcasted_iota(jnp.int32, (RW_HEAD, 128), 1)
    sub = lax.broadcasted_iota(jnp.int32, (RW_HEAD, 128), 0)
    diag = (lane % RW_HEAD) == sub
    sub8 = lax.broadcasted_iota(jnp.int32, (8, 128), 0)
    dirs = ((rf, vf, kkf, wf, kf, bf, yf_ref), (rb, vb, kkb, wb, kb, bb, yb_ref))

    def allreduce_rows(x):
        t = x[0:8]
        for i in range(1, 8):
            t = t + x[8 * i:8 * i + 8]
        for sh in (4, 2, 1):
            t = t + pltpu.roll(t, sh, 0)
        return t

    def col(row):
        return jnp.where(diag, row, 0.0).astype(BF16)

    def group(g, carry):
        tiles, ytiles = {}, {}
        for d, refs in enumerate(dirs):
            base = pl.multiple_of((g if d == 0 else c // 8 - 1 - g) * 8, 8)
            if d == 0:
                prev = lambda x, sh: jnp.where(sub8 >= sh, pltpu.roll(x, sh, 0), 1.0)
            else:
                prev = lambda x, sh: jnp.where(sub8 < 8 - sh, pltpu.roll(x, 8 - sh, 0), 1.0)
            last = 7 if d == 0 else 0
            for b in range(nb):
                for hp in range(2):
                    r_, v_, kk_, w_, k_, b_ = [ref[b, pl.ds(base, 8), pl.ds(hp * 128, 128)] for ref in refs[:6]]
                    gam = w_
                    for sh in (1, 2, 4):
                        gam = gam * prev(gam, sh)
                    inv = 1.0 / gam
                    tiles[d, b, hp] = (base, v_, b_ * inv, kk_ * prev(gam, 1), k_ * inv, r_ * gam,
                                       gam[last:last + 1, :])
                    ytiles[d, b, hp] = jnp.zeros((8, 128), F32)
        units = [(d, b, hp) for d in range(2) for b in range(nb) for hp in range(2)]
        for jj in range(8):
            lhs = []
            for (d, b, hp) in units:
                rw = slice(jj, jj + 1) if d == 0 else slice(7 - jj, 8 - jj)
                _, _, bh, kkh, kh, rh, _ = tiles[d, b, hp]
                lhs.append(jnp.concatenate([col(bh[rw]), col(kkh[rw])], axis=1))
                lhs.append(jnp.concatenate([col(kh[rw]), col(rh[rw])], axis=1))
            cm = _dot(jnp.concatenate(lhs, axis=0), rhs_pair)
            for u, (d, b, hp) in enumerate(units):
                j = jj if d == 0 else 7 - jj
                v_ = tiles[d, b, hp][1]
                r0 = 2 * u * RW_HEAD
                bc, kkc = cm[r0:r0 + RW_HEAD, :128], cm[r0:r0 + RW_HEAD, 128:]
                kc, rc = cm[r0 + RW_HEAD:r0 + 2 * RW_HEAD, :128], cm[r0 + RW_HEAD:r0 + 2 * RW_HEAD, 128:]
                s = s_ref[u]
                sa = jnp.concatenate([allreduce_rows(kkc * s)] * 8, axis=0)
                s = s - bc * sa + kc * v_[j:j + 1]
                s_ref[u] = s
                ytiles[d, b, hp] = jnp.where(sub8 == j, allreduce_rows(rc * s), ytiles[d, b, hp])
        for d, refs in enumerate(dirs):
            for b in range(nb):
                for hp in range(2):
                    u = (d * nb + b) * 2 + hp
                    gl = tiles[d, b, hp][6]
                    gh = gl.astype(BF16).astype(F32)
                    s_ref[u] = s_ref[u] * _dot(jnp.concatenate([col(gh), col(gl - gh)], axis=1), e22)
                    refs[6][b, pl.ds(tiles[d, b, hp][0], 8), pl.ds(hp * 128, 128)] = ytiles[d, b, hp]
        return carry

    lax.fori_loop(0, c // 8, group, 0)


def _rw_scan(r, v, kk, w0, k0, b0, w1, k1, b1, *, nb, t_all, n_ctx):
    c = RW_CHUNK
    nc, ncc = t_all // c, n_ctx // c
    hw = RW_HEADS * RW_HEAD
    shp = lambda a: a.reshape(nb, t_all, hw)
    fwd = lambda j: (0, j, 0)
    bwd = lambda j: (0, jnp.where(j < ncc, ncc - 1 - j, nc - 1 - (j - ncc)), 0)
    blk = (nb, c, hw)
    yshape = jax.ShapeDtypeStruct((nb, t_all, hw), F32)
    yf, yb = pl.pallas_call(
        functools.partial(_rw_scan_kernel, nb=nb),
        grid=(nc,),
        in_specs=[pl.BlockSpec(blk, fwd)] * 6 + [pl.BlockSpec(blk, bwd)] * 6,
        out_specs=[pl.BlockSpec(blk, fwd), pl.BlockSpec(blk, bwd)],
        out_shape=[yshape, yshape],
        scratch_shapes=[pltpu.VMEM((2 * nb * 2, RW_HEAD, 128), F32)],
        compiler_params=_cparams(("arbitrary",)),
        name="rw_scan",
    )(shp(r), shp(v), shp(kk), shp(w0), shp(k0), shp(b0),
      shp(r), shp(v), shp(kk), shp(w1), shp(k1), shp(b1))

    return yf.reshape(nb * t_all, hw), yb.reshape(nb * t_all, hw)


def _rw_post_kernel(yf_ref, yb_ref, bonus_ref, gate_ref, vec_ref, o_ref):
    e4 = _block_ones(RW_HEADS * RW_HEAD, RW_HEAD)
    y = yf_ref[...] + yb_ref[...]
    mean = _split_dot(y, e4) * (1.0 / RW_HEAD)
    yc = y - mean
    var = _split_dot(yc * yc, e4) * (1.0 / RW_HEAD)
    yn = yc * lax.rsqrt(var + RW_LN_EPS) * vec_ref[0:1, :] + vec_ref[1:2, :]
    o_ref[...] = (yn + bonus_ref[...]) * gate_ref[...]


def _rw_post(yf, yb, bonus, gate, vecs, *, nb, tpb_y, r0_y, nt):
    hw = RW_HEADS * RW_HEAD
    n_out = bonus.shape[0]
    ytok = lambda b, r: (b * tpb_y + r0_y + r, 0)
    tok = lambda b, r: (b * nt + r, 0)
    return pl.pallas_call(
        _rw_post_kernel,
        grid=(nb, nt),
        in_specs=[pl.BlockSpec((TM, hw), ytok), pl.BlockSpec((TM, hw), ytok),
                  pl.BlockSpec((TM, hw), tok), pl.BlockSpec((TM, hw), tok),
                  pl.BlockSpec((8, hw), lambda b, r: (0, 0))],
        out_specs=pl.BlockSpec((TM, hw), tok),
        out_shape=jax.ShapeDtypeStruct((n_out, hw), F32),
        compiler_params=_cparams(("parallel", "parallel")),
        name="rw_post",
    )(yf, yb, bonus, gate, vecs)


def _s5_scan_kernel(uf_ref, ub_ref, wb_ref, ab_ref, cf_ref, cb_ref, yf_ref, yb_ref, x_ref, st_ref, *, nb):
    c = S5_CHUNK
    nst = S5_GROUPS * S5_STATE

    @pl.when(pl.program_id(0) == 0)
    def _():
        st_ref[...] = jnp.zeros(st_ref.shape, F32)

    dirs = ((uf_ref, cf_ref, yf_ref), (ub_ref, cb_ref, yb_ref))
    for d, (u_ref, _, _) in enumerate(dirs):
        for b in range(nb):
            x_ref[d, b] = _dot(u_ref[b].astype(BF16), wb_ref[:, d * 2 * nst:(d + 1) * 2 * nst])

    def group(g, carry):
        for d in range(2):
            base = pl.multiple_of((g if d == 0 else c // 8 - 1 - g) * 8, 8)
            ar = ab_ref[d, :, 0:nst]
            ai = ab_ref[d, :, nst:2 * nst]
            for b in range(nb):
                u = d * nb + b
                xr = st_ref[u, :, 0:nst]
                xi = st_ref[u, :, nst:2 * nst]
                bur = x_ref[d, b, pl.ds(base, 8), 0:nst]
                bui = x_ref[d, b, pl.ds(base, 8), nst:2 * nst]
                rows_r, rows_i = [None] * 8, [None] * 8
                for jj in range(8):
                    j = jj if d == 0 else 7 - jj
                    xr, xi = (ar * xr - ai * xi + bur[j:j + 1, :], ar * xi + ai * xr + bui[j:j + 1, :])
                    rows_r[j], rows_i[j] = xr, xi
                st_ref[u, :, 0:nst] = xr
                st_ref[u, :, nst:2 * nst] = xi
                x_ref[d, b, pl.ds(base, 8), 0:nst] = jnp.concatenate(rows_r, axis=0)
                x_ref[d, b, pl.ds(base, 8), nst:2 * nst] = jnp.concatenate(rows_i, axis=0)
        return carry

    lax.fori_loop(0, c // 8, group, 0)

    for d, (_, c_ref, y_ref) in enumerate(dirs):
        for b in range(nb):
            y_ref[b] = _dot(x_ref[d, b].astype(BF16), c_ref[...])


def _s5_scan(p, w_b, ab, cf, cb, *, nb, t_all, n_ctx):
    c = S5_CHUNK
    nc, ncc = t_all // c, n_ctx // c
    nst2 = 2 * S5_GROUPS * S5_STATE
    width = S5_GROUPS * S5_GROUP_CH
    p3 = p.reshape(nb, t_all, N_EXT)
    fwd = lambda j: j
    bwd = lambda j: jnp.where(j < ncc, ncc - 1 - j, nc - 1 - (j - ncc))
    yshape = jax.ShapeDtypeStruct((nb, t_all, width), F32)
    full = lambda j: (0, 0)
    yf, yb = pl.pallas_call(
        functools.partial(_s5_scan_kernel, nb=nb),
        grid=(nc,),
        in_specs=[pl.BlockSpec((nb, c, width), lambda j: (0, fwd(j), EXT_S5 // width)),
                  pl.BlockSpec((nb, c, width), lambda j: (0, bwd(j), EXT_S5 // width)),
                  pl.BlockSpec((width, 2 * nst2), full),
                  pl.BlockSpec((2, 1, nst2), lambda j: (0, 0, 0)),
                  pl.BlockSpec((nst2, width), full), pl.BlockSpec((nst2, width), full)],
        out_specs=[pl.BlockSpec((nb, c, width), lambda j: (0, fwd(j), 0)),
                   pl.BlockSpec((nb, c, width), lambda j: (0, bwd(j), 0))],
        out_shape=[yshape, yshape],
        scratch_shapes=[pltpu.VMEM((2, nb, c, nst2), F32), pltpu.VMEM((2 * nb, 1, nst2), F32)],
        compiler_params=_cparams(("arbitrary",)),
        name="s5_scan",
    )(p3, p3, w_b, ab, cf, cb)
    return yf.reshape(nb * t_all, width), yb.reshape(nb * t_all, width)


def _s5_post_kernel(yf_ref, yb_ref, u_ref, d_ref, gw_ref, gb_ref, o_ref):
    y = yf_ref[...] + yb_ref[...] + d_ref[...] * u_ref[...]
    zg = 0.5 * y * (1.0 + jnp.tanh(math.sqrt(2.0 / math.pi) * (y + 0.044715 * (y * y * y))))
    o_ref[...] = zg * _sigmoid(_dot(zg.astype(BF16), gw_ref[...]) + gb_ref[...])


def _s5_post(yf, yb, p, d_row, glu_w, glu_b, *, nb, tpb, r0, nt):
    tok = lambda b, r: (b * tpb + r0 + r, 0)
    full = lambda b, r: (0, 0)
    return pl.pallas_call(
        _s5_post_kernel,
        grid=(nb, nt),
        in_specs=[pl.BlockSpec((TM, 256), tok), pl.BlockSpec((TM, 256), tok),
                  pl.BlockSpec((TM, 256), lambda b, r: (b * tpb + r0 + r, EXT_S5 // 256)),
                  pl.BlockSpec((1, 256), full), pl.BlockSpec((256, 256), full),
                  pl.BlockSpec((1, 256), full)],
        out_specs=pl.BlockSpec((TM, 256), lambda b, r: (b * nt + r, 0)),
        out_shape=jax.ShapeDtypeStruct((nb * nt * TM, 256), F32),
        compiler_params=_cparams(("parallel", "parallel")),
        name="s5_post",
    )(yf, yb, p, d_row, glu_w, glu_b)


def _merge_kernel(ya_ref, yb_ref, ys_ref, yd_ref, g0, g1, g2, g3, h_ref, m2_ref, wb_ref, wo_ref, o_ref):
    acc = None
    for n, (y_ref, g_ref) in enumerate(((ya_ref, g0), (yb_ref, g1), (ys_ref, g2), (yd_ref, g3))):
        term = _sigmoid(g_ref[...]) * _dot(y_ref[...].astype(BF16), wb_ref[n])
        acc = term if acc is None else acc + term
    o_ref[...] = h_ref[...] + m2_ref[0] * _dot(acc.astype(BF16), wo_ref[...])


def _merge(ya, yb, ys, yd, p, h, mod2, w_branch, w_out, *, nb, tpb, r0, nt):
    tok_in = lambda b, r: (b * tpb + r0 + r, 0)
    tok_out = lambda b, r: (b * nt + r, 0)
    full2 = lambda b, r: (0, 0)
    gate = [pl.BlockSpec((TM, D_MODEL), lambda b, r, c=c: (b * tpb + r0 + r, EXT_GATE // D_MODEL + c))
            for c in range(4)]
    return pl.pallas_call(
        _merge_kernel,
        grid=(nb, nt),
        in_specs=[pl.BlockSpec((TM, 256), tok_out)] * 4 + gate
                 + [pl.BlockSpec((TM, D_MODEL), tok_in),
                    pl.BlockSpec((1, 1, D_MODEL), lambda b, r: (jnp.where(r0 + r == 0, nb, b), 0, 0)),
                    pl.BlockSpec((4, 256, D_MODEL), lambda b, r: (0, 0, 0)),
                    pl.BlockSpec((D_MODEL, D_MODEL), full2)],
        out_specs=pl.BlockSpec((TM, D_MODEL), tok_out),
        out_shape=jax.ShapeDtypeStruct((nb * nt * TM, D_MODEL), F32),
        compiler_params=_cparams(("parallel", "parallel")),
        name="merge",
    )(ya, yb, ys, yd, p, p, p, p, h, mod2, w_branch, w_out)


def _router_kernel(h_ref, g_ref, sh_ref, sc_ref, wh_ref, wm_ref, b_ref, x_ref, lg_ref):
    x = _rms(h_ref[...], g_ref[...], NORM_EPS) * (1.0 + sc_ref[0]) + sh_ref[0]
    xh = x.astype(BF16)
    xm = (x - xh.astype(F32)).astype(BF16)
    bits = lax.bitcast_convert_type(xh.astype(F32), jnp.uint32)
    half = D_MODEL // 2
    x_ref[...] = (bits[:, :half] >> 16) | (bits[:, half:] & jnp.uint32(0xFFFF0000))
    lg = (_dot(xh, wh_ref[...]) + _dot(xm, wh_ref[...]) + _dot(xh, wm_ref[...])) + b_ref[...]

    lane = lax.broadcasted_iota(jnp.int32, lg.shape, 1)
    lanef = lane.astype(F32)
    neg = jnp.float32(-jnp.inf)
    big = jnp.float32(1e9)
    rmax = lambda v: jnp.max(v, axis=-1, keepdims=True)
    rmin = lambda v: jnp.min(v, axis=-1, keepdims=True)
    rsum = lambda v: jnp.sum(v, axis=-1, keepdims=True)

    gmask = lane < MOE_GROUPS
    mg = rmax(jnp.where(gmask, lg, neg))
    eg = jnp.where(gmask, jnp.exp(lg - mg), 0.0)
    pg = eg / rsum(eg)
    pg_top = rmax(pg)
    g_sel = rmin(jnp.where(jnp.logical_and(gmask, pg == pg_top), lanef, big))
    lo = MOE_GROUPS + MOE_PER_GROUP * g_sel
    emask = jnp.logical_and(lanef >= lo, lanef < lo + MOE_PER_GROUP)
    me = rmax(jnp.where(emask, lg, neg))
    ee = jnp.where(emask, jnp.exp(lg - me), 0.0)
    pe = jnp.where(emask, ee / rsum(ee), -1.0)
    p1 = rmax(pe)
    i1 = rmin(jnp.where(pe == p1, lanef, big))
    pe2 = jnp.where(lanef == i1, -1.0, pe)
    p2 = rmax(pe2)
    i2 = rmin(jnp.where(pe2 == p2, lanef, big))
    den = p1 + p2
    out = jnp.where(lane == 0, i1 - MOE_GROUPS, 0.0)
    out = jnp.where(lane == 1, i2 - MOE_GROUPS, out)
    out = jnp.where(lane == 2, pg_top * p1 / den, out)
    out = jnp.where(lane == 3, pg_top * p2 / den, out)
    lg_ref[...] = out


def _router(h, g, shift, scale, wh, wm, bias, *, nb, nt, ctx_first):
    tok = lambda b, r: (b * nt + r, 0)
    full = lambda b, r: (0, 0)
    if ctx_first:
        modmap = lambda b, r: (jnp.where(r == 0, nb, b), 0, 0)
    else:
        modmap = lambda b, r: (b, 0, 0)
    n = h.shape[0]
    return pl.pallas_call(
        _router_kernel,
        grid=(nb, nt),
        in_specs=[pl.BlockSpec((TM, D_MODEL), tok), pl.BlockSpec((1, D_MODEL), full),
                  pl.BlockSpec((1, 1, D_MODEL), modmap), pl.BlockSpec((1, 1, D_MODEL), modmap),
                  pl.BlockSpec((D_MODEL, 128), full), pl.BlockSpec((D_MODEL, 128), full),
                  pl.BlockSpec((1, 128), full)],
        out_specs=[pl.BlockSpec((TM, D_MODEL // 2), tok), pl.BlockSpec((TM, 128), tok)],
        out_shape=[jax.ShapeDtypeStruct((n, D_MODEL // 2), jnp.uint32), jax.ShapeDtypeStruct((n, 128), F32)],
        compiler_params=_cparams(("parallel", "parallel")),
        name="router",
    )(h, g, shift, scale, wh, wm, bias)


def _expert_kernel(be_ref, nv_ref, x_ref, wg_ref, wu_ref, wd_ref, o_ref, wgb_ref, wub_ref, wdb_ref):
    i = pl.program_id(0)

    @pl.when(jnp.logical_or(i == 0, be_ref[i] != be_ref[jnp.maximum(i - 1, 0)]))
    def _():
        wgb_ref[...] = wg_ref[0].astype(BF16)
        wub_ref[...] = wu_ref[0].astype(BF16)
        wdb_ref[...] = wd_ref[0].astype(BF16)

    @pl.when(i < nv_ref[0])
    def _():
        u = x_ref[...]
        x = jnp.concatenate([lax.bitcast_convert_type(u << 16, F32),
                             lax.bitcast_convert_type(u & jnp.uint32(0xFFFF0000), F32)], axis=1).astype(BF16)
        hb = _silu(_dot(x, wgb_ref[...])) * _dot(x, wub_ref[...])
        o_ref[...] = _dot(hb.astype(BF16), wdb_ref[...])

    @pl.when(i >= nv_ref[0])
    def _():
        o_ref[...] = jnp.zeros(o_ref.shape, F32)


def _experts(xs, block_e, n_valid, w_gate, w_up, w_down):
    n_slots = xs.shape[0]
    n_blocks = n_slots // MOE_BLK
    wmap = lambda i, be, nv: (be[i], 0, 0)
    return pl.pallas_call(
        _expert_kernel,
        grid_spec=pltpu.PrefetchScalarGridSpec(
            num_scalar_prefetch=2,
            grid=(n_blocks,),
            in_specs=[pl.BlockSpec((MOE_BLK, D_MODEL // 2), lambda i, be, nv: (i, 0)),
                      pl.BlockSpec((1, D_MODEL, D_EXPERT), wmap),
                      pl.BlockSpec((1, D_MODEL, D_EXPERT), wmap),
                      pl.BlockSpec((1, D_EXPERT, D_MODEL), wmap)],
            out_specs=pl.BlockSpec((MOE_BLK, D_MODEL), lambda i, be, nv: (i, 0)),
            scratch_shapes=[pltpu.VMEM((D_MODEL, D_EXPERT), BF16), pltpu.VMEM((D_MODEL, D_EXPERT), BF16),
                            pltpu.VMEM((D_EXPERT, D_MODEL), BF16)]),
        out_shape=jax.ShapeDtypeStruct((n_slots, D_MODEL), F32),
        compiler_params=_cparams(("arbitrary",)),
        name="experts",
    )(block_e, n_valid, xs, w_gate, w_up, w_down)


def _combine_kernel(h_ref, y0_ref, y1_ref, w_ref, m5_ref, g_ref, o_ref, *, final):
    w = w_ref[...]
    y = y0_ref[...] * w[:, 0:1] + y1_ref[...] * w[:, 1:2]
    h = h_ref[...] + m5_ref[0] * y
    if final:
        h = _rms(h, g_ref[...], NORM_EPS)
    o_ref[...] = h


def _combine(h, y0, y1, wts, mod5, g_final, *, nb, nt, ctx_first, final):
    tok = lambda b, r: (b * nt + r, 0)
    if ctx_first:
        modmap = lambda b, r: (jnp.where(r == 0, nb, b), 0, 0)
    else:
        modmap = lambda b, r: (b, 0, 0)
    return pl.pallas_call(
        functools.partial(_combine_kernel, final=final),
        grid=(nb, nt),
        in_specs=[pl.BlockSpec((TM, D_MODEL), tok)] * 3
                 + [pl.BlockSpec((TM, 128), tok), pl.BlockSpec((1, 1, D_MODEL), modmap),
                    pl.BlockSpec((1, D_MODEL), lambda b, r: (0, 0))],
        out_specs=pl.BlockSpec((TM, D_MODEL), tok),
        out_shape=jax.ShapeDtypeStruct(h.shape, F32),
        compiler_params=_cparams(("parallel", "parallel")),
        name="combine",
    )(h, y0, y1, wts, mod5, g_final)


def _moe(h, g2, shift, scale, mod5, wh, wm, rbias, w_gate, w_up, w_down, g_final, *, layer, nb, nt, ctx_first,
         final):
    n = h.shape[0]
    x_bf, route = _router(h, g2, shift, scale, wh, wm, rbias, nb=nb, nt=nt, ctx_first=ctx_first)
    idx = route[:, :MOE_TOPK].astype(jnp.int32)
    wts = route[:, MOE_TOPK:2 * MOE_TOPK]
    n_as = n * MOE_TOPK
    flat_e = idx.reshape(n_as)
    onehot = (flat_e[:, None] == jnp.arange(MOE_EXPERTS, dtype=jnp.int32)[None, :]).astype(jnp.int32)
    csum = jnp.cumsum(onehot, axis=0)
    counts = csum[-1]
    rank = jnp.sum(jnp.where(onehot > 0, csum - 1, 0), axis=1)
    padded = (counts + MOE_BLK - 1) // MOE_BLK * MOE_BLK
    pad_end = jnp.cumsum(padded)
    pad_start = pad_end - padded
    slot = pad_start[flat_e] + rank
    n_blocks = (n_as + MOE_EXPERTS * (MOE_BLK - 1) + MOE_BLK - 1) // MOE_BLK
    n_slots = n_blocks * MOE_BLK
    slot_tok = jnp.zeros((n_slots,), jnp.int32).at[slot].set(jnp.arange(n_as, dtype=jnp.int32) // MOE_TOPK)
    starts = jnp.arange(n_blocks, dtype=jnp.int32) * MOE_BLK
    block_e = jnp.minimum(jnp.sum((pad_end[None, :] <= starts[:, None]).astype(jnp.int32), axis=1),
                          MOE_EXPERTS - 1)
    n_valid = (pad_end[-1:] // MOE_BLK).astype(jnp.int32)
    xs = jnp.take(x_bf, slot_tok, axis=0)
    ys = _experts(xs, block_e + layer * MOE_EXPERTS, n_valid, w_gate, w_up, w_down)
    slot2 = slot.reshape(n, MOE_TOPK)
    y0 = jnp.take(ys, slot2[:, 0], axis=0)
    y1 = jnp.take(ys, slot2[:, 1], axis=0)
    wts_pad = jnp.pad(wts.astype(F32), ((0, 0), (0, 128 - MOE_TOPK)))
    return _combine(h, y0, y1, wts_pad, mod5, g_final, nb=nb, nt=nt, ctx_first=ctx_first, final=final)


_ROT_SRC = np.array(list(range(8, 16)) + list(range(0, 8)) + list(range(24, 32)) + list(range(16, 24)))
_ROT_SIGN = np.array([-1.0] * 8 + [1.0] * 8 + [-1.0] * 8 + [1.0] * 8, np.float32)


def _rot_cols(w):
    k = w.shape[-1] // ROPE_DIM
    src = np.concatenate([_ROT_SRC + ROPE_DIM * i for i in range(k)])
    sign = np.tile(_ROT_SIGN, k)
    return w[..., src] * sign


def _rope_tables(n_ctx, n_lat):
    rows = n_lat // GRID_W
    row = jnp.repeat(jnp.arange(rows, dtype=F32), GRID_W)
    col = (jnp.arange(rows * GRID_W) % GRID_W).astype(F32)
    nf = ROPE_DIM // 4
    inv = ROPE_BASE ** (-jnp.arange(nf, dtype=F32) / nf)
    ar = row[:, None] * inv
    ac = col[:, None] * inv
    ang = jnp.concatenate([ar, ar, ac, ac], axis=-1)
    cos = jnp.concatenate([jnp.ones((n_ctx, ROPE_DIM), F32), jnp.cos(ang)], axis=0)
    sin = jnp.concatenate([jnp.zeros((n_ctx, ROPE_DIM), F32), jnp.sin(ang)], axis=0)
    return cos, sin


def _block_diag(blocks):
    g, a, b = blocks.shape
    tiled = jnp.tile(blocks.reshape(g * a, b), (1, g))
    rows = lax.broadcasted_iota(jnp.int32, (g * a, g * b), 0) // a
    cols = lax.broadcasted_iota(jnp.int32, (g * a, g * b), 1) // b
    return jnp.where(rows == cols, tiled, 0.0)


def _pick_tk(t_all):
    best = 128
    for tk in range(128, ATTN_TK_MAX + 1, 128):
        if t_all % tk == 0:
            best = tk
    return best


def kernel(x, c, ctx, c_ctx, w_mod, b_mod, norm1_g, norm2_g, w_in, mla_q_norm_g, mla_kv_norm_g, mla_w_uq, mla_w_ukv, rw_mu, rw_w0, rw_w2, rw_a0, rw_a2, rw_g2, rw_k_k, rw_k_a, rw_r_k, rw_lnx_g, rw_lnx_b, s5_a_re, s5_a_im, s5_log_dt, s5_b_re, s5_b_im, s5_c_re, s5_c_im, s5_d, s5_glu_w, s5_glu_b, diff_lq1, diff_lk1, diff_lq2, diff_lk2, diff_subln_g, w_branch, w_out, router_g_w, router_g_b, router_e_w, router_e_b, exp_w_gate, exp_w_up, exp_w_down, final_norm_g):
    nb, n_lat, d = x.shape
    n_ctx = ctx.shape[1]
    depth = w_mod.shape[0]
    t_all = n_ctx + n_lat
    assert d == D_MODEL and n_ctx == TM and n_lat % TM == 0
    tpb = t_all // TM
    tk = _pick_tk(t_all)
    hw = RW_HEADS * RW_HEAD

    cos, sin = _rope_tables(n_ctx, n_lat)
    mla_scale = (MLA_NOPE + MLA_ROPE) ** -0.5 * LOG2E
    z32 = jnp.zeros((t_all, 32), F32)
    cq_tab = jnp.concatenate([jnp.ones((t_all, 64), F32), cos, z32], axis=1) * mla_scale
    sq_tab = jnp.concatenate([jnp.zeros((t_all, 64), F32), sin, z32], axis=1) * mla_scale
    ck_tab = jnp.concatenate([cos, sin, jnp.zeros((t_all, 64), F32)], axis=1)
    dcos = jnp.tile(cos, (1, 8))
    dsin = jnp.tile(sin, (1, 8))

    c_rows = jnp.concatenate([c, c_ctx[None, :], jnp.zeros((8 - nb - 1, d), F32)], axis=0)

    h = jnp.concatenate([ctx, x], axis=1).reshape(nb * t_all, d)

    for l in range(depth):
        last = l == depth - 1
        r0, nt = (1, tpb - 1) if last else (0, tpb)

        mod = _mm(c_rows, w_mod[l].astype(BF16), b_mod[l][None, :], tm=8, tn=1536, pre_silu=True, name="mod")
        mods = [mod[:nb + 1, i * d:(i + 1) * d].reshape(nb + 1, 1, d) for i in range(6)]

        wi = w_in[l]
        o_rw, o_s5, o_df, o_gt = 416, 1440, 1696, 2464
        w_kr = wi[:, 384:416]
        w_dq, w_dk, w_dv = wi[:, o_df:o_df + 256], wi[:, o_df + 256:o_df + 512], wi[:, o_df + 512:o_df + 768]
        w_ext = jnp.concatenate(
            [wi[:, o_rw:o_s5],
             wi[:, :416], _rot_cols(w_kr), jnp.zeros((d, 64), F32),
             wi[:, o_s5:o_df],
             w_dq, w_dk, w_dv, _rot_cols(w_dq), _rot_cols(w_dk),
             wi[:, o_gt:]], axis=1).astype(BF16)
        p = _inproj(h, norm1_g[l][None, :], mods[0], mods[1], w_ext, nb=nb, tpb=tpb)

        wq = mla_w_uq[l].reshape(MLA_Q_LORA, MLA_HEADS, MLA_NOPE + MLA_ROPE)
        zq = jnp.zeros((MLA_Q_LORA, MLA_HEADS, 32), F32)
        wa = jnp.concatenate([wq, zq], axis=2).reshape(MLA_Q_LORA, 512).astype(BF16)
        wb = jnp.concatenate([jnp.zeros((MLA_Q_LORA, MLA_HEADS, 64), F32), _rot_cols(wq[:, :, MLA_NOPE:]), zq],
                             axis=2).reshape(MLA_Q_LORA, 512).astype(BF16)
        wkv = mla_w_ukv[l].reshape(MLA_KV_LORA, MLA_HEADS, MLA_NOPE + MLA_V)
        wk = jnp.concatenate([wkv[:, :, :MLA_NOPE], jnp.zeros((MLA_KV_LORA, MLA_HEADS, 64), F32)],
                             axis=2).reshape(MLA_KV_LORA, 512).astype(BF16)
        wv = wkv[:, :, MLA_NOPE:].reshape(MLA_KV_LORA, MLA_HEADS * MLA_V).astype(BF16)
        pk_np = np.zeros((128, 512), np.float32)
        for hh in range(MLA_HEADS):
            for i in range(32):
                pk_np[i, hh * 128 + 64 + i] = 1.0
                pk_np[32 + i, hh * 128 + 64 + i] = 1.0
        pk = jnp.asarray(pk_np, BF16)
        q_m, k_m, v_m = _mla_prep(p, mla_q_norm_g[l][None, :], mla_kv_norm_g[l][None, :], wa, wb, wk, wv, pk,
                                  cq_tab, sq_tab, ck_tab, nb=nb, tpb=tpb)
        ya_lat = _flash_mla(q_m, k_m, v_m, nb=nb, t_all=t_all, q_start=n_ctx, q_len=n_lat, k_len=t_all, tk=tk)

        q_d, k_d, v_d = _diff_prep(p, dcos, dsin, nb=nb, tpb=tpb)
        lam_init = 0.8 - 0.6 * math.exp(-0.3 * l)
        lam = (jnp.exp(jnp.sum(diff_lq1[l] * diff_lk1[l])) - jnp.exp(jnp.sum(diff_lq2[l] * diff_lk2[l])) + lam_init)
        lam_row = jnp.full((1, 256), lam, F32)
        g_row = jnp.tile(diff_subln_g[l], DIFF_HEADS)[None, :]
        yd_lat = _flash_diff(q_d, k_d, v_d, lam_row, g_row, lam_init=lam_init, nb=nb, t_all=t_all,
                             q_start=n_ctx, q_len=n_lat, k_len=t_all, tk=tk)
        if last:
            ya, yd = ya_lat, yd_lat
        else:
            ya_ctx = _flash_mla(q_m, k_m, v_m, nb=nb, t_all=t_all, q_start=0, q_len=n_ctx, k_len=n_ctx, tk=n_ctx)
            yd_ctx = _flash_diff(q_d, k_d, v_d, lam_row, g_row, lam_init=lam_init, nb=nb, t_all=t_all,
                                 q_start=0, q_len=n_ctx, k_len=n_ctx, tk=n_ctx)
            comb = lambda a_c, a_l: jnp.concatenate(
                [a_c.reshape(nb, n_ctx, -1), a_l.reshape(nb, n_lat, -1)], axis=1).reshape(nb * t_all, -1)
            ya, yd = comb(ya_ctx, ya_lat), comb(yd_ctx, yd_lat)

        wlo = jnp.zeros((128, 4 * hw), F32)
        wlo = wlo.at[:64, 0:hw].set(rw_w2[l, 0]).at[:64, hw:2 * hw].set(rw_w2[l, 1])
        wlo = wlo.at[64:, 2 * hw:3 * hw].set(rw_a2[l, 0]).at[64:, 3 * hw:].set(rw_a2[l, 1])
        vecs = jnp.stack([rw_k_k[l], rw_k_a[l], rw_r_k[l].reshape(hw), rw_w0[l, 0], rw_w0[l, 1],
                          rw_a0[l, 0], rw_a0[l, 1], jnp.zeros((hw,), F32)], axis=0)
        (r_, v_, kk_, w0_, k0_, b0_, w1_, k1_, b1_, bonus, gate_rw) = _rw_prep(
            p, rw_mu[l][None, :], wlo.astype(BF16), rw_g2[l].astype(BF16), vecs, nb=nb, tpb=tpb, r0=0, nt=tpb)
        yf, yb_ = _rw_scan(r_, v_, kk_, w0_, k0_, b0_, w1_, k1_, b1_, nb=nb, t_all=t_all, n_ctx=n_ctx)
        ln_vecs = jnp.concatenate([rw_lnx_g[l][None, :], rw_lnx_b[l][None, :], jnp.zeros((6, hw), F32)], axis=0)
        if last:
            trim = lambda a: a.reshape(nb, t_all, hw)[:, n_ctx:].reshape(nb * n_lat, hw)
            bonus, gate_rw = trim(bonus), trim(gate_rw)
        y_rw = _rw_post(yf, yb_, bonus, gate_rw, ln_vecs, nb=nb, tpb_y=tpb, r0_y=r0, nt=nt)

        bbs, abs_, cfs = [], [], []
        for dd in range(2):
            lr, li = s5_a_re[l, dd], s5_a_im[l, dd]
            dt = jnp.exp(s5_log_dt[l, dd])[:, None]
            mag = jnp.exp(lr * dt)
            ab_re, ab_im = mag * jnp.cos(li * dt), mag * jnp.sin(li * dt)
            den = lr * lr + li * li
            nr, ni = ab_re - 1.0, ab_im
            cf_re = (nr * lr + ni * li) / den
            cf_im = (ni * lr - nr * li) / den
            bre, bim = s5_b_re[l, dd], s5_b_im[l, dd]
            bb_re = cf_re[..., None] * bre - cf_im[..., None] * bim
            bb_im = cf_re[..., None] * bim + cf_im[..., None] * bre
            bbs.append(jnp.concatenate([_block_diag(bb_re.transpose(0, 2, 1)),
                                        _block_diag(bb_im.transpose(0, 2, 1))], axis=1))
            abs_.append(jnp.concatenate([ab_re.reshape(-1), ab_im.reshape(-1)])[None, :])
            cfs.append(jnp.concatenate([_block_diag(s5_c_re[l, dd].transpose(0, 2, 1)),
                                        -_block_diag(s5_c_im[l, dd].transpose(0, 2, 1))], axis=0))
        yf_s5, yb_s5 = _s5_scan(p, jnp.concatenate(bbs, axis=1).astype(BF16), jnp.stack(abs_, axis=0),
                                cfs[0].astype(BF16), cfs[1].astype(BF16), nb=nb, t_all=t_all, n_ctx=n_ctx)
        y_s5 = _s5_post(yf_s5, yb_s5, p, s5_d[l].reshape(1, 256), s5_glu_w[l].astype(BF16),
                        s5_glu_b[l][None, :], nb=nb, tpb=tpb, r0=r0, nt=nt)

        h = _merge(ya, y_rw, y_s5, yd, p, h, mods[2], w_branch[l].astype(BF16), w_out[l].astype(BF16),
                   nb=nb, tpb=tpb, r0=r0, nt=nt)

        wr = jnp.concatenate([router_g_w[l], router_e_w[l], jnp.zeros((d, 128 - MOE_GROUPS - MOE_EXPERTS), F32)], axis=1)
        wr_h = wr.astype(BF16)
        wr_m = (wr - wr_h.astype(F32)).astype(BF16)
        rbias = jnp.concatenate([router_g_b[l], router_e_b[l],
                                 jnp.zeros((128 - MOE_GROUPS - MOE_EXPERTS,), F32)])[None, :]
        h = _moe(h, norm2_g[l][None, :], mods[3], mods[4], mods[5], wr_h, wr_m, rbias,
                 exp_w_gate.reshape(depth * MOE_EXPERTS, d, D_EXPERT), exp_w_up.reshape(depth * MOE_EXPERTS, d, D_EXPERT),
                 exp_w_down.reshape(depth * MOE_EXPERTS, D_EXPERT, d),
                 final_norm_g[None, :], layer=l, nb=nb, nt=nt, ctx_first=not last, final=last)

    return h.reshape(nb, n_lat, d)
```

```python
import functools
import math

import jax
import jax.numpy as jnp
import numpy as np
from jax import lax
from jax.experimental import pallas as pl
from jax.experimental.pallas import tpu as pltpu

F32 = jnp.float32
BF16 = jnp.bfloat16

TM = 256
VMEM_LIMIT = 48 * 1024 * 1024

D_MODEL = 1024
GRID_W = 64
ROPE_DIM = 32
ROPE_BASE = 10000.0
NORM_EPS = 1e-6
MLA_HEADS, MLA_NOPE, MLA_ROPE, MLA_V = 4, 64, 32, 64
MLA_Q_LORA, MLA_KV_LORA = 256, 128
RW_HEADS, RW_HEAD = 4, 64
RW_LN_EPS = 64e-5
S5_GROUPS, S5_GROUP_CH, S5_STATE = 16, 16, 64
DIFF_HEADS, DIFF_HD = 4, 32
DIFF_EPS = 1e-5
MOE_GROUPS, MOE_PER_GROUP, MOE_TOPK = 4, 8, 2
MOE_EXPERTS = MOE_GROUPS * MOE_PER_GROUP
D_EXPERT = 512
MOE_BLK = 256
RW_CHUNK = 64
S5_CHUNK = 128
ATTN_TK_MAX = 2816
LOG2E = math.log2(math.e)

EXT_RW, EXT_MLA, EXT_S5, EXT_DIFF, EXT_GATE = 0, 1024, 1536, 1792, 3072
N_EXT = 7168


def _cparams(sem, vmem=VMEM_LIMIT):
    return pltpu.CompilerParams(dimension_semantics=sem, vmem_limit_bytes=vmem)


def _dot(a, b):
    return jnp.dot(a, b, preferred_element_type=F32)


def _dot_nt(a, b):
    return lax.dot_general(a, b, (((1,), (1,)), ((), ())), preferred_element_type=F32)


def _split_dot(x, e):
    hi = x.astype(BF16)
    mid = (x - hi.astype(F32)).astype(BF16)
    return _dot(hi, e) + _dot(mid, e)


def _block_ones(n, blk):
    r = lax.broadcasted_iota(jnp.int32, (n, n), 0) // blk
    c = lax.broadcasted_iota(jnp.int32, (n, n), 1) // blk
    return (r == c).astype(BF16)


def _sigmoid(x):
    return 1.0 / (1.0 + jnp.exp(-x))


def _silu(x):
    return x * _sigmoid(x)


def _softplus(x):
    return jnp.maximum(x, 0.0) + jnp.log(1.0 + jnp.exp(-jnp.abs(x)))


def _rms(x, g, eps):
    return x * lax.rsqrt(jnp.mean(x * x, axis=-1, keepdims=True) + eps) * g


def _mm_kernel(x_ref, w_ref, b_ref, o_ref, *, pre_silu):
    x = x_ref[...].astype(F32)
    if pre_silu:
        x = _silu(x)
    o_ref[...] = _dot(x.astype(BF16), w_ref[...]) + b_ref[...]


def _mm(x, w, b, *, tm, tn, pre_silu=False, name="mm"):
    m, k = x.shape
    n = w.shape[1]
    return pl.pallas_call(
        functools.partial(_mm_kernel, pre_silu=pre_silu),
        grid=(m // tm, n // tn),
        in_specs=[pl.BlockSpec((tm, k), lambda i, j: (i, 0)),
                  pl.BlockSpec((k, tn), lambda i, j: (0, j)),
                  pl.BlockSpec((1, tn), lambda i, j: (0, j))],
        out_specs=pl.BlockSpec((tm, tn), lambda i, j: (i, j)),
        out_shape=jax.ShapeDtypeStruct((m, n), F32),
        compiler_params=_cparams(("parallel", "arbitrary")),
        name=name,
    )(x, w, b)


def _inproj_kernel(h_ref, g_ref, sh_ref, sc_ref, w_ref, o_ref):
    x = _rms(h_ref[...], g_ref[...], NORM_EPS)
    xn = (x * (1.0 + sc_ref[0]) + sh_ref[0]).astype(BF16)
    o_ref[...] = _dot(xn, w_ref[...])


def _inproj(h, g, shift, scale, w_ext, *, nb, tpb):
    n = h.shape[0]
    tn = N_EXT // 2
    modmap = lambda j, b, r: (jnp.where(r == 0, nb, b), 0, 0)
    return pl.pallas_call(
        _inproj_kernel,
        grid=(N_EXT // tn, nb, tpb),
        in_specs=[pl.BlockSpec((TM, D_MODEL), lambda j, b, r: (b * tpb + r, 0)),
                  pl.BlockSpec((1, D_MODEL), lambda j, b, r: (0, 0)),
                  pl.BlockSpec((1, 1, D_MODEL), modmap),
                  pl.BlockSpec((1, 1, D_MODEL), modmap),
                  pl.BlockSpec((D_MODEL, tn), lambda j, b, r: (0, j))],
        out_specs=pl.BlockSpec((TM, tn), lambda j, b, r: (b * tpb + r, j)),
        out_shape=jax.ShapeDtypeStruct((n, N_EXT), F32),
        compiler_params=_cparams(("parallel", "parallel", "parallel")),
        name="inproj",
    )(h, g, shift, scale, w_ext)


def _mla_prep_kernel(p_ref, gq_ref, gkv_ref, wa_ref, wb_ref, wk_ref, wv_ref, pk_ref,
                     cq_ref, sq_ref, ck_ref, q_ref, k_ref, v_ref):
    seg = p_ref[...]
    nq = _rms(seg[:, :MLA_Q_LORA], gq_ref[...], NORM_EPS).astype(BF16)
    nkv = _rms(seg[:, MLA_Q_LORA:MLA_Q_LORA + MLA_KV_LORA], gkv_ref[...], NORM_EPS).astype(BF16)
    cq = jnp.concatenate([cq_ref[...]] * MLA_HEADS, axis=1)
    sq = jnp.concatenate([sq_ref[...]] * MLA_HEADS, axis=1)
    q = _dot(nq, wa_ref[...]) * cq + _dot(nq, wb_ref[...]) * sq
    q_ref[...] = q.astype(BF16)
    kr = (seg[:, 384:512] * ck_ref[...]).astype(BF16)
    k_ref[...] = (_dot(nkv, wk_ref[...]) + _dot(kr, pk_ref[...])).astype(BF16)
    v_ref[...] = _dot(nkv, wv_ref[...]).astype(BF16)


def _mla_prep(p, gq, gkv, wa, wb, wk, wv, pk, cq_tab, sq_tab, ck_tab, *, nb, tpb):
    n = p.shape[0]
    tok = lambda b, r: (b * tpb + r, 0)
    pos = lambda b, r: (r, 0)
    full = lambda b, r: (0, 0)
    return pl.pallas_call(
        _mla_prep_kernel,
        grid=(nb, tpb),
        in_specs=[pl.BlockSpec((TM, 512), lambda b, r: (b * tpb + r, EXT_MLA // 512)),
                  pl.BlockSpec((1, MLA_Q_LORA), full),
                  pl.BlockSpec((1, MLA_KV_LORA), full),
                  pl.BlockSpec((MLA_Q_LORA, 512), full),
                  pl.BlockSpec((MLA_Q_LORA, 512), full),
                  pl.BlockSpec((MLA_KV_LORA, 512), full),
                  pl.BlockSpec((MLA_KV_LORA, 256), full),
                  pl.BlockSpec((128, 512), full),
                  pl.BlockSpec((TM, 128), pos),
                  pl.BlockSpec((TM, 128), pos),
                  pl.BlockSpec((TM, 128), pos)],
        out_specs=[pl.BlockSpec((TM, 512), tok),
                   pl.BlockSpec((TM, 512), tok),
                   pl.BlockSpec((TM, 256), tok)],
        out_shape=[jax.ShapeDtypeStruct((n, 512), BF16),
                   jax.ShapeDtypeStruct((n, 512), BF16),
                   jax.ShapeDtypeStruct((n, 256), BF16)],
        compiler_params=_cparams(("parallel", "parallel")),
        name="mla_prep",
    )(p, gq, gkv, wa, wb, wk, wv, pk, cq_tab, sq_tab, ck_tab)


def _flash_mla_kernel(q_ref, k_ref, v_ref, o_ref, m_ref, l_ref, acc_ref):
    kk = pl.program_id(2)
    tq = q_ref.shape[0]

    @pl.when(kk == 0)
    def _():
        m_ref[...] = jnp.full(m_ref.shape, -jnp.inf, F32)
        l_ref[...] = jnp.zeros(l_ref.shape, F32)
        acc_ref[...] = jnp.zeros(acc_ref.shape, F32)

    ps, alphas = [], []
    for h in range(MLA_HEADS):
        s = _dot_nt(q_ref[:, h * 128:(h + 1) * 128], k_ref[:, h * 128:(h + 1) * 128])
        m_prev = m_ref[h]
        m_new = jnp.maximum(m_prev, jnp.max(s, axis=-1, keepdims=True))
        alpha = jnp.exp2(m_prev - m_new)
        p = jnp.exp2(s - m_new)
        l_ref[h] = alpha * l_ref[h] + jnp.sum(p, axis=-1, keepdims=True)
        m_ref[h] = m_new
        alphas.append(alpha)
        ps.append(p.astype(BF16))
    pv = _dot(jnp.concatenate(ps, axis=0), v_ref[...])
    acc_ref[...] = jnp.concatenate(alphas, axis=0) * acc_ref[...] + pv

    @pl.when(kk == pl.num_programs(2) - 1)
    def _():
        lane = lax.broadcasted_iota(jnp.int32, (tq, MLA_HEADS * MLA_V), 1)
        out = jnp.zeros((tq, MLA_HEADS * MLA_V), F32)
        for h in range(MLA_HEADS):
            out = jnp.where(lane // MLA_V == h, acc_ref[h * tq:(h + 1) * tq, :] / l_ref[h], out)
        o_ref[...] = out


def _flash_mla(q, k, v, *, nb, t_all, q_start, q_len, k_len, tk):
    tq = TM
    tpb_q, tpb_k = t_all // tq, t_all // tk
    q0 = q_start // tq
    nq = q_len // tq
    return pl.pallas_call(
        _flash_mla_kernel,
        grid=(nb, nq, k_len // tk),
        in_specs=[pl.BlockSpec((tq, 512), lambda b, i, kk: (b * tpb_q + q0 + i, 0)),
                  pl.BlockSpec((tk, 512), lambda b, i, kk: (b * tpb_k + kk, 0)),
                  pl.BlockSpec((tk, 256), lambda b, i, kk: (b * tpb_k + kk, 0))],
        out_specs=pl.BlockSpec((tq, 256), lambda b, i, kk: (b * nq + i, 0)),
        out_shape=jax.ShapeDtypeStruct((nb * q_len, 256), F32),
        scratch_shapes=[pltpu.VMEM((MLA_HEADS, tq, 1), F32), pltpu.VMEM((MLA_HEADS, tq, 1), F32),
                        pltpu.VMEM((MLA_HEADS * tq, 256), F32)],
        compiler_params=_cparams(("parallel", "parallel", "arbitrary")),
        name="flash_mla",
    )(q, k, v)


def _diff_prep(p, cos_tab, sin_tab, *, nb, tpb):
    n = p.shape[0]
    tok = lambda b, r: (b * tpb + r, 0)
    pos = lambda b, r: (r, 0)
    return pl.pallas_call(
        _diff_prep_kernel_cols,
        grid=(nb, tpb),
        in_specs=[pl.BlockSpec((TM, 256), lambda b, r, c=c: (b * tpb + r, EXT_DIFF // 256 + c))
                  for c in range(5)]
                 + [pl.BlockSpec((TM, 256), pos), pl.BlockSpec((TM, 256), pos)],
        out_specs=[pl.BlockSpec((TM, 256), tok)] * 3,
        out_shape=[jax.ShapeDtypeStruct((n, 256), BF16)] * 3,
        compiler_params=_cparams(("parallel", "parallel")),
        name="diff_prep",
    )(p, p, p, p, p, cos_tab, sin_tab)


def _diff_prep_kernel_cols(q_in, k_in, v_in, qr_in, kr_in, cos_ref, sin_ref, q_ref, k_ref, v_ref):
    cos, sin = cos_ref[...], sin_ref[...]
    scale = DIFF_HD ** -0.5 * LOG2E
    q_ref[...] = ((q_in[...] * cos + qr_in[...] * sin) * scale).astype(BF16)
    k_ref[...] = (k_in[...] * cos + kr_in[...] * sin).astype(BF16)
    v_ref[...] = v_in[...].astype(BF16)


def _flash_diff_kernel(q_ref, k_ref, v_ref, lam_ref, g_ref, o_ref, qs_ref, m_ref, l_ref, acc_ref, *, lam_init):
    kk = pl.program_id(2)
    tq = q_ref.shape[0]
    nsm = 2 * DIFF_HEADS

    @pl.when(kk == 0)
    def _():
        m_ref[...] = jnp.full(m_ref.shape, -jnp.inf, F32)
        l_ref[...] = jnp.zeros(l_ref.shape, F32)
        acc_ref[...] = jnp.zeros(acc_ref.shape, F32)
        q = q_ref[...]
        lane = lax.broadcasted_iota(jnp.int32, (tq, 256), 1)
        for i in range(nsm):
            qs_ref[i * tq:(i + 1) * tq, :] = jnp.where((lane // DIFF_HD) == i, q, jnp.zeros_like(q))

    k = k_ref[...]
    ps, alphas = [], []
    for i in range(nsm):
        rows = slice(i * tq, (i + 1) * tq)
        s = _dot_nt(qs_ref[rows, :], k)
        m_prev = m_ref[rows, :]
        m_new = jnp.maximum(m_prev, jnp.max(s, axis=-1, keepdims=True))
        alpha = jnp.exp2(m_prev - m_new)
        p = jnp.exp2(s - m_new)
        l_ref[rows, :] = alpha * l_ref[rows, :] + jnp.sum(p, axis=-1, keepdims=True)
        m_ref[rows, :] = m_new
        alphas.append(alpha)
        ps.append(p.astype(BF16))
    pv = _dot(jnp.concatenate(ps, axis=0), v_ref[...])
    acc_ref[...] = jnp.concatenate(alphas, axis=0) * acc_ref[...] + pv

    @pl.when(kk == pl.num_programs(2) - 1)
    def _():
        lane = lax.broadcasted_iota(jnp.int32, (tq, 256), 1)
        o = jnp.zeros((tq, 256), F32)
        for h in range(DIFF_HEADS):
            r0, r1 = (2 * h) * tq, (2 * h + 1) * tq
            o0 = acc_ref[r0:r0 + tq, :] / l_ref[r0:r0 + tq, :]
            o1 = acc_ref[r1:r1 + tq, :] / l_ref[r1:r1 + tq, :]
            o = jnp.where((lane // (2 * DIFF_HD)) == h, o0 - lam_ref[...] * o1, o)
        ms = _split_dot(o * o, _block_ones(256, 2 * DIFF_HD)) * (1.0 / (2 * DIFF_HD))
        o_ref[...] = o * lax.rsqrt(ms + DIFF_EPS) * g_ref[...] * (1.0 - lam_init)


def _flash_diff(q, k, v, lam_row, g_row, *, lam_init, nb, t_all, q_start, q_len, k_len, tk):
    tq = TM
    tpb_q, tpb_k = t_all // tq, t_all // tk
    q0 = q_start // tq
    nq = q_len // tq
    return pl.pallas_call(
        functools.partial(_flash_diff_kernel, lam_init=lam_init),
        grid=(nb, nq, k_len // tk),
        in_specs=[pl.BlockSpec((tq, 256), lambda b, i, kk: (b * tpb_q + q0 + i, 0)),
                  pl.BlockSpec((tk, 256), lambda b, i, kk: (b * tpb_k + kk, 0)),
                  pl.BlockSpec((tk, 256), lambda b, i, kk: (b * tpb_k + kk, 0)),
                  pl.BlockSpec((1, 256), lambda b, i, kk: (0, 0)),
                  pl.BlockSpec((1, 256), lambda b, i, kk: (0, 0))],
        out_specs=pl.BlockSpec((tq, 256), lambda b, i, kk: (b * nq + i, 0)),
        out_shape=jax.ShapeDtypeStruct((nb * q_len, 256), F32),
        scratch_shapes=[pltpu.VMEM((2 * DIFF_HEADS * tq, 256), BF16),
                        pltpu.VMEM((2 * DIFF_HEADS * tq, 1), F32),
                        pltpu.VMEM((2 * DIFF_HEADS * tq, 1), F32),
                        pltpu.VMEM((2 * DIFF_HEADS * tq, 256), F32)],
        compiler_params=_cparams(("parallel", "parallel", "arbitrary")),
        name="flash_diff",
    )(q, k, v, lam_row, g_row)


def _rw_prep_kernel(p_ref, prev_ref, next_ref, mu_ref, wlo_ref, g2_ref, vec_ref,
                    r_ref, v_ref, kk_ref, w0_ref, k0_ref, b0_ref, w1_ref, k1_ref, b1_ref,
                    bonus_ref, gate_ref, *, r0, lat_last):
    r = pl.program_id(1) + r0
    p = p_ref[...]
    row = lax.broadcasted_iota(jnp.int32, p.shape, 0)
    first_tile = jnp.logical_or(r == 0, r == 1)
    last_tile = jnp.logical_or(r == 0, r == lat_last)
    prev_row = jnp.where(first_tile, 0.0, prev_ref[7:8, :])
    next_row = jnp.where(last_tile, 0.0, next_ref[0:1, :])
    up = jnp.where(row == 0, prev_row, pltpu.roll(p, 1, 0))
    dn = jnp.where(row == TM - 1, next_row, pltpu.roll(p, TM - 1, 0))
    z = p + (0.5 * (up + dn) - p) * mu_ref[...]

    hw = RW_HEADS * RW_HEAD
    rr, k, v = z[:, :hw], z[:, hw:2 * hw], z[:, 2 * hw:3 * hw]
    lo = z[:, 3 * hw:3 * hw + 128]
    lane = lax.broadcasted_iota(jnp.int32, lo.shape, 1)
    lo = jnp.where(lane < 64, jnp.tanh(lo), lo).astype(BF16)
    wa = _dot(lo, wlo_ref[...])
    gate_ref[...] = _dot(_sigmoid(z[:, 3 * hw + 128:]).astype(BF16), g2_ref[...])

    e4 = _block_ones(hw, RW_HEAD)
    k_k, k_a, r_k = vec_ref[0:1, :], vec_ref[1:2, :], vec_ref[2:3, :]
    kk = k * k_k
    nrm = jnp.maximum(jnp.sqrt(_split_dot(kk * kk, e4)), 1e-12)
    kk = kk / nrm
    r_ref[...] = rr
    v_ref[...] = v
    kk_ref[...] = kk
    ksum = jnp.zeros_like(k)
    for d, (w_ref, kd_ref, b_ref) in enumerate(((w0_ref, k0_ref, b0_ref), (w1_ref, k1_ref, b1_ref))):
        w0 = vec_ref[3 + d:4 + d, :]
        a0 = vec_ref[5 + d:6 + d, :]
        wd = -_softplus(-(w0 + wa[:, d * hw:(d + 1) * hw])) - 0.5
        w_ref[...] = jnp.exp(-jnp.exp(wd))
        ad = _sigmoid(a0 + wa[:, (2 + d) * hw:(3 + d) * hw])
        kd = k * (1.0 + (ad - 1.0) * k_a)
        kd_ref[...] = kd
        b_ref[...] = kk * ad
        ksum = ksum + kd
    bonus_ref[...] = _split_dot(rr * ksum * r_k, e4) * v


def _rw_prep(p, mu, wlo, g2, vecs, *, nb, tpb, r0, nt):
    n_out = nb * nt * TM
    hw = RW_HEADS * RW_HEAD
    n_rows8 = p.shape[0] // 8
    tok = lambda b, r: (b * tpb + r0 + r, EXT_RW // 1024)
    prev = lambda b, r: (jnp.maximum((b * tpb + r0 + r) * (TM // 8) - 1, 0), EXT_RW // 1024)
    nxt = lambda b, r: (jnp.minimum((b * tpb + r0 + r + 1) * (TM // 8), n_rows8 - 1), EXT_RW // 1024)
    out = lambda b, r: (b * nt + r, 0)
    full = lambda b, r: (0, 0)
    return pl.pallas_call(
        functools.partial(_rw_prep_kernel, r0=r0, lat_last=tpb - 1),
        grid=(nb, nt),
        in_specs=[pl.BlockSpec((TM, 1024), tok),
                  pl.BlockSpec((8, 1024), prev),
                  pl.BlockSpec((8, 1024), nxt),
                  pl.BlockSpec((1, 1024), full),
                  pl.BlockSpec((128, 4 * hw), full),
                  pl.BlockSpec((128, hw), full),
                  pl.BlockSpec((8, hw), full)],
        out_specs=[pl.BlockSpec((TM, hw), out)] * 11,
        out_shape=[jax.ShapeDtypeStruct((n_out, hw), F32)] * 11,
        compiler_params=_cparams(("parallel", "parallel")),
        name="rw_prep",
    )(p, p, p, mu, wlo, g2, vecs)


def _rw_scan_kernel(rf, vf, kkf, wf, kf, bf, rb, vb, kkb, wb, kb, bb, yf_ref, yb_ref, s_ref, *, nb):
    c = RW_CHUNK

    @pl.when(pl.program_id(0) == 0)
    def _():
        s_ref[...] = jnp.zeros(s_ref.shape, F32)

    e2 = _block_ones(128, RW_HEAD)
    z2 = jnp.zeros((128, 128), BF16)
    rhs_pair = jnp.concatenate([jnp.concatenate([e2, z2], axis=1), jnp.concatenate([z2, e2], axis=1)], axis=0)
    e22 = jnp.concatenate([e2, e2], axis=0)
    lane = lax.broadcasted_iota(jnp.int32, (RW_HEAD, 128), 1)
    sub = lax.broadcasted_iota(jnp.int32, (RW_HEAD, 128), 0)
    diag = (lane % RW_HEAD) == sub
    sub8 = lax.broadcasted_iota(jnp.int32, (8, 128), 0)
    dirs = ((rf, vf, kkf, wf, kf, bf, yf_ref), (rb, vb, kkb, wb, kb, bb, yb_ref))

    def allreduce_rows(x):
        t = x[0:8]
        for i in range(1, 8):
            t = t + x[8 * i:8 * i + 8]
        for sh in (4, 2, 1):
            t = t + pltpu.roll(t, sh, 0)
        return t

    def col(row):
        return jnp.where(diag, row, 0.0).astype(BF16)

    def group(g, carry):
        tiles, ytiles = {}, {}
        for d, refs in enumerate(dirs):
            base = pl.multiple_of((g if d == 0 else c // 8 - 1 - g) * 8, 8)
            if d == 0:
                prev = lambda x, sh: jnp.where(sub8 >= sh, pltpu.roll(x, sh, 0), 1.0)
            else:
                prev = lambda x, sh: jnp.where(sub8 < 8 - sh, pltpu.roll(x, 8 - sh, 0), 1.0)
            last = 7 if d == 0 else 0
            for b in range(nb):
                for hp in range(2):
                    r_, v_, kk_, w_, k_, b_ = [ref[b, pl.ds(base, 8), pl.ds(hp * 128, 128)] for ref in refs[:6]]
                    gam = w_
                    for sh in (1, 2, 4):
                        gam = gam * prev(gam, sh)
                    inv = 1.0 / gam
                    tiles[d, b, hp] = (base, v_, b_ * inv, kk_ * prev(gam, 1), k_ * inv, r_ * gam,
                                       gam[last:last + 1, :])
                    ytiles[d, b, hp] = jnp.zeros((8, 128), F32)
        units = [(d, b, hp) for d in range(2) for b in range(nb) for hp in range(2)]
        for jj in range(8):
            lhs = []
            for (d, b, hp) in units:
                rw = slice(jj, jj + 1) if d == 0 else slice(7 - jj, 8 - jj)
                _, _, bh, kkh, kh, rh, _ = tiles[d, b, hp]
                lhs.append(jnp.concatenate([col(bh[rw]), col(kkh[rw])], axis=1))
                lhs.append(jnp.concatenate([col(kh[rw]), col(rh[rw])], axis=1))
            cm = _dot(jnp.concatenate(lhs, axis=0), rhs_pair)
            for u, (d, b, hp) in enumerate(units):
                j = jj if d == 0 else 7 - jj
                v_ = tiles[d, b, hp][1]
                r0 = 2 * u * RW_HEAD
                bc, kkc = cm[r0:r0 + RW_HEAD, :128], cm[r0:r0 + RW_HEAD, 128:]
                kc, rc = cm[r0 + RW_HEAD:r0 + 2 * RW_HEAD, :128], cm[r0 + RW_HEAD:r0 + 2 * RW_HEAD, 128:]
                s = s_ref[u]
                sa = jnp.concatenate([allreduce_rows(kkc * s)] * 8, axis=0)
                s = s - bc * sa + kc * v_[j:j + 1]
                s_ref[u] = s
                ytiles[d, b, hp] = jnp.where(sub8 == j, allreduce_rows(rc * s), ytiles[d, b, hp])
        for d, refs in enumerate(dirs):
            for b in range(nb):
                for hp in range(2):
                    u = (d * nb + b) * 2 + hp
                    gl = tiles[d, b, hp][6]
                    gh = gl.astype(BF16).astype(F32)
                    s_ref[u] = s_ref[u] * _dot(jnp.concatenate([col(gh), col(gl - gh)], axis=1), e22)
                    refs[6][b, pl.ds(tiles[d, b, hp][0], 8), pl.ds(hp * 128, 128)] = ytiles[d, b, hp]
        return carry

    lax.fori_loop(0, c // 8, group, 0)


def _rw_scan(r, v, kk, w0, k0, b0, w1, k1, b1, *, nb, t_all, n_ctx):
    c = RW_CHUNK
    nc, ncc = t_all // c, n_ctx // c
    hw = RW_HEADS * RW_HEAD
    shp = lambda a: a.reshape(nb, t_all, hw)
    fwd = lambda j: (0, j, 0)
    bwd = lambda j: (0, jnp.where(j < ncc, ncc - 1 - j, nc - 1 - (j - ncc)), 0)
    blk = (nb, c, hw)
    yshape = jax.ShapeDtypeStruct((nb, t_all, hw), F32)
    yf, yb = pl.pallas_call(
        functools.partial(_rw_scan_kernel, nb=nb),
        grid=(nc,),
        in_specs=[pl.BlockSpec(blk, fwd)] * 6 + [pl.BlockSpec(blk, bwd)] * 6,
        out_specs=[pl.BlockSpec(blk, fwd), pl.BlockSpec(blk, bwd)],
        out_shape=[yshape, yshape],
        scratch_shapes=[pltpu.VMEM((2 * nb * 2, RW_HEAD, 128), F32)],
        compiler_params=_cparams(("arbitrary",)),
        name="rw_scan",
    )(shp(r), shp(v), shp(kk), shp(w0), shp(k0), shp(b0),
      shp(r), shp(v), shp(kk), shp(w1), shp(k1), shp(b1))

    return yf.reshape(nb * t_all, hw), yb.reshape(nb * t_all, hw)


def _rw_post_kernel(yf_ref, yb_ref, bonus_ref, gate_ref, vec_ref, o_ref):
    e4 = _block_ones(RW_HEADS * RW_HEAD, RW_HEAD)
    y = yf_ref[...] + yb_ref[...]
    mean = _split_dot(y, e4) * (1.0 / RW_HEAD)
    yc = y - mean
    var = _split_dot(yc * yc, e4) * (1.0 / RW_HEAD)
    yn = yc * lax.rsqrt(var + RW_LN_EPS) * vec_ref[0:1, :] + vec_ref[1:2, :]
    o_ref[...] = (yn + bonus_ref[...]) * gate_ref[...]


def _rw_post(yf, yb, bonus, gate, vecs, *, nb, tpb_y, r0_y, nt):
    hw = RW_HEADS * RW_HEAD
    n_out = bonus.shape[0]
    ytok = lambda b, r: (b * tpb_y + r0_y + r, 0)
    tok = lambda b, r: (b * nt + r, 0)
    return pl.pallas_call(
        _rw_post_kernel,
        grid=(nb, nt),
        in_specs=[pl.BlockSpec((TM, hw), ytok), pl.BlockSpec((TM, hw), ytok),
                  pl.BlockSpec((TM, hw), tok), pl.BlockSpec((TM, hw), tok),
                  pl.BlockSpec((8, hw), lambda b, r: (0, 0))],
        out_specs=pl.BlockSpec((TM, hw), tok),
        out_shape=jax.ShapeDtypeStruct((n_out, hw), F32),
        compiler_params=_cparams(("parallel", "parallel")),
        name="rw_post",
    )(yf, yb, bonus, gate, vecs)


def _s5_scan_kernel(uf_ref, ub_ref, wb_ref, ab_ref, cf_ref, cb_ref, yf_ref, yb_ref, x_ref, st_ref, *, nb):
    c = S5_CHUNK
    nst = S5_GROUPS * S5_STATE

    @pl.when(pl.program_id(0) == 0)
    def _():
        st_ref[...] = jnp.zeros(st_ref.shape, F32)

    dirs = ((uf_ref, cf_ref, yf_ref), (ub_ref, cb_ref, yb_ref))
    for d, (u_ref, _, _) in enumerate(dirs):
        for b in range(nb):
            x_ref[d, b] = _dot(u_ref[b].astype(BF16), wb_ref[:, d * 2 * nst:(d + 1) * 2 * nst])

    def group(g, carry):
        for d in range(2):
            base = pl.multiple_of((g if d == 0 else c // 8 - 1 - g) * 8, 8)
            ar = ab_ref[d, :, 0:nst]
            ai = ab_ref[d, :, nst:2 * nst]
            for b in range(nb):
                u = d * nb + b
                xr = st_ref[u, :, 0:nst]
                xi = st_ref[u, :, nst:2 * nst]
                bur = x_ref[d, b, pl.ds(base, 8), 0:nst]
                bui = x_ref[d, b, pl.ds(base, 8), nst:2 * nst]
                rows_r, rows_i = [None] * 8, [None] * 8
                for jj in range(8):
                    j = jj if d == 0 else 7 - jj
                    xr, xi = (ar * xr - ai * xi + bur[j:j + 1, :], ar * xi + ai * xr + bui[j:j + 1, :])
                    rows_r[j], rows_i[j] = xr, xi
                st_ref[u, :, 0:nst] = xr
                st_ref[u, :, nst:2 * nst] = xi
                x_ref[d, b, pl.ds(base, 8), 0:nst] = jnp.concatenate(rows_r, axis=0)
                x_ref[d, b, pl.ds(base, 8), nst:2 * nst] = jnp.concatenate(rows_i, axis=0)
        return carry

    lax.fori_loop(0, c // 8, group, 0)

    for d, (_, c_ref, y_ref) in enumerate(dirs):
        for b in range(nb):
            y_ref[b] = _dot(x_ref[d, b].astype(BF16), c_ref[...])


def _s5_scan(p, w_b, ab, cf, cb, *, nb, t_all, n_ctx):
    c = S5_CHUNK
    nc, ncc = t_all // c, n_ctx // c
    nst2 = 2 * S5_GROUPS * S5_STATE
    width = S5_GROUPS * S5_GROUP_CH
    p3 = p.reshape(nb, t_all, N_EXT)
    fwd = lambda j: j
    bwd = lambda j: jnp.where(j < ncc, ncc - 1 - j, nc - 1 - (j - ncc))
    yshape = jax.ShapeDtypeStruct((nb, t_all, width), F32)
    full = lambda j: (0, 0)
    yf, yb = pl.pallas_call(
        functools.partial(_s5_scan_kernel, nb=nb),
        grid=(nc,),
        in_specs=[pl.BlockSpec((nb, c, width), lambda j: (0, fwd(j), EXT_S5 // width)),
                  pl.BlockSpec((nb, c, width), lambda j: (0, bwd(j), EXT_S5 // width)),
                  pl.BlockSpec((width, 2 * nst2), full),
                  pl.BlockSpec((2, 1, nst2), lambda j: (0, 0, 0)),
                  pl.BlockSpec((nst2, width), full), pl.BlockSpec((nst2, width), full)],
        out_specs=[pl.BlockSpec((nb, c, width), lambda j: (0, fwd(j), 0)),
                   pl.BlockSpec((nb, c, width), lambda j: (0, bwd(j), 0))],
        out_shape=[yshape, yshape],
        scratch_shapes=[pltpu.VMEM((2, nb, c, nst2), F32), pltpu.VMEM((2 * nb, 1, nst2), F32)],
        compiler_params=_cparams(("arbitrary",)),
        name="s5_scan",
    )(p3, p3, w_b, ab, cf, cb)
    return yf.reshape(nb * t_all, width), yb.reshape(nb * t_all, width)


def _s5_post_kernel(yf_ref, yb_ref, u_ref, d_ref, gw_ref, gb_ref, o_ref):
    y = yf_ref[...] + yb_ref[...] + d_ref[...] * u_ref[...]
    zg = 0.5 * y * (1.0 + jnp.tanh(math.sqrt(2.0 / math.pi) * (y + 0.044715 * (y * y * y))))
    o_ref[...] = zg * _sigmoid(_dot(zg.astype(BF16), gw_ref[...]) + gb_ref[...])


def _s5_post(yf, yb, p, d_row, glu_w, glu_b, *, nb, tpb, r0, nt):
    tok = lambda b, r: (b * tpb + r0 + r, 0)
    full = lambda b, r: (0, 0)
    return pl.pallas_call(
        _s5_post_kernel,
        grid=(nb, nt),
        in_specs=[pl.BlockSpec((TM, 256), tok), pl.BlockSpec((TM, 256), tok),
                  pl.BlockSpec((TM, 256), lambda b, r: (b * tpb + r0 + r, EXT_S5 // 256)),
                  pl.BlockSpec((1, 256), full), pl.BlockSpec((256, 256), full),
                  pl.BlockSpec((1, 256), full)],
        out_specs=pl.BlockSpec((TM, 256), lambda b, r: (b * nt + r, 0)),
        out_shape=jax.ShapeDtypeStruct((nb * nt * TM, 256), F32),
        compiler_params=_cparams(("parallel", "parallel")),
        name="s5_post",
    )(yf, yb, p, d_row, glu_w, glu_b)


def _merge_kernel(ya_ref, yb_ref, ys_ref, yd_ref, g0, g1, g2, g3, h_ref, m2_ref, wb_ref, wo_ref, o_ref):
    acc = None
    for n, (y_ref, g_ref) in enumerate(((ya_ref, g0), (yb_ref, g1), (ys_ref, g2), (yd_ref, g3))):
        term = _sigmoid(g_ref[...]) * _dot(y_ref[...].astype(BF16), wb_ref[n])
        acc = term if acc is None else acc + term
    o_ref[...] = h_ref[...] + m2_ref[0] * _dot(acc.astype(BF16), wo_ref[...])


def _merge(ya, yb, ys, yd, p, h, mod2, w_branch, w_out, *, nb, tpb, r0, nt):
    tok_in = lambda b, r: (b * tpb + r0 + r, 0)
    tok_out = lambda b, r: (b * nt + r, 0)
    full2 = lambda b, r: (0, 0)
    gate = [pl.BlockSpec((TM, D_MODEL), lambda b, r, c=c: (b * tpb + r0 + r, EXT_GATE // D_MODEL + c))
            for c in range(4)]
    return pl.pallas_call(
        _merge_kernel,
        grid=(nb, nt),
        in_specs=[pl.BlockSpec((TM, 256), tok_out)] * 4 + gate
                 + [pl.BlockSpec((TM, D_MODEL), tok_in),
                    pl.BlockSpec((1, 1, D_MODEL), lambda b, r: (jnp.where(r0 + r == 0, nb, b), 0, 0)),
                    pl.BlockSpec((4, 256, D_MODEL), lambda b, r: (0, 0, 0)),
                    pl.BlockSpec((D_MODEL, D_MODEL), full2)],
        out_specs=pl.BlockSpec((TM, D_MODEL), tok_out),
        out_shape=jax.ShapeDtypeStruct((nb * nt * TM, D_MODEL), F32),
        compiler_params=_cparams(("parallel", "parallel")),
        name="merge",
    )(ya, yb, ys, yd, p, p, p, p, h, mod2, w_branch, w_out)


def _router_kernel(h_ref, g_ref, sh_ref, sc_ref, wh_ref, wm_ref, b_ref, x_ref, lg_ref):
    x = _rms(h_ref[...], g_ref[...], NORM_EPS) * (1.0 + sc_ref[0]) + sh_ref[0]
    xh = x.astype(BF16)
    xm = (x - xh.astype(F32)).astype(BF16)
    bits = lax.bitcast_convert_type(xh.astype(F32), jnp.uint32)
    half = D_MODEL // 2
    x_ref[...] = (bits[:, :half] >> 16) | (bits[:, half:] & jnp.uint32(0xFFFF0000))
    lg = (_dot(xh, wh_ref[...]) + _dot(xm, wh_ref[...]) + _dot(xh, wm_ref[...])) + b_ref[...]

    lane = lax.broadcasted_iota(jnp.int32, lg.shape, 1)
    lanef = lane.astype(F32)
    neg = jnp.float32(-jnp.inf)
    big = jnp.float32(1e9)
    rmax = lambda v: jnp.max(v, axis=-1, keepdims=True)
    rmin = lambda v: jnp.min(v, axis=-1, keepdims=True)
    rsum = lambda v: jnp.sum(v, axis=-1, keepdims=True)

    gmask = lane < MOE_GROUPS
    mg = rmax(jnp.where(gmask, lg, neg))
    eg = jnp.where(gmask, jnp.exp(lg - mg), 0.0)
    pg = eg / rsum(eg)
    pg_top = rmax(pg)
    g_sel = rmin(jnp.where(jnp.logical_and(gmask, pg == pg_top), lanef, big))
    lo = MOE_GROUPS + MOE_PER_GROUP * g_sel
    emask = jnp.logical_and(lanef >= lo, lanef < lo + MOE_PER_GROUP)
    me = rmax(jnp.where(emask, lg, neg))
    ee = jnp.where(emask, jnp.exp(lg - me), 0.0)
    pe = jnp.where(emask, ee / rsum(ee), -1.0)
    p1 = rmax(pe)
    i1 = rmin(jnp.where(pe == p1, lanef, big))
    pe2 = jnp.where(lanef == i1, -1.0, pe)
    p2 = rmax(pe2)
    i2 = rmin(jnp.where(pe2 == p2, lanef, big))
    den = p1 + p2
    out = jnp.where(lane == 0, i1 - MOE_GROUPS, 0.0)
    out = jnp.where(lane == 1, i2 - MOE_GROUPS, out)
    out = jnp.where(lane == 2, pg_top * p1 / den, out)
    out = jnp.where(lane == 3, pg_top * p2 / den, out)
    lg_ref[...] = out


def _router(h, g, shift, scale, wh, wm, bias, *, nb, nt, ctx_first):
    tok = lambda b, r: (b * nt + r, 0)
    full = lambda b, r: (0, 0)
    if ctx_first:
        modmap = lambda b, r: (jnp.where(r == 0, nb, b), 0, 0)
    else:
        modmap = lambda b, r: (b, 0, 0)
    n = h.shape[0]
    return pl.pallas_call(
        _router_kernel,
        grid=(nb, nt),
        in_specs=[pl.BlockSpec((TM, D_MODEL), tok), pl.BlockSpec((1, D_MODEL), full),
                  pl.BlockSpec((1, 1, D_MODEL), modmap), pl.BlockSpec((1, 1, D_MODEL), modmap),
                  pl.BlockSpec((D_MODEL, 128), full), pl.BlockSpec((D_MODEL, 128), full),
                  pl.BlockSpec((1, 128), full)],
        out_specs=[pl.BlockSpec((TM, D_MODEL // 2), tok), pl.BlockSpec((TM, 128), tok)],
        out_shape=[jax.ShapeDtypeStruct((n, D_MODEL // 2), jnp.uint32), jax.ShapeDtypeStruct((n, 128), F32)],
        compiler_params=_cparams(("parallel", "parallel")),
        name="router",
    )(h, g, shift, scale, wh, wm, bias)


def _expert_kernel(be_ref, nv_ref, x_ref, wg_ref, wu_ref, wd_ref, o_ref, wgb_ref, wub_ref, wdb_ref):
    i = pl.program_id(0)

    @pl.when(jnp.logical_or(i == 0, be_ref[i] != be_ref[jnp.maximum(i - 1, 0)]))
    def _():
        wgb_ref[...] = wg_ref[0].astype(BF16)
        wub_ref[...] = wu_ref[0].astype(BF16)
        wdb_ref[...] = wd_ref[0].astype(BF16)

    @pl.when(i < nv_ref[0])
    def _():
        u = x_ref[...]
        x = jnp.concatenate([lax.bitcast_convert_type(u << 16, F32),
                             lax.bitcast_convert_type(u & jnp.uint32(0xFFFF0000), F32)], axis=1).astype(BF16)
        hb = _silu(_dot(x, wgb_ref[...])) * _dot(x, wub_ref[...])
        o_ref[...] = _dot(hb.astype(BF16), wdb_ref[...])

    @pl.when(i >= nv_ref[0])
    def _():
        o_ref[...] = jnp.zeros(o_ref.shape, F32)


def _experts(xs, block_e, n_valid, w_gate, w_up, w_down):
    n_slots = xs.shape[0]
    n_blocks = n_slots // MOE_BLK
    wmap = lambda i, be, nv: (be[i], 0, 0)
    return pl.pallas_call(
        _expert_kernel,
        grid_spec=pltpu.PrefetchScalarGridSpec(
            num_scalar_prefetch=2,
            grid=(n_blocks,),
            in_specs=[pl.BlockSpec((MOE_BLK, D_MODEL // 2), lambda i, be, nv: (i, 0)),
                      pl.BlockSpec((1, D_MODEL, D_EXPERT), wmap),
                      pl.BlockSpec((1, D_MODEL, D_EXPERT), wmap),
                      pl.BlockSpec((1, D_EXPERT, D_MODEL), wmap)],
            out_specs=pl.BlockSpec((MOE_BLK, D_MODEL), lambda i, be, nv: (i, 0)),
            scratch_shapes=[pltpu.VMEM((D_MODEL, D_EXPERT), BF16), pltpu.VMEM((D_MODEL, D_EXPERT), BF16),
                            pltpu.VMEM((D_EXPERT, D_MODEL), BF16)]),
        out_shape=jax.ShapeDtypeStruct((n_slots, D_MODEL), F32),
        compiler_params=_cparams(("arbitrary",)),
        name="experts",
    )(block_e, n_valid, xs, w_gate, w_up, w_down)


def _combine_kernel(h_ref, y0_ref, y1_ref, w_ref, m5_ref, g_ref, o_ref, *, final):
    w = w_ref[...]
    y = y0_ref[...] * w[:, 0:1] + y1_ref[...] * w[:, 1:2]
    h = h_ref[...] + m5_ref[0] * y
    if final:
        h = _rms(h, g_ref[...], NORM_EPS)
    o_ref[...] = h


def _combine(h, y0, y1, wts, mod5, g_final, *, nb, nt, ctx_first, final):
    tok = lambda b, r: (b * nt + r, 0)
    if ctx_first:
        modmap = lambda b, r: (jnp.where(r == 0, nb, b), 0, 0)
    else:
        modmap = lambda b, r: (b, 0, 0)
    return pl.pallas_call(
        functools.partial(_combine_kernel, final=final),
        grid=(nb, nt),
        in_specs=[pl.BlockSpec((TM, D_MODEL), tok)] * 3
                 + [pl.BlockSpec((TM, 128), tok), pl.BlockSpec((1, 1, D_MODEL), modmap),
                    pl.BlockSpec((1, D_MODEL), lambda b, r: (0, 0))],
        out_specs=pl.BlockSpec((TM, D_MODEL), tok),
        out_shape=jax.ShapeDtypeStruct(h.shape, F32),
        compiler_params=_cparams(("parallel", "parallel")),
        name="combine",
    )(h, y0, y1, wts, mod5, g_final)


def _moe(h, g2, shift, scale, mod5, wh, wm, rbias, w_gate, w_up, w_down, g_final, *, layer, nb, nt, ctx_first,
         final):
    n = h.shape[0]
    x_bf, route = _router(h, g2, shift, scale, wh, wm, rbias, nb=nb, nt=nt, ctx_first=ctx_first)
    idx = route[:, :MOE_TOPK].astype(jnp.int32)
    wts = route[:, MOE_TOPK:2 * MOE_TOPK]
    n_as = n * MOE_TOPK
    flat_e = idx.reshape(n_as)
    onehot = (flat_e[:, None] == jnp.arange(MOE_EXPERTS, dtype=jnp.int32)[None, :]).astype(jnp.int32)
    csum = jnp.cumsum(onehot, axis=0)
    counts = csum[-1]
    rank = jnp.sum(jnp.where(onehot > 0, csum - 1, 0), axis=1)
    padded = (counts + MOE_BLK - 1) // MOE_BLK * MOE_BLK
    pad_end = jnp.cumsum(padded)
    pad_start = pad_end - padded
    slot = pad_start[flat_e] + rank
    n_blocks = (n_as + MOE_EXPERTS * (MOE_BLK - 1) + MOE_BLK - 1) // MOE_BLK
    n_slots = n_blocks * MOE_BLK
    slot_tok = jnp.zeros((n_slots,), jnp.int32).at[slot].set(jnp.arange(n_as, dtype=jnp.int32) // MOE_TOPK,
                                                             unique_indices=True)
    starts = jnp.arange(n_blocks, dtype=jnp.int32) * MOE_BLK
    block_e = jnp.minimum(jnp.sum((pad_end[None, :] <= starts[:, None]).astype(jnp.int32), axis=1),
                          MOE_EXPERTS - 1)
    n_valid = (pad_end[-1:] // MOE_BLK).astype(jnp.int32)
    xs = jnp.take(x_bf, slot_tok, axis=0, mode="clip")
    ys = _experts(xs, block_e + layer * MOE_EXPERTS, n_valid, w_gate, w_up, w_down)
    slot2 = slot.reshape(n, MOE_TOPK)
    y0 = jnp.take(ys, slot2[:, 0], axis=0, mode="clip")
    y1 = jnp.take(ys, slot2[:, 1], axis=0, mode="clip")
    wts_pad = jnp.pad(wts.astype(F32), ((0, 0), (0, 128 - MOE_TOPK)))
    return _combine(h, y0, y1, wts_pad, mod5, g_final, nb=nb, nt=nt, ctx_first=ctx_first, final=final)


_ROT_SRC = np.array(list(range(8, 16)) + list(range(0, 8)) + list(range(24, 32)) + list(range(16, 24)))
_ROT_SIGN = np.array([-1.0] * 8 + [1.0] * 8 + [-1.0] * 8 + [1.0] * 8, np.float32)


def _rot_cols(w):
    k = w.shape[-1] // ROPE_DIM
    src = np.concatenate([_ROT_SRC + ROPE_DIM * i for i in range(k)])
    sign = np.tile(_ROT_SIGN, k)
    return w[..., src] * sign


def _rope_tables(n_ctx, n_lat):
    rows = n_lat // GRID_W
    row = jnp.repeat(jnp.arange(rows, dtype=F32), GRID_W)
    col = (jnp.arange(rows * GRID_W) % GRID_W).astype(F32)
    nf = ROPE_DIM // 4
    inv = ROPE_BASE ** (-jnp.arange(nf, dtype=F32) / nf)
    ar = row[:, None] * inv
    ac = col[:, None] * inv
    ang = jnp.concatenate([ar, ar, ac, ac], axis=-1)
    cos = jnp.concatenate([jnp.ones((n_ctx, ROPE_DIM), F32), jnp.cos(ang)], axis=0)
    sin = jnp.concatenate([jnp.zeros((n_ctx, ROPE_DIM), F32), jnp.sin(ang)], axis=0)
    return cos, sin


def _block_diag(blocks):
    g, a, b = blocks.shape
    tiled = jnp.tile(blocks.reshape(g * a, b), (1, g))
    rows = lax.broadcasted_iota(jnp.int32, (g * a, g * b), 0) // a
    cols = lax.broadcasted_iota(jnp.int32, (g * a, g * b), 1) // b
    return jnp.where(rows == cols, tiled, 0.0)


def _pick_tk(t_all):
    best = 128
    for tk in range(128, ATTN_TK_MAX + 1, 128):
        if t_all % tk == 0:
            best = tk
    return best


def kernel(x, c, ctx, c_ctx, w_mod, b_mod, norm1_g, norm2_g, w_in, mla_q_norm_g, mla_kv_norm_g, mla_w_uq, mla_w_ukv, rw_mu, rw_w0, rw_w2, rw_a0, rw_a2, rw_g2, rw_k_k, rw_k_a, rw_r_k, rw_lnx_g, rw_lnx_b, s5_a_re, s5_a_im, s5_log_dt, s5_b_re, s5_b_im, s5_c_re, s5_c_im, s5_d, s5_glu_w, s5_glu_b, diff_lq1, diff_lk1, diff_lq2, diff_lk2, diff_subln_g, w_branch, w_out, router_g_w, router_g_b, router_e_w, router_e_b, exp_w_gate, exp_w_up, exp_w_down, final_norm_g):
    nb, n_lat, d = x.shape
    n_ctx = ctx.shape[1]
    depth = w_mod.shape[0]
    t_all = n_ctx + n_lat
    assert d == D_MODEL and n_ctx == TM and n_lat % TM == 0
    tpb = t_all // TM
    tk = _pick_tk(t_all)
    hw = RW_HEADS * RW_HEAD

    cos, sin = _rope_tables(n_ctx, n_lat)
    mla_scale = (MLA_NOPE + MLA_ROPE) ** -0.5 * LOG2E
    z32 = jnp.zeros((t_all, 32), F32)
    cq_tab = jnp.concatenate([jnp.ones((t_all, 64), F32), cos, z32], axis=1) * mla_scale
    sq_tab = jnp.concatenate([jnp.zeros((t_all, 64), F32), sin, z32], axis=1) * mla_scale
    ck_tab = jnp.concatenate([cos, sin, jnp.zeros((t_all, 64), F32)], axis=1)
    dcos = jnp.tile(cos, (1, 8))
    dsin = jnp.tile(sin, (1, 8))

    c_rows = jnp.concatenate([c, c_ctx[None, :], jnp.zeros((8 - nb - 1, d), F32)], axis=0)

    h = jnp.concatenate([ctx, x], axis=1).reshape(nb * t_all, d)

    for l in range(depth):
        last = l == depth - 1
        r0, nt = (1, tpb - 1) if last else (0, tpb)

        mod = _mm(c_rows, w_mod[l].astype(BF16), b_mod[l][None, :], tm=8, tn=1536, pre_silu=True, name="mod")
        mods = [mod[:nb + 1, i * d:(i + 1) * d].reshape(nb + 1, 1, d) for i in range(6)]

        wi = w_in[l]
        o_rw, o_s5, o_df, o_gt = 416, 1440, 1696, 2464
        w_kr = wi[:, 384:416]
        w_dq, w_dk, w_dv = wi[:, o_df:o_df + 256], wi[:, o_df + 256:o_df + 512], wi[:, o_df + 512:o_df + 768]
        w_ext = jnp.concatenate(
            [wi[:, o_rw:o_s5],
             wi[:, :416], _rot_cols(w_kr), jnp.zeros((d, 64), F32),
             wi[:, o_s5:o_df],
             w_dq, w_dk, w_dv, _rot_cols(w_dq), _rot_cols(w_dk),
             wi[:, o_gt:]], axis=1).astype(BF16)
        p = _inproj(h, norm1_g[l][None, :], mods[0], mods[1], w_ext, nb=nb, tpb=tpb)

        wq = mla_w_uq[l].reshape(MLA_Q_LORA, MLA_HEADS, MLA_NOPE + MLA_ROPE)
        zq = jnp.zeros((MLA_Q_LORA, MLA_HEADS, 32), F32)
        wa = jnp.concatenate([wq, zq], axis=2).reshape(MLA_Q_LORA, 512).astype(BF16)
        wb = jnp.concatenate([jnp.zeros((MLA_Q_LORA, MLA_HEADS, 64), F32), _rot_cols(wq[:, :, MLA_NOPE:]), zq],
                             axis=2).reshape(MLA_Q_LORA, 512).astype(BF16)
        wkv = mla_w_ukv[l].reshape(MLA_KV_LORA, MLA_HEADS, MLA_NOPE + MLA_V)
        wk = jnp.concatenate([wkv[:, :, :MLA_NOPE], jnp.zeros((MLA_KV_LORA, MLA_HEADS, 64), F32)],
                             axis=2).reshape(MLA_KV_LORA, 512).astype(BF16)
        wv = wkv[:, :, MLA_NOPE:].reshape(MLA_KV_LORA, MLA_HEADS * MLA_V).astype(BF16)
        pk_np = np.zeros((128, 512), np.float32)
        for hh in range(MLA_HEADS):
            for i in range(32):
                pk_np[i, hh * 128 + 64 + i] = 1.0
                pk_np[32 + i, hh * 128 + 64 + i] = 1.0
        pk = jnp.asarray(pk_np, BF16)
        q_m, k_m, v_m = _mla_prep(p, mla_q_norm_g[l][None, :], mla_kv_norm_g[l][None, :], wa, wb, wk, wv, pk,
                                  cq_tab, sq_tab, ck_tab, nb=nb, tpb=tpb)
        ya_lat = _flash_mla(q_m, k_m, v_m, nb=nb, t_all=t_all, q_start=n_ctx, q_len=n_lat, k_len=t_all, tk=tk)

        q_d, k_d, v_d = _diff_prep(p, dcos, dsin, nb=nb, tpb=tpb)
        lam_init = 0.8 - 0.6 * math.exp(-0.3 * l)
        lam = (jnp.exp(jnp.sum(diff_lq1[l] * diff_lk1[l])) - jnp.exp(jnp.sum(diff_lq2[l] * diff_lk2[l])) + lam_init)
        lam_row = jnp.full((1, 256), lam, F32)
        g_row = jnp.tile(diff_subln_g[l], DIFF_HEADS)[None, :]
        yd_lat = _flash_diff(q_d, k_d, v_d, lam_row, g_row, lam_init=lam_init, nb=nb, t_all=t_all,
                             q_start=n_ctx, q_len=n_lat, k_len=t_all, tk=tk)
        if last:
            ya, yd = ya_lat, yd_lat
        else:
            ya_ctx = _flash_mla(q_m, k_m, v_m, nb=nb, t_all=t_all, q_start=0, q_len=n_ctx, k_len=n_ctx, tk=n_ctx)
            yd_ctx = _flash_diff(q_d, k_d, v_d, lam_row, g_row, lam_init=lam_init, nb=nb, t_all=t_all,
                                 q_start=0, q_len=n_ctx, k_len=n_ctx, tk=n_ctx)
            comb = lambda a_c, a_l: jnp.concatenate(
                [a_c.reshape(nb, n_ctx, -1), a_l.reshape(nb, n_lat, -1)], axis=1).reshape(nb * t_all, -1)
            ya, yd = comb(ya_ctx, ya_lat), comb(yd_ctx, yd_lat)

        wlo = jnp.zeros((128, 4 * hw), F32)
        wlo = wlo.at[:64, 0:hw].set(rw_w2[l, 0]).at[:64, hw:2 * hw].set(rw_w2[l, 1])
        wlo = wlo.at[64:, 2 * hw:3 * hw].set(rw_a2[l, 0]).at[64:, 3 * hw:].set(rw_a2[l, 1])
        vecs = jnp.stack([rw_k_k[l], rw_k_a[l], rw_r_k[l].reshape(hw), rw_w0[l, 0], rw_w0[l, 1],
                          rw_a0[l, 0], rw_a0[l, 1], jnp.zeros((hw,), F32)], axis=0)
        (r_, v_, kk_, w0_, k0_, b0_, w1_, k1_, b1_, bonus, gate_rw) = _rw_prep(
            p, rw_mu[l][None, :], wlo.astype(BF16), rw_g2[l].astype(BF16), vecs, nb=nb, tpb=tpb, r0=0, nt=tpb)
        yf, yb_ = _rw_scan(r_, v_, kk_, w0_, k0_, b0_, w1_, k1_, b1_, nb=nb, t_all=t_all, n_ctx=n_ctx)
        ln_vecs = jnp.concatenate([rw_lnx_g[l][None, :], rw_lnx_b[l][None, :], jnp.zeros((6, hw), F32)], axis=0)
        if last:
            trim = lambda a: a.reshape(nb, t_all, hw)[:, n_ctx:].reshape(nb * n_lat, hw)
            bonus, gate_rw = trim(bonus), trim(gate_rw)
        y_rw = _rw_post(yf, yb_, bonus, gate_rw, ln_vecs, nb=nb, tpb_y=tpb, r0_y=r0, nt=nt)

        bbs, abs_, cfs = [], [], []
        for dd in range(2):
            lr, li = s5_a_re[l, dd], s5_a_im[l, dd]
            dt = jnp.exp(s5_log_dt[l, dd])[:, None]
            mag = jnp.exp(lr * dt)
            ab_re, ab_im = mag * jnp.cos(li * dt), mag * jnp.sin(li * dt)
            den = lr * lr + li * li
            nr, ni = ab_re - 1.0, ab_im
            cf_re = (nr * lr + ni * li) / den
            cf_im = (ni * lr - nr * li) / den
            bre, bim = s5_b_re[l, dd], s5_b_im[l, dd]
            bb_re = cf_re[..., None] * bre - cf_im[..., None] * bim
            bb_im = cf_re[..., None] * bim + cf_im[..., None] * bre
            bbs.append(jnp.concatenate([_block_diag(bb_re.transpose(0, 2, 1)),
                                        _block_diag(bb_im.transpose(0, 2, 1))], axis=1))
            abs_.append(jnp.concatenate([ab_re.reshape(-1), ab_im.reshape(-1)])[None, :])
            cfs.append(jnp.concatenate([_block_diag(s5_c_re[l, dd].transpose(0, 2, 1)),
                                        -_block_diag(s5_c_im[l, dd].transpose(0, 2, 1))], axis=0))
        yf_s5, yb_s5 = _s5_scan(p, jnp.concatenate(bbs, axis=1).astype(BF16), jnp.stack(abs_, axis=0),
                                cfs[0].astype(BF16), cfs[1].astype(BF16), nb=nb, t_all=t_all, n_ctx=n_ctx)
        y_s5 = _s5_post(yf_s5, yb_s5, p, s5_d[l].reshape(1, 256), s5_glu_w[l].astype(BF16),
                        s5_glu_b[l][None, :], nb=nb, tpb=tpb, r0=r0, nt=nt)

        h = _merge(ya, y_rw, y_s5, yd, p, h, mods[2], w_branch[l].astype(BF16), w_out[l].astype(BF16),
                   nb=nb, tpb=tpb, r0=r0, nt=nt)

        wr = jnp.concatenate([router_g_w[l], router_e_w[l], jnp.zeros((d, 128 - MOE_GROUPS - MOE_EXPERTS), F32)], axis=1)
        wr_h = wr.astype(BF16)
        wr_m = (wr - wr_h.astype(F32)).astype(BF16)
        rbias = jnp.concatenate([router_g_b[l], router_e_b[l],
                                 jnp.zeros((128 - MOE_GROUPS - MOE_EXPERTS,), F32)])[None, :]
        h = _moe(h, norm2_g[l][None, :], mods[3], mods[4], mods[5], wr_h, wr_m, rbias,
                 exp_w_gate.reshape(depth * MOE_EXPERTS, d, D_EXPERT), exp_w_up.reshape(depth * MOE_EXPERTS, d, D_EXPERT),
                 exp_w_down.reshape(depth * MOE_EXPERTS, D_EXPERT, d),
                 final_norm_g[None, :], layer=l, nb=nb, nt=nt, ctx_first=not last, final=last)

    return h.reshape(nb, n_lat, d)
```

```python
import functools
import math

import jax
import jax.numpy as jnp
import numpy as np
from jax import lax
from jax.experimental import pallas as pl
from jax.experimental.pallas import tpu as pltpu
from jax.experimental.pallas import tpu_sc as plsc

F32 = jnp.float32
BF16 = jnp.bfloat16

TM = 256
VMEM_LIMIT = 48 * 1024 * 1024

D_MODEL = 1024
GRID_W = 64
ROPE_DIM = 32
ROPE_BASE = 10000.0
NORM_EPS = 1e-6
MLA_HEADS, MLA_NOPE, MLA_ROPE, MLA_V = 4, 64, 32, 64
MLA_Q_LORA, MLA_KV_LORA = 256, 128
RW_HEADS, RW_HEAD = 4, 64
RW_LN_EPS = 64e-5
S5_GROUPS, S5_GROUP_CH, S5_STATE = 16, 16, 64
DIFF_HEADS, DIFF_HD = 4, 32
DIFF_EPS = 1e-5
MOE_GROUPS, MOE_PER_GROUP, MOE_TOPK = 4, 8, 2
MOE_EXPERTS = MOE_GROUPS * MOE_PER_GROUP
D_EXPERT = 512
MOE_BLK = 256
RW_CHUNK = 64
S5_CHUNK = 128
ATTN_TK_MAX = 2816
SC_INDEX_WINDOW = 128
SC_ROW_PIECES = 4
LOG2E = math.log2(math.e)

EXT_RW, EXT_MLA, EXT_S5, EXT_DIFF, EXT_GATE = 0, 1024, 1536, 1792, 3072
N_EXT = 7168


def _cparams(sem, vmem=VMEM_LIMIT):
    return pltpu.CompilerParams(dimension_semantics=sem, vmem_limit_bytes=vmem)


def _dot(a, b):
    return jnp.dot(a, b, preferred_element_type=F32)


def _dot_nt(a, b):
    return lax.dot_general(a, b, (((1,), (1,)), ((), ())), preferred_element_type=F32)


def _split_dot(x, e):
    hi = x.astype(BF16)
    mid = (x - hi.astype(F32)).astype(BF16)
    return _dot(hi, e) + _dot(mid, e)


def _block_ones(n, blk):
    r = lax.broadcasted_iota(jnp.int32, (n, n), 0) // blk
    c = lax.broadcasted_iota(jnp.int32, (n, n), 1) // blk
    return (r == c).astype(BF16)


def _sigmoid(x):
    return 1.0 / (1.0 + jnp.exp(-x))


def _silu(x):
    return x * _sigmoid(x)


def _softplus(x):
    return jnp.maximum(x, 0.0) + jnp.log(1.0 + jnp.exp(-jnp.abs(x)))


def _rms(x, g, eps):
    return x * lax.rsqrt(jnp.mean(x * x, axis=-1, keepdims=True) + eps) * g


def _mm_kernel(x_ref, w_ref, b_ref, o_ref, *, pre_silu):
    x = x_ref[...].astype(F32)
    if pre_silu:
        x = _silu(x)
    o_ref[...] = _dot(x.astype(BF16), w_ref[...]) + b_ref[...]


def _mm(x, w, b, *, tm, tn, pre_silu=False, name="mm"):
    m, k = x.shape
    n = w.shape[1]
    return pl.pallas_call(
        functools.partial(_mm_kernel, pre_silu=pre_silu),
        grid=(m // tm, n // tn),
        in_specs=[pl.BlockSpec((tm, k), lambda i, j: (i, 0)),
                  pl.BlockSpec((k, tn), lambda i, j: (0, j)),
                  pl.BlockSpec((1, tn), lambda i, j: (0, j))],
        out_specs=pl.BlockSpec((tm, tn), lambda i, j: (i, j)),
        out_shape=jax.ShapeDtypeStruct((m, n), F32),
        compiler_params=_cparams(("parallel", "arbitrary")),
        name=name,
    )(x, w, b)


def _inproj_kernel(h_ref, g_ref, sh_ref, sc_ref, w_ref, o_ref):
    x = _rms(h_ref[...], g_ref[...], NORM_EPS)
    xn = (x * (1.0 + sc_ref[0]) + sh_ref[0]).astype(BF16)
    o_ref[...] = _dot(xn, w_ref[...])


def _inproj(h, g, shift, scale, w_ext, *, nb, tpb):
    n = h.shape[0]
    tn = N_EXT // 2
    modmap = lambda j, b, r: (jnp.where(r == 0, nb, b), 0, 0)
    return pl.pallas_call(
        _inproj_kernel,
        grid=(N_EXT // tn, nb, tpb),
        in_specs=[pl.BlockSpec((TM, D_MODEL), lambda j, b, r: (b * tpb + r, 0)),
                  pl.BlockSpec((1, D_MODEL), lambda j, b, r: (0, 0)),
                  pl.BlockSpec((1, 1, D_MODEL), modmap),
                  pl.BlockSpec((1, 1, D_MODEL), modmap),
                  pl.BlockSpec((D_MODEL, tn), lambda j, b, r: (0, j))],
        out_specs=pl.BlockSpec((TM, tn), lambda j, b, r: (b * tpb + r, j)),
        out_shape=jax.ShapeDtypeStruct((n, N_EXT), F32),
        compiler_params=_cparams(("parallel", "parallel", "parallel")),
        name="inproj",
    )(h, g, shift, scale, w_ext)


def _mla_prep_kernel(p_ref, gq_ref, gkv_ref, wa_ref, wb_ref, wk_ref, wv_ref, pk_ref,
                     cq_ref, sq_ref, ck_ref, q_ref, k_ref, v_ref):
    seg = p_ref[...]
    nq = _rms(seg[:, :MLA_Q_LORA], gq_ref[...], NORM_EPS).astype(BF16)
    nkv = _rms(seg[:, MLA_Q_LORA:MLA_Q_LORA + MLA_KV_LORA], gkv_ref[...], NORM_EPS).astype(BF16)
    cq = jnp.concatenate([cq_ref[...]] * MLA_HEADS, axis=1)
    sq = jnp.concatenate([sq_ref[...]] * MLA_HEADS, axis=1)
    q = _dot(nq, wa_ref[...]) * cq + _dot(nq, wb_ref[...]) * sq
    q_ref[...] = q.astype(BF16)
    kr = (seg[:, 384:512] * ck_ref[...]).astype(BF16)
    k_ref[...] = (_dot(nkv, wk_ref[...]) + _dot(kr, pk_ref[...])).astype(BF16)
    v_ref[...] = _dot(nkv, wv_ref[...]).astype(BF16)


def _mla_prep(p, gq, gkv, wa, wb, wk, wv, pk, cq_tab, sq_tab, ck_tab, *, nb, tpb):
    n = p.shape[0]
    tok = lambda b, r: (b * tpb + r, 0)
    pos = lambda b, r: (r, 0)
    full = lambda b, r: (0, 0)
    return pl.pallas_call(
        _mla_prep_kernel,
        grid=(nb, tpb),
        in_specs=[pl.BlockSpec((TM, 512), lambda b, r: (b * tpb + r, EXT_MLA // 512)),
                  pl.BlockSpec((1, MLA_Q_LORA), full),
                  pl.BlockSpec((1, MLA_KV_LORA), full),
                  pl.BlockSpec((MLA_Q_LORA, 512), full),
                  pl.BlockSpec((MLA_Q_LORA, 512), full),
                  pl.BlockSpec((MLA_KV_LORA, 512), full),
                  pl.BlockSpec((MLA_KV_LORA, 256), full),
                  pl.BlockSpec((128, 512), full),
                  pl.BlockSpec((TM, 128), pos),
                  pl.BlockSpec((TM, 128), pos),
                  pl.BlockSpec((TM, 128), pos)],
        out_specs=[pl.BlockSpec((TM, 512), tok),
                   pl.BlockSpec((TM, 512), tok),
                   pl.BlockSpec((TM, 256), tok)],
        out_shape=[jax.ShapeDtypeStruct((n, 512), BF16),
                   jax.ShapeDtypeStruct((n, 512), BF16),
                   jax.ShapeDtypeStruct((n, 256), BF16)],
        compiler_params=_cparams(("parallel", "parallel")),
        name="mla_prep",
    )(p, gq, gkv, wa, wb, wk, wv, pk, cq_tab, sq_tab, ck_tab)


def _flash_mla_kernel(q_ref, k_ref, v_ref, o_ref, m_ref, l_ref, acc_ref):
    kk = pl.program_id(2)
    tq = q_ref.shape[0]

    @pl.when(kk == 0)
    def _():
        m_ref[...] = jnp.full(m_ref.shape, -jnp.inf, F32)
        l_ref[...] = jnp.zeros(l_ref.shape, F32)
        acc_ref[...] = jnp.zeros(acc_ref.shape, F32)

    ps, alphas = [], []
    for h in range(MLA_HEADS):
        s = _dot_nt(q_ref[:, h * 128:(h + 1) * 128], k_ref[:, h * 128:(h + 1) * 128])
        m_prev = m_ref[h]
        m_new = jnp.maximum(m_prev, jnp.max(s, axis=-1, keepdims=True))
        alpha = jnp.exp2(m_prev - m_new)
        p = jnp.exp2(s - m_new)
        l_ref[h] = alpha * l_ref[h] + jnp.sum(p, axis=-1, keepdims=True)
        m_ref[h] = m_new
        alphas.append(alpha)
        ps.append(p.astype(BF16))
    pv = _dot(jnp.concatenate(ps, axis=0), v_ref[...])
    acc_ref[...] = jnp.concatenate(alphas, axis=0) * acc_ref[...] + pv

    @pl.when(kk == pl.num_programs(2) - 1)
    def _():
        lane = lax.broadcasted_iota(jnp.int32, (tq, MLA_HEADS * MLA_V), 1)
        out = jnp.zeros((tq, MLA_HEADS * MLA_V), F32)
        for h in range(MLA_HEADS):
            out = jnp.where(lane // MLA_V == h, acc_ref[h * tq:(h + 1) * tq, :] / l_ref[h], out)
        o_ref[...] = out


def _flash_mla(q, k, v, *, nb, t_all, q_start, q_len, k_len, tk):
    tq = TM
    tpb_q, tpb_k = t_all // tq, t_all // tk
    q0 = q_start // tq
    nq = q_len // tq
    return pl.pallas_call(
        _flash_mla_kernel,
        grid=(nb, nq, k_len // tk),
        in_specs=[pl.BlockSpec((tq, 512), lambda b, i, kk: (b * tpb_q + q0 + i, 0)),
                  pl.BlockSpec((tk, 512), lambda b, i, kk: (b * tpb_k + kk, 0)),
                  pl.BlockSpec((tk, 256), lambda b, i, kk: (b * tpb_k + kk, 0))],
        out_specs=pl.BlockSpec((tq, 256), lambda b, i, kk: (b * nq + i, 0)),
        out_shape=jax.ShapeDtypeStruct((nb * q_len, 256), F32),
        scratch_shapes=[pltpu.VMEM((MLA_HEADS, tq, 1), F32), pltpu.VMEM((MLA_HEADS, tq, 1), F32),
                        pltpu.VMEM((MLA_HEADS * tq, 256), F32)],
        compiler_params=_cparams(("parallel", "parallel", "arbitrary")),
        name="flash_mla",
    )(q, k, v)


def _diff_prep(p, cos_tab, sin_tab, *, nb, tpb):
    n = p.shape[0]
    tok = lambda b, r: (b * tpb + r, 0)
    pos = lambda b, r: (r, 0)
    return pl.pallas_call(
        _diff_prep_kernel_cols,
        grid=(nb, tpb),
        in_specs=[pl.BlockSpec((TM, 256), lambda b, r, c=c: (b * tpb + r, EXT_DIFF // 256 + c))
                  for c in range(5)]
                 + [pl.BlockSpec((TM, 256), pos), pl.BlockSpec((TM, 256), pos)],
        out_specs=[pl.BlockSpec((TM, 256), tok)] * 3,
        out_shape=[jax.ShapeDtypeStruct((n, 256), BF16)] * 3,
        compiler_params=_cparams(("parallel", "parallel")),
        name="diff_prep",
    )(p, p, p, p, p, cos_tab, sin_tab)


def _diff_prep_kernel_cols(q_in, k_in, v_in, qr_in, kr_in, cos_ref, sin_ref, q_ref, k_ref, v_ref):
    cos, sin = cos_ref[...], sin_ref[...]
    scale = DIFF_HD ** -0.5 * LOG2E
    q_ref[...] = ((q_in[...] * cos + qr_in[...] * sin) * scale).astype(BF16)
    k_ref[...] = (k_in[...] * cos + kr_in[...] * sin).astype(BF16)
    v_ref[...] = v_in[...].astype(BF16)


def _flash_diff_kernel(q_ref, k_ref, v_ref, lam_ref, g_ref, o_ref, qs_ref, m_ref, l_ref, acc_ref, *, lam_init):
    kk = pl.program_id(2)
    tq = q_ref.shape[0]
    nsm = 2 * DIFF_HEADS

    @pl.when(kk == 0)
    def _():
        m_ref[...] = jnp.full(m_ref.shape, -jnp.inf, F32)
        l_ref[...] = jnp.zeros(l_ref.shape, F32)
        acc_ref[...] = jnp.zeros(acc_ref.shape, F32)
        q = q_ref[...]
        lane = lax.broadcasted_iota(jnp.int32, (tq, 256), 1)
        for i in range(nsm):
            qs_ref[i * tq:(i + 1) * tq, :] = jnp.where((lane // DIFF_HD) == i, q, jnp.zeros_like(q))

    k = k_ref[...]
    ps, alphas = [], []
    for i in range(nsm):
        rows = slice(i * tq, (i + 1) * tq)
        s = _dot_nt(qs_ref[rows, :], k)
        m_prev = m_ref[rows, :]
        m_new = jnp.maximum(m_prev, jnp.max(s, axis=-1, keepdims=True))
        alpha = jnp.exp2(m_prev - m_new)
        p = jnp.exp2(s - m_new)
        l_ref[rows, :] = alpha * l_ref[rows, :] + jnp.sum(p, axis=-1, keepdims=True)
        m_ref[rows, :] = m_new
        alphas.append(alpha)
        ps.append(p.astype(BF16))
    pv = _dot(jnp.concatenate(ps, axis=0), v_ref[...])
    acc_ref[...] = jnp.concatenate(alphas, axis=0) * acc_ref[...] + pv

    @pl.when(kk == pl.num_programs(2) - 1)
    def _():
        lane = lax.broadcasted_iota(jnp.int32, (tq, 256), 1)
        o = jnp.zeros((tq, 256), F32)
        for h in range(DIFF_HEADS):
            r0, r1 = (2 * h) * tq, (2 * h + 1) * tq
            o0 = acc_ref[r0:r0 + tq, :] / l_ref[r0:r0 + tq, :]
            o1 = acc_ref[r1:r1 + tq, :] / l_ref[r1:r1 + tq, :]
            o = jnp.where((lane // (2 * DIFF_HD)) == h, o0 - lam_ref[...] * o1, o)
        ms = _split_dot(o * o, _block_ones(256, 2 * DIFF_HD)) * (1.0 / (2 * DIFF_HD))
        o_ref[...] = o * lax.rsqrt(ms + DIFF_EPS) * g_ref[...] * (1.0 - lam_init)


def _flash_diff(q, k, v, lam_row, g_row, *, lam_init, nb, t_all, q_start, q_len, k_len, tk):
    tq = TM
    tpb_q, tpb_k = t_all // tq, t_all // tk
    q0 = q_start // tq
    nq = q_len // tq
    return pl.pallas_call(
        functools.partial(_flash_diff_kernel, lam_init=lam_init),
        grid=(nb, nq, k_len // tk),
        in_specs=[pl.BlockSpec((tq, 256), lambda b, i, kk: (b * tpb_q + q0 + i, 0)),
                  pl.BlockSpec((tk, 256), lambda b, i, kk: (b * tpb_k + kk, 0)),
                  pl.BlockSpec((tk, 256), lambda b, i, kk: (b * tpb_k + kk, 0)),
                  pl.BlockSpec((1, 256), lambda b, i, kk: (0, 0)),
                  pl.BlockSpec((1, 256), lambda b, i, kk: (0, 0))],
        out_specs=pl.BlockSpec((tq, 256), lambda b, i, kk: (b * nq + i, 0)),
        out_shape=jax.ShapeDtypeStruct((nb * q_len, 256), F32),
        scratch_shapes=[pltpu.VMEM((2 * DIFF_HEADS * tq, 256), BF16),
                        pltpu.VMEM((2 * DIFF_HEADS * tq, 1), F32),
                        pltpu.VMEM((2 * DIFF_HEADS * tq, 1), F32),
                        pltpu.VMEM((2 * DIFF_HEADS * tq, 256), F32)],
        compiler_params=_cparams(("parallel", "parallel", "arbitrary")),
        name="flash_diff",
    )(q, k, v, lam_row, g_row)


def _rw_prep_kernel(p_ref, prev_ref, next_ref, mu_ref, wlo_ref, g2_ref, vec_ref,
                    r_ref, v_ref, kk_ref, w0_ref, k0_ref, b0_ref, w1_ref, k1_ref, b1_ref,
                    bonus_ref, gate_ref, *, r0, lat_last):
    r = pl.program_id(1) + r0
    p = p_ref[...]
    row = lax.broadcasted_iota(jnp.int32, p.shape, 0)
    first_tile = jnp.logical_or(r == 0, r == 1)
    last_tile = jnp.logical_or(r == 0, r == lat_last)
    prev_row = jnp.where(first_tile, 0.0, prev_ref[7:8, :])
    next_row = jnp.where(last_tile, 0.0, next_ref[0:1, :])
    up = jnp.where(row == 0, prev_row, pltpu.roll(p, 1, 0))
    dn = jnp.where(row == TM - 1, next_row, pltpu.roll(p, TM - 1, 0))
    z = p + (0.5 * (up + dn) - p) * mu_ref[...]

    hw = RW_HEADS * RW_HEAD
    rr, k, v = z[:, :hw], z[:, hw:2 * hw], z[:, 2 * hw:3 * hw]
    lo = z[:, 3 * hw:3 * hw + 128]
    lane = lax.broadcasted_iota(jnp.int32, lo.shape, 1)
    lo = jnp.where(lane < 64, jnp.tanh(lo), lo).astype(BF16)
    wa = _dot(lo, wlo_ref[...])
    gate_ref[...] = _dot(_sigmoid(z[:, 3 * hw + 128:]).astype(BF16), g2_ref[...])

    e4 = _block_ones(hw, RW_HEAD)
    k_k, k_a, r_k = vec_ref[0:1, :], vec_ref[1:2, :], vec_ref[2:3, :]
    kk = k * k_k
    nrm = jnp.maximum(jnp.sqrt(_split_dot(kk * kk, e4)), 1e-12)
    kk = kk / nrm
    r_ref[...] = rr
    v_ref[...] = v
    kk_ref[...] = kk
    ksum = jnp.zeros_like(k)
    for d, (w_ref, kd_ref, b_ref) in enumerate(((w0_ref, k0_ref, b0_ref), (w1_ref, k1_ref, b1_ref))):
        w0 = vec_ref[3 + d:4 + d, :]
        a0 = vec_ref[5 + d:6 + d, :]
        wd = -_softplus(-(w0 + wa[:, d * hw:(d + 1) * hw])) - 0.5
        w_ref[...] = jnp.exp(-jnp.exp(wd))
        ad = _sigmoid(a0 + wa[:, (2 + d) * hw:(3 + d) * hw])
        kd = k * (1.0 + (ad - 1.0) * k_a)
        kd_ref[...] = kd
        b_ref[...] = kk * ad
        ksum = ksum + kd
    bonus_ref[...] = _split_dot(rr * ksum * r_k, e4) * v


def _rw_prep(p, mu, wlo, g2, vecs, *, nb, tpb, r0, nt):
    n_out = nb * nt * TM
    hw = RW_HEADS * RW_HEAD
    n_rows8 = p.shape[0] // 8
    tok = lambda b, r: (b * tpb + r0 + r, EXT_RW // 1024)
    prev = lambda b, r: (jnp.maximum((b * tpb + r0 + r) * (TM // 8) - 1, 0), EXT_RW // 1024)
    nxt = lambda b, r: (jnp.minimum((b * tpb + r0 + r + 1) * (TM // 8), n_rows8 - 1), EXT_RW // 1024)
    out = lambda b, r: (b * nt + r, 0)
    full = lambda b, r: (0, 0)
    return pl.pallas_call(
        functools.partial(_rw_prep_kernel, r0=r0, lat_last=tpb - 1),
        grid=(nb, nt),
        in_specs=[pl.BlockSpec((TM, 1024), tok),
                  pl.BlockSpec((8, 1024), prev),
                  pl.BlockSpec((8, 1024), nxt),
                  pl.BlockSpec((1, 1024), full),
                  pl.BlockSpec((128, 4 * hw), full),
                  pl.BlockSpec((128, hw), full),
                  pl.BlockSpec((8, hw), full)],
        out_specs=[pl.BlockSpec((TM, hw), out)] * 11,
        out_shape=[jax.ShapeDtypeStruct((n_out, hw), F32)] * 11,
        compiler_params=_cparams(("parallel", "parallel")),
        name="rw_prep",
    )(p, p, p, mu, wlo, g2, vecs)


def _rw_scan_kernel(rf, vf, kkf, wf, kf, bf, rb, vb, kkb, wb, kb, bb, yf_ref, yb_ref, s_ref, *, nb):
    c = RW_CHUNK

    @pl.when(pl.program_id(0) == 0)
    def _():
        s_ref[...] = jnp.zeros(s_ref.shape, F32)

    e2 = _block_ones(128, RW_HEAD)
    z2 = jnp.zeros((128, 128), BF16)
    rhs_pair = jnp.concatenate([jnp.concatenate([e2, z2], axis=1), jnp.concatenate([z2, e2], axis=1)], axis=0)
    e22 = jnp.concatenate([e2, e2], axis=0)
    lane = lax.broadcasted_iota(jnp.int32, (RW_HEAD, 128), 1)
    sub = lax.broadcasted_iota(jnp.int32, (RW_HEAD, 128), 0)
    diag = (lane % RW_HEAD) == sub
    sub8 = lax.broadcasted_iota(jnp.int32, (8, 128), 0)
    dirs = ((rf, vf, kkf, wf, kf, bf, yf_ref), (rb, vb, kkb, wb, kb, bb, yb_ref))

    def allreduce_rows(x):
        t = x[0:8]
        for i in range(1, 8):
            t = t + x[8 * i:8 * i + 8]
        for sh in (4, 2, 1):
            t = t + pltpu.roll(t, sh, 0)
        return t

    def col(row):
        return jnp.where(diag, row, 0.0).astype(BF16)

    def group(g, carry):
        tiles, ytiles = {}, {}
        for d, refs in enumerate(dirs):
            base = pl.multiple_of((g if d == 0 else c // 8 - 1 - g) * 8, 8)
            if d == 0:
                prev = lambda x, sh: jnp.where(sub8 >= sh, pltpu.roll(x, sh, 0), 1.0)
            else:
                prev = lambda x, sh: jnp.where(sub8 < 8 - sh, pltpu.roll(x, 8 - sh, 0), 1.0)
            last = 7 if d == 0 else 0
            for b in range(nb):
                for hp in range(2):
                    r_, v_, kk_, w_, k_, b_ = [ref[b, pl.ds(base, 8), pl.ds(hp * 128, 128)] for ref in refs[:6]]
                    gam = w_
                    for sh in (1, 2, 4):
                        gam = gam * prev(gam, sh)
                    inv = 1.0 / gam
                    tiles[d, b, hp] = (base, v_, b_ * inv, kk_ * prev(gam, 1), k_ * inv, r_ * gam,
                                       gam[last:last + 1, :])
                    ytiles[d, b, hp] = jnp.zeros((8, 128), F32)
        units = [(d, b, hp) for d in range(2) for b in range(nb) for hp in range(2)]
        for jj in range(8):
            lhs = []
            for (d, b, hp) in units:
                rw = slice(jj, jj + 1) if d == 0 else slice(7 - jj, 8 - jj)
                _, _, bh, kkh, kh, rh, _ = tiles[d, b, hp]
                lhs.append(jnp.concatenate([col(bh[rw]), col(kkh[rw])], axis=1))
                lhs.append(jnp.concatenate([col(kh[rw]), col(rh[rw])], axis=1))
            cm = _dot(jnp.concatenate(lhs, axis=0), rhs_pair)
            for u, (d, b, hp) in enumerate(units):
                j = jj if d == 0 else 7 - jj
                v_ = tiles[d, b, hp][1]
                r0 = 2 * u * RW_HEAD
                bc, kkc = cm[r0:r0 + RW_HEAD, :128], cm[r0:r0 + RW_HEAD, 128:]
                kc, rc = cm[r0 + RW_HEAD:r0 + 2 * RW_HEAD, :128], cm[r0 + RW_HEAD:r0 + 2 * RW_HEAD, 128:]
                s = s_ref[u]
                sa = jnp.concatenate([allreduce_rows(kkc * s)] * 8, axis=0)
                s = s - bc * sa + kc * v_[j:j + 1]
                s_ref[u] = s
                ytiles[d, b, hp] = jnp.where(sub8 == j, allreduce_rows(rc * s), ytiles[d, b, hp])
        for d, refs in enumerate(dirs):
            for b in range(nb):
                for hp in range(2):
                    u = (d * nb + b) * 2 + hp
                    gl = tiles[d, b, hp][6]
                    gh = gl.astype(BF16).astype(F32)
                    s_ref[u] = s_ref[u] * _dot(jnp.concatenate([col(gh), col(gl - gh)], axis=1), e22)
                    refs[6][b, pl.ds(tiles[d, b, hp][0], 8), pl.ds(hp * 128, 128)] = ytiles[d, b, hp]
        return carry

    lax.fori_loop(0, c // 8, group, 0)


def _rw_scan(r, v, kk, w0, k0, b0, w1, k1, b1, *, nb, t_all, n_ctx):
    c = RW_CHUNK
    nc, ncc = t_all // c, n_ctx // c
    hw = RW_HEADS * RW_HEAD
    shp = lambda a: a.reshape(nb, t_all, hw)
    fwd = lambda j: (0, j, 0)
    bwd = lambda j: (0, jnp.where(j < ncc, ncc - 1 - j, nc - 1 - (j - ncc)), 0)
    blk = (nb, c, hw)
    yshape = jax.ShapeDtypeStruct((nb, t_all, hw), F32)
    yf, yb = pl.pallas_call(
        functools.partial(_rw_scan_kernel, nb=nb),
        grid=(nc,),
        in_specs=[pl.BlockSpec(blk, fwd)] * 6 + [pl.BlockSpec(blk, bwd)] * 6,
        out_specs=[pl.BlockSpec(blk, fwd), pl.BlockSpec(blk, bwd)],
        out_shape=[yshape, yshape],
        scratch_shapes=[pltpu.VMEM((2 * nb * 2, RW_HEAD, 128), F32)],
        compiler_params=_cparams(("arbitrary",)),
        name="rw_scan",
    )(shp(r), shp(v), shp(kk), shp(w0), shp(k0), shp(b0),
      shp(r), shp(v), shp(kk), shp(w1), shp(k1), shp(b1))

    return yf.reshape(nb * t_all, hw), yb.reshape(nb * t_all, hw)


def _rw_post_kernel(yf_ref, yb_ref, bonus_ref, gate_ref, vec_ref, o_ref):
    e4 = _block_ones(RW_HEADS * RW_HEAD, RW_HEAD)
    y = yf_ref[...] + yb_ref[...]
    mean = _split_dot(y, e4) * (1.0 / RW_HEAD)
    yc = y - mean
    var = _split_dot(yc * yc, e4) * (1.0 / RW_HEAD)
    yn = yc * lax.rsqrt(var + RW_LN_EPS) * vec_ref[0:1, :] + vec_ref[1:2, :]
    o_ref[...] = (yn + bonus_ref[...]) * gate_ref[...]


def _rw_post(yf, yb, bonus, gate, vecs, *, nb, tpb_y, r0_y, nt):
    hw = RW_HEADS * RW_HEAD
    n_out = bonus.shape[0]
    ytok = lambda b, r: (b * tpb_y + r0_y + r, 0)
    tok = lambda b, r: (b * nt + r, 0)
    return pl.pallas_call(
        _rw_post_kernel,
        grid=(nb, nt),
        in_specs=[pl.BlockSpec((TM, hw), ytok), pl.BlockSpec((TM, hw), ytok),
                  pl.BlockSpec((TM, hw), tok), pl.BlockSpec((TM, hw), tok),
                  pl.BlockSpec((8, hw), lambda b, r: (0, 0))],
        out_specs=pl.BlockSpec((TM, hw), tok),
        out_shape=jax.ShapeDtypeStruct((n_out, hw), F32),
        compiler_params=_cparams(("parallel", "parallel")),
        name="rw_post",
    )(yf, yb, bonus, gate, vecs)


def _s5_scan_kernel(uf_ref, ub_ref, wb_ref, ab_ref, cf_ref, cb_ref, yf_ref, yb_ref, x_ref, st_ref, *, nb):
    c = S5_CHUNK
    nst = S5_GROUPS * S5_STATE

    @pl.when(pl.program_id(0) == 0)
    def _():
        st_ref[...] = jnp.zeros(st_ref.shape, F32)

    dirs = ((uf_ref, cf_ref, yf_ref), (ub_ref, cb_ref, yb_ref))
    for d, (u_ref, _, _) in enumerate(dirs):
        for b in range(nb):
            x_ref[d, b] = _dot(u_ref[b].astype(BF16), wb_ref[:, d * 2 * nst:(d + 1) * 2 * nst])

    def group(g, carry):
        for d in range(2):
            base = pl.multiple_of((g if d == 0 else c // 8 - 1 - g) * 8, 8)
            ar = ab_ref[d, :, 0:nst]
            ai = ab_ref[d, :, nst:2 * nst]
            for b in range(nb):
                u = d * nb + b
                xr = st_ref[u, :, 0:nst]
                xi = st_ref[u, :, nst:2 * nst]
                bur = x_ref[d, b, pl.ds(base, 8), 0:nst]
                bui = x_ref[d, b, pl.ds(base, 8), nst:2 * nst]
                rows_r, rows_i = [None] * 8, [None] * 8
                for jj in range(8):
                    j = jj if d == 0 else 7 - jj
                    xr, xi = (ar * xr - ai * xi + bur[j:j + 1, :], ar * xi + ai * xr + bui[j:j + 1, :])
                    rows_r[j], rows_i[j] = xr, xi
                st_ref[u, :, 0:nst] = xr
                st_ref[u, :, nst:2 * nst] = xi
                x_ref[d, b, pl.ds(base, 8), 0:nst] = jnp.concatenate(rows_r, axis=0)
                x_ref[d, b, pl.ds(base, 8), nst:2 * nst] = jnp.concatenate(rows_i, axis=0)
        return carry

    lax.fori_loop(0, c // 8, group, 0)

    for d, (_, c_ref, y_ref) in enumerate(dirs):
        for b in range(nb):
            y_ref[b] = _dot(x_ref[d, b].astype(BF16), c_ref[...])


def _s5_scan(p, w_b, ab, cf, cb, *, nb, t_all, n_ctx):
    c = S5_CHUNK
    nc, ncc = t_all // c, n_ctx // c
    nst2 = 2 * S5_GROUPS * S5_STATE
    width = S5_GROUPS * S5_GROUP_CH
    p3 = p.reshape(nb, t_all, N_EXT)
    fwd = lambda j: j
    bwd = lambda j: jnp.where(j < ncc, ncc - 1 - j, nc - 1 - (j - ncc))
    yshape = jax.ShapeDtypeStruct((nb, t_all, width), F32)
    full = lambda j: (0, 0)
    yf, yb = pl.pallas_call(
        functools.partial(_s5_scan_kernel, nb=nb),
        grid=(nc,),
        in_specs=[pl.BlockSpec((nb, c, width), lambda j: (0, fwd(j), EXT_S5 // width)),
                  pl.BlockSpec((nb, c, width), lambda j: (0, bwd(j), EXT_S5 // width)),
                  pl.BlockSpec((width, 2 * nst2), full),
                  pl.BlockSpec((2, 1, nst2), lambda j: (0, 0, 0)),
                  pl.BlockSpec((nst2, width), full), pl.BlockSpec((nst2, width), full)],
        out_specs=[pl.BlockSpec((nb, c, width), lambda j: (0, fwd(j), 0)),
                   pl.BlockSpec((nb, c, width), lambda j: (0, bwd(j), 0))],
        out_shape=[yshape, yshape],
        scratch_shapes=[pltpu.VMEM((2, nb, c, nst2), F32), pltpu.VMEM((2 * nb, 1, nst2), F32)],
        compiler_params=_cparams(("arbitrary",)),
        name="s5_scan",
    )(p3, p3, w_b, ab, cf, cb)
    return yf.reshape(nb * t_all, width), yb.reshape(nb * t_all, width)


def _s5_post_kernel(yf_ref, yb_ref, u_ref, d_ref, gw_ref, gb_ref, o_ref):
    y = yf_ref[...] + yb_ref[...] + d_ref[...] * u_ref[...]
    zg = 0.5 * y * (1.0 + jnp.tanh(math.sqrt(2.0 / math.pi) * (y + 0.044715 * (y * y * y))))
    o_ref[...] = zg * _sigmoid(_dot(zg.astype(BF16), gw_ref[...]) + gb_ref[...])


def _s5_post(yf, yb, p, d_row, glu_w, glu_b, *, nb, tpb, r0, nt):
    tok = lambda b, r: (b * tpb + r0 + r, 0)
    full = lambda b, r: (0, 0)
    return pl.pallas_call(
        _s5_post_kernel,
        grid=(nb, nt),
        in_specs=[pl.BlockSpec((TM, 256), tok), pl.BlockSpec((TM, 256), tok),
                  pl.BlockSpec((TM, 256), lambda b, r: (b * tpb + r0 + r, EXT_S5 // 256)),
                  pl.BlockSpec((1, 256), full), pl.BlockSpec((256, 256), full),
                  pl.BlockSpec((1, 256), full)],
        out_specs=pl.BlockSpec((TM, 256), lambda b, r: (b * nt + r, 0)),
        out_shape=jax.ShapeDtypeStruct((nb * nt * TM, 256), F32),
        compiler_params=_cparams(("parallel", "parallel")),
        name="s5_post",
    )(yf, yb, p, d_row, glu_w, glu_b)


def _merge_kernel(ya_ref, yb_ref, ys_ref, yd_ref, g0, g1, g2, g3, h_ref, m2_ref, wb_ref, wo_ref, o_ref):
    acc = None
    for n, (y_ref, g_ref) in enumerate(((ya_ref, g0), (yb_ref, g1), (ys_ref, g2), (yd_ref, g3))):
        term = _sigmoid(g_ref[...]) * _dot(y_ref[...].astype(BF16), wb_ref[n])
        acc = term if acc is None else acc + term
    o_ref[...] = h_ref[...] + m2_ref[0] * _dot(acc.astype(BF16), wo_ref[...])


def _merge(ya, yb, ys, yd, p, h, mod2, w_branch, w_out, *, nb, tpb, r0, nt):
    tok_in = lambda b, r: (b * tpb + r0 + r, 0)
    tok_out = lambda b, r: (b * nt + r, 0)
    full2 = lambda b, r: (0, 0)
    gate = [pl.BlockSpec((TM, D_MODEL), lambda b, r, c=c: (b * tpb + r0 + r, EXT_GATE // D_MODEL + c))
            for c in range(4)]
    return pl.pallas_call(
        _merge_kernel,
        grid=(nb, nt),
        in_specs=[pl.BlockSpec((TM, 256), tok_out)] * 4 + gate
                 + [pl.BlockSpec((TM, D_MODEL), tok_in),
                    pl.BlockSpec((1, 1, D_MODEL), lambda b, r: (jnp.where(r0 + r == 0, nb, b), 0, 0)),
                    pl.BlockSpec((4, 256, D_MODEL), lambda b, r: (0, 0, 0)),
                    pl.BlockSpec((D_MODEL, D_MODEL), full2)],
        out_specs=pl.BlockSpec((TM, D_MODEL), tok_out),
        out_shape=jax.ShapeDtypeStruct((nb * nt * TM, D_MODEL), F32),
        compiler_params=_cparams(("parallel", "parallel")),
        name="merge",
    )(ya, yb, ys, yd, p, p, p, p, h, mod2, w_branch, w_out)


def _router_kernel(h_ref, g_ref, sh_ref, sc_ref, wh_ref, wm_ref, b_ref, x_ref, lg_ref):
    x = _rms(h_ref[...], g_ref[...], NORM_EPS) * (1.0 + sc_ref[0]) + sh_ref[0]
    xh = x.astype(BF16)
    xm = (x - xh.astype(F32)).astype(BF16)
    bits = lax.bitcast_convert_type(xh.astype(F32), jnp.uint32)
    half = D_MODEL // 2
    x_ref[...] = (bits[:, :half] >> 16) | (bits[:, half:] & jnp.uint32(0xFFFF0000))
    lg = (_dot(xh, wh_ref[...]) + _dot(xm, wh_ref[...]) + _dot(xh, wm_ref[...])) + b_ref[...]

    lane = lax.broadcasted_iota(jnp.int32, lg.shape, 1)
    lanef = lane.astype(F32)
    neg = jnp.float32(-jnp.inf)
    big = jnp.float32(1e9)
    rmax = lambda v: jnp.max(v, axis=-1, keepdims=True)
    rmin = lambda v: jnp.min(v, axis=-1, keepdims=True)
    rsum = lambda v: jnp.sum(v, axis=-1, keepdims=True)

    gmask = lane < MOE_GROUPS
    mg = rmax(jnp.where(gmask, lg, neg))
    eg = jnp.where(gmask, jnp.exp(lg - mg), 0.0)
    pg = eg / rsum(eg)
    pg_top = rmax(pg)
    g_sel = rmin(jnp.where(jnp.logical_and(gmask, pg == pg_top), lanef, big))
    lo = MOE_GROUPS + MOE_PER_GROUP * g_sel
    emask = jnp.logical_and(lanef >= lo, lanef < lo + MOE_PER_GROUP)
    me = rmax(jnp.where(emask, lg, neg))
    ee = jnp.where(emask, jnp.exp(lg - me), 0.0)
    pe = jnp.where(emask, ee / rsum(ee), -1.0)
    p1 = rmax(pe)
    i1 = rmin(jnp.where(pe == p1, lanef, big))
    pe2 = jnp.where(lanef == i1, -1.0, pe)
    p2 = rmax(pe2)
    i2 = rmin(jnp.where(pe2 == p2, lanef, big))
    den = p1 + p2
    out = jnp.where(lane == 0, i1 - MOE_GROUPS, 0.0)
    out = jnp.where(lane == 1, i2 - MOE_GROUPS, out)
    out = jnp.where(lane == 2, pg_top * p1 / den, out)
    out = jnp.where(lane == 3, pg_top * p2 / den, out)
    lg_ref[...] = out


def _router(h, g, shift, scale, wh, wm, bias, *, nb, nt, ctx_first):
    tok = lambda b, r: (b * nt + r, 0)
    full = lambda b, r: (0, 0)
    if ctx_first:
        modmap = lambda b, r: (jnp.where(r == 0, nb, b), 0, 0)
    else:
        modmap = lambda b, r: (b, 0, 0)
    n = h.shape[0]
    return pl.pallas_call(
        _router_kernel,
        grid=(nb, nt),
        in_specs=[pl.BlockSpec((TM, D_MODEL), tok), pl.BlockSpec((1, D_MODEL), full),
                  pl.BlockSpec((1, 1, D_MODEL), modmap), pl.BlockSpec((1, 1, D_MODEL), modmap),
                  pl.BlockSpec((D_MODEL, 128), full), pl.BlockSpec((D_MODEL, 128), full),
                  pl.BlockSpec((1, 128), full)],
        out_specs=[pl.BlockSpec((TM, D_MODEL // 2), tok), pl.BlockSpec((TM, 128), tok)],
        out_shape=[jax.ShapeDtypeStruct((n, D_MODEL // 2), jnp.uint32), jax.ShapeDtypeStruct((n, 128), F32)],
        compiler_params=_cparams(("parallel", "parallel")),
        name="router",
    )(h, g, shift, scale, wh, wm, bias)


def _expert_kernel(be_ref, nv_ref, x_ref, wg_ref, wu_ref, wd_ref, o_ref, wgb_ref, wub_ref, wdb_ref):
    i = pl.program_id(0)

    @pl.when(jnp.logical_or(i == 0, be_ref[i] != be_ref[jnp.maximum(i - 1, 0)]))
    def _():
        wgb_ref[...] = wg_ref[0].astype(BF16)
        wub_ref[...] = wu_ref[0].astype(BF16)
        wdb_ref[...] = wd_ref[0].astype(BF16)

    @pl.when(i < nv_ref[0])
    def _():
        u = x_ref[...]
        x = jnp.concatenate([lax.bitcast_convert_type(u << 16, F32),
                             lax.bitcast_convert_type(u & jnp.uint32(0xFFFF0000), F32)], axis=1).astype(BF16)
        hb = _silu(_dot(x, wgb_ref[...])) * _dot(x, wub_ref[...])
        o_ref[...] = _dot(hb.astype(BF16), wdb_ref[...])

    @pl.when(i >= nv_ref[0])
    def _():
        o_ref[...] = jnp.zeros(o_ref.shape, F32)


def _experts(xs, block_e, n_valid, w_gate, w_up, w_down):
    n_slots = xs.shape[0]
    n_blocks = n_slots // MOE_BLK
    wmap = lambda i, be, nv: (be[i], 0, 0)
    return pl.pallas_call(
        _expert_kernel,
        grid_spec=pltpu.PrefetchScalarGridSpec(
            num_scalar_prefetch=2,
            grid=(n_blocks,),
            in_specs=[pl.BlockSpec((MOE_BLK, D_MODEL // 2), lambda i, be, nv: (i, 0)),
                      pl.BlockSpec((1, D_MODEL, D_EXPERT), wmap),
                      pl.BlockSpec((1, D_MODEL, D_EXPERT), wmap),
                      pl.BlockSpec((1, D_EXPERT, D_MODEL), wmap)],
            out_specs=pl.BlockSpec((MOE_BLK, D_MODEL), lambda i, be, nv: (i, 0)),
            scratch_shapes=[pltpu.VMEM((D_MODEL, D_EXPERT), BF16), pltpu.VMEM((D_MODEL, D_EXPERT), BF16),
                            pltpu.VMEM((D_EXPERT, D_MODEL), BF16)]),
        out_shape=jax.ShapeDtypeStruct((n_slots, D_MODEL), F32),
        compiler_params=_cparams(("arbitrary",)),
        name="experts",
    )(block_e, n_valid, xs, w_gate, w_up, w_down)


def _combine_kernel(h_ref, y01_ref, w_ref, m5_ref, g_ref, o_ref, *, final):
    w = w_ref[...]
    y = y01_ref[:, :D_MODEL] * w[:, 0:1] + y01_ref[:, D_MODEL:] * w[:, 1:2]
    h = h_ref[...] + m5_ref[0] * y
    if final:
        h = _rms(h, g_ref[...], NORM_EPS)
    o_ref[...] = h


def _combine(h, y01, wts, mod5, g_final, *, nb, nt, ctx_first, final):
    tok = lambda b, r: (b * nt + r, 0)
    if ctx_first:
        modmap = lambda b, r: (jnp.where(r == 0, nb, b), 0, 0)
    else:
        modmap = lambda b, r: (b, 0, 0)
    return pl.pallas_call(
        functools.partial(_combine_kernel, final=final),
        grid=(nb, nt),
        in_specs=[pl.BlockSpec((TM, D_MODEL), tok), pl.BlockSpec((TM, 2 * D_MODEL), tok),
                  pl.BlockSpec((TM, 128), tok), pl.BlockSpec((1, 1, D_MODEL), modmap),
                  pl.BlockSpec((1, D_MODEL), lambda b, r: (0, 0))],
        out_specs=pl.BlockSpec((TM, D_MODEL), tok),
        out_shape=jax.ShapeDtypeStruct(h.shape, F32),
        compiler_params=_cparams(("parallel", "parallel")),
        name="combine",
    )(h, y01, wts, mod5, g_final)


def _sc_gather(table, idx):
    n, d = idx.shape[0], table.shape[1]
    piece = d // SC_ROW_PIECES
    tab = table.reshape(table.shape[0] * SC_ROW_PIECES, piece)
    idx_p = (idx[:, None] * SC_ROW_PIECES
             + jnp.arange(SC_ROW_PIECES, dtype=jnp.int32)[None, :]).reshape(1, n * SC_ROW_PIECES)
    mesh = plsc.VectorSubcoreMesh(core_axis_name="core", subcore_axis_name="subcore")

    @pl.kernel(out_type=jax.ShapeDtypeStruct((n * SC_ROW_PIECES, piece), table.dtype), mesh=mesh)
    def gather_kernel(x_hbm, i_hbm, o_hbm):
        def body(i_vmem, o_vmem):
            pltpu.sync_copy(x_hbm.at[i_vmem.at[0]], o_vmem)

        pltpu.emit_pipeline(
            body,
            grid=(n * SC_ROW_PIECES // SC_INDEX_WINDOW,),
            in_specs=[pl.BlockSpec((1, SC_INDEX_WINDOW), index_map=lambda i: (0, i))],
            out_specs=[pl.BlockSpec((SC_INDEX_WINDOW, piece), index_map=lambda i: (i, 0))],
            core_axis_name=("core", "subcore"),
            dimension_semantics=(pltpu.PARALLEL,),
        )(i_hbm, o_hbm)

    return gather_kernel(tab, idx_p).reshape(n, d)


def _moe(h, g2, shift, scale, mod5, wh, wm, rbias, w_gate, w_up, w_down, g_final, *, layer, nb, nt, ctx_first,
         final):
    n = h.shape[0]
    x_bf, route = _router(h, g2, shift, scale, wh, wm, rbias, nb=nb, nt=nt, ctx_first=ctx_first)
    idx = route[:, :MOE_TOPK].astype(jnp.int32)
    wts = route[:, MOE_TOPK:2 * MOE_TOPK]
    n_as = n * MOE_TOPK
    flat_e = idx.reshape(n_as)
    onehot = (flat_e[:, None] == jnp.arange(MOE_EXPERTS, dtype=jnp.int32)[None, :]).astype(jnp.int32)
    csum = jnp.cumsum(onehot, axis=0)
    counts = csum[-1]
    rank = jnp.sum(jnp.where(onehot > 0, csum - 1, 0), axis=1)
    padded = (counts + MOE_BLK - 1) // MOE_BLK * MOE_BLK
    pad_end = jnp.cumsum(padded)
    pad_start = pad_end - padded
    slot = pad_start[flat_e] + rank
    n_blocks = (n_as + MOE_EXPERTS * (MOE_BLK - 1) + MOE_BLK - 1) // MOE_BLK
    n_slots = n_blocks * MOE_BLK
    slot_tok = jnp.zeros((n_slots,), jnp.int32).at[slot].set(jnp.arange(n_as, dtype=jnp.int32) // MOE_TOPK,
                                                             unique_indices=True)
    starts = jnp.arange(n_blocks, dtype=jnp.int32) * MOE_BLK
    block_e = jnp.minimum(jnp.sum((pad_end[None, :] <= starts[:, None]).astype(jnp.int32), axis=1),
                          MOE_EXPERTS - 1)
    n_valid = (pad_end[-1:] // MOE_BLK).astype(jnp.int32)
    xs = _sc_gather(x_bf, slot_tok)
    ys = _experts(xs, block_e + layer * MOE_EXPERTS, n_valid, w_gate, w_up, w_down)
    y01 = _sc_gather(ys, slot).reshape(n, MOE_TOPK * D_MODEL)
    wts_pad = jnp.pad(wts.astype(F32), ((0, 0), (0, 128 - MOE_TOPK)))
    return _combine(h, y01, wts_pad, mod5, g_final, nb=nb, nt=nt, ctx_first=ctx_first, final=final)


_ROT_SRC = np.array(list(range(8, 16)) + list(range(0, 8)) + list(range(24, 32)) + list(range(16, 24)))
_ROT_SIGN = np.array([-1.0] * 8 + [1.0] * 8 + [-1.0] * 8 + [1.0] * 8, np.float32)


def _rot_cols(w):
    k = w.shape[-1] // ROPE_DIM
    src = np.concatenate([_ROT_SRC + ROPE_DIM * i for i in range(k)])
    sign = np.tile(_ROT_SIGN, k)
    return w[..., src] * sign


def _rope_tables(n_ctx, n_lat):
    rows = n_lat // GRID_W
    row = jnp.repeat(jnp.arange(rows, dtype=F32), GRID_W)
    col = (jnp.arange(rows * GRID_W) % GRID_W).astype(F32)
    nf = ROPE_DIM // 4
    inv = ROPE_BASE ** (-jnp.arange(nf, dtype=F32) / nf)
    ar = row[:, None] * inv
    ac = col[:, None] * inv
    ang = jnp.concatenate([ar, ar, ac, ac], axis=-1)
    cos = jnp.concatenate([jnp.ones((n_ctx, ROPE_DIM), F32), jnp.cos(ang)], axis=0)
    sin = jnp.concatenate([jnp.zeros((n_ctx, ROPE_DIM), F32), jnp.sin(ang)], axis=0)
    return cos, sin


def _block_diag(blocks):
    g, a, b = blocks.shape
    tiled = jnp.tile(blocks.reshape(g * a, b), (1, g))
    rows = lax.broadcasted_iota(jnp.int32, (g * a, g * b), 0) // a
    cols = lax.broadcasted_iota(jnp.int32, (g * a, g * b), 1) // b
    return jnp.where(rows == cols, tiled, 0.0)


def _pick_tk(t_all):
    best = 128
    for tk in range(128, ATTN_TK_MAX + 1, 128):
        if t_all % tk == 0:
            best = tk
    return best


def kernel(x, c, ctx, c_ctx, w_mod, b_mod, norm1_g, norm2_g, w_in, mla_q_norm_g, mla_kv_norm_g, mla_w_uq, mla_w_ukv, rw_mu, rw_w0, rw_w2, rw_a0, rw_a2, rw_g2, rw_k_k, rw_k_a, rw_r_k, rw_lnx_g, rw_lnx_b, s5_a_re, s5_a_im, s5_log_dt, s5_b_re, s5_b_im, s5_c_re, s5_c_im, s5_d, s5_glu_w, s5_glu_b, diff_lq1, diff_lk1, diff_lq2, diff_lk2, diff_subln_g, w_branch, w_out, router_g_w, router_g_b, router_e_w, router_e_b, exp_w_gate, exp_w_up, exp_w_down, final_norm_g):
    nb, n_lat, d = x.shape
    n_ctx = ctx.shape[1]
    depth = w_mod.shape[0]
    t_all = n_ctx + n_lat
    assert d == D_MODEL and n_ctx == TM and n_lat % TM == 0
    tpb = t_all // TM
    tk = _pick_tk(t_all)
    hw = RW_HEADS * RW_HEAD

    cos, sin = _rope_tables(n_ctx, n_lat)
    mla_scale = (MLA_NOPE + MLA_ROPE) ** -0.5 * LOG2E
    z32 = jnp.zeros((t_all, 32), F32)
    cq_tab = jnp.concatenate([jnp.ones((t_all, 64), F32), cos, z32], axis=1) * mla_scale
    sq_tab = jnp.concatenate([jnp.zeros((t_all, 64), F32), sin, z32], axis=1) * mla_scale
    ck_tab = jnp.concatenate([cos, sin, jnp.zeros((t_all, 64), F32)], axis=1)
    dcos = jnp.tile(cos, (1, 8))
    dsin = jnp.tile(sin, (1, 8))

    c_rows = jnp.concatenate([c, c_ctx[None, :], jnp.zeros((8 - nb - 1, d), F32)], axis=0)

    h = jnp.concatenate([ctx, x], axis=1).reshape(nb * t_all, d)

    for l in range(depth):
        last = l == depth - 1
        r0, nt = (1, tpb - 1) if last else (0, tpb)

        mod = _mm(c_rows, w_mod[l].astype(BF16), b_mod[l][None, :], tm=8, tn=1536, pre_silu=True, name="mod")
        mods = [mod[:nb + 1, i * d:(i + 1) * d].reshape(nb + 1, 1, d) for i in range(6)]

        wi = w_in[l]
        o_rw, o_s5, o_df, o_gt = 416, 1440, 1696, 2464
        w_kr = wi[:, 384:416]
        w_dq, w_dk, w_dv = wi[:, o_df:o_df + 256], wi[:, o_df + 256:o_df + 512], wi[:, o_df + 512:o_df + 768]
        w_ext = jnp.concatenate(
            [wi[:, o_rw:o_s5],
             wi[:, :416], _rot_cols(w_kr), jnp.zeros((d, 64), F32),
             wi[:, o_s5:o_df],
             w_dq, w_dk, w_dv, _rot_cols(w_dq), _rot_cols(w_dk),
             wi[:, o_gt:]], axis=1).astype(BF16)
        p = _inproj(h, norm1_g[l][None, :], mods[0], mods[1], w_ext, nb=nb, tpb=tpb)

        wq = mla_w_uq[l].reshape(MLA_Q_LORA, MLA_HEADS, MLA_NOPE + MLA_ROPE)
        zq = jnp.zeros((MLA_Q_LORA, MLA_HEADS, 32), F32)
        wa = jnp.concatenate([wq, zq], axis=2).reshape(MLA_Q_LORA, 512).astype(BF16)
        wb = jnp.concatenate([jnp.zeros((MLA_Q_LORA, MLA_HEADS, 64), F32), _rot_cols(wq[:, :, MLA_NOPE:]), zq],
                             axis=2).reshape(MLA_Q_LORA, 512).astype(BF16)
        wkv = mla_w_ukv[l].reshape(MLA_KV_LORA, MLA_HEADS, MLA_NOPE + MLA_V)
        wk = jnp.concatenate([wkv[:, :, :MLA_NOPE], jnp.zeros((MLA_KV_LORA, MLA_HEADS, 64), F32)],
                             axis=2).reshape(MLA_KV_LORA, 512).astype(BF16)
        wv = wkv[:, :, MLA_NOPE:].reshape(MLA_KV_LORA, MLA_HEADS * MLA_V).astype(BF16)
        pk_np = np.zeros((128, 512), np.float32)
        for hh in range(MLA_HEADS):
            for i in range(32):
                pk_np[i, hh * 128 + 64 + i] = 1.0
                pk_np[32 + i, hh * 128 + 64 + i] = 1.0
        pk = jnp.asarray(pk_np, BF16)
        q_m, k_m, v_m = _mla_prep(p, mla_q_norm_g[l][None, :], mla_kv_norm_g[l][None, :], wa, wb, wk, wv, pk,
                                  cq_tab, sq_tab, ck_tab, nb=nb, tpb=tpb)
        ya_lat = _flash_mla(q_m, k_m, v_m, nb=nb, t_all=t_all, q_start=n_ctx, q_len=n_lat, k_len=t_all, tk=tk)

        q_d, k_d, v_d = _diff_prep(p, dcos, dsin, nb=nb, tpb=tpb)
        lam_init = 0.8 - 0.6 * math.exp(-0.3 * l)
        lam = (jnp.exp(jnp.sum(diff_lq1[l] * diff_lk1[l])) - jnp.exp(jnp.sum(diff_lq2[l] * diff_lk2[l])) + lam_init)
        lam_row = jnp.full((1, 256), lam, F32)
        g_row = jnp.tile(diff_subln_g[l], DIFF_HEADS)[None, :]
        yd_lat = _flash_diff(q_d, k_d, v_d, lam_row, g_row, lam_init=lam_init, nb=nb, t_all=t_all,
                             q_start=n_ctx, q_len=n_lat, k_len=t_all, tk=tk)
        if last:
            ya, yd = ya_lat, yd_lat
        else:
            ya_ctx = _flash_mla(q_m, k_m, v_m, nb=nb, t_all=t_all, q_start=0, q_len=n_ctx, k_len=n_ctx, tk=n_ctx)
            yd_ctx = _flash_diff(q_d, k_d, v_d, lam_row, g_row, lam_init=lam_init, nb=nb, t_all=t_all,
                                 q_start=0, q_len=n_ctx, k_len=n_ctx, tk=n_ctx)
            comb = lambda a_c, a_l: jnp.concatenate(
                [a_c.reshape(nb, n_ctx, -1), a_l.reshape(nb, n_lat, -1)], axis=1).reshape(nb * t_all, -1)
            ya, yd = comb(ya_ctx, ya_lat), comb(yd_ctx, yd_lat)

        wlo = jnp.zeros((128, 4 * hw), F32)
        wlo = wlo.at[:64, 0:hw].set(rw_w2[l, 0]).at[:64, hw:2 * hw].set(rw_w2[l, 1])
        wlo = wlo.at[64:, 2 * hw:3 * hw].set(rw_a2[l, 0]).at[64:, 3 * hw:].set(rw_a2[l, 1])
        vecs = jnp.stack([rw_k_k[l], rw_k_a[l], rw_r_k[l].reshape(hw), rw_w0[l, 0], rw_w0[l, 1],
                          rw_a0[l, 0], rw_a0[l, 1], jnp.zeros((hw,), F32)], axis=0)
        (r_, v_, kk_, w0_, k0_, b0_, w1_, k1_, b1_, bonus, gate_rw) = _rw_prep(
            p, rw_mu[l][None, :], wlo.astype(BF16), rw_g2[l].astype(BF16), vecs, nb=nb, tpb=tpb, r0=0, nt=tpb)
        yf, yb_ = _rw_scan(r_, v_, kk_, w0_, k0_, b0_, w1_, k1_, b1_, nb=nb, t_all=t_all, n_ctx=n_ctx)
        ln_vecs = jnp.concatenate([rw_lnx_g[l][None, :], rw_lnx_b[l][None, :], jnp.zeros((6, hw), F32)], axis=0)
        if last:
            trim = lambda a: a.reshape(nb, t_all, hw)[:, n_ctx:].reshape(nb * n_lat, hw)
            bonus, gate_rw = trim(bonus), trim(gate_rw)
        y_rw = _rw_post(yf, yb_, bonus, gate_rw, ln_vecs, nb=nb, tpb_y=tpb, r0_y=r0, nt=nt)

        bbs, abs_, cfs = [], [], []
        for dd in range(2):
            lr, li = s5_a_re[l, dd], s5_a_im[l, dd]
            dt = jnp.exp(s5_log_dt[l, dd])[:, None]
            mag = jnp.exp(lr * dt)
            ab_re, ab_im = mag * jnp.cos(li * dt), mag * jnp.sin(li * dt)
            den = lr * lr + li * li
            nr, ni = ab_re - 1.0, ab_im
            cf_re = (nr * lr + ni * li) / den
            cf_im = (ni * lr - nr * li) / den
            bre, bim = s5_b_re[l, dd], s5_b_im[l, dd]
            bb_re = cf_re[..., None] * bre - cf_im[..., None] * bim
            bb_im = cf_re[..., None] * bim + cf_im[..., None] * bre
            bbs.append(jnp.concatenate([_block_diag(bb_re.transpose(0, 2, 1)),
                                        _block_diag(bb_im.transpose(0, 2, 1))], axis=1))
            abs_.append(jnp.concatenate([ab_re.reshape(-1), ab_im.reshape(-1)])[None, :])
            cfs.append(jnp.concatenate([_block_diag(s5_c_re[l, dd].transpose(0, 2, 1)),
                                        -_block_diag(s5_c_im[l, dd].transpose(0, 2, 1))], axis=0))
        yf_s5, yb_s5 = _s5_scan(p, jnp.concatenate(bbs, axis=1).astype(BF16), jnp.stack(abs_, axis=0),
                                cfs[0].astype(BF16), cfs[1].astype(BF16), nb=nb, t_all=t_all, n_ctx=n_ctx)
        y_s5 = _s5_post(yf_s5, yb_s5, p, s5_d[l].reshape(1, 256), s5_glu_w[l].astype(BF16),
                        s5_glu_b[l][None, :], nb=nb, tpb=tpb, r0=r0, nt=nt)

        h = _merge(ya, y_rw, y_s5, yd, p, h, mods[2], w_branch[l].astype(BF16), w_out[l].astype(BF16),
                   nb=nb, tpb=tpb, r0=r0, nt=nt)

        wr = jnp.concatenate([router_g_w[l], router_e_w[l], jnp.zeros((d, 128 - MOE_GROUPS - MOE_EXPERTS), F32)], axis=1)
        wr_h = wr.astype(BF16)
        wr_m = (wr - wr_h.astype(F32)).astype(BF16)
        rbias = jnp.concatenate([router_g_b[l], router_e_b[l],
                                 jnp.zeros((128 - MOE_GROUPS - MOE_EXPERTS,), F32)])[None, :]
        h = _moe(h, norm2_g[l][None, :], mods[3], mods[4], mods[5], wr_h, wr_m, rbias,
                 exp_w_gate.reshape(depth * MOE_EXPERTS, d, D_EXPERT), exp_w_up.reshape(depth * MOE_EXPERTS, d, D_EXPERT),
                 exp_w_down.reshape(depth * MOE_EXPERTS, D_EXPERT, d),
                 final_norm_g[None, :], layer=l, nb=nb, nt=nt, ctx_first=not last, final=last)

    return h.reshape(nb, n_lat, d)
```

```python
import functools
import math

import jax
import jax.numpy as jnp
import numpy as np
from jax import lax
from jax.experimental import pallas as pl
from jax.experimental.pallas import tpu as pltpu
from jax.experimental.pallas import tpu_sc as plsc

F32 = jnp.float32
BF16 = jnp.bfloat16

TM = 256
VMEM_LIMIT = 48 * 1024 * 1024

D_MODEL = 1024
GRID_W = 64
ROPE_DIM = 32
ROPE_BASE = 10000.0
NORM_EPS = 1e-6
MLA_HEADS, MLA_NOPE, MLA_ROPE, MLA_V = 4, 64, 32, 64
MLA_Q_LORA, MLA_KV_LORA = 256, 128
RW_HEADS, RW_HEAD = 4, 64
RW_LN_EPS = 64e-5
S5_GROUPS, S5_GROUP_CH, S5_STATE = 16, 16, 64
DIFF_HEADS, DIFF_HD = 4, 32
DIFF_EPS = 1e-5
MOE_GROUPS, MOE_PER_GROUP, MOE_TOPK = 4, 8, 2
MOE_EXPERTS = MOE_GROUPS * MOE_PER_GROUP
D_EXPERT = 512
MOE_BLK = 256
RW_CHUNK = 64
S5_CHUNK = 128
ATTN_TK_MAX = 2816
SC_INDEX_TILE = 128
SC_ROWS_PER_STEP = 32
LOG2E = math.log2(math.e)

EXT_RW, EXT_MLA, EXT_S5, EXT_DIFF, EXT_GATE = 0, 1024, 1536, 1792, 3072
N_EXT = 7168


def _cparams(sem, vmem=VMEM_LIMIT):
    return pltpu.CompilerParams(dimension_semantics=sem, vmem_limit_bytes=vmem)


def _dot(a, b):
    return jnp.dot(a, b, preferred_element_type=F32)


def _dot_nt(a, b):
    return lax.dot_general(a, b, (((1,), (1,)), ((), ())), preferred_element_type=F32)


def _split_dot(x, e):
    hi = x.astype(BF16)
    mid = (x - hi.astype(F32)).astype(BF16)
    return _dot(hi, e) + _dot(mid, e)


def _block_ones(n, blk):
    r = lax.broadcasted_iota(jnp.int32, (n, n), 0) // blk
    c = lax.broadcasted_iota(jnp.int32, (n, n), 1) // blk
    return (r == c).astype(BF16)


def _sigmoid(x):
    return 1.0 / (1.0 + jnp.exp(-x))


def _silu(x):
    return x * _sigmoid(x)


def _softplus(x):
    return jnp.maximum(x, 0.0) + jnp.log(1.0 + jnp.exp(-jnp.abs(x)))


def _rms(x, g, eps):
    return x * lax.rsqrt(jnp.mean(x * x, axis=-1, keepdims=True) + eps) * g


def _mm_kernel(x_ref, w_ref, b_ref, o_ref, *, pre_silu):
    x = x_ref[...].astype(F32)
    if pre_silu:
        x = _silu(x)
    o_ref[...] = _dot(x.astype(BF16), w_ref[...]) + b_ref[...]


def _mm(x, w, b, *, tm, tn, pre_silu=False, name="mm"):
    m, k = x.shape
    n = w.shape[1]
    return pl.pallas_call(
        functools.partial(_mm_kernel, pre_silu=pre_silu),
        grid=(m // tm, n // tn),
        in_specs=[pl.BlockSpec((tm, k), lambda i, j: (i, 0)),
                  pl.BlockSpec((k, tn), lambda i, j: (0, j)),
                  pl.BlockSpec((1, tn), lambda i, j: (0, j))],
        out_specs=pl.BlockSpec((tm, tn), lambda i, j: (i, j)),
        out_shape=jax.ShapeDtypeStruct((m, n), F32),
        compiler_params=_cparams(("parallel", "arbitrary")),
        name=name,
    )(x, w, b)


def _inproj_kernel(h_ref, g_ref, sh_ref, sc_ref, w_ref, o_ref):
    x = _rms(h_ref[...], g_ref[...], NORM_EPS)
    xn = (x * (1.0 + sc_ref[0]) + sh_ref[0]).astype(BF16)
    o_ref[...] = _dot(xn, w_ref[...])


def _inproj(h, g, shift, scale, w_ext, *, nb, tpb):
    n = h.shape[0]
    tn = N_EXT // 2
    modmap = lambda j, b, r: (jnp.where(r == 0, nb, b), 0, 0)
    return pl.pallas_call(
        _inproj_kernel,
        grid=(N_EXT // tn, nb, tpb),
        in_specs=[pl.BlockSpec((TM, D_MODEL), lambda j, b, r: (b * tpb + r, 0)),
                  pl.BlockSpec((1, D_MODEL), lambda j, b, r: (0, 0)),
                  pl.BlockSpec((1, 1, D_MODEL), modmap),
                  pl.BlockSpec((1, 1, D_MODEL), modmap),
                  pl.BlockSpec((D_MODEL, tn), lambda j, b, r: (0, j))],
        out_specs=pl.BlockSpec((TM, tn), lambda j, b, r: (b * tpb + r, j)),
        out_shape=jax.ShapeDtypeStruct((n, N_EXT), F32),
        compiler_params=_cparams(("parallel", "parallel", "parallel")),
        name="inproj",
    )(h, g, shift, scale, w_ext)


def _mla_prep_kernel(p_ref, gq_ref, gkv_ref, wa_ref, wb_ref, wk_ref, wv_ref, pk_ref,
                     cq_ref, sq_ref, ck_ref, q_ref, k_ref, v_ref):
    seg = p_ref[...]
    nq = _rms(seg[:, :MLA_Q_LORA], gq_ref[...], NORM_EPS).astype(BF16)
    nkv = _rms(seg[:, MLA_Q_LORA:MLA_Q_LORA + MLA_KV_LORA], gkv_ref[...], NORM_EPS).astype(BF16)
    cq = jnp.concatenate([cq_ref[...]] * MLA_HEADS, axis=1)
    sq = jnp.concatenate([sq_ref[...]] * MLA_HEADS, axis=1)
    q = _dot(nq, wa_ref[...]) * cq + _dot(nq, wb_ref[...]) * sq
    q_ref[...] = q.astype(BF16)
    kr = (seg[:, 384:512] * ck_ref[...]).astype(BF16)
    k_ref[...] = (_dot(nkv, wk_ref[...]) + _dot(kr, pk_ref[...])).astype(BF16)
    v_ref[...] = _dot(nkv, wv_ref[...]).astype(BF16)


def _mla_prep(p, gq, gkv, wa, wb, wk, wv, pk, cq_tab, sq_tab, ck_tab, *, nb, tpb):
    n = p.shape[0]
    tok = lambda b, r: (b * tpb + r, 0)
    pos = lambda b, r: (r, 0)
    full = lambda b, r: (0, 0)
    return pl.pallas_call(
        _mla_prep_kernel,
        grid=(nb, tpb),
        in_specs=[pl.BlockSpec((TM, 512), lambda b, r: (b * tpb + r, EXT_MLA // 512)),
                  pl.BlockSpec((1, MLA_Q_LORA), full),
                  pl.BlockSpec((1, MLA_KV_LORA), full),
                  pl.BlockSpec((MLA_Q_LORA, 512), full),
                  pl.BlockSpec((MLA_Q_LORA, 512), full),
                  pl.BlockSpec((MLA_KV_LORA, 512), full),
                  pl.BlockSpec((MLA_KV_LORA, 256), full),
                  pl.BlockSpec((128, 512), full),
                  pl.BlockSpec((TM, 128), pos),
                  pl.BlockSpec((TM, 128), pos),
                  pl.BlockSpec((TM, 128), pos)],
        out_specs=[pl.BlockSpec((TM, 512), tok),
                   pl.BlockSpec((TM, 512), tok),
                   pl.BlockSpec((TM, 256), tok)],
        out_shape=[jax.ShapeDtypeStruct((n, 512), BF16),
                   jax.ShapeDtypeStruct((n, 512), BF16),
                   jax.ShapeDtypeStruct((n, 256), BF16)],
        compiler_params=_cparams(("parallel", "parallel")),
        name="mla_prep",
    )(p, gq, gkv, wa, wb, wk, wv, pk, cq_tab, sq_tab, ck_tab)


def _flash_mla_kernel(q_ref, k_ref, v_ref, o_ref, m_ref, l_ref, acc_ref):
    kk = pl.program_id(2)
    tq = q_ref.shape[0]

    @pl.when(kk == 0)
    def _():
        m_ref[...] = jnp.full(m_ref.shape, -jnp.inf, F32)
        l_ref[...] = jnp.zeros(l_ref.shape, F32)
        acc_ref[...] = jnp.zeros(acc_ref.shape, F32)

    ps, alphas = [], []
    for h in range(MLA_HEADS):
        s = _dot_nt(q_ref[:, h * 128:(h + 1) * 128], k_ref[:, h * 128:(h + 1) * 128])
        m_prev = m_ref[h]
        m_new = jnp.maximum(m_prev, jnp.max(s, axis=-1, keepdims=True))
        alpha = jnp.exp2(m_prev - m_new)
        p = jnp.exp2(s - m_new)
        l_ref[h] = alpha * l_ref[h] + jnp.sum(p, axis=-1, keepdims=True)
        m_ref[h] = m_new
        alphas.append(alpha)
        ps.append(p.astype(BF16))
    pv = _dot(jnp.concatenate(ps, axis=0), v_ref[...])
    acc_ref[...] = jnp.concatenate(alphas, axis=0) * acc_ref[...] + pv

    @pl.when(kk == pl.num_programs(2) - 1)
    def _():
        lane = lax.broadcasted_iota(jnp.int32, (tq, MLA_HEADS * MLA_V), 1)
        out = jnp.zeros((tq, MLA_HEADS * MLA_V), F32)
        for h in range(MLA_HEADS):
            out = jnp.where(lane // MLA_V == h, acc_ref[h * tq:(h + 1) * tq, :] / l_ref[h], out)
        o_ref[...] = out


def _flash_mla(q, k, v, *, nb, t_all, q_start, q_len, k_len, tk):
    tq = TM
    tpb_q, tpb_k = t_all // tq, t_all // tk
    q0 = q_start // tq
    nq = q_len // tq
    return pl.pallas_call(
        _flash_mla_kernel,
        grid=(nb, nq, k_len // tk),
        in_specs=[pl.BlockSpec((tq, 512), lambda b, i, kk: (b * tpb_q + q0 + i, 0)),
                  pl.BlockSpec((tk, 512), lambda b, i, kk: (b * tpb_k + kk, 0)),
                  pl.BlockSpec((tk, 256), lambda b, i, kk: (b * tpb_k + kk, 0))],
        out_specs=pl.BlockSpec((tq, 256), lambda b, i, kk: (b * nq + i, 0)),
        out_shape=jax.ShapeDtypeStruct((nb * q_len, 256), F32),
        scratch_shapes=[pltpu.VMEM((MLA_HEADS, tq, 1), F32), pltpu.VMEM((MLA_HEADS, tq, 1), F32),
                        pltpu.VMEM((MLA_HEADS * tq, 256), F32)],
        compiler_params=_cparams(("parallel", "parallel", "arbitrary")),
        name="flash_mla",
    )(q, k, v)


def _diff_prep(p, cos_tab, sin_tab, *, nb, tpb):
    n = p.shape[0]
    tok = lambda b, r: (b * tpb + r, 0)
    pos = lambda b, r: (r, 0)
    return pl.pallas_call(
        _diff_prep_kernel_cols,
        grid=(nb, tpb),
        in_specs=[pl.BlockSpec((TM, 256), lambda b, r, c=c: (b * tpb + r, EXT_DIFF // 256 + c))
                  for c in range(5)]
                 + [pl.BlockSpec((TM, 256), pos), pl.BlockSpec((TM, 256), pos)],
        out_specs=[pl.BlockSpec((TM, 256), tok)] * 3,
        out_shape=[jax.ShapeDtypeStruct((n, 256), BF16)] * 3,
        compiler_params=_cparams(("parallel", "parallel")),
        name="diff_prep",
    )(p, p, p, p, p, cos_tab, sin_tab)


def _diff_prep_kernel_cols(q_in, k_in, v_in, qr_in, kr_in, cos_ref, sin_ref, q_ref, k_ref, v_ref):
    cos, sin = cos_ref[...], sin_ref[...]
    scale = DIFF_HD ** -0.5 * LOG2E
    q_ref[...] = ((q_in[...] * cos + qr_in[...] * sin) * scale).astype(BF16)
    k_ref[...] = (k_in[...] * cos + kr_in[...] * sin).astype(BF16)
    v_ref[...] = v_in[...].astype(BF16)


def _flash_diff_kernel(q_ref, k_ref, v_ref, lam_ref, g_ref, o_ref, qs_ref, m_ref, l_ref, acc_ref, *, lam_init):
    kk = pl.program_id(2)
    tq = q_ref.shape[0]
    nsm = 2 * DIFF_HEADS

    @pl.when(kk == 0)
    def _():
        m_ref[...] = jnp.full(m_ref.shape, -jnp.inf, F32)
        l_ref[...] = jnp.zeros(l_ref.shape, F32)
        acc_ref[...] = jnp.zeros(acc_ref.shape, F32)
        q = q_ref[...]
        lane = lax.broadcasted_iota(jnp.int32, (tq, 256), 1)
        for i in range(nsm):
            qs_ref[i * tq:(i + 1) * tq, :] = jnp.where((lane // DIFF_HD) == i, q, jnp.zeros_like(q))

    k = k_ref[...]
    ps, alphas = [], []
    for i in range(nsm):
        rows = slice(i * tq, (i + 1) * tq)
        s = _dot_nt(qs_ref[rows, :], k)
        m_prev = m_ref[rows, :]
        m_new = jnp.maximum(m_prev, jnp.max(s, axis=-1, keepdims=True))
        alpha = jnp.exp2(m_prev - m_new)
        p = jnp.exp2(s - m_new)
        l_ref[rows, :] = alpha * l_ref[rows, :] + jnp.sum(p, axis=-1, keepdims=True)
        m_ref[rows, :] = m_new
        alphas.append(alpha)
        ps.append(p.astype(BF16))
    pv = _dot(jnp.concatenate(ps, axis=0), v_ref[...])
    acc_ref[...] = jnp.concatenate(alphas, axis=0) * acc_ref[...] + pv

    @pl.when(kk == pl.num_programs(2) - 1)
    def _():
        lane = lax.broadcasted_iota(jnp.int32, (tq, 256), 1)
        o = jnp.zeros((tq, 256), F32)
        for h in range(DIFF_HEADS):
            r0, r1 = (2 * h) * tq, (2 * h + 1) * tq
            o0 = acc_ref[r0:r0 + tq, :] / l_ref[r0:r0 + tq, :]
            o1 = acc_ref[r1:r1 + tq, :] / l_ref[r1:r1 + tq, :]
            o = jnp.where((lane // (2 * DIFF_HD)) == h, o0 - lam_ref[...] * o1, o)
        ms = _split_dot(o * o, _block_ones(256, 2 * DIFF_HD)) * (1.0 / (2 * DIFF_HD))
        o_ref[...] = o * lax.rsqrt(ms + DIFF_EPS) * g_ref[...] * (1.0 - lam_init)


def _flash_diff(q, k, v, lam_row, g_row, *, lam_init, nb, t_all, q_start, q_len, k_len, tk):
    tq = TM
    tpb_q, tpb_k = t_all // tq, t_all // tk
    q0 = q_start // tq
    nq = q_len // tq
    return pl.pallas_call(
        functools.partial(_flash_diff_kernel, lam_init=lam_init),
        grid=(nb, nq, k_len // tk),
        in_specs=[pl.BlockSpec((tq, 256), lambda b, i, kk: (b * tpb_q + q0 + i, 0)),
                  pl.BlockSpec((tk, 256), lambda b, i, kk: (b * tpb_k + kk, 0)),
                  pl.BlockSpec((tk, 256), lambda b, i, kk: (b * tpb_k + kk, 0)),
                  pl.BlockSpec((1, 256), lambda b, i, kk: (0, 0)),
                  pl.BlockSpec((1, 256), lambda b, i, kk: (0, 0))],
        out_specs=pl.BlockSpec((tq, 256), lambda b, i, kk: (b * nq + i, 0)),
        out_shape=jax.ShapeDtypeStruct((nb * q_len, 256), F32),
        scratch_shapes=[pltpu.VMEM((2 * DIFF_HEADS * tq, 256), BF16),
                        pltpu.VMEM((2 * DIFF_HEADS * tq, 1), F32),
                        pltpu.VMEM((2 * DIFF_HEADS * tq, 1), F32),
                        pltpu.VMEM((2 * DIFF_HEADS * tq, 256), F32)],
        compiler_params=_cparams(("parallel", "parallel", "arbitrary")),
        name="flash_diff",
    )(q, k, v, lam_row, g_row)


def _rw_prep_kernel(p_ref, prev_ref, next_ref, mu_ref, wlo_ref, g2_ref, vec_ref,
                    r_ref, v_ref, kk_ref, w0_ref, k0_ref, b0_ref, w1_ref, k1_ref, b1_ref,
                    bonus_ref, gate_ref, *, r0, lat_last):
    r = pl.program_id(1) + r0
    p = p_ref[...]
    row = lax.broadcasted_iota(jnp.int32, p.shape, 0)
    first_tile = jnp.logical_or(r == 0, r == 1)
    last_tile = jnp.logical_or(r == 0, r == lat_last)
    prev_row = jnp.where(first_tile, 0.0, prev_ref[7:8, :])
    next_row = jnp.where(last_tile, 0.0, next_ref[0:1, :])
    up = jnp.where(row == 0, prev_row, pltpu.roll(p, 1, 0))
    dn = jnp.where(row == TM - 1, next_row, pltpu.roll(p, TM - 1, 0))
    z = p + (0.5 * (up + dn) - p) * mu_ref[...]

    hw = RW_HEADS * RW_HEAD
    rr, k, v = z[:, :hw], z[:, hw:2 * hw], z[:, 2 * hw:3 * hw]
    lo = z[:, 3 * hw:3 * hw + 128]
    lane = lax.broadcasted_iota(jnp.int32, lo.shape, 1)
    lo = jnp.where(lane < 64, jnp.tanh(lo), lo).astype(BF16)
    wa = _dot(lo, wlo_ref[...])
    gate_ref[...] = _dot(_sigmoid(z[:, 3 * hw + 128:]).astype(BF16), g2_ref[...])

    e4 = _block_ones(hw, RW_HEAD)
    k_k, k_a, r_k = vec_ref[0:1, :], vec_ref[1:2, :], vec_ref[2:3, :]
    kk = k * k_k
    nrm = jnp.maximum(jnp.sqrt(_split_dot(kk * kk, e4)), 1e-12)
    kk = kk / nrm
    r_ref[...] = rr
    v_ref[...] = v
    kk_ref[...] = kk
    ksum = jnp.zeros_like(k)
    for d, (w_ref, kd_ref, b_ref) in enumerate(((w0_ref, k0_ref, b0_ref), (w1_ref, k1_ref, b1_ref))):
        w0 = vec_ref[3 + d:4 + d, :]
        a0 = vec_ref[5 + d:6 + d, :]
        wd = -_softplus(-(w0 + wa[:, d * hw:(d + 1) * hw])) - 0.5
        w_ref[...] = jnp.exp(-jnp.exp(wd))
        ad = _sigmoid(a0 + wa[:, (2 + d) * hw:(3 + d) * hw])
        kd = k * (1.0 + (ad - 1.0) * k_a)
        kd_ref[...] = kd
        b_ref[...] = kk * ad
        ksum = ksum + kd
    bonus_ref[...] = _split_dot(rr * ksum * r_k, e4) * v


def _rw_prep(p, mu, wlo, g2, vecs, *, nb, tpb, r0, nt):
    n_out = nb * nt * TM
    hw = RW_HEADS * RW_HEAD
    n_rows8 = p.shape[0] // 8
    tok = lambda b, r: (b * tpb + r0 + r, EXT_RW // 1024)
    prev = lambda b, r: (jnp.maximum((b * tpb + r0 + r) * (TM // 8) - 1, 0), EXT_RW // 1024)
    nxt = lambda b, r: (jnp.minimum((b * tpb + r0 + r + 1) * (TM // 8), n_rows8 - 1), EXT_RW // 1024)
    out = lambda b, r: (b * nt + r, 0)
    full = lambda b, r: (0, 0)
    return pl.pallas_call(
        functools.partial(_rw_prep_kernel, r0=r0, lat_last=tpb - 1),
        grid=(nb, nt),
        in_specs=[pl.BlockSpec((TM, 1024), tok),
                  pl.BlockSpec((8, 1024), prev),
                  pl.BlockSpec((8, 1024), nxt),
                  pl.BlockSpec((1, 1024), full),
                  pl.BlockSpec((128, 4 * hw), full),
                  pl.BlockSpec((128, hw), full),
                  pl.BlockSpec((8, hw), full)],
        out_specs=[pl.BlockSpec((TM, hw), out)] * 11,
        out_shape=[jax.ShapeDtypeStruct((n_out, hw), F32)] * 11,
        compiler_params=_cparams(("parallel", "parallel")),
        name="rw_prep",
    )(p, p, p, mu, wlo, g2, vecs)


def _rw_scan_kernel(rf, vf, kkf, wf, kf, bf, rb, vb, kkb, wb, kb, bb, yf_ref, yb_ref, s_ref, *, nb):
    c = RW_CHUNK

    @pl.when(pl.program_id(0) == 0)
    def _():
        s_ref[...] = jnp.zeros(s_ref.shape, F32)

    e2 = _block_ones(128, RW_HEAD)
    z2 = jnp.zeros((128, 128), BF16)
    rhs_pair = jnp.concatenate([jnp.concatenate([e2, z2], axis=1), jnp.concatenate([z2, e2], axis=1)], axis=0)
    e22 = jnp.concatenate([e2, e2], axis=0)
    lane = lax.broadcasted_iota(jnp.int32, (RW_HEAD, 128), 1)
    sub = lax.broadcasted_iota(jnp.int32, (RW_HEAD, 128), 0)
    diag = (lane % RW_HEAD) == sub
    sub8 = lax.broadcasted_iota(jnp.int32, (8, 128), 0)
    dirs = ((rf, vf, kkf, wf, kf, bf, yf_ref), (rb, vb, kkb, wb, kb, bb, yb_ref))

    def allreduce_rows(x):
        t = x[0:8]
        for i in range(1, 8):
            t = t + x[8 * i:8 * i + 8]
        for sh in (4, 2, 1):
            t = t + pltpu.roll(t, sh, 0)
        return t

    def col(row):
        return jnp.where(diag, row, 0.0).astype(BF16)

    def group(g, carry):
        tiles, ytiles = {}, {}
        for d, refs in enumerate(dirs):
            base = pl.multiple_of((g if d == 0 else c // 8 - 1 - g) * 8, 8)
            if d == 0:
                prev = lambda x, sh: jnp.where(sub8 >= sh, pltpu.roll(x, sh, 0), 1.0)
            else:
                prev = lambda x, sh: jnp.where(sub8 < 8 - sh, pltpu.roll(x, 8 - sh, 0), 1.0)
            last = 7 if d == 0 else 0
            for b in range(nb):
                for hp in range(2):
                    r_, v_, kk_, w_, k_, b_ = [ref[b, pl.ds(base, 8), pl.ds(hp * 128, 128)] for ref in refs[:6]]
                    gam = w_
                    for sh in (1, 2, 4):
                        gam = gam * prev(gam, sh)
                    inv = 1.0 / gam
                    tiles[d, b, hp] = (base, v_, b_ * inv, kk_ * prev(gam, 1), k_ * inv, r_ * gam,
                                       gam[last:last + 1, :])
                    ytiles[d, b, hp] = jnp.zeros((8, 128), F32)
        units = [(d, b, hp) for d in range(2) for b in range(nb) for hp in range(2)]
        for jj in range(8):
            lhs = []
            for (d, b, hp) in units:
                rw = slice(jj, jj + 1) if d == 0 else slice(7 - jj, 8 - jj)
                _, _, bh, kkh, kh, rh, _ = tiles[d, b, hp]
                lhs.append(jnp.concatenate([col(bh[rw]), col(kkh[rw])], axis=1))
                lhs.append(jnp.concatenate([col(kh[rw]), col(rh[rw])], axis=1))
            cm = _dot(jnp.concatenate(lhs, axis=0), rhs_pair)
            for u, (d, b, hp) in enumerate(units):
                j = jj if d == 0 else 7 - jj
                v_ = tiles[d, b, hp][1]
                r0 = 2 * u * RW_HEAD
                bc, kkc = cm[r0:r0 + RW_HEAD, :128], cm[r0:r0 + RW_HEAD, 128:]
                kc, rc = cm[r0 + RW_HEAD:r0 + 2 * RW_HEAD, :128], cm[r0 + RW_HEAD:r0 + 2 * RW_HEAD, 128:]
                s = s_ref[u]
                sa = jnp.concatenate([allreduce_rows(kkc * s)] * 8, axis=0)
                s = s - bc * sa + kc * v_[j:j + 1]
                s_ref[u] = s
                ytiles[d, b, hp] = jnp.where(sub8 == j, allreduce_rows(rc * s), ytiles[d, b, hp])
        for d, refs in enumerate(dirs):
            for b in range(nb):
                for hp in range(2):
                    u = (d * nb + b) * 2 + hp
                    gl = tiles[d, b, hp][6]
                    gh = gl.astype(BF16).astype(F32)
                    s_ref[u] = s_ref[u] * _dot(jnp.concatenate([col(gh), col(gl - gh)], axis=1), e22)
                    refs[6][b, pl.ds(tiles[d, b, hp][0], 8), pl.ds(hp * 128, 128)] = ytiles[d, b, hp]
        return carry

    lax.fori_loop(0, c // 8, group, 0)


def _rw_scan(r, v, kk, w0, k0, b0, w1, k1, b1, *, nb, t_all, n_ctx):
    c = RW_CHUNK
    nc, ncc = t_all // c, n_ctx // c
    hw = RW_HEADS * RW_HEAD
    shp = lambda a: a.reshape(nb, t_all, hw)
    fwd = lambda j: (0, j, 0)
    bwd = lambda j: (0, jnp.where(j < ncc, ncc - 1 - j, nc - 1 - (j - ncc)), 0)
    blk = (nb, c, hw)
    yshape = jax.ShapeDtypeStruct((nb, t_all, hw), F32)
    yf, yb = pl.pallas_call(
        functools.partial(_rw_scan_kernel, nb=nb),
        grid=(nc,),
        in_specs=[pl.BlockSpec(blk, fwd)] * 6 + [pl.BlockSpec(blk, bwd)] * 6,
        out_specs=[pl.BlockSpec(blk, fwd), pl.BlockSpec(blk, bwd)],
        out_shape=[yshape, yshape],
        scratch_shapes=[pltpu.VMEM((2 * nb * 2, RW_HEAD, 128), F32)],
        compiler_params=_cparams(("arbitrary",)),
        name="rw_scan",
    )(shp(r), shp(v), shp(kk), shp(w0), shp(k0), shp(b0),
      shp(r), shp(v), shp(kk), shp(w1), shp(k1), shp(b1))

    return yf.reshape(nb * t_all, hw), yb.reshape(nb * t_all, hw)


def _rw_post_kernel(yf_ref, yb_ref, bonus_ref, gate_ref, vec_ref, o_ref):
    e4 = _block_ones(RW_HEADS * RW_HEAD, RW_HEAD)
    y = yf_ref[...] + yb_ref[...]
    mean = _split_dot(y, e4) * (1.0 / RW_HEAD)
    yc = y - mean
    var = _split_dot(yc * yc, e4) * (1.0 / RW_HEAD)
    yn = yc * lax.rsqrt(var + RW_LN_EPS) * vec_ref[0:1, :] + vec_ref[1:2, :]
    o_ref[...] = (yn + bonus_ref[...]) * gate_ref[...]


def _rw_post(yf, yb, bonus, gate, vecs, *, nb, tpb_y, r0_y, nt):
    hw = RW_HEADS * RW_HEAD
    n_out = bonus.shape[0]
    ytok = lambda b, r: (b * tpb_y + r0_y + r, 0)
    tok = lambda b, r: (b * nt + r, 0)
    return pl.pallas_call(
        _rw_post_kernel,
        grid=(nb, nt),
        in_specs=[pl.BlockSpec((TM, hw), ytok), pl.BlockSpec((TM, hw), ytok),
                  pl.BlockSpec((TM, hw), tok), pl.BlockSpec((TM, hw), tok),
                  pl.BlockSpec((8, hw), lambda b, r: (0, 0))],
        out_specs=pl.BlockSpec((TM, hw), tok),
        out_shape=jax.ShapeDtypeStruct((n_out, hw), F32),
        compiler_params=_cparams(("parallel", "parallel")),
        name="rw_post",
    )(yf, yb, bonus, gate, vecs)


def _s5_scan_kernel(uf_ref, ub_ref, wb_ref, ab_ref, cf_ref, cb_ref, yf_ref, yb_ref, x_ref, st_ref, *, nb):
    c = S5_CHUNK
    nst = S5_GROUPS * S5_STATE

    @pl.when(pl.program_id(0) == 0)
    def _():
        st_ref[...] = jnp.zeros(st_ref.shape, F32)

    dirs = ((uf_ref, cf_ref, yf_ref), (ub_ref, cb_ref, yb_ref))
    for d, (u_ref, _, _) in enumerate(dirs):
        for b in range(nb):
            x_ref[d, b] = _dot(u_ref[b].astype(BF16), wb_ref[:, d * 2 * nst:(d + 1) * 2 * nst])

    def group(g, carry):
        for d in range(2):
            base = pl.multiple_of((g if d == 0 else c // 8 - 1 - g) * 8, 8)
            ar = ab_ref[d, :, 0:nst]
            ai = ab_ref[d, :, nst:2 * nst]
            for b in range(nb):
                u = d * nb + b
                xr = st_ref[u, :, 0:nst]
                xi = st_ref[u, :, nst:2 * nst]
                bur = x_ref[d, b, pl.ds(base, 8), 0:nst]
                bui = x_ref[d, b, pl.ds(base, 8), nst:2 * nst]
                rows_r, rows_i = [None] * 8, [None] * 8
                for jj in range(8):
                    j = jj if d == 0 else 7 - jj
                    xr, xi = (ar * xr - ai * xi + bur[j:j + 1, :], ar * xi + ai * xr + bui[j:j + 1, :])
                    rows_r[j], rows_i[j] = xr, xi
                st_ref[u, :, 0:nst] = xr
                st_ref[u, :, nst:2 * nst] = xi
                x_ref[d, b, pl.ds(base, 8), 0:nst] = jnp.concatenate(rows_r, axis=0)
                x_ref[d, b, pl.ds(base, 8), nst:2 * nst] = jnp.concatenate(rows_i, axis=0)
        return carry

    lax.fori_loop(0, c // 8, group, 0)

    for d, (_, c_ref, y_ref) in enumerate(dirs):
        for b in range(nb):
            y_ref[b] = _dot(x_ref[d, b].astype(BF16), c_ref[...])


def _s5_scan(p, w_b, ab, cf, cb, *, nb, t_all, n_ctx):
    c = S5_CHUNK
    nc, ncc = t_all // c, n_ctx // c
    nst2 = 2 * S5_GROUPS * S5_STATE
    width = S5_GROUPS * S5_GROUP_CH
    p3 = p.reshape(nb, t_all, N_EXT)
    fwd = lambda j: j
    bwd = lambda j: jnp.where(j < ncc, ncc - 1 - j, nc - 1 - (j - ncc))
    yshape = jax.ShapeDtypeStruct((nb, t_all, width), F32)
    full = lambda j: (0, 0)
    yf, yb = pl.pallas_call(
        functools.partial(_s5_scan_kernel, nb=nb),
        grid=(nc,),
        in_specs=[pl.BlockSpec((nb, c, width), lambda j: (0, fwd(j), EXT_S5 // width)),
                  pl.BlockSpec((nb, c, width), lambda j: (0, bwd(j), EXT_S5 // width)),
                  pl.BlockSpec((width, 2 * nst2), full),
                  pl.BlockSpec((2, 1, nst2), lambda j: (0, 0, 0)),
                  pl.BlockSpec((nst2, width), full), pl.BlockSpec((nst2, width), full)],
        out_specs=[pl.BlockSpec((nb, c, width), lambda j: (0, fwd(j), 0)),
                   pl.BlockSpec((nb, c, width), lambda j: (0, bwd(j), 0))],
        out_shape=[yshape, yshape],
        scratch_shapes=[pltpu.VMEM((2, nb, c, nst2), F32), pltpu.VMEM((2 * nb, 1, nst2), F32)],
        compiler_params=_cparams(("arbitrary",)),
        name="s5_scan",
    )(p3, p3, w_b, ab, cf, cb)
    return yf.reshape(nb * t_all, width), yb.reshape(nb * t_all, width)


def _s5_post_kernel(yf_ref, yb_ref, u_ref, d_ref, gw_ref, gb_ref, o_ref):
    y = yf_ref[...] + yb_ref[...] + d_ref[...] * u_ref[...]
    zg = 0.5 * y * (1.0 + jnp.tanh(math.sqrt(2.0 / math.pi) * (y + 0.044715 * (y * y * y))))
    o_ref[...] = zg * _sigmoid(_dot(zg.astype(BF16), gw_ref[...]) + gb_ref[...])


def _s5_post(yf, yb, p, d_row, glu_w, glu_b, *, nb, tpb, r0, nt):
    tok = lambda b, r: (b * tpb + r0 + r, 0)
    full = lambda b, r: (0, 0)
    return pl.pallas_call(
        _s5_post_kernel,
        grid=(nb, nt),
        in_specs=[pl.BlockSpec((TM, 256), tok), pl.BlockSpec((TM, 256), tok),
                  pl.BlockSpec((TM, 256), lambda b, r: (b * tpb + r0 + r, EXT_S5 // 256)),
                  pl.BlockSpec((1, 256), full), pl.BlockSpec((256, 256), full),
                  pl.BlockSpec((1, 256), full)],
        out_specs=pl.BlockSpec((TM, 256), lambda b, r: (b * nt + r, 0)),
        out_shape=jax.ShapeDtypeStruct((nb * nt * TM, 256), F32),
        compiler_params=_cparams(("parallel", "parallel")),
        name="s5_post",
    )(yf, yb, p, d_row, glu_w, glu_b)


def _merge_kernel(ya_ref, yb_ref, ys_ref, yd_ref, g0, g1, g2, g3, h_ref, m2_ref, wb_ref, wo_ref, o_ref):
    acc = None
    for n, (y_ref, g_ref) in enumerate(((ya_ref, g0), (yb_ref, g1), (ys_ref, g2), (yd_ref, g3))):
        term = _sigmoid(g_ref[...]) * _dot(y_ref[...].astype(BF16), wb_ref[n])
        acc = term if acc is None else acc + term
    o_ref[...] = h_ref[...] + m2_ref[0] * _dot(acc.astype(BF16), wo_ref[...])


def _merge(ya, yb, ys, yd, p, h, mod2, w_branch, w_out, *, nb, tpb, r0, nt):
    tok_in = lambda b, r: (b * tpb + r0 + r, 0)
    tok_out = lambda b, r: (b * nt + r, 0)
    full2 = lambda b, r: (0, 0)
    gate = [pl.BlockSpec((TM, D_MODEL), lambda b, r, c=c: (b * tpb + r0 + r, EXT_GATE // D_MODEL + c))
            for c in range(4)]
    return pl.pallas_call(
        _merge_kernel,
        grid=(nb, nt),
        in_specs=[pl.BlockSpec((TM, 256), tok_out)] * 4 + gate
                 + [pl.BlockSpec((TM, D_MODEL), tok_in),
                    pl.BlockSpec((1, 1, D_MODEL), lambda b, r: (jnp.where(r0 + r == 0, nb, b), 0, 0)),
                    pl.BlockSpec((4, 256, D_MODEL), lambda b, r: (0, 0, 0)),
                    pl.BlockSpec((D_MODEL, D_MODEL), full2)],
        out_specs=pl.BlockSpec((TM, D_MODEL), tok_out),
        out_shape=jax.ShapeDtypeStruct((nb * nt * TM, D_MODEL), F32),
        compiler_params=_cparams(("parallel", "parallel")),
        name="merge",
    )(ya, yb, ys, yd, p, p, p, p, h, mod2, w_branch, w_out)


def _router_kernel(h_ref, g_ref, sh_ref, sc_ref, wh_ref, wm_ref, b_ref, x_ref, lg_ref):
    x = _rms(h_ref[...], g_ref[...], NORM_EPS) * (1.0 + sc_ref[0]) + sh_ref[0]
    xh = x.astype(BF16)
    xm = (x - xh.astype(F32)).astype(BF16)
    bits = lax.bitcast_convert_type(xh.astype(F32), jnp.uint32)
    half = D_MODEL // 2
    x_ref[...] = (bits[:, :half] >> 16) | (bits[:, half:] & jnp.uint32(0xFFFF0000))
    lg = (_dot(xh, wh_ref[...]) + _dot(xm, wh_ref[...]) + _dot(xh, wm_ref[...])) + b_ref[...]

    lane = lax.broadcasted_iota(jnp.int32, lg.shape, 1)
    lanef = lane.astype(F32)
    neg = jnp.float32(-jnp.inf)
    big = jnp.float32(1e9)
    rmax = lambda v: jnp.max(v, axis=-1, keepdims=True)
    rmin = lambda v: jnp.min(v, axis=-1, keepdims=True)
    rsum = lambda v: jnp.sum(v, axis=-1, keepdims=True)

    gmask = lane < MOE_GROUPS
    mg = rmax(jnp.where(gmask, lg, neg))
    eg = jnp.where(gmask, jnp.exp(lg - mg), 0.0)
    pg = eg / rsum(eg)
    pg_top = rmax(pg)
    g_sel = rmin(jnp.where(jnp.logical_and(gmask, pg == pg_top), lanef, big))
    lo = MOE_GROUPS + MOE_PER_GROUP * g_sel
    emask = jnp.logical_and(lanef >= lo, lanef < lo + MOE_PER_GROUP)
    me = rmax(jnp.where(emask, lg, neg))
    ee = jnp.where(emask, jnp.exp(lg - me), 0.0)
    pe = jnp.where(emask, ee / rsum(ee), -1.0)
    p1 = rmax(pe)
    i1 = rmin(jnp.where(pe == p1, lanef, big))
    pe2 = jnp.where(lanef == i1, -1.0, pe)
    p2 = rmax(pe2)
    i2 = rmin(jnp.where(pe2 == p2, lanef, big))
    den = p1 + p2
    out = jnp.where(lane == 0, i1 - MOE_GROUPS, 0.0)
    out = jnp.where(lane == 1, i2 - MOE_GROUPS, out)
    out = jnp.where(lane == 2, pg_top * p1 / den, out)
    out = jnp.where(lane == 3, pg_top * p2 / den, out)
    lg_ref[...] = out


def _router(h, g, shift, scale, wh, wm, bias, *, nb, nt, ctx_first):
    tok = lambda b, r: (b * nt + r, 0)
    full = lambda b, r: (0, 0)
    if ctx_first:
        modmap = lambda b, r: (jnp.where(r == 0, nb, b), 0, 0)
    else:
        modmap = lambda b, r: (b, 0, 0)
    n = h.shape[0]
    return pl.pallas_call(
        _router_kernel,
        grid=(nb, nt),
        in_specs=[pl.BlockSpec((TM, D_MODEL), tok), pl.BlockSpec((1, D_MODEL), full),
                  pl.BlockSpec((1, 1, D_MODEL), modmap), pl.BlockSpec((1, 1, D_MODEL), modmap),
                  pl.BlockSpec((D_MODEL, 128), full), pl.BlockSpec((D_MODEL, 128), full),
                  pl.BlockSpec((1, 128), full)],
        out_specs=[pl.BlockSpec((TM, D_MODEL // 2), tok), pl.BlockSpec((TM, 128), tok)],
        out_shape=[jax.ShapeDtypeStruct((n, D_MODEL // 2), jnp.uint32), jax.ShapeDtypeStruct((n, 128), F32)],
        compiler_params=_cparams(("parallel", "parallel")),
        name="router",
    )(h, g, shift, scale, wh, wm, bias)


def _expert_kernel(be_ref, nv_ref, x_ref, wg_ref, wu_ref, wd_ref, o_ref, wgb_ref, wub_ref, wdb_ref):
    i = pl.program_id(0)

    @pl.when(jnp.logical_or(i == 0, be_ref[i] != be_ref[jnp.maximum(i - 1, 0)]))
    def _():
        wgb_ref[...] = wg_ref[0].astype(BF16)
        wub_ref[...] = wu_ref[0].astype(BF16)
        wdb_ref[...] = wd_ref[0].astype(BF16)

    @pl.when(i < nv_ref[0])
    def _():
        u = x_ref[...]
        x = jnp.concatenate([lax.bitcast_convert_type(u << 16, F32),
                             lax.bitcast_convert_type(u & jnp.uint32(0xFFFF0000), F32)], axis=1).astype(BF16)
        hb = _silu(_dot(x, wgb_ref[...])) * _dot(x, wub_ref[...])
        o_ref[...] = _dot(hb.astype(BF16), wdb_ref[...])

    @pl.when(i >= nv_ref[0])
    def _():
        o_ref[...] = jnp.zeros(o_ref.shape, F32)


def _experts(xs, block_e, n_valid, w_gate, w_up, w_down):
    n_slots = xs.shape[0]
    n_blocks = n_slots // MOE_BLK
    wmap = lambda i, be, nv: (be[i], 0, 0)
    return pl.pallas_call(
        _expert_kernel,
        grid_spec=pltpu.PrefetchScalarGridSpec(
            num_scalar_prefetch=2,
            grid=(n_blocks,),
            in_specs=[pl.BlockSpec((MOE_BLK, D_MODEL // 2), lambda i, be, nv: (i, 0)),
                      pl.BlockSpec((1, D_MODEL, D_EXPERT), wmap),
                      pl.BlockSpec((1, D_MODEL, D_EXPERT), wmap),
                      pl.BlockSpec((1, D_EXPERT, D_MODEL), wmap)],
            out_specs=pl.BlockSpec((MOE_BLK, D_MODEL), lambda i, be, nv: (i, 0)),
            scratch_shapes=[pltpu.VMEM((D_MODEL, D_EXPERT), BF16), pltpu.VMEM((D_MODEL, D_EXPERT), BF16),
                            pltpu.VMEM((D_EXPERT, D_MODEL), BF16)]),
        out_shape=jax.ShapeDtypeStruct((n_slots, D_MODEL), F32),
        compiler_params=_cparams(("arbitrary",)),
        name="experts",
    )(block_e, n_valid, xs, w_gate, w_up, w_down)


def _combine_kernel(h_ref, y01_ref, w_ref, m5_ref, g_ref, o_ref, *, final):
    w = w_ref[...]
    y = y01_ref[:, :D_MODEL] * w[:, 0:1] + y01_ref[:, D_MODEL:] * w[:, 1:2]
    h = h_ref[...] + m5_ref[0] * y
    if final:
        h = _rms(h, g_ref[...], NORM_EPS)
    o_ref[...] = h


def _combine(h, y01, wts, mod5, g_final, *, nb, nt, ctx_first, final):
    tok = lambda b, r: (b * nt + r, 0)
    if ctx_first:
        modmap = lambda b, r: (jnp.where(r == 0, nb, b), 0, 0)
    else:
        modmap = lambda b, r: (b, 0, 0)
    return pl.pallas_call(
        functools.partial(_combine_kernel, final=final),
        grid=(nb, nt),
        in_specs=[pl.BlockSpec((TM, D_MODEL), tok), pl.BlockSpec((TM, 2 * D_MODEL), tok),
                  pl.BlockSpec((TM, 128), tok), pl.BlockSpec((1, 1, D_MODEL), modmap),
                  pl.BlockSpec((1, D_MODEL), lambda b, r: (0, 0))],
        out_specs=pl.BlockSpec((TM, D_MODEL), tok),
        out_shape=jax.ShapeDtypeStruct(h.shape, F32),
        compiler_params=_cparams(("parallel", "parallel")),
        name="combine",
    )(h, y01, wts, mod5, g_final)


def _sc_gather(table, idx):
    n, d = idx.shape[0], table.shape[1]
    steps = n // SC_ROWS_PER_STEP
    idx_p = jnp.pad(idx.reshape(steps, SC_ROWS_PER_STEP), ((0, 0), (0, SC_INDEX_TILE - SC_ROWS_PER_STEP)))
    mesh = plsc.VectorSubcoreMesh(core_axis_name="core", subcore_axis_name="subcore")

    @pl.kernel(out_type=jax.ShapeDtypeStruct((n, d), table.dtype), mesh=mesh)
    def gather_kernel(x_hbm, i_hbm, o_hbm):
        def body(i_vmem, o_vmem):
            pltpu.sync_copy(x_hbm.at[i_vmem.at[0, pl.ds(0, SC_ROWS_PER_STEP)]], o_vmem)

        pltpu.emit_pipeline(
            body,
            grid=(steps,),
            in_specs=[pl.BlockSpec((1, SC_INDEX_TILE), index_map=lambda i: (i, 0))],
            out_specs=[pl.BlockSpec((SC_ROWS_PER_STEP, d), index_map=lambda i: (i, 0))],
            core_axis_name=("core", "subcore"),
            dimension_semantics=(pltpu.PARALLEL,),
        )(i_hbm, o_hbm)

    return gather_kernel(table, idx_p)


def _moe(h, g2, shift, scale, mod5, wh, wm, rbias, w_gate, w_up, w_down, g_final, *, layer, nb, nt, ctx_first,
         final):
    n = h.shape[0]
    x_bf, route = _router(h, g2, shift, scale, wh, wm, rbias, nb=nb, nt=nt, ctx_first=ctx_first)
    idx = route[:, :MOE_TOPK].astype(jnp.int32)
    wts = route[:, MOE_TOPK:2 * MOE_TOPK]
    n_as = n * MOE_TOPK
    flat_e = idx.reshape(n_as)
    onehot = (flat_e[:, None] == jnp.arange(MOE_EXPERTS, dtype=jnp.int32)[None, :]).astype(jnp.int32)
    csum = jnp.cumsum(onehot, axis=0)
    counts = csum[-1]
    rank = jnp.sum(jnp.where(onehot > 0, csum - 1, 0), axis=1)
    padded = (counts + MOE_BLK - 1) // MOE_BLK * MOE_BLK
    pad_end = jnp.cumsum(padded)
    pad_start = pad_end - padded
    slot = pad_start[flat_e] + rank
    n_blocks = (n_as + MOE_EXPERTS * (MOE_BLK - 1) + MOE_BLK - 1) // MOE_BLK
    n_slots = n_blocks * MOE_BLK
    slot_tok = jnp.zeros((n_slots,), jnp.int32).at[slot].set(jnp.arange(n_as, dtype=jnp.int32) // MOE_TOPK,
                                                             unique_indices=True)
    starts = jnp.arange(n_blocks, dtype=jnp.int32) * MOE_BLK
    block_e = jnp.minimum(jnp.sum((pad_end[None, :] <= starts[:, None]).astype(jnp.int32), axis=1),
                          MOE_EXPERTS - 1)
    n_valid = (pad_end[-1:] // MOE_BLK).astype(jnp.int32)
    xs = _sc_gather(x_bf, slot_tok)
    ys = _experts(xs, block_e + layer * MOE_EXPERTS, n_valid, w_gate, w_up, w_down)
    y01 = _sc_gather(ys, slot).reshape(n, MOE_TOPK * D_MODEL)
    wts_pad = jnp.pad(wts.astype(F32), ((0, 0), (0, 128 - MOE_TOPK)))
    return _combine(h, y01, wts_pad, mod5, g_final, nb=nb, nt=nt, ctx_first=ctx_first, final=final)


_ROT_SRC = np.array(list(range(8, 16)) + list(range(0, 8)) + list(range(24, 32)) + list(range(16, 24)))
_ROT_SIGN = np.array([-1.0] * 8 + [1.0] * 8 + [-1.0] * 8 + [1.0] * 8, np.float32)


def _rot_cols(w):
    k = w.shape[-1] // ROPE_DIM
    src = np.concatenate([_ROT_SRC + ROPE_DIM * i for i in range(k)])
    sign = np.tile(_ROT_SIGN, k)
    return w[..., src] * sign


def _rope_tables(n_ctx, n_lat):
    rows = n_lat // GRID_W
    row = jnp.repeat(jnp.arange(rows, dtype=F32), GRID_W)
    col = (jnp.arange(rows * GRID_W) % GRID_W).astype(F32)
    nf = ROPE_DIM // 4
    inv = ROPE_BASE ** (-jnp.arange(nf, dtype=F32) / nf)
    ar = row[:, None] * inv
    ac = col[:, None] * inv
    ang = jnp.concatenate([ar, ar, ac, ac], axis=-1)
    cos = jnp.concatenate([jnp.ones((n_ctx, ROPE_DIM), F32), jnp.cos(ang)], axis=0)
    sin = jnp.concatenate([jnp.zeros((n_ctx, ROPE_DIM), F32), jnp.sin(ang)], axis=0)
    return cos, sin


def _block_diag(blocks):
    g, a, b = blocks.shape
    tiled = jnp.tile(blocks.reshape(g * a, b), (1, g))
    rows = lax.broadcasted_iota(jnp.int32, (g * a, g * b), 0) // a
    cols = lax.broadcasted_iota(jnp.int32, (g * a, g * b), 1) // b
    return jnp.where(rows == cols, tiled, 0.0)


def _pick_tk(t_all):
    best = 128
    for tk in range(128, ATTN_TK_MAX + 1, 128):
        if t_all % tk == 0:
            best = tk
    return best


def kernel(x, c, ctx, c_ctx, w_mod, b_mod, norm1_g, norm2_g, w_in, mla_q_norm_g, mla_kv_norm_g, mla_w_uq, mla_w_ukv, rw_mu, rw_w0, rw_w2, rw_a0, rw_a2, rw_g2, rw_k_k, rw_k_a, rw_r_k, rw_lnx_g, rw_lnx_b, s5_a_re, s5_a_im, s5_log_dt, s5_b_re, s5_b_im, s5_c_re, s5_c_im, s5_d, s5_glu_w, s5_glu_b, diff_lq1, diff_lk1, diff_lq2, diff_lk2, diff_subln_g, w_branch, w_out, router_g_w, router_g_b, router_e_w, router_e_b, exp_w_gate, exp_w_up, exp_w_down, final_norm_g):
    nb, n_lat, d = x.shape
    n_ctx = ctx.shape[1]
    depth = w_mod.shape[0]
    t_all = n_ctx + n_lat
    assert d == D_MODEL and n_ctx == TM and n_lat % TM == 0
    tpb = t_all // TM
    tk = _pick_tk(t_all)
    hw = RW_HEADS * RW_HEAD

    cos, sin = _rope_tables(n_ctx, n_lat)
    mla_scale = (MLA_NOPE + MLA_ROPE) ** -0.5 * LOG2E
    z32 = jnp.zeros((t_all, 32), F32)
    cq_tab = jnp.concatenate([jnp.ones((t_all, 64), F32), cos, z32], axis=1) * mla_scale
    sq_tab = jnp.concatenate([jnp.zeros((t_all, 64), F32), sin, z32], axis=1) * mla_scale
    ck_tab = jnp.concatenate([cos, sin, jnp.zeros((t_all, 64), F32)], axis=1)
    dcos = jnp.tile(cos, (1, 8))
    dsin = jnp.tile(sin, (1, 8))

    c_rows = jnp.concatenate([c, c_ctx[None, :], jnp.zeros((8 - nb - 1, d), F32)], axis=0)

    h = jnp.concatenate([ctx, x], axis=1).reshape(nb * t_all, d)

    for l in range(depth):
        last = l == depth - 1
        r0, nt = (1, tpb - 1) if last else (0, tpb)

        mod = _mm(c_rows, w_mod[l].astype(BF16), b_mod[l][None, :], tm=8, tn=1536, pre_silu=True, name="mod")
        mods = [mod[:nb + 1, i * d:(i + 1) * d].reshape(nb + 1, 1, d) for i in range(6)]

        wi = w_in[l]
        o_rw, o_s5, o_df, o_gt = 416, 1440, 1696, 2464
        w_kr = wi[:, 384:416]
        w_dq, w_dk, w_dv = wi[:, o_df:o_df + 256], wi[:, o_df + 256:o_df + 512], wi[:, o_df + 512:o_df + 768]
        w_ext = jnp.concatenate(
            [wi[:, o_rw:o_s5],
             wi[:, :416], _rot_cols(w_kr), jnp.zeros((d, 64), F32),
             wi[:, o_s5:o_df],
             w_dq, w_dk, w_dv, _rot_cols(w_dq), _rot_cols(w_dk),
             wi[:, o_gt:]], axis=1).astype(BF16)
        p = _inproj(h, norm1_g[l][None, :], mods[0], mods[1], w_ext, nb=nb, tpb=tpb)

        wq = mla_w_uq[l].reshape(MLA_Q_LORA, MLA_HEADS, MLA_NOPE + MLA_ROPE)
        zq = jnp.zeros((MLA_Q_LORA, MLA_HEADS, 32), F32)
        wa = jnp.concatenate([wq, zq], axis=2).reshape(MLA_Q_LORA, 512).astype(BF16)
        wb = jnp.concatenate([jnp.zeros((MLA_Q_LORA, MLA_HEADS, 64), F32), _rot_cols(wq[:, :, MLA_NOPE:]), zq],
                             axis=2).reshape(MLA_Q_LORA, 512).astype(BF16)
        wkv = mla_w_ukv[l].reshape(MLA_KV_LORA, MLA_HEADS, MLA_NOPE + MLA_V)
        wk = jnp.concatenate([wkv[:, :, :MLA_NOPE], jnp.zeros((MLA_KV_LORA, MLA_HEADS, 64), F32)],
                             axis=2).reshape(MLA_KV_LORA, 512).astype(BF16)
        wv = wkv[:, :, MLA_NOPE:].reshape(MLA_KV_LORA, MLA_HEADS * MLA_V).astype(BF16)
        pk_np = np.zeros((128, 512), np.float32)
        for hh in range(MLA_HEADS):
            for i in range(32):
                pk_np[i, hh * 128 + 64 + i] = 1.0
                pk_np[32 + i, hh * 128 + 64 + i] = 1.0
        pk = jnp.asarray(pk_np, BF16)
        q_m, k_m, v_m = _mla_prep(p, mla_q_norm_g[l][None, :], mla_kv_norm_g[l][None, :], wa, wb, wk, wv, pk,
                                  cq_tab, sq_tab, ck_tab, nb=nb, tpb=tpb)
        ya_lat = _flash_mla(q_m, k_m, v_m, nb=nb, t_all=t_all, q_start=n_ctx, q_len=n_lat, k_len=t_all, tk=tk)

        q_d, k_d, v_d = _diff_prep(p, dcos, dsin, nb=nb, tpb=tpb)
        lam_init = 0.8 - 0.6 * math.exp(-0.3 * l)
        lam = (jnp.exp(jnp.sum(diff_lq1[l] * diff_lk1[l])) - jnp.exp(jnp.sum(diff_lq2[l] * diff_lk2[l])) + lam_init)
        lam_row = jnp.full((1, 256), lam, F32)
        g_row = jnp.tile(diff_subln_g[l], DIFF_HEADS)[None, :]
        yd_lat = _flash_diff(q_d, k_d, v_d, lam_row, g_row, lam_init=lam_init, nb=nb, t_all=t_all,
                             q_start=n_ctx, q_len=n_lat, k_len=t_all, tk=tk)
        if last:
            ya, yd = ya_lat, yd_lat
        else:
            ya_ctx = _flash_mla(q_m, k_m, v_m, nb=nb, t_all=t_all, q_start=0, q_len=n_ctx, k_len=n_ctx, tk=n_ctx)
            yd_ctx = _flash_diff(q_d, k_d, v_d, lam_row, g_row, lam_init=lam_init, nb=nb, t_all=t_all,
                                 q_start=0, q_len=n_ctx, k_len=n_ctx, tk=n_ctx)
            comb = lambda a_c, a_l: jnp.concatenate(
                [a_c.reshape(nb, n_ctx, -1), a_l.reshape(nb, n_lat, -1)], axis=1).reshape(nb * t_all, -1)
            ya, yd = comb(ya_ctx, ya_lat), comb(yd_ctx, yd_lat)

        wlo = jnp.zeros((128, 4 * hw), F32)
        wlo = wlo.at[:64, 0:hw].set(rw_w2[l, 0]).at[:64, hw:2 * hw].set(rw_w2[l, 1])
        wlo = wlo.at[64:, 2 * hw:3 * hw].set(rw_a2[l, 0]).at[64:, 3 * hw:].set(rw_a2[l, 1])
        vecs = jnp.stack([rw_k_k[l], rw_k_a[l], rw_r_k[l].reshape(hw), rw_w0[l, 0], rw_w0[l, 1],
                          rw_a0[l, 0], rw_a0[l, 1], jnp.zeros((hw,), F32)], axis=0)
        (r_, v_, kk_, w0_, k0_, b0_, w1_, k1_, b1_, bonus, gate_rw) = _rw_prep(
            p, rw_mu[l][None, :], wlo.astype(BF16), rw_g2[l].astype(BF16), vecs, nb=nb, tpb=tpb, r0=0, nt=tpb)
        yf, yb_ = _rw_scan(r_, v_, kk_, w0_, k0_, b0_, w1_, k1_, b1_, nb=nb, t_all=t_all, n_ctx=n_ctx)
        ln_vecs = jnp.concatenate([rw_lnx_g[l][None, :], rw_lnx_b[l][None, :], jnp.zeros((6, hw), F32)], axis=0)
        if last:
            trim = lambda a: a.reshape(nb, t_all, hw)[:, n_ctx:].reshape(nb * n_lat, hw)
            bonus, gate_rw = trim(bonus), trim(gate_rw)
        y_rw = _rw_post(yf, yb_, bonus, gate_rw, ln_vecs, nb=nb, tpb_y=tpb, r0_y=r0, nt=nt)

        bbs, abs_, cfs = [], [], []
        for dd in range(2):
            lr, li = s5_a_re[l, dd], s5_a_im[l, dd]
            dt = jnp.exp(s5_log_dt[l, dd])[:, None]
            mag = jnp.exp(lr * dt)
            ab_re, ab_im = mag * jnp.cos(li * dt), mag * jnp.sin(li * dt)
            den = lr * lr + li * li
            nr, ni = ab_re - 1.0, ab_im
            cf_re = (nr * lr + ni * li) / den
            cf_im = (ni * lr - nr * li) / den
            bre, bim = s5_b_re[l, dd], s5_b_im[l, dd]
            bb_re = cf_re[..., None] * bre - cf_im[..., None] * bim
            bb_im = cf_re[..., None] * bim + cf_im[..., None] * bre
            bbs.append(jnp.concatenate([_block_diag(bb_re.transpose(0, 2, 1)),
                                        _block_diag(bb_im.transpose(0, 2, 1))], axis=1))
            abs_.append(jnp.concatenate([ab_re.reshape(-1), ab_im.reshape(-1)])[None, :])
            cfs.append(jnp.concatenate([_block_diag(s5_c_re[l, dd].transpose(0, 2, 1)),
                                        -_block_diag(s5_c_im[l, dd].transpose(0, 2, 1))], axis=0))
        yf_s5, yb_s5 = _s5_scan(p, jnp.concatenate(bbs, axis=1).astype(BF16), jnp.stack(abs_, axis=0),
                                cfs[0].astype(BF16), cfs[1].astype(BF16), nb=nb, t_all=t_all, n_ctx=n_ctx)
        y_s5 = _s5_post(yf_s5, yb_s5, p, s5_d[l].reshape(1, 256), s5_glu_w[l].astype(BF16),
                        s5_glu_b[l][None, :], nb=nb, tpb=tpb, r0=r0, nt=nt)

        h = _merge(ya, y_rw, y_s5, yd, p, h, mods[2], w_branch[l].astype(BF16), w_out[l].astype(BF16),
                   nb=nb, tpb=tpb, r0=r0, nt=nt)

        wr = jnp.concatenate([router_g_w[l], router_e_w[l], jnp.zeros((d, 128 - MOE_GROUPS - MOE_EXPERTS), F32)], axis=1)
        wr_h = wr.astype(BF16)
        wr_m = (wr - wr_h.astype(F32)).astype(BF16)
        rbias = jnp.concatenate([router_g_b[l], router_e_b[l],
                                 jnp.zeros((128 - MOE_GROUPS - MOE_EXPERTS,), F32)])[None, :]
        h = _moe(h, norm2_g[l][None, :], mods[3], mods[4], mods[5], wr_h, wr_m, rbias,
                 exp_w_gate.reshape(depth * MOE_EXPERTS, d, D_EXPERT), exp_w_up.reshape(depth * MOE_EXPERTS, d, D_EXPERT),
                 exp_w_down.reshape(depth * MOE_EXPERTS, D_EXPERT, d),
                 final_norm_g[None, :], layer=l, nb=nb, nt=nt, ctx_first=not last, final=last)

    return h.reshape(nb, n_lat, d)
```

```python
import functools
import math

import jax
import jax.numpy as jnp
import numpy as np
from jax import lax
from jax.experimental import pallas as pl
from jax.experimental.pallas import tpu as pltpu
from jax.experimental.pallas import tpu_sc as plsc

F32 = jnp.float32
BF16 = jnp.bfloat16

TM = 256
VMEM_LIMIT = 48 * 1024 * 1024

D_MODEL = 1024
GRID_W = 64
ROPE_DIM = 32
ROPE_BASE = 10000.0
NORM_EPS = 1e-6
MLA_HEADS, MLA_NOPE, MLA_ROPE, MLA_V = 4, 64, 32, 64
MLA_Q_LORA, MLA_KV_LORA = 256, 128
RW_HEADS, RW_HEAD = 4, 64
RW_LN_EPS = 64e-5
S5_GROUPS, S5_GROUP_CH, S5_STATE = 16, 16, 64
DIFF_HEADS, DIFF_HD = 4, 32
DIFF_EPS = 1e-5
MOE_GROUPS, MOE_PER_GROUP, MOE_TOPK = 4, 8, 2
MOE_EXPERTS = MOE_GROUPS * MOE_PER_GROUP
D_EXPERT = 512
MOE_BLK = 256
RW_CHUNK = 64
S5_CHUNK = 128
ATTN_TK_MAX = 2816
SC_INDEX_TILE = 128
SC_ROWS_PER_STEP = 32
LOG2E = math.log2(math.e)

EXT_RW, EXT_MLA, EXT_S5, EXT_DIFF, EXT_GATE = 0, 1024, 1536, 1792, 3072
N_EXT = 7168


def _cparams(sem, vmem=VMEM_LIMIT):
    return pltpu.CompilerParams(dimension_semantics=sem, vmem_limit_bytes=vmem)


def _dot(a, b):
    return jnp.dot(a, b, preferred_element_type=F32)


def _dot_nt(a, b):
    return lax.dot_general(a, b, (((1,), (1,)), ((), ())), preferred_element_type=F32)


def _split_dot(x, e):
    hi = x.astype(BF16)
    mid = (x - hi.astype(F32)).astype(BF16)
    return _dot(hi, e) + _dot(mid, e)


def _block_ones(n, blk):
    r = lax.broadcasted_iota(jnp.int32, (n, n), 0) // blk
    c = lax.broadcasted_iota(jnp.int32, (n, n), 1) // blk
    return (r == c).astype(BF16)


def _sigmoid(x):
    return 1.0 / (1.0 + jnp.exp(-x))


def _silu(x):
    return x * _sigmoid(x)


def _softplus(x):
    return jnp.maximum(x, 0.0) + jnp.log(1.0 + jnp.exp(-jnp.abs(x)))


def _rms(x, g, eps):
    return x * lax.rsqrt(jnp.mean(x * x, axis=-1, keepdims=True) + eps) * g


def _mm_kernel(x_ref, w_ref, b_ref, o_ref, *, pre_silu):
    x = x_ref[...].astype(F32)
    if pre_silu:
        x = _silu(x)
    o_ref[...] = _dot(x.astype(BF16), w_ref[...]) + b_ref[...]


def _mm(x, w, b, *, tm, tn, pre_silu=False, name="mm"):
    m, k = x.shape
    n = w.shape[1]
    return pl.pallas_call(
        functools.partial(_mm_kernel, pre_silu=pre_silu),
        grid=(m // tm, n // tn),
        in_specs=[pl.BlockSpec((tm, k), lambda i, j: (i, 0)),
                  pl.BlockSpec((k, tn), lambda i, j: (0, j)),
                  pl.BlockSpec((1, tn), lambda i, j: (0, j))],
        out_specs=pl.BlockSpec((tm, tn), lambda i, j: (i, j)),
        out_shape=jax.ShapeDtypeStruct((m, n), F32),
        compiler_params=_cparams(("parallel", "arbitrary")),
        name=name,
    )(x, w, b)


def _inproj_kernel(h_ref, g_ref, sh_ref, sc_ref, w_ref, o_ref):
    x = _rms(h_ref[...], g_ref[...], NORM_EPS)
    xn = (x * (1.0 + sc_ref[0]) + sh_ref[0]).astype(BF16)
    o_ref[...] = _dot(xn, w_ref[...])


def _inproj(h, g, shift, scale, w_ext, *, nb, tpb):
    n = h.shape[0]
    tn = N_EXT // 2
    modmap = lambda j, b, r: (jnp.where(r == 0, nb, b), 0, 0)
    return pl.pallas_call(
        _inproj_kernel,
        grid=(N_EXT // tn, nb, tpb),
        in_specs=[pl.BlockSpec((TM, D_MODEL), lambda j, b, r: (b * tpb + r, 0)),
                  pl.BlockSpec((1, D_MODEL), lambda j, b, r: (0, 0)),
                  pl.BlockSpec((1, 1, D_MODEL), modmap),
                  pl.BlockSpec((1, 1, D_MODEL), modmap),
                  pl.BlockSpec((D_MODEL, tn), lambda j, b, r: (0, j))],
        out_specs=pl.BlockSpec((TM, tn), lambda j, b, r: (b * tpb + r, j)),
        out_shape=jax.ShapeDtypeStruct((n, N_EXT), F32),
        compiler_params=_cparams(("parallel", "parallel", "parallel")),
        name="inproj",
    )(h, g, shift, scale, w_ext)


def _mla_prep_kernel(p_ref, gq_ref, gkv_ref, wa_ref, wb_ref, wk_ref, wv_ref, pk_ref,
                     cq_ref, sq_ref, ck_ref, q_ref, k_ref, v_ref):
    seg = p_ref[...]
    nq = _rms(seg[:, :MLA_Q_LORA], gq_ref[...], NORM_EPS).astype(BF16)
    nkv = _rms(seg[:, MLA_Q_LORA:MLA_Q_LORA + MLA_KV_LORA], gkv_ref[...], NORM_EPS).astype(BF16)
    cq = jnp.concatenate([cq_ref[...]] * MLA_HEADS, axis=1)
    sq = jnp.concatenate([sq_ref[...]] * MLA_HEADS, axis=1)
    q = _dot(nq, wa_ref[...]) * cq + _dot(nq, wb_ref[...]) * sq
    q_ref[...] = q.astype(BF16)
    kr = (seg[:, 384:512] * ck_ref[...]).astype(BF16)
    k_ref[...] = (_dot(nkv, wk_ref[...]) + _dot(kr, pk_ref[...])).astype(BF16)
    v_ref[...] = _dot(nkv, wv_ref[...]).astype(BF16)


def _mla_prep(p, gq, gkv, wa, wb, wk, wv, pk, cq_tab, sq_tab, ck_tab, *, nb, tpb):
    n = p.shape[0]
    tok = lambda b, r: (b * tpb + r, 0)
    pos = lambda b, r: (r, 0)
    full = lambda b, r: (0, 0)
    return pl.pallas_call(
        _mla_prep_kernel,
        grid=(nb, tpb),
        in_specs=[pl.BlockSpec((TM, 512), lambda b, r: (b * tpb + r, EXT_MLA // 512)),
                  pl.BlockSpec((1, MLA_Q_LORA), full),
                  pl.BlockSpec((1, MLA_KV_LORA), full),
                  pl.BlockSpec((MLA_Q_LORA, 512), full),
                  pl.BlockSpec((MLA_Q_LORA, 512), full),
                  pl.BlockSpec((MLA_KV_LORA, 512), full),
                  pl.BlockSpec((MLA_KV_LORA, 256), full),
                  pl.BlockSpec((128, 512), full),
                  pl.BlockSpec((TM, 128), pos),
                  pl.BlockSpec((TM, 128), pos),
                  pl.BlockSpec((TM, 128), pos)],
        out_specs=[pl.BlockSpec((TM, 512), tok),
                   pl.BlockSpec((TM, 512), tok),
                   pl.BlockSpec((TM, 256), tok)],
        out_shape=[jax.ShapeDtypeStruct((n, 512), BF16),
                   jax.ShapeDtypeStruct((n, 512), BF16),
                   jax.ShapeDtypeStruct((n, 256), BF16)],
        compiler_params=_cparams(("parallel", "parallel")),
        name="mla_prep",
    )(p, gq, gkv, wa, wb, wk, wv, pk, cq_tab, sq_tab, ck_tab)


def _flash_mla_kernel(q_ref, k_ref, v_ref, o_ref, m_ref, l_ref, acc_ref):
    kk = pl.program_id(2)
    tq = q_ref.shape[0]

    @pl.when(kk == 0)
    def _():
        m_ref[...] = jnp.full(m_ref.shape, -jnp.inf, F32)
        l_ref[...] = jnp.zeros(l_ref.shape, F32)
        acc_ref[...] = jnp.zeros(acc_ref.shape, F32)

    ps, alphas = [], []
    for h in range(MLA_HEADS):
        s = _dot_nt(q_ref[:, h * 128:(h + 1) * 128], k_ref[:, h * 128:(h + 1) * 128])
        m_prev = m_ref[h]
        m_new = jnp.maximum(m_prev, jnp.max(s, axis=-1, keepdims=True))
        alpha = jnp.exp2(m_prev - m_new)
        p = jnp.exp2(s - m_new)
        l_ref[h] = alpha * l_ref[h] + jnp.sum(p, axis=-1, keepdims=True)
        m_ref[h] = m_new
        alphas.append(alpha)
        ps.append(p.astype(BF16))
    pv = _dot(jnp.concatenate(ps, axis=0), v_ref[...])
    acc_ref[...] = jnp.concatenate(alphas, axis=0) * acc_ref[...] + pv

    @pl.when(kk == pl.num_programs(2) - 1)
    def _():
        lane = lax.broadcasted_iota(jnp.int32, (tq, MLA_HEADS * MLA_V), 1)
        out = jnp.zeros((tq, MLA_HEADS * MLA_V), F32)
        for h in range(MLA_HEADS):
            out = jnp.where(lane // MLA_V == h, acc_ref[h * tq:(h + 1) * tq, :] / l_ref[h], out)
        o_ref[...] = out


def _flash_mla(q, k, v, *, nb, t_all, q_start, q_len, k_len, tk):
    tq = TM
    tpb_q, tpb_k = t_all // tq, t_all // tk
    q0 = q_start // tq
    nq = q_len // tq
    return pl.pallas_call(
        _flash_mla_kernel,
        grid=(nb, nq, k_len // tk),
        in_specs=[pl.BlockSpec((tq, 512), lambda b, i, kk: (b * tpb_q + q0 + i, 0)),
                  pl.BlockSpec((tk, 512), lambda b, i, kk: (b * tpb_k + kk, 0)),
                  pl.BlockSpec((tk, 256), lambda b, i, kk: (b * tpb_k + kk, 0))],
        out_specs=pl.BlockSpec((tq, 256), lambda b, i, kk: (b * nq + i, 0)),
        out_shape=jax.ShapeDtypeStruct((nb * q_len, 256), F32),
        scratch_shapes=[pltpu.VMEM((MLA_HEADS, tq, 1), F32), pltpu.VMEM((MLA_HEADS, tq, 1), F32),
                        pltpu.VMEM((MLA_HEADS * tq, 256), F32)],
        compiler_params=_cparams(("parallel", "parallel", "arbitrary")),
        name="flash_mla",
    )(q, k, v)


def _diff_prep(p, cos_tab, sin_tab, *, nb, tpb):
    n = p.shape[0]
    tok = lambda b, r: (b * tpb + r, 0)
    pos = lambda b, r: (r, 0)
    return pl.pallas_call(
        _diff_prep_kernel_cols,
        grid=(nb, tpb),
        in_specs=[pl.BlockSpec((TM, 256), lambda b, r, c=c: (b * tpb + r, EXT_DIFF // 256 + c))
                  for c in range(5)]
                 + [pl.BlockSpec((TM, 256), pos), pl.BlockSpec((TM, 256), pos)],
        out_specs=[pl.BlockSpec((TM, 256), tok)] * 3,
        out_shape=[jax.ShapeDtypeStruct((n, 256), BF16)] * 3,
        compiler_params=_cparams(("parallel", "parallel")),
        name="diff_prep",
    )(p, p, p, p, p, cos_tab, sin_tab)


def _diff_prep_kernel_cols(q_in, k_in, v_in, qr_in, kr_in, cos_ref, sin_ref, q_ref, k_ref, v_ref):
    cos, sin = cos_ref[...], sin_ref[...]
    scale = DIFF_HD ** -0.5 * LOG2E
    q_ref[...] = ((q_in[...] * cos + qr_in[...] * sin) * scale).astype(BF16)
    k_ref[...] = (k_in[...] * cos + kr_in[...] * sin).astype(BF16)
    v_ref[...] = v_in[...].astype(BF16)


def _flash_diff_kernel(q_ref, k_ref, v_ref, lam_ref, g_ref, o_ref, qs_ref, m_ref, l_ref, acc_ref, *, lam_init):
    kk = pl.program_id(2)
    tq = q_ref.shape[0]
    nsm = 2 * DIFF_HEADS

    @pl.when(kk == 0)
    def _():
        m_ref[...] = jnp.full(m_ref.shape, -jnp.inf, F32)
        l_ref[...] = jnp.zeros(l_ref.shape, F32)
        acc_ref[...] = jnp.zeros(acc_ref.shape, F32)
        q = q_ref[...]
        lane = lax.broadcasted_iota(jnp.int32, (tq, 256), 1)
        for i in range(nsm):
            qs_ref[i * tq:(i + 1) * tq, :] = jnp.where((lane // DIFF_HD) == i, q, jnp.zeros_like(q))

    k = k_ref[...]
    ps, alphas = [], []
    for i in range(nsm):
        rows = slice(i * tq, (i + 1) * tq)
        s = _dot_nt(qs_ref[rows, :], k)
        m_prev = m_ref[rows, :]
        m_new = jnp.maximum(m_prev, jnp.max(s, axis=-1, keepdims=True))
        alpha = jnp.exp2(m_prev - m_new)
        p = jnp.exp2(s - m_new)
        l_ref[rows, :] = alpha * l_ref[rows, :] + jnp.sum(p, axis=-1, keepdims=True)
        m_ref[rows, :] = m_new
        alphas.append(alpha)
        ps.append(p.astype(BF16))
    pv = _dot(jnp.concatenate(ps, axis=0), v_ref[...])
    acc_ref[...] = jnp.concatenate(alphas, axis=0) * acc_ref[...] + pv

    @pl.when(kk == pl.num_programs(2) - 1)
    def _():
        lane = lax.broadcasted_iota(jnp.int32, (tq, 256), 1)
        o = jnp.zeros((tq, 256), F32)
        for h in range(DIFF_HEADS):
            r0, r1 = (2 * h) * tq, (2 * h + 1) * tq
            o0 = acc_ref[r0:r0 + tq, :] / l_ref[r0:r0 + tq, :]
            o1 = acc_ref[r1:r1 + tq, :] / l_ref[r1:r1 + tq, :]
            o = jnp.where((lane // (2 * DIFF_HD)) == h, o0 - lam_ref[...] * o1, o)
        ms = _split_dot(o * o, _block_ones(256, 2 * DIFF_HD)) * (1.0 / (2 * DIFF_HD))
        o_ref[...] = o * lax.rsqrt(ms + DIFF_EPS) * g_ref[...] * (1.0 - lam_init)


def _flash_diff(q, k, v, lam_row, g_row, *, lam_init, nb, t_all, q_start, q_len, k_len, tk):
    tq = TM
    tpb_q, tpb_k = t_all // tq, t_all // tk
    q0 = q_start // tq
    nq = q_len // tq
    return pl.pallas_call(
        functools.partial(_flash_diff_kernel, lam_init=lam_init),
        grid=(nb, nq, k_len // tk),
        in_specs=[pl.BlockSpec((tq, 256), lambda b, i, kk: (b * tpb_q + q0 + i, 0)),
                  pl.BlockSpec((tk, 256), lambda b, i, kk: (b * tpb_k + kk, 0)),
                  pl.BlockSpec((tk, 256), lambda b, i, kk: (b * tpb_k + kk, 0)),
                  pl.BlockSpec((1, 256), lambda b, i, kk: (0, 0)),
                  pl.BlockSpec((1, 256), lambda b, i, kk: (0, 0))],
        out_specs=pl.BlockSpec((tq, 256), lambda b, i, kk: (b * nq + i, 0)),
        out_shape=jax.ShapeDtypeStruct((nb * q_len, 256), F32),
        scratch_shapes=[pltpu.VMEM((2 * DIFF_HEADS * tq, 256), BF16),
                        pltpu.VMEM((2 * DIFF_HEADS * tq, 1), F32),
                        pltpu.VMEM((2 * DIFF_HEADS * tq, 1), F32),
                        pltpu.VMEM((2 * DIFF_HEADS * tq, 256), F32)],
        compiler_params=_cparams(("parallel", "parallel", "arbitrary")),
        name="flash_diff",
    )(q, k, v, lam_row, g_row)


def _rw_prep_kernel(p_ref, prev_ref, next_ref, mu_ref, wlo_ref, g2_ref, vec_ref,
                    r_ref, v_ref, kk_ref, w0_ref, k0_ref, b0_ref, w1_ref, k1_ref, b1_ref,
                    bonus_ref, gate_ref, *, r0, lat_last):
    r = pl.program_id(1) + r0
    p = p_ref[...]
    row = lax.broadcasted_iota(jnp.int32, p.shape, 0)
    first_tile = jnp.logical_or(r == 0, r == 1)
    last_tile = jnp.logical_or(r == 0, r == lat_last)
    prev_row = jnp.where(first_tile, 0.0, prev_ref[7:8, :])
    next_row = jnp.where(last_tile, 0.0, next_ref[0:1, :])
    up = jnp.where(row == 0, prev_row, pltpu.roll(p, 1, 0))
    dn = jnp.where(row == TM - 1, next_row, pltpu.roll(p, TM - 1, 0))
    z = p + (0.5 * (up + dn) - p) * mu_ref[...]

    hw = RW_HEADS * RW_HEAD
    rr, k, v = z[:, :hw], z[:, hw:2 * hw], z[:, 2 * hw:3 * hw]
    lo = z[:, 3 * hw:3 * hw + 128]
    lane = lax.broadcasted_iota(jnp.int32, lo.shape, 1)
    lo = jnp.where(lane < 64, jnp.tanh(lo), lo).astype(BF16)
    wa = _dot(lo, wlo_ref[...])
    gate_ref[...] = _dot(_sigmoid(z[:, 3 * hw + 128:]).astype(BF16), g2_ref[...])

    e4 = _block_ones(hw, RW_HEAD)
    k_k, k_a, r_k = vec_ref[0:1, :], vec_ref[1:2, :], vec_ref[2:3, :]
    kk = k * k_k
    nrm = jnp.maximum(jnp.sqrt(_split_dot(kk * kk, e4)), 1e-12)
    kk = kk / nrm
    r_ref[...] = rr
    v_ref[...] = v
    kk_ref[...] = kk
    ksum = jnp.zeros_like(k)
    for d, (w_ref, kd_ref, b_ref) in enumerate(((w0_ref, k0_ref, b0_ref), (w1_ref, k1_ref, b1_ref))):
        w0 = vec_ref[3 + d:4 + d, :]
        a0 = vec_ref[5 + d:6 + d, :]
        wd = -_softplus(-(w0 + wa[:, d * hw:(d + 1) * hw])) - 0.5
        w_ref[...] = jnp.exp(-jnp.exp(wd))
        ad = _sigmoid(a0 + wa[:, (2 + d) * hw:(3 + d) * hw])
        kd = k * (1.0 + (ad - 1.0) * k_a)
        kd_ref[...] = kd
        b_ref[...] = kk * ad
        ksum = ksum + kd
    bonus_ref[...] = _split_dot(rr * ksum * r_k, e4) * v


def _rw_prep(p, mu, wlo, g2, vecs, *, nb, tpb, r0, nt):
    n_out = nb * nt * TM
    hw = RW_HEADS * RW_HEAD
    n_rows8 = p.shape[0] // 8
    tok = lambda b, r: (b * tpb + r0 + r, EXT_RW // 1024)
    prev = lambda b, r: (jnp.maximum((b * tpb + r0 + r) * (TM // 8) - 1, 0), EXT_RW // 1024)
    nxt = lambda b, r: (jnp.minimum((b * tpb + r0 + r + 1) * (TM // 8), n_rows8 - 1), EXT_RW // 1024)
    out = lambda b, r: (b * nt + r, 0)
    full = lambda b, r: (0, 0)
    return pl.pallas_call(
        functools.partial(_rw_prep_kernel, r0=r0, lat_last=tpb - 1),
        grid=(nb, nt),
        in_specs=[pl.BlockSpec((TM, 1024), tok),
                  pl.BlockSpec((8, 1024), prev),
                  pl.BlockSpec((8, 1024), nxt),
                  pl.BlockSpec((1, 1024), full),
                  pl.BlockSpec((128, 4 * hw), full),
                  pl.BlockSpec((128, hw), full),
                  pl.BlockSpec((8, hw), full)],
        out_specs=[pl.BlockSpec((TM, hw), out)] * 11,
        out_shape=[jax.ShapeDtypeStruct((n_out, hw), F32)] * 11,
        compiler_params=_cparams(("parallel", "parallel")),
        name="rw_prep",
    )(p, p, p, mu, wlo, g2, vecs)


def _rw_scan_kernel(rf, vf, kkf, wf, kf, bf, rb, vb, kkb, wb, kb, bb, yf_ref, yb_ref, s_ref, *, nb):
    c = RW_CHUNK

    @pl.when(pl.program_id(0) == 0)
    def _():
        s_ref[...] = jnp.zeros(s_ref.shape, F32)

    e2 = _block_ones(128, RW_HEAD)
    z2 = jnp.zeros((128, 128), BF16)
    rhs_pair = jnp.concatenate([jnp.concatenate([e2, z2], axis=1), jnp.concatenate([z2, e2], axis=1)], axis=0)
    e22 = jnp.concatenate([e2, e2], axis=0)
    lane = lax.broadcasted_iota(jnp.int32, (RW_HEAD, 128), 1)
    sub = lax.broadcasted_iota(jnp.int32, (RW_HEAD, 128), 0)
    diag = (lane % RW_HEAD) == sub
    sub8 = lax.broadcasted_iota(jnp.int32, (8, 128), 0)
    dirs = ((rf, vf, kkf, wf, kf, bf, yf_ref), (rb, vb, kkb, wb, kb, bb, yb_ref))

    def allreduce_rows(x):
        t = x[0:8]
        for i in range(1, 8):
            t = t + x[8 * i:8 * i + 8]
        for sh in (4, 2, 1):
            t = t + pltpu.roll(t, sh, 0)
        return t

    def col(row):
        return jnp.where(diag, row, 0.0).astype(BF16)

    def group(g, carry):
        tiles, ytiles = {}, {}
        for d, refs in enumerate(dirs):
            base = pl.multiple_of((g if d == 0 else c // 8 - 1 - g) * 8, 8)
            if d == 0:
                prev = lambda x, sh: jnp.where(sub8 >= sh, pltpu.roll(x, sh, 0), 1.0)
            else:
                prev = lambda x, sh: jnp.where(sub8 < 8 - sh, pltpu.roll(x, 8 - sh, 0), 1.0)
            last = 7 if d == 0 else 0
            for b in range(nb):
                for hp in range(2):
                    r_, v_, kk_, w_, k_, b_ = [ref[b, pl.ds(base, 8), pl.ds(hp * 128, 128)] for ref in refs[:6]]
                    gam = w_
                    for sh in (1, 2, 4):
                        gam = gam * prev(gam, sh)
                    inv = 1.0 / gam
                    tiles[d, b, hp] = (base, v_, b_ * inv, kk_ * prev(gam, 1), k_ * inv, r_ * gam,
                                       gam[last:last + 1, :])
                    ytiles[d, b, hp] = jnp.zeros((8, 128), F32)
        units = [(d, b, hp) for d in range(2) for b in range(nb) for hp in range(2)]
        for jj in range(8):
            lhs = []
            for (d, b, hp) in units:
                rw = slice(jj, jj + 1) if d == 0 else slice(7 - jj, 8 - jj)
                _, _, bh, kkh, kh, rh, _ = tiles[d, b, hp]
                lhs.append(jnp.concatenate([col(bh[rw]), col(kkh[rw])], axis=1))
                lhs.append(jnp.concatenate([col(kh[rw]), col(rh[rw])], axis=1))
            cm = _dot(jnp.concatenate(lhs, axis=0), rhs_pair)
            for u, (d, b, hp) in enumerate(units):
                j = jj if d == 0 else 7 - jj
                v_ = tiles[d, b, hp][1]
                r0 = 2 * u * RW_HEAD
                bc, kkc = cm[r0:r0 + RW_HEAD, :128], cm[r0:r0 + RW_HEAD, 128:]
                kc, rc = cm[r0 + RW_HEAD:r0 + 2 * RW_HEAD, :128], cm[r0 + RW_HEAD:r0 + 2 * RW_HEAD, 128:]
                s = s_ref[u]
                sa = jnp.concatenate([allreduce_rows(kkc * s)] * 8, axis=0)
                s = s - bc * sa + kc * v_[j:j + 1]
                s_ref[u] = s
                ytiles[d, b, hp] = jnp.where(sub8 == j, allreduce_rows(rc * s), ytiles[d, b, hp])
        for d, refs in enumerate(dirs):
            for b in range(nb):
                for hp in range(2):
                    u = (d * nb + b) * 2 + hp
                    gl = tiles[d, b, hp][6]
                    gh = gl.astype(BF16).astype(F32)
                    s_ref[u] = s_ref[u] * _dot(jnp.concatenate([col(gh), col(gl - gh)], axis=1), e22)
                    refs[6][b, pl.ds(tiles[d, b, hp][0], 8), pl.ds(hp * 128, 128)] = ytiles[d, b, hp]
        return carry

    lax.fori_loop(0, c // 8, group, 0)


def _rw_scan(r, v, kk, w0, k0, b0, w1, k1, b1, *, nb, t_all, n_ctx):
    c = RW_CHUNK
    nc, ncc = t_all // c, n_ctx // c
    hw = RW_HEADS * RW_HEAD
    shp = lambda a: a.reshape(nb, t_all, hw)
    fwd = lambda j: (0, j, 0)
    bwd = lambda j: (0, jnp.where(j < ncc, ncc - 1 - j, nc - 1 - (j - ncc)), 0)
    blk = (nb, c, hw)
    yshape = jax.ShapeDtypeStruct((nb, t_all, hw), F32)
    yf, yb = pl.pallas_call(
        functools.partial(_rw_scan_kernel, nb=nb),
        grid=(nc,),
        in_specs=[pl.BlockSpec(blk, fwd)] * 6 + [pl.BlockSpec(blk, bwd)] * 6,
        out_specs=[pl.BlockSpec(blk, fwd), pl.BlockSpec(blk, bwd)],
        out_shape=[yshape, yshape],
        scratch_shapes=[pltpu.VMEM((2 * nb * 2, RW_HEAD, 128), F32)],
        compiler_params=_cparams(("arbitrary",)),
        name="rw_scan",
    )(shp(r), shp(v), shp(kk), shp(w0), shp(k0), shp(b0),
      shp(r), shp(v), shp(kk), shp(w1), shp(k1), shp(b1))

    return yf.reshape(nb * t_all, hw), yb.reshape(nb * t_all, hw)


def _rw_post_kernel(yf_ref, yb_ref, bonus_ref, gate_ref, vec_ref, o_ref):
    e4 = _block_ones(RW_HEADS * RW_HEAD, RW_HEAD)
    y = yf_ref[...] + yb_ref[...]
    mean = _split_dot(y, e4) * (1.0 / RW_HEAD)
    yc = y - mean
    var = _split_dot(yc * yc, e4) * (1.0 / RW_HEAD)
    yn = yc * lax.rsqrt(var + RW_LN_EPS) * vec_ref[0:1, :] + vec_ref[1:2, :]
    o_ref[...] = (yn + bonus_ref[...]) * gate_ref[...]


def _rw_post(yf, yb, bonus, gate, vecs, *, nb, tpb_y, r0_y, nt):
    hw = RW_HEADS * RW_HEAD
    n_out = bonus.shape[0]
    ytok = lambda b, r: (b * tpb_y + r0_y + r, 0)
    tok = lambda b, r: (b * nt + r, 0)
    return pl.pallas_call(
        _rw_post_kernel,
        grid=(nb, nt),
        in_specs=[pl.BlockSpec((TM, hw), ytok), pl.BlockSpec((TM, hw), ytok),
                  pl.BlockSpec((TM, hw), tok), pl.BlockSpec((TM, hw), tok),
                  pl.BlockSpec((8, hw), lambda b, r: (0, 0))],
        out_specs=pl.BlockSpec((TM, hw), tok),
        out_shape=jax.ShapeDtypeStruct((n_out, hw), F32),
        compiler_params=_cparams(("parallel", "parallel")),
        name="rw_post",
    )(yf, yb, bonus, gate, vecs)


def _s5_scan_kernel(uf_ref, ub_ref, wb_ref, ab_ref, cf_ref, cb_ref, yf_ref, yb_ref, x_ref, st_ref, *, nb):
    c = S5_CHUNK
    nst = S5_GROUPS * S5_STATE

    @pl.when(pl.program_id(0) == 0)
    def _():
        st_ref[...] = jnp.zeros(st_ref.shape, F32)

    dirs = ((uf_ref, cf_ref, yf_ref), (ub_ref, cb_ref, yb_ref))
    for d, (u_ref, _, _) in enumerate(dirs):
        for b in range(nb):
            x_ref[d, b] = _dot(u_ref[b].astype(BF16), wb_ref[:, d * 2 * nst:(d + 1) * 2 * nst])

    def group(g, carry):
        for d in range(2):
            base = pl.multiple_of((g if d == 0 else c // 8 - 1 - g) * 8, 8)
            ar = ab_ref[d, :, 0:nst]
            ai = ab_ref[d, :, nst:2 * nst]
            for b in range(nb):
                u = d * nb + b
                xr = st_ref[u, :, 0:nst]
                xi = st_ref[u, :, nst:2 * nst]
                bur = x_ref[d, b, pl.ds(base, 8), 0:nst]
                bui = x_ref[d, b, pl.ds(base, 8), nst:2 * nst]
                rows_r, rows_i = [None] * 8, [None] * 8
                for jj in range(8):
                    j = jj if d == 0 else 7 - jj
                    xr, xi = (ar * xr - ai * xi + bur[j:j + 1, :], ar * xi + ai * xr + bui[j:j + 1, :])
                    rows_r[j], rows_i[j] = xr, xi
                st_ref[u, :, 0:nst] = xr
                st_ref[u, :, nst:2 * nst] = xi
                x_ref[d, b, pl.ds(base, 8), 0:nst] = jnp.concatenate(rows_r, axis=0)
                x_ref[d, b, pl.ds(base, 8), nst:2 * nst] = jnp.concatenate(rows_i, axis=0)
        return carry

    lax.fori_loop(0, c // 8, group, 0)

    for d, (_, c_ref, y_ref) in enumerate(dirs):
        for b in range(nb):
            y_ref[b] = _dot(x_ref[d, b].astype(BF16), c_ref[...])


def _s5_scan(p, w_b, ab, cf, cb, *, nb, t_all, n_ctx):
    c = S5_CHUNK
    nc, ncc = t_all // c, n_ctx // c
    nst2 = 2 * S5_GROUPS * S5_STATE
    width = S5_GROUPS * S5_GROUP_CH
    p3 = p.reshape(nb, t_all, N_EXT)
    fwd = lambda j: j
    bwd = lambda j: jnp.where(j < ncc, ncc - 1 - j, nc - 1 - (j - ncc))
    yshape = jax.ShapeDtypeStruct((nb, t_all, width), F32)
    full = lambda j: (0, 0)
    yf, yb = pl.pallas_call(
        functools.partial(_s5_scan_kernel, nb=nb),
        grid=(nc,),
        in_specs=[pl.BlockSpec((nb, c, width), lambda j: (0, fwd(j), EXT_S5 // width)),
                  pl.BlockSpec((nb, c, width), lambda j: (0, bwd(j), EXT_S5 // width)),
                  pl.BlockSpec((width, 2 * nst2), full),
                  pl.BlockSpec((2, 1, nst2), lambda j: (0, 0, 0)),
                  pl.BlockSpec((nst2, width), full), pl.BlockSpec((nst2, width), full)],
        out_specs=[pl.BlockSpec((nb, c, width), lambda j: (0, fwd(j), 0)),
                   pl.BlockSpec((nb, c, width), lambda j: (0, bwd(j), 0))],
        out_shape=[yshape, yshape],
        scratch_shapes=[pltpu.VMEM((2, nb, c, nst2), F32), pltpu.VMEM((2 * nb, 1, nst2), F32)],
        compiler_params=_cparams(("arbitrary",)),
        name="s5_scan",
    )(p3, p3, w_b, ab, cf, cb)
    return yf.reshape(nb * t_all, width), yb.reshape(nb * t_all, width)


def _s5_post_kernel(yf_ref, yb_ref, u_ref, d_ref, gw_ref, gb_ref, o_ref):
    y = yf_ref[...] + yb_ref[...] + d_ref[...] * u_ref[...]
    zg = 0.5 * y * (1.0 + jnp.tanh(math.sqrt(2.0 / math.pi) * (y + 0.044715 * (y * y * y))))
    o_ref[...] = zg * _sigmoid(_dot(zg.astype(BF16), gw_ref[...]) + gb_ref[...])


def _s5_post(yf, yb, p, d_row, glu_w, glu_b, *, nb, tpb, r0, nt):
    tok = lambda b, r: (b * tpb + r0 + r, 0)
    full = lambda b, r: (0, 0)
    return pl.pallas_call(
        _s5_post_kernel,
        grid=(nb, nt),
        in_specs=[pl.BlockSpec((TM, 256), tok), pl.BlockSpec((TM, 256), tok),
                  pl.BlockSpec((TM, 256), lambda b, r: (b * tpb + r0 + r, EXT_S5 // 256)),
                  pl.BlockSpec((1, 256), full), pl.BlockSpec((256, 256), full),
                  pl.BlockSpec((1, 256), full)],
        out_specs=pl.BlockSpec((TM, 256), lambda b, r: (b * nt + r, 0)),
        out_shape=jax.ShapeDtypeStruct((nb * nt * TM, 256), F32),
        compiler_params=_cparams(("parallel", "parallel")),
        name="s5_post",
    )(yf, yb, p, d_row, glu_w, glu_b)


def _merge_kernel(ya_ref, yb_ref, ys_ref, yd_ref, g0, g1, g2, g3, h_ref, m2_ref, wb_ref, wo_ref, o_ref):
    acc = None
    for n, (y_ref, g_ref) in enumerate(((ya_ref, g0), (yb_ref, g1), (ys_ref, g2), (yd_ref, g3))):
        term = _sigmoid(g_ref[...]) * _dot(y_ref[...].astype(BF16), wb_ref[n])
        acc = term if acc is None else acc + term
    o_ref[...] = h_ref[...] + m2_ref[0] * _dot(acc.astype(BF16), wo_ref[...])


def _merge(ya, yb, ys, yd, p, h, mod2, w_branch, w_out, *, nb, tpb, r0, nt):
    tok_in = lambda b, r: (b * tpb + r0 + r, 0)
    tok_out = lambda b, r: (b * nt + r, 0)
    full2 = lambda b, r: (0, 0)
    gate = [pl.BlockSpec((TM, D_MODEL), lambda b, r, c=c: (b * tpb + r0 + r, EXT_GATE // D_MODEL + c))
            for c in range(4)]
    return pl.pallas_call(
        _merge_kernel,
        grid=(nb, nt),
        in_specs=[pl.BlockSpec((TM, 256), tok_out)] * 4 + gate
                 + [pl.BlockSpec((TM, D_MODEL), tok_in),
                    pl.BlockSpec((1, 1, D_MODEL), lambda b, r: (jnp.where(r0 + r == 0, nb, b), 0, 0)),
                    pl.BlockSpec((4, 256, D_MODEL), lambda b, r: (0, 0, 0)),
                    pl.BlockSpec((D_MODEL, D_MODEL), full2)],
        out_specs=pl.BlockSpec((TM, D_MODEL), tok_out),
        out_shape=jax.ShapeDtypeStruct((nb * nt * TM, D_MODEL), F32),
        compiler_params=_cparams(("parallel", "parallel")),
        name="merge",
    )(ya, yb, ys, yd, p, p, p, p, h, mod2, w_branch, w_out)


def _router_kernel(h_ref, g_ref, sh_ref, sc_ref, wh_ref, wm_ref, b_ref, x_ref, lg_ref):
    x = _rms(h_ref[...], g_ref[...], NORM_EPS) * (1.0 + sc_ref[0]) + sh_ref[0]
    xh = x.astype(BF16)
    xm = (x - xh.astype(F32)).astype(BF16)
    bits = lax.bitcast_convert_type(xh.astype(F32), jnp.uint32)
    half = D_MODEL // 2
    x_ref[...] = (bits[:, :half] >> 16) | (bits[:, half:] & jnp.uint32(0xFFFF0000))
    lg = (_dot(xh, wh_ref[...]) + _dot(xm, wh_ref[...]) + _dot(xh, wm_ref[...])) + b_ref[...]

    lane = lax.broadcasted_iota(jnp.int32, lg.shape, 1)
    lanef = lane.astype(F32)
    neg = jnp.float32(-jnp.inf)
    big = jnp.float32(1e9)
    rmax = lambda v: jnp.max(v, axis=-1, keepdims=True)
    rmin = lambda v: jnp.min(v, axis=-1, keepdims=True)
    rsum = lambda v: jnp.sum(v, axis=-1, keepdims=True)

    gmask = lane < MOE_GROUPS
    mg = rmax(jnp.where(gmask, lg, neg))
    eg = jnp.where(gmask, jnp.exp(lg - mg), 0.0)
    pg = eg / rsum(eg)
    pg_top = rmax(pg)
    g_sel = rmin(jnp.where(jnp.logical_and(gmask, pg == pg_top), lanef, big))
    lo = MOE_GROUPS + MOE_PER_GROUP * g_sel
    emask = jnp.logical_and(lanef >= lo, lanef < lo + MOE_PER_GROUP)
    me = rmax(jnp.where(emask, lg, neg))
    ee = jnp.where(emask, jnp.exp(lg - me), 0.0)
    pe = jnp.where(emask, ee / rsum(ee), -1.0)
    p1 = rmax(pe)
    i1 = rmin(jnp.where(pe == p1, lanef, big))
    pe2 = jnp.where(lanef == i1, -1.0, pe)
    p2 = rmax(pe2)
    i2 = rmin(jnp.where(pe2 == p2, lanef, big))
    den = p1 + p2
    out = jnp.where(lane == 0, i1 - MOE_GROUPS, 0.0)
    out = jnp.where(lane == 1, i2 - MOE_GROUPS, out)
    out = jnp.where(lane == 2, pg_top * p1 / den, out)
    out = jnp.where(lane == 3, pg_top * p2 / den, out)
    lg_ref[...] = out


def _router(h, g, shift, scale, wh, wm, bias, *, nb, nt, ctx_first):
    tok = lambda b, r: (b * nt + r, 0)
    full = lambda b, r: (0, 0)
    if ctx_first:
        modmap = lambda b, r: (jnp.where(r == 0, nb, b), 0, 0)
    else:
        modmap = lambda b, r: (b, 0, 0)
    n = h.shape[0]
    return pl.pallas_call(
        _router_kernel,
        grid=(nb, nt),
        in_specs=[pl.BlockSpec((TM, D_MODEL), tok), pl.BlockSpec((1, D_MODEL), full),
                  pl.BlockSpec((1, 1, D_MODEL), modmap), pl.BlockSpec((1, 1, D_MODEL), modmap),
                  pl.BlockSpec((D_MODEL, 128), full), pl.BlockSpec((D_MODEL, 128), full),
                  pl.BlockSpec((1, 128), full)],
        out_specs=[pl.BlockSpec((TM, D_MODEL // 2), tok), pl.BlockSpec((TM, 128), tok)],
        out_shape=[jax.ShapeDtypeStruct((n, D_MODEL // 2), jnp.uint32), jax.ShapeDtypeStruct((n, 128), F32)],
        compiler_params=_cparams(("parallel", "parallel")),
        name="router",
    )(h, g, shift, scale, wh, wm, bias)


def _expert_kernel(be_ref, nv_ref, x_ref, wg_ref, wu_ref, wd_ref, o_ref, wgb_ref, wub_ref, wdb_ref):
    i = pl.program_id(0)

    @pl.when(jnp.logical_or(i == 0, be_ref[i] != be_ref[jnp.maximum(i - 1, 0)]))
    def _():
        wgb_ref[...] = wg_ref[0].astype(BF16)
        wub_ref[...] = wu_ref[0].astype(BF16)
        wdb_ref[...] = wd_ref[0].astype(BF16)

    @pl.when(i < nv_ref[0])
    def _():
        u = x_ref[...]
        x = jnp.concatenate([lax.bitcast_convert_type(u << 16, F32),
                             lax.bitcast_convert_type(u & jnp.uint32(0xFFFF0000), F32)], axis=1).astype(BF16)
        hb = _silu(_dot(x, wgb_ref[...])) * _dot(x, wub_ref[...])
        o_ref[...] = _dot(hb.astype(BF16), wdb_ref[...])

    @pl.when(i >= nv_ref[0])
    def _():
        o_ref[...] = jnp.zeros(o_ref.shape, F32)


def _experts(xs, block_e, n_valid, w_gate, w_up, w_down):
    n_slots = xs.shape[0]
    n_blocks = n_slots // MOE_BLK
    wmap = lambda i, be, nv: (be[i], 0, 0)
    return pl.pallas_call(
        _expert_kernel,
        grid_spec=pltpu.PrefetchScalarGridSpec(
            num_scalar_prefetch=2,
            grid=(n_blocks,),
            in_specs=[pl.BlockSpec((MOE_BLK, D_MODEL // 2), lambda i, be, nv: (i, 0)),
                      pl.BlockSpec((1, D_MODEL, D_EXPERT), wmap),
                      pl.BlockSpec((1, D_MODEL, D_EXPERT), wmap),
                      pl.BlockSpec((1, D_EXPERT, D_MODEL), wmap)],
            out_specs=pl.BlockSpec((MOE_BLK, D_MODEL), lambda i, be, nv: (i, 0)),
            scratch_shapes=[pltpu.VMEM((D_MODEL, D_EXPERT), BF16), pltpu.VMEM((D_MODEL, D_EXPERT), BF16),
                            pltpu.VMEM((D_EXPERT, D_MODEL), BF16)]),
        out_shape=jax.ShapeDtypeStruct((n_slots, D_MODEL), F32),
        compiler_params=_cparams(("arbitrary",)),
        name="experts",
    )(block_e, n_valid, xs, w_gate, w_up, w_down)


def _combine_kernel(h_ref, y0_ref, y1_ref, w_ref, m5_ref, g_ref, o_ref, *, final):
    w = w_ref[...]
    y = y0_ref[...] * w[:, 0:1] + y1_ref[...] * w[:, 1:2]
    h = h_ref[...] + m5_ref[0] * y
    if final:
        h = _rms(h, g_ref[...], NORM_EPS)
    o_ref[...] = h


def _combine(h, y01, wts, mod5, g_final, *, nb, nt, ctx_first, final):
    tok = lambda b, r: (b * nt + r, 0)
    if ctx_first:
        modmap = lambda b, r: (jnp.where(r == 0, nb, b), 0, 0)
    else:
        modmap = lambda b, r: (b, 0, 0)
    return pl.pallas_call(
        functools.partial(_combine_kernel, final=final),
        grid=(nb, nt),
        in_specs=[pl.BlockSpec((TM, D_MODEL), tok), pl.BlockSpec((TM, D_MODEL), tok),
                  pl.BlockSpec((TM, D_MODEL), lambda b, r: ((nb + b) * nt + r, 0)),
                  pl.BlockSpec((TM, 128), tok), pl.BlockSpec((1, 1, D_MODEL), modmap),
                  pl.BlockSpec((1, D_MODEL), lambda b, r: (0, 0))],
        out_specs=pl.BlockSpec((TM, D_MODEL), tok),
        out_shape=jax.ShapeDtypeStruct(h.shape, F32),
        compiler_params=_cparams(("parallel", "parallel")),
        name="combine",
    )(h, y01, y01, wts, mod5, g_final)


def _sc_gather(table, idx):
    n, d = idx.shape[0], table.shape[1]
    steps = n // SC_ROWS_PER_STEP
    idx_p = jnp.pad(idx.reshape(steps, SC_ROWS_PER_STEP), ((0, 0), (0, SC_INDEX_TILE - SC_ROWS_PER_STEP)))
    mesh = plsc.VectorSubcoreMesh(core_axis_name="core", subcore_axis_name="subcore")

    @pl.kernel(out_type=jax.ShapeDtypeStruct((n, d), table.dtype), mesh=mesh)
    def gather_kernel(x_hbm, i_hbm, o_hbm):
        def body(i_vmem, o_vmem):
            pltpu.sync_copy(x_hbm.at[i_vmem.at[0, pl.ds(0, SC_ROWS_PER_STEP)]], o_vmem)

        pltpu.emit_pipeline(
            body,
            grid=(steps,),
            in_specs=[pl.BlockSpec((1, SC_INDEX_TILE), index_map=lambda i: (i, 0))],
            out_specs=[pl.BlockSpec((SC_ROWS_PER_STEP, d), index_map=lambda i: (i, 0))],
            core_axis_name=("core", "subcore"),
            dimension_semantics=(pltpu.PARALLEL,),
        )(i_hbm, o_hbm)

    return gather_kernel(table, idx_p)


def _moe(h, g2, shift, scale, mod5, wh, wm, rbias, w_gate, w_up, w_down, g_final, *, layer, nb, nt, ctx_first,
         final):
    n = h.shape[0]
    x_bf, route = _router(h, g2, shift, scale, wh, wm, rbias, nb=nb, nt=nt, ctx_first=ctx_first)
    idx = route[:, :MOE_TOPK].astype(jnp.int32)
    wts = route[:, MOE_TOPK:2 * MOE_TOPK]
    n_as = n * MOE_TOPK
    flat_e = idx.reshape(n_as)
    onehot = (flat_e[:, None] == jnp.arange(MOE_EXPERTS, dtype=jnp.int32)[None, :]).astype(jnp.int32)
    csum = jnp.cumsum(onehot, axis=0)
    counts = csum[-1]
    rank = jnp.sum(jnp.where(onehot > 0, csum - 1, 0), axis=1)
    padded = (counts + MOE_BLK - 1) // MOE_BLK * MOE_BLK
    pad_end = jnp.cumsum(padded)
    pad_start = pad_end - padded
    slot = pad_start[flat_e] + rank
    n_blocks = (n_as + MOE_EXPERTS * (MOE_BLK - 1) + MOE_BLK - 1) // MOE_BLK
    n_slots = n_blocks * MOE_BLK
    slot_tok = (jnp.arange(n_slots, dtype=jnp.int32) % n).at[slot].set(
        jnp.arange(n_as, dtype=jnp.int32) // MOE_TOPK, unique_indices=True)
    starts = jnp.arange(n_blocks, dtype=jnp.int32) * MOE_BLK
    block_e = jnp.minimum(jnp.sum((pad_end[None, :] <= starts[:, None]).astype(jnp.int32), axis=1),
                          MOE_EXPERTS - 1)
    n_valid = (pad_end[-1:] // MOE_BLK).astype(jnp.int32)
    xs = _sc_gather(x_bf, slot_tok)
    ys = _experts(xs, block_e + layer * MOE_EXPERTS, n_valid, w_gate, w_up, w_down)
    y01 = _sc_gather(ys, slot.reshape(n, MOE_TOPK).T.reshape(n_as))
    wts_pad = jnp.pad(wts.astype(F32), ((0, 0), (0, 128 - MOE_TOPK)))
    return _combine(h, y01, wts_pad, mod5, g_final, nb=nb, nt=nt, ctx_first=ctx_first, final=final)


_ROT_SRC = np.array(list(range(8, 16)) + list(range(0, 8)) + list(range(24, 32)) + list(range(16, 24)))
_ROT_SIGN = np.array([-1.0] * 8 + [1.0] * 8 + [-1.0] * 8 + [1.0] * 8, np.float32)


def _rot_cols(w):
    k = w.shape[-1] // ROPE_DIM
    src = np.concatenate([_ROT_SRC + ROPE_DIM * i for i in range(k)])
    sign = np.tile(_ROT_SIGN, k)
    return w[..., src] * sign


def _rope_tables(n_ctx, n_lat):
    rows = n_lat // GRID_W
    row = jnp.repeat(jnp.arange(rows, dtype=F32), GRID_W)
    col = (jnp.arange(rows * GRID_W) % GRID_W).astype(F32)
    nf = ROPE_DIM // 4
    inv = ROPE_BASE ** (-jnp.arange(nf, dtype=F32) / nf)
    ar = row[:, None] * inv
    ac = col[:, None] * inv
    ang = jnp.concatenate([ar, ar, ac, ac], axis=-1)
    cos = jnp.concatenate([jnp.ones((n_ctx, ROPE_DIM), F32), jnp.cos(ang)], axis=0)
    sin = jnp.concatenate([jnp.zeros((n_ctx, ROPE_DIM), F32), jnp.sin(ang)], axis=0)
    return cos, sin


def _block_diag(blocks):
    g, a, b = blocks.shape
    tiled = jnp.tile(blocks.reshape(g * a, b), (1, g))
    rows = lax.broadcasted_iota(jnp.int32, (g * a, g * b), 0) // a
    cols = lax.broadcasted_iota(jnp.int32, (g * a, g * b), 1) // b
    return jnp.where(rows == cols, tiled, 0.0)


def _pick_tk(t_all):
    best = 128
    for tk in range(128, ATTN_TK_MAX + 1, 128):
        if t_all % tk == 0:
            best = tk
    return best


def kernel(x, c, ctx, c_ctx, w_mod, b_mod, norm1_g, norm2_g, w_in, mla_q_norm_g, mla_kv_norm_g, mla_w_uq, mla_w_ukv, rw_mu, rw_w0, rw_w2, rw_a0, rw_a2, rw_g2, rw_k_k, rw_k_a, rw_r_k, rw_lnx_g, rw_lnx_b, s5_a_re, s5_a_im, s5_log_dt, s5_b_re, s5_b_im, s5_c_re, s5_c_im, s5_d, s5_glu_w, s5_glu_b, diff_lq1, diff_lk1, diff_lq2, diff_lk2, diff_subln_g, w_branch, w_out, router_g_w, router_g_b, router_e_w, router_e_b, exp_w_gate, exp_w_up, exp_w_down, final_norm_g):
    nb, n_lat, d = x.shape
    n_ctx = ctx.shape[1]
    depth = w_mod.shape[0]
    t_all = n_ctx + n_lat
    assert d == D_MODEL and n_ctx == TM and n_lat % TM == 0
    tpb = t_all // TM
    tk = _pick_tk(t_all)
    hw = RW_HEADS * RW_HEAD

    cos, sin = _rope_tables(n_ctx, n_lat)
    mla_scale = (MLA_NOPE + MLA_ROPE) ** -0.5 * LOG2E
    z32 = jnp.zeros((t_all, 32), F32)
    cq_tab = jnp.concatenate([jnp.ones((t_all, 64), F32), cos, z32], axis=1) * mla_scale
    sq_tab = jnp.concatenate([jnp.zeros((t_all, 64), F32), sin, z32], axis=1) * mla_scale
    ck_tab = jnp.concatenate([cos, sin, jnp.zeros((t_all, 64), F32)], axis=1)
    dcos = jnp.tile(cos, (1, 8))
    dsin = jnp.tile(sin, (1, 8))

    c_rows = jnp.concatenate([c, c_ctx[None, :], jnp.zeros((8 - nb - 1, d), F32)], axis=0)

    h = jnp.concatenate([ctx, x], axis=1).reshape(nb * t_all, d)

    for l in range(depth):
        last = l == depth - 1
        r0, nt = (1, tpb - 1) if last else (0, tpb)

        mod = _mm(c_rows, w_mod[l].astype(BF16), b_mod[l][None, :], tm=8, tn=1536, pre_silu=True, name="mod")
        mods = [mod[:nb + 1, i * d:(i + 1) * d].reshape(nb + 1, 1, d) for i in range(6)]

        wi = w_in[l]
        o_rw, o_s5, o_df, o_gt = 416, 1440, 1696, 2464
        w_kr = wi[:, 384:416]
        w_dq, w_dk, w_dv = wi[:, o_df:o_df + 256], wi[:, o_df + 256:o_df + 512], wi[:, o_df + 512:o_df + 768]
        w_ext = jnp.concatenate(
            [wi[:, o_rw:o_s5],
             wi[:, :416], _rot_cols(w_kr), jnp.zeros((d, 64), F32),
             wi[:, o_s5:o_df],
             w_dq, w_dk, w_dv, _rot_cols(w_dq), _rot_cols(w_dk),
             wi[:, o_gt:]], axis=1).astype(BF16)
        p = _inproj(h, norm1_g[l][None, :], mods[0], mods[1], w_ext, nb=nb, tpb=tpb)

        wq = mla_w_uq[l].reshape(MLA_Q_LORA, MLA_HEADS, MLA_NOPE + MLA_ROPE)
        zq = jnp.zeros((MLA_Q_LORA, MLA_HEADS, 32), F32)
        wa = jnp.concatenate([wq, zq], axis=2).reshape(MLA_Q_LORA, 512).astype(BF16)
        wb = jnp.concatenate([jnp.zeros((MLA_Q_LORA, MLA_HEADS, 64), F32), _rot_cols(wq[:, :, MLA_NOPE:]), zq],
                             axis=2).reshape(MLA_Q_LORA, 512).astype(BF16)
        wkv = mla_w_ukv[l].reshape(MLA_KV_LORA, MLA_HEADS, MLA_NOPE + MLA_V)
        wk = jnp.concatenate([wkv[:, :, :MLA_NOPE], jnp.zeros((MLA_KV_LORA, MLA_HEADS, 64), F32)],
                             axis=2).reshape(MLA_KV_LORA, 512).astype(BF16)
        wv = wkv[:, :, MLA_NOPE:].reshape(MLA_KV_LORA, MLA_HEADS * MLA_V).astype(BF16)
        pk_np = np.zeros((128, 512), np.float32)
        for hh in range(MLA_HEADS):
            for i in range(32):
                pk_np[i, hh * 128 + 64 + i] = 1.0
                pk_np[32 + i, hh * 128 + 64 + i] = 1.0
        pk = jnp.asarray(pk_np, BF16)
        q_m, k_m, v_m = _mla_prep(p, mla_q_norm_g[l][None, :], mla_kv_norm_g[l][None, :], wa, wb, wk, wv, pk,
                                  cq_tab, sq_tab, ck_tab, nb=nb, tpb=tpb)
        ya_lat = _flash_mla(q_m, k_m, v_m, nb=nb, t_all=t_all, q_start=n_ctx, q_len=n_lat, k_len=t_all, tk=tk)

        q_d, k_d, v_d = _diff_prep(p, dcos, dsin, nb=nb, tpb=tpb)
        lam_init = 0.8 - 0.6 * math.exp(-0.3 * l)
        lam = (jnp.exp(jnp.sum(diff_lq1[l] * diff_lk1[l])) - jnp.exp(jnp.sum(diff_lq2[l] * diff_lk2[l])) + lam_init)
        lam_row = jnp.full((1, 256), lam, F32)
        g_row = jnp.tile(diff_subln_g[l], DIFF_HEADS)[None, :]
        yd_lat = _flash_diff(q_d, k_d, v_d, lam_row, g_row, lam_init=lam_init, nb=nb, t_all=t_all,
                             q_start=n_ctx, q_len=n_lat, k_len=t_all, tk=tk)
        if last:
            ya, yd = ya_lat, yd_lat
        else:
            ya_ctx = _flash_mla(q_m, k_m, v_m, nb=nb, t_all=t_all, q_start=0, q_len=n_ctx, k_len=n_ctx, tk=n_ctx)
            yd_ctx = _flash_diff(q_d, k_d, v_d, lam_row, g_row, lam_init=lam_init, nb=nb, t_all=t_all,
                                 q_start=0, q_len=n_ctx, k_len=n_ctx, tk=n_ctx)
            comb = lambda a_c, a_l: jnp.concatenate(
                [a_c.reshape(nb, n_ctx, -1), a_l.reshape(nb, n_lat, -1)], axis=1).reshape(nb * t_all, -1)
            ya, yd = comb(ya_ctx, ya_lat), comb(yd_ctx, yd_lat)

        wlo = jnp.zeros((128, 4 * hw), F32)
        wlo = wlo.at[:64, 0:hw].set(rw_w2[l, 0]).at[:64, hw:2 * hw].set(rw_w2[l, 1])
        wlo = wlo.at[64:, 2 * hw:3 * hw].set(rw_a2[l, 0]).at[64:, 3 * hw:].set(rw_a2[l, 1])
        vecs = jnp.stack([rw_k_k[l], rw_k_a[l], rw_r_k[l].reshape(hw), rw_w0[l, 0], rw_w0[l, 1],
                          rw_a0[l, 0], rw_a0[l, 1], jnp.zeros((hw,), F32)], axis=0)
        (r_, v_, kk_, w0_, k0_, b0_, w1_, k1_, b1_, bonus, gate_rw) = _rw_prep(
            p, rw_mu[l][None, :], wlo.astype(BF16), rw_g2[l].astype(BF16), vecs, nb=nb, tpb=tpb, r0=0, nt=tpb)
        yf, yb_ = _rw_scan(r_, v_, kk_, w0_, k0_, b0_, w1_, k1_, b1_, nb=nb, t_all=t_all, n_ctx=n_ctx)
        ln_vecs = jnp.concatenate([rw_lnx_g[l][None, :], rw_lnx_b[l][None, :], jnp.zeros((6, hw), F32)], axis=0)
        if last:
            trim = lambda a: a.reshape(nb, t_all, hw)[:, n_ctx:].reshape(nb * n_lat, hw)
            bonus, gate_rw = trim(bonus), trim(gate_rw)
        y_rw = _rw_post(yf, yb_, bonus, gate_rw, ln_vecs, nb=nb, tpb_y=tpb, r0_y=r0, nt=nt)

        bbs, abs_, cfs = [], [], []
        for dd in range(2):
            lr, li = s5_a_re[l, dd], s5_a_im[l, dd]
            dt = jnp.exp(s5_log_dt[l, dd])[:, None]
            mag = jnp.exp(lr * dt)
            ab_re, ab_im = mag * jnp.cos(li * dt), mag * jnp.sin(li * dt)
            den = lr * lr + li * li
            nr, ni = ab_re - 1.0, ab_im
            cf_re = (nr * lr + ni * li) / den
            cf_im = (ni * lr - nr * li) / den
            bre, bim = s5_b_re[l, dd], s5_b_im[l, dd]
            bb_re = cf_re[..., None] * bre - cf_im[..., None] * bim
            bb_im = cf_re[..., None] * bim + cf_im[..., None] * bre
            bbs.append(jnp.concatenate([_block_diag(bb_re.transpose(0, 2, 1)),
                                        _block_diag(bb_im.transpose(0, 2, 1))], axis=1))
            abs_.append(jnp.concatenate([ab_re.reshape(-1), ab_im.reshape(-1)])[None, :])
            cfs.append(jnp.concatenate([_block_diag(s5_c_re[l, dd].transpose(0, 2, 1)),
                                        -_block_diag(s5_c_im[l, dd].transpose(0, 2, 1))], axis=0))
        yf_s5, yb_s5 = _s5_scan(p, jnp.concatenate(bbs, axis=1).astype(BF16), jnp.stack(abs_, axis=0),
                                cfs[0].astype(BF16), cfs[1].astype(BF16), nb=nb, t_all=t_all, n_ctx=n_ctx)
        y_s5 = _s5_post(yf_s5, yb_s5, p, s5_d[l].reshape(1, 256), s5_glu_w[l].astype(BF16),
                        s5_glu_b[l][None, :], nb=nb, tpb=tpb, r0=r0, nt=nt)

        h = _merge(ya, y_rw, y_s5, yd, p, h, mods[2], w_branch[l].astype(BF16), w_out[l].astype(BF16),
                   nb=nb, tpb=tpb, r0=r0, nt=nt)

        wr = jnp.concatenate([router_g_w[l], router_e_w[l], jnp.zeros((d, 128 - MOE_GROUPS - MOE_EXPERTS), F32)], axis=1)
        wr_h = wr.astype(BF16)
        wr_m = (wr - wr_h.astype(F32)).astype(BF16)
        rbias = jnp.concatenate([router_g_b[l], router_e_b[l],
                                 jnp.zeros((128 - MOE_GROUPS - MOE_EXPERTS,), F32)])[None, :]
        h = _moe(h, norm2_g[l][None, :], mods[3], mods[4], mods[5], wr_h, wr_m, rbias,
                 exp_w_gate.reshape(depth * MOE_EXPERTS, d, D_EXPERT), exp_w_up.reshape(depth * MOE_EXPERTS, d, D_EXPERT),
                 exp_w_down.reshape(depth * MOE_EXPERTS, D_EXPERT, d),
                 final_norm_g[None, :], layer=l, nb=nb, nt=nt, ctx_first=not last, final=last)

    return h.reshape(nb, n_lat, d)
```

```python
import functools
import math

import jax
import jax.numpy as jnp
import numpy as np
from jax import lax
from jax.experimental import pallas as pl
from jax.experimental.pallas import tpu as pltpu
from jax.experimental.pallas import tpu_sc as plsc

F32 = jnp.float32
BF16 = jnp.bfloat16

TM = 256
VMEM_LIMIT = 48 * 1024 * 1024

D_MODEL = 1024
GRID_W = 64
ROPE_DIM = 32
ROPE_BASE = 10000.0
NORM_EPS = 1e-6
MLA_HEADS, MLA_NOPE, MLA_ROPE, MLA_V = 4, 64, 32, 64
MLA_Q_LORA, MLA_KV_LORA = 256, 128
RW_HEADS, RW_HEAD = 4, 64
RW_LN_EPS = 64e-5
S5_GROUPS, S5_GROUP_CH, S5_STATE = 16, 16, 64
DIFF_HEADS, DIFF_HD = 4, 32
DIFF_EPS = 1e-5
MOE_GROUPS, MOE_PER_GROUP, MOE_TOPK = 4, 8, 2
MOE_EXPERTS = MOE_GROUPS * MOE_PER_GROUP
D_EXPERT = 512
MOE_BLK = 256
RW_CHUNK = 128
S5_CHUNK = 128
ATTN_TK_MAX = 2816
SC_INDEX_TILE = 128
SC_ROWS_PER_STEP = 32
LOG2E = math.log2(math.e)

EXT_RW, EXT_MLA, EXT_S5, EXT_DIFF, EXT_GATE = 0, 1024, 1536, 1792, 3072
N_EXT = 7168


def _cparams(sem, vmem=VMEM_LIMIT):
    return pltpu.CompilerParams(dimension_semantics=sem, vmem_limit_bytes=vmem)


def _dot(a, b):
    return jnp.dot(a, b, preferred_element_type=F32)


def _dot_nt(a, b):
    return lax.dot_general(a, b, (((1,), (1,)), ((), ())), preferred_element_type=F32)


def _split_dot(x, e):
    hi = x.astype(BF16)
    mid = (x - hi.astype(F32)).astype(BF16)
    return _dot(hi, e) + _dot(mid, e)


def _block_ones(n, blk):
    r = lax.broadcasted_iota(jnp.int32, (n, n), 0) // blk
    c = lax.broadcasted_iota(jnp.int32, (n, n), 1) // blk
    return (r == c).astype(BF16)


def _sigmoid(x):
    return 1.0 / (1.0 + jnp.exp(-x))


def _silu(x):
    return x * _sigmoid(x)


def _softplus(x):
    return jnp.maximum(x, 0.0) + jnp.log(1.0 + jnp.exp(-jnp.abs(x)))


def _rms(x, g, eps):
    return x * lax.rsqrt(jnp.mean(x * x, axis=-1, keepdims=True) + eps) * g


def _mod_kernel(x_ref, w_ref, b_ref, o_ref):
    o_ref[...] = _dot(_silu(x_ref[...]).astype(BF16), w_ref[...]) + b_ref[...]


def _mod(x, w, b, *, tn):
    m, k = x.shape
    n = w.shape[1]
    return pl.pallas_call(
        _mod_kernel,
        grid=(n // tn,),
        in_specs=[pl.BlockSpec((m, k), lambda j: (0, 0)),
                  pl.BlockSpec((k, tn), lambda j: (0, j)),
                  pl.BlockSpec((1, tn), lambda j: (0, j))],
        out_specs=pl.BlockSpec((m, tn), lambda j: (0, j)),
        out_shape=jax.ShapeDtypeStruct((m, n), F32),
        compiler_params=_cparams(("parallel",)),
        name="mod",
    )(x, w, b)


def _inproj_kernel(h_ref, g_ref, sh_ref, sc_ref, w_ref, o_ref):
    x = _rms(h_ref[...], g_ref[...], NORM_EPS)
    xn = (x * (1.0 + sc_ref[0]) + sh_ref[0]).astype(BF16)
    o_ref[...] = _dot(xn, w_ref[...])


def _inproj(h, g, shift, scale, w_ext, *, nb, tpb):
    n = h.shape[0]
    tn = N_EXT // 2
    modmap = lambda j, b, r: (jnp.where(r == 0, nb, b), 0, 0)
    return pl.pallas_call(
        _inproj_kernel,
        grid=(N_EXT // tn, nb, tpb),
        in_specs=[pl.BlockSpec((TM, D_MODEL), lambda j, b, r: (b * tpb + r, 0)),
                  pl.BlockSpec((1, D_MODEL), lambda j, b, r: (0, 0)),
                  pl.BlockSpec((1, 1, D_MODEL), modmap),
                  pl.BlockSpec((1, 1, D_MODEL), modmap),
                  pl.BlockSpec((D_MODEL, tn), lambda j, b, r: (0, j))],
        out_specs=pl.BlockSpec((TM, tn), lambda j, b, r: (b * tpb + r, j)),
        out_shape=jax.ShapeDtypeStruct((n, N_EXT), F32),
        compiler_params=_cparams(("parallel", "parallel", "parallel")),
        name="inproj",
    )(h, g, shift, scale, w_ext)


def _mla_prep_kernel(p_ref, gq_ref, gkv_ref, wa_ref, wb_ref, wk_ref, wv_ref, pk_ref,
                     cq_ref, sq_ref, ck_ref, q_ref, k_ref, v_ref):
    seg = p_ref[...]
    nq = _rms(seg[:, :MLA_Q_LORA], gq_ref[...], NORM_EPS).astype(BF16)
    nkv = _rms(seg[:, MLA_Q_LORA:MLA_Q_LORA + MLA_KV_LORA], gkv_ref[...], NORM_EPS).astype(BF16)
    cq = jnp.concatenate([cq_ref[...]] * MLA_HEADS, axis=1)
    sq = jnp.concatenate([sq_ref[...]] * MLA_HEADS, axis=1)
    q = _dot(nq, wa_ref[...]) * cq + _dot(nq, wb_ref[...]) * sq
    q_ref[...] = q.astype(BF16)
    kr = (seg[:, 384:512] * ck_ref[...]).astype(BF16)
    k_ref[...] = (_dot(nkv, wk_ref[...]) + _dot(kr, pk_ref[...])).astype(BF16)
    v_ref[...] = _dot(nkv, wv_ref[...]).astype(BF16)


def _mla_prep(p, gq, gkv, wa, wb, wk, wv, pk, cq_tab, sq_tab, ck_tab, *, nb, tpb):
    n = p.shape[0]
    tok = lambda b, r: (b * tpb + r, 0)
    pos = lambda b, r: (r, 0)
    full = lambda b, r: (0, 0)
    return pl.pallas_call(
        _mla_prep_kernel,
        grid=(nb, tpb),
        in_specs=[pl.BlockSpec((TM, 512), lambda b, r: (b * tpb + r, EXT_MLA // 512)),
                  pl.BlockSpec((1, MLA_Q_LORA), full),
                  pl.BlockSpec((1, MLA_KV_LORA), full),
                  pl.BlockSpec((MLA_Q_LORA, 512), full),
                  pl.BlockSpec((MLA_Q_LORA, 512), full),
                  pl.BlockSpec((MLA_KV_LORA, 512), full),
                  pl.BlockSpec((MLA_KV_LORA, 256), full),
                  pl.BlockSpec((128, 512), full),
                  pl.BlockSpec((TM, 128), pos),
                  pl.BlockSpec((TM, 128), pos),
                  pl.BlockSpec((TM, 128), pos)],
        out_specs=[pl.BlockSpec((TM, 512), tok),
                   pl.BlockSpec((TM, 512), tok),
                   pl.BlockSpec((TM, 256), tok)],
        out_shape=[jax.ShapeDtypeStruct((n, 512), BF16),
                   jax.ShapeDtypeStruct((n, 512), BF16),
                   jax.ShapeDtypeStruct((n, 256), BF16)],
        compiler_params=_cparams(("parallel", "parallel")),
        name="mla_prep",
    )(p, gq, gkv, wa, wb, wk, wv, pk, cq_tab, sq_tab, ck_tab)


def _flash_mla_kernel(q_ref, k_ref, v_ref, o_ref, m_ref, l_ref, acc_ref):
    kk = pl.program_id(2)
    tq = q_ref.shape[0]

    @pl.when(kk == 0)
    def _():
        m_ref[...] = jnp.full(m_ref.shape, -jnp.inf, F32)
        l_ref[...] = jnp.zeros(l_ref.shape, F32)
        acc_ref[...] = jnp.zeros(acc_ref.shape, F32)

    ps, alphas = [], []
    for h in range(MLA_HEADS):
        s = _dot_nt(q_ref[:, h * 128:(h + 1) * 128], k_ref[:, h * 128:(h + 1) * 128])
        m_prev = m_ref[h]
        m_new = jnp.maximum(m_prev, jnp.max(s, axis=-1, keepdims=True))
        alpha = jnp.exp2(m_prev - m_new)
        p = jnp.exp2(s - m_new)
        l_ref[h] = alpha * l_ref[h] + jnp.sum(p, axis=-1, keepdims=True)
        m_ref[h] = m_new
        alphas.append(alpha)
        ps.append(p.astype(BF16))
    pv = _dot(jnp.concatenate(ps, axis=0), v_ref[...])
    acc_ref[...] = jnp.concatenate(alphas, axis=0) * acc_ref[...] + pv

    @pl.when(kk == pl.num_programs(2) - 1)
    def _():
        lane = lax.broadcasted_iota(jnp.int32, (tq, MLA_HEADS * MLA_V), 1)
        out = jnp.zeros((tq, MLA_HEADS * MLA_V), F32)
        for h in range(MLA_HEADS):
            out = jnp.where(lane // MLA_V == h, acc_ref[h * tq:(h + 1) * tq, :] / l_ref[h], out)
        o_ref[...] = out


def _flash_mla(q, k, v, *, nb, t_all, q_start, q_len, k_len, tk):
    tq = TM
    tpb_q, tpb_k = t_all // tq, t_all // tk
    q0 = q_start // tq
    nq = q_len // tq
    return pl.pallas_call(
        _flash_mla_kernel,
        grid=(nb, nq, k_len // tk),
        in_specs=[pl.BlockSpec((tq, 512), lambda b, i, kk: (b * tpb_q + q0 + i, 0)),
                  pl.BlockSpec((tk, 512), lambda b, i, kk: (b * tpb_k + kk, 0)),
                  pl.BlockSpec((tk, 256), lambda b, i, kk: (b * tpb_k + kk, 0))],
        out_specs=pl.BlockSpec((tq, 256), lambda b, i, kk: (b * nq + i, 0)),
        out_shape=jax.ShapeDtypeStruct((nb * q_len, 256), F32),
        scratch_shapes=[pltpu.VMEM((MLA_HEADS, tq, 1), F32), pltpu.VMEM((MLA_HEADS, tq, 1), F32),
                        pltpu.VMEM((MLA_HEADS * tq, 256), F32)],
        compiler_params=_cparams(("parallel", "parallel", "arbitrary")),
        name="flash_mla",
    )(q, k, v)


def _diff_prep(p, cos_tab, sin_tab, *, nb, tpb):
    n = p.shape[0]
    tok = lambda b, r: (b * tpb + r, 0)
    pos = lambda b, r: (r, 0)
    return pl.pallas_call(
        _diff_prep_kernel_cols,
        grid=(nb, tpb),
        in_specs=[pl.BlockSpec((TM, 256), lambda b, r, c=c: (b * tpb + r, EXT_DIFF // 256 + c))
                  for c in range(5)]
                 + [pl.BlockSpec((TM, 256), pos), pl.BlockSpec((TM, 256), pos)],
        out_specs=[pl.BlockSpec((TM, 256), tok)] * 3,
        out_shape=[jax.ShapeDtypeStruct((n, 256), BF16)] * 3,
        compiler_params=_cparams(("parallel", "parallel")),
        name="diff_prep",
    )(p, p, p, p, p, cos_tab, sin_tab)


def _diff_prep_kernel_cols(q_in, k_in, v_in, qr_in, kr_in, cos_ref, sin_ref, q_ref, k_ref, v_ref):
    cos, sin = cos_ref[...], sin_ref[...]
    scale = DIFF_HD ** -0.5 * LOG2E
    q_ref[...] = ((q_in[...] * cos + qr_in[...] * sin) * scale).astype(BF16)
    k_ref[...] = (k_in[...] * cos + kr_in[...] * sin).astype(BF16)
    v_ref[...] = v_in[...].astype(BF16)


def _flash_diff_kernel(q_ref, k_ref, v_ref, lam_ref, g_ref, o_ref, qs_ref, m_ref, l_ref, acc_ref, *, lam_init):
    kk = pl.program_id(2)
    tq = q_ref.shape[0]
    nsm = 2 * DIFF_HEADS

    @pl.when(kk == 0)
    def _():
        m_ref[...] = jnp.full(m_ref.shape, -jnp.inf, F32)
        l_ref[...] = jnp.zeros(l_ref.shape, F32)
        acc_ref[...] = jnp.zeros(acc_ref.shape, F32)
        q = q_ref[...]
        lane = lax.broadcasted_iota(jnp.int32, (tq, 256), 1)
        for i in range(nsm):
            qs_ref[i * tq:(i + 1) * tq, :] = jnp.where((lane // DIFF_HD) == i, q, jnp.zeros_like(q))

    k = k_ref[...]
    ps, alphas = [], []
    for i in range(nsm):
        rows = slice(i * tq, (i + 1) * tq)
        s = _dot_nt(qs_ref[rows, :], k)
        m_prev = m_ref[rows, :]
        m_new = jnp.maximum(m_prev, jnp.max(s, axis=-1, keepdims=True))
        alpha = jnp.exp2(m_prev - m_new)
        p = jnp.exp2(s - m_new)
        l_ref[rows, :] = alpha * l_ref[rows, :] + jnp.sum(p, axis=-1, keepdims=True)
        m_ref[rows, :] = m_new
        alphas.append(alpha)
        ps.append(p.astype(BF16))
    pv = _dot(jnp.concatenate(ps, axis=0), v_ref[...])
    acc_ref[...] = jnp.concatenate(alphas, axis=0) * acc_ref[...] + pv

    @pl.when(kk == pl.num_programs(2) - 1)
    def _():
        lane = lax.broadcasted_iota(jnp.int32, (tq, 256), 1)
        o = jnp.zeros((tq, 256), F32)
        for h in range(DIFF_HEADS):
            r0, r1 = (2 * h) * tq, (2 * h + 1) * tq
            o0 = acc_ref[r0:r0 + tq, :] / l_ref[r0:r0 + tq, :]
            o1 = acc_ref[r1:r1 + tq, :] / l_ref[r1:r1 + tq, :]
            o = jnp.where((lane // (2 * DIFF_HD)) == h, o0 - lam_ref[...] * o1, o)
        ms = _split_dot(o * o, _block_ones(256, 2 * DIFF_HD)) * (1.0 / (2 * DIFF_HD))
        o_ref[...] = o * lax.rsqrt(ms + DIFF_EPS) * g_ref[...] * (1.0 - lam_init)


def _flash_diff(q, k, v, lam_row, g_row, *, lam_init, nb, t_all, q_start, q_len, k_len, tk):
    tq = TM
    tpb_q, tpb_k = t_all // tq, t_all // tk
    q0 = q_start // tq
    nq = q_len // tq
    return pl.pallas_call(
        functools.partial(_flash_diff_kernel, lam_init=lam_init),
        grid=(nb, nq, k_len // tk),
        in_specs=[pl.BlockSpec((tq, 256), lambda b, i, kk: (b * tpb_q + q0 + i, 0)),
                  pl.BlockSpec((tk, 256), lambda b, i, kk: (b * tpb_k + kk, 0)),
                  pl.BlockSpec((tk, 256), lambda b, i, kk: (b * tpb_k + kk, 0)),
                  pl.BlockSpec((1, 256), lambda b, i, kk: (0, 0)),
                  pl.BlockSpec((1, 256), lambda b, i, kk: (0, 0))],
        out_specs=pl.BlockSpec((tq, 256), lambda b, i, kk: (b * nq + i, 0)),
        out_shape=jax.ShapeDtypeStruct((nb * q_len, 256), F32),
        scratch_shapes=[pltpu.VMEM((2 * DIFF_HEADS * tq, 256), BF16),
                        pltpu.VMEM((2 * DIFF_HEADS * tq, 1), F32),
                        pltpu.VMEM((2 * DIFF_HEADS * tq, 1), F32),
                        pltpu.VMEM((2 * DIFF_HEADS * tq, 256), F32)],
        compiler_params=_cparams(("parallel", "parallel", "arbitrary")),
        name="flash_diff",
    )(q, k, v, lam_row, g_row)


def _rw_prep_kernel(p_ref, prev_ref, next_ref, mu_ref, wlo_ref, g2_ref, vec_ref,
                    r_ref, v_ref, kk_ref, w0_ref, k0_ref, b0_ref, w1_ref, k1_ref, b1_ref,
                    bonus_ref, gate_ref, *, r0, lat_last):
    r = pl.program_id(1) + r0
    p = p_ref[...]
    row = lax.broadcasted_iota(jnp.int32, p.shape, 0)
    first_tile = jnp.logical_or(r == 0, r == 1)
    last_tile = jnp.logical_or(r == 0, r == lat_last)
    prev_row = jnp.where(first_tile, 0.0, prev_ref[7:8, :])
    next_row = jnp.where(last_tile, 0.0, next_ref[0:1, :])
    up = jnp.where(row == 0, prev_row, pltpu.roll(p, 1, 0))
    dn = jnp.where(row == TM - 1, next_row, pltpu.roll(p, TM - 1, 0))
    z = p + (0.5 * (up + dn) - p) * mu_ref[...]

    hw = RW_HEADS * RW_HEAD
    rr, k, v = z[:, :hw], z[:, hw:2 * hw], z[:, 2 * hw:3 * hw]
    lo = z[:, 3 * hw:3 * hw + 128]
    lane = lax.broadcasted_iota(jnp.int32, lo.shape, 1)
    lo = jnp.where(lane < 64, jnp.tanh(lo), lo).astype(BF16)
    wa = _dot(lo, wlo_ref[...])
    gate_ref[...] = _dot(_sigmoid(z[:, 3 * hw + 128:]).astype(BF16), g2_ref[...])

    e4 = _block_ones(hw, RW_HEAD)
    k_k, k_a, r_k = vec_ref[0:1, :], vec_ref[1:2, :], vec_ref[2:3, :]
    kk = k * k_k
    nrm = jnp.maximum(jnp.sqrt(_split_dot(kk * kk, e4)), 1e-12)
    kk = kk / nrm
    r_ref[...] = rr
    v_ref[...] = v
    kk_ref[...] = kk
    ksum = jnp.zeros_like(k)
    for d, (w_ref, kd_ref, b_ref) in enumerate(((w0_ref, k0_ref, b0_ref), (w1_ref, k1_ref, b1_ref))):
        w0 = vec_ref[3 + d:4 + d, :]
        a0 = vec_ref[5 + d:6 + d, :]
        wd = -_softplus(-(w0 + wa[:, d * hw:(d + 1) * hw])) - 0.5
        w_ref[...] = jnp.exp(-jnp.exp(wd))
        ad = _sigmoid(a0 + wa[:, (2 + d) * hw:(3 + d) * hw])
        kd = k * (1.0 + (ad - 1.0) * k_a)
        kd_ref[...] = kd
        b_ref[...] = kk * ad
        ksum = ksum + kd
    bonus_ref[...] = _split_dot(rr * ksum * r_k, e4) * v


def _rw_prep(p, mu, wlo, g2, vecs, *, nb, tpb, r0, nt):
    n_out = nb * nt * TM
    hw = RW_HEADS * RW_HEAD
    n_rows8 = p.shape[0] // 8
    tok = lambda b, r: (b * tpb + r0 + r, EXT_RW // 1024)
    prev = lambda b, r: (jnp.maximum((b * tpb + r0 + r) * (TM // 8) - 1, 0), EXT_RW // 1024)
    nxt = lambda b, r: (jnp.minimum((b * tpb + r0 + r + 1) * (TM // 8), n_rows8 - 1), EXT_RW // 1024)
    out = lambda b, r: (b * nt + r, 0)
    full = lambda b, r: (0, 0)
    return pl.pallas_call(
        functools.partial(_rw_prep_kernel, r0=r0, lat_last=tpb - 1),
        grid=(nb, nt),
        in_specs=[pl.BlockSpec((TM, 1024), tok),
                  pl.BlockSpec((8, 1024), prev),
                  pl.BlockSpec((8, 1024), nxt),
                  pl.BlockSpec((1, 1024), full),
                  pl.BlockSpec((128, 4 * hw), full),
                  pl.BlockSpec((128, hw), full),
                  pl.BlockSpec((8, hw), full)],
        out_specs=[pl.BlockSpec((TM, hw), out)] * 11,
        out_shape=[jax.ShapeDtypeStruct((n_out, hw), F32)] * 11,
        compiler_params=_cparams(("parallel", "parallel")),
        name="rw_prep",
    )(p, p, p, mu, wlo, g2, vecs)


def _rw_scan_kernel(rf, vf, kkf, wf, kf, bf, rb, vb, kkb, wb, kb, bb, yf_ref, yb_ref, s_ref, *, nb):
    c = RW_CHUNK

    @pl.when(pl.program_id(0) == 0)
    def _():
        s_ref[...] = jnp.zeros(s_ref.shape, F32)

    e2 = _block_ones(128, RW_HEAD)
    z2 = jnp.zeros((128, 128), BF16)
    rhs_pair = jnp.concatenate([jnp.concatenate([e2, z2], axis=1), jnp.concatenate([z2, e2], axis=1)], axis=0)
    e22 = jnp.concatenate([e2, e2], axis=0)
    lane = lax.broadcasted_iota(jnp.int32, (RW_HEAD, 128), 1)
    sub = lax.broadcasted_iota(jnp.int32, (RW_HEAD, 128), 0)
    diag = (lane % RW_HEAD) == sub
    sub8 = lax.broadcasted_iota(jnp.int32, (8, 128), 0)
    dirs = ((rf, vf, kkf, wf, kf, bf, yf_ref), (rb, vb, kkb, wb, kb, bb, yb_ref))

    def allreduce_rows(x):
        t = x[0:8]
        for i in range(1, 8):
            t = t + x[8 * i:8 * i + 8]
        for sh in (4, 2, 1):
            t = t + pltpu.roll(t, sh, 0)
        return t

    def col(row):
        return jnp.where(diag, row, 0.0).astype(BF16)

    def group(g, carry):
        tiles, ytiles = {}, {}
        for d, refs in enumerate(dirs):
            base = pl.multiple_of((g if d == 0 else c // 8 - 1 - g) * 8, 8)
            if d == 0:
                prev = lambda x, sh: jnp.where(sub8 >= sh, pltpu.roll(x, sh, 0), 1.0)
            else:
                prev = lambda x, sh: jnp.where(sub8 < 8 - sh, pltpu.roll(x, 8 - sh, 0), 1.0)
            last = 7 if d == 0 else 0
            for b in range(nb):
                for hp in range(2):
                    r_, v_, kk_, w_, k_, b_ = [ref[b, pl.ds(base, 8), pl.ds(hp * 128, 128)] for ref in refs[:6]]
                    gam = w_
                    for sh in (1, 2, 4):
                        gam = gam * prev(gam, sh)
                    inv = 1.0 / gam
                    tiles[d, b, hp] = (base, v_, b_ * inv, kk_ * prev(gam, 1), k_ * inv, r_ * gam,
                                       gam[last:last + 1, :])
                    ytiles[d, b, hp] = jnp.zeros((8, 128), F32)
        units = [(d, b, hp) for d in range(2) for b in range(nb) for hp in range(2)]
        for jj in range(8):
            lhs = []
            for (d, b, hp) in units:
                rw = slice(jj, jj + 1) if d == 0 else slice(7 - jj, 8 - jj)
                _, _, bh, kkh, kh, rh, _ = tiles[d, b, hp]
                lhs.append(jnp.concatenate([col(bh[rw]), col(kkh[rw])], axis=1))
                lhs.append(jnp.concatenate([col(kh[rw]), col(rh[rw])], axis=1))
            cm = _dot(jnp.concatenate(lhs, axis=0), rhs_pair)
            for u, (d, b, hp) in enumerate(units):
                j = jj if d == 0 else 7 - jj
                v_ = tiles[d, b, hp][1]
                r0 = 2 * u * RW_HEAD
                bc, kkc = cm[r0:r0 + RW_HEAD, :128], cm[r0:r0 + RW_HEAD, 128:]
                kc, rc = cm[r0 + RW_HEAD:r0 + 2 * RW_HEAD, :128], cm[r0 + RW_HEAD:r0 + 2 * RW_HEAD, 128:]
                s = s_ref[u]
                sa = jnp.concatenate([allreduce_rows(kkc * s)] * 8, axis=0)
                s = s - bc * sa + kc * v_[j:j + 1]
                s_ref[u] = s
                ytiles[d, b, hp] = jnp.where(sub8 == j, allreduce_rows(rc * s), ytiles[d, b, hp])
        for d, refs in enumerate(dirs):
            for b in range(nb):
                for hp in range(2):
                    u = (d * nb + b) * 2 + hp
                    gl = tiles[d, b, hp][6]
                    gh = gl.astype(BF16).astype(F32)
                    s_ref[u] = s_ref[u] * _dot(jnp.concatenate([col(gh), col(gl - gh)], axis=1), e22)
                    refs[6][b, pl.ds(tiles[d, b, hp][0], 8), pl.ds(hp * 128, 128)] = ytiles[d, b, hp]
        return carry

    lax.fori_loop(0, c // 8, group, 0)


def _rw_scan(r, v, kk, w0, k0, b0, w1, k1, b1, *, nb, t_all, n_ctx):
    c = RW_CHUNK
    nc, ncc = t_all // c, n_ctx // c
    hw = RW_HEADS * RW_HEAD
    shp = lambda a: a.reshape(nb, t_all, hw)
    fwd = lambda j: (0, j, 0)
    bwd = lambda j: (0, jnp.where(j < ncc, ncc - 1 - j, nc - 1 - (j - ncc)), 0)
    blk = (nb, c, hw)
    yshape = jax.ShapeDtypeStruct((nb, t_all, hw), F32)
    yf, yb = pl.pallas_call(
        functools.partial(_rw_scan_kernel, nb=nb),
        grid=(nc,),
        in_specs=[pl.BlockSpec(blk, fwd)] * 6 + [pl.BlockSpec(blk, bwd)] * 6,
        out_specs=[pl.BlockSpec(blk, fwd), pl.BlockSpec(blk, bwd)],
        out_shape=[yshape, yshape],
        scratch_shapes=[pltpu.VMEM((2 * nb * 2, RW_HEAD, 128), F32)],
        compiler_params=_cparams(("arbitrary",)),
        name="rw_scan",
    )(shp(r), shp(v), shp(kk), shp(w0), shp(k0), shp(b0),
      shp(r), shp(v), shp(kk), shp(w1), shp(k1), shp(b1))

    return yf.reshape(nb * t_all, hw), yb.reshape(nb * t_all, hw)


def _rw_post_kernel(yf_ref, yb_ref, bonus_ref, gate_ref, vec_ref, o_ref):
    e4 = _block_ones(RW_HEADS * RW_HEAD, RW_HEAD)
    y = yf_ref[...] + yb_ref[...]
    mean = _split_dot(y, e4) * (1.0 / RW_HEAD)
    yc = y - mean
    var = _split_dot(yc * yc, e4) * (1.0 / RW_HEAD)
    yn = yc * lax.rsqrt(var + RW_LN_EPS) * vec_ref[0:1, :] + vec_ref[1:2, :]
    o_ref[...] = (yn + bonus_ref[...]) * gate_ref[...]


def _rw_post(yf, yb, bonus, gate, vecs, *, nb, tpb_y, r0_y, nt):
    hw = RW_HEADS * RW_HEAD
    n_out = bonus.shape[0]
    ytok = lambda b, r: (b * tpb_y + r0_y + r, 0)
    tok = lambda b, r: (b * nt + r, 0)
    return pl.pallas_call(
        _rw_post_kernel,
        grid=(nb, nt),
        in_specs=[pl.BlockSpec((TM, hw), ytok), pl.BlockSpec((TM, hw), ytok),
                  pl.BlockSpec((TM, hw), tok), pl.BlockSpec((TM, hw), tok),
                  pl.BlockSpec((8, hw), lambda b, r: (0, 0))],
        out_specs=pl.BlockSpec((TM, hw), tok),
        out_shape=jax.ShapeDtypeStruct((n_out, hw), F32),
        compiler_params=_cparams(("parallel", "parallel")),
        name="rw_post",
    )(yf, yb, bonus, gate, vecs)


def _s5_scan_kernel(uf_ref, ub_ref, wb_ref, ab_ref, cf_ref, cb_ref, yf_ref, yb_ref, x_ref, st_ref, *, nb):
    c = S5_CHUNK
    nst = S5_GROUPS * S5_STATE

    @pl.when(pl.program_id(0) == 0)
    def _():
        st_ref[...] = jnp.zeros(st_ref.shape, F32)

    dirs = ((uf_ref, cf_ref, yf_ref), (ub_ref, cb_ref, yb_ref))
    for d, (u_ref, _, _) in enumerate(dirs):
        for b in range(nb):
            x_ref[d, b] = _dot(u_ref[b].astype(BF16), wb_ref[:, d * 2 * nst:(d + 1) * 2 * nst])

    def group(g, carry):
        for d in range(2):
            base = pl.multiple_of((g if d == 0 else c // 8 - 1 - g) * 8, 8)
            ar = ab_ref[d, :, 0:nst]
            ai = ab_ref[d, :, nst:2 * nst]
            for b in range(nb):
                u = d * nb + b
                xr = st_ref[u, :, 0:nst]
                xi = st_ref[u, :, nst:2 * nst]
                bur = x_ref[d, b, pl.ds(base, 8), 0:nst]
                bui = x_ref[d, b, pl.ds(base, 8), nst:2 * nst]
                rows_r, rows_i = [None] * 8, [None] * 8
                for jj in range(8):
                    j = jj if d == 0 else 7 - jj
                    xr, xi = (ar * xr - ai * xi + bur[j:j + 1, :], ar * xi + ai * xr + bui[j:j + 1, :])
                    rows_r[j], rows_i[j] = xr, xi
                st_ref[u, :, 0:nst] = xr
                st_ref[u, :, nst:2 * nst] = xi
                x_ref[d, b, pl.ds(base, 8), 0:nst] = jnp.concatenate(rows_r, axis=0)
                x_ref[d, b, pl.ds(base, 8), nst:2 * nst] = jnp.concatenate(rows_i, axis=0)
        return carry

    lax.fori_loop(0, c // 8, group, 0)

    for d, (_, c_ref, y_ref) in enumerate(dirs):
        for b in range(nb):
            y_ref[b] = _dot(x_ref[d, b].astype(BF16), c_ref[...])


def _s5_scan(p, w_b, ab, cf, cb, *, nb, t_all, n_ctx):
    c = S5_CHUNK
    nc, ncc = t_all // c, n_ctx // c
    nst2 = 2 * S5_GROUPS * S5_STATE
    width = S5_GROUPS * S5_GROUP_CH
    p3 = p.reshape(nb, t_all, N_EXT)
    fwd = lambda j: j
    bwd = lambda j: jnp.where(j < ncc, ncc - 1 - j, nc - 1 - (j - ncc))
    yshape = jax.ShapeDtypeStruct((nb, t_all, width), F32)
    full = lambda j: (0, 0)
    yf, yb = pl.pallas_call(
        functools.partial(_s5_scan_kernel, nb=nb),
        grid=(nc,),
        in_specs=[pl.BlockSpec((nb, c, width), lambda j: (0, fwd(j), EXT_S5 // width)),
                  pl.BlockSpec((nb, c, width), lambda j: (0, bwd(j), EXT_S5 // width)),
                  pl.BlockSpec((width, 2 * nst2), full),
                  pl.BlockSpec((2, 1, nst2), lambda j: (0, 0, 0)),
                  pl.BlockSpec((nst2, width), full), pl.BlockSpec((nst2, width), full)],
        out_specs=[pl.BlockSpec((nb, c, width), lambda j: (0, fwd(j), 0)),
                   pl.BlockSpec((nb, c, width), lambda j: (0, bwd(j), 0))],
        out_shape=[yshape, yshape],
        scratch_shapes=[pltpu.VMEM((2, nb, c, nst2), F32), pltpu.VMEM((2 * nb, 1, nst2), F32)],
        compiler_params=_cparams(("arbitrary",)),
        name="s5_scan",
    )(p3, p3, w_b, ab, cf, cb)
    return yf.reshape(nb * t_all, width), yb.reshape(nb * t_all, width)


def _s5_post_kernel(yf_ref, yb_ref, u_ref, d_ref, gw_ref, gb_ref, o_ref):
    y = yf_ref[...] + yb_ref[...] + d_ref[...] * u_ref[...]
    zg = 0.5 * y * (1.0 + jnp.tanh(math.sqrt(2.0 / math.pi) * (y + 0.044715 * (y * y * y))))
    o_ref[...] = zg * _sigmoid(_dot(zg.astype(BF16), gw_ref[...]) + gb_ref[...])


def _s5_post(yf, yb, p, d_row, glu_w, glu_b, *, nb, tpb, r0, nt):
    tok = lambda b, r: (b * tpb + r0 + r, 0)
    full = lambda b, r: (0, 0)
    return pl.pallas_call(
        _s5_post_kernel,
        grid=(nb, nt),
        in_specs=[pl.BlockSpec((TM, 256), tok), pl.BlockSpec((TM, 256), tok),
                  pl.BlockSpec((TM, 256), lambda b, r: (b * tpb + r0 + r, EXT_S5 // 256)),
                  pl.BlockSpec((1, 256), full), pl.BlockSpec((256, 256), full),
                  pl.BlockSpec((1, 256), full)],
        out_specs=pl.BlockSpec((TM, 256), lambda b, r: (b * nt + r, 0)),
        out_shape=jax.ShapeDtypeStruct((nb * nt * TM, 256), F32),
        compiler_params=_cparams(("parallel", "parallel")),
        name="s5_post",
    )(yf, yb, p, d_row, glu_w, glu_b)


def _merge_kernel(ya_ref, yb_ref, ys_ref, yd_ref, g0, g1, g2, g3, h_ref, m2_ref, wb_ref, wo_ref, o_ref):
    acc = None
    for n, (y_ref, g_ref) in enumerate(((ya_ref, g0), (yb_ref, g1), (ys_ref, g2), (yd_ref, g3))):
        term = _sigmoid(g_ref[...]) * _dot(y_ref[...].astype(BF16), wb_ref[n])
        acc = term if acc is None else acc + term
    o_ref[...] = h_ref[...] + m2_ref[0] * _dot(acc.astype(BF16), wo_ref[...])


def _merge(ya, yb, ys, yd, p, h, mod2, w_branch, w_out, *, nb, tpb, r0, nt):
    tok_in = lambda b, r: (b * tpb + r0 + r, 0)
    tok_out = lambda b, r: (b * nt + r, 0)
    full2 = lambda b, r: (0, 0)
    gate = [pl.BlockSpec((TM, D_MODEL), lambda b, r, c=c: (b * tpb + r0 + r, EXT_GATE // D_MODEL + c))
            for c in range(4)]
    return pl.pallas_call(
        _merge_kernel,
        grid=(nb, nt),
        in_specs=[pl.BlockSpec((TM, 256), tok_out)] * 4 + gate
                 + [pl.BlockSpec((TM, D_MODEL), tok_in),
                    pl.BlockSpec((1, 1, D_MODEL), lambda b, r: (jnp.where(r0 + r == 0, nb, b), 0, 0)),
                    pl.BlockSpec((4, 256, D_MODEL), lambda b, r: (0, 0, 0)),
                    pl.BlockSpec((D_MODEL, D_MODEL), full2)],
        out_specs=pl.BlockSpec((TM, D_MODEL), tok_out),
        out_shape=jax.ShapeDtypeStruct((nb * nt * TM, D_MODEL), F32),
        compiler_params=_cparams(("parallel", "parallel")),
        name="merge",
    )(ya, yb, ys, yd, p, p, p, p, h, mod2, w_branch, w_out)


def _router_kernel(h_ref, g_ref, sh_ref, sc_ref, wh_ref, wm_ref, b_ref, x_ref, lg_ref):
    x = _rms(h_ref[...], g_ref[...], NORM_EPS) * (1.0 + sc_ref[0]) + sh_ref[0]
    xh = x.astype(BF16)
    xm = (x - xh.astype(F32)).astype(BF16)
    bits = lax.bitcast_convert_type(xh.astype(F32), jnp.uint32)
    half = D_MODEL // 2
    x_ref[...] = (bits[:, :half] >> 16) | (bits[:, half:] & jnp.uint32(0xFFFF0000))
    lg = (_dot(xh, wh_ref[...]) + _dot(xm, wh_ref[...]) + _dot(xh, wm_ref[...])) + b_ref[...]

    lane = lax.broadcasted_iota(jnp.int32, lg.shape, 1)
    lanef = lane.astype(F32)
    neg = jnp.float32(-jnp.inf)
    big = jnp.float32(1e9)
    rmax = lambda v: jnp.max(v, axis=-1, keepdims=True)
    rmin = lambda v: jnp.min(v, axis=-1, keepdims=True)
    rsum = lambda v: jnp.sum(v, axis=-1, keepdims=True)

    gmask = lane < MOE_GROUPS
    mg = rmax(jnp.where(gmask, lg, neg))
    eg = jnp.where(gmask, jnp.exp(lg - mg), 0.0)
    pg = eg / rsum(eg)
    pg_top = rmax(pg)
    g_sel = rmin(jnp.where(jnp.logical_and(gmask, pg == pg_top), lanef, big))
    lo = MOE_GROUPS + MOE_PER_GROUP * g_sel
    emask = jnp.logical_and(lanef >= lo, lanef < lo + MOE_PER_GROUP)
    me = rmax(jnp.where(emask, lg, neg))
    ee = jnp.where(emask, jnp.exp(lg - me), 0.0)
    pe = jnp.where(emask, ee / rsum(ee), -1.0)
    p1 = rmax(pe)
    i1 = rmin(jnp.where(pe == p1, lanef, big))
    pe2 = jnp.where(lanef == i1, -1.0, pe)
    p2 = rmax(pe2)
    i2 = rmin(jnp.where(pe2 == p2, lanef, big))
    den = p1 + p2
    out = jnp.where(lane == 0, i1 - MOE_GROUPS, 0.0)
    out = jnp.where(lane == 1, i2 - MOE_GROUPS, out)
    out = jnp.where(lane == 2, pg_top * p1 / den, out)
    out = jnp.where(lane == 3, pg_top * p2 / den, out)
    lg_ref[...] = out


def _router(h, g, shift, scale, wh, wm, bias, *, nb, nt, ctx_first):
    tok = lambda b, r: (b * nt + r, 0)
    full = lambda b, r: (0, 0)
    if ctx_first:
        modmap = lambda b, r: (jnp.where(r == 0, nb, b), 0, 0)
    else:
        modmap = lambda b, r: (b, 0, 0)
    n = h.shape[0]
    return pl.pallas_call(
        _router_kernel,
        grid=(nb, nt),
        in_specs=[pl.BlockSpec((TM, D_MODEL), tok), pl.BlockSpec((1, D_MODEL), full),
                  pl.BlockSpec((1, 1, D_MODEL), modmap), pl.BlockSpec((1, 1, D_MODEL), modmap),
                  pl.BlockSpec((D_MODEL, 128), full), pl.BlockSpec((D_MODEL, 128), full),
                  pl.BlockSpec((1, 128), full)],
        out_specs=[pl.BlockSpec((TM, D_MODEL // 2), tok), pl.BlockSpec((TM, 128), tok)],
        out_shape=[jax.ShapeDtypeStruct((n, D_MODEL // 2), jnp.uint32), jax.ShapeDtypeStruct((n, 128), F32)],
        compiler_params=_cparams(("parallel", "parallel")),
        name="router",
    )(h, g, shift, scale, wh, wm, bias)


def _expert_kernel(be_ref, nv_ref, x_ref, wg_ref, wu_ref, wd_ref, o_ref, wgb_ref, wub_ref, wdb_ref):
    i = pl.program_id(0)

    @pl.when(jnp.logical_or(i == 0, be_ref[i] != be_ref[jnp.maximum(i - 1, 0)]))
    def _():
        wgb_ref[...] = wg_ref[0].astype(BF16)
        wub_ref[...] = wu_ref[0].astype(BF16)
        wdb_ref[...] = wd_ref[0].astype(BF16)

    @pl.when(i < nv_ref[0])
    def _():
        u = x_ref[...]
        x = jnp.concatenate([lax.bitcast_convert_type(u << 16, F32),
                             lax.bitcast_convert_type(u & jnp.uint32(0xFFFF0000), F32)], axis=1).astype(BF16)
        hb = _silu(_dot(x, wgb_ref[...])) * _dot(x, wub_ref[...])
        o_ref[...] = _dot(hb.astype(BF16), wdb_ref[...])

    @pl.when(i >= nv_ref[0])
    def _():
        o_ref[...] = jnp.zeros(o_ref.shape, F32)


def _experts(xs, block_e, n_valid, w_gate, w_up, w_down):
    n_slots = xs.shape[0]
    n_blocks = n_slots // MOE_BLK
    wmap = lambda i, be, nv: (be[i], 0, 0)
    return pl.pallas_call(
        _expert_kernel,
        grid_spec=pltpu.PrefetchScalarGridSpec(
            num_scalar_prefetch=2,
            grid=(n_blocks,),
            in_specs=[pl.BlockSpec((MOE_BLK, D_MODEL // 2), lambda i, be, nv: (i, 0)),
                      pl.BlockSpec((1, D_MODEL, D_EXPERT), wmap),
                      pl.BlockSpec((1, D_MODEL, D_EXPERT), wmap),
                      pl.BlockSpec((1, D_EXPERT, D_MODEL), wmap)],
            out_specs=pl.BlockSpec((MOE_BLK, D_MODEL), lambda i, be, nv: (i, 0)),
            scratch_shapes=[pltpu.VMEM((D_MODEL, D_EXPERT), BF16), pltpu.VMEM((D_MODEL, D_EXPERT), BF16),
                            pltpu.VMEM((D_EXPERT, D_MODEL), BF16)]),
        out_shape=jax.ShapeDtypeStruct((n_slots, D_MODEL), F32),
        compiler_params=_cparams(("arbitrary",)),
        name="experts",
    )(block_e, n_valid, xs, w_gate, w_up, w_down)


def _combine_kernel(h_ref, y0_ref, y1_ref, w_ref, m5_ref, g_ref, o_ref, *, final):
    w = w_ref[...]
    y = y0_ref[...] * w[:, 0:1] + y1_ref[...] * w[:, 1:2]
    h = h_ref[...] + m5_ref[0] * y
    if final:
        h = _rms(h, g_ref[...], NORM_EPS)
    o_ref[...] = h


def _combine(h, y01, wts, mod5, g_final, *, nb, nt, ctx_first, final):
    tok = lambda b, r: (b * nt + r, 0)
    if ctx_first:
        modmap = lambda b, r: (jnp.where(r == 0, nb, b), 0, 0)
    else:
        modmap = lambda b, r: (b, 0, 0)
    return pl.pallas_call(
        functools.partial(_combine_kernel, final=final),
        grid=(nb, nt),
        in_specs=[pl.BlockSpec((TM, D_MODEL), tok), pl.BlockSpec((TM, D_MODEL), tok),
                  pl.BlockSpec((TM, D_MODEL), lambda b, r: ((nb + b) * nt + r, 0)),
                  pl.BlockSpec((TM, 128), tok), pl.BlockSpec((1, 1, D_MODEL), modmap),
                  pl.BlockSpec((1, D_MODEL), lambda b, r: (0, 0))],
        out_specs=pl.BlockSpec((TM, D_MODEL), tok),
        out_shape=jax.ShapeDtypeStruct(h.shape, F32),
        compiler_params=_cparams(("parallel", "parallel")),
        name="combine",
    )(h, y01, y01, wts, mod5, g_final)


def _sc_gather(table, idx):
    n, d = idx.shape[0], table.shape[1]
    steps = n // SC_ROWS_PER_STEP
    idx_p = jnp.pad(idx.reshape(steps, SC_ROWS_PER_STEP), ((0, 0), (0, SC_INDEX_TILE - SC_ROWS_PER_STEP)))
    mesh = plsc.VectorSubcoreMesh(core_axis_name="core", subcore_axis_name="subcore")

    @pl.kernel(out_type=jax.ShapeDtypeStruct((n, d), table.dtype), mesh=mesh)
    def gather_kernel(x_hbm, i_hbm, o_hbm):
        def body(i_vmem, o_vmem):
            pltpu.sync_copy(x_hbm.at[i_vmem.at[0, pl.ds(0, SC_ROWS_PER_STEP)]], o_vmem)

        pltpu.emit_pipeline(
            body,
            grid=(steps,),
            in_specs=[pl.BlockSpec((1, SC_INDEX_TILE), index_map=lambda i: (i, 0))],
            out_specs=[pl.BlockSpec((SC_ROWS_PER_STEP, d), index_map=lambda i: (i, 0))],
            core_axis_name=("core", "subcore"),
            dimension_semantics=(pltpu.PARALLEL,),
        )(i_hbm, o_hbm)

    return gather_kernel(table, idx_p)


def _moe(h, g2, shift, scale, mod5, wh, wm, rbias, w_gate, w_up, w_down, g_final, *, layer, nb, nt, ctx_first,
         final):
    n = h.shape[0]
    x_bf, route = _router(h, g2, shift, scale, wh, wm, rbias, nb=nb, nt=nt, ctx_first=ctx_first)
    idx = route[:, :MOE_TOPK].astype(jnp.int32)
    wts = route[:, MOE_TOPK:2 * MOE_TOPK]
    n_as = n * MOE_TOPK
    flat_e = idx.reshape(n_as)
    onehot = (flat_e[:, None] == jnp.arange(MOE_EXPERTS, dtype=jnp.int32)[None, :]).astype(jnp.int32)
    csum = jnp.cumsum(onehot, axis=0)
    counts = csum[-1]
    rank = jnp.sum(jnp.where(onehot > 0, csum - 1, 0), axis=1)
    padded = (counts + MOE_BLK - 1) // MOE_BLK * MOE_BLK
    pad_end = jnp.cumsum(padded)
    pad_start = pad_end - padded
    slot = pad_start[flat_e] + rank
    n_blocks = (n_as + MOE_EXPERTS * (MOE_BLK - 1) + MOE_BLK - 1) // MOE_BLK
    n_slots = n_blocks * MOE_BLK
    slot_tok = (jnp.arange(n_slots, dtype=jnp.int32) % n).at[slot].set(
        jnp.arange(n_as, dtype=jnp.int32) // MOE_TOPK, unique_indices=True)
    starts = jnp.arange(n_blocks, dtype=jnp.int32) * MOE_BLK
    block_e = jnp.minimum(jnp.sum((pad_end[None, :] <= starts[:, None]).astype(jnp.int32), axis=1),
                          MOE_EXPERTS - 1)
    n_valid = (pad_end[-1:] // MOE_BLK).astype(jnp.int32)
    xs = _sc_gather(x_bf, slot_tok)
    ys = _experts(xs, block_e + layer * MOE_EXPERTS, n_valid, w_gate, w_up, w_down)
    y01 = _sc_gather(ys, slot.reshape(n, MOE_TOPK).T.reshape(n_as))
    wts_pad = jnp.pad(wts.astype(F32), ((0, 0), (0, 128 - MOE_TOPK)))
    return _combine(h, y01, wts_pad, mod5, g_final, nb=nb, nt=nt, ctx_first=ctx_first, final=final)


_ROT_SRC = np.array(list(range(8, 16)) + list(range(0, 8)) + list(range(24, 32)) + list(range(16, 24)))
_ROT_SIGN = np.array([-1.0] * 8 + [1.0] * 8 + [-1.0] * 8 + [1.0] * 8, np.float32)


def _rot_cols(w):
    k = w.shape[-1] // ROPE_DIM
    src = np.concatenate([_ROT_SRC + ROPE_DIM * i for i in range(k)])
    sign = np.tile(_ROT_SIGN, k)
    return w[..., src] * sign


def _rope_tables(n_ctx, n_lat):
    rows = n_lat // GRID_W
    row = jnp.repeat(jnp.arange(rows, dtype=F32), GRID_W)
    col = (jnp.arange(rows * GRID_W) % GRID_W).astype(F32)
    nf = ROPE_DIM // 4
    inv = ROPE_BASE ** (-jnp.arange(nf, dtype=F32) / nf)
    ar = row[:, None] * inv
    ac = col[:, None] * inv
    ang = jnp.concatenate([ar, ar, ac, ac], axis=-1)
    cos = jnp.concatenate([jnp.ones((n_ctx, ROPE_DIM), F32), jnp.cos(ang)], axis=0)
    sin = jnp.concatenate([jnp.zeros((n_ctx, ROPE_DIM), F32), jnp.sin(ang)], axis=0)
    return cos, sin


def _block_diag(blocks):
    g, a, b = blocks.shape
    tiled = jnp.tile(blocks.reshape(g * a, b), (1, g))
    rows = lax.broadcasted_iota(jnp.int32, (g * a, g * b), 0) // a
    cols = lax.broadcasted_iota(jnp.int32, (g * a, g * b), 1) // b
    return jnp.where(rows == cols, tiled, 0.0)


def _pick_tk(t_all):
    best = 128
    for tk in range(128, ATTN_TK_MAX + 1, 128):
        if t_all % tk == 0:
            best = tk
    return best


def kernel(x, c, ctx, c_ctx, w_mod, b_mod, norm1_g, norm2_g, w_in, mla_q_norm_g, mla_kv_norm_g, mla_w_uq, mla_w_ukv, rw_mu, rw_w0, rw_w2, rw_a0, rw_a2, rw_g2, rw_k_k, rw_k_a, rw_r_k, rw_lnx_g, rw_lnx_b, s5_a_re, s5_a_im, s5_log_dt, s5_b_re, s5_b_im, s5_c_re, s5_c_im, s5_d, s5_glu_w, s5_glu_b, diff_lq1, diff_lk1, diff_lq2, diff_lk2, diff_subln_g, w_branch, w_out, router_g_w, router_g_b, router_e_w, router_e_b, exp_w_gate, exp_w_up, exp_w_down, final_norm_g):
    nb, n_lat, d = x.shape
    n_ctx = ctx.shape[1]
    depth = w_mod.shape[0]
    t_all = n_ctx + n_lat
    assert d == D_MODEL and n_ctx == TM and n_lat % TM == 0
    tpb = t_all // TM
    tk = _pick_tk(t_all)
    hw = RW_HEADS * RW_HEAD

    cos, sin = _rope_tables(n_ctx, n_lat)
    mla_scale = (MLA_NOPE + MLA_ROPE) ** -0.5 * LOG2E
    z32 = jnp.zeros((t_all, 32), F32)
    cq_tab = jnp.concatenate([jnp.ones((t_all, 64), F32), cos, z32], axis=1) * mla_scale
    sq_tab = jnp.concatenate([jnp.zeros((t_all, 64), F32), sin, z32], axis=1) * mla_scale
    ck_tab = jnp.concatenate([cos, sin, jnp.zeros((t_all, 64), F32)], axis=1)
    dcos = jnp.tile(cos, (1, 8))
    dsin = jnp.tile(sin, (1, 8))

    c_rows = jnp.concatenate([c, c_ctx[None, :], jnp.zeros((8 - nb - 1, d), F32)], axis=0)

    h = jnp.concatenate([ctx, x], axis=1).reshape(nb * t_all, d)

    for l in range(depth):
        last = l == depth - 1
        r0, nt = (1, tpb - 1) if last else (0, tpb)

        mod = _mod(c_rows, w_mod[l].astype(BF16), b_mod[l][None, :], tn=1536)
        mods = [mod[:nb + 1, i * d:(i + 1) * d].reshape(nb + 1, 1, d) for i in range(6)]

        wi = w_in[l]
        o_rw, o_s5, o_df, o_gt = 416, 1440, 1696, 2464
        w_kr = wi[:, 384:416]
        w_dq, w_dk, w_dv = wi[:, o_df:o_df + 256], wi[:, o_df + 256:o_df + 512], wi[:, o_df + 512:o_df + 768]
        w_ext = jnp.concatenate(
            [wi[:, o_rw:o_s5],
             wi[:, :416], _rot_cols(w_kr), jnp.zeros((d, 64), F32),
             wi[:, o_s5:o_df],
             w_dq, w_dk, w_dv, _rot_cols(w_dq), _rot_cols(w_dk),
             wi[:, o_gt:]], axis=1).astype(BF16)
        p = _inproj(h, norm1_g[l][None, :], mods[0], mods[1], w_ext, nb=nb, tpb=tpb)

        wq = mla_w_uq[l].reshape(MLA_Q_LORA, MLA_HEADS, MLA_NOPE + MLA_ROPE)
        zq = jnp.zeros((MLA_Q_LORA, MLA_HEADS, 32), F32)
        wa = jnp.concatenate([wq, zq], axis=2).reshape(MLA_Q_LORA, 512).astype(BF16)
        wb = jnp.concatenate([jnp.zeros((MLA_Q_LORA, MLA_HEADS, 64), F32), _rot_cols(wq[:, :, MLA_NOPE:]), zq],
                             axis=2).reshape(MLA_Q_LORA, 512).astype(BF16)
        wkv = mla_w_ukv[l].reshape(MLA_KV_LORA, MLA_HEADS, MLA_NOPE + MLA_V)
        wk = jnp.concatenate([wkv[:, :, :MLA_NOPE], jnp.zeros((MLA_KV_LORA, MLA_HEADS, 64), F32)],
                             axis=2).reshape(MLA_KV_LORA, 512).astype(BF16)
        wv = wkv[:, :, MLA_NOPE:].reshape(MLA_KV_LORA, MLA_HEADS * MLA_V).astype(BF16)
        pk_np = np.zeros((128, 512), np.float32)
        for hh in range(MLA_HEADS):
            for i in range(32):
                pk_np[i, hh * 128 + 64 + i] = 1.0
                pk_np[32 + i, hh * 128 + 64 + i] = 1.0
        pk = jnp.asarray(pk_np, BF16)
        q_m, k_m, v_m = _mla_prep(p, mla_q_norm_g[l][None, :], mla_kv_norm_g[l][None, :], wa, wb, wk, wv, pk,
                                  cq_tab, sq_tab, ck_tab, nb=nb, tpb=tpb)
        ya_lat = _flash_mla(q_m, k_m, v_m, nb=nb, t_all=t_all, q_start=n_ctx, q_len=n_lat, k_len=t_all, tk=tk)

        q_d, k_d, v_d = _diff_prep(p, dcos, dsin, nb=nb, tpb=tpb)
        lam_init = 0.8 - 0.6 * math.exp(-0.3 * l)
        lam = (jnp.exp(jnp.sum(diff_lq1[l] * diff_lk1[l])) - jnp.exp(jnp.sum(diff_lq2[l] * diff_lk2[l])) + lam_init)
        lam_row = jnp.full((1, 256), lam, F32)
        g_row = jnp.tile(diff_subln_g[l], DIFF_HEADS)[None, :]
        yd_lat = _flash_diff(q_d, k_d, v_d, lam_row, g_row, lam_init=lam_init, nb=nb, t_all=t_all,
                             q_start=n_ctx, q_len=n_lat, k_len=t_all, tk=tk)
        if last:
            ya, yd = ya_lat, yd_lat
        else:
            ya_ctx = _flash_mla(q_m, k_m, v_m, nb=nb, t_all=t_all, q_start=0, q_len=n_ctx, k_len=n_ctx, tk=n_ctx)
            yd_ctx = _flash_diff(q_d, k_d, v_d, lam_row, g_row, lam_init=lam_init, nb=nb, t_all=t_all,
                                 q_start=0, q_len=n_ctx, k_len=n_ctx, tk=n_ctx)
            comb = lambda a_c, a_l: jnp.concatenate(
                [a_c.reshape(nb, n_ctx, -1), a_l.reshape(nb, n_lat, -1)], axis=1).reshape(nb * t_all, -1)
            ya, yd = comb(ya_ctx, ya_lat), comb(yd_ctx, yd_lat)

        wlo = jnp.zeros((128, 4 * hw), F32)
        wlo = wlo.at[:64, 0:hw].set(rw_w2[l, 0]).at[:64, hw:2 * hw].set(rw_w2[l, 1])
        wlo = wlo.at[64:, 2 * hw:3 * hw].set(rw_a2[l, 0]).at[64:, 3 * hw:].set(rw_a2[l, 1])
        vecs = jnp.stack([rw_k_k[l], rw_k_a[l], rw_r_k[l].reshape(hw), rw_w0[l, 0], rw_w0[l, 1],
                          rw_a0[l, 0], rw_a0[l, 1], jnp.zeros((hw,), F32)], axis=0)
        (r_, v_, kk_, w0_, k0_, b0_, w1_, k1_, b1_, bonus, gate_rw) = _rw_prep(
            p, rw_mu[l][None, :], wlo.astype(BF16), rw_g2[l].astype(BF16), vecs, nb=nb, tpb=tpb, r0=0, nt=tpb)
        yf, yb_ = _rw_scan(r_, v_, kk_, w0_, k0_, b0_, w1_, k1_, b1_, nb=nb, t_all=t_all, n_ctx=n_ctx)
        ln_vecs = jnp.concatenate([rw_lnx_g[l][None, :], rw_lnx_b[l][None, :], jnp.zeros((6, hw), F32)], axis=0)
        if last:
            trim = lambda a: a.reshape(nb, t_all, hw)[:, n_ctx:].reshape(nb * n_lat, hw)
            bonus, gate_rw = trim(bonus), trim(gate_rw)
        y_rw = _rw_post(yf, yb_, bonus, gate_rw, ln_vecs, nb=nb, tpb_y=tpb, r0_y=r0, nt=nt)

        bbs, abs_, cfs = [], [], []
        for dd in range(2):
            lr, li = s5_a_re[l, dd], s5_a_im[l, dd]
            dt = jnp.exp(s5_log_dt[l, dd])[:, None]
            mag = jnp.exp(lr * dt)
            ab_re, ab_im = mag * jnp.cos(li * dt), mag * jnp.sin(li * dt)
            den = lr * lr + li * li
            nr, ni = ab_re - 1.0, ab_im
            cf_re = (nr * lr + ni * li) / den
            cf_im = (ni * lr - nr * li) / den
            bre, bim = s5_b_re[l, dd], s5_b_im[l, dd]
            bb_re = cf_re[..., None] * bre - cf_im[..., None] * bim
            bb_im = cf_re[..., None] * bim + cf_im[..., None] * bre
            bbs.append(jnp.concatenate([_block_diag(bb_re.transpose(0, 2, 1)),
                                        _block_diag(bb_im.transpose(0, 2, 1))], axis=1))
            abs_.append(jnp.concatenate([ab_re.reshape(-1), ab_im.reshape(-1)])[None, :])
            cfs.append(jnp.concatenate([_block_diag(s5_c_re[l, dd].transpose(0, 2, 1)),
                                        -_block_diag(s5_c_im[l, dd].transpose(0, 2, 1))], axis=0))
        yf_s5, yb_s5 = _s5_scan(p, jnp.concatenate(bbs, axis=1).astype(BF16), jnp.stack(abs_, axis=0),
                                cfs[0].astype(BF16), cfs[1].astype(BF16), nb=nb, t_all=t_all, n_ctx=n_ctx)
        y_s5 = _s5_post(yf_s5, yb_s5, p, s5_d[l].reshape(1, 256), s5_glu_w[l].astype(BF16),
                        s5_glu_b[l][None, :], nb=nb, tpb=tpb, r0=r0, nt=nt)

        h = _merge(ya, y_rw, y_s5, yd, p, h, mods[2], w_branch[l].astype(BF16), w_out[l].astype(BF16),
                   nb=nb, tpb=tpb, r0=r0, nt=nt)

        wr = jnp.concatenate([router_g_w[l], router_e_w[l], jnp.zeros((d, 128 - MOE_GROUPS - MOE_EXPERTS), F32)], axis=1)
        wr_h = wr.astype(BF16)
        wr_m = (wr - wr_h.astype(F32)).astype(BF16)
        rbias = jnp.concatenate([router_g_b[l], router_e_b[l],
                                 jnp.zeros((128 - MOE_GROUPS - MOE_EXPERTS,), F32)])[None, :]
        h = _moe(h, norm2_g[l][None, :], mods[3], mods[4], mods[5], wr_h, wr_m, rbias,
                 exp_w_gate.reshape(depth * MOE_EXPERTS, d, D_EXPERT), exp_w_up.reshape(depth * MOE_EXPERTS, d, D_EXPERT),
                 exp_w_down.reshape(depth * MOE_EXPERTS, D_EXPERT, d),
                 final_norm_g[None, :], layer=l, nb=nb, nt=nt, ctx_first=not last, final=last)

    return h.reshape(nb, n_lat, d)
```

```python
import functools
import math

import jax
import jax.numpy as jnp
import numpy as np
from jax import lax
from jax.experimental import pallas as pl
from jax.experimental.pallas import tpu as pltpu
from jax.experimental.pallas import tpu_sc as plsc

F32 = jnp.float32
BF16 = jnp.bfloat16

TM = 256
VMEM_LIMIT = 48 * 1024 * 1024

D_MODEL = 1024
GRID_W = 64
ROPE_DIM = 32
ROPE_BASE = 10000.0
NORM_EPS = 1e-6
MLA_HEADS, MLA_NOPE, MLA_ROPE, MLA_V = 4, 64, 32, 64
MLA_Q_LORA, MLA_KV_LORA = 256, 128
RW_HEADS, RW_HEAD = 4, 64
RW_LN_EPS = 64e-5
S5_GROUPS, S5_GROUP_CH, S5_STATE = 16, 16, 64
DIFF_HEADS, DIFF_HD = 4, 32
DIFF_EPS = 1e-5
MOE_GROUPS, MOE_PER_GROUP, MOE_TOPK = 4, 8, 2
MOE_EXPERTS = MOE_GROUPS * MOE_PER_GROUP
D_EXPERT = 512
MOE_BLK = 256
RW_CHUNK = 128
S5_CHUNK = 256
ATTN_TK_MAX = 2816
SC_INDEX_TILE = 128
SC_ROWS_PER_STEP = 32
LOG2E = math.log2(math.e)

EXT_RW, EXT_MLA, EXT_S5, EXT_DIFF, EXT_GATE = 0, 1024, 1536, 1792, 3072
N_EXT = 7168


def _cparams(sem, vmem=VMEM_LIMIT):
    return pltpu.CompilerParams(dimension_semantics=sem, vmem_limit_bytes=vmem)


def _dot(a, b):
    return jnp.dot(a, b, preferred_element_type=F32)


def _dot_nt(a, b):
    return lax.dot_general(a, b, (((1,), (1,)), ((), ())), preferred_element_type=F32)


def _split_dot(x, e):
    hi = x.astype(BF16)
    mid = (x - hi.astype(F32)).astype(BF16)
    return _dot(hi, e) + _dot(mid, e)


def _block_ones(n, blk):
    r = lax.broadcasted_iota(jnp.int32, (n, n), 0) // blk
    c = lax.broadcasted_iota(jnp.int32, (n, n), 1) // blk
    return (r == c).astype(BF16)


def _sigmoid(x):
    return 1.0 / (1.0 + jnp.exp(-x))


def _silu(x):
    return x * _sigmoid(x)


def _softplus(x):
    return jnp.maximum(x, 0.0) + jnp.log(1.0 + jnp.exp(-jnp.abs(x)))


def _rms(x, g, eps):
    return x * lax.rsqrt(jnp.mean(x * x, axis=-1, keepdims=True) + eps) * g


def _mod_kernel(x_ref, w_ref, b_ref, o_ref):
    o_ref[...] = _dot(_silu(x_ref[...]).astype(BF16), w_ref[...]) + b_ref[...]


def _mod(x, w, b, *, tn):
    m, k = x.shape
    n = w.shape[1]
    return pl.pallas_call(
        _mod_kernel,
        grid=(n // tn,),
        in_specs=[pl.BlockSpec((m, k), lambda j: (0, 0)),
                  pl.BlockSpec((k, tn), lambda j: (0, j)),
                  pl.BlockSpec((1, tn), lambda j: (0, j))],
        out_specs=pl.BlockSpec((m, tn), lambda j: (0, j)),
        out_shape=jax.ShapeDtypeStruct((m, n), F32),
        compiler_params=_cparams(("parallel",)),
        name="mod",
    )(x, w, b)


def _inproj_kernel(h_ref, g_ref, sh_ref, sc_ref, w_ref, o_ref):
    x = _rms(h_ref[...], g_ref[...], NORM_EPS)
    xn = (x * (1.0 + sc_ref[0]) + sh_ref[0]).astype(BF16)
    o_ref[...] = _dot(xn, w_ref[...])


def _inproj(h, g, shift, scale, w_ext, *, nb, tpb):
    n = h.shape[0]
    tn = N_EXT // 2
    modmap = lambda j, b, r: (jnp.where(r == 0, nb, b), 0, 0)
    return pl.pallas_call(
        _inproj_kernel,
        grid=(N_EXT // tn, nb, tpb),
        in_specs=[pl.BlockSpec((TM, D_MODEL), lambda j, b, r: (b * tpb + r, 0)),
                  pl.BlockSpec((1, D_MODEL), lambda j, b, r: (0, 0)),
                  pl.BlockSpec((1, 1, D_MODEL), modmap),
                  pl.BlockSpec((1, 1, D_MODEL), modmap),
                  pl.BlockSpec((D_MODEL, tn), lambda j, b, r: (0, j))],
        out_specs=pl.BlockSpec((TM, tn), lambda j, b, r: (b * tpb + r, j)),
        out_shape=jax.ShapeDtypeStruct((n, N_EXT), F32),
        compiler_params=_cparams(("parallel", "parallel", "parallel")),
        name="inproj",
    )(h, g, shift, scale, w_ext)


def _mla_prep_kernel(p_ref, gq_ref, gkv_ref, wa_ref, wb_ref, wk_ref, wv_ref, pk_ref,
                     cq_ref, sq_ref, ck_ref, q_ref, k_ref, v_ref):
    seg = p_ref[...]
    nq = _rms(seg[:, :MLA_Q_LORA], gq_ref[...], NORM_EPS).astype(BF16)
    nkv = _rms(seg[:, MLA_Q_LORA:MLA_Q_LORA + MLA_KV_LORA], gkv_ref[...], NORM_EPS).astype(BF16)
    cq = jnp.concatenate([cq_ref[...]] * MLA_HEADS, axis=1)
    sq = jnp.concatenate([sq_ref[...]] * MLA_HEADS, axis=1)
    q = _dot(nq, wa_ref[...]) * cq + _dot(nq, wb_ref[...]) * sq
    q_ref[...] = q.astype(BF16)
    kr = (seg[:, 384:512] * ck_ref[...]).astype(BF16)
    k_ref[...] = (_dot(nkv, wk_ref[...]) + _dot(kr, pk_ref[...])).astype(BF16)
    v_ref[...] = _dot(nkv, wv_ref[...]).astype(BF16)


def _mla_prep(p, gq, gkv, wa, wb, wk, wv, pk, cq_tab, sq_tab, ck_tab, *, nb, tpb):
    n = p.shape[0]
    tok = lambda b, r: (b * tpb + r, 0)
    pos = lambda b, r: (r, 0)
    full = lambda b, r: (0, 0)
    return pl.pallas_call(
        _mla_prep_kernel,
        grid=(nb, tpb),
        in_specs=[pl.BlockSpec((TM, 512), lambda b, r: (b * tpb + r, EXT_MLA // 512)),
                  pl.BlockSpec((1, MLA_Q_LORA), full),
                  pl.BlockSpec((1, MLA_KV_LORA), full),
                  pl.BlockSpec((MLA_Q_LORA, 512), full),
                  pl.BlockSpec((MLA_Q_LORA, 512), full),
                  pl.BlockSpec((MLA_KV_LORA, 512), full),
                  pl.BlockSpec((MLA_KV_LORA, 256), full),
                  pl.BlockSpec((128, 512), full),
                  pl.BlockSpec((TM, 128), pos),
                  pl.BlockSpec((TM, 128), pos),
                  pl.BlockSpec((TM, 128), pos)],
        out_specs=[pl.BlockSpec((TM, 512), tok),
                   pl.BlockSpec((TM, 512), tok),
                   pl.BlockSpec((TM, 256), tok)],
        out_shape=[jax.ShapeDtypeStruct((n, 512), BF16),
                   jax.ShapeDtypeStruct((n, 512), BF16),
                   jax.ShapeDtypeStruct((n, 256), BF16)],
        compiler_params=_cparams(("parallel", "parallel")),
        name="mla_prep",
    )(p, gq, gkv, wa, wb, wk, wv, pk, cq_tab, sq_tab, ck_tab)


def _flash_mla_kernel(q_ref, k_ref, v_ref, o_ref, m_ref, l_ref, acc_ref):
    kk = pl.program_id(2)
    tq = q_ref.shape[0]

    @pl.when(kk == 0)
    def _():
        m_ref[...] = jnp.full(m_ref.shape, -jnp.inf, F32)
        l_ref[...] = jnp.zeros(l_ref.shape, F32)
        acc_ref[...] = jnp.zeros(acc_ref.shape, F32)

    ps, alphas = [], []
    for h in range(MLA_HEADS):
        s = _dot_nt(q_ref[:, h * 128:(h + 1) * 128], k_ref[:, h * 128:(h + 1) * 128])
        m_prev = m_ref[h][:, :1]
        m_new = jnp.maximum(m_prev, jnp.max(s, axis=-1, keepdims=True))
        alpha = jnp.exp2(m_prev - m_new)
        p = jnp.exp2(s - m_new)
        l_ref[h] = jnp.broadcast_to(alpha * l_ref[h][:, :1] + jnp.sum(p, axis=-1, keepdims=True), (tq, 128))
        m_ref[h] = jnp.broadcast_to(m_new, (tq, 128))
        alphas.append(alpha)
        ps.append(p.astype(BF16))
    pv = _dot(jnp.concatenate(ps, axis=0), v_ref[...])
    acc_ref[...] = jnp.concatenate(alphas, axis=0) * acc_ref[...] + pv

    @pl.when(kk == pl.num_programs(2) - 1)
    def _():
        lane = lax.broadcasted_iota(jnp.int32, (tq, MLA_HEADS * MLA_V), 1)
        out = jnp.zeros((tq, MLA_HEADS * MLA_V), F32)
        for h in range(MLA_HEADS):
            out = jnp.where(lane // MLA_V == h, acc_ref[h * tq:(h + 1) * tq, :] / l_ref[h][:, :1], out)
        o_ref[...] = out


def _flash_mla(q, k, v, *, nb, t_all, q_start, q_len, k_len, tk):
    tq = TM
    tpb_q, tpb_k = t_all // tq, t_all // tk
    q0 = q_start // tq
    nq = q_len // tq
    return pl.pallas_call(
        _flash_mla_kernel,
        grid=(nb, nq, k_len // tk),
        in_specs=[pl.BlockSpec((tq, 512), lambda b, i, kk: (b * tpb_q + q0 + i, 0)),
                  pl.BlockSpec((tk, 512), lambda b, i, kk: (b * tpb_k + kk, 0)),
                  pl.BlockSpec((tk, 256), lambda b, i, kk: (b * tpb_k + kk, 0))],
        out_specs=pl.BlockSpec((tq, 256), lambda b, i, kk: (b * nq + i, 0)),
        out_shape=jax.ShapeDtypeStruct((nb * q_len, 256), F32),
        scratch_shapes=[pltpu.VMEM((MLA_HEADS, tq, 128), F32), pltpu.VMEM((MLA_HEADS, tq, 128), F32),
                        pltpu.VMEM((MLA_HEADS * tq, 256), F32)],
        compiler_params=_cparams(("parallel", "parallel", "arbitrary")),
        name="flash_mla",
    )(q, k, v)


def _diff_prep(p, cos_tab, sin_tab, *, nb, tpb):
    n = p.shape[0]
    tok = lambda b, r: (b * tpb + r, 0)
    pos = lambda b, r: (r, 0)
    return pl.pallas_call(
        _diff_prep_kernel_cols,
        grid=(nb, tpb),
        in_specs=[pl.BlockSpec((TM, 256), lambda b, r, c=c: (b * tpb + r, EXT_DIFF // 256 + c))
                  for c in range(5)]
                 + [pl.BlockSpec((TM, 256), pos), pl.BlockSpec((TM, 256), pos)],
        out_specs=[pl.BlockSpec((TM, 256), tok)] * 3,
        out_shape=[jax.ShapeDtypeStruct((n, 256), BF16)] * 3,
        compiler_params=_cparams(("parallel", "parallel")),
        name="diff_prep",
    )(p, p, p, p, p, cos_tab, sin_tab)


def _diff_prep_kernel_cols(q_in, k_in, v_in, qr_in, kr_in, cos_ref, sin_ref, q_ref, k_ref, v_ref):
    cos, sin = cos_ref[...], sin_ref[...]
    scale = DIFF_HD ** -0.5 * LOG2E
    q_ref[...] = ((q_in[...] * cos + qr_in[...] * sin) * scale).astype(BF16)
    k_ref[...] = (k_in[...] * cos + kr_in[...] * sin).astype(BF16)
    v_ref[...] = v_in[...].astype(BF16)


def _flash_diff_kernel(q_ref, k_ref, v_ref, lam_ref, g_ref, o_ref, qs_ref, m_ref, l_ref, acc_ref, *, lam_init):
    kk = pl.program_id(2)
    tq = q_ref.shape[0]
    nsm = 2 * DIFF_HEADS

    @pl.when(kk == 0)
    def _():
        m_ref[...] = jnp.full(m_ref.shape, -jnp.inf, F32)
        l_ref[...] = jnp.zeros(l_ref.shape, F32)
        acc_ref[...] = jnp.zeros(acc_ref.shape, F32)
        q = q_ref[...]
        lane = lax.broadcasted_iota(jnp.int32, (tq, 256), 1)
        for i in range(nsm):
            qs_ref[i * tq:(i + 1) * tq, :] = jnp.where((lane // DIFF_HD) == i, q, jnp.zeros_like(q))

    k = k_ref[...]
    ps, alphas = [], []
    for i in range(nsm):
        rows = slice(i * tq, (i + 1) * tq)
        s = _dot_nt(qs_ref[rows, :], k)
        m_prev = m_ref[rows, :][:, :1]
        m_new = jnp.maximum(m_prev, jnp.max(s, axis=-1, keepdims=True))
        alpha = jnp.exp2(m_prev - m_new)
        p = jnp.exp2(s - m_new)
        l_ref[rows, :] = jnp.broadcast_to(alpha * l_ref[rows, :][:, :1] + jnp.sum(p, axis=-1, keepdims=True),
                                          (tq, 128))
        m_ref[rows, :] = jnp.broadcast_to(m_new, (tq, 128))
        alphas.append(alpha)
        ps.append(p.astype(BF16))
    pv = _dot(jnp.concatenate(ps, axis=0), v_ref[...])
    acc_ref[...] = jnp.concatenate(alphas, axis=0) * acc_ref[...] + pv

    @pl.when(kk == pl.num_programs(2) - 1)
    def _():
        lane = lax.broadcasted_iota(jnp.int32, (tq, 256), 1)
        o = jnp.zeros((tq, 256), F32)
        for h in range(DIFF_HEADS):
            r0, r1 = (2 * h) * tq, (2 * h + 1) * tq
            o0 = acc_ref[r0:r0 + tq, :] / l_ref[r0:r0 + tq, :][:, :1]
            o1 = acc_ref[r1:r1 + tq, :] / l_ref[r1:r1 + tq, :][:, :1]
            o = jnp.where((lane // (2 * DIFF_HD)) == h, o0 - lam_ref[...] * o1, o)
        ms = _split_dot(o * o, _block_ones(256, 2 * DIFF_HD)) * (1.0 / (2 * DIFF_HD))
        o_ref[...] = o * lax.rsqrt(ms + DIFF_EPS) * g_ref[...] * (1.0 - lam_init)


def _flash_diff(q, k, v, lam_row, g_row, *, lam_init, nb, t_all, q_start, q_len, k_len, tk):
    tq = TM
    tpb_q, tpb_k = t_all // tq, t_all // tk
    q0 = q_start // tq
    nq = q_len // tq
    return pl.pallas_call(
        functools.partial(_flash_diff_kernel, lam_init=lam_init),
        grid=(nb, nq, k_len // tk),
        in_specs=[pl.BlockSpec((tq, 256), lambda b, i, kk: (b * tpb_q + q0 + i, 0)),
                  pl.BlockSpec((tk, 256), lambda b, i, kk: (b * tpb_k + kk, 0)),
                  pl.BlockSpec((tk, 256), lambda b, i, kk: (b * tpb_k + kk, 0)),
                  pl.BlockSpec((1, 256), lambda b, i, kk: (0, 0)),
                  pl.BlockSpec((1, 256), lambda b, i, kk: (0, 0))],
        out_specs=pl.BlockSpec((tq, 256), lambda b, i, kk: (b * nq + i, 0)),
        out_shape=jax.ShapeDtypeStruct((nb * q_len, 256), F32),
        scratch_shapes=[pltpu.VMEM((2 * DIFF_HEADS * tq, 256), BF16),
                        pltpu.VMEM((2 * DIFF_HEADS * tq, 128), F32),
                        pltpu.VMEM((2 * DIFF_HEADS * tq, 128), F32),
                        pltpu.VMEM((2 * DIFF_HEADS * tq, 256), F32)],
        compiler_params=_cparams(("parallel", "parallel", "arbitrary")),
        name="flash_diff",
    )(q, k, v, lam_row, g_row)


def _rw_prep_kernel(p_ref, prev_ref, next_ref, mu_ref, wlo_ref, g2_ref, vec_ref,
                    r_ref, v_ref, kk_ref, w0_ref, k0_ref, b0_ref, w1_ref, k1_ref, b1_ref,
                    bonus_ref, gate_ref, *, r0, lat_last):
    r = pl.program_id(1) + r0
    p = p_ref[...]
    row = lax.broadcasted_iota(jnp.int32, p.shape, 0)
    first_tile = jnp.logical_or(r == 0, r == 1)
    last_tile = jnp.logical_or(r == 0, r == lat_last)
    prev_row = jnp.where(first_tile, 0.0, prev_ref[7:8, :])
    next_row = jnp.where(last_tile, 0.0, next_ref[0:1, :])
    up = jnp.where(row == 0, prev_row, pltpu.roll(p, 1, 0))
    dn = jnp.where(row == TM - 1, next_row, pltpu.roll(p, TM - 1, 0))
    z = p + (0.5 * (up + dn) - p) * mu_ref[...]

    hw = RW_HEADS * RW_HEAD
    rr, k, v = z[:, :hw], z[:, hw:2 * hw], z[:, 2 * hw:3 * hw]
    lo = z[:, 3 * hw:3 * hw + 128]
    lane = lax.broadcasted_iota(jnp.int32, lo.shape, 1)
    lo = jnp.where(lane < 64, jnp.tanh(lo), lo).astype(BF16)
    wa = _dot(lo, wlo_ref[...])
    gate_ref[...] = _dot(_sigmoid(z[:, 3 * hw + 128:]).astype(BF16), g2_ref[...])

    e4 = _block_ones(hw, RW_HEAD)
    k_k, k_a, r_k = vec_ref[0:1, :], vec_ref[1:2, :], vec_ref[2:3, :]
    kk = k * k_k
    nrm = jnp.maximum(jnp.sqrt(_split_dot(kk * kk, e4)), 1e-12)
    kk = kk / nrm
    r_ref[...] = rr
    v_ref[...] = v
    kk_ref[...] = kk
    ksum = jnp.zeros_like(k)
    for d, (w_ref, kd_ref, b_ref) in enumerate(((w0_ref, k0_ref, b0_ref), (w1_ref, k1_ref, b1_ref))):
        w0 = vec_ref[3 + d:4 + d, :]
        a0 = vec_ref[5 + d:6 + d, :]
        wd = -_softplus(-(w0 + wa[:, d * hw:(d + 1) * hw])) - 0.5
        w_ref[...] = jnp.exp(-jnp.exp(wd))
        ad = _sigmoid(a0 + wa[:, (2 + d) * hw:(3 + d) * hw])
        kd = k * (1.0 + (ad - 1.0) * k_a)
        kd_ref[...] = kd
        b_ref[...] = kk * ad
        ksum = ksum + kd
    bonus_ref[...] = _split_dot(rr * ksum * r_k, e4) * v


def _rw_prep(p, mu, wlo, g2, vecs, *, nb, tpb, r0, nt):
    n_out = nb * nt * TM
    hw = RW_HEADS * RW_HEAD
    n_rows8 = p.shape[0] // 8
    tok = lambda b, r: (b * tpb + r0 + r, EXT_RW // 1024)
    prev = lambda b, r: (jnp.maximum((b * tpb + r0 + r) * (TM // 8) - 1, 0), EXT_RW // 1024)
    nxt = lambda b, r: (jnp.minimum((b * tpb + r0 + r + 1) * (TM // 8), n_rows8 - 1), EXT_RW // 1024)
    out = lambda b, r: (b * nt + r, 0)
    full = lambda b, r: (0, 0)
    return pl.pallas_call(
        functools.partial(_rw_prep_kernel, r0=r0, lat_last=tpb - 1),
        grid=(nb, nt),
        in_specs=[pl.BlockSpec((TM, 1024), tok),
                  pl.BlockSpec((8, 1024), prev),
                  pl.BlockSpec((8, 1024), nxt),
                  pl.BlockSpec((1, 1024), full),
                  pl.BlockSpec((128, 4 * hw), full),
                  pl.BlockSpec((128, hw), full),
                  pl.BlockSpec((8, hw), full)],
        out_specs=[pl.BlockSpec((TM, hw), out)] * 11,
        out_shape=[jax.ShapeDtypeStruct((n_out, hw), F32)] * 11,
        compiler_params=_cparams(("parallel", "parallel")),
        name="rw_prep",
    )(p, p, p, mu, wlo, g2, vecs)


def _rw_scan_kernel(rf, vf, kkf, wf, kf, bf, rb, vb, kkb, wb, kb, bb, yf_ref, yb_ref, s_ref, *, nb):
    c = RW_CHUNK

    @pl.when(pl.program_id(0) == 0)
    def _():
        s_ref[...] = jnp.zeros(s_ref.shape, F32)

    e2 = _block_ones(128, RW_HEAD)
    z2 = jnp.zeros((128, 128), BF16)
    rhs_pair = jnp.concatenate([jnp.concatenate([e2, z2], axis=1), jnp.concatenate([z2, e2], axis=1)], axis=0)
    e22 = jnp.concatenate([e2, e2], axis=0)
    lane = lax.broadcasted_iota(jnp.int32, (RW_HEAD, 128), 1)
    sub = lax.broadcasted_iota(jnp.int32, (RW_HEAD, 128), 0)
    diag = (lane % RW_HEAD) == sub
    sub8 = lax.broadcasted_iota(jnp.int32, (8, 128), 0)
    dirs = ((rf, vf, kkf, wf, kf, bf, yf_ref), (rb, vb, kkb, wb, kb, bb, yb_ref))

    def allreduce_rows(x):
        t = x[0:8]
        for i in range(1, 8):
            t = t + x[8 * i:8 * i + 8]
        for sh in (4, 2, 1):
            t = t + pltpu.roll(t, sh, 0)
        return t

    def col(row):
        return jnp.where(diag, row, 0.0).astype(BF16)

    def group(g, carry):
        tiles, ytiles = {}, {}
        for d, refs in enumerate(dirs):
            base = pl.multiple_of((g if d == 0 else c // 8 - 1 - g) * 8, 8)
            if d == 0:
                prev = lambda x, sh: jnp.where(sub8 >= sh, pltpu.roll(x, sh, 0), 1.0)
            else:
                prev = lambda x, sh: jnp.where(sub8 < 8 - sh, pltpu.roll(x, 8 - sh, 0), 1.0)
            last = 7 if d == 0 else 0
            for b in range(nb):
                for hp in range(2):
                    r_, v_, kk_, w_, k_, b_ = [ref[b, pl.ds(base, 8), pl.ds(hp * 128, 128)] for ref in refs[:6]]
                    gam = w_
                    for sh in (1, 2, 4):
                        gam = gam * prev(gam, sh)
                    inv = 1.0 / gam
                    tiles[d, b, hp] = (base, v_, b_ * inv, kk_ * prev(gam, 1), k_ * inv, r_ * gam,
                                       gam[last:last + 1, :])
                    ytiles[d, b, hp] = jnp.zeros((8, 128), F32)
        units = [(d, b, hp) for d in range(2) for b in range(nb) for hp in range(2)]
        for jj in range(8):
            lhs = []
            for (d, b, hp) in units:
                rw = slice(jj, jj + 1) if d == 0 else slice(7 - jj, 8 - jj)
                _, _, bh, kkh, kh, rh, _ = tiles[d, b, hp]
                lhs.append(jnp.concatenate([col(bh[rw]), col(kkh[rw])], axis=1))
                lhs.append(jnp.concatenate([col(kh[rw]), col(rh[rw])], axis=1))
            cm = _dot(jnp.concatenate(lhs, axis=0), rhs_pair)
            for u, (d, b, hp) in enumerate(units):
                j = jj if d == 0 else 7 - jj
                v_ = tiles[d, b, hp][1]
                r0 = 2 * u * RW_HEAD
                bc, kkc = cm[r0:r0 + RW_HEAD, :128], cm[r0:r0 + RW_HEAD, 128:]
                kc, rc = cm[r0 + RW_HEAD:r0 + 2 * RW_HEAD, :128], cm[r0 + RW_HEAD:r0 + 2 * RW_HEAD, 128:]
                s = s_ref[u]
                sa = jnp.concatenate([allreduce_rows(kkc * s)] * 8, axis=0)
                s = s - bc * sa + kc * v_[j:j + 1]
                s_ref[u] = s
                ytiles[d, b, hp] = jnp.where(sub8 == j, allreduce_rows(rc * s), ytiles[d, b, hp])
        for d, refs in enumerate(dirs):
            for b in range(nb):
                for hp in range(2):
                    u = (d * nb + b) * 2 + hp
                    gl = tiles[d, b, hp][6]
                    gh = gl.astype(BF16).astype(F32)
                    s_ref[u] = s_ref[u] * _dot(jnp.concatenate([col(gh), col(gl - gh)], axis=1), e22)
                    refs[6][b, pl.ds(tiles[d, b, hp][0], 8), pl.ds(hp * 128, 128)] = ytiles[d, b, hp]
        return carry

    lax.fori_loop(0, c // 8, group, 0)


def _rw_scan(r, v, kk, w0, k0, b0, w1, k1, b1, *, nb, t_all, n_ctx):
    c = RW_CHUNK
    nc, ncc = t_all // c, n_ctx // c
    hw = RW_HEADS * RW_HEAD
    shp = lambda a: a.reshape(nb, t_all, hw)
    fwd = lambda j: (0, j, 0)
    bwd = lambda j: (0, jnp.where(j < ncc, ncc - 1 - j, nc - 1 - (j - ncc)), 0)
    blk = (nb, c, hw)
    yshape = jax.ShapeDtypeStruct((nb, t_all, hw), F32)
    yf, yb = pl.pallas_call(
        functools.partial(_rw_scan_kernel, nb=nb),
        grid=(nc,),
        in_specs=[pl.BlockSpec(blk, fwd)] * 6 + [pl.BlockSpec(blk, bwd)] * 6,
        out_specs=[pl.BlockSpec(blk, fwd), pl.BlockSpec(blk, bwd)],
        out_shape=[yshape, yshape],
        scratch_shapes=[pltpu.VMEM((2 * nb * 2, RW_HEAD, 128), F32)],
        compiler_params=_cparams(("arbitrary",)),
        name="rw_scan",
    )(shp(r), shp(v), shp(kk), shp(w0), shp(k0), shp(b0),
      shp(r), shp(v), shp(kk), shp(w1), shp(k1), shp(b1))

    return yf.reshape(nb * t_all, hw), yb.reshape(nb * t_all, hw)


def _rw_post_kernel(yf_ref, yb_ref, bonus_ref, gate_ref, vec_ref, o_ref):
    e4 = _block_ones(RW_HEADS * RW_HEAD, RW_HEAD)
    y = yf_ref[...] + yb_ref[...]
    mean = _split_dot(y, e4) * (1.0 / RW_HEAD)
    yc = y - mean
    var = _split_dot(yc * yc, e4) * (1.0 / RW_HEAD)
    yn = yc * lax.rsqrt(var + RW_LN_EPS) * vec_ref[0:1, :] + vec_ref[1:2, :]
    o_ref[...] = (yn + bonus_ref[...]) * gate_ref[...]


def _rw_post(yf, yb, bonus, gate, vecs, *, nb, tpb_y, r0_y, nt):
    hw = RW_HEADS * RW_HEAD
    n_out = bonus.shape[0]
    ytok = lambda b, r: (b * tpb_y + r0_y + r, 0)
    tok = lambda b, r: (b * nt + r, 0)
    return pl.pallas_call(
        _rw_post_kernel,
        grid=(nb, nt),
        in_specs=[pl.BlockSpec((TM, hw), ytok), pl.BlockSpec((TM, hw), ytok),
                  pl.BlockSpec((TM, hw), tok), pl.BlockSpec((TM, hw), tok),
                  pl.BlockSpec((8, hw), lambda b, r: (0, 0))],
        out_specs=pl.BlockSpec((TM, hw), tok),
        out_shape=jax.ShapeDtypeStruct((n_out, hw), F32),
        compiler_params=_cparams(("parallel", "parallel")),
        name="rw_post",
    )(yf, yb, bonus, gate, vecs)


def _s5_scan_kernel(uf_ref, ub_ref, wb_ref, ab_ref, cf_ref, cb_ref, yf_ref, yb_ref, x_ref, st_ref, *, nb):
    c = S5_CHUNK
    nst = S5_GROUPS * S5_STATE

    @pl.when(pl.program_id(0) == 0)
    def _():
        st_ref[...] = jnp.zeros(st_ref.shape, F32)

    dirs = ((uf_ref, cf_ref, yf_ref), (ub_ref, cb_ref, yb_ref))
    for d, (u_ref, _, _) in enumerate(dirs):
        for b in range(nb):
            x_ref[d, b] = _dot(u_ref[b].astype(BF16), wb_ref[:, d * 2 * nst:(d + 1) * 2 * nst])

    def group(g, carry):
        for d in range(2):
            base = pl.multiple_of((g if d == 0 else c // 8 - 1 - g) * 8, 8)
            ar = ab_ref[d, :, 0:nst]
            ai = ab_ref[d, :, nst:2 * nst]
            for b in range(nb):
                u = d * nb + b
                xr = st_ref[u, :, 0:nst]
                xi = st_ref[u, :, nst:2 * nst]
                bur = x_ref[d, b, pl.ds(base, 8), 0:nst]
                bui = x_ref[d, b, pl.ds(base, 8), nst:2 * nst]
                rows_r, rows_i = [None] * 8, [None] * 8
                for jj in range(8):
                    j = jj if d == 0 else 7 - jj
                    xr, xi = (ar * xr - ai * xi + bur[j:j + 1, :], ar * xi + ai * xr + bui[j:j + 1, :])
                    rows_r[j], rows_i[j] = xr, xi
                st_ref[u, :, 0:nst] = xr
                st_ref[u, :, nst:2 * nst] = xi
                x_ref[d, b, pl.ds(base, 8), 0:nst] = jnp.concatenate(rows_r, axis=0)
                x_ref[d, b, pl.ds(base, 8), nst:2 * nst] = jnp.concatenate(rows_i, axis=0)
        return carry

    lax.fori_loop(0, c // 8, group, 0)

    for d, (_, c_ref, y_ref) in enumerate(dirs):
        for b in range(nb):
            y_ref[b] = _dot(x_ref[d, b].astype(BF16), c_ref[...])


def _s5_scan(p, w_b, ab, cf, cb, *, nb, t_all, n_ctx):
    c = S5_CHUNK
    nc, ncc = t_all // c, n_ctx // c
    nst2 = 2 * S5_GROUPS * S5_STATE
    width = S5_GROUPS * S5_GROUP_CH
    p3 = p.reshape(nb, t_all, N_EXT)
    fwd = lambda j: j
    bwd = lambda j: jnp.where(j < ncc, ncc - 1 - j, nc - 1 - (j - ncc))
    yshape = jax.ShapeDtypeStruct((nb, t_all, width), F32)
    full = lambda j: (0, 0)
    yf, yb = pl.pallas_call(
        functools.partial(_s5_scan_kernel, nb=nb),
        grid=(nc,),
        in_specs=[pl.BlockSpec((nb, c, width), lambda j: (0, fwd(j), EXT_S5 // width)),
                  pl.BlockSpec((nb, c, width), lambda j: (0, bwd(j), EXT_S5 // width)),
                  pl.BlockSpec((width, 2 * nst2), full),
                  pl.BlockSpec((2, 1, nst2), lambda j: (0, 0, 0)),
                  pl.BlockSpec((nst2, width), full), pl.BlockSpec((nst2, width), full)],
        out_specs=[pl.BlockSpec((nb, c, width), lambda j: (0, fwd(j), 0)),
                   pl.BlockSpec((nb, c, width), lambda j: (0, bwd(j), 0))],
        out_shape=[yshape, yshape],
        scratch_shapes=[pltpu.VMEM((2, nb, c, nst2), F32), pltpu.VMEM((2 * nb, 1, nst2), F32)],
        compiler_params=_cparams(("arbitrary",)),
        name="s5_scan",
    )(p3, p3, w_b, ab, cf, cb)
    return yf.reshape(nb * t_all, width), yb.reshape(nb * t_all, width)


def _s5_post_kernel(yf_ref, yb_ref, u_ref, d_ref, gw_ref, gb_ref, o_ref):
    y = yf_ref[...] + yb_ref[...] + d_ref[...] * u_ref[...]
    zg = 0.5 * y * (1.0 + jnp.tanh(math.sqrt(2.0 / math.pi) * (y + 0.044715 * (y * y * y))))
    o_ref[...] = zg * _sigmoid(_dot(zg.astype(BF16), gw_ref[...]) + gb_ref[...])


def _s5_post(yf, yb, p, d_row, glu_w, glu_b, *, nb, tpb, r0, nt):
    tok = lambda b, r: (b * tpb + r0 + r, 0)
    full = lambda b, r: (0, 0)
    return pl.pallas_call(
        _s5_post_kernel,
        grid=(nb, nt),
        in_specs=[pl.BlockSpec((TM, 256), tok), pl.BlockSpec((TM, 256), tok),
                  pl.BlockSpec((TM, 256), lambda b, r: (b * tpb + r0 + r, EXT_S5 // 256)),
                  pl.BlockSpec((1, 256), full), pl.BlockSpec((256, 256), full),
                  pl.BlockSpec((1, 256), full)],
        out_specs=pl.BlockSpec((TM, 256), lambda b, r: (b * nt + r, 0)),
        out_shape=jax.ShapeDtypeStruct((nb * nt * TM, 256), F32),
        compiler_params=_cparams(("parallel", "parallel")),
        name="s5_post",
    )(yf, yb, p, d_row, glu_w, glu_b)


def _merge_kernel(ya_ref, yb_ref, ys_ref, yd_ref, g0, g1, g2, g3, h_ref, m2_ref, wb_ref, wo_ref, o_ref):
    acc = None
    for n, (y_ref, g_ref) in enumerate(((ya_ref, g0), (yb_ref, g1), (ys_ref, g2), (yd_ref, g3))):
        term = _sigmoid(g_ref[...]) * _dot(y_ref[...].astype(BF16), wb_ref[n])
        acc = term if acc is None else acc + term
    o_ref[...] = h_ref[...] + m2_ref[0] * _dot(acc.astype(BF16), wo_ref[...])


def _merge(ya, yb, ys, yd, p, h, mod2, w_branch, w_out, *, nb, tpb, r0, nt):
    tok_in = lambda b, r: (b * tpb + r0 + r, 0)
    tok_out = lambda b, r: (b * nt + r, 0)
    full2 = lambda b, r: (0, 0)
    gate = [pl.BlockSpec((TM, D_MODEL), lambda b, r, c=c: (b * tpb + r0 + r, EXT_GATE // D_MODEL + c))
            for c in range(4)]
    return pl.pallas_call(
        _merge_kernel,
        grid=(nb, nt),
        in_specs=[pl.BlockSpec((TM, 256), tok_out)] * 4 + gate
                 + [pl.BlockSpec((TM, D_MODEL), tok_in),
                    pl.BlockSpec((1, 1, D_MODEL), lambda b, r: (jnp.where(r0 + r == 0, nb, b), 0, 0)),
                    pl.BlockSpec((4, 256, D_MODEL), lambda b, r: (0, 0, 0)),
                    pl.BlockSpec((D_MODEL, D_MODEL), full2)],
        out_specs=pl.BlockSpec((TM, D_MODEL), tok_out),
        out_shape=jax.ShapeDtypeStruct((nb * nt * TM, D_MODEL), F32),
        compiler_params=_cparams(("parallel", "parallel")),
        name="merge",
    )(ya, yb, ys, yd, p, p, p, p, h, mod2, w_branch, w_out)


def _router_kernel(h_ref, g_ref, sh_ref, sc_ref, wh_ref, wm_ref, b_ref, x_ref, lg_ref):
    x = _rms(h_ref[...], g_ref[...], NORM_EPS) * (1.0 + sc_ref[0]) + sh_ref[0]
    xh = x.astype(BF16)
    xm = (x - xh.astype(F32)).astype(BF16)
    bits = lax.bitcast_convert_type(xh.astype(F32), jnp.uint32)
    half = D_MODEL // 2
    x_ref[...] = (bits[:, :half] >> 16) | (bits[:, half:] & jnp.uint32(0xFFFF0000))
    lg = (_dot(xh, wh_ref[...]) + _dot(xm, wh_ref[...]) + _dot(xh, wm_ref[...])) + b_ref[...]

    lane = lax.broadcasted_iota(jnp.int32, lg.shape, 1)
    lanef = lane.astype(F32)
    neg = jnp.float32(-jnp.inf)
    big = jnp.float32(1e9)
    rmax = lambda v: jnp.max(v, axis=-1, keepdims=True)
    rmin = lambda v: jnp.min(v, axis=-1, keepdims=True)
    rsum = lambda v: jnp.sum(v, axis=-1, keepdims=True)

    gmask = lane < MOE_GROUPS
    mg = rmax(jnp.where(gmask, lg, neg))
    eg = jnp.where(gmask, jnp.exp(lg - mg), 0.0)
    pg = eg / rsum(eg)
    pg_top = rmax(pg)
    g_sel = rmin(jnp.where(jnp.logical_and(gmask, pg == pg_top), lanef, big))
    lo = MOE_GROUPS + MOE_PER_GROUP * g_sel
    emask = jnp.logical_and(lanef >= lo, lanef < lo + MOE_PER_GROUP)
    me = rmax(jnp.where(emask, lg, neg))
    ee = jnp.where(emask, jnp.exp(lg - me), 0.0)
    pe = jnp.where(emask, ee / rsum(ee), -1.0)
    p1 = rmax(pe)
    i1 = rmin(jnp.where(pe == p1, lanef, big))
    pe2 = jnp.where(lanef == i1, -1.0, pe)
    p2 = rmax(pe2)
    i2 = rmin(jnp.where(pe2 == p2, lanef, big))
    den = p1 + p2
    out = jnp.where(lane == 0, i1 - MOE_GROUPS, 0.0)
    out = jnp.where(lane == 1, i2 - MOE_GROUPS, out)
    out = jnp.where(lane == 2, pg_top * p1 / den, out)
    out = jnp.where(lane == 3, pg_top * p2 / den, out)
    lg_ref[...] = out


def _router(h, g, shift, scale, wh, wm, bias, *, nb, nt, ctx_first):
    tok = lambda b, r: (b * nt + r, 0)
    full = lambda b, r: (0, 0)
    if ctx_first:
        modmap = lambda b, r: (jnp.where(r == 0, nb, b), 0, 0)
    else:
        modmap = lambda b, r: (b, 0, 0)
    n = h.shape[0]
    return pl.pallas_call(
        _router_kernel,
        grid=(nb, nt),
        in_specs=[pl.BlockSpec((TM, D_MODEL), tok), pl.BlockSpec((1, D_MODEL), full),
                  pl.BlockSpec((1, 1, D_MODEL), modmap), pl.BlockSpec((1, 1, D_MODEL), modmap),
                  pl.BlockSpec((D_MODEL, 128), full), pl.BlockSpec((D_MODEL, 128), full),
                  pl.BlockSpec((1, 128), full)],
        out_specs=[pl.BlockSpec((TM, D_MODEL // 2), tok), pl.BlockSpec((TM, 128), tok)],
        out_shape=[jax.ShapeDtypeStruct((n, D_MODEL // 2), jnp.uint32), jax.ShapeDtypeStruct((n, 128), F32)],
        compiler_params=_cparams(("parallel", "parallel")),
        name="router",
    )(h, g, shift, scale, wh, wm, bias)


def _expert_kernel(be_ref, nv_ref, x_ref, wg_ref, wu_ref, wd_ref, o_ref, wgb_ref, wub_ref, wdb_ref):
    i = pl.program_id(0)

    @pl.when(jnp.logical_or(i == 0, be_ref[i] != be_ref[jnp.maximum(i - 1, 0)]))
    def _():
        wgb_ref[...] = wg_ref[0].astype(BF16)
        wub_ref[...] = wu_ref[0].astype(BF16)
        wdb_ref[...] = wd_ref[0].astype(BF16)

    @pl.when(i < nv_ref[0])
    def _():
        u = x_ref[...]
        x = jnp.concatenate([lax.bitcast_convert_type(u << 16, F32),
                             lax.bitcast_convert_type(u & jnp.uint32(0xFFFF0000), F32)], axis=1).astype(BF16)
        hb = _silu(_dot(x, wgb_ref[...])) * _dot(x, wub_ref[...])
        o_ref[...] = _dot(hb.astype(BF16), wdb_ref[...])

    @pl.when(i >= nv_ref[0])
    def _():
        o_ref[...] = jnp.zeros(o_ref.shape, F32)


def _experts(xs, block_e, n_valid, w_gate, w_up, w_down):
    n_slots = xs.shape[0]
    n_blocks = n_slots // MOE_BLK
    wmap = lambda i, be, nv: (be[i], 0, 0)
    return pl.pallas_call(
        _expert_kernel,
        grid_spec=pltpu.PrefetchScalarGridSpec(
            num_scalar_prefetch=2,
            grid=(n_blocks,),
            in_specs=[pl.BlockSpec((MOE_BLK, D_MODEL // 2), lambda i, be, nv: (i, 0)),
                      pl.BlockSpec((1, D_MODEL, D_EXPERT), wmap),
                      pl.BlockSpec((1, D_MODEL, D_EXPERT), wmap),
                      pl.BlockSpec((1, D_EXPERT, D_MODEL), wmap)],
            out_specs=pl.BlockSpec((MOE_BLK, D_MODEL), lambda i, be, nv: (i, 0)),
            scratch_shapes=[pltpu.VMEM((D_MODEL, D_EXPERT), BF16), pltpu.VMEM((D_MODEL, D_EXPERT), BF16),
                            pltpu.VMEM((D_EXPERT, D_MODEL), BF16)]),
        out_shape=jax.ShapeDtypeStruct((n_slots, D_MODEL), F32),
        compiler_params=_cparams(("arbitrary",)),
        name="experts",
    )(block_e, n_valid, xs, w_gate, w_up, w_down)


def _combine_kernel(h_ref, y0_ref, y1_ref, w_ref, m5_ref, g_ref, o_ref, *, final):
    w = w_ref[...]
    y = y0_ref[...] * w[:, 0:1] + y1_ref[...] * w[:, 1:2]
    h = h_ref[...] + m5_ref[0] * y
    if final:
        h = _rms(h, g_ref[...], NORM_EPS)
    o_ref[...] = h


def _combine(h, y01, wts, mod5, g_final, *, nb, nt, ctx_first, final):
    tok = lambda b, r: (b * nt + r, 0)
    if ctx_first:
        modmap = lambda b, r: (jnp.where(r == 0, nb, b), 0, 0)
    else:
        modmap = lambda b, r: (b, 0, 0)
    return pl.pallas_call(
        functools.partial(_combine_kernel, final=final),
        grid=(nb, nt),
        in_specs=[pl.BlockSpec((TM, D_MODEL), tok), pl.BlockSpec((TM, D_MODEL), tok),
                  pl.BlockSpec((TM, D_MODEL), lambda b, r: ((nb + b) * nt + r, 0)),
                  pl.BlockSpec((TM, 128), tok), pl.BlockSpec((1, 1, D_MODEL), modmap),
                  pl.BlockSpec((1, D_MODEL), lambda b, r: (0, 0))],
        out_specs=pl.BlockSpec((TM, D_MODEL), tok),
        out_shape=jax.ShapeDtypeStruct(h.shape, F32),
        compiler_params=_cparams(("parallel", "parallel")),
        name="combine",
    )(h, y01, y01, wts, mod5, g_final)


def _sc_gather(table, idx):
    n, d = idx.shape[0], table.shape[1]
    steps = n // SC_ROWS_PER_STEP
    idx_p = jnp.pad(idx.reshape(steps, SC_ROWS_PER_STEP), ((0, 0), (0, SC_INDEX_TILE - SC_ROWS_PER_STEP)))
    mesh = plsc.VectorSubcoreMesh(core_axis_name="core", subcore_axis_name="subcore")

    @pl.kernel(out_type=jax.ShapeDtypeStruct((n, d), table.dtype), mesh=mesh)
    def gather_kernel(x_hbm, i_hbm, o_hbm):
        def body(i_vmem, o_vmem):
            pltpu.sync_copy(x_hbm.at[i_vmem.at[0, pl.ds(0, SC_ROWS_PER_STEP)]], o_vmem)

        pltpu.emit_pipeline(
            body,
            grid=(steps,),
            in_specs=[pl.BlockSpec((1, SC_INDEX_TILE), index_map=lambda i: (i, 0))],
            out_specs=[pl.BlockSpec((SC_ROWS_PER_STEP, d), index_map=lambda i: (i, 0))],
            core_axis_name=("core", "subcore"),
            dimension_semantics=(pltpu.PARALLEL,),
        )(i_hbm, o_hbm)

    return gather_kernel(table, idx_p)


def _moe(h, g2, shift, scale, mod5, wh, wm, rbias, w_gate, w_up, w_down, g_final, *, layer, nb, nt, ctx_first,
         final):
    n = h.shape[0]
    x_bf, route = _router(h, g2, shift, scale, wh, wm, rbias, nb=nb, nt=nt, ctx_first=ctx_first)
    idx = route[:, :MOE_TOPK].astype(jnp.int32)
    wts = route[:, MOE_TOPK:2 * MOE_TOPK]
    n_as = n * MOE_TOPK
    flat_e = idx.reshape(n_as)
    onehot = (flat_e[:, None] == jnp.arange(MOE_EXPERTS, dtype=jnp.int32)[None, :]).astype(jnp.int32)
    csum = jnp.cumsum(onehot, axis=0)
    counts = csum[-1]
    rank = jnp.sum(jnp.where(onehot > 0, csum - 1, 0), axis=1)
    padded = (counts + MOE_BLK - 1) // MOE_BLK * MOE_BLK
    pad_end = jnp.cumsum(padded)
    pad_start = pad_end - padded
    slot = pad_start[flat_e] + rank
    n_blocks = (n_as + MOE_EXPERTS * (MOE_BLK - 1) + MOE_BLK - 1) // MOE_BLK
    n_slots = n_blocks * MOE_BLK
    slot_tok = (jnp.arange(n_slots, dtype=jnp.int32) % n).at[slot].set(
        jnp.arange(n_as, dtype=jnp.int32) // MOE_TOPK, unique_indices=True)
    starts = jnp.arange(n_blocks, dtype=jnp.int32) * MOE_BLK
    block_e = jnp.minimum(jnp.sum((pad_end[None, :] <= starts[:, None]).astype(jnp.int32), axis=1),
                          MOE_EXPERTS - 1)
    n_valid = (pad_end[-1:] // MOE_BLK).astype(jnp.int32)
    xs = _sc_gather(x_bf, slot_tok)
    ys = _experts(xs, block_e + layer * MOE_EXPERTS, n_valid, w_gate, w_up, w_down)
    y01 = _sc_gather(ys, slot.reshape(n, MOE_TOPK).T.reshape(n_as))
    wts_pad = jnp.pad(wts.astype(F32), ((0, 0), (0, 128 - MOE_TOPK)))
    return _combine(h, y01, wts_pad, mod5, g_final, nb=nb, nt=nt, ctx_first=ctx_first, final=final)


_ROT_SRC = np.array(list(range(8, 16)) + list(range(0, 8)) + list(range(24, 32)) + list(range(16, 24)))
_ROT_SIGN = np.array([-1.0] * 8 + [1.0] * 8 + [-1.0] * 8 + [1.0] * 8, np.float32)


def _rot_cols(w):
    k = w.shape[-1] // ROPE_DIM
    src = np.concatenate([_ROT_SRC + ROPE_DIM * i for i in range(k)])
    sign = np.tile(_ROT_SIGN, k)
    return w[..., src] * sign


def _rope_tables(n_ctx, n_lat):
    rows = n_lat // GRID_W
    row = jnp.repeat(jnp.arange(rows, dtype=F32), GRID_W)
    col = (jnp.arange(rows * GRID_W) % GRID_W).astype(F32)
    nf = ROPE_DIM // 4
    inv = ROPE_BASE ** (-jnp.arange(nf, dtype=F32) / nf)
    ar = row[:, None] * inv
    ac = col[:, None] * inv
    ang = jnp.concatenate([ar, ar, ac, ac], axis=-1)
    cos = jnp.concatenate([jnp.ones((n_ctx, ROPE_DIM), F32), jnp.cos(ang)], axis=0)
    sin = jnp.concatenate([jnp.zeros((n_ctx, ROPE_DIM), F32), jnp.sin(ang)], axis=0)
    return cos, sin


def _block_diag(blocks):
    g, a, b = blocks.shape
    tiled = jnp.tile(blocks.reshape(g * a, b), (1, g))
    rows = lax.broadcasted_iota(jnp.int32, (g * a, g * b), 0) // a
    cols = lax.broadcasted_iota(jnp.int32, (g * a, g * b), 1) // b
    return jnp.where(rows == cols, tiled, 0.0)


def _pick_tk(t_all):
    best = 128
    for tk in range(128, ATTN_TK_MAX + 1, 128):
        if t_all % tk == 0:
            best = tk
    return best


def kernel(x, c, ctx, c_ctx, w_mod, b_mod, norm1_g, norm2_g, w_in, mla_q_norm_g, mla_kv_norm_g, mla_w_uq, mla_w_ukv, rw_mu, rw_w0, rw_w2, rw_a0, rw_a2, rw_g2, rw_k_k, rw_k_a, rw_r_k, rw_lnx_g, rw_lnx_b, s5_a_re, s5_a_im, s5_log_dt, s5_b_re, s5_b_im, s5_c_re, s5_c_im, s5_d, s5_glu_w, s5_glu_b, diff_lq1, diff_lk1, diff_lq2, diff_lk2, diff_subln_g, w_branch, w_out, router_g_w, router_g_b, router_e_w, router_e_b, exp_w_gate, exp_w_up, exp_w_down, final_norm_g):
    nb, n_lat, d = x.shape
    n_ctx = ctx.shape[1]
    depth = w_mod.shape[0]
    t_all = n_ctx + n_lat
    assert d == D_MODEL and n_ctx == TM and n_lat % TM == 0
    tpb = t_all // TM
    tk = _pick_tk(t_all)
    hw = RW_HEADS * RW_HEAD

    cos, sin = _rope_tables(n_ctx, n_lat)
    mla_scale = (MLA_NOPE + MLA_ROPE) ** -0.5 * LOG2E
    z32 = jnp.zeros((t_all, 32), F32)
    cq_tab = jnp.concatenate([jnp.ones((t_all, 64), F32), cos, z32], axis=1) * mla_scale
    sq_tab = jnp.concatenate([jnp.zeros((t_all, 64), F32), sin, z32], axis=1) * mla_scale
    ck_tab = jnp.concatenate([cos, sin, jnp.zeros((t_all, 64), F32)], axis=1)
    dcos = jnp.tile(cos, (1, 8))
    dsin = jnp.tile(sin, (1, 8))

    c_rows = jnp.concatenate([c, c_ctx[None, :], jnp.zeros((8 - nb - 1, d), F32)], axis=0)

    h = jnp.concatenate([ctx, x], axis=1).reshape(nb * t_all, d)

    for l in range(depth):
        last = l == depth - 1
        r0, nt = (1, tpb - 1) if last else (0, tpb)

        mod = _mod(c_rows, w_mod[l].astype(BF16), b_mod[l][None, :], tn=1536)
        mods = [mod[:nb + 1, i * d:(i + 1) * d].reshape(nb + 1, 1, d) for i in range(6)]

        wi = w_in[l]
        o_rw, o_s5, o_df, o_gt = 416, 1440, 1696, 2464
        w_kr = wi[:, 384:416]
        w_dq, w_dk, w_dv = wi[:, o_df:o_df + 256], wi[:, o_df + 256:o_df + 512], wi[:, o_df + 512:o_df + 768]
        w_ext = jnp.concatenate(
            [wi[:, o_rw:o_s5],
             wi[:, :416], _rot_cols(w_kr), jnp.zeros((d, 64), F32),
             wi[:, o_s5:o_df],
             w_dq, w_dk, w_dv, _rot_cols(w_dq), _rot_cols(w_dk),
             wi[:, o_gt:]], axis=1).astype(BF16)
        p = _inproj(h, norm1_g[l][None, :], mods[0], mods[1], w_ext, nb=nb, tpb=tpb)

        wq = mla_w_uq[l].reshape(MLA_Q_LORA, MLA_HEADS, MLA_NOPE + MLA_ROPE)
        zq = jnp.zeros((MLA_Q_LORA, MLA_HEADS, 32), F32)
        wa = jnp.concatenate([wq, zq], axis=2).reshape(MLA_Q_LORA, 512).astype(BF16)
        wb = jnp.concatenate([jnp.zeros((MLA_Q_LORA, MLA_HEADS, 64), F32), _rot_cols(wq[:, :, MLA_NOPE:]), zq],
                             axis=2).reshape(MLA_Q_LORA, 512).astype(BF16)
        wkv = mla_w_ukv[l].reshape(MLA_KV_LORA, MLA_HEADS, MLA_NOPE + MLA_V)
        wk = jnp.concatenate([wkv[:, :, :MLA_NOPE], jnp.zeros((MLA_KV_LORA, MLA_HEADS, 64), F32)],
                             axis=2).reshape(MLA_KV_LORA, 512).astype(BF16)
        wv = wkv[:, :, MLA_NOPE:].reshape(MLA_KV_LORA, MLA_HEADS * MLA_V).astype(BF16)
        pk_np = np.zeros((128, 512), np.float32)
        for hh in range(MLA_HEADS):
            for i in range(32):
                pk_np[i, hh * 128 + 64 + i] = 1.0
                pk_np[32 + i, hh * 128 + 64 + i] = 1.0
        pk = jnp.asarray(pk_np, BF16)
        q_m, k_m, v_m = _mla_prep(p, mla_q_norm_g[l][None, :], mla_kv_norm_g[l][None, :], wa, wb, wk, wv, pk,
                                  cq_tab, sq_tab, ck_tab, nb=nb, tpb=tpb)
        ya_lat = _flash_mla(q_m, k_m, v_m, nb=nb, t_all=t_all, q_start=n_ctx, q_len=n_lat, k_len=t_all, tk=tk)

        q_d, k_d, v_d = _diff_prep(p, dcos, dsin, nb=nb, tpb=tpb)
        lam_init = 0.8 - 0.6 * math.exp(-0.3 * l)
        lam = (jnp.exp(jnp.sum(diff_lq1[l] * diff_lk1[l])) - jnp.exp(jnp.sum(diff_lq2[l] * diff_lk2[l])) + lam_init)
        lam_row = jnp.full((1, 256), lam, F32)
        g_row = jnp.tile(diff_subln_g[l], DIFF_HEADS)[None, :]
        yd_lat = _flash_diff(q_d, k_d, v_d, lam_row, g_row, lam_init=lam_init, nb=nb, t_all=t_all,
                             q_start=n_ctx, q_len=n_lat, k_len=t_all, tk=tk)
        if last:
            ya, yd = ya_lat, yd_lat
        else:
            ya_ctx = _flash_mla(q_m, k_m, v_m, nb=nb, t_all=t_all, q_start=0, q_len=n_ctx, k_len=n_ctx, tk=n_ctx)
            yd_ctx = _flash_diff(q_d, k_d, v_d, lam_row, g_row, lam_init=lam_init, nb=nb, t_all=t_all,
                                 q_start=0, q_len=n_ctx, k_len=n_ctx, tk=n_ctx)
            comb = lambda a_c, a_l: jnp.concatenate(
                [a_c.reshape(nb, n_ctx, -1), a_l.reshape(nb, n_lat, -1)], axis=1).reshape(nb * t_all, -1)
            ya, yd = comb(ya_ctx, ya_lat), comb(yd_ctx, yd_lat)

        wlo = jnp.zeros((128, 4 * hw), F32)
        wlo = wlo.at[:64, 0:hw].set(rw_w2[l, 0]).at[:64, hw:2 * hw].set(rw_w2[l, 1])
        wlo = wlo.at[64:, 2 * hw:3 * hw].set(rw_a2[l, 0]).at[64:, 3 * hw:].set(rw_a2[l, 1])
        vecs = jnp.stack([rw_k_k[l], rw_k_a[l], rw_r_k[l].reshape(hw), rw_w0[l, 0], rw_w0[l, 1],
                          rw_a0[l, 0], rw_a0[l, 1], jnp.zeros((hw,), F32)], axis=0)
        (r_, v_, kk_, w0_, k0_, b0_, w1_, k1_, b1_, bonus, gate_rw) = _rw_prep(
            p, rw_mu[l][None, :], wlo.astype(BF16), rw_g2[l].astype(BF16), vecs, nb=nb, tpb=tpb, r0=0, nt=tpb)
        yf, yb_ = _rw_scan(r_, v_, kk_, w0_, k0_, b0_, w1_, k1_, b1_, nb=nb, t_all=t_all, n_ctx=n_ctx)
        ln_vecs = jnp.concatenate([rw_lnx_g[l][None, :], rw_lnx_b[l][None, :], jnp.zeros((6, hw), F32)], axis=0)
        if last:
            trim = lambda a: a.reshape(nb, t_all, hw)[:, n_ctx:].reshape(nb * n_lat, hw)
            bonus, gate_rw = trim(bonus), trim(gate_rw)
        y_rw = _rw_post(yf, yb_, bonus, gate_rw, ln_vecs, nb=nb, tpb_y=tpb, r0_y=r0, nt=nt)

        bbs, abs_, cfs = [], [], []
        for dd in range(2):
            lr, li = s5_a_re[l, dd], s5_a_im[l, dd]
            dt = jnp.exp(s5_log_dt[l, dd])[:, None]
            mag = jnp.exp(lr * dt)
            ab_re, ab_im = mag * jnp.cos(li * dt), mag * jnp.sin(li * dt)
            den = lr * lr + li * li
            nr, ni = ab_re - 1.0, ab_im
            cf_re = (nr * lr + ni * li) / den
            cf_im = (ni * lr - nr * li) / den
            bre, bim = s5_b_re[l, dd], s5_b_im[l, dd]
            bb_re = cf_re[..., None] * bre - cf_im[..., None] * bim
            bb_im = cf_re[..., None] * bim + cf_im[..., None] * bre
            bbs.append(jnp.concatenate([_block_diag(bb_re.transpose(0, 2, 1)),
                                        _block_diag(bb_im.transpose(0, 2, 1))], axis=1))
            abs_.append(jnp.concatenate([ab_re.reshape(-1), ab_im.reshape(-1)])[None, :])
            cfs.append(jnp.concatenate([_block_diag(s5_c_re[l, dd].transpose(0, 2, 1)),
                                        -_block_diag(s5_c_im[l, dd].transpose(0, 2, 1))], axis=0))
        yf_s5, yb_s5 = _s5_scan(p, jnp.concatenate(bbs, axis=1).astype(BF16), jnp.stack(abs_, axis=0),
                                cfs[0].astype(BF16), cfs[1].astype(BF16), nb=nb, t_all=t_all, n_ctx=n_ctx)
        y_s5 = _s5_post(yf_s5, yb_s5, p, s5_d[l].reshape(1, 256), s5_glu_w[l].astype(BF16),
                        s5_glu_b[l][None, :], nb=nb, tpb=tpb, r0=r0, nt=nt)

        h = _merge(ya, y_rw, y_s5, yd, p, h, mods[2], w_branch[l].astype(BF16), w_out[l].astype(BF16),
                   nb=nb, tpb=tpb, r0=r0, nt=nt)

        wr = jnp.concatenate([router_g_w[l], router_e_w[l], jnp.zeros((d, 128 - MOE_GROUPS - MOE_EXPERTS), F32)], axis=1)
        wr_h = wr.astype(BF16)
        wr_m = (wr - wr_h.astype(F32)).astype(BF16)
        rbias = jnp.concatenate([router_g_b[l], router_e_b[l],
                                 jnp.zeros((128 - MOE_GROUPS - MOE_EXPERTS,), F32)])[None, :]
        h = _moe(h, norm2_g[l][None, :], mods[3], mods[4], mods[5], wr_h, wr_m, rbias,
                 exp_w_gate.reshape(depth * MOE_EXPERTS, d, D_EXPERT), exp_w_up.reshape(depth * MOE_EXPERTS, d, D_EXPERT),
                 exp_w_down.reshape(depth * MOE_EXPERTS, D_EXPERT, d),
                 final_norm_g[None, :], layer=l, nb=nb, nt=nt, ctx_first=not last, final=last)

    return h.reshape(nb, n_lat, d)
```

```python
import functools
import math

import jax
import jax.numpy as jnp
import numpy as np
from jax import lax
from jax.experimental import pallas as pl
from jax.experimental.pallas import tpu as pltpu
from jax.experimental.pallas import tpu_sc as plsc

F32 = jnp.float32
BF16 = jnp.bfloat16

TM = 256
VMEM_LIMIT = 48 * 1024 * 1024

D_MODEL = 1024
GRID_W = 64
ROPE_DIM = 32
ROPE_BASE = 10000.0
NORM_EPS = 1e-6
MLA_HEADS, MLA_NOPE, MLA_ROPE, MLA_V = 4, 64, 32, 64
MLA_Q_LORA, MLA_KV_LORA = 256, 128
RW_HEADS, RW_HEAD = 4, 64
RW_LN_EPS = 64e-5
S5_GROUPS, S5_GROUP_CH, S5_STATE = 16, 16, 64
DIFF_HEADS, DIFF_HD = 4, 32
DIFF_EPS = 1e-5
MOE_GROUPS, MOE_PER_GROUP, MOE_TOPK = 4, 8, 2
MOE_EXPERTS = MOE_GROUPS * MOE_PER_GROUP
D_EXPERT = 512
MOE_BLK = 256
RW_CHUNK = 128
S5_CHUNK = 256
ATTN_TK_MAX = 2816
SC_INDEX_TILE = 128
SC_ROWS_PER_STEP = 32
LOG2E = math.log2(math.e)

EXT_RW, EXT_MLA, EXT_S5, EXT_DIFF, EXT_GATE = 0, 1024, 1536, 1792, 3072
N_EXT = 7168


def _cparams(sem, vmem=VMEM_LIMIT):
    return pltpu.CompilerParams(dimension_semantics=sem, vmem_limit_bytes=vmem)


def _dot(a, b):
    return jnp.dot(a, b, preferred_element_type=F32)


def _dot_nt(a, b):
    return lax.dot_general(a, b, (((1,), (1,)), ((), ())), preferred_element_type=F32)


def _split_dot(x, e):
    hi = x.astype(BF16)
    mid = (x - hi.astype(F32)).astype(BF16)
    return _dot(hi, e) + _dot(mid, e)


def _block_ones(n, blk):
    r = lax.broadcasted_iota(jnp.int32, (n, n), 0) // blk
    c = lax.broadcasted_iota(jnp.int32, (n, n), 1) // blk
    return (r == c).astype(BF16)


def _sigmoid(x):
    return 1.0 / (1.0 + jnp.exp(-x))


def _silu(x):
    return x * _sigmoid(x)


def _softplus(x):
    return jnp.maximum(x, 0.0) + jnp.log(1.0 + jnp.exp(-jnp.abs(x)))


def _rms(x, g, eps):
    return x * lax.rsqrt(jnp.mean(x * x, axis=-1, keepdims=True) + eps) * g


def _mod_kernel(x_ref, w_ref, b_ref, o_ref):
    o_ref[...] = _dot(_silu(x_ref[...]).astype(BF16), w_ref[...]) + b_ref[...]


def _mod(x, w, b, *, tn):
    m, k = x.shape
    n = w.shape[1]
    return pl.pallas_call(
        _mod_kernel,
        grid=(n // tn,),
        in_specs=[pl.BlockSpec((m, k), lambda j: (0, 0)),
                  pl.BlockSpec((k, tn), lambda j: (0, j)),
                  pl.BlockSpec((1, tn), lambda j: (0, j))],
        out_specs=pl.BlockSpec((m, tn), lambda j: (0, j)),
        out_shape=jax.ShapeDtypeStruct((m, n), F32),
        compiler_params=_cparams(("parallel",)),
        name="mod",
    )(x, w, b)


def _inproj_kernel(h_ref, g_ref, sh_ref, sc_ref, w_ref, o_ref):
    x = _rms(h_ref[...], g_ref[...], NORM_EPS)
    xn = (x * (1.0 + sc_ref[0]) + sh_ref[0]).astype(BF16)
    o_ref[...] = _dot(xn, w_ref[...])


def _inproj(h, g, shift, scale, w_ext, *, nb, tpb):
    n = h.shape[0]
    tn = N_EXT // 2
    modmap = lambda j, b, r: (jnp.where(r == 0, nb, b), 0, 0)
    return pl.pallas_call(
        _inproj_kernel,
        grid=(N_EXT // tn, nb, tpb),
        in_specs=[pl.BlockSpec((TM, D_MODEL), lambda j, b, r: (b * tpb + r, 0)),
                  pl.BlockSpec((1, D_MODEL), lambda j, b, r: (0, 0)),
                  pl.BlockSpec((1, 1, D_MODEL), modmap),
                  pl.BlockSpec((1, 1, D_MODEL), modmap),
                  pl.BlockSpec((D_MODEL, tn), lambda j, b, r: (0, j))],
        out_specs=pl.BlockSpec((TM, tn), lambda j, b, r: (b * tpb + r, j)),
        out_shape=jax.ShapeDtypeStruct((n, N_EXT), F32),
        compiler_params=_cparams(("parallel", "parallel", "parallel")),
        name="inproj",
    )(h, g, shift, scale, w_ext)


def _mla_prep_kernel(p_ref, gq_ref, gkv_ref, wa_ref, wb_ref, wk_ref, wv_ref, pk_ref,
                     cq_ref, sq_ref, ck_ref, q_ref, k_ref, v_ref):
    seg = p_ref[...]
    nq = _rms(seg[:, :MLA_Q_LORA], gq_ref[...], NORM_EPS).astype(BF16)
    nkv = _rms(seg[:, MLA_Q_LORA:MLA_Q_LORA + MLA_KV_LORA], gkv_ref[...], NORM_EPS).astype(BF16)
    cq = jnp.concatenate([cq_ref[...]] * MLA_HEADS, axis=1)
    sq = jnp.concatenate([sq_ref[...]] * MLA_HEADS, axis=1)
    q = _dot(nq, wa_ref[...]) * cq + _dot(nq, wb_ref[...]) * sq
    q_ref[...] = q.astype(BF16)
    kr = (seg[:, 384:512] * ck_ref[...]).astype(BF16)
    k_ref[...] = (_dot(nkv, wk_ref[...]) + _dot(kr, pk_ref[...])).astype(BF16)
    v_ref[...] = _dot(nkv, wv_ref[...]).astype(BF16)


def _mla_prep(p, gq, gkv, wa, wb, wk, wv, pk, cq_tab, sq_tab, ck_tab, *, nb, tpb):
    n = p.shape[0]
    tok = lambda b, r: (b * tpb + r, 0)
    pos = lambda b, r: (r, 0)
    full = lambda b, r: (0, 0)
    return pl.pallas_call(
        _mla_prep_kernel,
        grid=(nb, tpb),
        in_specs=[pl.BlockSpec((TM, 512), lambda b, r: (b * tpb + r, EXT_MLA // 512)),
                  pl.BlockSpec((1, MLA_Q_LORA), full),
                  pl.BlockSpec((1, MLA_KV_LORA), full),
                  pl.BlockSpec((MLA_Q_LORA, 512), full),
                  pl.BlockSpec((MLA_Q_LORA, 512), full),
                  pl.BlockSpec((MLA_KV_LORA, 512), full),
                  pl.BlockSpec((MLA_KV_LORA, 256), full),
                  pl.BlockSpec((128, 512), full),
                  pl.BlockSpec((TM, 128), pos),
                  pl.BlockSpec((TM, 128), pos),
                  pl.BlockSpec((TM, 128), pos)],
        out_specs=[pl.BlockSpec((TM, 512), tok),
                   pl.BlockSpec((TM, 512), tok),
                   pl.BlockSpec((TM, 256), tok)],
        out_shape=[jax.ShapeDtypeStruct((n, 512), BF16),
                   jax.ShapeDtypeStruct((n, 512), BF16),
                   jax.ShapeDtypeStruct((n, 256), BF16)],
        compiler_params=_cparams(("parallel", "parallel")),
        name="mla_prep",
    )(p, gq, gkv, wa, wb, wk, wv, pk, cq_tab, sq_tab, ck_tab)


def _flash_mla_kernel(q_ref, k_ref, v_ref, o_ref, m_ref, l_ref, acc_ref):
    kk = pl.program_id(2)
    tq = q_ref.shape[0]

    @pl.when(kk == 0)
    def _():
        m_ref[...] = jnp.full(m_ref.shape, -jnp.inf, F32)
        l_ref[...] = jnp.zeros(l_ref.shape, F32)
        acc_ref[...] = jnp.zeros(acc_ref.shape, F32)

    ps, alphas = [], []
    for h in range(MLA_HEADS):
        s = _dot_nt(q_ref[:, h * 128:(h + 1) * 128], k_ref[:, h * 128:(h + 1) * 128])
        m_prev = m_ref[h][:, :1]
        m_new = jnp.maximum(m_prev, jnp.max(s, axis=-1, keepdims=True))
        alpha = jnp.exp2(m_prev - m_new)
        p = jnp.exp2(s - m_new)
        l_ref[h] = jnp.broadcast_to(alpha * l_ref[h][:, :1] + jnp.sum(p, axis=-1, keepdims=True), (tq, 128))
        m_ref[h] = jnp.broadcast_to(m_new, (tq, 128))
        alphas.append(alpha)
        ps.append(p.astype(BF16))
    pv = _dot(jnp.concatenate(ps, axis=0), v_ref[...])
    acc_ref[...] = jnp.concatenate(alphas, axis=0) * acc_ref[...] + pv

    @pl.when(kk == pl.num_programs(2) - 1)
    def _():
        lane = lax.broadcasted_iota(jnp.int32, (tq, MLA_HEADS * MLA_V), 1)
        out = jnp.zeros((tq, MLA_HEADS * MLA_V), F32)
        for h in range(MLA_HEADS):
            out = jnp.where(lane // MLA_V == h, acc_ref[h * tq:(h + 1) * tq, :] / l_ref[h][:, :1], out)
        o_ref[...] = out


def _flash_mla(q, k, v, *, nb, t_all, q_start, q_len, k_len, tk):
    tq = TM
    tpb_q, tpb_k = t_all // tq, t_all // tk
    q0 = q_start // tq
    nq = q_len // tq
    return pl.pallas_call(
        _flash_mla_kernel,
        grid=(nb, nq, k_len // tk),
        in_specs=[pl.BlockSpec((tq, 512), lambda b, i, kk: (b * tpb_q + q0 + i, 0)),
                  pl.BlockSpec((tk, 512), lambda b, i, kk: (b * tpb_k + kk, 0)),
                  pl.BlockSpec((tk, 256), lambda b, i, kk: (b * tpb_k + kk, 0))],
        out_specs=pl.BlockSpec((tq, 256), lambda b, i, kk: (b * nq + i, 0)),
        out_shape=jax.ShapeDtypeStruct((nb * q_len, 256), F32),
        scratch_shapes=[pltpu.VMEM((MLA_HEADS, tq, 128), F32), pltpu.VMEM((MLA_HEADS, tq, 128), F32),
                        pltpu.VMEM((MLA_HEADS * tq, 256), F32)],
        compiler_params=_cparams(("parallel", "parallel", "arbitrary")),
        name="flash_mla",
    )(q, k, v)


def _diff_prep(p, cos_tab, sin_tab, *, nb, tpb):
    n = p.shape[0]
    tok = lambda b, r: (b * tpb + r, 0)
    pos = lambda b, r: (r, 0)
    return pl.pallas_call(
        _diff_prep_kernel_cols,
        grid=(nb, tpb),
        in_specs=[pl.BlockSpec((TM, 256), lambda b, r, c=c: (b * tpb + r, EXT_DIFF // 256 + c))
                  for c in range(5)]
                 + [pl.BlockSpec((TM, 256), pos), pl.BlockSpec((TM, 256), pos)],
        out_specs=[pl.BlockSpec((TM, 256), tok)] * 3,
        out_shape=[jax.ShapeDtypeStruct((n, 256), BF16)] * 3,
        compiler_params=_cparams(("parallel", "parallel")),
        name="diff_prep",
    )(p, p, p, p, p, cos_tab, sin_tab)


def _diff_prep_kernel_cols(q_in, k_in, v_in, qr_in, kr_in, cos_ref, sin_ref, q_ref, k_ref, v_ref):
    cos, sin = cos_ref[...], sin_ref[...]
    scale = DIFF_HD ** -0.5 * LOG2E
    q_ref[...] = ((q_in[...] * cos + qr_in[...] * sin) * scale).astype(BF16)
    k_ref[...] = (k_in[...] * cos + kr_in[...] * sin).astype(BF16)
    v_ref[...] = v_in[...].astype(BF16)


def _flash_diff_kernel(q_ref, k_ref, v_ref, lam_ref, g_ref, o_ref, qs_ref, m_ref, l_ref, acc_ref, *, lam_init):
    kk = pl.program_id(2)
    tq = q_ref.shape[0]
    nsm = 2 * DIFF_HEADS

    @pl.when(kk == 0)
    def _():
        m_ref[...] = jnp.full(m_ref.shape, -jnp.inf, F32)
        l_ref[...] = jnp.zeros(l_ref.shape, F32)
        acc_ref[...] = jnp.zeros(acc_ref.shape, F32)
        q = q_ref[...]
        lane = lax.broadcasted_iota(jnp.int32, (tq, 256), 1)
        for i in range(nsm):
            qs_ref[i * tq:(i + 1) * tq, :] = jnp.where((lane // DIFF_HD) == i, q, jnp.zeros_like(q))

    k = k_ref[...]
    ps, alphas = [], []
    for i in range(nsm):
        rows = slice(i * tq, (i + 1) * tq)
        s = _dot_nt(qs_ref[rows, :], k)
        m_prev = m_ref[rows, :][:, :1]
        m_new = jnp.maximum(m_prev, jnp.max(s, axis=-1, keepdims=True))
        alpha = jnp.exp2(m_prev - m_new)
        p = jnp.exp2(s - m_new)
        l_ref[rows, :] = jnp.broadcast_to(alpha * l_ref[rows, :][:, :1] + jnp.sum(p, axis=-1, keepdims=True),
                                          (tq, 128))
        m_ref[rows, :] = jnp.broadcast_to(m_new, (tq, 128))
        alphas.append(alpha)
        ps.append(p.astype(BF16))
    pv = _dot(jnp.concatenate(ps, axis=0), v_ref[...])
    acc_ref[...] = jnp.concatenate(alphas, axis=0) * acc_ref[...] + pv

    @pl.when(kk == pl.num_programs(2) - 1)
    def _():
        lane = lax.broadcasted_iota(jnp.int32, (tq, 256), 1)
        o = jnp.zeros((tq, 256), F32)
        for h in range(DIFF_HEADS):
            r0, r1 = (2 * h) * tq, (2 * h + 1) * tq
            o0 = acc_ref[r0:r0 + tq, :] / l_ref[r0:r0 + tq, :][:, :1]
            o1 = acc_ref[r1:r1 + tq, :] / l_ref[r1:r1 + tq, :][:, :1]
            o = jnp.where((lane // (2 * DIFF_HD)) == h, o0 - lam_ref[...] * o1, o)
        ms = _split_dot(o * o, _block_ones(256, 2 * DIFF_HD)) * (1.0 / (2 * DIFF_HD))
        o_ref[...] = o * lax.rsqrt(ms + DIFF_EPS) * g_ref[...] * (1.0 - lam_init)


def _flash_diff(q, k, v, lam_row, g_row, *, lam_init, nb, t_all, q_start, q_len, k_len, tk):
    tq = TM
    tpb_q, tpb_k = t_all // tq, t_all // tk
    q0 = q_start // tq
    nq = q_len // tq
    return pl.pallas_call(
        functools.partial(_flash_diff_kernel, lam_init=lam_init),
        grid=(nb, nq, k_len // tk),
        in_specs=[pl.BlockSpec((tq, 256), lambda b, i, kk: (b * tpb_q + q0 + i, 0)),
                  pl.BlockSpec((tk, 256), lambda b, i, kk: (b * tpb_k + kk, 0)),
                  pl.BlockSpec((tk, 256), lambda b, i, kk: (b * tpb_k + kk, 0)),
                  pl.BlockSpec((1, 256), lambda b, i, kk: (0, 0)),
                  pl.BlockSpec((1, 256), lambda b, i, kk: (0, 0))],
        out_specs=pl.BlockSpec((tq, 256), lambda b, i, kk: (b * nq + i, 0)),
        out_shape=jax.ShapeDtypeStruct((nb * q_len, 256), F32),
        scratch_shapes=[pltpu.VMEM((2 * DIFF_HEADS * tq, 256), BF16),
                        pltpu.VMEM((2 * DIFF_HEADS * tq, 128), F32),
                        pltpu.VMEM((2 * DIFF_HEADS * tq, 128), F32),
                        pltpu.VMEM((2 * DIFF_HEADS * tq, 256), F32)],
        compiler_params=_cparams(("parallel", "parallel", "arbitrary")),
        name="flash_diff",
    )(q, k, v, lam_row, g_row)


def _rw_prep_kernel(p_ref, prev_ref, next_ref, mu_ref, wlo_ref, g2_ref, vec_ref,
                    r_ref, v_ref, kk_ref, w0_ref, k0_ref, b0_ref, w1_ref, k1_ref, b1_ref,
                    bonus_ref, gate_ref, *, r0, lat_last):
    r = pl.program_id(1) + r0
    p = p_ref[...]
    row = lax.broadcasted_iota(jnp.int32, p.shape, 0)
    first_tile = jnp.logical_or(r == 0, r == 1)
    last_tile = jnp.logical_or(r == 0, r == lat_last)
    prev_row = jnp.where(first_tile, 0.0, prev_ref[7:8, :])
    next_row = jnp.where(last_tile, 0.0, next_ref[0:1, :])
    up = jnp.where(row == 0, prev_row, pltpu.roll(p, 1, 0))
    dn = jnp.where(row == TM - 1, next_row, pltpu.roll(p, TM - 1, 0))
    z = p + (0.5 * (up + dn) - p) * mu_ref[...]

    hw = RW_HEADS * RW_HEAD
    rr, k, v = z[:, :hw], z[:, hw:2 * hw], z[:, 2 * hw:3 * hw]
    lo = z[:, 3 * hw:3 * hw + 128]
    lane = lax.broadcasted_iota(jnp.int32, lo.shape, 1)
    lo = jnp.where(lane < 64, jnp.tanh(lo), lo).astype(BF16)
    wa = _dot(lo, wlo_ref[...])
    gate_ref[...] = _dot(_sigmoid(z[:, 3 * hw + 128:]).astype(BF16), g2_ref[...])

    e4 = _block_ones(hw, RW_HEAD)
    k_k, k_a, r_k = vec_ref[0:1, :], vec_ref[1:2, :], vec_ref[2:3, :]
    kk = k * k_k
    nrm = jnp.maximum(jnp.sqrt(_split_dot(kk * kk, e4)), 1e-12)
    kk = kk / nrm
    r_ref[...] = rr
    v_ref[...] = v
    kk_ref[...] = kk
    ksum = jnp.zeros_like(k)
    for d, (w_ref, kd_ref, b_ref) in enumerate(((w0_ref, k0_ref, b0_ref), (w1_ref, k1_ref, b1_ref))):
        w0 = vec_ref[3 + d:4 + d, :]
        a0 = vec_ref[5 + d:6 + d, :]
        wd = -_softplus(-(w0 + wa[:, d * hw:(d + 1) * hw])) - 0.5
        w_ref[...] = jnp.exp(-jnp.exp(wd))
        ad = _sigmoid(a0 + wa[:, (2 + d) * hw:(3 + d) * hw])
        kd = k * (1.0 + (ad - 1.0) * k_a)
        kd_ref[...] = kd
        b_ref[...] = kk * ad
        ksum = ksum + kd
    bonus_ref[...] = _split_dot(rr * ksum * r_k, e4) * v


def _rw_prep(p, mu, wlo, g2, vecs, *, nb, tpb, r0, nt):
    n_out = nb * nt * TM
    hw = RW_HEADS * RW_HEAD
    n_rows8 = p.shape[0] // 8
    tok = lambda b, r: (b * tpb + r0 + r, EXT_RW // 1024)
    prev = lambda b, r: (jnp.maximum((b * tpb + r0 + r) * (TM // 8) - 1, 0), EXT_RW // 1024)
    nxt = lambda b, r: (jnp.minimum((b * tpb + r0 + r + 1) * (TM // 8), n_rows8 - 1), EXT_RW // 1024)
    out = lambda b, r: (b * nt + r, 0)
    full = lambda b, r: (0, 0)
    return pl.pallas_call(
        functools.partial(_rw_prep_kernel, r0=r0, lat_last=tpb - 1),
        grid=(nb, nt),
        in_specs=[pl.BlockSpec((TM, 1024), tok),
                  pl.BlockSpec((8, 1024), prev),
                  pl.BlockSpec((8, 1024), nxt),
                  pl.BlockSpec((1, 1024), full),
                  pl.BlockSpec((128, 4 * hw), full),
                  pl.BlockSpec((128, hw), full),
                  pl.BlockSpec((8, hw), full)],
        out_specs=[pl.BlockSpec((TM, hw), out)] * 11,
        out_shape=[jax.ShapeDtypeStruct((n_out, hw), F32)] * 11,
        compiler_params=_cparams(("parallel", "parallel")),
        name="rw_prep",
    )(p, p, p, mu, wlo, g2, vecs)


def _rw_scan_kernel(rf, vf, kkf, wf, kf, bf, rb, vb, kkb, wb, kb, bb, yf_ref, yb_ref, s_ref, *, nb):
    c = RW_CHUNK

    @pl.when(pl.program_id(0) == 0)
    def _():
        s_ref[...] = jnp.zeros(s_ref.shape, F32)

    e2 = _block_ones(128, RW_HEAD)
    z2 = jnp.zeros((128, 128), BF16)
    rhs_pair = jnp.concatenate([jnp.concatenate([e2, z2], axis=1), jnp.concatenate([z2, e2], axis=1)], axis=0)
    e22 = jnp.concatenate([e2, e2], axis=0)
    lane = lax.broadcasted_iota(jnp.int32, (RW_HEAD, 128), 1)
    sub = lax.broadcasted_iota(jnp.int32, (RW_HEAD, 128), 0)
    diag = (lane % RW_HEAD) == sub
    sub8 = lax.broadcasted_iota(jnp.int32, (8, 128), 0)
    dirs = ((rf, vf, kkf, wf, kf, bf, yf_ref), (rb, vb, kkb, wb, kb, bb, yb_ref))

    def allreduce_rows(x):
        t = x[0:8]
        for i in range(1, 8):
            t = t + x[8 * i:8 * i + 8]
        for sh in (4, 2, 1):
            t = t + pltpu.roll(t, sh, 0)
        return t

    def col(row):
        return jnp.where(diag, row, 0.0).astype(BF16)

    def group(g, carry):
        tiles, ytiles = {}, {}
        for d, refs in enumerate(dirs):
            base = pl.multiple_of((g if d == 0 else c // 8 - 1 - g) * 8, 8)
            if d == 0:
                prev = lambda x, sh: jnp.where(sub8 >= sh, pltpu.roll(x, sh, 0), 1.0)
            else:
                prev = lambda x, sh: jnp.where(sub8 < 8 - sh, pltpu.roll(x, 8 - sh, 0), 1.0)
            last = 7 if d == 0 else 0
            for b in range(nb):
                for hp in range(2):
                    r_, v_, kk_, w_, k_, b_ = [ref[b, pl.ds(base, 8), pl.ds(hp * 128, 128)] for ref in refs[:6]]
                    gam = w_
                    for sh in (1, 2, 4):
                        gam = gam * prev(gam, sh)
                    inv = 1.0 / gam
                    tiles[d, b, hp] = (base, v_, b_ * inv, kk_ * prev(gam, 1), k_ * inv, r_ * gam,
                                       gam[last:last + 1, :])
                    ytiles[d, b, hp] = jnp.zeros((8, 128), F32)
        units = [(d, b, hp) for d in range(2) for b in range(nb) for hp in range(2)]
        for jj in range(8):
            lhs = []
            for (d, b, hp) in units:
                rw = slice(jj, jj + 1) if d == 0 else slice(7 - jj, 8 - jj)
                _, _, bh, kkh, kh, rh, _ = tiles[d, b, hp]
                lhs.append(jnp.concatenate([col(bh[rw]), col(kkh[rw])], axis=1))
                lhs.append(jnp.concatenate([col(kh[rw]), col(rh[rw])], axis=1))
            cm = _dot(jnp.concatenate(lhs, axis=0), rhs_pair)
            for u, (d, b, hp) in enumerate(units):
                j = jj if d == 0 else 7 - jj
                v_ = tiles[d, b, hp][1]
                r0 = 2 * u * RW_HEAD
                bc, kkc = cm[r0:r0 + RW_HEAD, :128], cm[r0:r0 + RW_HEAD, 128:]
                kc, rc = cm[r0 + RW_HEAD:r0 + 2 * RW_HEAD, :128], cm[r0 + RW_HEAD:r0 + 2 * RW_HEAD, 128:]
                s = s_ref[u]
                sa = jnp.concatenate([allreduce_rows(kkc * s)] * 8, axis=0)
                s = s - bc * sa + kc * v_[j:j + 1]
                s_ref[u] = s
                ytiles[d, b, hp] = jnp.where(sub8 == j, allreduce_rows(rc * s), ytiles[d, b, hp])
        for d, refs in enumerate(dirs):
            for b in range(nb):
                for hp in range(2):
                    u = (d * nb + b) * 2 + hp
                    gl = tiles[d, b, hp][6]
                    gh = gl.astype(BF16).astype(F32)
                    s_ref[u] = s_ref[u] * _dot(jnp.concatenate([col(gh), col(gl - gh)], axis=1), e22)
                    refs[6][b, pl.ds(tiles[d, b, hp][0], 8), pl.ds(hp * 128, 128)] = ytiles[d, b, hp]
        return carry

    lax.fori_loop(0, c // 8, group, 0)


def _rw_scan(r, v, kk, w0, k0, b0, w1, k1, b1, *, nb, t_all, n_ctx):
    c = RW_CHUNK
    nc, ncc = t_all // c, n_ctx // c
    hw = RW_HEADS * RW_HEAD
    shp = lambda a: a.reshape(nb, t_all, hw)
    fwd = lambda j: (0, j, 0)
    bwd = lambda j: (0, jnp.where(j < ncc, ncc - 1 - j, nc - 1 - (j - ncc)), 0)
    blk = (nb, c, hw)
    yshape = jax.ShapeDtypeStruct((nb, t_all, hw), F32)
    yf, yb = pl.pallas_call(
        functools.partial(_rw_scan_kernel, nb=nb),
        grid=(nc,),
        in_specs=[pl.BlockSpec(blk, fwd)] * 6 + [pl.BlockSpec(blk, bwd)] * 6,
        out_specs=[pl.BlockSpec(blk, fwd), pl.BlockSpec(blk, bwd)],
        out_shape=[yshape, yshape],
        scratch_shapes=[pltpu.VMEM((2 * nb * 2, RW_HEAD, 128), F32)],
        compiler_params=_cparams(("arbitrary",)),
        name="rw_scan",
    )(shp(r), shp(v), shp(kk), shp(w0), shp(k0), shp(b0),
      shp(r), shp(v), shp(kk), shp(w1), shp(k1), shp(b1))

    return yf.reshape(nb * t_all, hw), yb.reshape(nb * t_all, hw)


def _s5_scan_kernel(uf_ref, ub_ref, wb_ref, ab_ref, cf_ref, cb_ref, yf_ref, yb_ref, x_ref, st_ref, *, nb):
    c = S5_CHUNK
    nst = S5_GROUPS * S5_STATE

    @pl.when(pl.program_id(0) == 0)
    def _():
        st_ref[...] = jnp.zeros(st_ref.shape, F32)

    dirs = ((uf_ref, cf_ref, yf_ref), (ub_ref, cb_ref, yb_ref))
    for d, (u_ref, _, _) in enumerate(dirs):
        for b in range(nb):
            x_ref[d, b] = _dot(u_ref[b].astype(BF16), wb_ref[:, d * 2 * nst:(d + 1) * 2 * nst])

    def group(g, carry):
        for d in range(2):
            base = pl.multiple_of((g if d == 0 else c // 8 - 1 - g) * 8, 8)
            ar = ab_ref[d, :, 0:nst]
            ai = ab_ref[d, :, nst:2 * nst]
            for b in range(nb):
                u = d * nb + b
                xr = st_ref[u, :, 0:nst]
                xi = st_ref[u, :, nst:2 * nst]
                bur = x_ref[d, b, pl.ds(base, 8), 0:nst]
                bui = x_ref[d, b, pl.ds(base, 8), nst:2 * nst]
                rows_r, rows_i = [None] * 8, [None] * 8
                for jj in range(8):
                    j = jj if d == 0 else 7 - jj
                    xr, xi = (ar * xr - ai * xi + bur[j:j + 1, :], ar * xi + ai * xr + bui[j:j + 1, :])
                    rows_r[j], rows_i[j] = xr, xi
                st_ref[u, :, 0:nst] = xr
                st_ref[u, :, nst:2 * nst] = xi
                x_ref[d, b, pl.ds(base, 8), 0:nst] = jnp.concatenate(rows_r, axis=0)
                x_ref[d, b, pl.ds(base, 8), nst:2 * nst] = jnp.concatenate(rows_i, axis=0)
        return carry

    lax.fori_loop(0, c // 8, group, 0)

    for d, (_, c_ref, y_ref) in enumerate(dirs):
        for b in range(nb):
            y_ref[b] = _dot(x_ref[d, b].astype(BF16), c_ref[...])


def _s5_scan(p, w_b, ab, cf, cb, *, nb, t_all, n_ctx):
    c = S5_CHUNK
    nc, ncc = t_all // c, n_ctx // c
    nst2 = 2 * S5_GROUPS * S5_STATE
    width = S5_GROUPS * S5_GROUP_CH
    p3 = p.reshape(nb, t_all, N_EXT)
    fwd = lambda j: j
    bwd = lambda j: jnp.where(j < ncc, ncc - 1 - j, nc - 1 - (j - ncc))
    yshape = jax.ShapeDtypeStruct((nb, t_all, width), F32)
    full = lambda j: (0, 0)
    yf, yb = pl.pallas_call(
        functools.partial(_s5_scan_kernel, nb=nb),
        grid=(nc,),
        in_specs=[pl.BlockSpec((nb, c, width), lambda j: (0, fwd(j), EXT_S5 // width)),
                  pl.BlockSpec((nb, c, width), lambda j: (0, bwd(j), EXT_S5 // width)),
                  pl.BlockSpec((width, 2 * nst2), full),
                  pl.BlockSpec((2, 1, nst2), lambda j: (0, 0, 0)),
                  pl.BlockSpec((nst2, width), full), pl.BlockSpec((nst2, width), full)],
        out_specs=[pl.BlockSpec((nb, c, width), lambda j: (0, fwd(j), 0)),
                   pl.BlockSpec((nb, c, width), lambda j: (0, bwd(j), 0))],
        out_shape=[yshape, yshape],
        scratch_shapes=[pltpu.VMEM((2, nb, c, nst2), F32), pltpu.VMEM((2 * nb, 1, nst2), F32)],
        compiler_params=_cparams(("arbitrary",)),
        name="s5_scan",
    )(p3, p3, w_b, ab, cf, cb)
    return yf.reshape(nb * t_all, width), yb.reshape(nb * t_all, width)


def _merge_kernel(ya_ref, yd_ref, rf_ref, rb_ref, bonus_ref, grw_ref, lnv_ref, sf_ref, sb_ref, u_ref, d_ref,
                  gw_ref, gb_ref, g0, g1, g2, g3, h_ref, m2_ref, wb_ref, wo_ref, o_ref):
    e4 = _block_ones(RW_HEADS * RW_HEAD, RW_HEAD)
    y = rf_ref[...] + rb_ref[...]
    yc = y - _split_dot(y, e4) * (1.0 / RW_HEAD)
    var = _split_dot(yc * yc, e4) * (1.0 / RW_HEAD)
    y_rw = (yc * lax.rsqrt(var + RW_LN_EPS) * lnv_ref[0:1, :] + lnv_ref[1:2, :] + bonus_ref[...]) * grw_ref[...]
    y = sf_ref[...] + sb_ref[...] + d_ref[...] * u_ref[...]
    zg = 0.5 * y * (1.0 + jnp.tanh(math.sqrt(2.0 / math.pi) * (y + 0.044715 * (y * y * y))))
    y_s5 = zg * _sigmoid(_dot(zg.astype(BF16), gw_ref[...]) + gb_ref[...])

    acc = None
    for n, (yn, g_ref) in enumerate(((ya_ref[...], g0), (y_rw, g1), (y_s5, g2), (yd_ref[...], g3))):
        term = _sigmoid(g_ref[...]) * _dot(yn.astype(BF16), wb_ref[n])
        acc = term if acc is None else acc + term
    o_ref[...] = h_ref[...] + m2_ref[0] * _dot(acc.astype(BF16), wo_ref[...])


def _merge(ya, yd, rw_f, rw_b, bonus, gate_rw, ln_vecs, s5_f, s5_b, d_row, glu_w, glu_b, p, h, mod2, w_branch,
           w_out, *, nb, tpb, r0, nt):
    tok_in = lambda b, r: (b * tpb + r0 + r, 0)
    tok_out = lambda b, r: (b * nt + r, 0)
    full2 = lambda b, r: (0, 0)
    w256 = RW_HEADS * RW_HEAD
    gate = [pl.BlockSpec((TM, D_MODEL), lambda b, r, c=c: (b * tpb + r0 + r, EXT_GATE // D_MODEL + c))
            for c in range(4)]
    return pl.pallas_call(
        _merge_kernel,
        grid=(nb, nt),
        in_specs=[pl.BlockSpec((TM, w256), tok_out), pl.BlockSpec((TM, w256), tok_out),
                  pl.BlockSpec((TM, w256), tok_in), pl.BlockSpec((TM, w256), tok_in),
                  pl.BlockSpec((TM, w256), tok_out), pl.BlockSpec((TM, w256), tok_out),
                  pl.BlockSpec((8, w256), full2),
                  pl.BlockSpec((TM, w256), tok_in), pl.BlockSpec((TM, w256), tok_in),
                  pl.BlockSpec((TM, w256), lambda b, r: (b * tpb + r0 + r, EXT_S5 // w256)),
                  pl.BlockSpec((1, w256), full2), pl.BlockSpec((w256, w256), full2), pl.BlockSpec((1, w256), full2)]
                 + gate
                 + [pl.BlockSpec((TM, D_MODEL), tok_in),
                    pl.BlockSpec((1, 1, D_MODEL), lambda b, r: (jnp.where(r0 + r == 0, nb, b), 0, 0)),
                    pl.BlockSpec((4, w256, D_MODEL), lambda b, r: (0, 0, 0)),
                    pl.BlockSpec((D_MODEL, D_MODEL), full2)],
        out_specs=pl.BlockSpec((TM, D_MODEL), tok_out),
        out_shape=jax.ShapeDtypeStruct((nb * nt * TM, D_MODEL), F32),
        compiler_params=_cparams(("parallel", "parallel")),
        name="merge",
    )(ya, yd, rw_f, rw_b, bonus, gate_rw, ln_vecs, s5_f, s5_b, p, d_row, glu_w, glu_b, p, p, p, p, h, mod2,
      w_branch, w_out)


def _router_kernel(h_ref, g_ref, sh_ref, sc_ref, wh_ref, wm_ref, b_ref, x_ref, lg_ref):
    x = _rms(h_ref[...], g_ref[...], NORM_EPS) * (1.0 + sc_ref[0]) + sh_ref[0]
    xh = x.astype(BF16)
    xm = (x - xh.astype(F32)).astype(BF16)
    bits = lax.bitcast_convert_type(xh.astype(F32), jnp.uint32)
    half = D_MODEL // 2
    x_ref[...] = (bits[:, :half] >> 16) | (bits[:, half:] & jnp.uint32(0xFFFF0000))
    lg = (_dot(xh, wh_ref[...]) + _dot(xm, wh_ref[...]) + _dot(xh, wm_ref[...])) + b_ref[...]

    lane = lax.broadcasted_iota(jnp.int32, lg.shape, 1)
    lanef = lane.astype(F32)
    neg = jnp.float32(-jnp.inf)
    big = jnp.float32(1e9)
    rmax = lambda v: jnp.max(v, axis=-1, keepdims=True)
    rmin = lambda v: jnp.min(v, axis=-1, keepdims=True)
    rsum = lambda v: jnp.sum(v, axis=-1, keepdims=True)

    gmask = lane < MOE_GROUPS
    mg = rmax(jnp.where(gmask, lg, neg))
    eg = jnp.where(gmask, jnp.exp(lg - mg), 0.0)
    pg = eg / rsum(eg)
    pg_top = rmax(pg)
    g_sel = rmin(jnp.where(jnp.logical_and(gmask, pg == pg_top), lanef, big))
    lo = MOE_GROUPS + MOE_PER_GROUP * g_sel
    emask = jnp.logical_and(lanef >= lo, lanef < lo + MOE_PER_GROUP)
    me = rmax(jnp.where(emask, lg, neg))
    ee = jnp.where(emask, jnp.exp(lg - me), 0.0)
    pe = jnp.where(emask, ee / rsum(ee), -1.0)
    p1 = rmax(pe)
    i1 = rmin(jnp.where(pe == p1, lanef, big))
    pe2 = jnp.where(lanef == i1, -1.0, pe)
    p2 = rmax(pe2)
    i2 = rmin(jnp.where(pe2 == p2, lanef, big))
    den = p1 + p2
    out = jnp.where(lane == 0, i1 - MOE_GROUPS, 0.0)
    out = jnp.where(lane == 1, i2 - MOE_GROUPS, out)
    out = jnp.where(lane == 2, pg_top * p1 / den, out)
    out = jnp.where(lane == 3, pg_top * p2 / den, out)
    lg_ref[...] = out


def _router(h, g, shift, scale, wh, wm, bias, *, nb, nt, ctx_first):
    tok = lambda b, r: (b * nt + r, 0)
    full = lambda b, r: (0, 0)
    if ctx_first:
        modmap = lambda b, r: (jnp.where(r == 0, nb, b), 0, 0)
    else:
        modmap = lambda b, r: (b, 0, 0)
    n = h.shape[0]
    return pl.pallas_call(
        _router_kernel,
        grid=(nb, nt),
        in_specs=[pl.BlockSpec((TM, D_MODEL), tok), pl.BlockSpec((1, D_MODEL), full),
                  pl.BlockSpec((1, 1, D_MODEL), modmap), pl.BlockSpec((1, 1, D_MODEL), modmap),
                  pl.BlockSpec((D_MODEL, 128), full), pl.BlockSpec((D_MODEL, 128), full),
                  pl.BlockSpec((1, 128), full)],
        out_specs=[pl.BlockSpec((TM, D_MODEL // 2), tok), pl.BlockSpec((TM, 128), tok)],
        out_shape=[jax.ShapeDtypeStruct((n, D_MODEL // 2), jnp.uint32), jax.ShapeDtypeStruct((n, 128), F32)],
        compiler_params=_cparams(("parallel", "parallel")),
        name="router",
    )(h, g, shift, scale, wh, wm, bias)


def _expert_kernel(be_ref, nv_ref, x_ref, wg_ref, wu_ref, wd_ref, o_ref, wgb_ref, wub_ref, wdb_ref):
    i = pl.program_id(0)

    @pl.when(jnp.logical_or(i == 0, be_ref[i] != be_ref[jnp.maximum(i - 1, 0)]))
    def _():
        wgb_ref[...] = wg_ref[0].astype(BF16)
        wub_ref[...] = wu_ref[0].astype(BF16)
        wdb_ref[...] = wd_ref[0].astype(BF16)

    @pl.when(i < nv_ref[0])
    def _():
        u = x_ref[...]
        x = jnp.concatenate([lax.bitcast_convert_type(u << 16, F32),
                             lax.bitcast_convert_type(u & jnp.uint32(0xFFFF0000), F32)], axis=1).astype(BF16)
        hb = _silu(_dot(x, wgb_ref[...])) * _dot(x, wub_ref[...])
        o_ref[...] = _dot(hb.astype(BF16), wdb_ref[...])

    @pl.when(i >= nv_ref[0])
    def _():
        o_ref[...] = jnp.zeros(o_ref.shape, F32)


def _experts(xs, block_e, n_valid, w_gate, w_up, w_down):
    n_slots = xs.shape[0]
    n_blocks = n_slots // MOE_BLK
    wmap = lambda i, be, nv: (be[i], 0, 0)
    return pl.pallas_call(
        _expert_kernel,
        grid_spec=pltpu.PrefetchScalarGridSpec(
            num_scalar_prefetch=2,
            grid=(n_blocks,),
            in_specs=[pl.BlockSpec((MOE_BLK, D_MODEL // 2), lambda i, be, nv: (i, 0)),
                      pl.BlockSpec((1, D_MODEL, D_EXPERT), wmap),
                      pl.BlockSpec((1, D_MODEL, D_EXPERT), wmap),
                      pl.BlockSpec((1, D_EXPERT, D_MODEL), wmap)],
            out_specs=pl.BlockSpec((MOE_BLK, D_MODEL), lambda i, be, nv: (i, 0)),
            scratch_shapes=[pltpu.VMEM((D_MODEL, D_EXPERT), BF16), pltpu.VMEM((D_MODEL, D_EXPERT), BF16),
                            pltpu.VMEM((D_EXPERT, D_MODEL), BF16)]),
        out_shape=jax.ShapeDtypeStruct((n_slots, D_MODEL), F32),
        compiler_params=_cparams(("arbitrary",)),
        name="experts",
    )(block_e, n_valid, xs, w_gate, w_up, w_down)


def _combine_kernel(h_ref, y0_ref, y1_ref, w_ref, m5_ref, g_ref, o_ref, *, final):
    w = w_ref[...]
    y = y0_ref[...] * w[:, 0:1] + y1_ref[...] * w[:, 1:2]
    h = h_ref[...] + m5_ref[0] * y
    if final:
        h = _rms(h, g_ref[...], NORM_EPS)
    o_ref[...] = h


def _combine(h, y01, wts, mod5, g_final, *, nb, nt, ctx_first, final):
    tok = lambda b, r: (b * nt + r, 0)
    if ctx_first:
        modmap = lambda b, r: (jnp.where(r == 0, nb, b), 0, 0)
    else:
        modmap = lambda b, r: (b, 0, 0)
    return pl.pallas_call(
        functools.partial(_combine_kernel, final=final),
        grid=(nb, nt),
        in_specs=[pl.BlockSpec((TM, D_MODEL), tok), pl.BlockSpec((TM, D_MODEL), tok),
                  pl.BlockSpec((TM, D_MODEL), lambda b, r: ((nb + b) * nt + r, 0)),
                  pl.BlockSpec((TM, 128), tok), pl.BlockSpec((1, 1, D_MODEL), modmap),
                  pl.BlockSpec((1, D_MODEL), lambda b, r: (0, 0))],
        out_specs=pl.BlockSpec((TM, D_MODEL), tok),
        out_shape=jax.ShapeDtypeStruct(h.shape, F32),
        compiler_params=_cparams(("parallel", "parallel")),
        name="combine",
    )(h, y01, y01, wts, mod5, g_final)


def _sc_gather(table, idx):
    n, d = idx.shape[0], table.shape[1]
    steps = n // SC_ROWS_PER_STEP
    idx_p = jnp.pad(idx.reshape(steps, SC_ROWS_PER_STEP), ((0, 0), (0, SC_INDEX_TILE - SC_ROWS_PER_STEP)))
    mesh = plsc.VectorSubcoreMesh(core_axis_name="core", subcore_axis_name="subcore")

    @pl.kernel(out_type=jax.ShapeDtypeStruct((n, d), table.dtype), mesh=mesh)
    def gather_kernel(x_hbm, i_hbm, o_hbm):
        def body(i_vmem, o_vmem):
            pltpu.sync_copy(x_hbm.at[i_vmem.at[0, pl.ds(0, SC_ROWS_PER_STEP)]], o_vmem)

        pltpu.emit_pipeline(
            body,
            grid=(steps,),
            in_specs=[pl.BlockSpec((1, SC_INDEX_TILE), index_map=lambda i: (i, 0))],
            out_specs=[pl.BlockSpec((SC_ROWS_PER_STEP, d), index_map=lambda i: (i, 0))],
            core_axis_name=("core", "subcore"),
            dimension_semantics=(pltpu.PARALLEL,),
        )(i_hbm, o_hbm)

    return gather_kernel(table, idx_p)


def _moe(h, g2, shift, scale, mod5, wh, wm, rbias, w_gate, w_up, w_down, g_final, *, layer, nb, nt, ctx_first,
         final):
    n = h.shape[0]
    x_bf, route = _router(h, g2, shift, scale, wh, wm, rbias, nb=nb, nt=nt, ctx_first=ctx_first)
    idx = route[:, :MOE_TOPK].astype(jnp.int32)
    wts = route[:, MOE_TOPK:2 * MOE_TOPK]
    n_as = n * MOE_TOPK
    flat_e = idx.reshape(n_as)
    onehot = (flat_e[:, None] == jnp.arange(MOE_EXPERTS, dtype=jnp.int32)[None, :]).astype(jnp.int32)
    csum = jnp.cumsum(onehot, axis=0)
    counts = csum[-1]
    rank = jnp.sum(jnp.where(onehot > 0, csum - 1, 0), axis=1)
    padded = (counts + MOE_BLK - 1) // MOE_BLK * MOE_BLK
    pad_end = jnp.cumsum(padded)
    pad_start = pad_end - padded
    slot = pad_start[flat_e] + rank
    n_blocks = (n_as + MOE_EXPERTS * (MOE_BLK - 1) + MOE_BLK - 1) // MOE_BLK
    n_slots = n_blocks * MOE_BLK
    slot_tok = (jnp.arange(n_slots, dtype=jnp.int32) % n).at[slot].set(
        jnp.arange(n_as, dtype=jnp.int32) // MOE_TOPK, unique_indices=True)
    starts = jnp.arange(n_blocks, dtype=jnp.int32) * MOE_BLK
    block_e = jnp.minimum(jnp.sum((pad_end[None, :] <= starts[:, None]).astype(jnp.int32), axis=1),
                          MOE_EXPERTS - 1)
    n_valid = (pad_end[-1:] // MOE_BLK).astype(jnp.int32)
    xs = _sc_gather(x_bf, slot_tok)
    ys = _experts(xs, block_e + layer * MOE_EXPERTS, n_valid, w_gate, w_up, w_down)
    y01 = _sc_gather(ys, slot.reshape(n, MOE_TOPK).T.reshape(n_as))
    wts_pad = jnp.pad(wts.astype(F32), ((0, 0), (0, 128 - MOE_TOPK)))
    return _combine(h, y01, wts_pad, mod5, g_final, nb=nb, nt=nt, ctx_first=ctx_first, final=final)


_ROT_SRC = np.array(list(range(8, 16)) + list(range(0, 8)) + list(range(24, 32)) + list(range(16, 24)))
_ROT_SIGN = np.array([-1.0] * 8 + [1.0] * 8 + [-1.0] * 8 + [1.0] * 8, np.float32)


def _rot_cols(w):
    k = w.shape[-1] // ROPE_DIM
    src = np.concatenate([_ROT_SRC + ROPE_DIM * i for i in range(k)])
    sign = np.tile(_ROT_SIGN, k)
    return w[..., src] * sign


def _rope_tables(n_ctx, n_lat):
    rows = n_lat // GRID_W
    row = jnp.repeat(jnp.arange(rows, dtype=F32), GRID_W)
    col = (jnp.arange(rows * GRID_W) % GRID_W).astype(F32)
    nf = ROPE_DIM // 4
    inv = ROPE_BASE ** (-jnp.arange(nf, dtype=F32) / nf)
    ar = row[:, None] * inv
    ac = col[:, None] * inv
    ang = jnp.concatenate([ar, ar, ac, ac], axis=-1)
    cos = jnp.concatenate([jnp.ones((n_ctx, ROPE_DIM), F32), jnp.cos(ang)], axis=0)
    sin = jnp.concatenate([jnp.zeros((n_ctx, ROPE_DIM), F32), jnp.sin(ang)], axis=0)
    return cos, sin


def _block_diag(blocks):
    g, a, b = blocks.shape
    tiled = jnp.tile(blocks.reshape(g * a, b), (1, g))
    rows = lax.broadcasted_iota(jnp.int32, (g * a, g * b), 0) // a
    cols = lax.broadcasted_iota(jnp.int32, (g * a, g * b), 1) // b
    return jnp.where(rows == cols, tiled, 0.0)


def _pick_tk(t_all):
    best = 128
    for tk in range(128, ATTN_TK_MAX + 1, 128):
        if t_all % tk == 0:
            best = tk
    return best


def kernel(x, c, ctx, c_ctx, w_mod, b_mod, norm1_g, norm2_g, w_in, mla_q_norm_g, mla_kv_norm_g, mla_w_uq, mla_w_ukv, rw_mu, rw_w0, rw_w2, rw_a0, rw_a2, rw_g2, rw_k_k, rw_k_a, rw_r_k, rw_lnx_g, rw_lnx_b, s5_a_re, s5_a_im, s5_log_dt, s5_b_re, s5_b_im, s5_c_re, s5_c_im, s5_d, s5_glu_w, s5_glu_b, diff_lq1, diff_lk1, diff_lq2, diff_lk2, diff_subln_g, w_branch, w_out, router_g_w, router_g_b, router_e_w, router_e_b, exp_w_gate, exp_w_up, exp_w_down, final_norm_g):
    nb, n_lat, d = x.shape
    n_ctx = ctx.shape[1]
    depth = w_mod.shape[0]
    t_all = n_ctx + n_lat
    assert d == D_MODEL and n_ctx == TM and n_lat % TM == 0
    tpb = t_all // TM
    tk = _pick_tk(t_all)
    hw = RW_HEADS * RW_HEAD

    cos, sin = _rope_tables(n_ctx, n_lat)
    mla_scale = (MLA_NOPE + MLA_ROPE) ** -0.5 * LOG2E
    z32 = jnp.zeros((t_all, 32), F32)
    cq_tab = jnp.concatenate([jnp.ones((t_all, 64), F32), cos, z32], axis=1) * mla_scale
    sq_tab = jnp.concatenate([jnp.zeros((t_all, 64), F32), sin, z32], axis=1) * mla_scale
    ck_tab = jnp.concatenate([cos, sin, jnp.zeros((t_all, 64), F32)], axis=1)
    dcos = jnp.tile(cos, (1, 8))
    dsin = jnp.tile(sin, (1, 8))

    c_rows = jnp.concatenate([c, c_ctx[None, :], jnp.zeros((8 - nb - 1, d), F32)], axis=0)

    h = jnp.concatenate([ctx, x], axis=1).reshape(nb * t_all, d)

    for l in range(depth):
        last = l == depth - 1
        r0, nt = (1, tpb - 1) if last else (0, tpb)

        mod = _mod(c_rows, w_mod[l].astype(BF16), b_mod[l][None, :], tn=1536)
        mods = [mod[:nb + 1, i * d:(i + 1) * d].reshape(nb + 1, 1, d) for i in range(6)]

        wi = w_in[l]
        o_rw, o_s5, o_df, o_gt = 416, 1440, 1696, 2464
        w_kr = wi[:, 384:416]
        w_dq, w_dk, w_dv = wi[:, o_df:o_df + 256], wi[:, o_df + 256:o_df + 512], wi[:, o_df + 512:o_df + 768]
        w_ext = jnp.concatenate(
            [wi[:, o_rw:o_s5],
             wi[:, :416], _rot_cols(w_kr), jnp.zeros((d, 64), F32),
             wi[:, o_s5:o_df],
             w_dq, w_dk, w_dv, _rot_cols(w_dq), _rot_cols(w_dk),
             wi[:, o_gt:]], axis=1).astype(BF16)
        p = _inproj(h, norm1_g[l][None, :], mods[0], mods[1], w_ext, nb=nb, tpb=tpb)

        wq = mla_w_uq[l].reshape(MLA_Q_LORA, MLA_HEADS, MLA_NOPE + MLA_ROPE)
        zq = jnp.zeros((MLA_Q_LORA, MLA_HEADS, 32), F32)
        wa = jnp.concatenate([wq, zq], axis=2).reshape(MLA_Q_LORA, 512).astype(BF16)
        wb = jnp.concatenate([jnp.zeros((MLA_Q_LORA, MLA_HEADS, 64), F32), _rot_cols(wq[:, :, MLA_NOPE:]), zq],
                             axis=2).reshape(MLA_Q_LORA, 512).astype(BF16)
        wkv = mla_w_ukv[l].reshape(MLA_KV_LORA, MLA_HEADS, MLA_NOPE + MLA_V)
        wk = jnp.concatenate([wkv[:, :, :MLA_NOPE], jnp.zeros((MLA_KV_LORA, MLA_HEADS, 64), F32)],
                             axis=2).reshape(MLA_KV_LORA, 512).astype(BF16)
        wv = wkv[:, :, MLA_NOPE:].reshape(MLA_KV_LORA, MLA_HEADS * MLA_V).astype(BF16)
        pk_np = np.zeros((128, 512), np.float32)
        for hh in range(MLA_HEADS):
            for i in range(32):
                pk_np[i, hh * 128 + 64 + i] = 1.0
                pk_np[32 + i, hh * 128 + 64 + i] = 1.0
        pk = jnp.asarray(pk_np, BF16)
        q_m, k_m, v_m = _mla_prep(p, mla_q_norm_g[l][None, :], mla_kv_norm_g[l][None, :], wa, wb, wk, wv, pk,
                                  cq_tab, sq_tab, ck_tab, nb=nb, tpb=tpb)
        ya_lat = _flash_mla(q_m, k_m, v_m, nb=nb, t_all=t_all, q_start=n_ctx, q_len=n_lat, k_len=t_all, tk=tk)

        q_d, k_d, v_d = _diff_prep(p, dcos, dsin, nb=nb, tpb=tpb)
        lam_init = 0.8 - 0.6 * math.exp(-0.3 * l)
        lam = (jnp.exp(jnp.sum(diff_lq1[l] * diff_lk1[l])) - jnp.exp(jnp.sum(diff_lq2[l] * diff_lk2[l])) + lam_init)
        lam_row = jnp.full((1, 256), lam, F32)
        g_row = jnp.tile(diff_subln_g[l], DIFF_HEADS)[None, :]
        yd_lat = _flash_diff(q_d, k_d, v_d, lam_row, g_row, lam_init=lam_init, nb=nb, t_all=t_all,
                             q_start=n_ctx, q_len=n_lat, k_len=t_all, tk=tk)
        if last:
            ya, yd = ya_lat, yd_lat
        else:
            ya_ctx = _flash_mla(q_m, k_m, v_m, nb=nb, t_all=t_all, q_start=0, q_len=n_ctx, k_len=n_ctx, tk=n_ctx)
            yd_ctx = _flash_diff(q_d, k_d, v_d, lam_row, g_row, lam_init=lam_init, nb=nb, t_all=t_all,
                                 q_start=0, q_len=n_ctx, k_len=n_ctx, tk=n_ctx)
            comb = lambda a_c, a_l: jnp.concatenate(
                [a_c.reshape(nb, n_ctx, -1), a_l.reshape(nb, n_lat, -1)], axis=1).reshape(nb * t_all, -1)
            ya, yd = comb(ya_ctx, ya_lat), comb(yd_ctx, yd_lat)

        wlo = jnp.zeros((128, 4 * hw), F32)
        wlo = wlo.at[:64, 0:hw].set(rw_w2[l, 0]).at[:64, hw:2 * hw].set(rw_w2[l, 1])
        wlo = wlo.at[64:, 2 * hw:3 * hw].set(rw_a2[l, 0]).at[64:, 3 * hw:].set(rw_a2[l, 1])
        vecs = jnp.stack([rw_k_k[l], rw_k_a[l], rw_r_k[l].reshape(hw), rw_w0[l, 0], rw_w0[l, 1],
                          rw_a0[l, 0], rw_a0[l, 1], jnp.zeros((hw,), F32)], axis=0)
        (r_, v_, kk_, w0_, k0_, b0_, w1_, k1_, b1_, bonus, gate_rw) = _rw_prep(
            p, rw_mu[l][None, :], wlo.astype(BF16), rw_g2[l].astype(BF16), vecs, nb=nb, tpb=tpb, r0=0, nt=tpb)
        yf, yb_ = _rw_scan(r_, v_, kk_, w0_, k0_, b0_, w1_, k1_, b1_, nb=nb, t_all=t_all, n_ctx=n_ctx)
        ln_vecs = jnp.concatenate([rw_lnx_g[l][None, :], rw_lnx_b[l][None, :], jnp.zeros((6, hw), F32)], axis=0)
        if last:
            trim = lambda a: a.reshape(nb, t_all, hw)[:, n_ctx:].reshape(nb * n_lat, hw)
            bonus, gate_rw = trim(bonus), trim(gate_rw)

        bbs, abs_, cfs = [], [], []
        for dd in range(2):
            lr, li = s5_a_re[l, dd], s5_a_im[l, dd]
            dt = jnp.exp(s5_log_dt[l, dd])[:, None]
            mag = jnp.exp(lr * dt)
            ab_re, ab_im = mag * jnp.cos(li * dt), mag * jnp.sin(li * dt)
            den = lr * lr + li * li
            nr, ni = ab_re - 1.0, ab_im
            cf_re = (nr * lr + ni * li) / den
            cf_im = (ni * lr - nr * li) / den
            bre, bim = s5_b_re[l, dd], s5_b_im[l, dd]
            bb_re = cf_re[..., None] * bre - cf_im[..., None] * bim
            bb_im = cf_re[..., None] * bim + cf_im[..., None] * bre
            bbs.append(jnp.concatenate([_block_diag(bb_re.transpose(0, 2, 1)),
                                        _block_diag(bb_im.transpose(0, 2, 1))], axis=1))
            abs_.append(jnp.concatenate([ab_re.reshape(-1), ab_im.reshape(-1)])[None, :])
            cfs.append(jnp.concatenate([_block_diag(s5_c_re[l, dd].transpose(0, 2, 1)),
                                        -_block_diag(s5_c_im[l, dd].transpose(0, 2, 1))], axis=0))
        yf_s5, yb_s5 = _s5_scan(p, jnp.concatenate(bbs, axis=1).astype(BF16), jnp.stack(abs_, axis=0),
                                cfs[0].astype(BF16), cfs[1].astype(BF16), nb=nb, t_all=t_all, n_ctx=n_ctx)
        h = _merge(ya, yd, yf, yb_, bonus, gate_rw, ln_vecs, yf_s5, yb_s5, s5_d[l].reshape(1, 256),
                   s5_glu_w[l].astype(BF16), s5_glu_b[l][None, :], p, h, mods[2], w_branch[l].astype(BF16),
                   w_out[l].astype(BF16), nb=nb, tpb=tpb, r0=r0, nt=nt)

        wr = jnp.concatenate([router_g_w[l], router_e_w[l], jnp.zeros((d, 128 - MOE_GROUPS - MOE_EXPERTS), F32)], axis=1)
        wr_h = wr.astype(BF16)
        wr_m = (wr - wr_h.astype(F32)).astype(BF16)
        rbias = jnp.concatenate([router_g_b[l], router_e_b[l],
                                 jnp.zeros((128 - MOE_GROUPS - MOE_EXPERTS,), F32)])[None, :]
        h = _moe(h, norm2_g[l][None, :], mods[3], mods[4], mods[5], wr_h, wr_m, rbias,
                 exp_w_gate.reshape(depth * MOE_EXPERTS, d, D_EXPERT), exp_w_up.reshape(depth * MOE_EXPERTS, d, D_EXPERT),
                 exp_w_down.reshape(depth * MOE_EXPERTS, D_EXPERT, d),
                 final_norm_g[None, :], layer=l, nb=nb, nt=nt, ctx_first=not last, final=last)

    return h.reshape(nb, n_lat, d)
```
